```python
import math
import jax, jax.numpy as jnp
from jax import lax
import numpy as np

D_MODEL = 1024
BATCH = 16
SEQ = 4096
DEPTH = 2

N_EVEN = (DEPTH + 1) // 2
N_ODD = DEPTH // 2

SSM_WIDTH = D_MODEL // 4
SSM_GROUP = 16
SSM_GROUPS = SSM_WIDTH // SSM_GROUP
SSM_STATE = 64
GMLP_WIDTH = D_MODEL - SSM_WIDTH
GMLP_HEAD = 128
GMLP_HEADS = GMLP_WIDTH // GMLP_HEAD
CHUNK = 128
EVEN_IN = SSM_WIDTH + 2 * GMLP_WIDTH

CONV_WIDTH = 3
D_FF = 2816
EPS = 1e-6
DT_MIN = 1e-3
DT_MAX = 1e-1
LAMBDA_RE_MAX = -1e-4
RESID_SCALE = (2 * DEPTH) ** -0.5

kernel_name = "hybrid_s5_gmlp_shortconv_convffn"


def rmsnorm(x, g):
    xf = x.astype(jnp.float32)
    y = xf * lax.rsqrt(jnp.mean(xf * xf, axis=-1, keepdims=True) + EPS)
    return (y * g.astype(jnp.float32)).astype(x.dtype)


def causal_dwconv(x, w, b):
    k_w = w.shape[0]
    s = x.shape[1]
    xp = jnp.pad(x, ((0, 0), (k_w - 1, 0), (0, 0)))
    return b + sum(w[k] * xp[:, k:k + s] for k in range(k_w))


def s5_mixer(u, lam_re, lam_im, log_dt, b_re, b_im, c_re, c_im, d_skip, w_glu, b_glu):
    bsz, s, _ = u.shape
    uf = u.astype(jnp.float32).reshape(bsz, s, SSM_GROUPS, SSM_GROUP)
    lr = jnp.minimum(lam_re.astype(jnp.float32), LAMBDA_RE_MAX)
    li = lam_im.astype(jnp.float32)
    dt = jnp.exp(log_dt.astype(jnp.float32))[:, None]
    mag = jnp.exp(lr * dt)
    ab_re = mag * jnp.cos(li * dt)
    ab_im = mag * jnp.sin(li * dt)
    den = lr * lr + li * li
    nr = ab_re - 1.0
    ni = ab_im
    z_re = ((nr * lr + ni * li) / den)[..., None]
    z_im = ((ni * lr - nr * li) / den)[..., None]
    br = b_re.astype(jnp.float32)
    bi = b_im.astype(jnp.float32)
    bb_re = z_re * br - z_im * bi
    bb_im = z_re * bi + z_im * br
    x_re = jnp.einsum('bsgh,gph->bsgp', uf, bb_re)
    x_im = jnp.einsum('bsgh,gph->bsgp', uf, bb_im)
    a_re = jnp.broadcast_to(ab_re, (1, s) + ab_re.shape)
    a_im = jnp.broadcast_to(ab_im, (1, s) + ab_im.shape)

    def combine(left, right):
        a1r, a1i, b1r, b1i = left
        a2r, a2i, b2r, b2i = right
        return (a2r * a1r - a2i * a1i,
                a2r * a1i + a2i * a1r,
                a2r * b1r - a2i * b1i + b2r,
                a2r * b1i + a2i * b1r + b2i)

    _, _, h_re, h_im = lax.associative_scan(combine, (a_re, a_im, x_re, x_im), axis=1)
    y = (jnp.einsum('bsgp,ghp->bsgh', h_re, c_re.astype(jnp.float32))
         - jnp.einsum('bsgp,ghp->bsgh', h_im, c_im.astype(jnp.float32)))
    y = (y + d_skip.astype(jnp.float32).reshape(SSM_GROUPS, SSM_GROUP) * uf).reshape(bsz, s, SSM_WIDTH)
    y = jax.nn.gelu(y)
    y = y * jax.nn.sigmoid(y @ w_glu.astype(jnp.float32) + b_glu.astype(jnp.float32))
    return y.astype(u.dtype)


def gmlp_mixer(uv, w_s, b_s, g_v):
    bsz, s, _ = uv.shape
    u, v = jnp.split(jax.nn.gelu(uv), 2, axis=-1)
    v = rmsnorm(v, g_v).reshape(bsz, s // CHUNK, CHUNK, GMLP_HEADS, GMLP_HEAD)
    mask = jnp.tril(jnp.ones((CHUNK, CHUNK), dtype=bool))
    w = jnp.where(mask, w_s, 0)
    gate = jnp.einsum('hts,bnshc->bnthc', w, v) + b_s.T[None, None, :, :, None]
    return u * gate.reshape(bsz, s, GMLP_WIDTH)


def shortconv_mixer(p, w_conv, b_conv):
    bg, cg, hx = jnp.split(p, 3, axis=-1)
    return bg * causal_dwconv(cg * hx, w_conv, b_conv)


def conv_ffn(x, w_up, w_conv, b_conv, w_down):
    h = causal_dwconv(x @ w_up, w_conv, b_conv)
    gate, val = jnp.split(h, 2, axis=-1)
    return (jax.nn.silu(gate) * val) @ w_down


def _fwd_setup_inputs(seed: int = 0) -> dict:
    key = jax.random.key(seed)
    ks = iter(jax.random.split(key, 32))
    f32 = jnp.float32
    nrm = lambda shape, std: std * jax.random.normal(next(ks), shape, f32)
    d = D_MODEL
    inp = {}
    inp["x"] = nrm((BATCH, SEQ, d), 1.0)
    inp["mix_norm_g"] = 1.0 + nrm((DEPTH, d), 0.02)
    inp["ffn_norm_g"] = 1.0 + nrm((DEPTH, d), 0.02)
    inp["final_norm_g"] = 1.0 + nrm((d,), 0.02)
    inp["ev_w_in"] = nrm((N_EVEN, d, EVEN_IN), d ** -0.5)
    inp["ev_w_out"] = nrm((N_EVEN, d, d), d ** -0.5 * RESID_SCALE)
    inp["s5_lam_re"] = -0.5 + nrm((N_EVEN, SSM_GROUPS, SSM_STATE), 0.01)
    n_idx = jnp.arange(SSM_STATE, dtype=f32)
    inp["s5_lam_im"] = math.pi * n_idx + nrm((N_EVEN, SSM_GROUPS, SSM_STATE), 0.01)
    inp["s5_log_dt"] = jax.random.uniform(next(ks), (N_EVEN, SSM_GROUPS), f32,
                                          math.log(DT_MIN), math.log(DT_MAX))
    inp["s5_b_re"] = nrm((N_EVEN, SSM_GROUPS, SSM_STATE, SSM_GROUP), (2 * SSM_GROUP) ** -0.5)
    inp["s5_b_im"] = nrm((N_EVEN, SSM_GROUPS, SSM_STATE, SSM_GROUP), (2 * SSM_GROUP) ** -0.5)
    inp["s5_c_re"] = nrm((N_EVEN, SSM_GROUPS, SSM_GROUP, SSM_STATE), SSM_STATE ** -0.5)
    inp["s5_c_im"] = nrm((N_EVEN, SSM_GROUPS, SSM_GROUP, SSM_STATE), SSM_STATE ** -0.5)
    inp["s5_d"] = nrm((N_EVEN, SSM_WIDTH), 1.0)
    inp["s5_w_glu"] = nrm((N_EVEN, SSM_WIDTH, SSM_WIDTH), SSM_WIDTH ** -0.5)
    inp["s5_b_glu"] = nrm((N_EVEN, SSM_WIDTH), 0.01)
    inp["gm_w_s"] = nrm((N_EVEN, GMLP_HEADS, CHUNK, CHUNK), CHUNK ** -0.5)
    inp["gm_b_s"] = 1.0 + nrm((N_EVEN, GMLP_HEADS, CHUNK), 0.01)
    inp["gm_v_g"] = 1.0 + nrm((N_EVEN, GMLP_WIDTH), 0.02)
    inp["od_w_in"] = nrm((N_ODD, d, 3 * d), d ** -0.5)
    inp["od_conv_w"] = nrm((N_ODD, CONV_WIDTH, d), CONV_WIDTH ** -0.5)
    inp["od_conv_b"] = nrm((N_ODD, d), 0.01)
    inp["od_w_out"] = nrm((N_ODD, d, d), d ** -0.5 * RESID_SCALE)
    inp["ffn_w_up"] = nrm((DEPTH, d, 2 * D_FF), d ** -0.5)
    inp["ffn_conv_w"] = nrm((DEPTH, CONV_WIDTH, 2 * D_FF), CONV_WIDTH ** -0.5)
    inp["ffn_conv_b"] = nrm((DEPTH, 2 * D_FF), 0.01)
    inp["ffn_w_down"] = nrm((DEPTH, D_FF, d), D_FF ** -0.5 * RESID_SCALE)
    return inp


def _fwd_reference(x, mix_norm_g, ffn_norm_g, final_norm_g,
              ev_w_in, ev_w_out, s5_lam_re, s5_lam_im, s5_log_dt,
              s5_b_re, s5_b_im, s5_c_re, s5_c_im, s5_d, s5_w_glu, s5_b_glu,
              gm_w_s, gm_b_s, gm_v_g,
              od_w_in, od_conv_w, od_conv_b, od_w_out,
              ffn_w_up, ffn_conv_w, ffn_conv_b, ffn_w_down):
    h = x
    for layer in range(DEPTH):
        y = rmsnorm(h, mix_norm_g[layer])
        if layer % 2 == 0:
            e = layer // 2
            p = y @ ev_w_in[e]
            a_out = s5_mixer(p[..., :SSM_WIDTH], s5_lam_re[e], s5_lam_im[e], s5_log_dt[e],
                             s5_b_re[e], s5_b_im[e], s5_c_re[e], s5_c_im[e],
                             s5_d[e], s5_w_glu[e], s5_b_glu[e])
            b_out = gmlp_mixer(p[..., SSM_WIDTH:], gm_w_s[e], gm_b_s[e], gm_v_g[e])
            mix = jnp.concatenate([a_out, b_out], axis=-1) @ ev_w_out[e]
        else:
            o = layer // 2
            mix = shortconv_mixer(y @ od_w_in[o], od_conv_w[o], od_conv_b[o]) @ od_w_out[o]
        h = h + mix
        h = h + conv_ffn(rmsnorm(h, ffn_norm_g[layer]), ffn_w_up[layer], ffn_conv_w[layer],
                         ffn_conv_b[layer], ffn_w_down[layer])
    return rmsnorm(h, final_norm_g)


import jax as _jax
import jax.numpy as _jnp

TWIN_FORMAT = 'train_step'
FWD_PARAMS = ['x', 'mix_norm_g', 'ffn_norm_g', 'final_norm_g', 'ev_w_in', 'ev_w_out', 's5_lam_re', 's5_lam_im', 's5_log_dt', 's5_b_re', 's5_b_im', 's5_c_re', 's5_c_im', 's5_d', 's5_w_glu', 's5_b_glu', 'gm_w_s', 'gm_b_s', 'gm_v_g', 'od_w_in', 'od_conv_w', 'od_conv_b', 'od_w_out', 'ffn_w_up', 'ffn_conv_w', 'ffn_conv_b', 'ffn_w_down']
TWIN_WEIGHTS = ['mix_norm_g', 'ffn_norm_g', 'final_norm_g', 'ev_w_in', 'ev_w_out', 's5_lam_re', 's5_lam_im', 's5_log_dt', 's5_b_re', 's5_b_im', 's5_c_re', 's5_c_im', 's5_d', 's5_w_glu', 's5_b_glu', 'gm_w_s', 'gm_b_s', 'gm_v_g', 'od_w_in', 'od_conv_w', 'od_conv_b', 'od_w_out', 'ffn_w_up', 'ffn_conv_w', 'ffn_conv_b', 'ffn_w_down']
TWIN_DIFF_INPUT = 'x'
TWIN_INPUTS = ['x', 'mix_norm_g', 'ffn_norm_g', 'final_norm_g', 'ev_w_in', 'ev_w_out', 's5_lam_re', 's5_lam_im', 's5_log_dt', 's5_b_re', 's5_b_im', 's5_c_re', 's5_c_im', 's5_d', 's5_w_glu', 's5_b_glu', 'gm_w_s', 'gm_b_s', 'gm_v_g', 'od_w_in', 'od_conv_w', 'od_conv_b', 'od_w_out', 'ffn_w_up', 'ffn_conv_w', 'ffn_conv_b', 'ffn_w_down', 'loss_target', 'm_mix_norm_g', 'm_ffn_norm_g', 'm_final_norm_g', 'm_ev_w_in', 'm_ev_w_out', 'm_s5_lam_re', 'm_s5_lam_im', 'm_s5_log_dt', 'm_s5_b_re', 'm_s5_b_im', 'm_s5_c_re', 'm_s5_c_im', 'm_s5_d', 'm_s5_w_glu', 'm_s5_b_glu', 'm_gm_w_s', 'm_gm_b_s', 'm_gm_v_g', 'm_od_w_in', 'm_od_conv_w', 'm_od_conv_b', 'm_od_w_out', 'm_ffn_w_up', 'm_ffn_conv_w', 'm_ffn_conv_b', 'm_ffn_w_down', 'v_mix_norm_g', 'v_ffn_norm_g', 'v_final_norm_g', 'v_ev_w_in', 'v_ev_w_out', 'v_s5_lam_re', 'v_s5_lam_im', 'v_s5_log_dt', 'v_s5_b_re', 'v_s5_b_im', 'v_s5_c_re', 'v_s5_c_im', 'v_s5_d', 'v_s5_w_glu', 'v_s5_b_glu', 'v_gm_w_s', 'v_gm_b_s', 'v_gm_v_g', 'v_od_w_in', 'v_od_conv_w', 'v_od_conv_b', 'v_od_w_out', 'v_ffn_w_up', 'v_ffn_conv_w', 'v_ffn_conv_b', 'v_ffn_w_down']
TWIN_OUTPUTS = ['loss', 'grad_x', 'grad_mix_norm_g', 'grad_ffn_norm_g', 'grad_final_norm_g', 'grad_ev_w_in', 'grad_ev_w_out', 'grad_s5_lam_re', 'grad_s5_lam_im', 'grad_s5_log_dt', 'grad_s5_b_re', 'grad_s5_b_im', 'grad_s5_c_re', 'grad_s5_c_im', 'grad_s5_d', 'grad_s5_w_glu', 'grad_s5_b_glu', 'grad_gm_w_s', 'grad_gm_b_s', 'grad_gm_v_g', 'grad_od_w_in', 'grad_od_conv_w', 'grad_od_conv_b', 'grad_od_w_out', 'grad_ffn_w_up', 'grad_ffn_conv_w', 'grad_ffn_conv_b', 'grad_ffn_w_down', 'delta_mix_norm_g', 'delta_ffn_norm_g', 'delta_final_norm_g', 'delta_ev_w_in', 'delta_ev_w_out', 'delta_s5_lam_re', 'delta_s5_lam_im', 'delta_s5_log_dt', 'delta_s5_b_re', 'delta_s5_b_im', 'delta_s5_c_re', 'delta_s5_c_im', 'delta_s5_d', 'delta_s5_w_glu', 'delta_s5_b_glu', 'delta_gm_w_s', 'delta_gm_b_s', 'delta_gm_v_g', 'delta_od_w_in', 'delta_od_conv_w', 'delta_od_conv_b', 'delta_od_w_out', 'delta_ffn_w_up', 'delta_ffn_conv_w', 'delta_ffn_conv_b', 'delta_ffn_w_down', 'new_m_mix_norm_g', 'new_m_ffn_norm_g', 'new_m_final_norm_g', 'new_m_ev_w_in', 'new_m_ev_w_out', 'new_m_s5_lam_re', 'new_m_s5_lam_im', 'new_m_s5_log_dt', 'new_m_s5_b_re', 'new_m_s5_b_im', 'new_m_s5_c_re', 'new_m_s5_c_im', 'new_m_s5_d', 'new_m_s5_w_glu', 'new_m_s5_b_glu', 'new_m_gm_w_s', 'new_m_gm_b_s', 'new_m_gm_v_g', 'new_m_od_w_in', 'new_m_od_conv_w', 'new_m_od_conv_b', 'new_m_od_w_out', 'new_m_ffn_w_up', 'new_m_ffn_conv_w', 'new_m_ffn_conv_b', 'new_m_ffn_w_down', 'new_v_mix_norm_g', 'new_v_ffn_norm_g', 'new_v_final_norm_g', 'new_v_ev_w_in', 'new_v_ev_w_out', 'new_v_s5_lam_re', 'new_v_s5_lam_im', 'new_v_s5_log_dt', 'new_v_s5_b_re', 'new_v_s5_b_im', 'new_v_s5_c_re', 'new_v_s5_c_im', 'new_v_s5_d', 'new_v_s5_w_glu', 'new_v_s5_b_glu', 'new_v_gm_w_s', 'new_v_gm_b_s', 'new_v_gm_v_g', 'new_v_od_w_in', 'new_v_od_conv_w', 'new_v_od_conv_b', 'new_v_od_w_out', 'new_v_ffn_w_up', 'new_v_ffn_conv_w', 'new_v_ffn_conv_b', 'new_v_ffn_w_down']
TWIN_LEAF_KINDS = {'loss': 'loss', 'grad_x': 'grad_x', 'grad_mix_norm_g': 'grad_w', 'grad_ffn_norm_g': 'grad_w', 'grad_final_norm_g': 'grad_w', 'grad_ev_w_in': 'grad_w', 'grad_ev_w_out': 'grad_w', 'grad_s5_lam_re': 'grad_w', 'grad_s5_lam_im': 'grad_w', 'grad_s5_log_dt': 'grad_w', 'grad_s5_b_re': 'grad_w', 'grad_s5_b_im': 'grad_w', 'grad_s5_c_re': 'grad_w', 'grad_s5_c_im': 'grad_w', 'grad_s5_d': 'grad_w', 'grad_s5_w_glu': 'grad_w', 'grad_s5_b_glu': 'grad_w', 'grad_gm_w_s': 'grad_w', 'grad_gm_b_s': 'grad_w', 'grad_gm_v_g': 'grad_w', 'grad_od_w_in': 'grad_w', 'grad_od_conv_w': 'grad_w', 'grad_od_conv_b': 'grad_w', 'grad_od_w_out': 'grad_w', 'grad_ffn_w_up': 'grad_w', 'grad_ffn_conv_w': 'grad_w', 'grad_ffn_conv_b': 'grad_w', 'grad_ffn_w_down': 'grad_w', 'delta_mix_norm_g': 'delta_w', 'delta_ffn_norm_g': 'delta_w', 'delta_final_norm_g': 'delta_w', 'delta_ev_w_in': 'delta_w', 'delta_ev_w_out': 'delta_w', 'delta_s5_lam_re': 'delta_w', 'delta_s5_lam_im': 'delta_w', 'delta_s5_log_dt': 'delta_w', 'delta_s5_b_re': 'delta_w', 'delta_s5_b_im': 'delta_w', 'delta_s5_c_re': 'delta_w', 'delta_s5_c_im': 'delta_w', 'delta_s5_d': 'delta_w', 'delta_s5_w_glu': 'delta_w', 'delta_s5_b_glu': 'delta_w', 'delta_gm_w_s': 'delta_w', 'delta_gm_b_s': 'delta_w', 'delta_gm_v_g': 'delta_w', 'delta_od_w_in': 'delta_w', 'delta_od_conv_w': 'delta_w', 'delta_od_conv_b': 'delta_w', 'delta_od_w_out': 'delta_w', 'delta_ffn_w_up': 'delta_w', 'delta_ffn_conv_w': 'delta_w', 'delta_ffn_conv_b': 'delta_w', 'delta_ffn_w_down': 'delta_w', 'new_m_mix_norm_g': 'new_m', 'new_m_ffn_norm_g': 'new_m', 'new_m_final_norm_g': 'new_m', 'new_m_ev_w_in': 'new_m', 'new_m_ev_w_out': 'new_m', 'new_m_s5_lam_re': 'new_m', 'new_m_s5_lam_im': 'new_m', 'new_m_s5_log_dt': 'new_m', 'new_m_s5_b_re': 'new_m', 'new_m_s5_b_im': 'new_m', 'new_m_s5_c_re': 'new_m', 'new_m_s5_c_im': 'new_m', 'new_m_s5_d': 'new_m', 'new_m_s5_w_glu': 'new_m', 'new_m_s5_b_glu': 'new_m', 'new_m_gm_w_s': 'new_m', 'new_m_gm_b_s': 'new_m', 'new_m_gm_v_g': 'new_m', 'new_m_od_w_in': 'new_m', 'new_m_od_conv_w': 'new_m', 'new_m_od_conv_b': 'new_m', 'new_m_od_w_out': 'new_m', 'new_m_ffn_w_up': 'new_m', 'new_m_ffn_conv_w': 'new_m', 'new_m_ffn_conv_b': 'new_m', 'new_m_ffn_w_down': 'new_m', 'new_v_mix_norm_g': 'new_v', 'new_v_ffn_norm_g': 'new_v', 'new_v_final_norm_g': 'new_v', 'new_v_ev_w_in': 'new_v', 'new_v_ev_w_out': 'new_v', 'new_v_s5_lam_re': 'new_v', 'new_v_s5_lam_im': 'new_v', 'new_v_s5_log_dt': 'new_v', 'new_v_s5_b_re': 'new_v', 'new_v_s5_b_im': 'new_v', 'new_v_s5_c_re': 'new_v', 'new_v_s5_c_im': 'new_v', 'new_v_s5_d': 'new_v', 'new_v_s5_w_glu': 'new_v', 'new_v_s5_b_glu': 'new_v', 'new_v_gm_w_s': 'new_v', 'new_v_gm_b_s': 'new_v', 'new_v_gm_v_g': 'new_v', 'new_v_od_w_in': 'new_v', 'new_v_od_conv_w': 'new_v', 'new_v_od_conv_b': 'new_v', 'new_v_od_w_out': 'new_v', 'new_v_ffn_w_up': 'new_v', 'new_v_ffn_conv_w': 'new_v', 'new_v_ffn_conv_b': 'new_v', 'new_v_ffn_w_down': 'new_v'}


def _forward(args):
    return _fwd_reference(*[args[k] for k in FWD_PARAMS])


def _output_shape():
    out = _jax.eval_shape(lambda: _forward(_fwd_setup_inputs(0)))
    return out.shape, out.dtype

N_MICROBATCH = 1
ADAM_LR = 0.001
ADAM_B1 = 0.9
ADAM_B2 = 0.999
ADAM_EPS = 1e-08
ADAM_WD = 0.01
ADAM_STEP = 10
PER_EXAMPLE_BATCH_AXIS = {'x': 0, 'loss_target': 0}
SHARED_INPUTS = []
_WEIGHT_DTYPES = {'mix_norm_g': _jnp.float32, 'ffn_norm_g': _jnp.float32, 'final_norm_g': _jnp.float32, 'ev_w_in': _jnp.float32, 'ev_w_out': _jnp.float32, 's5_lam_re': _jnp.float32, 's5_lam_im': _jnp.float32, 's5_log_dt': _jnp.float32, 's5_b_re': _jnp.float32, 's5_b_im': _jnp.float32, 's5_c_re': _jnp.float32, 's5_c_im': _jnp.float32, 's5_d': _jnp.float32, 's5_w_glu': _jnp.float32, 's5_b_glu': _jnp.float32, 'gm_w_s': _jnp.float32, 'gm_b_s': _jnp.float32, 'gm_v_g': _jnp.float32, 'od_w_in': _jnp.float32, 'od_conv_w': _jnp.float32, 'od_conv_b': _jnp.float32, 'od_w_out': _jnp.float32, 'ffn_w_up': _jnp.float32, 'ffn_conv_w': _jnp.float32, 'ffn_conv_b': _jnp.float32, 'ffn_w_down': _jnp.float32}
MOMENT_SCALE = {'mix_norm_g': 1.601308e-01, 'ffn_norm_g': 1.087929e-01, 'final_norm_g': 6.392622e+01, 'ev_w_in': 9.428723e-02, 'ev_w_out': 2.154424e-01, 's5_lam_re': 4.168935e-03, 's5_lam_im': 5.662345e-03, 's5_log_dt': 3.359128e+00, 's5_b_re': 2.479561e-03, 's5_b_im': 2.366771e-03, 's5_c_re': 3.576649e-03, 's5_c_im': 3.380052e-03, 's5_d': 6.487550e-02, 's5_w_glu': 1.699073e-02, 's5_b_glu': 2.398208e-02, 'gm_w_s': 6.707085e-02, 'gm_b_s': 9.887909e-02, 'gm_v_g': 6.746102e-02, 'od_w_in': 1.064295e-01, 'od_conv_w': 1.100644e-01, 'od_conv_b': 1.074700e-01, 'od_w_out': 2.124039e-01, 'ffn_w_up': 4.474713e-02, 'ffn_conv_w': 4.574553e-02, 'ffn_conv_b': 4.445127e-02, 'ffn_w_down': 1.456842e-01}


def _to_microbatches(a, axis):
    t = _jnp.moveaxis(a, axis, 0)
    t = t.reshape((N_MICROBATCH, t.shape[0] // N_MICROBATCH) + t.shape[1:])
    return _jnp.moveaxis(t, 1, axis + 1)


def setup_inputs(seed: int = 0) -> dict:
    inp = _fwd_setup_inputs(seed)
    key = _jax.random.fold_in(_jax.random.key(seed), 7919)
    shape, _ = _output_shape()
    out = dict(inp)
    out["loss_target"] = _jax.random.normal(_jax.random.fold_in(key, 0), shape, _jnp.float32)
    for i, name in enumerate(TWIN_WEIGHTS):
        w = inp[name].astype(_jnp.float32)
        if MOMENT_SCALE is None:
            s = _jnp.sqrt(_jnp.mean(_jnp.square(w)) + 1e-30)
        else:
            s = MOMENT_SCALE[name]
        km, kv = _jax.random.split(_jax.random.fold_in(key, i + 1))
        out[name] = w
        out["m_" + name] = s * _jax.random.normal(km, w.shape, _jnp.float32)
        out["v_" + name] = (s * s) * _jax.random.uniform(kv, w.shape, _jnp.float32, 0.5, 1.5)
    if N_MICROBATCH > 1:
        for name, axis in PER_EXAMPLE_BATCH_AXIS.items():
            out[name] = _to_microbatches(out[name], axis)
    return {'x': out['x'], 'mix_norm_g': out['mix_norm_g'], 'ffn_norm_g': out['ffn_norm_g'], 'final_norm_g': out['final_norm_g'], 'ev_w_in': out['ev_w_in'], 'ev_w_out': out['ev_w_out'], 's5_lam_re': out['s5_lam_re'], 's5_lam_im': out['s5_lam_im'], 's5_log_dt': out['s5_log_dt'], 's5_b_re': out['s5_b_re'], 's5_b_im': out['s5_b_im'], 's5_c_re': out['s5_c_re'], 's5_c_im': out['s5_c_im'], 's5_d': out['s5_d'], 's5_w_glu': out['s5_w_glu'], 's5_b_glu': out['s5_b_glu'], 'gm_w_s': out['gm_w_s'], 'gm_b_s': out['gm_b_s'], 'gm_v_g': out['gm_v_g'], 'od_w_in': out['od_w_in'], 'od_conv_w': out['od_conv_w'], 'od_conv_b': out['od_conv_b'], 'od_w_out': out['od_w_out'], 'ffn_w_up': out['ffn_w_up'], 'ffn_conv_w': out['ffn_conv_w'], 'ffn_conv_b': out['ffn_conv_b'], 'ffn_w_down': out['ffn_w_down'], 'loss_target': out['loss_target'], 'm_mix_norm_g': out['m_mix_norm_g'], 'm_ffn_norm_g': out['m_ffn_norm_g'], 'm_final_norm_g': out['m_final_norm_g'], 'm_ev_w_in': out['m_ev_w_in'], 'm_ev_w_out': out['m_ev_w_out'], 'm_s5_lam_re': out['m_s5_lam_re'], 'm_s5_lam_im': out['m_s5_lam_im'], 'm_s5_log_dt': out['m_s5_log_dt'], 'm_s5_b_re': out['m_s5_b_re'], 'm_s5_b_im': out['m_s5_b_im'], 'm_s5_c_re': out['m_s5_c_re'], 'm_s5_c_im': out['m_s5_c_im'], 'm_s5_d': out['m_s5_d'], 'm_s5_w_glu': out['m_s5_w_glu'], 'm_s5_b_glu': out['m_s5_b_glu'], 'm_gm_w_s': out['m_gm_w_s'], 'm_gm_b_s': out['m_gm_b_s'], 'm_gm_v_g': out['m_gm_v_g'], 'm_od_w_in': out['m_od_w_in'], 'm_od_conv_w': out['m_od_conv_w'], 'm_od_conv_b': out['m_od_conv_b'], 'm_od_w_out': out['m_od_w_out'], 'm_ffn_w_up': out['m_ffn_w_up'], 'm_ffn_conv_w': out['m_ffn_conv_w'], 'm_ffn_conv_b': out['m_ffn_conv_b'], 'm_ffn_w_down': out['m_ffn_w_down'], 'v_mix_norm_g': out['v_mix_norm_g'], 'v_ffn_norm_g': out['v_ffn_norm_g'], 'v_final_norm_g': out['v_final_norm_g'], 'v_ev_w_in': out['v_ev_w_in'], 'v_ev_w_out': out['v_ev_w_out'], 'v_s5_lam_re': out['v_s5_lam_re'], 'v_s5_lam_im': out['v_s5_lam_im'], 'v_s5_log_dt': out['v_s5_log_dt'], 'v_s5_b_re': out['v_s5_b_re'], 'v_s5_b_im': out['v_s5_b_im'], 'v_s5_c_re': out['v_s5_c_re'], 'v_s5_c_im': out['v_s5_c_im'], 'v_s5_d': out['v_s5_d'], 'v_s5_w_glu': out['v_s5_w_glu'], 'v_s5_b_glu': out['v_s5_b_glu'], 'v_gm_w_s': out['v_gm_w_s'], 'v_gm_b_s': out['v_gm_b_s'], 'v_gm_v_g': out['v_gm_v_g'], 'v_od_w_in': out['v_od_w_in'], 'v_od_conv_w': out['v_od_conv_w'], 'v_od_conv_b': out['v_od_conv_b'], 'v_od_w_out': out['v_od_w_out'], 'v_ffn_w_up': out['v_ffn_w_up'], 'v_ffn_conv_w': out['v_ffn_conv_w'], 'v_ffn_conv_b': out['v_ffn_conv_b'], 'v_ffn_w_down': out['v_ffn_w_down']}


def _loss(weights, diff, rest, loss_target):
    with _jax.named_scope("forward"):
        args = {**rest, TWIN_DIFF_INPUT: diff, **{k: w.astype(_WEIGHT_DTYPES[k]) for k, w in weights.items()}}
        y = _forward(args)
    with _jax.named_scope("loss_head"):
        err = _jnp.square(y.astype(_jnp.float32) - loss_target)
        return 0.5 * _jnp.sum(_jnp.mean(err, axis=-1)) if err.ndim else 0.5 * err


def _adamw(w, g, m, v):
    m = ADAM_B1 * m + (1.0 - ADAM_B1) * g
    v = ADAM_B2 * v + (1.0 - ADAM_B2) * _jnp.square(g)
    m_hat = m / (1.0 - ADAM_B1 ** ADAM_STEP)
    v_hat = v / (1.0 - ADAM_B2 ** ADAM_STEP)
    delta = -ADAM_LR * (m_hat / (_jnp.sqrt(v_hat) + ADAM_EPS) + ADAM_WD * w)
    return delta, m, v


def reference(x, mix_norm_g, ffn_norm_g, final_norm_g, ev_w_in, ev_w_out, s5_lam_re, s5_lam_im, s5_log_dt, s5_b_re, s5_b_im, s5_c_re, s5_c_im, s5_d, s5_w_glu, s5_b_glu, gm_w_s, gm_b_s, gm_v_g, od_w_in, od_conv_w, od_conv_b, od_w_out, ffn_w_up, ffn_conv_w, ffn_conv_b, ffn_w_down, loss_target, m_mix_norm_g, m_ffn_norm_g, m_final_norm_g, m_ev_w_in, m_ev_w_out, m_s5_lam_re, m_s5_lam_im, m_s5_log_dt, m_s5_b_re, m_s5_b_im, m_s5_c_re, m_s5_c_im, m_s5_d, m_s5_w_glu, m_s5_b_glu, m_gm_w_s, m_gm_b_s, m_gm_v_g, m_od_w_in, m_od_conv_w, m_od_conv_b, m_od_w_out, m_ffn_w_up, m_ffn_conv_w, m_ffn_conv_b, m_ffn_w_down, v_mix_norm_g, v_ffn_norm_g, v_final_norm_g, v_ev_w_in, v_ev_w_out, v_s5_lam_re, v_s5_lam_im, v_s5_log_dt, v_s5_b_re, v_s5_b_im, v_s5_c_re, v_s5_c_im, v_s5_d, v_s5_w_glu, v_s5_b_glu, v_gm_w_s, v_gm_b_s, v_gm_v_g, v_od_w_in, v_od_conv_w, v_od_conv_b, v_od_w_out, v_ffn_w_up, v_ffn_conv_w, v_ffn_conv_b, v_ffn_w_down):
    given = dict(x=x, mix_norm_g=mix_norm_g, ffn_norm_g=ffn_norm_g, final_norm_g=final_norm_g, ev_w_in=ev_w_in, ev_w_out=ev_w_out, s5_lam_re=s5_lam_re, s5_lam_im=s5_lam_im, s5_log_dt=s5_log_dt, s5_b_re=s5_b_re, s5_b_im=s5_b_im, s5_c_re=s5_c_re, s5_c_im=s5_c_im, s5_d=s5_d, s5_w_glu=s5_w_glu, s5_b_glu=s5_b_glu, gm_w_s=gm_w_s, gm_b_s=gm_b_s, gm_v_g=gm_v_g, od_w_in=od_w_in, od_conv_w=od_conv_w, od_conv_b=od_conv_b, od_w_out=od_w_out, ffn_w_up=ffn_w_up, ffn_conv_w=ffn_conv_w, ffn_conv_b=ffn_conv_b, ffn_w_down=ffn_w_down, loss_target=loss_target, m_mix_norm_g=m_mix_norm_g, m_ffn_norm_g=m_ffn_norm_g, m_final_norm_g=m_final_norm_g, m_ev_w_in=m_ev_w_in, m_ev_w_out=m_ev_w_out, m_s5_lam_re=m_s5_lam_re, m_s5_lam_im=m_s5_lam_im, m_s5_log_dt=m_s5_log_dt, m_s5_b_re=m_s5_b_re, m_s5_b_im=m_s5_b_im, m_s5_c_re=m_s5_c_re, m_s5_c_im=m_s5_c_im, m_s5_d=m_s5_d, m_s5_w_glu=m_s5_w_glu, m_s5_b_glu=m_s5_b_glu, m_gm_w_s=m_gm_w_s, m_gm_b_s=m_gm_b_s, m_gm_v_g=m_gm_v_g, m_od_w_in=m_od_w_in, m_od_conv_w=m_od_conv_w, m_od_conv_b=m_od_conv_b, m_od_w_out=m_od_w_out, m_ffn_w_up=m_ffn_w_up, m_ffn_conv_w=m_ffn_conv_w, m_ffn_conv_b=m_ffn_conv_b, m_ffn_w_down=m_ffn_w_down, v_mix_norm_g=v_mix_norm_g, v_ffn_norm_g=v_ffn_norm_g, v_final_norm_g=v_final_norm_g, v_ev_w_in=v_ev_w_in, v_ev_w_out=v_ev_w_out, v_s5_lam_re=v_s5_lam_re, v_s5_lam_im=v_s5_lam_im, v_s5_log_dt=v_s5_log_dt, v_s5_b_re=v_s5_b_re, v_s5_b_im=v_s5_b_im, v_s5_c_re=v_s5_c_re, v_s5_c_im=v_s5_c_im, v_s5_d=v_s5_d, v_s5_w_glu=v_s5_w_glu, v_s5_b_glu=v_s5_b_glu, v_gm_w_s=v_gm_w_s, v_gm_b_s=v_gm_b_s, v_gm_v_g=v_gm_v_g, v_od_w_in=v_od_w_in, v_od_conv_w=v_od_conv_w, v_od_conv_b=v_od_conv_b, v_od_w_out=v_od_w_out, v_ffn_w_up=v_ffn_w_up, v_ffn_conv_w=v_ffn_conv_w, v_ffn_conv_b=v_ffn_conv_b, v_ffn_w_down=v_ffn_w_down)
    weights = {n: given[n] for n in TWIN_WEIGHTS}
    shared = {n: given[n] for n in SHARED_INPUTS}
    per_example = {n: given[n] for n in ['x']}
    grad_fn = _jax.value_and_grad(_loss, argnums=(0, 1))

    def one_microbatch(ex, loss_target):
        ex = dict(ex)
        diff = ex.pop(TWIN_DIFF_INPUT)
        return grad_fn(weights, diff, {**shared, **ex}, loss_target)

    if N_MICROBATCH == 1:
        loss, (grad_w, grad_x) = one_microbatch(per_example, given["loss_target"])
    else:
        def body(carry, xs):
            loss_sum, grad_sum = carry
            l_k, (gw_k, gx_k) = one_microbatch(xs[0], xs[1])
            with _jax.named_scope("update"):
                return (loss_sum + l_k, _jax.tree.map(_jnp.add, grad_sum, gw_k)), gx_k

        init = (_jnp.zeros((), _jnp.float32), _jax.tree.map(_jnp.zeros_like, weights))
        (loss, grad_w), grad_x = _jax.lax.scan(body, init, (per_example, given["loss_target"]))
    with _jax.named_scope("update"):
        delta_w, new_m, new_v = {}, {}, {}
        for n in TWIN_WEIGHTS:
            delta_w[n], new_m[n], new_v[n] = _adamw(weights[n], grad_w[n], given["m_" + n], given["v_" + n])
    return (loss, grad_x, *[grad_w[n] for n in TWIN_WEIGHTS], *[delta_w[n] for n in TWIN_WEIGHTS],
            *[new_m[n] for n in TWIN_WEIGHTS], *[new_v[n] for n in TWIN_WEIGHTS])
```

```python
import functools
import math

import jax
import jax.numpy as jnp
from jax import lax
from jax.experimental import pallas as pl
from jax.experimental.pallas import tpu as pltpu

f32 = jnp.float32
bf16 = jnp.bfloat16
MESH = pl.DeviceIdType.MESH

SSM_GROUP = 16
SSM_STATE = 64
GMLP_HEAD = 128
CHUNK = 128
EPS = 1e-6
LAMBDA_RE_MAX = -1e-4
ADAM_LR, ADAM_B1, ADAM_B2, ADAM_EPS, ADAM_WD, ADAM_STEP = 0.001, 0.9, 0.999, 1e-08, 0.01, 10

LANES = 128
SUBLANES = 8
HALO = 16
VMEM_LIMIT = 56 * 1024 * 1024
N_CHIPS = 4

WEIGHTS = ['mix_norm_g', 'ffn_norm_g', 'final_norm_g', 'ev_w_in', 'ev_w_out', 's5_lam_re', 's5_lam_im', 's5_log_dt',
           's5_b_re', 's5_b_im', 's5_c_re', 's5_c_im', 's5_d', 's5_w_glu', 's5_b_glu', 'gm_w_s', 'gm_b_s', 'gm_v_g',
           'od_w_in', 'od_conv_w', 'od_conv_b', 'od_w_out', 'ffn_w_up', 'ffn_conv_w', 'ffn_conv_b', 'ffn_w_down']
SHARD_AXIS = {'ev_w_in': 2, 'ev_w_out': 1, 's5_w_glu': 1, 'od_w_in': 2, 'od_conv_w': 2, 'od_conv_b': 1, 'od_w_out': 1,
              'ffn_w_up': 2, 'ffn_conv_w': 2, 'ffn_w_down': 1}
GATHER_BF16 = ['ev_w_in', 'ev_w_out', 's5_w_glu', 'od_w_in', 'od_w_out', 'ffn_w_up', 'ffn_w_down']
GATHER_F32 = ['od_conv_w', 'od_conv_b', 'ffn_conv_w']

_GELU_K0 = math.sqrt(2.0 / math.pi)
_GELU_K1 = 0.044715
NT = (((1,), (1,)), ((), ()))
TN = (((0,), (0,)), ((), ()))


def _pick(n, cap):
    if n <= cap:
        return n
    best = None
    for d in range(LANES, cap + 1, LANES):
        if n % d == 0:
            best = d
    assert best is not None, (n, cap)
    return best


def _params(sem=None):
    return pltpu.CompilerParams(dimension_semantics=sem, vmem_limit_bytes=VMEM_LIMIT)


def _gelu(x):
    return 0.5 * x * (1.0 + jnp.tanh(_GELU_K0 * (x + _GELU_K1 * x * x * x)))


def _gelu_grad(x):
    t = jnp.tanh(_GELU_K0 * (x + _GELU_K1 * x * x * x))
    return 0.5 * (1.0 + t) + 0.5 * x * (1.0 - t * t) * _GELU_K0 * (1.0 + 3.0 * _GELU_K1 * x * x)


def _rms_stats(x):
    r = lax.rsqrt(jnp.mean(x * x, axis=-1, keepdims=True) + EPS)
    return x * r, r


def _rms_bwd(dy, xh, r, g):
    dxh = dy * g
    dx = r * (dxh - xh * jnp.mean(dxh * xh, axis=-1, keepdims=True))
    return dx, jnp.sum(dy * xh, axis=0, keepdims=True)


def _dot(a, b):
    return jnp.dot(a, b, preferred_element_type=f32)


def _dg(a, b, dims):
    return lax.dot_general(a, b, dims, preferred_element_type=f32)


def _row_fold(z):
    return z.reshape(z.shape[0] // SUBLANES, SUBLANES, z.shape[1]).sum(axis=0)


def _norm_mm(name, h, g, w, tm):
    T, D = h.shape
    N = w.shape[1]
    nc = _pick(N, 512)

    def body(h_ref, g_ref, w_ref, y_ref, o_ref):
        xh, _ = _rms_stats(h_ref[...])
        y = (xh * g_ref[...]).astype(bf16)
        y_ref[...] = y
        for j in range(N // nc):
            o_ref[:, j * nc:(j + 1) * nc] = _dot(y, w_ref[:, j * nc:(j + 1) * nc]).astype(bf16)

    return pl.pallas_call(
        body, name=name, grid=(T // tm,),
        in_specs=[pl.BlockSpec((tm, D), lambda i: (i, 0)), pl.BlockSpec((1, D), lambda i: (0, 0)),
                  pl.BlockSpec((D, N), lambda i: (0, 0))],
        out_specs=[pl.BlockSpec((tm, D), lambda i: (i, 0)), pl.BlockSpec((tm, N), lambda i: (i, 0))],
        out_shape=[jax.ShapeDtypeStruct((T, D), bf16), jax.ShapeDtypeStruct((T, N), bf16)],
        compiler_params=_params(("parallel",)))(h, g.reshape(1, D), w)


def _mm_resid(name, a, w, resid, tm):
    T, K = a.shape
    N = w.shape[1]

    def body(a_ref, w_ref, r_ref, o_ref):
        o_ref[...] = r_ref[...] + _dot(a_ref[...], w_ref[...])

    return pl.pallas_call(
        body, name=name, grid=(T // tm,),
        in_specs=[pl.BlockSpec((tm, K), lambda i: (i, 0)), pl.BlockSpec((K, N), lambda i: (0, 0)),
                  pl.BlockSpec((tm, N), lambda i: (i, 0))],
        out_specs=pl.BlockSpec((tm, N), lambda i: (i, 0)),
        out_shape=jax.ShapeDtypeStruct((T, N), f32),
        compiler_params=_params(("parallel",)))(a, w, resid)


def _mm_nt(name, dy, w, tm):
    T, N = dy.shape
    K = w.shape[0]
    kc = _pick(K, 512)

    def body(d_ref, w_ref, o_ref):
        d = d_ref[...].astype(bf16)
        for j in range(K // kc):
            o_ref[:, j * kc:(j + 1) * kc] = _dg(d, w_ref[j * kc:(j + 1) * kc, :], NT).astype(bf16)

    return pl.pallas_call(
        body, name=name, grid=(T // tm,),
        in_specs=[pl.BlockSpec((tm, N), lambda i: (i, 0)), pl.BlockSpec((K, N), lambda i: (0, 0))],
        out_specs=pl.BlockSpec((tm, K), lambda i: (i, 0)),
        out_shape=jax.ShapeDtypeStruct((T, K), bf16),
        compiler_params=_params(("parallel",)))(dy, w)


def _mm_nt_normbwd(name, dy, w, h, g, dh_in, tm):
    T, N = dy.shape
    D = w.shape[0]

    def body(d_ref, w_ref, h_ref, g_ref, dh_ref, o_ref, dg_ref):
        dz = _dg(d_ref[...].astype(bf16), w_ref[...], NT)
        xh, r = _rms_stats(h_ref[...])
        dx, dg = _rms_bwd(dz, xh, r, g_ref[...])
        o_ref[...] = dh_ref[...] + dx

        @pl.when(pl.program_id(0) == 0)
        def _():
            dg_ref[...] = jnp.zeros_like(dg_ref)
        dg_ref[...] += dg

    return pl.pallas_call(
        body, name=name, grid=(T // tm,),
        in_specs=[pl.BlockSpec((tm, N), lambda i: (i, 0)), pl.BlockSpec((D, N), lambda i: (0, 0)),
                  pl.BlockSpec((tm, D), lambda i: (i, 0)), pl.BlockSpec((1, D), lambda i: (0, 0)),
                  pl.BlockSpec((tm, D), lambda i: (i, 0))],
        out_specs=[pl.BlockSpec((tm, D), lambda i: (i, 0)), pl.BlockSpec((1, D), lambda i: (0, 0))],
        out_shape=[jax.ShapeDtypeStruct((T, D), f32), jax.ShapeDtypeStruct((1, D), f32)],
        compiler_params=_params(("arbitrary",)))(dy, w, h, g.reshape(1, D), dh_in)


def _mm_tn(name, a, b, tt):
    T, K = a.shape
    N = b.shape[1]
    tk = _pick(K, 512)
    tn = _pick(N, 1024)

    def body(a_ref, b_ref, o_ref):
        @pl.when(pl.program_id(2) == 0)
        def _():
            o_ref[...] = jnp.zeros_like(o_ref)
        o_ref[...] += _dg(a_ref[...].astype(bf16), b_ref[...].astype(bf16), TN)

    return pl.pallas_call(
        body, name=name, grid=(K // tk, N // tn, T // tt),
        in_specs=[pl.BlockSpec((tt, tk), lambda k, n, t: (t, k)), pl.BlockSpec((tt, tn), lambda k, n, t: (t, n))],
        out_specs=pl.BlockSpec((tk, tn), lambda k, n, t: (k, n)),
        out_shape=jax.ShapeDtypeStruct((K, N), f32),
        compiler_params=_params(("parallel", "parallel", "arbitrary")))(a, b)


def _final_loss(h, g, tgt, tm):
    T, D = h.shape

    def body(h_ref, g_ref, t_ref, dh_ref, loss_ref, dg_ref):
        xh, r = _rms_stats(h_ref[...])
        gg = g_ref[...]
        diff = xh * gg - t_ref[...]
        dy = diff * (1.0 / D)
        dx, dg = _rms_bwd(dy, xh, r, gg)
        dh_ref[...] = dx

        @pl.when(pl.program_id(0) == 0)
        def _():
            dg_ref[...] = jnp.zeros_like(dg_ref)
            loss_ref[...] = jnp.zeros_like(loss_ref)
        dg_ref[...] += dg
        loss_ref[...] += (0.5 / D) * jnp.sum(jnp.sum(diff * diff, axis=1, keepdims=True), axis=0, keepdims=True)

    return pl.pallas_call(
        body, name="final_loss", grid=(T // tm,),
        in_specs=[pl.BlockSpec((tm, D), lambda i: (i, 0)), pl.BlockSpec((1, D), lambda i: (0, 0)),
                  pl.BlockSpec((tm, D), lambda i: (i, 0))],
        out_specs=[pl.BlockSpec((tm, D), lambda i: (i, 0)), pl.BlockSpec((1, 1), lambda i: (0, 0)),
                   pl.BlockSpec((1, D), lambda i: (0, 0))],
        out_shape=[jax.ShapeDtypeStruct((T, D), f32), jax.ShapeDtypeStruct((1, 1), f32),
                   jax.ShapeDtypeStruct((1, D), f32)],
        compiler_params=_params(("arbitrary",)))(h, g.reshape(1, D), tgt)


def _taps(load, r0, R):
    main = load(r0, R)
    hs = pl.multiple_of(jnp.maximum(r0 - HALO, 0), HALO)
    halo = load(hs, HALO) * (r0 > 0).astype(f32)
    ext = jnp.concatenate([halo, main], axis=0)
    xm1 = pltpu.roll(ext, 1, 0)[HALO:]
    xm2 = pltpu.roll(ext, 2, 0)[HALO:]
    return xm2, xm1, main


def _rev_conv(ref, w, r0, R, S):
    main = ref[pl.ds(r0, R), :]
    hs = pl.multiple_of(jnp.minimum(r0 + R, S - HALO), HALO)
    halo = ref[pl.ds(hs, HALO), :] * (r0 + R < S).astype(f32)
    ext = jnp.concatenate([main, halo], axis=0)
    n = R + HALO
    xp1 = pltpu.roll(ext, n - 1, 0)[:R]
    xp2 = pltpu.roll(ext, n - 2, 0)[:R]
    return w[2:3] * main + w[1:2] * xp1 + w[0:1] * xp2


def _conv(w, b, taps):
    return b + w[0:1] * taps[0] + w[1:2] * taps[1] + w[2:3] * taps[2]


def _ref_load(ref):
    return lambda s, n: ref[pl.ds(s, n), :].astype(f32)


def _ffn_act(name, up3, cw, cb):
    B, S, F2 = up3.shape
    F = F2 // 2
    cwid = _pick(F, 256)
    nF = F // cwid
    R = min(256, S)

    def body(g_ref, v_ref, wg_ref, wv_ref, bg_ref, bv_ref, o_ref):
        wg, wv, bg, bv = wg_ref[...], wv_ref[...], bg_ref[...], bv_ref[...]

        def chunk(r, c):
            r0 = pl.multiple_of(r * R, R)
            cg = _conv(wg, bg, _taps(_ref_load(g_ref), r0, R))
            cv = _conv(wv, bv, _taps(_ref_load(v_ref), r0, R))
            o_ref[pl.ds(r0, R), :] = (cg * jax.nn.sigmoid(cg) * cv).astype(bf16)
            return c
        lax.fori_loop(0, S // R, chunk, 0)

    blk = lambda off: pl.BlockSpec((None, S, cwid), lambda b, j: (b, 0, off + j))
    wblk = lambda off: pl.BlockSpec((3, cwid), lambda b, j: (0, off + j))
    bblk = lambda off: pl.BlockSpec((1, cwid), lambda b, j: (0, off + j))
    return pl.pallas_call(
        body, name=name, grid=(B, nF),
        in_specs=[blk(0), blk(nF), wblk(0), wblk(nF), bblk(0), bblk(nF)],
        out_specs=pl.BlockSpec((None, S, cwid), lambda b, j: (b, 0, j)),
        out_shape=jax.ShapeDtypeStruct((B, S, F), bf16),
        compiler_params=_params(("parallel", "parallel")))(up3, up3, cw, cw, cb, cb)


def _ffn_act_bwd(name, up3, da3, cw, cb):
    B, S, F2 = up3.shape
    F = F2 // 2
    cwid = _pick(F, 256)
    nF = F // cwid
    R = min(256, S)

    def body(g_ref, v_ref, da_ref, wg_ref, wv_ref, bg_ref, bv_ref, wo_ref, dup_ref, dcw_ref, dcb_ref, dc_scr):
        is_val = pl.program_id(0) >= nF
        wg, wv, bg, bv, wo = wg_ref[...], wv_ref[...], bg_ref[...], bv_ref[...], wo_ref[...]

        def p1(r, acc):
            r0 = pl.multiple_of(r * R, R)
            tg = _taps(_ref_load(g_ref), r0, R)
            tv = _taps(_ref_load(v_ref), r0, R)
            cg = _conv(wg, bg, tg)
            cv = _conv(wv, bv, tv)
            da = da_ref[pl.ds(r0, R), :].astype(f32)
            sg = jax.nn.sigmoid(cg)
            dgate = da * cv * (sg * (1.0 + cg * (1.0 - sg)))
            dval = da * (cg * sg)
            dc = jnp.where(is_val, dval, dgate)
            dc_scr[pl.ds(r0, R), :] = dc
            own = [jnp.where(is_val, a, b) for a, b in zip(tv, tg)]
            return (acc[0] + _row_fold(dc * own[0]), acc[1] + _row_fold(dc * own[1]),
                    acc[2] + _row_fold(dc * own[2]), acc[3] + _row_fold(dc))
        z = jnp.zeros((SUBLANES, cwid), f32)
        acc = lax.fori_loop(0, S // R, p1, (z, z, z, z))

        @pl.when(pl.program_id(1) == 0)
        def _():
            dcw_ref[...] = jnp.zeros_like(dcw_ref)
            dcb_ref[...] = jnp.zeros_like(dcb_ref)
        for k in range(3):
            dcw_ref[k:k + 1, :] += jnp.sum(acc[k], axis=0, keepdims=True)
        dcb_ref[...] += jnp.sum(acc[3], axis=0, keepdims=True)

        def p2(r, c):
            r0 = pl.multiple_of(r * R, R)
            dup_ref[pl.ds(r0, R), :] = _rev_conv(dc_scr, wo, r0, R, S).astype(bf16)
            return c
        lax.fori_loop(0, S // R, p2, 0)

    blk = lambda off: pl.BlockSpec((None, S, cwid), lambda jj, b: (b, 0, off + jj % nF))
    wblk = lambda off: pl.BlockSpec((3, cwid), lambda jj, b: (0, off + jj % nF))
    bblk = lambda off: pl.BlockSpec((1, cwid), lambda jj, b: (0, off + jj % nF))
    return pl.pallas_call(
        body, name=name, grid=(2 * nF, B),
        in_specs=[blk(0), blk(nF), blk(0), wblk(0), wblk(nF), bblk(0), bblk(nF),
                  pl.BlockSpec((3, cwid), lambda jj, b: (0, jj))],
        out_specs=[pl.BlockSpec((None, S, cwid), lambda jj, b: (b, 0, jj)),
                   pl.BlockSpec((3, cwid), lambda jj, b: (0, jj)), pl.BlockSpec((1, cwid), lambda jj, b: (0, jj))],
        out_shape=[jax.ShapeDtypeStruct((B, S, F2), bf16), jax.ShapeDtypeStruct((3, F2), f32),
                   jax.ShapeDtypeStruct((1, F2), f32)],
        scratch_shapes=[pltpu.VMEM((S, cwid), f32)],
        compiler_params=_params(("parallel", "arbitrary")))(up3, up3, da3, cw, cw, cb, cb, cw)


def _od_act(p3, cw, cb):
    B, S, D3 = p3.shape
    D = D3 // 3
    cwid = _pick(D, 256)
    nD = D // cwid
    R = min(256, S)

    def body(bg_ref, cg_ref, hx_ref, w_ref, b_ref, o_ref):
        w, b = w_ref[...], b_ref[...]
        q = lambda s, n: cg_ref[pl.ds(s, n), :].astype(f32) * hx_ref[pl.ds(s, n), :].astype(f32)

        def chunk(r, c):
            r0 = pl.multiple_of(r * R, R)
            cq = _conv(w, b, _taps(q, r0, R))
            o_ref[pl.ds(r0, R), :] = (bg_ref[pl.ds(r0, R), :].astype(f32) * cq).astype(bf16)
            return c
        lax.fori_loop(0, S // R, chunk, 0)

    blk = lambda off: pl.BlockSpec((None, S, cwid), lambda b, j: (b, 0, off + j))
    return pl.pallas_call(
        body, name="od_act", grid=(B, nD),
        in_specs=[blk(0), blk(nD), blk(2 * nD), pl.BlockSpec((3, cwid), lambda b, j: (0, j)),
                  pl.BlockSpec((1, cwid), lambda b, j: (0, j))],
        out_specs=pl.BlockSpec((None, S, cwid), lambda b, j: (b, 0, j)),
        out_shape=jax.ShapeDtypeStruct((B, S, D), bf16),
        compiler_params=_params(("parallel", "parallel")))(p3, p3, p3, cw, cb)


def _od_act_bwd(p3, dsc3, cw, cb):
    B, S, D3 = p3.shape
    D = D3 // 3
    cwid = _pick(D, 256)
    nD = D // cwid
    R = min(256, S)

    def body(bg_ref, cg_ref, hx_ref, d_ref, w_ref, b_ref, dp_ref, dcw_ref, dcb_ref, dc_scr):
        part = pl.program_id(0) // nD
        w, b = w_ref[...], b_ref[...]
        q = lambda s, n: cg_ref[pl.ds(s, n), :].astype(f32) * hx_ref[pl.ds(s, n), :].astype(f32)

        def p1(r, acc):
            r0 = pl.multiple_of(r * R, R)
            tq = _taps(q, r0, R)
            cq = _conv(w, b, tq)
            d = d_ref[pl.ds(r0, R), :].astype(f32)
            dp_ref[pl.ds(r0, R), :] = (d * cq).astype(bf16)
            dcq = d * bg_ref[pl.ds(r0, R), :].astype(f32)
            dc_scr[pl.ds(r0, R), :] = dcq
            return (acc[0] + _row_fold(dcq * tq[0]), acc[1] + _row_fold(dcq * tq[1]),
                    acc[2] + _row_fold(dcq * tq[2]), acc[3] + _row_fold(dcq))
        z = jnp.zeros((SUBLANES, cwid), f32)
        acc = lax.fori_loop(0, S // R, p1, (z, z, z, z))

        @pl.when(pl.program_id(1) == 0)
        def _():
            dcw_ref[...] = jnp.zeros_like(dcw_ref)
            dcb_ref[...] = jnp.zeros_like(dcb_ref)
        for k in range(3):
            dcw_ref[k:k + 1, :] += jnp.sum(acc[k], axis=0, keepdims=True)
        dcb_ref[...] += jnp.sum(acc[3], axis=0, keepdims=True)

        @pl.when(part > 0)
        def _():
            def p2(r, c):
                r0 = pl.multiple_of(r * R, R)
                dq = _rev_conv(dc_scr, w, r0, R, S)
                dcg = dq * hx_ref[pl.ds(r0, R), :].astype(f32)
                dhx = dq * cg_ref[pl.ds(r0, R), :].astype(f32)
                dp_ref[pl.ds(r0, R), :] = jnp.where(part == 1, dcg, dhx).astype(bf16)
                return c
            lax.fori_loop(0, S // R, p2, 0)

    blk = lambda off: pl.BlockSpec((None, S, cwid), lambda jj, b: (b, 0, off + jj % nD))
    return pl.pallas_call(
        body, name="od_act_bwd", grid=(3 * nD, B),
        in_specs=[blk(0), blk(nD), blk(2 * nD), blk(0), pl.BlockSpec((3, cwid), lambda jj, b: (0, jj % nD)),
                  pl.BlockSpec((1, cwid), lambda jj, b: (0, jj % nD))],
        out_specs=[pl.BlockSpec((None, S, cwid), lambda jj, b: (b, 0, jj)),
                   pl.BlockSpec((None, 3, cwid), lambda jj, b: (jj // nD, 0, jj % nD)),
                   pl.BlockSpec((None, 1, cwid), lambda jj, b: (jj // nD, 0, jj % nD))],
        out_shape=[jax.ShapeDtypeStruct((B, S, D3), bf16), jax.ShapeDtypeStruct((3, 3, D), f32),
                   jax.ShapeDtypeStruct((3, 1, D), f32)],
        scratch_shapes=[pltpu.VMEM((S, cwid), f32)],
        compiler_params=_params(("parallel", "arbitrary")))(p3, p3, p3, dsc3, cw, cb)


def _gmlp_parts(p, gv, SW, GW):
    uv = p[:, SW:].astype(f32)
    ge = _gelu(uv)
    u, v = ge[:, :GW], ge[:, GW:]
    vh, r = _rms_stats(v)
    return uv, u, vh, r, vh * gv


def _tril():
    rows = lax.broadcasted_iota(jnp.int32, (CHUNK, CHUNK), 0)
    cols = lax.broadcasted_iota(jnp.int32, (CHUNK, CHUNK), 1)
    return rows >= cols


def _gmlp(p0, a_out, ws, bst, gv, SW):
    T, PW = p0.shape
    GW = (PW - SW) // 2
    H = GW // GMLP_HEAD
    D = SW + GW

    def body(p_ref, a_ref, ws_ref, b_ref, gv_ref, o_ref):
        _, u, _, _, vn = _gmlp_parts(p_ref[...], gv_ref[...], SW, GW)
        tri = _tril()
        o_ref[:, :SW] = a_ref[...]
        for hh in range(H):
            sl = slice(hh * GMLP_HEAD, (hh + 1) * GMLP_HEAD)
            wm = jnp.where(tri, ws_ref[hh], 0.0).astype(bf16)
            gate = _dot(wm, vn[:, sl].astype(bf16)) + b_ref[:, hh:hh + 1]
            o_ref[:, SW + hh * GMLP_HEAD:SW + (hh + 1) * GMLP_HEAD] = (u[:, sl] * gate).astype(bf16)

    return pl.pallas_call(
        body, name="gmlp", grid=(T // CHUNK,),
        in_specs=[pl.BlockSpec((CHUNK, PW), lambda i: (i, 0)), pl.BlockSpec((CHUNK, SW), lambda i: (i, 0)),
                  pl.BlockSpec((H, CHUNK, CHUNK), lambda i: (0, 0, 0)), pl.BlockSpec((CHUNK, H), lambda i: (0, 0)),
                  pl.BlockSpec((1, GW), lambda i: (0, 0))],
        out_specs=pl.BlockSpec((CHUNK, D), lambda i: (i, 0)),
        out_shape=jax.ShapeDtypeStruct((T, D), bf16),
        compiler_params=_params(("parallel",)))(p0, a_out, ws, bst, gv)


def _gmlp_bwd(p0, dmix, ws, bst, gv, SW):
    T, PW = p0.shape
    GW = (PW - SW) // 2
    H = GW // GMLP_HEAD
    D = SW + GW

    def body(p_ref, d_ref, ws_ref, b_ref, gv_ref, duv_ref, dws_ref, dbs_ref, dgv_ref):
        gv_ = gv_ref[...]
        uv, u, vh, r, vn = _gmlp_parts(p_ref[...], gv_, SW, GW)
        dout = d_ref[...][:, SW:].astype(f32)
        tri = _tril()

        @pl.when(pl.program_id(0) == 0)
        def _():
            dws_ref[...] = jnp.zeros_like(dws_ref)
            dbs_ref[...] = jnp.zeros_like(dbs_ref)
            dgv_ref[...] = jnp.zeros_like(dgv_ref)
        du, dvn = [], []
        for hh in range(H):
            sl = slice(hh * GMLP_HEAD, (hh + 1) * GMLP_HEAD)
            wm = jnp.where(tri, ws_ref[hh], 0.0).astype(bf16)
            vnh = vn[:, sl].astype(bf16)
            gate = _dot(wm, vnh) + b_ref[:, hh:hh + 1]
            dgate = dout[:, sl] * u[:, sl]
            du.append(dout[:, sl] * gate)
            dgb = dgate.astype(bf16)
            dws_ref[hh] += jnp.where(tri, _dg(dgb, vnh, NT), 0.0)
            dbs_ref[hh] += jnp.broadcast_to(jnp.sum(dgate, axis=1, keepdims=True), (CHUNK, CHUNK))
            dvn.append(_dg(wm, dgb, TN))
        dvn = jnp.concatenate(dvn, axis=1)
        dv, dgv = _rms_bwd(dvn, vh, r, gv_)
        dgv_ref[...] += dgv
        dge = jnp.concatenate(du + [dv], axis=1)
        duv_ref[...] = (dge * _gelu_grad(uv)).astype(bf16)

    return pl.pallas_call(
        body, name="gmlp_bwd", grid=(T // CHUNK,),
        in_specs=[pl.BlockSpec((CHUNK, PW), lambda i: (i, 0)), pl.BlockSpec((CHUNK, D), lambda i: (i, 0)),
                  pl.BlockSpec((H, CHUNK, CHUNK), lambda i: (0, 0, 0)), pl.BlockSpec((CHUNK, H), lambda i: (0, 0)),
                  pl.BlockSpec((1, GW), lambda i: (0, 0))],
        out_specs=[pl.BlockSpec((CHUNK, 2 * GW), lambda i: (i, 0)), pl.BlockSpec((H, CHUNK, CHUNK), lambda i: (0, 0, 0)),
                   pl.BlockSpec((H, CHUNK, CHUNK), lambda i: (0, 0, 0)), pl.BlockSpec((1, GW), lambda i: (0, 0))],
        out_shape=[jax.ShapeDtypeStruct((T, 2 * GW), bf16), jax.ShapeDtypeStruct((H, CHUNK, CHUNK), f32),
                   jax.ShapeDtypeStruct((H, CHUNK, CHUNK), f32), jax.ShapeDtypeStruct((1, GW), f32)],
        compiler_params=_params(("arbitrary",)))(p0, dmix, ws, bst, gv)


def _s5_disc(lr, li, ldt):
    lr = jnp.minimum(lr, LAMBDA_RE_MAX)
    dt = jnp.exp(ldt)
    mag = jnp.exp(lr * dt)
    ar = mag * jnp.cos(li * dt)
    ai = mag * jnp.sin(li * dt)
    den = lr * lr + li * li
    nr = ar - 1.0
    zr = (nr * lr + ai * li) / den
    zi = (ai * lr - nr * li) / den
    return ar, ai, zr, zi


def _s5_prep(lr, li, ldt):
    G, P = lr.shape

    def body(lr_ref, li_ref, ldt_ref, ar_ref, ai_ref, zr_ref, zi_ref):
        ar, ai, zr, zi = _s5_disc(lr_ref[...], li_ref[...], ldt_ref[...])
        ar_ref[...] = ar
        ai_ref[...] = ai
        zr_ref[...] = zr
        zi_ref[...] = zi

    s = jax.ShapeDtypeStruct((G, P), f32)
    return pl.pallas_call(body, name="s5_prep", out_shape=[s, s, s, s])(lr, li, ldt)


def _s5_prep_bwd(lr, li, ldt, dar, dai, dzr, dzi):
    G, P = lr.shape

    def body(lr_ref, li_ref, ldt_ref, dar_ref, dai_ref, dzr_ref, dzi_ref, o1, o2, o3):
        _, vjp = jax.vjp(_s5_disc, lr_ref[...], li_ref[...], ldt_ref[...])
        cts = tuple(jnp.sum(r[...], axis=0) for r in (dar_ref, dai_ref, dzr_ref, dzi_ref))
        a, b, c = vjp(cts)
        o1[...] = a
        o2[...] = b
        o3[...] = c

    s = jax.ShapeDtypeStruct((G, P), f32)
    return pl.pallas_call(body, name="s5_prep_bwd", out_shape=[s, s, jax.ShapeDtypeStruct((G, 1), f32)])(
        lr, li, ldt, dar, dai, dzr, dzi)


def _s5_bbd(zr, zi, bre, bim):
    SW, NS = bre.shape

    def body(zr_ref, zi_ref, br_ref, bi_ref, o_ref):
        zr_, zi_, br, bi = zr_ref[...], zi_ref[...], br_ref[...], bi_ref[...]
        o_ref[:, :NS] = (zr_ * br - zi_ * bi).astype(bf16)
        o_ref[:, NS:] = (zr_ * bi + zi_ * br).astype(bf16)

    return pl.pallas_call(body, name="s5_bbd", out_shape=jax.ShapeDtypeStruct((SW, 2 * NS), bf16))(zr, zi, bre, bim)


def _s5_bbd_bwd(dbbd, zr, zi, bre, bim):
    SW, NS = bre.shape

    def body(d_ref, zr_ref, zi_ref, br_ref, bi_ref, dbr_ref, dbi_ref, dzr_ref, dzi_ref):
        zr_, zi_, br, bi = zr_ref[...], zi_ref[...], br_ref[...], bi_ref[...]
        dr, di = d_ref[:, :NS], d_ref[:, NS:]
        dbr_ref[...] = zr_ * dr + zi_ * di
        dbi_ref[...] = zr_ * di - zi_ * dr
        dzr_ref[...] = jnp.sum(dr * br + di * bi, axis=0, keepdims=True)
        dzi_ref[...] = jnp.sum(di * br - dr * bi, axis=0, keepdims=True)

    m = jax.ShapeDtypeStruct((SW, NS), f32)
    v = jax.ShapeDtypeStruct((1, NS), f32)
    return pl.pallas_call(body, name="s5_bbd_bwd", out_shape=[m, m, v, v])(dbbd, zr, zi, bre, bim)


def _slab_cat(ref, NB):
    return jnp.concatenate([ref[j] for j in range(NB)], axis=1)


def _s5_in(p3, bbd, SW, tm):
    B, S, PW = p3.shape
    NS = bbd.shape[1] // 2
    NB = NS // LANES

    def body(u_ref, b_ref, xr_ref, xi_ref):
        x = _dot(u_ref[...], b_ref[...])
        for j in range(NB):
            xr_ref[j] = x[:, j * LANES:(j + 1) * LANES]
            xi_ref[j] = x[:, NS + j * LANES:NS + (j + 1) * LANES]

    slab = jax.ShapeDtypeStruct((B, NB, S, LANES), f32)
    sspec = pl.BlockSpec((None, NB, tm, LANES), lambda b, i: (b, 0, i, 0))
    return pl.pallas_call(
        body, name="s5_in", grid=(B, S // tm),
        in_specs=[pl.BlockSpec((None, tm, SW), lambda b, i: (b, i, 0)), pl.BlockSpec((SW, 2 * NS), lambda b, i: (0, 0))],
        out_specs=[sspec, sspec], out_shape=[slab, slab],
        compiler_params=_params(("parallel", "parallel")))(p3, bbd)


def _s5_scan(name, xr, xi, ar, ai, reverse, hr=None, hi=None):
    B, NB, S, _ = xr.shape
    L = S // SUBLANES
    nb = 2 if (hr is None and NB % 2 == 0) else 1
    with_da = hr is not None

    def body(*refs):
        if with_da:
            xr_ref, xi_ref, ar_ref, ai_ref, hr_ref, hi_ref, or_ref, oi_ref, dar_ref, dai_ref, pr_scr, pi_scr = refs
        else:
            xr_ref, xi_ref, ar_ref, ai_ref, or_ref, oi_ref, pr_scr, pi_scr = refs
        sign = -1.0 if reverse else 1.0
        a_r = [jnp.broadcast_to(ar_ref[j], (SUBLANES, LANES)) for j in range(nb)]
        a_i = [jnp.broadcast_to(ai_ref[j], (SUBLANES, LANES)) * sign for j in range(nb)]

        def step(t, carry):
            row = (L - 1 - t) if reverse else t
            rows = pl.ds(row, SUBLANES, stride=L)
            out = []
            for j in range(nb):
                sr, si, pr, pi = carry[j]
                nr = a_r[j] * sr - a_i[j] * si + xr_ref.at[j][rows, :]
                ni = a_r[j] * si + a_i[j] * sr + xi_ref.at[j][rows, :]
                or_ref.at[j][rows, :] = nr
                oi_ref.at[j][rows, :] = ni
                npr = a_r[j] * pr - a_i[j] * pi
                npi = a_r[j] * pi + a_i[j] * pr
                pr_scr[j, pl.ds(row, 1), :] = npr[0:1]
                pi_scr[j, pl.ds(row, 1), :] = npi[0:1]
                out.append((nr, ni, npr, npi))
            return tuple(out)
        z = jnp.zeros((SUBLANES, LANES), f32)
        one = jnp.ones((SUBLANES, LANES), f32)
        fin = lax.fori_loop(0, L, step, tuple((z, z, one, z) for _ in range(nb)))

        for j in range(nb):
            sr, si, plr, pli = fin[j]
            plr, pli = plr[0:1], pli[0:1]
            cr = jnp.zeros((1, LANES), f32)
            ci = jnp.zeros((1, LANES), f32)
            order = range(SUBLANES - 2, -1, -1) if reverse else range(1, SUBLANES)
            for c in order:
                src = c + 1 if reverse else c - 1
                cr, ci = (sr[src:src + 1] + plr * cr - pli * ci, si[src:src + 1] + plr * ci + pli * cr)
                rows = slice(c * L, (c + 1) * L)
                tr, ti = pr_scr[j], pi_scr[j]
                or_ref[j, rows, :] += tr * cr - ti * ci
                oi_ref[j, rows, :] += tr * ci + ti * cr
            if with_da:
                first = lax.broadcasted_iota(jnp.int32, (L, LANES), 0) == 0
                dar = jnp.zeros((1, LANES), f32)
                dai = jnp.zeros((1, LANES), f32)
                for c in range(SUBLANES):
                    rows = slice(c * L, (c + 1) * L)
                    if c == 0:
                        lr_, li_ = jnp.zeros((1, LANES), f32), jnp.zeros((1, LANES), f32)
                    else:
                        lr_, li_ = hr_ref[j, c * L - 1:c * L, :], hi_ref[j, c * L - 1:c * L, :]
                    hpr = jnp.where(first, lr_, pltpu.roll(hr_ref[j, rows, :], 1, 0))
                    hpi = jnp.where(first, li_, pltpu.roll(hi_ref[j, rows, :], 1, 0))
                    gr, gi = or_ref[j, rows, :], oi_ref[j, rows, :]
                    dar += jnp.sum(hpr * gr + hpi * gi, axis=0, keepdims=True)
                    dai += jnp.sum(hpr * gi - hpi * gr, axis=0, keepdims=True)
                dar_ref[j] = dar
                dai_ref[j] = dai

    slab = jax.ShapeDtypeStruct((B, NB, S, LANES), f32)
    sspec = pl.BlockSpec((None, nb, S, LANES), lambda b, j: (b, j, 0, 0))
    aspec = pl.BlockSpec((nb, 1, LANES), lambda b, j: (j, 0, 0))
    in_specs = [sspec, sspec, aspec, aspec]
    out_specs = [sspec, sspec]
    out_shape = [slab, slab]
    args = [xr, xi, ar, ai]
    if with_da:
        in_specs += [sspec, sspec]
        args += [hr, hi]
        dspec = pl.BlockSpec((None, nb, 1, LANES), lambda b, j: (b, j, 0, 0))
        out_specs += [dspec, dspec]
        out_shape += [jax.ShapeDtypeStruct((B, NB, 1, LANES), f32)] * 2
    return pl.pallas_call(
        body, name=name, grid=(B, NB // nb), in_specs=in_specs, out_specs=out_specs, out_shape=out_shape,
        scratch_shapes=[pltpu.VMEM((nb, L, LANES), f32), pltpu.VMEM((nb, L, LANES), f32)],
        compiler_params=_params(("parallel", "parallel")))(*args)


def _s5_out_parts(hr_ref, hi_ref, u_ref, cr_ref, ci_ref, d_ref, wg_ref, bg_ref, NB):
    hcr = _slab_cat(hr_ref, NB).astype(bf16)
    hci = _slab_cat(hi_ref, NB).astype(bf16)
    u = u_ref[...].astype(f32)
    y2 = _dot(hcr, cr_ref[...]) - _dot(hci, ci_ref[...]) + d_ref[...] * u
    yg = _gelu(y2)
    s = jax.nn.sigmoid(_dot(yg.astype(bf16), wg_ref[...]) + bg_ref[...])
    return hcr, hci, u, y2, yg, s


def _s5_out_specs(B, S, NB, NS, SW, tm):
    sspec = pl.BlockSpec((None, NB, tm, LANES), lambda b, i: (b, 0, i, 0))
    full = lambda r, c: pl.BlockSpec((r, c), lambda b, i: (0, 0))
    return sspec, [sspec, sspec, pl.BlockSpec((None, tm, SW), lambda b, i: (b, i, 0)), full(NS, SW), full(NS, SW),
                   full(1, SW), full(SW, SW), full(1, SW)]


def _s5_out(hr, hi, p3, cbr, cbi, dsk, wglu, bglu, tm):
    B, NB, S, _ = hr.shape
    NS, SW = cbr.shape

    def body(hr_ref, hi_ref, u_ref, cr_ref, ci_ref, d_ref, wg_ref, bg_ref, o_ref):
        _, _, _, _, yg, s = _s5_out_parts(hr_ref, hi_ref, u_ref, cr_ref, ci_ref, d_ref, wg_ref, bg_ref, NB)
        o_ref[...] = (yg * s).astype(bf16)

    _, in_specs = _s5_out_specs(B, S, NB, NS, SW, tm)
    return pl.pallas_call(
        body, name="s5_out", grid=(B, S // tm), in_specs=in_specs,
        out_specs=pl.BlockSpec((None, tm, SW), lambda b, i: (b, i, 0)),
        out_shape=jax.ShapeDtypeStruct((B, S, SW), bf16),
        compiler_params=_params(("parallel", "parallel")))(hr, hi, p3, cbr, cbi, dsk, wglu, bglu)


def _s5_out_bwd(hr, hi, p3, dmix3, cbr, cbi, dsk, wglu, bglu, tm):
    B, NB, S, _ = hr.shape
    NS, SW = cbr.shape

    def body(hr_ref, hi_ref, u_ref, cr_ref, ci_ref, d_ref, wg_ref, bg_ref, da_ref,
             dhr_ref, dhi_ref, du_ref, dcr_ref, dci_ref, dd_ref, dwg_ref, dbg_ref):
        hcr, hci, u, y2, yg, s = _s5_out_parts(hr_ref, hi_ref, u_ref, cr_ref, ci_ref, d_ref, wg_ref, bg_ref, NB)
        da = da_ref[...].astype(f32)
        dz = da * yg * s * (1.0 - s)
        dzb = dz.astype(bf16)
        dyg = da * s + _dg(dzb, wg_ref[...], NT)
        dy2 = dyg * _gelu_grad(y2)
        dyb = dy2.astype(bf16)

        @pl.when((pl.program_id(0) == 0) & (pl.program_id(1) == 0))
        def _():
            for r in (dcr_ref, dci_ref, dd_ref, dwg_ref, dbg_ref):
                r[...] = jnp.zeros_like(r)
        dwg_ref[...] += _dg(yg.astype(bf16), dzb, TN)
        dbg_ref[...] += jnp.sum(dz, axis=0, keepdims=True)
        dd_ref[...] += jnp.sum(dy2 * u, axis=0, keepdims=True)
        dcr_ref[...] += _dg(hcr, dyb, TN)
        dci_ref[...] -= _dg(hci, dyb, TN)
        du_ref[...] = dy2 * d_ref[...]
        dhr = _dg(dyb, cr_ref[...], NT)
        dhi = _dg(dyb, ci_ref[...], NT)
        for j in range(NB):
            dhr_ref[j] = dhr[:, j * LANES:(j + 1) * LANES]
            dhi_ref[j] = -dhi[:, j * LANES:(j + 1) * LANES]

    sspec, in_specs = _s5_out_specs(B, S, NB, NS, SW, tm)
    in_specs = in_specs + [pl.BlockSpec((None, tm, SW), lambda b, i: (b, i, 0))]
    full = lambda r, c: pl.BlockSpec((r, c), lambda b, i: (0, 0))
    slab = jax.ShapeDtypeStruct((B, NB, S, LANES), f32)
    mat = lambda r, c: jax.ShapeDtypeStruct((r, c), f32)
    return pl.pallas_call(
        body, name="s5_out_bwd", grid=(B, S // tm), in_specs=in_specs,
        out_specs=[sspec, sspec, pl.BlockSpec((None, tm, SW), lambda b, i: (b, i, 0)), full(NS, SW), full(NS, SW),
                   full(1, SW), full(SW, SW), full(1, SW)],
        out_shape=[slab, slab, jax.ShapeDtypeStruct((B, S, SW), f32), mat(NS, SW), mat(NS, SW), mat(1, SW),
                   mat(SW, SW), mat(1, SW)],
        compiler_params=_params(("arbitrary", "arbitrary")))(hr, hi, p3, cbr, cbi, dsk, wglu, bglu, dmix3)


def _s5_in_bwd(gr, gi, p3, bbd, du_skip, duv3, tm):
    B, NB, S, _ = gr.shape
    SW, NS2 = bbd.shape
    PW = SW + duv3.shape[2]

    def body(gr_ref, gi_ref, u_ref, b_ref, ds_ref, duv_ref, dp_ref, db_ref):
        g = jnp.concatenate([_slab_cat(gr_ref, NB), _slab_cat(gi_ref, NB)], axis=1).astype(bf16)
        du = _dg(g, b_ref[...], NT) + ds_ref[...]
        dp_ref[:, :SW] = du.astype(bf16)
        dp_ref[:, SW:] = duv_ref[...]

        @pl.when((pl.program_id(0) == 0) & (pl.program_id(1) == 0))
        def _():
            db_ref[...] = jnp.zeros_like(db_ref)
        db_ref[...] += _dg(u_ref[...], g, TN)

    sspec = pl.BlockSpec((None, NB, tm, LANES), lambda b, i: (b, 0, i, 0))
    row = lambda c: pl.BlockSpec((None, tm, c), lambda b, i: (b, i, 0))
    return pl.pallas_call(
        body, name="s5_in_bwd", grid=(B, S // tm),
        in_specs=[sspec, sspec, row(SW), pl.BlockSpec((SW, NS2), lambda b, i: (0, 0)), row(SW), row(PW - SW)],
        out_specs=[row(PW), pl.BlockSpec((SW, NS2), lambda b, i: (0, 0))],
        out_shape=[jax.ShapeDtypeStruct((B, S, PW), bf16), jax.ShapeDtypeStruct((SW, NS2), f32)],
        compiler_params=_params(("arbitrary", "arbitrary")))(gr, gi, p3, bbd, du_skip, duv3)


PACK_W = 1024


def _pair_sum(g2, recv, c_idx, tr):
    _, NCH, Rh, W = g2.shape

    def body(c_ref, a_ref, b_ref, o_ref):
        o_ref[...] = a_ref[...] + b_ref[...]

    return pl.pallas_call(
        body, name="pair_sum",
        grid_spec=pltpu.PrefetchScalarGridSpec(
            num_scalar_prefetch=1, grid=(NCH, Rh // tr),
            in_specs=[pl.BlockSpec((None, None, tr, W), lambda j, i, c: (c[0], j, i, 0)),
                      pl.BlockSpec((None, tr, W), lambda j, i, c: (j, i, 0))],
            out_specs=pl.BlockSpec((None, tr, W), lambda j, i, c: (j, i, 0))),
        out_shape=jax.ShapeDtypeStruct((NCH, Rh, W), f32),
        compiler_params=_params(("parallel", "parallel")))(c_idx, g2, recv)


def _chip_sum(r3, tr):
    NCH, Rh, W = r3.shape

    def body(a_ref, o_ref):
        o_ref[...] = ((a_ref[0] + a_ref[1]) + a_ref[2]) + a_ref[3]

    return pl.pallas_call(
        body, name="chip_sum", grid=(Rh // tr,),
        in_specs=[pl.BlockSpec((NCH, tr, W), lambda i: (0, i, 0))],
        out_specs=pl.BlockSpec((tr, W), lambda i: (i, 0)),
        out_shape=jax.ShapeDtypeStruct((Rh, W), f32),
        compiler_params=_params(("parallel",)))(r3)


def _adamw(g, w, m, v, tr):
    R, W = g.shape

    def body(g_ref, w_ref, m_ref, v_ref, d_ref, nm_ref, nv_ref):
        gg = g_ref[...]
        nm = ADAM_B1 * m_ref[...] + (1.0 - ADAM_B1) * gg
        nv = ADAM_B2 * v_ref[...] + (1.0 - ADAM_B2) * jnp.square(gg)
        m_hat = nm / (1.0 - ADAM_B1 ** ADAM_STEP)
        v_hat = nv / (1.0 - ADAM_B2 ** ADAM_STEP)
        d_ref[...] = -ADAM_LR * (m_hat / (jnp.sqrt(v_hat) + ADAM_EPS) + ADAM_WD * w_ref[...])
        nm_ref[...] = nm
        nv_ref[...] = nv

    spec = pl.BlockSpec((tr, W), lambda i: (i, 0))
    s = jax.ShapeDtypeStruct((R, W), f32)
    return pl.pallas_call(
        body, name="adamw", grid=(R // tr,), in_specs=[spec] * 4, out_specs=[spec] * 3, out_shape=[s, s, s],
        compiler_params=_params(("parallel",)))(g, w, m, v)


ANY = pl.BlockSpec(memory_space=pl.ANY)


def _place():
    x, y, c = lax.axis_index("x"), lax.axis_index("y"), lax.axis_index("c")
    return x, y, c, [(1 - x, y), (x, 1 - y), (1 - x, 1 - y)]


def _comm_gather(wpack):
    _, R, W = wpack.shape

    def body(w_ref, o_ref, send_sems, recv_sems, local_sem):
        x, y, c, chips = _place()
        sibling = (x, y, 1 - c)

        def slot(px, py, pc):
            return o_ref.at[2 * px + py, pc]

        def copy(k, block, to, src=None):
            return pltpu.make_async_remote_copy(
                src_ref=slot(*block) if src is None else src, dst_ref=slot(*block),
                send_sem=send_sems.at[k], recv_sem=recv_sems.at[k], device_id=to, device_id_type=MESH)

        mine = pltpu.make_async_copy(w_ref, o_ref.at[2 * x + y], local_sem)
        mine.start()
        first = [copy(j, (x, y, c), (*chip, c), src=w_ref.at[c]) for j, chip in enumerate(chips)]
        for cp in first:
            cp.start()
        passed = [copy(3 + j, (*chip, c), sibling) for j, chip in enumerate(chips)]
        for j, chip in enumerate(chips):
            copy(j, (*chip, c), (x, y, c)).wait_recv()
            passed[j].start()
        for j, chip in enumerate(chips):
            copy(3 + j, (*chip, 1 - c), (x, y, c)).wait_recv()
        for cp in first + passed:
            cp.wait_send()
        mine.wait()

    return pl.pallas_call(
        body, name="comm_gather", in_specs=[ANY], out_specs=ANY,
        out_shape=jax.ShapeDtypeStruct((N_CHIPS, 2, R, W), wpack.dtype),
        scratch_shapes=[pltpu.SemaphoreType.DMA((6,)), pltpu.SemaphoreType.DMA((6,)), pltpu.SemaphoreType.DMA])(wpack)


def _comm_pair_swap(g2):
    _, NCH, Rh, W = g2.shape

    def body(g_ref, o_ref, send_sem, recv_sem):
        x, y, c, _ = _place()
        cp = pltpu.make_async_remote_copy(src_ref=g_ref.at[1 - c], dst_ref=o_ref, send_sem=send_sem, recv_sem=recv_sem,
                                          device_id=(x, y, 1 - c), device_id_type=MESH)
        cp.start()
        cp.wait()

    return pl.pallas_call(
        body, name="comm_pair_swap", in_specs=[ANY], out_specs=ANY,
        out_shape=jax.ShapeDtypeStruct((NCH, Rh, W), g2.dtype),
        scratch_shapes=[pltpu.SemaphoreType.DMA, pltpu.SemaphoreType.DMA])(g2)


def _comm_chip_exchange(h):
    NCH, Rh, W = h.shape

    def body(h_ref, o_ref, send_sems, recv_sems, local_sem):
        x, y, c, chips = _place()
        k = 2 * x + y
        mine = pltpu.make_async_copy(h_ref.at[k], o_ref.at[k], local_sem)
        mine.start()
        sends = [pltpu.make_async_remote_copy(
            src_ref=h_ref.at[2 * cx + cy], dst_ref=o_ref.at[k], send_sem=send_sems.at[j], recv_sem=recv_sems.at[j],
            device_id=(cx, cy, c), device_id_type=MESH) for j, (cx, cy) in enumerate(chips)]
        for cp in sends:
            cp.start()
        for j, (cx, cy) in enumerate(chips):
            pltpu.make_async_remote_copy(
                src_ref=h_ref.at[k], dst_ref=o_ref.at[2 * cx + cy], send_sem=send_sems.at[j], recv_sem=recv_sems.at[j],
                device_id=(cx, cy, c), device_id_type=MESH).wait_recv()
        for cp in sends:
            cp.wait_send()
        mine.wait()

    return pl.pallas_call(
        body, name="comm_chip_exchange", in_specs=[ANY], out_specs=ANY,
        out_shape=jax.ShapeDtypeStruct((NCH, Rh, W), h.dtype),
        scratch_shapes=[pltpu.SemaphoreType.DMA((3,)), pltpu.SemaphoreType.DMA((3,)), pltpu.SemaphoreType.DMA])(h)


def _comm_pair_share(gs):
    Rh, W = gs.shape

    def body(g_ref, o_ref, send_sem, recv_sem, local_sem):
        x, y, c, _ = _place()
        mine = pltpu.make_async_copy(g_ref, o_ref.at[c], local_sem)
        mine.start()
        cp = pltpu.make_async_remote_copy(src_ref=g_ref, dst_ref=o_ref.at[c], send_sem=send_sem, recv_sem=recv_sem,
                                          device_id=(x, y, 1 - c), device_id_type=MESH)
        cp.start()
        pltpu.make_async_remote_copy(src_ref=g_ref, dst_ref=o_ref.at[1 - c], send_sem=send_sem, recv_sem=recv_sem,
                                     device_id=(x, y, 1 - c), device_id_type=MESH).wait_recv()
        cp.wait_send()
        mine.wait()

    return pl.pallas_call(
        body, name="comm_pair_share", in_specs=[ANY], out_specs=ANY,
        out_shape=jax.ShapeDtypeStruct((2, Rh, W), gs.dtype),
        scratch_shapes=[pltpu.SemaphoreType.DMA, pltpu.SemaphoreType.DMA, pltpu.SemaphoreType.DMA])(gs)


def _pad_rows(flat, unit):
    n = flat.shape[-1]
    pad = (-n) % unit
    if pad:
        flat = jnp.pad(flat, [(0, 0)] * (flat.ndim - 1) + [(0, pad)])
    return flat


def _split_chips(full, axis):
    sh = full.shape
    t = full.reshape(sh[:axis] + (N_CHIPS, sh[axis] // N_CHIPS) + sh[axis + 1:])
    return jnp.moveaxis(t, axis, 0).reshape(N_CHIPS, -1)


def _join_chips(stack, shard_shape, axis):
    t = jnp.moveaxis(stack.reshape((N_CHIPS,) + tuple(shard_shape)), 0, axis)
    sh = t.shape
    return t.reshape(sh[:axis] + (sh[axis] * sh[axis + 1],) + sh[axis + 2:])


def _block_diag(blocks):
    G, r, c = blocks.shape
    eye = jnp.eye(G, dtype=blocks.dtype)
    return (blocks[:, :, None, :] * eye[:, None, :, None]).reshape(G * r, G * c)


def _diag_blocks(m, G):
    r, c = m.shape[0] // G, m.shape[1] // G
    idx = jnp.arange(G)
    return m.reshape(G, r, G, c)[idx, :, idx, :]


def _gather_weights(w):
    parts = [w[n].astype(bf16).reshape(-1) for n in GATHER_BF16]
    parts += [lax.bitcast_convert_type(w[n], bf16).reshape(-1) for n in GATHER_F32]
    sizes = [p.shape[0] for p in parts]
    flat = _pad_rows(jnp.concatenate(parts), 2 * HALO * PACK_W)
    allp = _comm_gather(flat.reshape(2, -1, PACK_W)).reshape(N_CHIPS, -1)
    out, off = {}, 0
    for n, size in zip(GATHER_BF16 + GATHER_F32, sizes):
        seg = allp[:, off:off + size]
        off += size
        if n in GATHER_F32:
            seg = lax.bitcast_convert_type(seg.reshape(N_CHIPS, -1, 2), f32)
        out[n] = _join_chips(seg, w[n].shape, SHARD_AXIS[n])
    return out


def kernel(x, mix_norm_g, ffn_norm_g, final_norm_g, ev_w_in, ev_w_out, s5_lam_re, s5_lam_im, s5_log_dt, s5_b_re, s5_b_im, s5_c_re, s5_c_im, s5_d, s5_w_glu, s5_b_glu, gm_w_s, gm_b_s, gm_v_g, od_w_in, od_conv_w, od_conv_b, od_w_out, ffn_w_up, ffn_conv_w, ffn_conv_b, ffn_w_down, loss_target, m_mix_norm_g, m_ffn_norm_g, m_final_norm_g, m_ev_w_in, m_ev_w_out, m_s5_lam_re, m_s5_lam_im, m_s5_log_dt, m_s5_b_re, m_s5_b_im, m_s5_c_re, m_s5_c_im, m_s5_d, m_s5_w_glu, m_s5_b_glu, m_gm_w_s, m_gm_b_s, m_gm_v_g, m_od_w_in, m_od_conv_w, m_od_conv_b, m_od_w_out, m_ffn_w_up, m_ffn_conv_w, m_ffn_conv_b, m_ffn_w_down, v_mix_norm_g, v_ffn_norm_g, v_final_norm_g, v_ev_w_in, v_ev_w_out, v_s5_lam_re, v_s5_lam_im, v_s5_log_dt, v_s5_b_re, v_s5_b_im, v_s5_c_re, v_s5_c_im, v_s5_d, v_s5_w_glu, v_s5_b_glu, v_gm_w_s, v_gm_b_s, v_gm_v_g, v_od_w_in, v_od_conv_w, v_od_conv_b, v_od_w_out, v_ffn_w_up, v_ffn_conv_w, v_ffn_conv_b, v_ffn_w_down):
    loc = dict(locals())
    w = {n: loc[n] for n in WEIGHTS}
    mom = {n: loc["m_" + n] for n in WEIGHTS}
    var = {n: loc["v_" + n] for n in WEIGHTS}

    B, S, D = x.shape
    T = B * S
    SW = s5_d.shape[1]
    G = SW // SSM_GROUP
    NS = G * SSM_STATE
    NB = NS // LANES
    tm = min(256, S)
    tt = min(512, T)
    full = _gather_weights(w)

    h0 = x.reshape(T, D)
    w_ev_in = full['ev_w_in'][0]
    w_ev_out = full['ev_w_out'][0]
    y0, p0 = _norm_mm("ev_in", h0, mix_norm_g[0], w_ev_in, tm)
    PW = p0.shape[1]
    p03 = p0.reshape(B, S, PW)
    lr, li, ldt = s5_lam_re[0], s5_lam_im[0], s5_log_dt[0].reshape(G, 1)
    ar, ai, zr, zi = _s5_prep(lr, li, ldt)
    bre = _block_diag(jnp.swapaxes(s5_b_re[0], 1, 2))
    bim = _block_diag(jnp.swapaxes(s5_b_im[0], 1, 2))
    cbr = _block_diag(jnp.swapaxes(s5_c_re[0], 1, 2)).astype(bf16)
    cbi = _block_diag(jnp.swapaxes(s5_c_im[0], 1, 2)).astype(bf16)
    zr_row, zi_row = zr.reshape(1, NS), zi.reshape(1, NS)
    bbd = _s5_bbd(zr_row, zi_row, bre, bim)
    ar_s, ai_s = ar.reshape(NB, 1, LANES), ai.reshape(NB, 1, LANES)
    xr, xi = _s5_in(p03, bbd, SW, tm)
    hr, hi = _s5_scan("s5_scan", xr, xi, ar_s, ai_s, False)
    w_glu = full['s5_w_glu'][0]
    dsk, bglu = s5_d.reshape(1, SW), s5_b_glu.reshape(1, SW)
    a_out = _s5_out(hr, hi, p03, cbr, cbi, dsk, w_glu, bglu, tm)
    ws, bst, gv = gm_w_s[0], gm_b_s[0].T, gm_v_g.reshape(1, -1)
    mixcat = _gmlp(p0, a_out.reshape(T, SW), ws, bst, gv, SW)
    h1 = _mm_resid("ev_out", mixcat, w_ev_out, h0, tm)

    def ffn_fwd(l, h):
        z, up = _norm_mm(f"ffn_up{l}", h, ffn_norm_g[l], full['ffn_w_up'][l], tm)
        up3 = up.reshape(B, S, -1)
        act = _ffn_act(f"ffn_act{l}", up3, full['ffn_conv_w'][l], ffn_conv_b[l].reshape(1, -1))
        hn = _mm_resid(f"ffn_down{l}", act.reshape(T, -1), full['ffn_w_down'][l], h, tm)
        return hn, (z, up3, act)

    h2, ffn0 = ffn_fwd(0, h1)
    w_od_in, w_od_out = full['od_w_in'][0], full['od_w_out'][0]
    od_cw, od_cb = full['od_conv_w'][0], full['od_conv_b']
    y1, p1 = _norm_mm("od_in", h2, mix_norm_g[1], w_od_in, tm)
    p13 = p1.reshape(B, S, -1)
    sc = _od_act(p13, od_cw, od_cb)
    h3 = _mm_resid("od_out", sc.reshape(T, D), w_od_out, h2, tm)
    h4, ffn1 = ffn_fwd(1, h3)

    dh4, loss_part, d_final_g = _final_loss(h4, final_norm_g, loss_target.reshape(T, D), tm)
    loss = lax.psum(loss_part[0, 0], ("x", "y", "c"))

    grads = {}

    def ffn_bwd(l, dh, h_in, saved):
        z, up3, act = saved
        w_down, w_up = full['ffn_w_down'][l], full['ffn_w_up'][l]
        da = _mm_nt(f"ffn_down_bwd{l}", dh, w_down, tm)
        g_down = _mm_tn(f"ffn_down_dw{l}", act.reshape(T, -1), dh, tt)
        dup3, dcw, dcb = _ffn_act_bwd(f"ffn_act_bwd{l}", up3, da.reshape(B, S, -1), full['ffn_conv_w'][l],
                                      ffn_conv_b[l].reshape(1, -1))
        dup = dup3.reshape(T, -1)
        g_up = _mm_tn(f"ffn_up_dw{l}", z, dup, tt)
        dh_new, dg = _mm_nt_normbwd(f"ffn_up_bwd{l}", dup, w_up, h_in, ffn_norm_g[l], dh, tm)
        return dh_new, g_down, g_up, dcw, dcb[0], dg[0]

    dh3, gd1, gu1, gcw1, gcb1, gng1 = ffn_bwd(1, dh4, h3, ffn1)
    dsc = _mm_nt("od_out_bwd", dh3, w_od_out, tm)
    grads['od_w_out'] = _mm_tn("od_out_dw", sc.reshape(T, D), dh3, tt)[None]
    dp13, d_od_cw, d_od_cb = _od_act_bwd(p13, dsc.reshape(B, S, D), od_cw, od_cb)
    dp1 = dp13.reshape(T, -1)
    grads['od_w_in'] = _mm_tn("od_in_dw", y1, dp1, tt)[None]
    grads['od_conv_w'] = d_od_cw[0][None]
    grads['od_conv_b'] = d_od_cb[0]
    dh2, gmix1 = _mm_nt_normbwd("od_in_bwd", dp1, w_od_in, h2, mix_norm_g[1], dh3, tm)
    dh1, gd0, gu0, gcw0, gcb0, gng0 = ffn_bwd(0, dh2, h1, ffn0)
    grads['ffn_w_down'] = jnp.stack([gd0, gd1])
    grads['ffn_w_up'] = jnp.stack([gu0, gu1])
    grads['ffn_conv_w'] = jnp.stack([gcw0, gcw1])
    grads['ffn_conv_b'] = jnp.stack([gcb0, gcb1])
    grads['ffn_norm_g'] = jnp.stack([gng0, gng1])
    grads['final_norm_g'] = d_final_g[0]

    dmix = _mm_nt("ev_out_bwd", dh1, w_ev_out, tm)
    grads['ev_w_out'] = _mm_tn("ev_out_dw", mixcat, dh1, tt)[None]
    duv, d_ws, d_bs, d_gv = _gmlp_bwd(p0, dmix, ws, bst, gv, SW)
    grads['gm_w_s'] = d_ws[None]
    grads['gm_b_s'] = d_bs[:, :, 0][None]
    grads['gm_v_g'] = d_gv
    dhr, dhi, du_skip, d_cbr, d_cbi, d_dsk, d_wglu, d_bglu = _s5_out_bwd(
        hr, hi, p03, dmix.reshape(B, S, D), cbr, cbi, dsk, w_glu, bglu, tm)
    grads['s5_c_re'] = jnp.swapaxes(_diag_blocks(d_cbr, G), 1, 2)[None]
    grads['s5_c_im'] = jnp.swapaxes(_diag_blocks(d_cbi, G), 1, 2)[None]
    grads['s5_d'] = d_dsk
    grads['s5_w_glu'] = d_wglu[None]
    grads['s5_b_glu'] = d_bglu
    gr, gi, dar, dai = _s5_scan("s5_rscan", dhr, dhi, ar_s, ai_s, True, hr, hi)
    dp03, d_bbd = _s5_in_bwd(gr, gi, p03, bbd, du_skip, duv.reshape(B, S, -1), tm)
    d_bre, d_bim, d_zr, d_zi = _s5_bbd_bwd(d_bbd, zr_row, zi_row, bre, bim)
    grads['s5_b_re'] = jnp.swapaxes(_diag_blocks(d_bre, G), 1, 2)[None]
    grads['s5_b_im'] = jnp.swapaxes(_diag_blocks(d_bim, G), 1, 2)[None]
    shp = (-1, G, SSM_STATE)
    d_lr, d_li, d_ldt = _s5_prep_bwd(lr, li, ldt, dar.reshape(shp), dai.reshape(shp), d_zr.reshape(shp),
                                     d_zi.reshape(shp))
    grads['s5_lam_re'] = d_lr[None]
    grads['s5_lam_im'] = d_li[None]
    grads['s5_log_dt'] = d_ldt.reshape(1, G)
    dp0 = dp03.reshape(T, PW)
    grads['ev_w_in'] = _mm_tn("ev_in_dw", y0, dp0, tt)[None]
    grad_x, gmix0 = _mm_nt_normbwd("ev_in_bwd", dp0, w_ev_in, h0, mix_norm_g[0], dh1, tm)
    grads['mix_norm_g'] = jnp.concatenate([gmix0, gmix1], axis=0)

    segs = []
    for n in WEIGHTS:
        gfull = grads[n].astype(f32)
        if n in SHARD_AXIS:
            segs.append(_split_chips(gfull, SHARD_AXIS[n]))
        else:
            segs.append(jnp.broadcast_to(gfull.reshape(1, -1), (N_CHIPS, gfull.size)))
    n_pack = sum(s_.shape[1] for s_ in segs)
    tr = 256 if n_pack >= (1 << 22) else SUBLANES
    unit = 2 * tr * PACK_W
    gp = _pad_rows(jnp.concatenate(segs, axis=1), unit)
    Rh = gp.shape[1] // (2 * PACK_W)
    g2 = jnp.swapaxes(gp.reshape(N_CHIPS, 2, Rh, PACK_W), 0, 1)
    c_idx = lax.axis_index("c").astype(jnp.int32).reshape(1)
    recv = _comm_pair_swap(g2)
    hsum = _pair_sum(g2, recv, c_idx, tr)
    r3 = _comm_chip_exchange(hsum)
    gs = _chip_sum(r3, tr)
    gfin = _comm_pair_share(gs).reshape(2 * Rh, PACK_W)

    def pack_local(d):
        flat = _pad_rows(jnp.concatenate([d[n].astype(f32).reshape(-1) for n in WEIGHTS])[None], unit)
        return flat.reshape(2 * Rh, PACK_W)

    delta, new_m, new_v = _adamw(gfin, pack_local(w), pack_local(mom), pack_local(var), tr)

    def unpack(p):
        flat, out, off = p.reshape(-1), [], 0
        for n in WEIGHTS:
            size = w[n].size
            out.append(flat[off:off + size].reshape(w[n].shape))
            off += size
        return out

    return (loss, grad_x.reshape(B, S, D), *unpack(gfin), *unpack(delta), *unpack(new_m), *unpack(new_v))
```

```python
import functools
import math

import jax
import jax.numpy as jnp
from jax import lax
from jax.experimental import pallas as pl
from jax.experimental.pallas import tpu as pltpu

f32 = jnp.float32
bf16 = jnp.bfloat16
MESH = pl.DeviceIdType.MESH

SSM_GROUP = 16
SSM_STATE = 64
GMLP_HEAD = 128
CHUNK = 128
EPS = 1e-6
LAMBDA_RE_MAX = -1e-4
ADAM_LR, ADAM_B1, ADAM_B2, ADAM_EPS, ADAM_WD, ADAM_STEP = 0.001, 0.9, 0.999, 1e-08, 0.01, 10

LANES = 128
SUBLANES = 8
HALO = 16
VMEM_LIMIT = 56 * 1024 * 1024
N_CHIPS = 4

WEIGHTS = ['mix_norm_g', 'ffn_norm_g', 'final_norm_g', 'ev_w_in', 'ev_w_out', 's5_lam_re', 's5_lam_im', 's5_log_dt',
           's5_b_re', 's5_b_im', 's5_c_re', 's5_c_im', 's5_d', 's5_w_glu', 's5_b_glu', 'gm_w_s', 'gm_b_s', 'gm_v_g',
           'od_w_in', 'od_conv_w', 'od_conv_b', 'od_w_out', 'ffn_w_up', 'ffn_conv_w', 'ffn_conv_b', 'ffn_w_down']
SHARD_AXIS = {'ev_w_in': 2, 'ev_w_out': 1, 's5_w_glu': 1, 'od_w_in': 2, 'od_conv_w': 2, 'od_conv_b': 1, 'od_w_out': 1,
              'ffn_w_up': 2, 'ffn_conv_w': 2, 'ffn_w_down': 1}
GATHER_BF16 = ['ev_w_in', 'ev_w_out', 's5_w_glu', 'od_w_in', 'od_w_out', 'ffn_w_up', 'ffn_w_down']
GATHER_F32 = ['od_conv_w', 'od_conv_b', 'ffn_conv_w']

_GELU_K0 = math.sqrt(2.0 / math.pi)
_GELU_K1 = 0.044715
NT = (((1,), (1,)), ((), ()))
TN = (((0,), (0,)), ((), ()))


def _pick(n, cap):
    if n <= cap:
        return n
    best = None
    for d in range(LANES, cap + 1, LANES):
        if n % d == 0:
            best = d
    assert best is not None, (n, cap)
    return best


def _params(sem=None):
    return pltpu.CompilerParams(dimension_semantics=sem, vmem_limit_bytes=VMEM_LIMIT)


def _gelu(x):
    return 0.5 * x * (1.0 + jnp.tanh(_GELU_K0 * (x + _GELU_K1 * x * x * x)))


def _gelu_grad(x):
    t = jnp.tanh(_GELU_K0 * (x + _GELU_K1 * x * x * x))
    return 0.5 * (1.0 + t) + 0.5 * x * (1.0 - t * t) * _GELU_K0 * (1.0 + 3.0 * _GELU_K1 * x * x)


def _rms_stats(x):
    r = lax.rsqrt(jnp.mean(x * x, axis=-1, keepdims=True) + EPS)
    return x * r, r


def _rms_bwd(dy, xh, r, g):
    dxh = dy * g
    dx = r * (dxh - xh * jnp.mean(dxh * xh, axis=-1, keepdims=True))
    return dx, jnp.sum(dy * xh, axis=0, keepdims=True)


def _dot(a, b):
    return jnp.dot(a, b, preferred_element_type=f32)


def _dg(a, b, dims):
    return lax.dot_general(a, b, dims, preferred_element_type=f32)


def _row_fold(z):
    return z.reshape(z.shape[0] // SUBLANES, SUBLANES, z.shape[1]).sum(axis=0)


def _norm_mm(name, h, g, w, tm):
    T, D = h.shape
    N = w.shape[1]
    nc = _pick(N, 512)

    def body(h_ref, g_ref, w_ref, y_ref, o_ref):
        xh, _ = _rms_stats(h_ref[...])
        y = (xh * g_ref[...]).astype(bf16)
        y_ref[...] = y
        for j in range(N // nc):
            o_ref[:, j * nc:(j + 1) * nc] = _dot(y, w_ref[:, j * nc:(j + 1) * nc]).astype(bf16)

    return pl.pallas_call(
        body, name=name, grid=(T // tm,),
        in_specs=[pl.BlockSpec((tm, D), lambda i: (i, 0)), pl.BlockSpec((1, D), lambda i: (0, 0)),
                  pl.BlockSpec((D, N), lambda i: (0, 0))],
        out_specs=[pl.BlockSpec((tm, D), lambda i: (i, 0)), pl.BlockSpec((tm, N), lambda i: (i, 0))],
        out_shape=[jax.ShapeDtypeStruct((T, D), bf16), jax.ShapeDtypeStruct((T, N), bf16)],
        compiler_params=_params(("parallel",)))(h, g.reshape(1, D), w)


def _mm_resid(name, a, w, resid, tm):
    T, K = a.shape
    N = w.shape[1]

    def body(a_ref, w_ref, r_ref, o_ref):
        o_ref[...] = r_ref[...] + _dot(a_ref[...], w_ref[...])

    return pl.pallas_call(
        body, name=name, grid=(T // tm,),
        in_specs=[pl.BlockSpec((tm, K), lambda i: (i, 0)), pl.BlockSpec((K, N), lambda i: (0, 0)),
                  pl.BlockSpec((tm, N), lambda i: (i, 0))],
        out_specs=pl.BlockSpec((tm, N), lambda i: (i, 0)),
        out_shape=jax.ShapeDtypeStruct((T, N), f32),
        compiler_params=_params(("parallel",)))(a, w, resid)


def _mm_nt(name, dy, w, tm):
    T, N = dy.shape
    K = w.shape[0]
    kc = _pick(K, 512)

    def body(d_ref, w_ref, o_ref):
        d = d_ref[...].astype(bf16)
        for j in range(K // kc):
            o_ref[:, j * kc:(j + 1) * kc] = _dg(d, w_ref[j * kc:(j + 1) * kc, :], NT).astype(bf16)

    return pl.pallas_call(
        body, name=name, grid=(T // tm,),
        in_specs=[pl.BlockSpec((tm, N), lambda i: (i, 0)), pl.BlockSpec((K, N), lambda i: (0, 0))],
        out_specs=pl.BlockSpec((tm, K), lambda i: (i, 0)),
        out_shape=jax.ShapeDtypeStruct((T, K), bf16),
        compiler_params=_params(("parallel",)))(dy, w)


def _mm_nt_normbwd(name, dy, w, h, g, dh_in, tm):
    T, N = dy.shape
    D = w.shape[0]

    def body(d_ref, w_ref, h_ref, g_ref, dh_ref, o_ref, dg_ref):
        dz = _dg(d_ref[...].astype(bf16), w_ref[...], NT)
        xh, r = _rms_stats(h_ref[...])
        dx, dg = _rms_bwd(dz, xh, r, g_ref[...])
        o_ref[...] = dh_ref[...] + dx

        @pl.when(pl.program_id(0) == 0)
        def _():
            dg_ref[...] = jnp.zeros_like(dg_ref)
        dg_ref[...] += dg

    return pl.pallas_call(
        body, name=name, grid=(T // tm,),
        in_specs=[pl.BlockSpec((tm, N), lambda i: (i, 0)), pl.BlockSpec((D, N), lambda i: (0, 0)),
                  pl.BlockSpec((tm, D), lambda i: (i, 0)), pl.BlockSpec((1, D), lambda i: (0, 0)),
                  pl.BlockSpec((tm, D), lambda i: (i, 0))],
        out_specs=[pl.BlockSpec((tm, D), lambda i: (i, 0)), pl.BlockSpec((1, D), lambda i: (0, 0))],
        out_shape=[jax.ShapeDtypeStruct((T, D), f32), jax.ShapeDtypeStruct((1, D), f32)],
        compiler_params=_params(("arbitrary",)))(dy, w, h, g.reshape(1, D), dh_in)


def _mm_tn(name, a, b, tt):
    T, K = a.shape
    N = b.shape[1]
    tk = _pick(K, 512)
    tn = _pick(N, 1024)

    def body(a_ref, b_ref, o_ref):
        @pl.when(pl.program_id(2) == 0)
        def _():
            o_ref[...] = jnp.zeros_like(o_ref)
        o_ref[...] += _dg(a_ref[...].astype(bf16), b_ref[...].astype(bf16), TN)

    return pl.pallas_call(
        body, name=name, grid=(K // tk, N // tn, T // tt),
        in_specs=[pl.BlockSpec((tt, tk), lambda k, n, t: (t, k)), pl.BlockSpec((tt, tn), lambda k, n, t: (t, n))],
        out_specs=pl.BlockSpec((tk, tn), lambda k, n, t: (k, n)),
        out_shape=jax.ShapeDtypeStruct((K, N), f32),
        compiler_params=_params(("parallel", "parallel", "arbitrary")))(a, b)


def _final_loss(h, g, tgt, tm):
    T, D = h.shape

    def body(h_ref, g_ref, t_ref, dh_ref, loss_ref, dg_ref):
        xh, r = _rms_stats(h_ref[...])
        gg = g_ref[...]
        diff = xh * gg - t_ref[...]
        dy = diff * (1.0 / D)
        dx, dg = _rms_bwd(dy, xh, r, gg)
        dh_ref[...] = dx

        @pl.when(pl.program_id(0) == 0)
        def _():
            dg_ref[...] = jnp.zeros_like(dg_ref)
            loss_ref[...] = jnp.zeros_like(loss_ref)
        dg_ref[...] += dg
        loss_ref[...] += (0.5 / D) * jnp.sum(jnp.sum(diff * diff, axis=1, keepdims=True), axis=0, keepdims=True)

    return pl.pallas_call(
        body, name="final_loss", grid=(T // tm,),
        in_specs=[pl.BlockSpec((tm, D), lambda i: (i, 0)), pl.BlockSpec((1, D), lambda i: (0, 0)),
                  pl.BlockSpec((tm, D), lambda i: (i, 0))],
        out_specs=[pl.BlockSpec((tm, D), lambda i: (i, 0)), pl.BlockSpec((1, 1), lambda i: (0, 0)),
                   pl.BlockSpec((1, D), lambda i: (0, 0))],
        out_shape=[jax.ShapeDtypeStruct((T, D), f32), jax.ShapeDtypeStruct((1, 1), f32),
                   jax.ShapeDtypeStruct((1, D), f32)],
        compiler_params=_params(("arbitrary",)))(h, g.reshape(1, D), tgt)


def _taps(load, r0, R):
    main = load(r0, R)
    hs = pl.multiple_of(jnp.maximum(r0 - HALO, 0), HALO)
    halo = load(hs, HALO) * (r0 > 0).astype(f32)
    ext = jnp.concatenate([halo, main], axis=0)
    xm1 = pltpu.roll(ext, 1, 0)[HALO:]
    xm2 = pltpu.roll(ext, 2, 0)[HALO:]
    return xm2, xm1, main


def _rev_conv(ref, w, r0, R, S):
    main = ref[pl.ds(r0, R), :]
    hs = pl.multiple_of(jnp.minimum(r0 + R, S - HALO), HALO)
    halo = ref[pl.ds(hs, HALO), :] * (r0 + R < S).astype(f32)
    ext = jnp.concatenate([main, halo], axis=0)
    n = R + HALO
    xp1 = pltpu.roll(ext, n - 1, 0)[:R]
    xp2 = pltpu.roll(ext, n - 2, 0)[:R]
    return w[2:3] * main + w[1:2] * xp1 + w[0:1] * xp2


def _conv(w, b, taps):
    return b + w[0:1] * taps[0] + w[1:2] * taps[1] + w[2:3] * taps[2]


def _ref_load(ref):
    return lambda s, n: ref[pl.ds(s, n), :].astype(f32)


def _ffn_act(name, up3, cw, cb):
    B, S, F2 = up3.shape
    F = F2 // 2
    cwid = _pick(F, 256)
    nF = F // cwid
    R = min(256, S)

    def body(g_ref, v_ref, wg_ref, wv_ref, bg_ref, bv_ref, o_ref):
        wg, wv, bg, bv = wg_ref[...], wv_ref[...], bg_ref[...], bv_ref[...]

        def chunk(r, c):
            r0 = pl.multiple_of(r * R, R)
            cg = _conv(wg, bg, _taps(_ref_load(g_ref), r0, R))
            cv = _conv(wv, bv, _taps(_ref_load(v_ref), r0, R))
            o_ref[pl.ds(r0, R), :] = (cg * jax.nn.sigmoid(cg) * cv).astype(bf16)
            return c
        lax.fori_loop(0, S // R, chunk, 0)

    blk = lambda off: pl.BlockSpec((None, S, cwid), lambda b, j: (b, 0, off + j))
    wblk = lambda off: pl.BlockSpec((3, cwid), lambda b, j: (0, off + j))
    bblk = lambda off: pl.BlockSpec((1, cwid), lambda b, j: (0, off + j))
    return pl.pallas_call(
        body, name=name, grid=(B, nF),
        in_specs=[blk(0), blk(nF), wblk(0), wblk(nF), bblk(0), bblk(nF)],
        out_specs=pl.BlockSpec((None, S, cwid), lambda b, j: (b, 0, j)),
        out_shape=jax.ShapeDtypeStruct((B, S, F), bf16),
        compiler_params=_params(("parallel", "parallel")))(up3, up3, cw, cw, cb, cb)


def _ffn_act_bwd(name, up3, da3, cw, cb):
    B, S, F2 = up3.shape
    F = F2 // 2
    cwid = _pick(F, 256)
    nF = F // cwid
    R = min(256, S)

    def body(g_ref, v_ref, da_ref, wg_ref, wv_ref, bg_ref, bv_ref, wo_ref, dup_ref, dcw_ref, dcb_ref, dc_scr):
        is_val = pl.program_id(0) >= nF
        wg, wv, bg, bv, wo = wg_ref[...], wv_ref[...], bg_ref[...], bv_ref[...], wo_ref[...]

        def p1(r, acc):
            r0 = pl.multiple_of(r * R, R)
            tg = _taps(_ref_load(g_ref), r0, R)
            tv = _taps(_ref_load(v_ref), r0, R)
            cg = _conv(wg, bg, tg)
            cv = _conv(wv, bv, tv)
            da = da_ref[pl.ds(r0, R), :].astype(f32)
            sg = jax.nn.sigmoid(cg)
            dgate = da * cv * (sg * (1.0 + cg * (1.0 - sg)))
            dval = da * (cg * sg)
            dc = jnp.where(is_val, dval, dgate)
            dc_scr[pl.ds(r0, R), :] = dc
            own = [jnp.where(is_val, a, b) for a, b in zip(tv, tg)]
            return (acc[0] + _row_fold(dc * own[0]), acc[1] + _row_fold(dc * own[1]),
                    acc[2] + _row_fold(dc * own[2]), acc[3] + _row_fold(dc))
        z = jnp.zeros((SUBLANES, cwid), f32)
        acc = lax.fori_loop(0, S // R, p1, (z, z, z, z))

        @pl.when(pl.program_id(1) == 0)
        def _():
            dcw_ref[...] = jnp.zeros_like(dcw_ref)
            dcb_ref[...] = jnp.zeros_like(dcb_ref)
        for k in range(3):
            dcw_ref[k:k + 1, :] += jnp.sum(acc[k], axis=0, keepdims=True)
        dcb_ref[...] += jnp.sum(acc[3], axis=0, keepdims=True)

        def p2(r, c):
            r0 = pl.multiple_of(r * R, R)
            dup_ref[pl.ds(r0, R), :] = _rev_conv(dc_scr, wo, r0, R, S).astype(bf16)
            return c
        lax.fori_loop(0, S // R, p2, 0)

    blk = lambda off: pl.BlockSpec((None, S, cwid), lambda jj, b: (b, 0, off + jj % nF))
    wblk = lambda off: pl.BlockSpec((3, cwid), lambda jj, b: (0, off + jj % nF))
    bblk = lambda off: pl.BlockSpec((1, cwid), lambda jj, b: (0, off + jj % nF))
    return pl.pallas_call(
        body, name=name, grid=(2 * nF, B),
        in_specs=[blk(0), blk(nF), blk(0), wblk(0), wblk(nF), bblk(0), bblk(nF),
                  pl.BlockSpec((3, cwid), lambda jj, b: (0, jj))],
        out_specs=[pl.BlockSpec((None, S, cwid), lambda jj, b: (b, 0, jj)),
                   pl.BlockSpec((3, cwid), lambda jj, b: (0, jj)), pl.BlockSpec((1, cwid), lambda jj, b: (0, jj))],
        out_shape=[jax.ShapeDtypeStruct((B, S, F2), bf16), jax.ShapeDtypeStruct((3, F2), f32),
                   jax.ShapeDtypeStruct((1, F2), f32)],
        scratch_shapes=[pltpu.VMEM((S, cwid), f32)],
        compiler_params=_params(("parallel", "arbitrary")))(up3, up3, da3, cw, cw, cb, cb, cw)


def _od_act(p3, cw, cb):
    B, S, D3 = p3.shape
    D = D3 // 3
    cwid = _pick(D, 256)
    nD = D // cwid
    R = min(256, S)

    def body(bg_ref, cg_ref, hx_ref, w_ref, b_ref, o_ref):
        w, b = w_ref[...], b_ref[...]
        q = lambda s, n: cg_ref[pl.ds(s, n), :].astype(f32) * hx_ref[pl.ds(s, n), :].astype(f32)

        def chunk(r, c):
            r0 = pl.multiple_of(r * R, R)
            cq = _conv(w, b, _taps(q, r0, R))
            o_ref[pl.ds(r0, R), :] = (bg_ref[pl.ds(r0, R), :].astype(f32) * cq).astype(bf16)
            return c
        lax.fori_loop(0, S // R, chunk, 0)

    blk = lambda off: pl.BlockSpec((None, S, cwid), lambda b, j: (b, 0, off + j))
    return pl.pallas_call(
        body, name="od_act", grid=(B, nD),
        in_specs=[blk(0), blk(nD), blk(2 * nD), pl.BlockSpec((3, cwid), lambda b, j: (0, j)),
                  pl.BlockSpec((1, cwid), lambda b, j: (0, j))],
        out_specs=pl.BlockSpec((None, S, cwid), lambda b, j: (b, 0, j)),
        out_shape=jax.ShapeDtypeStruct((B, S, D), bf16),
        compiler_params=_params(("parallel", "parallel")))(p3, p3, p3, cw, cb)


def _od_act_bwd(p3, dsc3, cw, cb):
    B, S, D3 = p3.shape
    D = D3 // 3
    cwid = _pick(D, 256)
    nD = D // cwid
    R = min(256, S)

    def body(bg_ref, cg_ref, hx_ref, d_ref, w_ref, b_ref, dp_ref, dcw_ref, dcb_ref, dc_scr):
        part = pl.program_id(0) // nD
        w, b = w_ref[...], b_ref[...]
        q = lambda s, n: cg_ref[pl.ds(s, n), :].astype(f32) * hx_ref[pl.ds(s, n), :].astype(f32)

        def p1(r, acc):
            r0 = pl.multiple_of(r * R, R)
            tq = _taps(q, r0, R)
            cq = _conv(w, b, tq)
            d = d_ref[pl.ds(r0, R), :].astype(f32)
            dp_ref[pl.ds(r0, R), :] = (d * cq).astype(bf16)
            dcq = d * bg_ref[pl.ds(r0, R), :].astype(f32)
            dc_scr[pl.ds(r0, R), :] = dcq
            return (acc[0] + _row_fold(dcq * tq[0]), acc[1] + _row_fold(dcq * tq[1]),
                    acc[2] + _row_fold(dcq * tq[2]), acc[3] + _row_fold(dcq))
        z = jnp.zeros((SUBLANES, cwid), f32)
        acc = lax.fori_loop(0, S // R, p1, (z, z, z, z))

        @pl.when(pl.program_id(1) == 0)
        def _():
            dcw_ref[...] = jnp.zeros_like(dcw_ref)
            dcb_ref[...] = jnp.zeros_like(dcb_ref)
        for k in range(3):
            dcw_ref[k:k + 1, :] += jnp.sum(acc[k], axis=0, keepdims=True)
        dcb_ref[...] += jnp.sum(acc[3], axis=0, keepdims=True)

        @pl.when(part > 0)
        def _():
            def p2(r, c):
                r0 = pl.multiple_of(r * R, R)
                dq = _rev_conv(dc_scr, w, r0, R, S)
                dcg = dq * hx_ref[pl.ds(r0, R), :].astype(f32)
                dhx = dq * cg_ref[pl.ds(r0, R), :].astype(f32)
                dp_ref[pl.ds(r0, R), :] = jnp.where(part == 1, dcg, dhx).astype(bf16)
                return c
            lax.fori_loop(0, S // R, p2, 0)

    blk = lambda off: pl.BlockSpec((None, S, cwid), lambda jj, b: (b, 0, off + jj % nD))
    return pl.pallas_call(
        body, name="od_act_bwd", grid=(3 * nD, B),
        in_specs=[blk(0), blk(nD), blk(2 * nD), blk(0), pl.BlockSpec((3, cwid), lambda jj, b: (0, jj % nD)),
                  pl.BlockSpec((1, cwid), lambda jj, b: (0, jj % nD))],
        out_specs=[pl.BlockSpec((None, S, cwid), lambda jj, b: (b, 0, jj)),
                   pl.BlockSpec((None, 3, cwid), lambda jj, b: (jj // nD, 0, jj % nD)),
                   pl.BlockSpec((None, 1, cwid), lambda jj, b: (jj // nD, 0, jj % nD))],
        out_shape=[jax.ShapeDtypeStruct((B, S, D3), bf16), jax.ShapeDtypeStruct((3, 3, D), f32),
                   jax.ShapeDtypeStruct((3, 1, D), f32)],
        scratch_shapes=[pltpu.VMEM((S, cwid), f32)],
        compiler_params=_params(("parallel", "arbitrary")))(p3, p3, p3, dsc3, cw, cb)


def _gmlp_parts(p, gv, SW, GW):
    uv = p[:, SW:].astype(f32)
    ge = _gelu(uv)
    u, v = ge[:, :GW], ge[:, GW:]
    vh, r = _rms_stats(v)
    return uv, u, vh, r, vh * gv


def _tril():
    rows = lax.broadcasted_iota(jnp.int32, (CHUNK, CHUNK), 0)
    cols = lax.broadcasted_iota(jnp.int32, (CHUNK, CHUNK), 1)
    return rows >= cols


def _gmlp(p0, a_out, ws, bst, gv, SW):
    T, PW = p0.shape
    GW = (PW - SW) // 2
    H = GW // GMLP_HEAD
    D = SW + GW

    def body(p_ref, a_ref, ws_ref, b_ref, gv_ref, o_ref):
        _, u, _, _, vn = _gmlp_parts(p_ref[...], gv_ref[...], SW, GW)
        tri = _tril()
        o_ref[:, :SW] = a_ref[...]
        for hh in range(H):
            sl = slice(hh * GMLP_HEAD, (hh + 1) * GMLP_HEAD)
            wm = jnp.where(tri, ws_ref[hh], 0.0).astype(bf16)
            gate = _dot(wm, vn[:, sl].astype(bf16)) + b_ref[:, hh:hh + 1]
            o_ref[:, SW + hh * GMLP_HEAD:SW + (hh + 1) * GMLP_HEAD] = (u[:, sl] * gate).astype(bf16)

    return pl.pallas_call(
        body, name="gmlp", grid=(T // CHUNK,),
        in_specs=[pl.BlockSpec((CHUNK, PW), lambda i: (i, 0)), pl.BlockSpec((CHUNK, SW), lambda i: (i, 0)),
                  pl.BlockSpec((H, CHUNK, CHUNK), lambda i: (0, 0, 0)), pl.BlockSpec((CHUNK, H), lambda i: (0, 0)),
                  pl.BlockSpec((1, GW), lambda i: (0, 0))],
        out_specs=pl.BlockSpec((CHUNK, D), lambda i: (i, 0)),
        out_shape=jax.ShapeDtypeStruct((T, D), bf16),
        compiler_params=_params(("parallel",)))(p0, a_out, ws, bst, gv)


def _gmlp_bwd(p0, dmix, ws, bst, gv, SW):
    T, PW = p0.shape
    GW = (PW - SW) // 2
    H = GW // GMLP_HEAD
    D = SW + GW

    def body(p_ref, d_ref, ws_ref, b_ref, gv_ref, duv_ref, dws_ref, dbs_ref, dgv_ref):
        gv_ = gv_ref[...]
        uv, u, vh, r, vn = _gmlp_parts(p_ref[...], gv_, SW, GW)
        dout = d_ref[...][:, SW:].astype(f32)
        tri = _tril()

        @pl.when(pl.program_id(0) == 0)
        def _():
            dws_ref[...] = jnp.zeros_like(dws_ref)
            dbs_ref[...] = jnp.zeros_like(dbs_ref)
            dgv_ref[...] = jnp.zeros_like(dgv_ref)
        du, dvn = [], []
        for hh in range(H):
            sl = slice(hh * GMLP_HEAD, (hh + 1) * GMLP_HEAD)
            wm = jnp.where(tri, ws_ref[hh], 0.0).astype(bf16)
            vnh = vn[:, sl].astype(bf16)
            gate = _dot(wm, vnh) + b_ref[:, hh:hh + 1]
            dgate = dout[:, sl] * u[:, sl]
            du.append(dout[:, sl] * gate)
            dgb = dgate.astype(bf16)
            dws_ref[hh] += jnp.where(tri, _dg(dgb, vnh, NT), 0.0)
            dbs_ref[hh] += jnp.broadcast_to(jnp.sum(dgate, axis=1, keepdims=True), (CHUNK, CHUNK))
            dvn.append(_dg(wm, dgb, TN))
        dvn = jnp.concatenate(dvn, axis=1)
        dv, dgv = _rms_bwd(dvn, vh, r, gv_)
        dgv_ref[...] += dgv
        dge = jnp.concatenate(du + [dv], axis=1)
        duv_ref[...] = (dge * _gelu_grad(uv)).astype(bf16)

    return pl.pallas_call(
        body, name="gmlp_bwd", grid=(T // CHUNK,),
        in_specs=[pl.BlockSpec((CHUNK, PW), lambda i: (i, 0)), pl.BlockSpec((CHUNK, D), lambda i: (i, 0)),
                  pl.BlockSpec((H, CHUNK, CHUNK), lambda i: (0, 0, 0)), pl.BlockSpec((CHUNK, H), lambda i: (0, 0)),
                  pl.BlockSpec((1, GW), lambda i: (0, 0))],
        out_specs=[pl.BlockSpec((CHUNK, 2 * GW), lambda i: (i, 0)), pl.BlockSpec((H, CHUNK, CHUNK), lambda i: (0, 0, 0)),
                   pl.BlockSpec((H, CHUNK, CHUNK), lambda i: (0, 0, 0)), pl.BlockSpec((1, GW), lambda i: (0, 0))],
        out_shape=[jax.ShapeDtypeStruct((T, 2 * GW), bf16), jax.ShapeDtypeStruct((H, CHUNK, CHUNK), f32),
                   jax.ShapeDtypeStruct((H, CHUNK, CHUNK), f32), jax.ShapeDtypeStruct((1, GW), f32)],
        compiler_params=_params(("arbitrary",)))(p0, dmix, ws, bst, gv)


def _s5_disc(lr, li, ldt):
    lr = jnp.minimum(lr, LAMBDA_RE_MAX)
    dt = jnp.exp(ldt)
    mag = jnp.exp(lr * dt)
    ar = mag * jnp.cos(li * dt)
    ai = mag * jnp.sin(li * dt)
    den = lr * lr + li * li
    nr = ar - 1.0
    zr = (nr * lr + ai * li) / den
    zi = (ai * lr - nr * li) / den
    return ar, ai, zr, zi


def _s5_prep(lr, li, ldt):
    G, P = lr.shape

    def body(lr_ref, li_ref, ldt_ref, ar_ref, ai_ref, zr_ref, zi_ref):
        ar, ai, zr, zi = _s5_disc(lr_ref[...], li_ref[...], ldt_ref[...])
        ar_ref[...] = ar
        ai_ref[...] = ai
        zr_ref[...] = zr
        zi_ref[...] = zi

    s = jax.ShapeDtypeStruct((G, P), f32)
    return pl.pallas_call(body, name="s5_prep", out_shape=[s, s, s, s])(lr, li, ldt)


def _s5_prep_bwd(lr, li, ldt, dar, dai, dzr, dzi):
    G, P = lr.shape

    def body(lr_ref, li_ref, ldt_ref, dar_ref, dai_ref, dzr_ref, dzi_ref, o1, o2, o3):
        _, vjp = jax.vjp(_s5_disc, lr_ref[...], li_ref[...], ldt_ref[...])
        cts = tuple(jnp.sum(r[...], axis=0) for r in (dar_ref, dai_ref, dzr_ref, dzi_ref))
        a, b, c = vjp(cts)
        o1[...] = a
        o2[...] = b
        o3[...] = c

    s = jax.ShapeDtypeStruct((G, P), f32)
    return pl.pallas_call(body, name="s5_prep_bwd", out_shape=[s, s, jax.ShapeDtypeStruct((G, 1), f32)])(
        lr, li, ldt, dar, dai, dzr, dzi)


def _s5_bbd(zr, zi, bre, bim):
    SW, NS = bre.shape

    def body(zr_ref, zi_ref, br_ref, bi_ref, o_ref):
        zr_, zi_, br, bi = zr_ref[...], zi_ref[...], br_ref[...], bi_ref[...]
        o_ref[:, :NS] = (zr_ * br - zi_ * bi).astype(bf16)
        o_ref[:, NS:] = (zr_ * bi + zi_ * br).astype(bf16)

    return pl.pallas_call(body, name="s5_bbd", out_shape=jax.ShapeDtypeStruct((SW, 2 * NS), bf16))(zr, zi, bre, bim)


def _s5_bbd_bwd(dbbd, zr, zi, bre, bim):
    SW, NS = bre.shape

    def body(d_ref, zr_ref, zi_ref, br_ref, bi_ref, dbr_ref, dbi_ref, dzr_ref, dzi_ref):
        zr_, zi_, br, bi = zr_ref[...], zi_ref[...], br_ref[...], bi_ref[...]
        dr, di = d_ref[:, :NS], d_ref[:, NS:]
        dbr_ref[...] = zr_ * dr + zi_ * di
        dbi_ref[...] = zr_ * di - zi_ * dr
        dzr_ref[...] = jnp.sum(dr * br + di * bi, axis=0, keepdims=True)
        dzi_ref[...] = jnp.sum(di * br - dr * bi, axis=0, keepdims=True)

    m = jax.ShapeDtypeStruct((SW, NS), f32)
    v = jax.ShapeDtypeStruct((1, NS), f32)
    return pl.pallas_call(body, name="s5_bbd_bwd", out_shape=[m, m, v, v])(dbbd, zr, zi, bre, bim)


def _slab_cat(ref, NB):
    return jnp.concatenate([ref[j] for j in range(NB)], axis=1)


def _s5_in(p3, bbd, SW, tm):
    B, S, PW = p3.shape
    NS = bbd.shape[1] // 2
    NB = NS // LANES

    def body(u_ref, b_ref, xr_ref, xi_ref):
        x = _dot(u_ref[...], b_ref[...])
        for j in range(NB):
            xr_ref[j] = x[:, j * LANES:(j + 1) * LANES]
            xi_ref[j] = x[:, NS + j * LANES:NS + (j + 1) * LANES]

    slab = jax.ShapeDtypeStruct((B, NB, S, LANES), f32)
    sspec = pl.BlockSpec((None, NB, tm, LANES), lambda b, i: (b, 0, i, 0))
    return pl.pallas_call(
        body, name="s5_in", grid=(B, S // tm),
        in_specs=[pl.BlockSpec((None, tm, SW), lambda b, i: (b, i, 0)), pl.BlockSpec((SW, 2 * NS), lambda b, i: (0, 0))],
        out_specs=[sspec, sspec], out_shape=[slab, slab],
        compiler_params=_params(("parallel", "parallel")))(p3, bbd)


def _s5_scan(name, xr, xi, ar, ai, reverse, hr=None, hi=None):
    B, NB, S, _ = xr.shape
    L = S // SUBLANES
    nb = 2 if (hr is None and NB % 2 == 0) else 1
    with_da = hr is not None

    def body(*refs):
        if with_da:
            xr_ref, xi_ref, ar_ref, ai_ref, hr_ref, hi_ref, or_ref, oi_ref, dar_ref, dai_ref, pr_scr, pi_scr = refs
        else:
            xr_ref, xi_ref, ar_ref, ai_ref, or_ref, oi_ref, pr_scr, pi_scr = refs
        sign = -1.0 if reverse else 1.0
        a_r = [jnp.broadcast_to(ar_ref[j], (SUBLANES, LANES)) for j in range(nb)]
        a_i = [jnp.broadcast_to(ai_ref[j], (SUBLANES, LANES)) * sign for j in range(nb)]

        def step(t, carry):
            row = (L - 1 - t) if reverse else t
            rows = pl.ds(row, SUBLANES, stride=L)
            out = []
            for j in range(nb):
                sr, si, pr, pi = carry[j]
                nr = a_r[j] * sr - a_i[j] * si + xr_ref.at[j][rows, :]
                ni = a_r[j] * si + a_i[j] * sr + xi_ref.at[j][rows, :]
                or_ref.at[j][rows, :] = nr
                oi_ref.at[j][rows, :] = ni
                npr = a_r[j] * pr - a_i[j] * pi
                npi = a_r[j] * pi + a_i[j] * pr
                pr_scr[j, pl.ds(row, 1), :] = npr[0:1]
                pi_scr[j, pl.ds(row, 1), :] = npi[0:1]
                out.append((nr, ni, npr, npi))
            return tuple(out)
        z = jnp.zeros((SUBLANES, LANES), f32)
        one = jnp.ones((SUBLANES, LANES), f32)
        fin = lax.fori_loop(0, L, step, tuple((z, z, one, z) for _ in range(nb)))

        for j in range(nb):
            sr, si, plr, pli = fin[j]
            plr, pli = plr[0:1], pli[0:1]
            cr = jnp.zeros((1, LANES), f32)
            ci = jnp.zeros((1, LANES), f32)
            order = range(SUBLANES - 2, -1, -1) if reverse else range(1, SUBLANES)
            for c in order:
                src = c + 1 if reverse else c - 1
                cr, ci = (sr[src:src + 1] + plr * cr - pli * ci, si[src:src + 1] + plr * ci + pli * cr)
                rows = slice(c * L, (c + 1) * L)
                tr, ti = pr_scr[j], pi_scr[j]
                or_ref[j, rows, :] += tr * cr - ti * ci
                oi_ref[j, rows, :] += tr * ci + ti * cr
            if with_da:
                first = lax.broadcasted_iota(jnp.int32, (L, LANES), 0) == 0
                dar = jnp.zeros((1, LANES), f32)
                dai = jnp.zeros((1, LANES), f32)
                for c in range(SUBLANES):
                    rows = slice(c * L, (c + 1) * L)
                    if c == 0:
                        lr_, li_ = jnp.zeros((1, LANES), f32), jnp.zeros((1, LANES), f32)
                    else:
                        lr_, li_ = hr_ref[j, c * L - 1:c * L, :], hi_ref[j, c * L - 1:c * L, :]
                    hpr = jnp.where(first, lr_, pltpu.roll(hr_ref[j, rows, :], 1, 0))
                    hpi = jnp.where(first, li_, pltpu.roll(hi_ref[j, rows, :], 1, 0))
                    gr, gi = or_ref[j, rows, :], oi_ref[j, rows, :]
                    dar += jnp.sum(hpr * gr + hpi * gi, axis=0, keepdims=True)
                    dai += jnp.sum(hpr * gi - hpi * gr, axis=0, keepdims=True)
                dar_ref[j] = dar
                dai_ref[j] = dai

    slab = jax.ShapeDtypeStruct((B, NB, S, LANES), f32)
    sspec = pl.BlockSpec((None, nb, S, LANES), lambda b, j: (b, j, 0, 0))
    aspec = pl.BlockSpec((nb, 1, LANES), lambda b, j: (j, 0, 0))
    in_specs = [sspec, sspec, aspec, aspec]
    out_specs = [sspec, sspec]
    out_shape = [slab, slab]
    args = [xr, xi, ar, ai]
    if with_da:
        in_specs += [sspec, sspec]
        args += [hr, hi]
        dspec = pl.BlockSpec((None, nb, 1, LANES), lambda b, j: (b, j, 0, 0))
        out_specs += [dspec, dspec]
        out_shape += [jax.ShapeDtypeStruct((B, NB, 1, LANES), f32)] * 2
    return pl.pallas_call(
        body, name=name, grid=(B, NB // nb), in_specs=in_specs, out_specs=out_specs, out_shape=out_shape,
        scratch_shapes=[pltpu.VMEM((nb, L, LANES), f32), pltpu.VMEM((nb, L, LANES), f32)],
        compiler_params=_params(("parallel", "parallel")))(*args)


def _s5_out_parts(hr_ref, hi_ref, u_ref, cr_ref, ci_ref, d_ref, wg_ref, bg_ref, NB):
    hcr = _slab_cat(hr_ref, NB).astype(bf16)
    hci = _slab_cat(hi_ref, NB).astype(bf16)
    u = u_ref[...].astype(f32)
    y2 = _dot(hcr, cr_ref[...]) - _dot(hci, ci_ref[...]) + d_ref[...] * u
    yg = _gelu(y2)
    s = jax.nn.sigmoid(_dot(yg.astype(bf16), wg_ref[...]) + bg_ref[...])
    return hcr, hci, u, y2, yg, s


def _s5_out_specs(B, S, NB, NS, SW, tm):
    sspec = pl.BlockSpec((None, NB, tm, LANES), lambda b, i: (b, 0, i, 0))
    full = lambda r, c: pl.BlockSpec((r, c), lambda b, i: (0, 0))
    return sspec, [sspec, sspec, pl.BlockSpec((None, tm, SW), lambda b, i: (b, i, 0)), full(NS, SW), full(NS, SW),
                   full(1, SW), full(SW, SW), full(1, SW)]


def _s5_out(hr, hi, p3, cbr, cbi, dsk, wglu, bglu, tm):
    B, NB, S, _ = hr.shape
    NS, SW = cbr.shape

    def body(hr_ref, hi_ref, u_ref, cr_ref, ci_ref, d_ref, wg_ref, bg_ref, o_ref):
        _, _, _, _, yg, s = _s5_out_parts(hr_ref, hi_ref, u_ref, cr_ref, ci_ref, d_ref, wg_ref, bg_ref, NB)
        o_ref[...] = (yg * s).astype(bf16)

    _, in_specs = _s5_out_specs(B, S, NB, NS, SW, tm)
    return pl.pallas_call(
        body, name="s5_out", grid=(B, S // tm), in_specs=in_specs,
        out_specs=pl.BlockSpec((None, tm, SW), lambda b, i: (b, i, 0)),
        out_shape=jax.ShapeDtypeStruct((B, S, SW), bf16),
        compiler_params=_params(("parallel", "parallel")))(hr, hi, p3, cbr, cbi, dsk, wglu, bglu)


def _s5_out_bwd(hr, hi, p3, dmix3, cbr, cbi, dsk, wglu, bglu, tm):
    B, NB, S, _ = hr.shape
    NS, SW = cbr.shape

    def body(hr_ref, hi_ref, u_ref, cr_ref, ci_ref, d_ref, wg_ref, bg_ref, da_ref,
             dhr_ref, dhi_ref, du_ref, dcr_ref, dci_ref, dd_ref, dwg_ref, dbg_ref):
        hcr, hci, u, y2, yg, s = _s5_out_parts(hr_ref, hi_ref, u_ref, cr_ref, ci_ref, d_ref, wg_ref, bg_ref, NB)
        da = da_ref[...].astype(f32)
        dz = da * yg * s * (1.0 - s)
        dzb = dz.astype(bf16)
        dyg = da * s + _dg(dzb, wg_ref[...], NT)
        dy2 = dyg * _gelu_grad(y2)
        dyb = dy2.astype(bf16)

        @pl.when((pl.program_id(0) == 0) & (pl.program_id(1) == 0))
        def _():
            for r in (dcr_ref, dci_ref, dd_ref, dwg_ref, dbg_ref):
                r[...] = jnp.zeros_like(r)
        dwg_ref[...] += _dg(yg.astype(bf16), dzb, TN)
        dbg_ref[...] += jnp.sum(dz, axis=0, keepdims=True)
        dd_ref[...] += jnp.sum(dy2 * u, axis=0, keepdims=True)
        dcr_ref[...] += _dg(hcr, dyb, TN)
        dci_ref[...] -= _dg(hci, dyb, TN)
        du_ref[...] = dy2 * d_ref[...]
        dhr = _dg(dyb, cr_ref[...], NT)
        dhi = _dg(dyb, ci_ref[...], NT)
        for j in range(NB):
            dhr_ref[j] = dhr[:, j * LANES:(j + 1) * LANES]
            dhi_ref[j] = -dhi[:, j * LANES:(j + 1) * LANES]

    sspec, in_specs = _s5_out_specs(B, S, NB, NS, SW, tm)
    in_specs = in_specs + [pl.BlockSpec((None, tm, SW), lambda b, i: (b, i, 0))]
    full = lambda r, c: pl.BlockSpec((r, c), lambda b, i: (0, 0))
    slab = jax.ShapeDtypeStruct((B, NB, S, LANES), f32)
    mat = lambda r, c: jax.ShapeDtypeStruct((r, c), f32)
    return pl.pallas_call(
        body, name="s5_out_bwd", grid=(B, S // tm), in_specs=in_specs,
        out_specs=[sspec, sspec, pl.BlockSpec((None, tm, SW), lambda b, i: (b, i, 0)), full(NS, SW), full(NS, SW),
                   full(1, SW), full(SW, SW), full(1, SW)],
        out_shape=[slab, slab, jax.ShapeDtypeStruct((B, S, SW), f32), mat(NS, SW), mat(NS, SW), mat(1, SW),
                   mat(SW, SW), mat(1, SW)],
        compiler_params=_params(("arbitrary", "arbitrary")))(hr, hi, p3, cbr, cbi, dsk, wglu, bglu, dmix3)


def _s5_in_bwd(gr, gi, p3, bbd, du_skip, duv3, tm):
    B, NB, S, _ = gr.shape
    SW, NS2 = bbd.shape
    PW = SW + duv3.shape[2]

    def body(gr_ref, gi_ref, u_ref, b_ref, ds_ref, duv_ref, dp_ref, db_ref):
        g = jnp.concatenate([_slab_cat(gr_ref, NB), _slab_cat(gi_ref, NB)], axis=1).astype(bf16)
        du = _dg(g, b_ref[...], NT) + ds_ref[...]
        dp_ref[:, :SW] = du.astype(bf16)
        dp_ref[:, SW:] = duv_ref[...]

        @pl.when((pl.program_id(0) == 0) & (pl.program_id(1) == 0))
        def _():
            db_ref[...] = jnp.zeros_like(db_ref)
        db_ref[...] += _dg(u_ref[...], g, TN)

    sspec = pl.BlockSpec((None, NB, tm, LANES), lambda b, i: (b, 0, i, 0))
    row = lambda c: pl.BlockSpec((None, tm, c), lambda b, i: (b, i, 0))
    return pl.pallas_call(
        body, name="s5_in_bwd", grid=(B, S // tm),
        in_specs=[sspec, sspec, row(SW), pl.BlockSpec((SW, NS2), lambda b, i: (0, 0)), row(SW), row(PW - SW)],
        out_specs=[row(PW), pl.BlockSpec((SW, NS2), lambda b, i: (0, 0))],
        out_shape=[jax.ShapeDtypeStruct((B, S, PW), bf16), jax.ShapeDtypeStruct((SW, NS2), f32)],
        compiler_params=_params(("arbitrary", "arbitrary")))(gr, gi, p3, bbd, du_skip, duv3)


BIG = ['ev_w_in', 'ev_w_out', 'od_w_in', 'od_w_out', 'ffn_w_up', 'ffn_w_down']
ANY = pl.BlockSpec(memory_space=pl.ANY)


def _rtile(rows, mult):
    best = None
    for d in range(mult, min(rows, 512) + 1, mult):
        if rows % d == 0:
            best = d
    assert best is not None, (rows, mult)
    return best


def _pair_sum(name, g, recv, c_idx, out_dtype):
    NCH, R, W = g.shape
    HALF_W = W // 2
    tr = _rtile(R, 16)

    def body(c_ref, a_ref, b_ref, o_ref):
        o_ref[...] = (a_ref[...] + b_ref[...]).astype(out_dtype)

    return pl.pallas_call(
        body, name=name,
        grid_spec=pltpu.PrefetchScalarGridSpec(
            num_scalar_prefetch=1, grid=(NCH, R // tr),
            in_specs=[pl.BlockSpec((None, tr, HALF_W), lambda j, i, c: (j, i, c[0])),
                      pl.BlockSpec((None, tr, HALF_W), lambda j, i, c: (j, i, 0))],
            out_specs=pl.BlockSpec((None, tr, HALF_W), lambda j, i, c: (j, i, 0))),
        out_shape=jax.ShapeDtypeStruct((NCH, R, HALF_W), out_dtype),
        compiler_params=_params(("parallel", "parallel")))(c_idx, g, recv)


def _chip_sum(name, r3):
    NCH, R, Wh = r3.shape
    tr = _rtile(R, 16)

    def body(a_ref, o_ref):
        o_ref[...] = ((a_ref[0].astype(f32) + a_ref[1].astype(f32)) + a_ref[2].astype(f32)) + a_ref[3].astype(f32)

    return pl.pallas_call(
        body, name=name, grid=(R // tr,),
        in_specs=[pl.BlockSpec((NCH, tr, Wh), lambda i: (0, i, 0))],
        out_specs=pl.BlockSpec((tr, Wh), lambda i: (i, 0)),
        out_shape=jax.ShapeDtypeStruct((R, Wh), f32),
        compiler_params=_params(("parallel",)))(r3)


def _adam_math(gg, w, m, v):
    nm = ADAM_B1 * m + (1.0 - ADAM_B1) * gg
    nv = ADAM_B2 * v + (1.0 - ADAM_B2) * jnp.square(gg)
    m_hat = nm / (1.0 - ADAM_B1 ** ADAM_STEP)
    v_hat = nv / (1.0 - ADAM_B2 ** ADAM_STEP)
    return -ADAM_LR * (m_hat / (jnp.sqrt(v_hat) + ADAM_EPS) + ADAM_WD * w), nm, nv


def _adamw_rows(name, g, w, m, v, lead, prev=None):
    L, R, W = w.shape
    tr = _rtile(R, SUBLANES)

    def body(g_ref, w_ref, m_ref, v_ref, *rest):
        d_ref, nm_ref, nv_ref = rest[-3:]
        d, nm, nv = _adam_math(g_ref[...], w_ref[...], m_ref[...], v_ref[...])
        d_ref[...] = d
        nm_ref[...] = nm
        nv_ref[...] = nv

    wspec = pl.BlockSpec((None, tr, W), lambda i: (lead, i, 0))
    in_specs = [pl.BlockSpec((tr, W), lambda i: (i, 0)), wspec, wspec, wspec]
    args, aliases = [g, w, m, v], {}
    if prev is not None:
        in_specs += [ANY] * 3
        args += list(prev)
        aliases = {4: 0, 5: 1, 6: 2}
    s = jax.ShapeDtypeStruct((L, R, W), f32)
    return pl.pallas_call(
        body, name=name, grid=(R // tr,), in_specs=in_specs, out_specs=[wspec] * 3, out_shape=[s, s, s],
        input_output_aliases=aliases, compiler_params=_params(("parallel",)))(*args)


def _adamw_cols(name, gt, w, m, v, lead, prev=None):
    L, Dm, n = w.shape
    bn = LANES

    def body(g_ref, w_ref, m_ref, v_ref, *rest):
        go_ref, d_ref, nm_ref, nv_ref = rest[-4:]
        gg = g_ref[...].T
        d, nm, nv = _adam_math(gg, w_ref[...], m_ref[...], v_ref[...])
        go_ref[...] = gg
        d_ref[...] = d
        nm_ref[...] = nm
        nv_ref[...] = nv

    wspec = pl.BlockSpec((None, Dm, bn), lambda i: (lead, 0, i))
    in_specs = [pl.BlockSpec((bn, Dm), lambda i: (i, 0)), wspec, wspec, wspec]
    args, aliases = [gt, w, m, v], {}
    if prev is not None:
        in_specs += [ANY] * 4
        args += list(prev)
        aliases = {4: 0, 5: 1, 6: 2, 7: 3}
    s = jax.ShapeDtypeStruct((L, Dm, n), f32)
    return pl.pallas_call(
        body, name=name, grid=(n // bn,), in_specs=in_specs, out_specs=[wspec] * 4, out_shape=[s, s, s, s],
        input_output_aliases=aliases, compiler_params=_params(("parallel",)))(*args)


def _place():
    x, y, c = lax.axis_index("x"), lax.axis_index("y"), lax.axis_index("c")
    return x, y, c, [(1 - x, y), (x, 1 - y), (1 - x, 1 - y)]


def _gathered_shape(sh, kind):
    if kind == "rows":
        return sh[:-2] + (N_CHIPS * sh[-2], sh[-1])
    if kind == "cols":
        return sh[:-1] + (N_CHIPS * sh[-1],)
    return (N_CHIPS,) + sh


def _comm_gather(shards, kinds):
    n = len(shards)
    shapes = [s.shape for s in shards]

    def window(ref, a, k, h=None):
        sh, kind = shapes[a], kinds[a]
        r = sh[-2]
        start, size = (0, r) if h is None else (h * (r // 2), r // 2)
        lead = (slice(None),) * (len(sh) - 2)
        if kind == "rows":
            return ref.at[lead + (pl.ds(k * r + start, size), slice(None))]
        if kind == "cols":
            return ref.at[lead + (pl.ds(start, size), pl.ds(pl.multiple_of(k * sh[-1], LANES), sh[-1]))]
        return ref.at[(k,) + lead + (pl.ds(start, size), slice(None))]

    def body(*refs):
        s_refs, o_refs = refs[:n], refs[n:2 * n]
        send_sems, recv_sems, local_sems = refs[2 * n:]
        x, y, c, chips = _place()
        me, sibling = (x, y, c), (x, y, 1 - c)
        k = 2 * x + y

        def copy(a, j, kk, hh, to, src=None):
            dst = window(o_refs[a], a, kk, hh)
            return pltpu.make_async_remote_copy(
                src_ref=dst if src is None else src, dst_ref=dst, send_sem=send_sems.at[6 * a + j],
                recv_sem=recv_sems.at[6 * a + j], device_id=to, device_id_type=MESH)

        mine = [pltpu.make_async_copy(s_refs[a], window(o_refs[a], a, k), local_sems.at[a]) for a in range(n)]
        for cp in mine:
            cp.start()
        first = []
        for a in range(n):
            r = shapes[a][-2]
            lead = (slice(None),) * (len(shapes[a]) - 2)
            src = s_refs[a].at[lead + (pl.ds(c * (r // 2), r // 2), slice(None))]
            first += [copy(a, j, k, c, (*chip, c), src=src) for j, chip in enumerate(chips)]
        for cp in first:
            cp.start()
        passed = []
        for j, (cx, cy) in enumerate(chips):
            for a in range(n):
                copy(a, j, 2 * cx + cy, c, me).wait_recv()
                fwd = copy(a, 3 + j, 2 * cx + cy, c, sibling)
                fwd.start()
                passed.append(fwd)
        for j, (cx, cy) in enumerate(chips):
            for a in range(n):
                copy(a, 3 + j, 2 * cx + cy, 1 - c, me).wait_recv()
        for cp in first + passed:
            cp.wait_send()
        for cp in mine:
            cp.wait()

    return pl.pallas_call(
        body, name="comm_gather", in_specs=[ANY] * n, out_specs=[ANY] * n,
        out_shape=[jax.ShapeDtypeStruct(_gathered_shape(s.shape, kd), s.dtype) for s, kd in zip(shards, kinds)],
        scratch_shapes=[pltpu.SemaphoreType.DMA((6 * n,)), pltpu.SemaphoreType.DMA((6 * n,)),
                        pltpu.SemaphoreType.DMA((n,))])(*shards)


def _comm_pair_swap(gs):
    n = len(gs)

    def body(*refs):
        g_refs, o_refs, send_sems, recv_sems = refs[:n], refs[n:2 * n], refs[2 * n], refs[2 * n + 1]
        x, y, c, _ = _place()
        half = [g.shape[2] // 2 for g in gs]
        cps = [pltpu.make_async_remote_copy(
            src_ref=g_refs[a].at[:, :, pl.ds(pl.multiple_of((1 - c) * half[a], LANES), half[a])], dst_ref=o_refs[a], send_sem=send_sems.at[a],
            recv_sem=recv_sems.at[a], device_id=(x, y, 1 - c), device_id_type=MESH) for a in range(n)]
        for cp in cps:
            cp.start()
        for cp in cps:
            cp.wait()

    return pl.pallas_call(
        body, name="comm_pair_swap", in_specs=[ANY] * n, out_specs=[ANY] * n,
        out_shape=[jax.ShapeDtypeStruct(g.shape[:2] + (g.shape[2] // 2,), g.dtype) for g in gs],
        scratch_shapes=[pltpu.SemaphoreType.DMA((n,)), pltpu.SemaphoreType.DMA((n,))])(*gs)


def _comm_chip_exchange(hs):
    n = len(hs)

    def body(*refs):
        h_refs, o_refs = refs[:n], refs[n:2 * n]
        send_sems, recv_sems, local_sems = refs[2 * n:]
        x, y, c, chips = _place()
        k = 2 * x + y
        mine = [pltpu.make_async_copy(h_refs[a].at[k], o_refs[a].at[k], local_sems.at[a]) for a in range(n)]
        for cp in mine:
            cp.start()

        def copy(a, j, src_slot, dst_slot):
            cx, cy = chips[j]
            return pltpu.make_async_remote_copy(
                src_ref=h_refs[a].at[src_slot], dst_ref=o_refs[a].at[dst_slot], send_sem=send_sems.at[3 * a + j],
                recv_sem=recv_sems.at[3 * a + j], device_id=(cx, cy, c), device_id_type=MESH)

        sends = [copy(a, j, 2 * cx + cy, k) for a in range(n) for j, (cx, cy) in enumerate(chips)]
        for cp in sends:
            cp.start()
        for a in range(n):
            for j, (cx, cy) in enumerate(chips):
                copy(a, j, k, 2 * cx + cy).wait_recv()
        for cp in sends:
            cp.wait_send()
        for cp in mine:
            cp.wait()

    return pl.pallas_call(
        body, name="comm_chip_exchange", in_specs=[ANY] * n, out_specs=[ANY] * n,
        out_shape=[jax.ShapeDtypeStruct(h.shape, h.dtype) for h in hs],
        scratch_shapes=[pltpu.SemaphoreType.DMA((3 * n,)), pltpu.SemaphoreType.DMA((3 * n,)),
                        pltpu.SemaphoreType.DMA((n,))])(*hs)


def _comm_pair_share(gs):
    n = len(gs)

    def body(*refs):
        g_refs, o_refs = refs[:n], refs[n:2 * n]
        send_sems, recv_sems, local_sems = refs[2 * n:]
        x, y, c, _ = _place()
        half = [g.shape[1] for g in gs]
        own = [pl.ds(pl.multiple_of(c * hw, LANES), hw) for hw in half]
        other = [pl.ds(pl.multiple_of((1 - c) * hw, LANES), hw) for hw in half]
        mine = [pltpu.make_async_copy(g_refs[a], o_refs[a].at[:, own[a]], local_sems.at[a]) for a in range(n)]
        for cp in mine:
            cp.start()

        def copy(a, cols):
            return pltpu.make_async_remote_copy(
                src_ref=g_refs[a], dst_ref=o_refs[a].at[:, cols], send_sem=send_sems.at[a], recv_sem=recv_sems.at[a],
                device_id=(x, y, 1 - c), device_id_type=MESH)

        sends = [copy(a, own[a]) for a in range(n)]
        for cp in sends:
            cp.start()
        for a in range(n):
            copy(a, other[a]).wait_recv()
        for cp in sends:
            cp.wait_send()
        for cp in mine:
            cp.wait()

    return pl.pallas_call(
        body, name="comm_pair_share", in_specs=[ANY] * n, out_specs=[ANY] * n,
        out_shape=[jax.ShapeDtypeStruct((g.shape[0], 2 * g.shape[1]), g.dtype) for g in gs],
        scratch_shapes=[pltpu.SemaphoreType.DMA((n,)), pltpu.SemaphoreType.DMA((n,)),
                        pltpu.SemaphoreType.DMA((n,))])(*gs)


def _pad_rows(flat, unit):
    n = flat.shape[-1]
    pad = (-n) % unit
    if pad:
        flat = jnp.pad(flat, [(0, 0)] * (flat.ndim - 1) + [(0, pad)])
    return flat


def _split_chips(full, axis):
    sh = full.shape
    t = full.reshape(sh[:axis] + (N_CHIPS, sh[axis] // N_CHIPS) + sh[axis + 1:])
    return jnp.moveaxis(t, axis, 0).reshape(N_CHIPS, -1)


def _join_chips(stack, shard_shape, axis):
    t = jnp.moveaxis(stack.reshape((N_CHIPS,) + tuple(shard_shape)), 0, axis)
    sh = t.shape
    return t.reshape(sh[:axis] + (sh[axis] * sh[axis + 1],) + sh[axis + 2:])


def _block_diag(blocks):
    G, r, c = blocks.shape
    eye = jnp.eye(G, dtype=blocks.dtype)
    return (blocks[:, :, None, :] * eye[:, None, :, None]).reshape(G * r, G * c)


def _diag_blocks(m, G):
    r, c = m.shape[0] // G, m.shape[1] // G
    idx = jnp.arange(G)
    return m.reshape(G, r, G, c)[idx, :, idx, :]


def _gather_weights(w):
    kinds = {'ev_w_in': "chip", 'ev_w_out': "rows", 's5_w_glu': "rows", 'od_w_in': "cols", 'od_w_out': "rows",
             'ffn_w_up': "cols", 'ffn_w_down': "rows"}
    conv = jnp.concatenate([w[n].reshape(-1) for n in GATHER_F32])
    conv = _pad_rows(conv, 2 * SUBLANES * LANES).reshape(-1, LANES)
    shards = [w[n].astype(bf16) if w[n].shape[0] > 1 else w[n][0].astype(bf16) for n in GATHER_BF16] + [conv]
    got = _comm_gather(shards, [kinds[n] for n in GATHER_BF16] + ["chip"])
    out = {}
    for n, g in zip(GATHER_BF16, got[:-1]):
        if kinds[n] == "chip":
            g = jnp.swapaxes(g, 0, 1).reshape(g.shape[1], -1)
        out[n] = g if w[n].shape[0] > 1 else g[None]
    allc, off = got[-1].reshape(N_CHIPS, -1), 0
    for n in GATHER_F32:
        out[n] = _join_chips(allc[:, off:off + w[n].size], w[n].shape, SHARD_AXIS[n])
        off += w[n].size
    return out


def kernel(x, mix_norm_g, ffn_norm_g, final_norm_g, ev_w_in, ev_w_out, s5_lam_re, s5_lam_im, s5_log_dt, s5_b_re, s5_b_im, s5_c_re, s5_c_im, s5_d, s5_w_glu, s5_b_glu, gm_w_s, gm_b_s, gm_v_g, od_w_in, od_conv_w, od_conv_b, od_w_out, ffn_w_up, ffn_conv_w, ffn_conv_b, ffn_w_down, loss_target, m_mix_norm_g, m_ffn_norm_g, m_final_norm_g, m_ev_w_in, m_ev_w_out, m_s5_lam_re, m_s5_lam_im, m_s5_log_dt, m_s5_b_re, m_s5_b_im, m_s5_c_re, m_s5_c_im, m_s5_d, m_s5_w_glu, m_s5_b_glu, m_gm_w_s, m_gm_b_s, m_gm_v_g, m_od_w_in, m_od_conv_w, m_od_conv_b, m_od_w_out, m_ffn_w_up, m_ffn_conv_w, m_ffn_conv_b, m_ffn_w_down, v_mix_norm_g, v_ffn_norm_g, v_final_norm_g, v_ev_w_in, v_ev_w_out, v_s5_lam_re, v_s5_lam_im, v_s5_log_dt, v_s5_b_re, v_s5_b_im, v_s5_c_re, v_s5_c_im, v_s5_d, v_s5_w_glu, v_s5_b_glu, v_gm_w_s, v_gm_b_s, v_gm_v_g, v_od_w_in, v_od_conv_w, v_od_conv_b, v_od_w_out, v_ffn_w_up, v_ffn_conv_w, v_ffn_conv_b, v_ffn_w_down):
    loc = dict(locals())
    w = {n: loc[n] for n in WEIGHTS}
    mom = {n: loc["m_" + n] for n in WEIGHTS}
    var = {n: loc["v_" + n] for n in WEIGHTS}

    B, S, D = x.shape
    T = B * S
    SW = s5_d.shape[1]
    G = SW // SSM_GROUP
    NS = G * SSM_STATE
    NB = NS // LANES
    tm = min(256, S)
    tt = min(512, T)
    full = _gather_weights(w)

    h0 = x.reshape(T, D)
    w_ev_in = full['ev_w_in'][0]
    w_ev_out = full['ev_w_out'][0]
    y0, p0 = _norm_mm("ev_in", h0, mix_norm_g[0], w_ev_in, tm)
    PW = p0.shape[1]
    p03 = p0.reshape(B, S, PW)
    lr, li, ldt = s5_lam_re[0], s5_lam_im[0], s5_log_dt[0].reshape(G, 1)
    ar, ai, zr, zi = _s5_prep(lr, li, ldt)
    bre = _block_diag(jnp.swapaxes(s5_b_re[0], 1, 2))
    bim = _block_diag(jnp.swapaxes(s5_b_im[0], 1, 2))
    cbr = _block_diag(jnp.swapaxes(s5_c_re[0], 1, 2)).astype(bf16)
    cbi = _block_diag(jnp.swapaxes(s5_c_im[0], 1, 2)).astype(bf16)
    zr_row, zi_row = zr.reshape(1, NS), zi.reshape(1, NS)
    bbd = _s5_bbd(zr_row, zi_row, bre, bim)
    ar_s, ai_s = ar.reshape(NB, 1, LANES), ai.reshape(NB, 1, LANES)
    xr, xi = _s5_in(p03, bbd, SW, tm)
    hr, hi = _s5_scan("s5_scan", xr, xi, ar_s, ai_s, False)
    w_glu = full['s5_w_glu'][0]
    dsk, bglu = s5_d.reshape(1, SW), s5_b_glu.reshape(1, SW)
    a_out = _s5_out(hr, hi, p03, cbr, cbi, dsk, w_glu, bglu, tm)
    ws, bst, gv = gm_w_s[0], gm_b_s[0].T, gm_v_g.reshape(1, -1)
    mixcat = _gmlp(p0, a_out.reshape(T, SW), ws, bst, gv, SW)
    h1 = _mm_resid("ev_out", mixcat, w_ev_out, h0, tm)

    def ffn_fwd(l, h):
        z, up = _norm_mm(f"ffn_up{l}", h, ffn_norm_g[l], full['ffn_w_up'][l], tm)
        up3 = up.reshape(B, S, -1)
        act = _ffn_act(f"ffn_act{l}", up3, full['ffn_conv_w'][l], ffn_conv_b[l].reshape(1, -1))
        hn = _mm_resid(f"ffn_down{l}", act.reshape(T, -1), full['ffn_w_down'][l], h, tm)
        return hn, (z, up3, act)

    h2, ffn0 = ffn_fwd(0, h1)
    w_od_in, w_od_out = full['od_w_in'][0], full['od_w_out'][0]
    od_cw, od_cb = full['od_conv_w'][0], full['od_conv_b']
    y1, p1 = _norm_mm("od_in", h2, mix_norm_g[1], w_od_in, tm)
    p13 = p1.reshape(B, S, -1)
    sc = _od_act(p13, od_cw, od_cb)
    h3 = _mm_resid("od_out", sc.reshape(T, D), w_od_out, h2, tm)
    h4, ffn1 = ffn_fwd(1, h3)

    dh4, loss_part, d_final_g = _final_loss(h4, final_norm_g, loss_target.reshape(T, D), tm)
    loss = lax.psum(loss_part[0, 0], ("x", "y", "c"))

    grads = {}

    def ffn_bwd(l, dh, h_in, saved):
        z, up3, act = saved
        w_down, w_up = full['ffn_w_down'][l], full['ffn_w_up'][l]
        da = _mm_nt(f"ffn_down_bwd{l}", dh, w_down, tm)
        g_down = _mm_tn(f"ffn_down_dw{l}", act.reshape(T, -1), dh, tt)
        dup3, dcw, dcb = _ffn_act_bwd(f"ffn_act_bwd{l}", up3, da.reshape(B, S, -1), full['ffn_conv_w'][l],
                                      ffn_conv_b[l].reshape(1, -1))
        dup = dup3.reshape(T, -1)
        g_up = _mm_tn(f"ffn_up_dw{l}", dup, z, tt)
        dh_new, dg = _mm_nt_normbwd(f"ffn_up_bwd{l}", dup, w_up, h_in, ffn_norm_g[l], dh, tm)
        return dh_new, g_down, g_up, dcw, dcb[0], dg[0]

    dh3, gd1, gu1, gcw1, gcb1, gng1 = ffn_bwd(1, dh4, h3, ffn1)
    dsc = _mm_nt("od_out_bwd", dh3, w_od_out, tm)
    big = {'od_w_out': _mm_tn("od_out_dw", sc.reshape(T, D), dh3, tt)}
    dp13, d_od_cw, d_od_cb = _od_act_bwd(p13, dsc.reshape(B, S, D), od_cw, od_cb)
    dp1 = dp13.reshape(T, -1)
    big['od_w_in'] = _mm_tn("od_in_dw", dp1, y1, tt)
    grads['od_conv_w'] = d_od_cw[0][None]
    grads['od_conv_b'] = d_od_cb[0]
    dh2, gmix1 = _mm_nt_normbwd("od_in_bwd", dp1, w_od_in, h2, mix_norm_g[1], dh3, tm)
    dh1, gd0, gu0, gcw0, gcb0, gng0 = ffn_bwd(0, dh2, h1, ffn0)
    big.update({'ffn_w_down0': gd0, 'ffn_w_down1': gd1, 'ffn_w_up0': gu0, 'ffn_w_up1': gu1})
    grads['ffn_conv_w'] = jnp.stack([gcw0, gcw1])
    grads['ffn_conv_b'] = jnp.stack([gcb0, gcb1])
    grads['ffn_norm_g'] = jnp.stack([gng0, gng1])
    grads['final_norm_g'] = d_final_g[0]

    dmix = _mm_nt("ev_out_bwd", dh1, w_ev_out, tm)
    big['ev_w_out'] = _mm_tn("ev_out_dw", mixcat, dh1, tt)
    duv, d_ws, d_bs, d_gv = _gmlp_bwd(p0, dmix, ws, bst, gv, SW)
    grads['gm_w_s'] = d_ws[None]
    grads['gm_b_s'] = d_bs[:, :, 0][None]
    grads['gm_v_g'] = d_gv
    dhr, dhi, du_skip, d_cbr, d_cbi, d_dsk, d_wglu, d_bglu = _s5_out_bwd(
        hr, hi, p03, dmix.reshape(B, S, D), cbr, cbi, dsk, w_glu, bglu, tm)
    grads['s5_c_re'] = jnp.swapaxes(_diag_blocks(d_cbr, G), 1, 2)[None]
    grads['s5_c_im'] = jnp.swapaxes(_diag_blocks(d_cbi, G), 1, 2)[None]
    grads['s5_d'] = d_dsk
    grads['s5_w_glu'] = d_wglu[None]
    grads['s5_b_glu'] = d_bglu
    gr, gi, dar, dai = _s5_scan("s5_rscan", dhr, dhi, ar_s, ai_s, True, hr, hi)
    dp03, d_bbd = _s5_in_bwd(gr, gi, p03, bbd, du_skip, duv.reshape(B, S, -1), tm)
    d_bre, d_bim, d_zr, d_zi = _s5_bbd_bwd(d_bbd, zr_row, zi_row, bre, bim)
    grads['s5_b_re'] = jnp.swapaxes(_diag_blocks(d_bre, G), 1, 2)[None]
    grads['s5_b_im'] = jnp.swapaxes(_diag_blocks(d_bim, G), 1, 2)[None]
    shp = (-1, G, SSM_STATE)
    d_lr, d_li, d_ldt = _s5_prep_bwd(lr, li, ldt, dar.reshape(shp), dai.reshape(shp), d_zr.reshape(shp),
                                     d_zi.reshape(shp))
    grads['s5_lam_re'] = d_lr[None]
    grads['s5_lam_im'] = d_li[None]
    grads['s5_log_dt'] = d_ldt.reshape(1, G)
    dp0 = dp03.reshape(T, PW)
    big['ev_w_in'] = _mm_tn("ev_in_dw", dp0, y0, tt)
    grad_x, gmix0 = _mm_nt_normbwd("ev_in_bwd", dp0, w_ev_in, h0, mix_norm_g[0], dh1, tm)
    grads['mix_norm_g'] = jnp.concatenate([gmix0, gmix1], axis=0)

    small = [n for n in WEIGHTS if n not in BIG]
    segs = []
    for n in small:
        gfull = grads[n].astype(f32)
        if n in SHARD_AXIS:
            segs.append(_split_chips(gfull, SHARD_AXIS[n]))
        else:
            segs.append(jnp.broadcast_to(gfull.reshape(1, -1), (N_CHIPS, gfull.size)))
    unit = 2 * SUBLANES * D
    gsmall = _pad_rows(jnp.concatenate(segs, axis=1), unit).reshape(N_CHIPS, -1, D)
    names = list(big) + ["small"]
    parts = [big[n].reshape(N_CHIPS, -1, D) for n in big] + [gsmall]
    c_idx = lax.axis_index("c").astype(jnp.int32).reshape(1)
    recv = _comm_pair_swap(parts)
    hsum = [_pair_sum(f"pair_sum_{n}", g, r, c_idx, f32 if n == "small" else bf16)
            for n, g, r in zip(names, parts, recv)]
    r3 = _comm_chip_exchange(hsum)
    gsum = [_chip_sum(f"chip_sum_{n}", r) for n, r in zip(names, r3)]
    gfin = dict(zip(names, _comm_pair_share(gsum)))

    out_g, out_d, out_m, out_v = {}, {}, {}, {}

    def put(n, res):
        out_d[n], out_m[n], out_v[n] = res

    gnat = gfin['ev_w_in'].T
    out_g['ev_w_in'] = gnat[None]
    put('ev_w_in', _adamw_rows("adamw_ev_w_in", gnat, ev_w_in, m_ev_w_in, v_ev_w_in, 0))
    for n in ('ev_w_out', 'od_w_out'):
        out_g[n] = gfin[n][None]
        put(n, _adamw_rows("adamw_" + n, gfin[n], w[n], mom[n], var[n], 0))
    res = _adamw_cols("adamw_od_w_in", gfin['od_w_in'], od_w_in, m_od_w_in, v_od_w_in, 0)
    out_g['od_w_in'] = res[0]
    put('od_w_in', res[1:])
    res = _adamw_cols("adamw_ffn_w_up0", gfin['ffn_w_up0'], ffn_w_up, m_ffn_w_up, v_ffn_w_up, 0)
    res = _adamw_cols("adamw_ffn_w_up1", gfin['ffn_w_up1'], ffn_w_up, m_ffn_w_up, v_ffn_w_up, 1, prev=res)
    out_g['ffn_w_up'] = res[0]
    put('ffn_w_up', res[1:])
    res = _adamw_rows("adamw_ffn_w_down0", gfin['ffn_w_down0'], ffn_w_down, m_ffn_w_down, v_ffn_w_down, 0)
    res = _adamw_rows("adamw_ffn_w_down1", gfin['ffn_w_down1'], ffn_w_down, m_ffn_w_down, v_ffn_w_down, 1, prev=res)
    out_g['ffn_w_down'] = jnp.stack([gfin['ffn_w_down0'], gfin['ffn_w_down1']])
    put('ffn_w_down', res)

    def pack_local(d):
        flat = _pad_rows(jnp.concatenate([d[n].astype(f32).reshape(-1) for n in small]), unit)
        return flat.reshape(1, -1, D)

    res = _adamw_rows("adamw_small", gfin['small'], pack_local(w), pack_local(mom), pack_local(var), 0)
    for dst, p in zip((out_g, out_d, out_m, out_v), (gfin['small'],) + tuple(res)):
        flat, off = p.reshape(-1), 0
        for n in small:
            dst[n] = flat[off:off + w[n].size].reshape(w[n].shape)
            off += w[n].size

    return (loss, grad_x.reshape(B, S, D), *[out_g[n] for n in WEIGHTS], *[out_d[n] for n in WEIGHTS],
            *[out_m[n] for n in WEIGHTS], *[out_v[n] for n in WEIGHTS])
```

```python
import functools
import math

import jax
import jax.numpy as jnp
from jax import lax
from jax.experimental import pallas as pl
from jax.experimental.pallas import tpu as pltpu

f32 = jnp.float32
bf16 = jnp.bfloat16
MESH = pl.DeviceIdType.MESH

SSM_GROUP = 16
SSM_STATE = 64
GMLP_HEAD = 128
CHUNK = 128
EPS = 1e-6
LAMBDA_RE_MAX = -1e-4
ADAM_LR, ADAM_B1, ADAM_B2, ADAM_EPS, ADAM_WD, ADAM_STEP = 0.001, 0.9, 0.999, 1e-08, 0.01, 10

LANES = 128
SUBLANES = 8
NSUB = 32
HALO = 16
VMEM_LIMIT = 56 * 1024 * 1024
N_CHIPS = 4

WEIGHTS = ['mix_norm_g', 'ffn_norm_g', 'final_norm_g', 'ev_w_in', 'ev_w_out', 's5_lam_re', 's5_lam_im', 's5_log_dt',
           's5_b_re', 's5_b_im', 's5_c_re', 's5_c_im', 's5_d', 's5_w_glu', 's5_b_glu', 'gm_w_s', 'gm_b_s', 'gm_v_g',
           'od_w_in', 'od_conv_w', 'od_conv_b', 'od_w_out', 'ffn_w_up', 'ffn_conv_w', 'ffn_conv_b', 'ffn_w_down']
SHARD_AXIS = {'ev_w_in': 2, 'ev_w_out': 1, 's5_w_glu': 1, 'od_w_in': 2, 'od_conv_w': 2, 'od_conv_b': 1, 'od_w_out': 1,
              'ffn_w_up': 2, 'ffn_conv_w': 2, 'ffn_w_down': 1}
GATHER_BF16 = ['ev_w_in', 'ev_w_out', 's5_w_glu', 'od_w_in', 'od_w_out', 'ffn_w_up', 'ffn_w_down']
GATHER_F32 = ['od_conv_w', 'od_conv_b', 'ffn_conv_w']

_GELU_K0 = math.sqrt(2.0 / math.pi)
_GELU_K1 = 0.044715
NT = (((1,), (1,)), ((), ()))
TN = (((0,), (0,)), ((), ()))


def _pick(n, cap):
    if n <= cap:
        return n
    best = None
    for d in range(LANES, cap + 1, LANES):
        if n % d == 0:
            best = d
    assert best is not None, (n, cap)
    return best


def _params(sem=None):
    return pltpu.CompilerParams(dimension_semantics=sem, vmem_limit_bytes=VMEM_LIMIT)


def _gelu(x):
    return 0.5 * x * (1.0 + jnp.tanh(_GELU_K0 * (x + _GELU_K1 * x * x * x)))


def _gelu_grad(x):
    t = jnp.tanh(_GELU_K0 * (x + _GELU_K1 * x * x * x))
    return 0.5 * (1.0 + t) + 0.5 * x * (1.0 - t * t) * _GELU_K0 * (1.0 + 3.0 * _GELU_K1 * x * x)


def _rms_stats(x):
    r = lax.rsqrt(jnp.mean(x * x, axis=-1, keepdims=True) + EPS)
    return x * r, r


def _rms_bwd(dy, xh, r, g):
    dxh = dy * g
    dx = r * (dxh - xh * jnp.mean(dxh * xh, axis=-1, keepdims=True))
    return dx, jnp.sum(dy * xh, axis=0, keepdims=True)


def _dot(a, b):
    return jnp.dot(a, b, preferred_element_type=f32)


def _dg(a, b, dims):
    return lax.dot_general(a, b, dims, preferred_element_type=f32)


def _row_fold(z):
    return z.reshape(z.shape[0] // SUBLANES, SUBLANES, z.shape[1]).sum(axis=0)


def _norm_mm(name, h, g, w, tm):
    T, D = h.shape
    N = w.shape[1]
    nc = _pick(N, 512)

    def body(h_ref, g_ref, w_ref, y_ref, o_ref):
        xh, _ = _rms_stats(h_ref[...])
        y = (xh * g_ref[...]).astype(bf16)
        y_ref[...] = y
        for j in range(N // nc):
            o_ref[:, j * nc:(j + 1) * nc] = _dot(y, w_ref[:, j * nc:(j + 1) * nc]).astype(bf16)

    return pl.pallas_call(
        body, name=name, grid=(T // tm,),
        in_specs=[pl.BlockSpec((tm, D), lambda i: (i, 0)), pl.BlockSpec((1, D), lambda i: (0, 0)),
                  pl.BlockSpec((D, N), lambda i: (0, 0))],
        out_specs=[pl.BlockSpec((tm, D), lambda i: (i, 0)), pl.BlockSpec((tm, N), lambda i: (i, 0))],
        out_shape=[jax.ShapeDtypeStruct((T, D), bf16), jax.ShapeDtypeStruct((T, N), bf16)],
        compiler_params=_params(("parallel",)))(h, g.reshape(1, D), w)


def _mm_resid(name, a, w, resid, tm):
    T, K = a.shape
    N = w.shape[1]

    def body(a_ref, w_ref, r_ref, o_ref):
        o_ref[...] = r_ref[...] + _dot(a_ref[...], w_ref[...])

    return pl.pallas_call(
        body, name=name, grid=(T // tm,),
        in_specs=[pl.BlockSpec((tm, K), lambda i: (i, 0)), pl.BlockSpec((K, N), lambda i: (0, 0)),
                  pl.BlockSpec((tm, N), lambda i: (i, 0))],
        out_specs=pl.BlockSpec((tm, N), lambda i: (i, 0)),
        out_shape=jax.ShapeDtypeStruct((T, N), f32),
        compiler_params=_params(("parallel",)))(a, w, resid)


def _mm_nt(name, dy, w, tm):
    T, N = dy.shape
    K = w.shape[0]
    kc = _pick(K, 512)

    def body(d_ref, w_ref, o_ref):
        d = d_ref[...].astype(bf16)
        for j in range(K // kc):
            o_ref[:, j * kc:(j + 1) * kc] = _dg(d, w_ref[j * kc:(j + 1) * kc, :], NT).astype(bf16)

    return pl.pallas_call(
        body, name=name, grid=(T // tm,),
        in_specs=[pl.BlockSpec((tm, N), lambda i: (i, 0)), pl.BlockSpec((K, N), lambda i: (0, 0))],
        out_specs=pl.BlockSpec((tm, K), lambda i: (i, 0)),
        out_shape=jax.ShapeDtypeStruct((T, K), bf16),
        compiler_params=_params(("parallel",)))(dy, w)


def _mm_nt_normbwd(name, dys, w, h, g, dh_in, tm):
    n = len(dys)
    T = dys[0].shape[0]
    D = w.shape[0]
    widths = [d.shape[1] for d in dys]
    offs = [sum(widths[:i]) for i in range(n)]

    def body(*refs):
        d_refs = refs[:n]
        w_ref, h_ref, g_ref, dh_ref, o_ref, ob_ref, dg_ref = refs[n:]
        dz = _dg(d_refs[0][...], w_ref[:, :widths[0]], NT)
        for i in range(1, n):
            dz += _dg(d_refs[i][...], w_ref[:, offs[i]:offs[i] + widths[i]], NT)
        xh, r = _rms_stats(h_ref[...])
        dx, dg = _rms_bwd(dz, xh, r, g_ref[...])
        out = dh_ref[...] + dx
        o_ref[...] = out
        ob_ref[...] = out.astype(bf16)

        @pl.when(pl.program_id(0) == 0)
        def _():
            dg_ref[...] = jnp.zeros_like(dg_ref)
        dg_ref[...] += dg

    row = lambda c: pl.BlockSpec((tm, c), lambda i: (i, 0))
    return pl.pallas_call(
        body, name=name, grid=(T // tm,),
        in_specs=[row(c) for c in widths] + [pl.BlockSpec((D, sum(widths)), lambda i: (0, 0)), row(D),
                                             pl.BlockSpec((1, D), lambda i: (0, 0)), row(D)],
        out_specs=[row(D), row(D), pl.BlockSpec((1, D), lambda i: (0, 0))],
        out_shape=[jax.ShapeDtypeStruct((T, D), f32), jax.ShapeDtypeStruct((T, D), bf16),
                   jax.ShapeDtypeStruct((1, D), f32)],
        compiler_params=_params(("arbitrary",)))(*dys, w, h, g.reshape(1, D), dh_in)


def _mm_tn(name, a, b, tt, rows=None, row_off=0, prev=None):
    T, K = a.shape
    N = b.shape[1]
    rows = K if rows is None else rows
    tk = _pick(K, 1408)
    tn = _pick(N, 1024)
    assert row_off % tk == 0
    kb = row_off // tk

    def body(a_ref, b_ref, *rest):
        o_ref = rest[-1]

        @pl.when(pl.program_id(2) == 0)
        def _():
            o_ref[...] = jnp.zeros_like(o_ref)
        o_ref[...] += _dg(a_ref[...], b_ref[...], TN)

    in_specs = [pl.BlockSpec((tt, tk), lambda k, n, t: (t, k)), pl.BlockSpec((tt, tn), lambda k, n, t: (t, n))]
    args, aliases = [a, b], {}
    if prev is not None:
        in_specs.append(ANY)
        args.append(prev)
        aliases = {2: 0}
    return pl.pallas_call(
        body, name=name, grid=(K // tk, N // tn, T // tt), in_specs=in_specs,
        out_specs=pl.BlockSpec((tk, tn), lambda k, n, t: (k + kb, n)),
        out_shape=jax.ShapeDtypeStruct((rows, N), f32), input_output_aliases=aliases,
        compiler_params=_params(("parallel", "parallel", "arbitrary")))(*args)


def _final_loss(h, g, tgt, tm):
    T, D = h.shape

    def body(h_ref, g_ref, t_ref, dh_ref, dhb_ref, loss_ref, dg_ref):
        xh, r = _rms_stats(h_ref[...])
        gg = g_ref[...]
        diff = xh * gg - t_ref[...]
        dy = diff * (1.0 / D)
        dx, dg = _rms_bwd(dy, xh, r, gg)
        dh_ref[...] = dx
        dhb_ref[...] = dx.astype(bf16)

        @pl.when(pl.program_id(0) == 0)
        def _():
            dg_ref[...] = jnp.zeros_like(dg_ref)
            loss_ref[...] = jnp.zeros_like(loss_ref)
        dg_ref[...] += dg
        loss_ref[...] += (0.5 / D) * jnp.sum(jnp.sum(diff * diff, axis=1, keepdims=True), axis=0, keepdims=True)

    return pl.pallas_call(
        body, name="final_loss", grid=(T // tm,),
        in_specs=[pl.BlockSpec((tm, D), lambda i: (i, 0)), pl.BlockSpec((1, D), lambda i: (0, 0)),
                  pl.BlockSpec((tm, D), lambda i: (i, 0))],
        out_specs=[pl.BlockSpec((tm, D), lambda i: (i, 0)), pl.BlockSpec((tm, D), lambda i: (i, 0)),
                   pl.BlockSpec((1, 1), lambda i: (0, 0)), pl.BlockSpec((1, D), lambda i: (0, 0))],
        out_shape=[jax.ShapeDtypeStruct((T, D), f32), jax.ShapeDtypeStruct((T, D), bf16),
                   jax.ShapeDtypeStruct((1, 1), f32), jax.ShapeDtypeStruct((1, D), f32)],
        compiler_params=_params(("arbitrary",)))(h, g.reshape(1, D), tgt)


def _taps(load, r0, R):
    main = load(r0, R)
    hs = pl.multiple_of(jnp.maximum(r0 - HALO, 0), HALO)
    halo = load(hs, HALO) * (r0 > 0).astype(f32)
    ext = jnp.concatenate([halo, main], axis=0)
    xm1 = pltpu.roll(ext, 1, 0)[HALO:]
    xm2 = pltpu.roll(ext, 2, 0)[HALO:]
    return xm2, xm1, main


def _conv(w, b, taps):
    return b + w[0:1] * taps[0] + w[1:2] * taps[1] + w[2:3] * taps[2]


def _ref_load(ref):
    return lambda s, n: ref[pl.ds(s, n), :].astype(f32)


def _ffn_act(name, up3, cw, cb):
    B, S, F2 = up3.shape
    F = F2 // 2
    cwid = _pick(F, 256)
    nF = F // cwid
    R = min(256, S)

    def body(g_ref, v_ref, wg_ref, wv_ref, bg_ref, bv_ref, o_ref):
        wg, wv, bg, bv = wg_ref[...], wv_ref[...], bg_ref[...], bv_ref[...]

        def chunk(r, c):
            r0 = pl.multiple_of(r * R, R)
            cg = _conv(wg, bg, _taps(_ref_load(g_ref), r0, R))
            cv = _conv(wv, bv, _taps(_ref_load(v_ref), r0, R))
            o_ref[pl.ds(r0, R), :] = (cg * jax.nn.sigmoid(cg) * cv).astype(bf16)
            return c
        lax.fori_loop(0, S // R, chunk, 0)

    blk = lambda off: pl.BlockSpec((None, S, cwid), lambda b, j: (b, 0, off + j))
    wblk = lambda off: pl.BlockSpec((3, cwid), lambda b, j: (0, off + j))
    bblk = lambda off: pl.BlockSpec((1, cwid), lambda b, j: (0, off + j))
    return pl.pallas_call(
        body, name=name, grid=(B, nF),
        in_specs=[blk(0), blk(nF), wblk(0), wblk(nF), bblk(0), bblk(nF)],
        out_specs=pl.BlockSpec((None, S, cwid), lambda b, j: (b, 0, j)),
        out_shape=jax.ShapeDtypeStruct((B, S, F), bf16),
        compiler_params=_params(("parallel", "parallel")))(up3, up3, cw, cw, cb, cb)


def _rev_conv_rows(d, nxt, w):
    R = d.shape[0]
    ext = jnp.concatenate([d, nxt], axis=0)
    n = R + HALO
    xp1 = pltpu.roll(ext, n - 1, 0)[:R]
    xp2 = pltpu.roll(ext, n - 2, 0)[:R]
    return w[2:3] * d + w[1:2] * xp1 + w[0:1] * xp2


def _conv_grad_acc(acc, dc, taps):
    return (acc[0] + _row_fold(dc * taps[0]), acc[1] + _row_fold(dc * taps[1]), acc[2] + _row_fold(dc * taps[2]),
            acc[3] + _row_fold(dc))


def _conv_grad_out(dcw_ref, dcb_ref, acc):
    @pl.when(pl.program_id(1) == 0)
    def _():
        dcw_ref[...] = jnp.zeros_like(dcw_ref)
        dcb_ref[...] = jnp.zeros_like(dcb_ref)
    for k in range(3):
        dcw_ref[k:k + 1, :] += jnp.sum(acc[k], axis=0, keepdims=True)
    dcb_ref[...] += jnp.sum(acc[3], axis=0, keepdims=True)


def _ffn_act_bwd(name, up3, da3, cw, cb):
    B, S, F2 = up3.shape
    F = F2 // 2
    cwid = _pick(F, 256)
    nF = F // cwid
    R = min(256, S)
    nR = S // R

    def body(g_ref, v_ref, da_ref, wg_ref, wv_ref, bg_ref, bv_ref,
             dg_ref, dv_ref, dcwg_ref, dcwv_ref, dcbg_ref, dcbv_ref):
        wg, wv, bg, bv = wg_ref[...], wv_ref[...], bg_ref[...], bv_ref[...]

        def step(i, carry):
            ng, nv, accg, accv = carry
            r0 = pl.multiple_of((nR - 1 - i) * R, R)
            tg = _taps(_ref_load(g_ref), r0, R)
            tv = _taps(_ref_load(v_ref), r0, R)
            cg = _conv(wg, bg, tg)
            cv = _conv(wv, bv, tv)
            da = da_ref[pl.ds(r0, R), :].astype(f32)
            sg = jax.nn.sigmoid(cg)
            dgate = da * cv * (sg * (1.0 + cg * (1.0 - sg)))
            dval = da * (cg * sg)
            dg_ref[pl.ds(r0, R), :] = _rev_conv_rows(dgate, ng, wg).astype(bf16)
            dv_ref[pl.ds(r0, R), :] = _rev_conv_rows(dval, nv, wv).astype(bf16)
            return dgate[:HALO], dval[:HALO], _conv_grad_acc(accg, dgate, tg), _conv_grad_acc(accv, dval, tv)
        z = jnp.zeros((SUBLANES, cwid), f32)
        zh = jnp.zeros((HALO, cwid), f32)
        _, _, accg, accv = lax.fori_loop(0, nR, step, (zh, zh, (z, z, z, z), (z, z, z, z)))
        _conv_grad_out(dcwg_ref, dcbg_ref, accg)
        _conv_grad_out(dcwv_ref, dcbv_ref, accv)

    blk = lambda off: pl.BlockSpec((None, S, cwid), lambda j, b: (b, 0, off + j))
    wblk = lambda off: pl.BlockSpec((3, cwid), lambda j, b: (0, off + j))
    bblk = lambda off: pl.BlockSpec((1, cwid), lambda j, b: (0, off + j))
    half = jax.ShapeDtypeStruct((B, S, F), bf16)
    return pl.pallas_call(
        body, name=name, grid=(nF, B),
        in_specs=[blk(0), blk(nF), blk(0), wblk(0), wblk(nF), bblk(0), bblk(nF)],
        out_specs=[blk(0), blk(0), wblk(0), wblk(0), bblk(0), bblk(0)],
        out_shape=[half, half, jax.ShapeDtypeStruct((3, F), f32), jax.ShapeDtypeStruct((3, F), f32),
                   jax.ShapeDtypeStruct((1, F), f32), jax.ShapeDtypeStruct((1, F), f32)],
        compiler_params=_params(("parallel", "arbitrary")))(up3, up3, da3, cw, cw, cb, cb)


def _od_act(p3, cw, cb):
    B, S, D3 = p3.shape
    D = D3 // 3
    cwid = _pick(D, 256)
    nD = D // cwid
    R = min(256, S)

    def body(bg_ref, cg_ref, hx_ref, w_ref, b_ref, o_ref):
        w, b = w_ref[...], b_ref[...]
        q = lambda s, n: cg_ref[pl.ds(s, n), :].astype(f32) * hx_ref[pl.ds(s, n), :].astype(f32)

        def chunk(r, c):
            r0 = pl.multiple_of(r * R, R)
            cq = _conv(w, b, _taps(q, r0, R))
            o_ref[pl.ds(r0, R), :] = (bg_ref[pl.ds(r0, R), :].astype(f32) * cq).astype(bf16)
            return c
        lax.fori_loop(0, S // R, chunk, 0)

    blk = lambda off: pl.BlockSpec((None, S, cwid), lambda b, j: (b, 0, off + j))
    return pl.pallas_call(
        body, name="od_act", grid=(B, nD),
        in_specs=[blk(0), blk(nD), blk(2 * nD), pl.BlockSpec((3, cwid), lambda b, j: (0, j)),
                  pl.BlockSpec((1, cwid), lambda b, j: (0, j))],
        out_specs=pl.BlockSpec((None, S, cwid), lambda b, j: (b, 0, j)),
        out_shape=jax.ShapeDtypeStruct((B, S, D), bf16),
        compiler_params=_params(("parallel", "parallel")))(p3, p3, p3, cw, cb)


def _od_act_bwd(p3, dsc3, cw, cb):
    B, S, D3 = p3.shape
    D = D3 // 3
    cwid = _pick(D, 256)
    nD = D // cwid
    R = min(256, S)
    nR = S // R

    def body(bg_ref, cg_ref, hx_ref, d_ref, w_ref, b_ref, dbg_ref, dcg_ref, dhx_ref, dcw_ref, dcb_ref):
        w, b = w_ref[...], b_ref[...]
        q = lambda s, n: cg_ref[pl.ds(s, n), :].astype(f32) * hx_ref[pl.ds(s, n), :].astype(f32)

        def step(i, carry):
            nxt, acc = carry
            r0 = pl.multiple_of((nR - 1 - i) * R, R)
            rows = pl.ds(r0, R)
            tq = _taps(q, r0, R)
            cq = _conv(w, b, tq)
            d = d_ref[rows, :].astype(f32)
            dbg_ref[rows, :] = (d * cq).astype(bf16)
            dcq = d * bg_ref[rows, :].astype(f32)
            dq = _rev_conv_rows(dcq, nxt, w)
            dcg_ref[rows, :] = (dq * hx_ref[rows, :].astype(f32)).astype(bf16)
            dhx_ref[rows, :] = (dq * cg_ref[rows, :].astype(f32)).astype(bf16)
            return dcq[:HALO], _conv_grad_acc(acc, dcq, tq)
        z = jnp.zeros((SUBLANES, cwid), f32)
        _, acc = lax.fori_loop(0, nR, step, (jnp.zeros((HALO, cwid), f32), (z, z, z, z)))
        _conv_grad_out(dcw_ref, dcb_ref, acc)

    blk = lambda off: pl.BlockSpec((None, S, cwid), lambda j, b: (b, 0, off + j))
    part = jax.ShapeDtypeStruct((B, S, D), bf16)
    return pl.pallas_call(
        body, name="od_act_bwd", grid=(nD, B),
        in_specs=[blk(0), blk(nD), blk(2 * nD), blk(0), pl.BlockSpec((3, cwid), lambda j, b: (0, j)),
                  pl.BlockSpec((1, cwid), lambda j, b: (0, j))],
        out_specs=[blk(0), blk(0), blk(0), pl.BlockSpec((3, cwid), lambda j, b: (0, j)),
                   pl.BlockSpec((1, cwid), lambda j, b: (0, j))],
        out_shape=[part, part, part, jax.ShapeDtypeStruct((3, D), f32), jax.ShapeDtypeStruct((1, D), f32)],
        compiler_params=_params(("parallel", "arbitrary")))(p3, p3, p3, dsc3, cw, cb)


def _gmlp_parts(p, gv, SW, GW):
    uv = p[:, SW:].astype(f32)
    ge = _gelu(uv)
    u, v = ge[:, :GW], ge[:, GW:]
    vh, r = _rms_stats(v)
    return uv, u, vh, r, vh * gv


def _tril():
    rows = lax.broadcasted_iota(jnp.int32, (CHUNK, CHUNK), 0)
    cols = lax.broadcasted_iota(jnp.int32, (CHUNK, CHUNK), 1)
    return rows >= cols


def _gmlp(p0, a_out, ws, bst, gv, SW):
    T, PW = p0.shape
    GW = (PW - SW) // 2
    H = GW // GMLP_HEAD
    D = SW + GW

    def body(p_ref, a_ref, ws_ref, b_ref, gv_ref, o_ref):
        _, u, _, _, vn = _gmlp_parts(p_ref[...], gv_ref[...], SW, GW)
        tri = _tril()
        o_ref[:, :SW] = a_ref[...]
        for hh in range(H):
            sl = slice(hh * GMLP_HEAD, (hh + 1) * GMLP_HEAD)
            wm = jnp.where(tri, ws_ref[hh], 0.0).astype(bf16)
            gate = _dot(wm, vn[:, sl].astype(bf16)) + b_ref[:, hh:hh + 1]
            o_ref[:, SW + hh * GMLP_HEAD:SW + (hh + 1) * GMLP_HEAD] = (u[:, sl] * gate).astype(bf16)

    return pl.pallas_call(
        body, name="gmlp", grid=(T // CHUNK,),
        in_specs=[pl.BlockSpec((CHUNK, PW), lambda i: (i, 0)), pl.BlockSpec((CHUNK, SW), lambda i: (i, 0)),
                  pl.BlockSpec((H, CHUNK, CHUNK), lambda i: (0, 0, 0)), pl.BlockSpec((CHUNK, H), lambda i: (0, 0)),
                  pl.BlockSpec((1, GW), lambda i: (0, 0))],
        out_specs=pl.BlockSpec((CHUNK, D), lambda i: (i, 0)),
        out_shape=jax.ShapeDtypeStruct((T, D), bf16),
        compiler_params=_params(("parallel",)))(p0, a_out, ws, bst, gv)


def _gmlp_bwd(p0, dmix, ws, bst, gv, SW):
    T, PW = p0.shape
    GW = (PW - SW) // 2
    H = GW // GMLP_HEAD
    D = SW + GW

    def body(p_ref, d_ref, ws_ref, b_ref, gv_ref, duv_ref, dws_ref, dbs_ref, dgv_ref):
        gv_ = gv_ref[...]
        uv, u, vh, r, vn = _gmlp_parts(p_ref[...], gv_, SW, GW)
        dout = d_ref[...][:, SW:].astype(f32)
        tri = _tril()

        @pl.when(pl.program_id(0) == 0)
        def _():
            dws_ref[...] = jnp.zeros_like(dws_ref)
            dbs_ref[...] = jnp.zeros_like(dbs_ref)
            dgv_ref[...] = jnp.zeros_like(dgv_ref)
        du, dvn = [], []
        for hh in range(H):
            sl = slice(hh * GMLP_HEAD, (hh + 1) * GMLP_HEAD)
            wm = jnp.where(tri, ws_ref[hh], 0.0).astype(bf16)
            vnh = vn[:, sl].astype(bf16)
            gate = _dot(wm, vnh) + b_ref[:, hh:hh + 1]
            dgate = dout[:, sl] * u[:, sl]
            du.append(dout[:, sl] * gate)
            dgb = dgate.astype(bf16)
            dws_ref[hh] += jnp.where(tri, _dg(dgb, vnh, NT), 0.0)
            dbs_ref[hh] += jnp.broadcast_to(jnp.sum(dgate, axis=1, keepdims=True), (CHUNK, CHUNK))
            dvn.append(_dg(wm, dgb, TN))
        dvn = jnp.concatenate(dvn, axis=1)
        dv, dgv = _rms_bwd(dvn, vh, r, gv_)
        dgv_ref[...] += dgv
        dge = jnp.concatenate(du + [dv], axis=1)
        duv_ref[...] = (dge * _gelu_grad(uv)).astype(bf16)

    return pl.pallas_call(
        body, name="gmlp_bwd", grid=(T // CHUNK,),
        in_specs=[pl.BlockSpec((CHUNK, PW), lambda i: (i, 0)), pl.BlockSpec((CHUNK, D), lambda i: (i, 0)),
                  pl.BlockSpec((H, CHUNK, CHUNK), lambda i: (0, 0, 0)), pl.BlockSpec((CHUNK, H), lambda i: (0, 0)),
                  pl.BlockSpec((1, GW), lambda i: (0, 0))],
        out_specs=[pl.BlockSpec((CHUNK, 2 * GW), lambda i: (i, 0)), pl.BlockSpec((H, CHUNK, CHUNK), lambda i: (0, 0, 0)),
                   pl.BlockSpec((H, CHUNK, CHUNK), lambda i: (0, 0, 0)), pl.BlockSpec((1, GW), lambda i: (0, 0))],
        out_shape=[jax.ShapeDtypeStruct((T, 2 * GW), bf16), jax.ShapeDtypeStruct((H, CHUNK, CHUNK), f32),
                   jax.ShapeDtypeStruct((H, CHUNK, CHUNK), f32), jax.ShapeDtypeStruct((1, GW), f32)],
        compiler_params=_params(("arbitrary",)))(p0, dmix, ws, bst, gv)


def _s5_disc(lr, li, ldt):
    lr = jnp.minimum(lr, LAMBDA_RE_MAX)
    dt = jnp.exp(ldt)
    mag = jnp.exp(lr * dt)
    ar = mag * jnp.cos(li * dt)
    ai = mag * jnp.sin(li * dt)
    den = lr * lr + li * li
    nr = ar - 1.0
    zr = (nr * lr + ai * li) / den
    zi = (ai * lr - nr * li) / den
    return ar, ai, zr, zi


def _s5_prep(lr, li, ldt):
    G, P = lr.shape

    def body(lr_ref, li_ref, ldt_ref, ar_ref, ai_ref, zr_ref, zi_ref):
        ar, ai, zr, zi = _s5_disc(lr_ref[...], li_ref[...], ldt_ref[...])
        ar_ref[...] = ar
        ai_ref[...] = ai
        zr_ref[...] = zr
        zi_ref[...] = zi

    s = jax.ShapeDtypeStruct((G, P), f32)
    return pl.pallas_call(body, name="s5_prep", out_shape=[s, s, s, s])(lr, li, ldt)


def _s5_prep_bwd(lr, li, ldt, dar, dai, dzr, dzi):
    G, P = lr.shape

    def body(lr_ref, li_ref, ldt_ref, dar_ref, dai_ref, dzr_ref, dzi_ref, o1, o2, o3):
        _, vjp = jax.vjp(_s5_disc, lr_ref[...], li_ref[...], ldt_ref[...])
        cts = tuple(jnp.sum(r[...], axis=0) for r in (dar_ref, dai_ref, dzr_ref, dzi_ref))
        a, b, c = vjp(cts)
        o1[...] = a
        o2[...] = b
        o3[...] = c

    s = jax.ShapeDtypeStruct((G, P), f32)
    return pl.pallas_call(body, name="s5_prep_bwd", out_shape=[s, s, jax.ShapeDtypeStruct((G, 1), f32)])(
        lr, li, ldt, dar, dai, dzr, dzi)


def _s5_bbd(zr, zi, bre, bim):
    SW, NS = bre.shape

    def body(zr_ref, zi_ref, br_ref, bi_ref, o_ref):
        zr_, zi_, br, bi = zr_ref[...], zi_ref[...], br_ref[...], bi_ref[...]
        o_ref[:, :NS] = (zr_ * br - zi_ * bi).astype(bf16)
        o_ref[:, NS:] = (zr_ * bi + zi_ * br).astype(bf16)

    return pl.pallas_call(body, name="s5_bbd", out_shape=jax.ShapeDtypeStruct((SW, 2 * NS), bf16))(zr, zi, bre, bim)


def _s5_bbd_bwd(dbbd, zr, zi, bre, bim):
    SW, NS = bre.shape

    def body(d_ref, zr_ref, zi_ref, br_ref, bi_ref, dbr_ref, dbi_ref, dzr_ref, dzi_ref):
        zr_, zi_, br, bi = zr_ref[...], zi_ref[...], br_ref[...], bi_ref[...]
        dr, di = d_ref[:, :NS], d_ref[:, NS:]
        dbr_ref[...] = zr_ * dr + zi_ * di
        dbi_ref[...] = zr_ * di - zi_ * dr
        dzr_ref[...] = jnp.sum(dr * br + di * bi, axis=0, keepdims=True)
        dzi_ref[...] = jnp.sum(di * br - dr * bi, axis=0, keepdims=True)

    m = jax.ShapeDtypeStruct((SW, NS), f32)
    v = jax.ShapeDtypeStruct((1, NS), f32)
    return pl.pallas_call(body, name="s5_bbd_bwd", out_shape=[m, m, v, v])(dbbd, zr, zi, bre, bim)


def _slab_cat(ref, NB):
    return jnp.concatenate([ref[j] for j in range(NB)], axis=1)


def _s5_in(p3, bbd, SW, tm):
    B, S, PW = p3.shape
    NS = bbd.shape[1] // 2
    NB = NS // LANES

    def body(u_ref, b_ref, xr_ref, xi_ref):
        x = _dot(u_ref[...], b_ref[...])
        for j in range(NB):
            xr_ref[j] = x[:, j * LANES:(j + 1) * LANES]
            xi_ref[j] = x[:, NS + j * LANES:NS + (j + 1) * LANES]

    slab = jax.ShapeDtypeStruct((B, NB, S, LANES), f32)
    sspec = pl.BlockSpec((None, NB, tm, LANES), lambda b, i: (b, 0, i, 0))
    return pl.pallas_call(
        body, name="s5_in", grid=(B, S // tm),
        in_specs=[pl.BlockSpec((None, tm, SW), lambda b, i: (b, i, 0)), pl.BlockSpec((SW, 2 * NS), lambda b, i: (0, 0))],
        out_specs=[sspec, sspec], out_shape=[slab, slab],
        compiler_params=_params(("parallel", "parallel")))(p3, bbd)


def _s5_scan(name, xr, xi, ar, ai, reverse, hr=None, hi=None):
    B, NB, S, _ = xr.shape
    L = S // NSUB
    nb = 2 if (hr is None and NB % 2 == 0) else 1
    with_da = hr is not None

    def body(*refs):
        if with_da:
            xr_ref, xi_ref, ar_ref, ai_ref, hr_ref, hi_ref, or_ref, oi_ref, dar_ref, dai_ref, pr_scr, pi_scr = refs
        else:
            xr_ref, xi_ref, ar_ref, ai_ref, or_ref, oi_ref, pr_scr, pi_scr = refs
        sign = -1.0 if reverse else 1.0
        a_r = [jnp.broadcast_to(ar_ref[j], (NSUB, LANES)) for j in range(nb)]
        a_i = [jnp.broadcast_to(ai_ref[j], (NSUB, LANES)) * sign for j in range(nb)]

        def step(t, carry):
            row = (L - 1 - t) if reverse else t
            rows = pl.ds(row, NSUB, stride=L)
            out = []
            for j in range(nb):
                sr, si, pr, pi = carry[j]
                nr = a_r[j] * sr - a_i[j] * si + xr_ref.at[j][rows, :]
                ni = a_r[j] * si + a_i[j] * sr + xi_ref.at[j][rows, :]
                or_ref.at[j][rows, :] = nr
                oi_ref.at[j][rows, :] = ni
                npr = a_r[j] * pr - a_i[j] * pi
                npi = a_r[j] * pi + a_i[j] * pr
                pr_scr[j, pl.ds(row, 1), :] = npr[0:1]
                pi_scr[j, pl.ds(row, 1), :] = npi[0:1]
                out.append((nr, ni, npr, npi))
            return tuple(out)
        z = jnp.zeros((NSUB, LANES), f32)
        one = jnp.ones((NSUB, LANES), f32)
        fin = lax.fori_loop(0, L, step, tuple((z, z, one, z) for _ in range(nb)))

        for j in range(nb):
            sr, si, plr, pli = fin[j]
            plr, pli = plr[0:1], pli[0:1]
            cr = jnp.zeros((1, LANES), f32)
            ci = jnp.zeros((1, LANES), f32)
            order = range(NSUB - 2, -1, -1) if reverse else range(1, NSUB)
            for c in order:
                src = c + 1 if reverse else c - 1
                cr, ci = (sr[src:src + 1] + plr * cr - pli * ci, si[src:src + 1] + plr * ci + pli * cr)
                rows = slice(c * L, (c + 1) * L)
                tr, ti = pr_scr[j], pi_scr[j]
                or_ref[j, rows, :] += tr * cr - ti * ci
                oi_ref[j, rows, :] += tr * ci + ti * cr
            if with_da:
                first = lax.broadcasted_iota(jnp.int32, (L, LANES), 0) == 0
                dar = jnp.zeros((1, LANES), f32)
                dai = jnp.zeros((1, LANES), f32)
                for c in range(NSUB):
                    rows = slice(c * L, (c + 1) * L)
                    if c == 0:
                        lr_, li_ = jnp.zeros((1, LANES), f32), jnp.zeros((1, LANES), f32)
                    else:
                        lr_, li_ = hr_ref[j, c * L - 1:c * L, :], hi_ref[j, c * L - 1:c * L, :]
                    hpr = jnp.where(first, lr_, pltpu.roll(hr_ref[j, rows, :], 1, 0))
                    hpi = jnp.where(first, li_, pltpu.roll(hi_ref[j, rows, :], 1, 0))
                    gr, gi = or_ref[j, rows, :], oi_ref[j, rows, :]
                    dar += jnp.sum(hpr * gr + hpi * gi, axis=0, keepdims=True)
                    dai += jnp.sum(hpr * gi - hpi * gr, axis=0, keepdims=True)
                dar_ref[j] = dar
                dai_ref[j] = dai

    slab = jax.ShapeDtypeStruct((B, NB, S, LANES), f32)
    sspec = pl.BlockSpec((None, nb, S, LANES), lambda b, j: (b, j, 0, 0))
    aspec = pl.BlockSpec((nb, 1, LANES), lambda b, j: (j, 0, 0))
    in_specs = [sspec, sspec, aspec, aspec]
    out_specs = [sspec, sspec]
    out_shape = [slab, slab]
    args = [xr, xi, ar, ai]
    if with_da:
        in_specs += [sspec, sspec]
        args += [hr, hi]
        dspec = pl.BlockSpec((None, nb, 1, LANES), lambda b, j: (b, j, 0, 0))
        out_specs += [dspec, dspec]
        out_shape += [jax.ShapeDtypeStruct((B, NB, 1, LANES), f32)] * 2
    return pl.pallas_call(
        body, name=name, grid=(B, NB // nb), in_specs=in_specs, out_specs=out_specs, out_shape=out_shape,
        scratch_shapes=[pltpu.VMEM((nb, L, LANES), f32), pltpu.VMEM((nb, L, LANES), f32)],
        compiler_params=_params(("parallel", "parallel")))(*args)


def _s5_out_parts(hr_ref, hi_ref, u_ref, cr_ref, ci_ref, d_ref, wg_ref, bg_ref, NB):
    hcr = _slab_cat(hr_ref, NB).astype(bf16)
    hci = _slab_cat(hi_ref, NB).astype(bf16)
    u = u_ref[...].astype(f32)
    y2 = _dot(hcr, cr_ref[...]) - _dot(hci, ci_ref[...]) + d_ref[...] * u
    yg = _gelu(y2)
    s = jax.nn.sigmoid(_dot(yg.astype(bf16), wg_ref[...]) + bg_ref[...])
    return hcr, hci, u, y2, yg, s


def _s5_out_specs(B, S, NB, NS, SW, tm):
    sspec = pl.BlockSpec((None, NB, tm, LANES), lambda b, i: (b, 0, i, 0))
    full = lambda r, c: pl.BlockSpec((r, c), lambda b, i: (0, 0))
    return sspec, [sspec, sspec, pl.BlockSpec((None, tm, SW), lambda b, i: (b, i, 0)), full(NS, SW), full(NS, SW),
                   full(1, SW), full(SW, SW), full(1, SW)]


def _s5_out(hr, hi, p3, cbr, cbi, dsk, wglu, bglu, tm):
    B, NB, S, _ = hr.shape
    NS, SW = cbr.shape

    def body(hr_ref, hi_ref, u_ref, cr_ref, ci_ref, d_ref, wg_ref, bg_ref, o_ref):
        _, _, _, _, yg, s = _s5_out_parts(hr_ref, hi_ref, u_ref, cr_ref, ci_ref, d_ref, wg_ref, bg_ref, NB)
        o_ref[...] = (yg * s).astype(bf16)

    _, in_specs = _s5_out_specs(B, S, NB, NS, SW, tm)
    return pl.pallas_call(
        body, name="s5_out", grid=(B, S // tm), in_specs=in_specs,
        out_specs=pl.BlockSpec((None, tm, SW), lambda b, i: (b, i, 0)),
        out_shape=jax.ShapeDtypeStruct((B, S, SW), bf16),
        compiler_params=_params(("parallel", "parallel")))(hr, hi, p3, cbr, cbi, dsk, wglu, bglu)


def _s5_out_bwd(hr, hi, p3, dmix3, cbr, cbi, dsk, wglu, bglu, tm):
    B, NB, S, _ = hr.shape
    NS, SW = cbr.shape

    def body(hr_ref, hi_ref, u_ref, cr_ref, ci_ref, d_ref, wg_ref, bg_ref, da_ref,
             dhr_ref, dhi_ref, du_ref, dcr_ref, dci_ref, dd_ref, dwg_ref, dbg_ref):
        hcr, hci, u, y2, yg, s = _s5_out_parts(hr_ref, hi_ref, u_ref, cr_ref, ci_ref, d_ref, wg_ref, bg_ref, NB)
        da = da_ref[...].astype(f32)
        dz = da * yg * s * (1.0 - s)
        dzb = dz.astype(bf16)
        dyg = da * s + _dg(dzb, wg_ref[...], NT)
        dy2 = dyg * _gelu_grad(y2)
        dyb = dy2.astype(bf16)

        @pl.when((pl.program_id(0) == 0) & (pl.program_id(1) == 0))
        def _():
            for r in (dcr_ref, dci_ref, dd_ref, dwg_ref, dbg_ref):
                r[...] = jnp.zeros_like(r)
        dwg_ref[...] += _dg(yg.astype(bf16), dzb, TN)
        dbg_ref[...] += jnp.sum(dz, axis=0, keepdims=True)
        dd_ref[...] += jnp.sum(dy2 * u, axis=0, keepdims=True)
        dcr_ref[...] += _dg(hcr, dyb, TN)
        dci_ref[...] -= _dg(hci, dyb, TN)
        du_ref[...] = dy2 * d_ref[...]
        dhr = _dg(dyb, cr_ref[...], NT)
        dhi = _dg(dyb, ci_ref[...], NT)
        for j in range(NB):
            dhr_ref[j] = dhr[:, j * LANES:(j + 1) * LANES]
            dhi_ref[j] = -dhi[:, j * LANES:(j + 1) * LANES]

    sspec, in_specs = _s5_out_specs(B, S, NB, NS, SW, tm)
    in_specs = in_specs + [pl.BlockSpec((None, tm, SW), lambda b, i: (b, i, 0))]
    full = lambda r, c: pl.BlockSpec((r, c), lambda b, i: (0, 0))
    slab = jax.ShapeDtypeStruct((B, NB, S, LANES), f32)
    mat = lambda r, c: jax.ShapeDtypeStruct((r, c), f32)
    return pl.pallas_call(
        body, name="s5_out_bwd", grid=(B, S // tm), in_specs=in_specs,
        out_specs=[sspec, sspec, pl.BlockSpec((None, tm, SW), lambda b, i: (b, i, 0)), full(NS, SW), full(NS, SW),
                   full(1, SW), full(SW, SW), full(1, SW)],
        out_shape=[slab, slab, jax.ShapeDtypeStruct((B, S, SW), f32), mat(NS, SW), mat(NS, SW), mat(1, SW),
                   mat(SW, SW), mat(1, SW)],
        compiler_params=_params(("arbitrary", "arbitrary")))(hr, hi, p3, cbr, cbi, dsk, wglu, bglu, dmix3)


def _s5_in_bwd(gr, gi, p3, bbd, du_skip, duv3, tm):
    B, NB, S, _ = gr.shape
    SW, NS2 = bbd.shape
    PW = SW + duv3.shape[2]

    def body(gr_ref, gi_ref, u_ref, b_ref, ds_ref, duv_ref, dp_ref, db_ref):
        g = jnp.concatenate([_slab_cat(gr_ref, NB), _slab_cat(gi_ref, NB)], axis=1).astype(bf16)
        du = _dg(g, b_ref[...], NT) + ds_ref[...]
        dp_ref[:, :SW] = du.astype(bf16)
        dp_ref[:, SW:] = duv_ref[...]

        @pl.when((pl.program_id(0) == 0) & (pl.program_id(1) == 0))
        def _():
            db_ref[...] = jnp.zeros_like(db_ref)
        db_ref[...] += _dg(u_ref[...], g, TN)

    sspec = pl.BlockSpec((None, NB, tm, LANES), lambda b, i: (b, 0, i, 0))
    row = lambda c: pl.BlockSpec((None, tm, c), lambda b, i: (b, i, 0))
    return pl.pallas_call(
        body, name="s5_in_bwd", grid=(B, S // tm),
        in_specs=[sspec, sspec, row(SW), pl.BlockSpec((SW, NS2), lambda b, i: (0, 0)), row(SW), row(PW - SW)],
        out_specs=[row(PW), pl.BlockSpec((SW, NS2), lambda b, i: (0, 0))],
        out_shape=[jax.ShapeDtypeStruct((B, S, PW), bf16), jax.ShapeDtypeStruct((SW, NS2), f32)],
        compiler_params=_params(("arbitrary", "arbitrary")))(gr, gi, p3, bbd, du_skip, duv3)


BIG = ['ev_w_in', 'ev_w_out', 'od_w_in', 'od_w_out', 'ffn_w_up', 'ffn_w_down']
ANY = pl.BlockSpec(memory_space=pl.ANY)


def _rtile(rows, mult):
    best = None
    for d in range(mult, min(rows, 512) + 1, mult):
        if rows % d == 0:
            best = d
    assert best is not None, (rows, mult)
    return best


def _pair_sum(name, g, recv, c_idx, out_dtype):
    NCH, R, W = g.shape
    HALF_W = W // 2
    tr = _rtile(R, 16)

    def body(c_ref, a_ref, b_ref, o_ref):
        o_ref[...] = (a_ref[...] + b_ref[...]).astype(out_dtype)

    return pl.pallas_call(
        body, name=name,
        grid_spec=pltpu.PrefetchScalarGridSpec(
            num_scalar_prefetch=1, grid=(NCH, R // tr),
            in_specs=[pl.BlockSpec((None, tr, HALF_W), lambda j, i, c: (j, i, c[0])),
                      pl.BlockSpec((None, tr, HALF_W), lambda j, i, c: (j, i, 0))],
            out_specs=pl.BlockSpec((None, tr, HALF_W), lambda j, i, c: (j, i, 0))),
        out_shape=jax.ShapeDtypeStruct((NCH, R, HALF_W), out_dtype),
        compiler_params=_params(("parallel", "parallel")))(c_idx, g, recv)


def _chip_sum(name, r3):
    NCH, R, Wh = r3.shape
    tr = _rtile(R, 16)

    def body(a_ref, o_ref):
        o_ref[...] = ((a_ref[0].astype(f32) + a_ref[1].astype(f32)) + a_ref[2].astype(f32)) + a_ref[3].astype(f32)

    return pl.pallas_call(
        body, name=name, grid=(R // tr,),
        in_specs=[pl.BlockSpec((NCH, tr, Wh), lambda i: (0, i, 0))],
        out_specs=pl.BlockSpec((tr, Wh), lambda i: (i, 0)),
        out_shape=jax.ShapeDtypeStruct((R, Wh), f32),
        compiler_params=_params(("parallel",)))(r3)


def _adam_math(gg, w, m, v):
    nm = ADAM_B1 * m + (1.0 - ADAM_B1) * gg
    nv = ADAM_B2 * v + (1.0 - ADAM_B2) * jnp.square(gg)
    m_hat = nm / (1.0 - ADAM_B1 ** ADAM_STEP)
    v_hat = nv / (1.0 - ADAM_B2 ** ADAM_STEP)
    return -ADAM_LR * (m_hat / (jnp.sqrt(v_hat) + ADAM_EPS) + ADAM_WD * w), nm, nv


def _adamw_rows(name, g, w, m, v, lead, prev=None):
    L, R, W = w.shape
    tr = _rtile(R, SUBLANES)

    def body(g_ref, w_ref, m_ref, v_ref, *rest):
        d_ref, nm_ref, nv_ref = rest[-3:]
        d, nm, nv = _adam_math(g_ref[...], w_ref[...], m_ref[...], v_ref[...])
        d_ref[...] = d
        nm_ref[...] = nm
        nv_ref[...] = nv

    wspec = pl.BlockSpec((None, tr, W), lambda i: (lead, i, 0))
    in_specs = [pl.BlockSpec((tr, W), lambda i: (i, 0)), wspec, wspec, wspec]
    args, aliases = [g, w, m, v], {}
    if prev is not None:
        in_specs += [ANY] * 3
        args += list(prev)
        aliases = {4: 0, 5: 1, 6: 2}
    s = jax.ShapeDtypeStruct((L, R, W), f32)
    return pl.pallas_call(
        body, name=name, grid=(R // tr,), in_specs=in_specs, out_specs=[wspec] * 3, out_shape=[s, s, s],
        input_output_aliases=aliases, compiler_params=_params(("parallel",)))(*args)


def _adamw_cols(name, gt, w, m, v, lead, prev=None):
    L, Dm, n = w.shape
    bn = LANES

    def body(g_ref, w_ref, m_ref, v_ref, *rest):
        go_ref, d_ref, nm_ref, nv_ref = rest[-4:]
        gg = g_ref[...].T
        d, nm, nv = _adam_math(gg, w_ref[...], m_ref[...], v_ref[...])
        go_ref[...] = gg
        d_ref[...] = d
        nm_ref[...] = nm
        nv_ref[...] = nv

    wspec = pl.BlockSpec((None, Dm, bn), lambda i: (lead, 0, i))
    in_specs = [pl.BlockSpec((bn, Dm), lambda i: (i, 0)), wspec, wspec, wspec]
    args, aliases = [gt, w, m, v], {}
    if prev is not None:
        in_specs += [ANY] * 4
        args += list(prev)
        aliases = {4: 0, 5: 1, 6: 2, 7: 3}
    s = jax.ShapeDtypeStruct((L, Dm, n), f32)
    return pl.pallas_call(
        body, name=name, grid=(n // bn,), in_specs=in_specs, out_specs=[wspec] * 4, out_shape=[s, s, s, s],
        input_output_aliases=aliases, compiler_params=_params(("parallel",)))(*args)


def _place():
    x, y, c = lax.axis_index("x"), lax.axis_index("y"), lax.axis_index("c")
    return x, y, c, [(1 - x, y), (x, 1 - y), (1 - x, 1 - y)]


def _gathered_shape(sh, kind):
    if kind == "rows":
        return sh[:-2] + (N_CHIPS * sh[-2], sh[-1])
    if kind == "cols":
        return sh[:-1] + (N_CHIPS * sh[-1],)
    return (N_CHIPS,) + sh


def _comm_gather(shards, kinds):
    n = len(shards)
    shapes = [s.shape for s in shards]

    def window(ref, a, k, h=None):
        sh, kind = shapes[a], kinds[a]
        r = sh[-2]
        start, size = (0, r) if h is None else (h * (r // 2), r // 2)
        lead = (slice(None),) * (len(sh) - 2)
        if kind == "rows":
            return ref.at[lead + (pl.ds(k * r + start, size), slice(None))]
        if kind == "cols":
            return ref.at[lead + (pl.ds(start, size), pl.ds(pl.multiple_of(k * sh[-1], LANES), sh[-1]))]
        return ref.at[(k,) + lead + (pl.ds(start, size), slice(None))]

    def body(*refs):
        s_refs, o_refs = refs[:n], refs[n:2 * n]
        send_sems, recv_sems, local_sems = refs[2 * n:]
        x, y, c, chips = _place()
        me, sibling = (x, y, c), (x, y, 1 - c)
        k = 2 * x + y

        def copy(a, j, kk, hh, to, src=None):
            dst = window(o_refs[a], a, kk, hh)
            return pltpu.make_async_remote_copy(
                src_ref=dst if src is None else src, dst_ref=dst, send_sem=send_sems.at[6 * a + j],
                recv_sem=recv_sems.at[6 * a + j], device_id=to, device_id_type=MESH)

        mine = [pltpu.make_async_copy(s_refs[a], window(o_refs[a], a, k), local_sems.at[a]) for a in range(n)]
        for cp in mine:
            cp.start()
        first = []
        for a in range(n):
            r = shapes[a][-2]
            lead = (slice(None),) * (len(shapes[a]) - 2)
            src = s_refs[a].at[lead + (pl.ds(c * (r // 2), r // 2), slice(None))]
            first += [copy(a, j, k, c, (*chip, c), src=src) for j, chip in enumerate(chips)]
        for cp in first:
            cp.start()
        passed = []
        for j, (cx, cy) in enumerate(chips):
            for a in range(n):
                copy(a, j, 2 * cx + cy, c, me).wait_recv()
                fwd = copy(a, 3 + j, 2 * cx + cy, c, sibling)
                fwd.start()
                passed.append(fwd)
        for j, (cx, cy) in enumerate(chips):
            for a in range(n):
                copy(a, 3 + j, 2 * cx + cy, 1 - c, me).wait_recv()
        for cp in first + passed:
            cp.wait_send()
        for cp in mine:
            cp.wait()

    return pl.pallas_call(
        body, name="comm_gather", in_specs=[ANY] * n, out_specs=[ANY] * n,
        out_shape=[jax.ShapeDtypeStruct(_gathered_shape(s.shape, kd), s.dtype) for s, kd in zip(shards, kinds)],
        scratch_shapes=[pltpu.SemaphoreType.DMA((6 * n,)), pltpu.SemaphoreType.DMA((6 * n,)),
                        pltpu.SemaphoreType.DMA((n,))])(*shards)


def _comm_pair_swap(gs):
    n = len(gs)

    def body(*refs):
        g_refs, o_refs, send_sems, recv_sems = refs[:n], refs[n:2 * n], refs[2 * n], refs[2 * n + 1]
        x, y, c, _ = _place()
        half = [g.shape[2] // 2 for g in gs]
        cps = [pltpu.make_async_remote_copy(
            src_ref=g_refs[a].at[:, :, pl.ds(pl.multiple_of((1 - c) * half[a], LANES), half[a])], dst_ref=o_refs[a], send_sem=send_sems.at[a],
            recv_sem=recv_sems.at[a], device_id=(x, y, 1 - c), device_id_type=MESH) for a in range(n)]
        for cp in cps:
            cp.start()
        for cp in cps:
            cp.wait()

    return pl.pallas_call(
        body, name="comm_pair_swap", in_specs=[ANY] * n, out_specs=[ANY] * n,
        out_shape=[jax.ShapeDtypeStruct(g.shape[:2] + (g.shape[2] // 2,), g.dtype) for g in gs],
        scratch_shapes=[pltpu.SemaphoreType.DMA((n,)), pltpu.SemaphoreType.DMA((n,))])(*gs)


def _comm_chip_exchange(hs):
    n = len(hs)

    def body(*refs):
        h_refs, o_refs = refs[:n], refs[n:2 * n]
        send_sems, recv_sems, local_sems = refs[2 * n:]
        x, y, c, chips = _place()
        k = 2 * x + y
        mine = [pltpu.make_async_copy(h_refs[a].at[k], o_refs[a].at[k], local_sems.at[a]) for a in range(n)]
        for cp in mine:
            cp.start()

        def copy(a, j, src_slot, dst_slot):
            cx, cy = chips[j]
            return pltpu.make_async_remote_copy(
                src_ref=h_refs[a].at[src_slot], dst_ref=o_refs[a].at[dst_slot], send_sem=send_sems.at[3 * a + j],
                recv_sem=recv_sems.at[3 * a + j], device_id=(cx, cy, c), device_id_type=MESH)

        sends = [copy(a, j, 2 * cx + cy, k) for a in range(n) for j, (cx, cy) in enumerate(chips)]
        for cp in sends:
            cp.start()
        for a in range(n):
            for j, (cx, cy) in enumerate(chips):
                copy(a, j, k, 2 * cx + cy).wait_recv()
        for cp in sends:
            cp.wait_send()
        for cp in mine:
            cp.wait()

    return pl.pallas_call(
        body, name="comm_chip_exchange", in_specs=[ANY] * n, out_specs=[ANY] * n,
        out_shape=[jax.ShapeDtypeStruct(h.shape, h.dtype) for h in hs],
        scratch_shapes=[pltpu.SemaphoreType.DMA((3 * n,)), pltpu.SemaphoreType.DMA((3 * n,)),
                        pltpu.SemaphoreType.DMA((n,))])(*hs)


def _comm_pair_share(gs):
    n = len(gs)

    def body(*refs):
        g_refs, o_refs = refs[:n], refs[n:2 * n]
        send_sems, recv_sems, local_sems = refs[2 * n:]
        x, y, c, _ = _place()
        half = [g.shape[1] for g in gs]
        own = [pl.ds(pl.multiple_of(c * hw, LANES), hw) for hw in half]
        other = [pl.ds(pl.multiple_of((1 - c) * hw, LANES), hw) for hw in half]
        mine = [pltpu.make_async_copy(g_refs[a], o_refs[a].at[:, own[a]], local_sems.at[a]) for a in range(n)]
        for cp in mine:
            cp.start()

        def copy(a, cols):
            return pltpu.make_async_remote_copy(
                src_ref=g_refs[a], dst_ref=o_refs[a].at[:, cols], send_sem=send_sems.at[a], recv_sem=recv_sems.at[a],
                device_id=(x, y, 1 - c), device_id_type=MESH)

        sends = [copy(a, own[a]) for a in range(n)]
        for cp in sends:
            cp.start()
        for a in range(n):
            copy(a, other[a]).wait_recv()
        for cp in sends:
            cp.wait_send()
        for cp in mine:
            cp.wait()

    return pl.pallas_call(
        body, name="comm_pair_share", in_specs=[ANY] * n, out_specs=[ANY] * n,
        out_shape=[jax.ShapeDtypeStruct((g.shape[0], 2 * g.shape[1]), g.dtype) for g in gs],
        scratch_shapes=[pltpu.SemaphoreType.DMA((n,)), pltpu.SemaphoreType.DMA((n,)),
                        pltpu.SemaphoreType.DMA((n,))])(*gs)


def _pad_rows(flat, unit):
    n = flat.shape[-1]
    pad = (-n) % unit
    if pad:
        flat = jnp.pad(flat, [(0, 0)] * (flat.ndim - 1) + [(0, pad)])
    return flat


def _split_chips(full, axis):
    sh = full.shape
    t = full.reshape(sh[:axis] + (N_CHIPS, sh[axis] // N_CHIPS) + sh[axis + 1:])
    return jnp.moveaxis(t, axis, 0).reshape(N_CHIPS, -1)


def _join_chips(stack, shard_shape, axis):
    t = jnp.moveaxis(stack.reshape((N_CHIPS,) + tuple(shard_shape)), 0, axis)
    sh = t.shape
    return t.reshape(sh[:axis] + (sh[axis] * sh[axis + 1],) + sh[axis + 2:])


def _block_diag(blocks):
    G, r, c = blocks.shape
    eye = jnp.eye(G, dtype=blocks.dtype)
    return (blocks[:, :, None, :] * eye[:, None, :, None]).reshape(G * r, G * c)


def _diag_blocks(m, G):
    r, c = m.shape[0] // G, m.shape[1] // G
    idx = jnp.arange(G)
    return m.reshape(G, r, G, c)[idx, :, idx, :]


def _gather_weights(w):
    kinds = {'ev_w_in': "chip", 'ev_w_out': "rows", 's5_w_glu': "rows", 'od_w_in': "cols", 'od_w_out': "rows",
             'ffn_w_up': "cols", 'ffn_w_down': "rows"}
    conv = jnp.concatenate([w[n].reshape(-1) for n in GATHER_F32])
    conv = _pad_rows(conv, 2 * SUBLANES * LANES).reshape(-1, LANES)
    shards = [w[n].astype(bf16) if w[n].shape[0] > 1 else w[n][0].astype(bf16) for n in GATHER_BF16] + [conv]
    got = _comm_gather(shards, [kinds[n] for n in GATHER_BF16] + ["chip"])
    out = {}
    for n, g in zip(GATHER_BF16, got[:-1]):
        if kinds[n] == "chip":
            g = jnp.swapaxes(g, 0, 1).reshape(g.shape[1], -1)
        out[n] = g if w[n].shape[0] > 1 else g[None]
    allc, off = got[-1].reshape(N_CHIPS, -1), 0
    for n in GATHER_F32:
        out[n] = _join_chips(allc[:, off:off + w[n].size], w[n].shape, SHARD_AXIS[n])
        off += w[n].size
    return out


def kernel(x, mix_norm_g, ffn_norm_g, final_norm_g, ev_w_in, ev_w_out, s5_lam_re, s5_lam_im, s5_log_dt, s5_b_re, s5_b_im, s5_c_re, s5_c_im, s5_d, s5_w_glu, s5_b_glu, gm_w_s, gm_b_s, gm_v_g, od_w_in, od_conv_w, od_conv_b, od_w_out, ffn_w_up, ffn_conv_w, ffn_conv_b, ffn_w_down, loss_target, m_mix_norm_g, m_ffn_norm_g, m_final_norm_g, m_ev_w_in, m_ev_w_out, m_s5_lam_re, m_s5_lam_im, m_s5_log_dt, m_s5_b_re, m_s5_b_im, m_s5_c_re, m_s5_c_im, m_s5_d, m_s5_w_glu, m_s5_b_glu, m_gm_w_s, m_gm_b_s, m_gm_v_g, m_od_w_in, m_od_conv_w, m_od_conv_b, m_od_w_out, m_ffn_w_up, m_ffn_conv_w, m_ffn_conv_b, m_ffn_w_down, v_mix_norm_g, v_ffn_norm_g, v_final_norm_g, v_ev_w_in, v_ev_w_out, v_s5_lam_re, v_s5_lam_im, v_s5_log_dt, v_s5_b_re, v_s5_b_im, v_s5_c_re, v_s5_c_im, v_s5_d, v_s5_w_glu, v_s5_b_glu, v_gm_w_s, v_gm_b_s, v_gm_v_g, v_od_w_in, v_od_conv_w, v_od_conv_b, v_od_w_out, v_ffn_w_up, v_ffn_conv_w, v_ffn_conv_b, v_ffn_w_down):
    loc = dict(locals())
    w = {n: loc[n] for n in WEIGHTS}
    mom = {n: loc["m_" + n] for n in WEIGHTS}
    var = {n: loc["v_" + n] for n in WEIGHTS}

    B, S, D = x.shape
    T = B * S
    SW = s5_d.shape[1]
    G = SW // SSM_GROUP
    NS = G * SSM_STATE
    NB = NS // LANES
    tm = min(256, S)
    tt = min(1024, T)
    full = _gather_weights(w)

    h0 = x.reshape(T, D)
    w_ev_in = full['ev_w_in'][0]
    w_ev_out = full['ev_w_out'][0]
    y0, p0 = _norm_mm("ev_in", h0, mix_norm_g[0], w_ev_in, tm)
    PW = p0.shape[1]
    p03 = p0.reshape(B, S, PW)
    lr, li, ldt = s5_lam_re[0], s5_lam_im[0], s5_log_dt[0].reshape(G, 1)
    ar, ai, zr, zi = _s5_prep(lr, li, ldt)
    bre = _block_diag(jnp.swapaxes(s5_b_re[0], 1, 2))
    bim = _block_diag(jnp.swapaxes(s5_b_im[0], 1, 2))
    cbr = _block_diag(jnp.swapaxes(s5_c_re[0], 1, 2)).astype(bf16)
    cbi = _block_diag(jnp.swapaxes(s5_c_im[0], 1, 2)).astype(bf16)
    zr_row, zi_row = zr.reshape(1, NS), zi.reshape(1, NS)
    bbd = _s5_bbd(zr_row, zi_row, bre, bim)
    ar_s, ai_s = ar.reshape(NB, 1, LANES), ai.reshape(NB, 1, LANES)
    xr, xi = _s5_in(p03, bbd, SW, tm)
    hr, hi = _s5_scan("s5_scan", xr, xi, ar_s, ai_s, False)
    w_glu = full['s5_w_glu'][0]
    dsk, bglu = s5_d.reshape(1, SW), s5_b_glu.reshape(1, SW)
    a_out = _s5_out(hr, hi, p03, cbr, cbi, dsk, w_glu, bglu, tm)
    ws, bst, gv = gm_w_s[0], gm_b_s[0].T, gm_v_g.reshape(1, -1)
    mixcat = _gmlp(p0, a_out.reshape(T, SW), ws, bst, gv, SW)
    h1 = _mm_resid("ev_out", mixcat, w_ev_out, h0, tm)

    def ffn_fwd(l, h):
        z, up = _norm_mm(f"ffn_up{l}", h, ffn_norm_g[l], full['ffn_w_up'][l], tm)
        up3 = up.reshape(B, S, -1)
        act = _ffn_act(f"ffn_act{l}", up3, full['ffn_conv_w'][l], ffn_conv_b[l].reshape(1, -1))
        hn = _mm_resid(f"ffn_down{l}", act.reshape(T, -1), full['ffn_w_down'][l], h, tm)
        return hn, (z, up3, act)

    h2, ffn0 = ffn_fwd(0, h1)
    w_od_in, w_od_out = full['od_w_in'][0], full['od_w_out'][0]
    od_cw, od_cb = full['od_conv_w'][0], full['od_conv_b']
    y1, p1 = _norm_mm("od_in", h2, mix_norm_g[1], w_od_in, tm)
    p13 = p1.reshape(B, S, -1)
    sc = _od_act(p13, od_cw, od_cb)
    h3 = _mm_resid("od_out", sc.reshape(T, D), w_od_out, h2, tm)
    h4, ffn1 = ffn_fwd(1, h3)

    dh4, dh4b, loss_part, d_final_g = _final_loss(h4, final_norm_g, loss_target.reshape(T, D), tm)
    loss = lax.psum(loss_part[0, 0], ("x", "y", "c"))

    grads = {}

    def ffn_bwd(l, dh, dhb, h_in, saved):
        z, up3, act = saved
        w_down, w_up = full['ffn_w_down'][l], full['ffn_w_up'][l]
        da = _mm_nt(f"ffn_down_bwd{l}", dhb, w_down, tm)
        g_down = _mm_tn(f"ffn_down_dw{l}", act.reshape(T, -1), dhb, tt)
        dg3, dv3, dcwg, dcwv, dcbg, dcbv = _ffn_act_bwd(
            f"ffn_act_bwd{l}", up3, da.reshape(B, S, -1), full['ffn_conv_w'][l], ffn_conv_b[l].reshape(1, -1))
        dupg, dupv = dg3.reshape(T, -1), dv3.reshape(T, -1)
        F = dupg.shape[1]
        g_up = _mm_tn(f"ffn_up_dw{l}_gate", dupg, z, tt, rows=2 * F)
        g_up = _mm_tn(f"ffn_up_dw{l}_val", dupv, z, tt, rows=2 * F, row_off=F, prev=g_up)
        dh_new, dhb_new, dg = _mm_nt_normbwd(f"ffn_up_bwd{l}", [dupg, dupv], w_up, h_in, ffn_norm_g[l], dh, tm)
        dcw = jnp.concatenate([dcwg, dcwv], axis=1)
        dcb = jnp.concatenate([dcbg, dcbv], axis=1)
        return dh_new, dhb_new, g_down, g_up, dcw, dcb[0], dg[0]

    dh3, dh3b, gd1, gu1, gcw1, gcb1, gng1 = ffn_bwd(1, dh4, dh4b, h3, ffn1)
    dsc = _mm_nt("od_out_bwd", dh3b, w_od_out, tm)
    big = {'od_w_out': _mm_tn("od_out_dw", sc.reshape(T, D), dh3b, tt)}
    dbg3, dcg3, dhx3, d_od_cw, d_od_cb = _od_act_bwd(p13, dsc.reshape(B, S, D), od_cw, od_cb)
    dp1 = [t.reshape(T, D) for t in (dbg3, dcg3, dhx3)]
    g_od_in = None
    for i, piece in enumerate(dp1):
        g_od_in = _mm_tn(f"od_in_dw{i}", piece, y1, tt, rows=3 * D, row_off=i * D, prev=g_od_in)
    big['od_w_in'] = g_od_in
    grads['od_conv_w'] = d_od_cw[None]
    grads['od_conv_b'] = d_od_cb
    dh2, dh2b, gmix1 = _mm_nt_normbwd("od_in_bwd", dp1, w_od_in, h2, mix_norm_g[1], dh3, tm)
    dh1, dh1b, gd0, gu0, gcw0, gcb0, gng0 = ffn_bwd(0, dh2, dh2b, h1, ffn0)
    big.update({'ffn_w_down0': gd0, 'ffn_w_down1': gd1, 'ffn_w_up0': gu0, 'ffn_w_up1': gu1})
    grads['ffn_conv_w'] = jnp.stack([gcw0, gcw1])
    grads['ffn_conv_b'] = jnp.stack([gcb0, gcb1])
    grads['ffn_norm_g'] = jnp.stack([gng0, gng1])
    grads['final_norm_g'] = d_final_g[0]

    dmix = _mm_nt("ev_out_bwd", dh1b, w_ev_out, tm)
    big['ev_w_out'] = _mm_tn("ev_out_dw", mixcat, dh1b, tt)
    duv, d_ws, d_bs, d_gv = _gmlp_bwd(p0, dmix, ws, bst, gv, SW)
    grads['gm_w_s'] = d_ws[None]
    grads['gm_b_s'] = d_bs[:, :, 0][None]
    grads['gm_v_g'] = d_gv
    dhr, dhi, du_skip, d_cbr, d_cbi, d_dsk, d_wglu, d_bglu = _s5_out_bwd(
        hr, hi, p03, dmix.reshape(B, S, D), cbr, cbi, dsk, w_glu, bglu, tm)
    grads['s5_c_re'] = jnp.swapaxes(_diag_blocks(d_cbr, G), 1, 2)[None]
    grads['s5_c_im'] = jnp.swapaxes(_diag_blocks(d_cbi, G), 1, 2)[None]
    grads['s5_d'] = d_dsk
    grads['s5_w_glu'] = d_wglu[None]
    grads['s5_b_glu'] = d_bglu
    gr, gi, dar, dai = _s5_scan("s5_rscan", dhr, dhi, ar_s, ai_s, True, hr, hi)
    dp03, d_bbd = _s5_in_bwd(gr, gi, p03, bbd, du_skip, duv.reshape(B, S, -1), tm)
    d_bre, d_bim, d_zr, d_zi = _s5_bbd_bwd(d_bbd, zr_row, zi_row, bre, bim)
    grads['s5_b_re'] = jnp.swapaxes(_diag_blocks(d_bre, G), 1, 2)[None]
    grads['s5_b_im'] = jnp.swapaxes(_diag_blocks(d_bim, G), 1, 2)[None]
    shp = (-1, G, SSM_STATE)
    d_lr, d_li, d_ldt = _s5_prep_bwd(lr, li, ldt, dar.reshape(shp), dai.reshape(shp), d_zr.reshape(shp),
                                     d_zi.reshape(shp))
    grads['s5_lam_re'] = d_lr[None]
    grads['s5_lam_im'] = d_li[None]
    grads['s5_log_dt'] = d_ldt.reshape(1, G)
    dp0 = dp03.reshape(T, PW)
    big['ev_w_in'] = _mm_tn("ev_in_dw", dp0, y0, tt)
    grad_x, _, gmix0 = _mm_nt_normbwd("ev_in_bwd", [dp0], w_ev_in, h0, mix_norm_g[0], dh1, tm)
    grads['mix_norm_g'] = jnp.concatenate([gmix0, gmix1], axis=0)

    small = [n for n in WEIGHTS if n not in BIG]
    segs = []
    for n in small:
        gfull = grads[n].astype(f32)
        if n in SHARD_AXIS:
            segs.append(_split_chips(gfull, SHARD_AXIS[n]))
        else:
            segs.append(jnp.broadcast_to(gfull.reshape(1, -1), (N_CHIPS, gfull.size)))
    unit = 2 * SUBLANES * D
    gsmall = _pad_rows(jnp.concatenate(segs, axis=1), unit).reshape(N_CHIPS, -1, D)
    names = list(big) + ["small"]
    parts = [big[n].reshape(N_CHIPS, -1, D) for n in big] + [gsmall]
    c_idx = lax.axis_index("c").astype(jnp.int32).reshape(1)
    recv = _comm_pair_swap(parts)
    hsum = [_pair_sum(f"pair_sum_{n}", g, r, c_idx, f32 if n == "small" else bf16)
            for n, g, r in zip(names, parts, recv)]
    r3 = _comm_chip_exchange(hsum)
    gsum = [_chip_sum(f"chip_sum_{n}", r) for n, r in zip(names, r3)]
    gfin = dict(zip(names, _comm_pair_share(gsum)))

    out_g, out_d, out_m, out_v = {}, {}, {}, {}

    def put(n, res):
        out_d[n], out_m[n], out_v[n] = res

    gnat = gfin['ev_w_in'].T
    out_g['ev_w_in'] = gnat[None]
    put('ev_w_in', _adamw_rows("adamw_ev_w_in", gnat, ev_w_in, m_ev_w_in, v_ev_w_in, 0))
    for n in ('ev_w_out', 'od_w_out'):
        out_g[n] = gfin[n][None]
        put(n, _adamw_rows("adamw_" + n, gfin[n], w[n], mom[n], var[n], 0))
    res = _adamw_cols("adamw_od_w_in", gfin['od_w_in'], od_w_in, m_od_w_in, v_od_w_in, 0)
    out_g['od_w_in'] = res[0]
    put('od_w_in', res[1:])
    res = _adamw_cols("adamw_ffn_w_up0", gfin['ffn_w_up0'], ffn_w_up, m_ffn_w_up, v_ffn_w_up, 0)
    res = _adamw_cols("adamw_ffn_w_up1", gfin['ffn_w_up1'], ffn_w_up, m_ffn_w_up, v_ffn_w_up, 1, prev=res)
    out_g['ffn_w_up'] = res[0]
    put('ffn_w_up', res[1:])
    res = _adamw_rows("adamw_ffn_w_down0", gfin['ffn_w_down0'], ffn_w_down, m_ffn_w_down, v_ffn_w_down, 0)
    res = _adamw_rows("adamw_ffn_w_down1", gfin['ffn_w_down1'], ffn_w_down, m_ffn_w_down, v_ffn_w_down, 1, prev=res)
    out_g['ffn_w_down'] = jnp.stack([gfin['ffn_w_down0'], gfin['ffn_w_down1']])
    put('ffn_w_down', res)

    def pack_local(d):
        flat = _pad_rows(jnp.concatenate([d[n].astype(f32).reshape(-1) for n in small]), unit)
        return flat.reshape(1, -1, D)

    res = _adamw_rows("adamw_small", gfin['small'], pack_local(w), pack_local(mom), pack_local(var), 0)
    for dst, p in zip((out_g, out_d, out_m, out_v), (gfin['small'],) + tuple(res)):
        flat, off = p.reshape(-1), 0
        for n in small:
            dst[n] = flat[off:off + w[n].size].reshape(w[n].shape)
            off += w[n].size

    return (loss, grad_x.reshape(B, S, D), *[out_g[n] for n in WEIGHTS], *[out_d[n] for n in WEIGHTS],
            *[out_m[n] for n in WEIGHTS], *[out_v[n] for n in WEIGHTS])
```

```python
import functools
import math

import jax
import jax.numpy as jnp
from jax import lax
from jax.experimental import pallas as pl
from jax.experimental.pallas import tpu as pltpu

f32 = jnp.float32
bf16 = jnp.bfloat16
MESH = pl.DeviceIdType.MESH

SSM_GROUP = 16
SSM_STATE = 64
GMLP_HEAD = 128
CHUNK = 128
EPS = 1e-6
LAMBDA_RE_MAX = -1e-4
ADAM_LR, ADAM_B1, ADAM_B2, ADAM_EPS, ADAM_WD, ADAM_STEP = 0.001, 0.9, 0.999, 1e-08, 0.01, 10

LANES = 128
SUBLANES = 8
NSUB = 32
HALO = 16
VMEM_LIMIT = 56 * 1024 * 1024
N_CHIPS = 4

WEIGHTS = ['mix_norm_g', 'ffn_norm_g', 'final_norm_g', 'ev_w_in', 'ev_w_out', 's5_lam_re', 's5_lam_im', 's5_log_dt',
           's5_b_re', 's5_b_im', 's5_c_re', 's5_c_im', 's5_d', 's5_w_glu', 's5_b_glu', 'gm_w_s', 'gm_b_s', 'gm_v_g',
           'od_w_in', 'od_conv_w', 'od_conv_b', 'od_w_out', 'ffn_w_up', 'ffn_conv_w', 'ffn_conv_b', 'ffn_w_down']
SHARD_AXIS = {'ev_w_in': 2, 'ev_w_out': 1, 's5_w_glu': 1, 'od_w_in': 2, 'od_conv_w': 2, 'od_conv_b': 1, 'od_w_out': 1,
              'ffn_w_up': 2, 'ffn_conv_w': 2, 'ffn_w_down': 1}
GATHER_BF16 = ['ev_w_in', 'ev_w_out', 's5_w_glu', 'od_w_in', 'od_w_out', 'ffn_w_up', 'ffn_w_down']
GATHER_F32 = ['od_conv_w', 'od_conv_b', 'ffn_conv_w']

_GELU_K0 = math.sqrt(2.0 / math.pi)
_GELU_K1 = 0.044715
NT = (((1,), (1,)), ((), ()))
TN = (((0,), (0,)), ((), ()))


def _pick(n, cap):
    if n <= cap:
        return n
    best = None
    for d in range(LANES, cap + 1, LANES):
        if n % d == 0:
            best = d
    assert best is not None, (n, cap)
    return best


def _params(sem=None):
    return pltpu.CompilerParams(dimension_semantics=sem, vmem_limit_bytes=VMEM_LIMIT)


def _gelu(x):
    return 0.5 * x * (1.0 + jnp.tanh(_GELU_K0 * (x + _GELU_K1 * x * x * x)))


def _gelu_grad(x):
    t = jnp.tanh(_GELU_K0 * (x + _GELU_K1 * x * x * x))
    return 0.5 * (1.0 + t) + 0.5 * x * (1.0 - t * t) * _GELU_K0 * (1.0 + 3.0 * _GELU_K1 * x * x)


def _rms_stats(x):
    r = lax.rsqrt(jnp.mean(x * x, axis=-1, keepdims=True) + EPS)
    return x * r, r


def _rms_bwd(dy, xh, r, g):
    dxh = dy * g
    dx = r * (dxh - xh * jnp.mean(dxh * xh, axis=-1, keepdims=True))
    return dx, jnp.sum(dy * xh, axis=0, keepdims=True)


def _dot(a, b):
    return jnp.dot(a, b, preferred_element_type=f32)


def _dg(a, b, dims):
    return lax.dot_general(a, b, dims, preferred_element_type=f32)


def _row_fold(z):
    return z.reshape(z.shape[0] // SUBLANES, SUBLANES, z.shape[1]).sum(axis=0)


def _norm_mm(name, h, g, w, tm):
    T, D = h.shape
    N = w.shape[1]
    nc = _pick(N, 512)

    def body(h_ref, g_ref, w_ref, y_ref, o_ref):
        xh, _ = _rms_stats(h_ref[...])
        y = (xh * g_ref[...]).astype(bf16)
        y_ref[...] = y
        for j in range(N // nc):
            o_ref[:, j * nc:(j + 1) * nc] = _dot(y, w_ref[:, j * nc:(j + 1) * nc]).astype(bf16)

    return pl.pallas_call(
        body, name=name, grid=(T // tm,),
        in_specs=[pl.BlockSpec((tm, D), lambda i: (i, 0)), pl.BlockSpec((1, D), lambda i: (0, 0)),
                  pl.BlockSpec((D, N), lambda i: (0, 0))],
        out_specs=[pl.BlockSpec((tm, D), lambda i: (i, 0)), pl.BlockSpec((tm, N), lambda i: (i, 0))],
        out_shape=[jax.ShapeDtypeStruct((T, D), bf16), jax.ShapeDtypeStruct((T, N), bf16)],
        compiler_params=_params(("parallel",)))(h, g.reshape(1, D), w)


def _mm_resid(name, a, w, resid, tm):
    T, K = a.shape
    N = w.shape[1]

    def body(a_ref, w_ref, r_ref, o_ref):
        o_ref[...] = r_ref[...] + _dot(a_ref[...], w_ref[...])

    return pl.pallas_call(
        body, name=name, grid=(T // tm,),
        in_specs=[pl.BlockSpec((tm, K), lambda i: (i, 0)), pl.BlockSpec((K, N), lambda i: (0, 0)),
                  pl.BlockSpec((tm, N), lambda i: (i, 0))],
        out_specs=pl.BlockSpec((tm, N), lambda i: (i, 0)),
        out_shape=jax.ShapeDtypeStruct((T, N), f32),
        compiler_params=_params(("parallel",)))(a, w, resid)


def _mm_nt(name, dy, w, tm):
    T, N = dy.shape
    K = w.shape[0]
    kc = _pick(K, 512)

    def body(d_ref, w_ref, o_ref):
        d = d_ref[...].astype(bf16)
        for j in range(K // kc):
            o_ref[:, j * kc:(j + 1) * kc] = _dg(d, w_ref[j * kc:(j + 1) * kc, :], NT).astype(bf16)

    return pl.pallas_call(
        body, name=name, grid=(T // tm,),
        in_specs=[pl.BlockSpec((tm, N), lambda i: (i, 0)), pl.BlockSpec((K, N), lambda i: (0, 0))],
        out_specs=pl.BlockSpec((tm, K), lambda i: (i, 0)),
        out_shape=jax.ShapeDtypeStruct((T, K), bf16),
        compiler_params=_params(("parallel",)))(dy, w)


def _mm_nt_normbwd(name, dys, w, h, g, dh_in, tm):
    n = len(dys)
    T = dys[0].shape[0]
    D = w.shape[0]
    widths = [d.shape[1] for d in dys]
    offs = [sum(widths[:i]) for i in range(n)]

    def body(*refs):
        d_refs = refs[:n]
        w_ref, h_ref, g_ref, dh_ref, o_ref, ob_ref, dg_ref = refs[n:]
        dz = _dg(d_refs[0][...], w_ref[:, :widths[0]], NT)
        for i in range(1, n):
            dz += _dg(d_refs[i][...], w_ref[:, offs[i]:offs[i] + widths[i]], NT)
        xh, r = _rms_stats(h_ref[...])
        dx, dg = _rms_bwd(dz, xh, r, g_ref[...])
        out = dh_ref[...] + dx
        o_ref[...] = out
        ob_ref[...] = out.astype(bf16)

        @pl.when(pl.program_id(0) == 0)
        def _():
            dg_ref[...] = jnp.zeros_like(dg_ref)
        dg_ref[...] += dg

    row = lambda c: pl.BlockSpec((tm, c), lambda i: (i, 0))
    return pl.pallas_call(
        body, name=name, grid=(T // tm,),
        in_specs=[row(c) for c in widths] + [pl.BlockSpec((D, sum(widths)), lambda i: (0, 0)), row(D),
                                             pl.BlockSpec((1, D), lambda i: (0, 0)), row(D)],
        out_specs=[row(D), row(D), pl.BlockSpec((1, D), lambda i: (0, 0))],
        out_shape=[jax.ShapeDtypeStruct((T, D), f32), jax.ShapeDtypeStruct((T, D), bf16),
                   jax.ShapeDtypeStruct((1, D), f32)],
        compiler_params=_params(("arbitrary",)))(*dys, w, h, g.reshape(1, D), dh_in)


def _mm_tn(name, a, b, tt, rows=None, row_off=0, prev=None):
    T, K = a.shape
    N = b.shape[1]
    rows = K if rows is None else rows
    tk = _pick(K, 1408)
    tn = _pick(N, 1024)
    assert row_off % tk == 0
    kb = row_off // tk

    def body(a_ref, b_ref, *rest):
        o_ref = rest[-1]

        @pl.when(pl.program_id(2) == 0)
        def _():
            o_ref[...] = jnp.zeros_like(o_ref)
        o_ref[...] += _dg(a_ref[...], b_ref[...], TN)

    in_specs = [pl.BlockSpec((tt, tk), lambda k, n, t: (t, k)), pl.BlockSpec((tt, tn), lambda k, n, t: (t, n))]
    args, aliases = [a, b], {}
    if prev is not None:
        in_specs.append(ANY)
        args.append(prev)
        aliases = {2: 0}
    return pl.pallas_call(
        body, name=name, grid=(K // tk, N // tn, T // tt), in_specs=in_specs,
        out_specs=pl.BlockSpec((tk, tn), lambda k, n, t: (k + kb, n)),
        out_shape=jax.ShapeDtypeStruct((rows, N), f32), input_output_aliases=aliases,
        compiler_params=_params(("parallel", "parallel", "arbitrary")))(*args)


def _final_loss(h, g, tgt, tm):
    T, D = h.shape

    def body(h_ref, g_ref, t_ref, dh_ref, dhb_ref, loss_ref, dg_ref):
        xh, r = _rms_stats(h_ref[...])
        gg = g_ref[...]
        diff = xh * gg - t_ref[...]
        dy = diff * (1.0 / D)
        dx, dg = _rms_bwd(dy, xh, r, gg)
        dh_ref[...] = dx
        dhb_ref[...] = dx.astype(bf16)

        @pl.when(pl.program_id(0) == 0)
        def _():
            dg_ref[...] = jnp.zeros_like(dg_ref)
            loss_ref[...] = jnp.zeros_like(loss_ref)
        dg_ref[...] += dg
        loss_ref[...] += (0.5 / D) * jnp.sum(jnp.sum(diff * diff, axis=1, keepdims=True), axis=0, keepdims=True)

    return pl.pallas_call(
        body, name="final_loss", grid=(T // tm,),
        in_specs=[pl.BlockSpec((tm, D), lambda i: (i, 0)), pl.BlockSpec((1, D), lambda i: (0, 0)),
                  pl.BlockSpec((tm, D), lambda i: (i, 0))],
        out_specs=[pl.BlockSpec((tm, D), lambda i: (i, 0)), pl.BlockSpec((tm, D), lambda i: (i, 0)),
                   pl.BlockSpec((1, 1), lambda i: (0, 0)), pl.BlockSpec((1, D), lambda i: (0, 0))],
        out_shape=[jax.ShapeDtypeStruct((T, D), f32), jax.ShapeDtypeStruct((T, D), bf16),
                   jax.ShapeDtypeStruct((1, 1), f32), jax.ShapeDtypeStruct((1, D), f32)],
        compiler_params=_params(("arbitrary",)))(h, g.reshape(1, D), tgt)


def _taps(load, r0, R):
    main = load(r0, R)
    hs = pl.multiple_of(jnp.maximum(r0 - HALO, 0), HALO)
    halo = load(hs, HALO) * (r0 > 0).astype(f32)
    ext = jnp.concatenate([halo, main], axis=0)
    xm1 = pltpu.roll(ext, 1, 0)[HALO:]
    xm2 = pltpu.roll(ext, 2, 0)[HALO:]
    return xm2, xm1, main


def _conv(w, b, taps):
    return b + w[0:1] * taps[0] + w[1:2] * taps[1] + w[2:3] * taps[2]


def _ref_load(ref):
    return lambda s, n: ref[pl.ds(s, n), :].astype(f32)


def _ffn_act(name, up3, cw, cb):
    B, S, F2 = up3.shape
    F = F2 // 2
    cwid = _pick(F, 256)
    nF = F // cwid
    R = min(256, S)

    def body(g_ref, v_ref, wg_ref, wv_ref, bg_ref, bv_ref, o_ref):
        wg, wv, bg, bv = wg_ref[...], wv_ref[...], bg_ref[...], bv_ref[...]

        def chunk(r, c):
            r0 = pl.multiple_of(r * R, R)
            cg = _conv(wg, bg, _taps(_ref_load(g_ref), r0, R))
            cv = _conv(wv, bv, _taps(_ref_load(v_ref), r0, R))
            o_ref[pl.ds(r0, R), :] = (cg * jax.nn.sigmoid(cg) * cv).astype(bf16)
            return c
        lax.fori_loop(0, S // R, chunk, 0)

    blk = lambda off: pl.BlockSpec((None, S, cwid), lambda b, j: (b, 0, off + j))
    wblk = lambda off: pl.BlockSpec((3, cwid), lambda b, j: (0, off + j))
    bblk = lambda off: pl.BlockSpec((1, cwid), lambda b, j: (0, off + j))
    return pl.pallas_call(
        body, name=name, grid=(B, nF),
        in_specs=[blk(0), blk(nF), wblk(0), wblk(nF), bblk(0), bblk(nF)],
        out_specs=pl.BlockSpec((None, S, cwid), lambda b, j: (b, 0, j)),
        out_shape=jax.ShapeDtypeStruct((B, S, F), bf16),
        compiler_params=_params(("parallel", "parallel")))(up3, up3, cw, cw, cb, cb)


def _rev_conv_rows(d, nxt, w):
    R = d.shape[0]
    ext = jnp.concatenate([d, nxt], axis=0)
    n = R + HALO
    xp1 = pltpu.roll(ext, n - 1, 0)[:R]
    xp2 = pltpu.roll(ext, n - 2, 0)[:R]
    return w[2:3] * d + w[1:2] * xp1 + w[0:1] * xp2


def _conv_grad_acc(acc, dc, taps):
    return (acc[0] + _row_fold(dc * taps[0]), acc[1] + _row_fold(dc * taps[1]), acc[2] + _row_fold(dc * taps[2]),
            acc[3] + _row_fold(dc))


def _conv_grad_out(dcw_ref, dcb_ref, acc):
    @pl.when(pl.program_id(1) == 0)
    def _():
        dcw_ref[...] = jnp.zeros_like(dcw_ref)
        dcb_ref[...] = jnp.zeros_like(dcb_ref)
    for k in range(3):
        dcw_ref[k:k + 1, :] += jnp.sum(acc[k], axis=0, keepdims=True)
    dcb_ref[...] += jnp.sum(acc[3], axis=0, keepdims=True)


def _ffn_act_bwd(name, up3, da3, cw, cb):
    B, S, F2 = up3.shape
    F = F2 // 2
    cwid = _pick(F, 256)
    nF = F // cwid
    R = min(256, S)
    nR = S // R

    def body(g_ref, v_ref, da_ref, wg_ref, wv_ref, bg_ref, bv_ref,
             dg_ref, dv_ref, dcwg_ref, dcwv_ref, dcbg_ref, dcbv_ref):
        wg, wv, bg, bv = wg_ref[...], wv_ref[...], bg_ref[...], bv_ref[...]

        def step(i, carry):
            ng, nv, accg, accv = carry
            r0 = pl.multiple_of((nR - 1 - i) * R, R)
            tg = _taps(_ref_load(g_ref), r0, R)
            tv = _taps(_ref_load(v_ref), r0, R)
            cg = _conv(wg, bg, tg)
            cv = _conv(wv, bv, tv)
            da = da_ref[pl.ds(r0, R), :].astype(f32)
            sg = jax.nn.sigmoid(cg)
            dgate = da * cv * (sg * (1.0 + cg * (1.0 - sg)))
            dval = da * (cg * sg)
            dg_ref[pl.ds(r0, R), :] = _rev_conv_rows(dgate, ng, wg).astype(bf16)
            dv_ref[pl.ds(r0, R), :] = _rev_conv_rows(dval, nv, wv).astype(bf16)
            return dgate[:HALO], dval[:HALO], _conv_grad_acc(accg, dgate, tg), _conv_grad_acc(accv, dval, tv)
        z = jnp.zeros((SUBLANES, cwid), f32)
        zh = jnp.zeros((HALO, cwid), f32)
        _, _, accg, accv = lax.fori_loop(0, nR, step, (zh, zh, (z, z, z, z), (z, z, z, z)))
        _conv_grad_out(dcwg_ref, dcbg_ref, accg)
        _conv_grad_out(dcwv_ref, dcbv_ref, accv)

    blk = lambda off: pl.BlockSpec((None, S, cwid), lambda j, b: (b, 0, off + j))
    wblk = lambda off: pl.BlockSpec((3, cwid), lambda j, b: (0, off + j))
    bblk = lambda off: pl.BlockSpec((1, cwid), lambda j, b: (0, off + j))
    half = jax.ShapeDtypeStruct((B, S, F), bf16)
    return pl.pallas_call(
        body, name=name, grid=(nF, B),
        in_specs=[blk(0), blk(nF), blk(0), wblk(0), wblk(nF), bblk(0), bblk(nF)],
        out_specs=[blk(0), blk(0), wblk(0), wblk(0), bblk(0), bblk(0)],
        out_shape=[half, half, jax.ShapeDtypeStruct((3, F), f32), jax.ShapeDtypeStruct((3, F), f32),
                   jax.ShapeDtypeStruct((1, F), f32), jax.ShapeDtypeStruct((1, F), f32)],
        compiler_params=_params(("parallel", "arbitrary")))(up3, up3, da3, cw, cw, cb, cb)


def _od_act(p3, cw, cb):
    B, S, D3 = p3.shape
    D = D3 // 3
    cwid = _pick(D, 256)
    nD = D // cwid
    R = min(256, S)

    def body(bg_ref, cg_ref, hx_ref, w_ref, b_ref, o_ref):
        w, b = w_ref[...], b_ref[...]
        q = lambda s, n: cg_ref[pl.ds(s, n), :].astype(f32) * hx_ref[pl.ds(s, n), :].astype(f32)

        def chunk(r, c):
            r0 = pl.multiple_of(r * R, R)
            cq = _conv(w, b, _taps(q, r0, R))
            o_ref[pl.ds(r0, R), :] = (bg_ref[pl.ds(r0, R), :].astype(f32) * cq).astype(bf16)
            return c
        lax.fori_loop(0, S // R, chunk, 0)

    blk = lambda off: pl.BlockSpec((None, S, cwid), lambda b, j: (b, 0, off + j))
    return pl.pallas_call(
        body, name="od_act", grid=(B, nD),
        in_specs=[blk(0), blk(nD), blk(2 * nD), pl.BlockSpec((3, cwid), lambda b, j: (0, j)),
                  pl.BlockSpec((1, cwid), lambda b, j: (0, j))],
        out_specs=pl.BlockSpec((None, S, cwid), lambda b, j: (b, 0, j)),
        out_shape=jax.ShapeDtypeStruct((B, S, D), bf16),
        compiler_params=_params(("parallel", "parallel")))(p3, p3, p3, cw, cb)


def _od_act_bwd(p3, dsc3, cw, cb):
    B, S, D3 = p3.shape
    D = D3 // 3
    cwid = _pick(D, 256)
    nD = D // cwid
    R = min(256, S)
    nR = S // R

    def body(bg_ref, cg_ref, hx_ref, d_ref, w_ref, b_ref, dbg_ref, dcg_ref, dhx_ref, dcw_ref, dcb_ref):
        w, b = w_ref[...], b_ref[...]
        q = lambda s, n: cg_ref[pl.ds(s, n), :].astype(f32) * hx_ref[pl.ds(s, n), :].astype(f32)

        def step(i, carry):
            nxt, acc = carry
            r0 = pl.multiple_of((nR - 1 - i) * R, R)
            rows = pl.ds(r0, R)
            tq = _taps(q, r0, R)
            cq = _conv(w, b, tq)
            d = d_ref[rows, :].astype(f32)
            dbg_ref[rows, :] = (d * cq).astype(bf16)
            dcq = d * bg_ref[rows, :].astype(f32)
            dq = _rev_conv_rows(dcq, nxt, w)
            dcg_ref[rows, :] = (dq * hx_ref[rows, :].astype(f32)).astype(bf16)
            dhx_ref[rows, :] = (dq * cg_ref[rows, :].astype(f32)).astype(bf16)
            return dcq[:HALO], _conv_grad_acc(acc, dcq, tq)
        z = jnp.zeros((SUBLANES, cwid), f32)
        _, acc = lax.fori_loop(0, nR, step, (jnp.zeros((HALO, cwid), f32), (z, z, z, z)))
        _conv_grad_out(dcw_ref, dcb_ref, acc)

    blk = lambda off: pl.BlockSpec((None, S, cwid), lambda j, b: (b, 0, off + j))
    part = jax.ShapeDtypeStruct((B, S, D), bf16)
    return pl.pallas_call(
        body, name="od_act_bwd", grid=(nD, B),
        in_specs=[blk(0), blk(nD), blk(2 * nD), blk(0), pl.BlockSpec((3, cwid), lambda j, b: (0, j)),
                  pl.BlockSpec((1, cwid), lambda j, b: (0, j))],
        out_specs=[blk(0), blk(0), blk(0), pl.BlockSpec((3, cwid), lambda j, b: (0, j)),
                   pl.BlockSpec((1, cwid), lambda j, b: (0, j))],
        out_shape=[part, part, part, jax.ShapeDtypeStruct((3, D), f32), jax.ShapeDtypeStruct((1, D), f32)],
        compiler_params=_params(("parallel", "arbitrary")))(p3, p3, p3, dsc3, cw, cb)


def _gmlp_parts(p, gv, SW, GW):
    uv = p[:, SW:].astype(f32)
    ge = _gelu(uv)
    u, v = ge[:, :GW], ge[:, GW:]
    vh, r = _rms_stats(v)
    return uv, u, vh, r, vh * gv


def _tril():
    rows = lax.broadcasted_iota(jnp.int32, (CHUNK, CHUNK), 0)
    cols = lax.broadcasted_iota(jnp.int32, (CHUNK, CHUNK), 1)
    return rows >= cols


def _gmlp(p0, a_out, ws, bst, gv, SW):
    T, PW = p0.shape
    GW = (PW - SW) // 2
    H = GW // GMLP_HEAD
    D = SW + GW

    def body(p_ref, a_ref, ws_ref, b_ref, gv_ref, o_ref):
        _, u, _, _, vn = _gmlp_parts(p_ref[...], gv_ref[...], SW, GW)
        tri = _tril()
        o_ref[:, :SW] = a_ref[...]
        for hh in range(H):
            sl = slice(hh * GMLP_HEAD, (hh + 1) * GMLP_HEAD)
            wm = jnp.where(tri, ws_ref[hh], 0.0).astype(bf16)
            gate = _dot(wm, vn[:, sl].astype(bf16)) + b_ref[:, hh:hh + 1]
            o_ref[:, SW + hh * GMLP_HEAD:SW + (hh + 1) * GMLP_HEAD] = (u[:, sl] * gate).astype(bf16)

    return pl.pallas_call(
        body, name="gmlp", grid=(T // CHUNK,),
        in_specs=[pl.BlockSpec((CHUNK, PW), lambda i: (i, 0)), pl.BlockSpec((CHUNK, SW), lambda i: (i, 0)),
                  pl.BlockSpec((H, CHUNK, CHUNK), lambda i: (0, 0, 0)), pl.BlockSpec((CHUNK, H), lambda i: (0, 0)),
                  pl.BlockSpec((1, GW), lambda i: (0, 0))],
        out_specs=pl.BlockSpec((CHUNK, D), lambda i: (i, 0)),
        out_shape=jax.ShapeDtypeStruct((T, D), bf16),
        compiler_params=_params(("parallel",)))(p0, a_out, ws, bst, gv)


def _gmlp_bwd(p0, dmix, ws, bst, gv, SW):
    T, PW = p0.shape
    GW = (PW - SW) // 2
    H = GW // GMLP_HEAD
    D = SW + GW

    def body(p_ref, d_ref, ws_ref, b_ref, gv_ref, duv_ref, dws_ref, dbs_ref, dgv_ref):
        gv_ = gv_ref[...]
        uv, u, vh, r, vn = _gmlp_parts(p_ref[...], gv_, SW, GW)
        dout = d_ref[...][:, SW:].astype(f32)
        tri = _tril()

        @pl.when(pl.program_id(0) == 0)
        def _():
            dws_ref[...] = jnp.zeros_like(dws_ref)
            dbs_ref[...] = jnp.zeros_like(dbs_ref)
            dgv_ref[...] = jnp.zeros_like(dgv_ref)
        du, dvn = [], []
        for hh in range(H):
            sl = slice(hh * GMLP_HEAD, (hh + 1) * GMLP_HEAD)
            wm = jnp.where(tri, ws_ref[hh], 0.0).astype(bf16)
            vnh = vn[:, sl].astype(bf16)
            gate = _dot(wm, vnh) + b_ref[:, hh:hh + 1]
            dgate = dout[:, sl] * u[:, sl]
            du.append(dout[:, sl] * gate)
            dgb = dgate.astype(bf16)
            dws_ref[hh] += jnp.where(tri, _dg(dgb, vnh, NT), 0.0)
            dbs_ref[hh] += jnp.broadcast_to(jnp.sum(dgate, axis=1, keepdims=True), (CHUNK, CHUNK))
            dvn.append(_dg(wm, dgb, TN))
        dvn = jnp.concatenate(dvn, axis=1)
        dv, dgv = _rms_bwd(dvn, vh, r, gv_)
        dgv_ref[...] += dgv
        dge = jnp.concatenate(du + [dv], axis=1)
        duv_ref[...] = (dge * _gelu_grad(uv)).astype(bf16)

    return pl.pallas_call(
        body, name="gmlp_bwd", grid=(T // CHUNK,),
        in_specs=[pl.BlockSpec((CHUNK, PW), lambda i: (i, 0)), pl.BlockSpec((CHUNK, D), lambda i: (i, 0)),
                  pl.BlockSpec((H, CHUNK, CHUNK), lambda i: (0, 0, 0)), pl.BlockSpec((CHUNK, H), lambda i: (0, 0)),
                  pl.BlockSpec((1, GW), lambda i: (0, 0))],
        out_specs=[pl.BlockSpec((CHUNK, 2 * GW), lambda i: (i, 0)), pl.BlockSpec((H, CHUNK, CHUNK), lambda i: (0, 0, 0)),
                   pl.BlockSpec((H, CHUNK, CHUNK), lambda i: (0, 0, 0)), pl.BlockSpec((1, GW), lambda i: (0, 0))],
        out_shape=[jax.ShapeDtypeStruct((T, 2 * GW), bf16), jax.ShapeDtypeStruct((H, CHUNK, CHUNK), f32),
                   jax.ShapeDtypeStruct((H, CHUNK, CHUNK), f32), jax.ShapeDtypeStruct((1, GW), f32)],
        compiler_params=_params(("arbitrary",)))(p0, dmix, ws, bst, gv)


def _s5_disc(lr, li, ldt):
    lr = jnp.minimum(lr, LAMBDA_RE_MAX)
    dt = jnp.exp(ldt)
    mag = jnp.exp(lr * dt)
    ar = mag * jnp.cos(li * dt)
    ai = mag * jnp.sin(li * dt)
    den = lr * lr + li * li
    nr = ar - 1.0
    zr = (nr * lr + ai * li) / den
    zi = (ai * lr - nr * li) / den
    return ar, ai, zr, zi


def _s5_prep(lr, li, ldt):
    G, P = lr.shape

    def body(lr_ref, li_ref, ldt_ref, ar_ref, ai_ref, zr_ref, zi_ref):
        ar, ai, zr, zi = _s5_disc(lr_ref[...], li_ref[...], ldt_ref[...])
        ar_ref[...] = ar
        ai_ref[...] = ai
        zr_ref[...] = zr
        zi_ref[...] = zi

    s = jax.ShapeDtypeStruct((G, P), f32)
    return pl.pallas_call(body, name="s5_prep", out_shape=[s, s, s, s])(lr, li, ldt)


def _s5_prep_bwd(lr, li, ldt, dar, dai, dzr, dzi):
    G, P = lr.shape

    def body(lr_ref, li_ref, ldt_ref, dar_ref, dai_ref, dzr_ref, dzi_ref, o1, o2, o3):
        _, vjp = jax.vjp(_s5_disc, lr_ref[...], li_ref[...], ldt_ref[...])
        cts = tuple(jnp.sum(r[...], axis=0) for r in (dar_ref, dai_ref, dzr_ref, dzi_ref))
        a, b, c = vjp(cts)
        o1[...] = a
        o2[...] = b
        o3[...] = c

    s = jax.ShapeDtypeStruct((G, P), f32)
    return pl.pallas_call(body, name="s5_prep_bwd", out_shape=[s, s, jax.ShapeDtypeStruct((G, 1), f32)])(
        lr, li, ldt, dar, dai, dzr, dzi)


def _s5_bbd(zr, zi, bre, bim):
    SW, NS = bre.shape

    def body(zr_ref, zi_ref, br_ref, bi_ref, o_ref):
        zr_, zi_, br, bi = zr_ref[...], zi_ref[...], br_ref[...], bi_ref[...]
        o_ref[:, :NS] = (zr_ * br - zi_ * bi).astype(bf16)
        o_ref[:, NS:] = (zr_ * bi + zi_ * br).astype(bf16)

    return pl.pallas_call(body, name="s5_bbd", out_shape=jax.ShapeDtypeStruct((SW, 2 * NS), bf16))(zr, zi, bre, bim)


def _s5_bbd_bwd(dbbd, zr, zi, bre, bim):
    SW, NS = bre.shape

    def body(d_ref, zr_ref, zi_ref, br_ref, bi_ref, dbr_ref, dbi_ref, dzr_ref, dzi_ref):
        zr_, zi_, br, bi = zr_ref[...], zi_ref[...], br_ref[...], bi_ref[...]
        dr, di = d_ref[:, :NS], d_ref[:, NS:]
        dbr_ref[...] = zr_ * dr + zi_ * di
        dbi_ref[...] = zr_ * di - zi_ * dr
        dzr_ref[...] = jnp.sum(dr * br + di * bi, axis=0, keepdims=True)
        dzi_ref[...] = jnp.sum(di * br - dr * bi, axis=0, keepdims=True)

    m = jax.ShapeDtypeStruct((SW, NS), f32)
    v = jax.ShapeDtypeStruct((1, NS), f32)
    return pl.pallas_call(body, name="s5_bbd_bwd", out_shape=[m, m, v, v])(dbbd, zr, zi, bre, bim)


def _slab_cat(ref, NB):
    return jnp.concatenate([ref[j] for j in range(NB)], axis=1)


def _s5_in(p3, bbd, SW, tm):
    B, S, PW = p3.shape
    NS = bbd.shape[1] // 2
    NB = NS // LANES

    def body(u_ref, b_ref, xr_ref, xi_ref):
        x = _dot(u_ref[...], b_ref[...])
        for j in range(NB):
            xr_ref[j] = x[:, j * LANES:(j + 1) * LANES]
            xi_ref[j] = x[:, NS + j * LANES:NS + (j + 1) * LANES]

    slab = jax.ShapeDtypeStruct((B, NB, S, LANES), f32)
    sspec = pl.BlockSpec((None, NB, tm, LANES), lambda b, i: (b, 0, i, 0))
    return pl.pallas_call(
        body, name="s5_in", grid=(B, S // tm),
        in_specs=[pl.BlockSpec((None, tm, SW), lambda b, i: (b, i, 0)), pl.BlockSpec((SW, 2 * NS), lambda b, i: (0, 0))],
        out_specs=[sspec, sspec], out_shape=[slab, slab],
        compiler_params=_params(("parallel", "parallel")))(p3, bbd)


def _s5_scan(name, xr, xi, ar, ai, reverse, hr=None, hi=None):
    B, NB, S, _ = xr.shape
    L = S // NSUB
    nb = 2 if (hr is None and NB % 2 == 0) else 1
    with_da = hr is not None

    def body(*refs):
        if with_da:
            xr_ref, xi_ref, ar_ref, ai_ref, hr_ref, hi_ref, or_ref, oi_ref, dar_ref, dai_ref, pr_scr, pi_scr = refs
        else:
            xr_ref, xi_ref, ar_ref, ai_ref, or_ref, oi_ref, pr_scr, pi_scr = refs
        sign = -1.0 if reverse else 1.0
        a_r = [jnp.broadcast_to(ar_ref[j], (NSUB, LANES)) for j in range(nb)]
        a_i = [jnp.broadcast_to(ai_ref[j], (NSUB, LANES)) * sign for j in range(nb)]

        def step(t, carry):
            row = (L - 1 - t) if reverse else t
            rows = pl.ds(row, NSUB, stride=L)
            out = []
            for j in range(nb):
                sr, si, pr, pi = carry[j]
                nr = a_r[j] * sr - a_i[j] * si + xr_ref.at[j][rows, :]
                ni = a_r[j] * si + a_i[j] * sr + xi_ref.at[j][rows, :]
                or_ref.at[j][rows, :] = nr
                oi_ref.at[j][rows, :] = ni
                npr = a_r[j] * pr - a_i[j] * pi
                npi = a_r[j] * pi + a_i[j] * pr
                pr_scr[j, pl.ds(row, 1), :] = npr[0:1]
                pi_scr[j, pl.ds(row, 1), :] = npi[0:1]
                out.append((nr, ni, npr, npi))
            return tuple(out)
        z = jnp.zeros((NSUB, LANES), f32)
        one = jnp.ones((NSUB, LANES), f32)
        fin = lax.fori_loop(0, L, step, tuple((z, z, one, z) for _ in range(nb)))

        for j in range(nb):
            sr, si, plr, pli = fin[j]
            plr, pli = plr[0:1], pli[0:1]
            cr = jnp.zeros((1, LANES), f32)
            ci = jnp.zeros((1, LANES), f32)
            order = range(NSUB - 2, -1, -1) if reverse else range(1, NSUB)
            for c in order:
                src = c + 1 if reverse else c - 1
                cr, ci = (sr[src:src + 1] + plr * cr - pli * ci, si[src:src + 1] + plr * ci + pli * cr)
                rows = slice(c * L, (c + 1) * L)
                tr, ti = pr_scr[j], pi_scr[j]
                or_ref[j, rows, :] += tr * cr - ti * ci
                oi_ref[j, rows, :] += tr * ci + ti * cr
            if with_da:
                first = lax.broadcasted_iota(jnp.int32, (L, LANES), 0) == 0
                dar = jnp.zeros((1, LANES), f32)
                dai = jnp.zeros((1, LANES), f32)
                for c in range(NSUB):
                    rows = slice(c * L, (c + 1) * L)
                    if c == 0:
                        lr_, li_ = jnp.zeros((1, LANES), f32), jnp.zeros((1, LANES), f32)
                    else:
                        lr_, li_ = hr_ref[j, c * L - 1:c * L, :], hi_ref[j, c * L - 1:c * L, :]
                    hpr = jnp.where(first, lr_, pltpu.roll(hr_ref[j, rows, :], 1, 0))
                    hpi = jnp.where(first, li_, pltpu.roll(hi_ref[j, rows, :], 1, 0))
                    gr, gi = or_ref[j, rows, :], oi_ref[j, rows, :]
                    dar += jnp.sum(hpr * gr + hpi * gi, axis=0, keepdims=True)
                    dai += jnp.sum(hpr * gi - hpi * gr, axis=0, keepdims=True)
                dar_ref[j] = dar
                dai_ref[j] = dai

    slab = jax.ShapeDtypeStruct((B, NB, S, LANES), f32)
    sspec = pl.BlockSpec((None, nb, S, LANES), lambda b, j: (b, j, 0, 0))
    aspec = pl.BlockSpec((nb, 1, LANES), lambda b, j: (j, 0, 0))
    in_specs = [sspec, sspec, aspec, aspec]
    out_specs = [sspec, sspec]
    out_shape = [slab, slab]
    args = [xr, xi, ar, ai]
    if with_da:
        in_specs += [sspec, sspec]
        args += [hr, hi]
        dspec = pl.BlockSpec((None, nb, 1, LANES), lambda b, j: (b, j, 0, 0))
        out_specs += [dspec, dspec]
        out_shape += [jax.ShapeDtypeStruct((B, NB, 1, LANES), f32)] * 2
    return pl.pallas_call(
        body, name=name, grid=(B, NB // nb), in_specs=in_specs, out_specs=out_specs, out_shape=out_shape,
        scratch_shapes=[pltpu.VMEM((nb, L, LANES), f32), pltpu.VMEM((nb, L, LANES), f32)],
        compiler_params=_params(("parallel", "parallel")))(*args)


def _s5_out_parts(hr_ref, hi_ref, u_ref, cr_ref, ci_ref, d_ref, wg_ref, bg_ref, NB):
    hcr = _slab_cat(hr_ref, NB).astype(bf16)
    hci = _slab_cat(hi_ref, NB).astype(bf16)
    u = u_ref[...].astype(f32)
    y2 = _dot(hcr, cr_ref[...]) - _dot(hci, ci_ref[...]) + d_ref[...] * u
    yg = _gelu(y2)
    s = jax.nn.sigmoid(_dot(yg.astype(bf16), wg_ref[...]) + bg_ref[...])
    return hcr, hci, u, y2, yg, s


def _s5_out_specs(B, S, NB, NS, SW, tm):
    sspec = pl.BlockSpec((None, NB, tm, LANES), lambda b, i: (b, 0, i, 0))
    full = lambda r, c: pl.BlockSpec((r, c), lambda b, i: (0, 0))
    return sspec, [sspec, sspec, pl.BlockSpec((None, tm, SW), lambda b, i: (b, i, 0)), full(NS, SW), full(NS, SW),
                   full(1, SW), full(SW, SW), full(1, SW)]


def _s5_out(hr, hi, p3, cbr, cbi, dsk, wglu, bglu, tm):
    B, NB, S, _ = hr.shape
    NS, SW = cbr.shape

    def body(hr_ref, hi_ref, u_ref, cr_ref, ci_ref, d_ref, wg_ref, bg_ref, o_ref):
        _, _, _, _, yg, s = _s5_out_parts(hr_ref, hi_ref, u_ref, cr_ref, ci_ref, d_ref, wg_ref, bg_ref, NB)
        o_ref[...] = (yg * s).astype(bf16)

    _, in_specs = _s5_out_specs(B, S, NB, NS, SW, tm)
    return pl.pallas_call(
        body, name="s5_out", grid=(B, S // tm), in_specs=in_specs,
        out_specs=pl.BlockSpec((None, tm, SW), lambda b, i: (b, i, 0)),
        out_shape=jax.ShapeDtypeStruct((B, S, SW), bf16),
        compiler_params=_params(("parallel", "parallel")))(hr, hi, p3, cbr, cbi, dsk, wglu, bglu)


def _s5_out_bwd(hr, hi, p3, dmix3, cbr, cbi, dsk, wglu, bglu, tm):
    B, NB, S, _ = hr.shape
    NS, SW = cbr.shape

    def body(hr_ref, hi_ref, u_ref, cr_ref, ci_ref, d_ref, wg_ref, bg_ref, da_ref,
             dhr_ref, dhi_ref, du_ref, dcr_ref, dci_ref, dd_ref, dwg_ref, dbg_ref):
        hcr, hci, u, y2, yg, s = _s5_out_parts(hr_ref, hi_ref, u_ref, cr_ref, ci_ref, d_ref, wg_ref, bg_ref, NB)
        da = da_ref[...].astype(f32)
        dz = da * yg * s * (1.0 - s)
        dzb = dz.astype(bf16)
        dyg = da * s + _dg(dzb, wg_ref[...], NT)
        dy2 = dyg * _gelu_grad(y2)
        dyb = dy2.astype(bf16)

        @pl.when((pl.program_id(0) == 0) & (pl.program_id(1) == 0))
        def _():
            for r in (dcr_ref, dci_ref, dd_ref, dwg_ref, dbg_ref):
                r[...] = jnp.zeros_like(r)
        dwg_ref[...] += _dg(yg.astype(bf16), dzb, TN)
        dbg_ref[...] += jnp.sum(dz, axis=0, keepdims=True)
        dd_ref[...] += jnp.sum(dy2 * u, axis=0, keepdims=True)
        dcr_ref[...] += _dg(hcr, dyb, TN)
        dci_ref[...] -= _dg(hci, dyb, TN)
        du_ref[...] = dy2 * d_ref[...]
        dhr = _dg(dyb, cr_ref[...], NT)
        dhi = _dg(dyb, ci_ref[...], NT)
        for j in range(NB):
            dhr_ref[j] = dhr[:, j * LANES:(j + 1) * LANES]
            dhi_ref[j] = -dhi[:, j * LANES:(j + 1) * LANES]

    sspec, in_specs = _s5_out_specs(B, S, NB, NS, SW, tm)
    in_specs = in_specs + [pl.BlockSpec((None, tm, SW), lambda b, i: (b, i, 0))]
    full = lambda r, c: pl.BlockSpec((r, c), lambda b, i: (0, 0))
    slab = jax.ShapeDtypeStruct((B, NB, S, LANES), f32)
    mat = lambda r, c: jax.ShapeDtypeStruct((r, c), f32)
    return pl.pallas_call(
        body, name="s5_out_bwd", grid=(B, S // tm), in_specs=in_specs,
        out_specs=[sspec, sspec, pl.BlockSpec((None, tm, SW), lambda b, i: (b, i, 0)), full(NS, SW), full(NS, SW),
                   full(1, SW), full(SW, SW), full(1, SW)],
        out_shape=[slab, slab, jax.ShapeDtypeStruct((B, S, SW), f32), mat(NS, SW), mat(NS, SW), mat(1, SW),
                   mat(SW, SW), mat(1, SW)],
        compiler_params=_params(("arbitrary", "arbitrary")))(hr, hi, p3, cbr, cbi, dsk, wglu, bglu, dmix3)


def _s5_in_bwd(gr, gi, p3, bbd, du_skip, duv3, tm):
    B, NB, S, _ = gr.shape
    SW, NS2 = bbd.shape
    PW = SW + duv3.shape[2]

    def body(gr_ref, gi_ref, u_ref, b_ref, ds_ref, duv_ref, dp_ref, db_ref):
        g = jnp.concatenate([_slab_cat(gr_ref, NB), _slab_cat(gi_ref, NB)], axis=1).astype(bf16)
        du = _dg(g, b_ref[...], NT) + ds_ref[...]
        dp_ref[:, :SW] = du.astype(bf16)
        dp_ref[:, SW:] = duv_ref[...]

        @pl.when((pl.program_id(0) == 0) & (pl.program_id(1) == 0))
        def _():
            db_ref[...] = jnp.zeros_like(db_ref)
        db_ref[...] += _dg(u_ref[...], g, TN)

    sspec = pl.BlockSpec((None, NB, tm, LANES), lambda b, i: (b, 0, i, 0))
    row = lambda c: pl.BlockSpec((None, tm, c), lambda b, i: (b, i, 0))
    return pl.pallas_call(
        body, name="s5_in_bwd", grid=(B, S // tm),
        in_specs=[sspec, sspec, row(SW), pl.BlockSpec((SW, NS2), lambda b, i: (0, 0)), row(SW), row(PW - SW)],
        out_specs=[row(PW), pl.BlockSpec((SW, NS2), lambda b, i: (0, 0))],
        out_shape=[jax.ShapeDtypeStruct((B, S, PW), bf16), jax.ShapeDtypeStruct((SW, NS2), f32)],
        compiler_params=_params(("arbitrary", "arbitrary")))(gr, gi, p3, bbd, du_skip, duv3)


BIG = ['ev_w_in', 'ev_w_out', 'od_w_in', 'od_w_out', 'ffn_w_up', 'ffn_w_down']
ANY = pl.BlockSpec(memory_space=pl.ANY)


def _rtile(rows, mult):
    best = None
    for d in range(mult, min(rows, 512) + 1, mult):
        if rows % d == 0:
            best = d
    assert best is not None, (rows, mult)
    return best


def _pair_sum(name, g, recv, c_idx, out_dtype):
    NCH, R, W = g.shape
    HALF_W = W // 2
    tr = _rtile(R, 16)

    def body(c_ref, a_ref, b_ref, o_ref):
        o_ref[...] = (a_ref[...] + b_ref[...]).astype(out_dtype)

    return pl.pallas_call(
        body, name=name,
        grid_spec=pltpu.PrefetchScalarGridSpec(
            num_scalar_prefetch=1, grid=(NCH, R // tr),
            in_specs=[pl.BlockSpec((None, tr, HALF_W), lambda j, i, c: (j, i, c[0])),
                      pl.BlockSpec((None, tr, HALF_W), lambda j, i, c: (j, i, 0))],
            out_specs=pl.BlockSpec((None, tr, HALF_W), lambda j, i, c: (j, i, 0))),
        out_shape=jax.ShapeDtypeStruct((NCH, R, HALF_W), out_dtype),
        compiler_params=_params(("parallel", "parallel")))(c_idx, g, recv)


def _chip_sum(name, r3, h, k_idx):
    NCH, R, Wh = r3.shape
    tr = _rtile(R, 16)

    def body(k_ref, a_ref, own_ref, o_ref):
        own = own_ref[...].astype(f32)
        t = [jnp.where(k_ref[0] == s, own, a_ref[s].astype(f32)) for s in range(NCH)]
        o_ref[...] = ((t[0] + t[1]) + t[2]) + t[3]

    return pl.pallas_call(
        body, name=name,
        grid_spec=pltpu.PrefetchScalarGridSpec(
            num_scalar_prefetch=1, grid=(R // tr,),
            in_specs=[pl.BlockSpec((NCH, tr, Wh), lambda i, k: (0, i, 0)),
                      pl.BlockSpec((None, tr, Wh), lambda i, k: (k[0], i, 0))],
            out_specs=pl.BlockSpec((tr, Wh), lambda i, k: (i, 0))),
        out_shape=jax.ShapeDtypeStruct((R, Wh), f32),
        compiler_params=_params(("parallel",)))(k_idx, r3, h)


def _adam_math(gg, w, m, v):
    nm = ADAM_B1 * m + (1.0 - ADAM_B1) * gg
    nv = ADAM_B2 * v + (1.0 - ADAM_B2) * jnp.square(gg)
    m_hat = nm / (1.0 - ADAM_B1 ** ADAM_STEP)
    v_hat = nv / (1.0 - ADAM_B2 ** ADAM_STEP)
    return -ADAM_LR * (m_hat / (jnp.sqrt(v_hat) + ADAM_EPS) + ADAM_WD * w), nm, nv


def _adamw(name, mine, theirs, c_idx, w, m, v, lead, transposed, prev=None):
    L, R, W = w.shape
    if transposed:
        bw = LANES if W % LANES == 0 else W
        gspec = pl.BlockSpec((bw, R // 2), lambda i, hf, c: (i, 0))
        wspec = pl.BlockSpec((None, R // 2, bw), lambda i, hf, c: (lead, hf, i))
        grid = (W // bw, 2)
    else:
        tr = _rtile(R, SUBLANES)
        gspec = pl.BlockSpec((tr, W // 2), lambda i, hf, c: (i, 0))
        wspec = pl.BlockSpec((None, tr, W // 2), lambda i, hf, c: (lead, i, hf))
        grid = (R // tr, 2)

    def body(c_ref, a_ref, b_ref, w_ref, m_ref, v_ref, *rest):
        go_ref, d_ref, nm_ref, nv_ref = rest[-4:]
        gg = jnp.where(pl.program_id(1) == c_ref[0], a_ref[...], b_ref[...])
        if transposed:
            gg = gg.T
        d, nm, nv = _adam_math(gg, w_ref[...], m_ref[...], v_ref[...])
        go_ref[...] = gg
        d_ref[...] = d
        nm_ref[...] = nm
        nv_ref[...] = nv

    in_specs = [gspec, gspec, wspec, wspec, wspec]
    args, aliases = [c_idx, mine, theirs, w, m, v], {}
    if prev is not None:
        in_specs += [ANY] * 4
        args += list(prev)
        aliases = {6: 0, 7: 1, 8: 2, 9: 3}
    s = jax.ShapeDtypeStruct((L, R, W), f32)
    return pl.pallas_call(
        body, name=name,
        grid_spec=pltpu.PrefetchScalarGridSpec(num_scalar_prefetch=1, grid=grid, in_specs=in_specs,
                                               out_specs=[wspec] * 4),
        out_shape=[s, s, s, s], input_output_aliases=aliases,
        compiler_params=_params(("parallel", "arbitrary")))(*args)


def _place():
    x, y, c = lax.axis_index("x"), lax.axis_index("y"), lax.axis_index("c")
    return x, y, c, [(1 - x, y), (x, 1 - y), (1 - x, 1 - y)]


def _gathered_shape(sh, kind):
    if kind == "rows":
        return sh[:-2] + (N_CHIPS * sh[-2], sh[-1])
    if kind == "cols":
        return sh[:-1] + (N_CHIPS * sh[-1],)
    return (N_CHIPS,) + sh


def _place_shard(name, shard, kind, k_idx):
    sh = shard.shape
    r, C = sh[-2], sh[-1]
    L = sh[0] if len(sh) == 3 else 1
    tr = _rtile(r, 16)
    nr = r // tr
    if kind == "rows":
        out3, omap = (L, N_CHIPS * r, C), lambda l, i, k: (l, k[0] * nr + i, 0)
    elif kind == "cols":
        out3, omap = (L, r, N_CHIPS * C), lambda l, i, k: (l, i, k[0])
    else:
        out3, omap = (N_CHIPS, r, C), lambda l, i, k: (k[0], i, 0)

    def body(k_ref, s_ref, o_ref):
        o_ref[...] = s_ref[...]

    out = pl.pallas_call(
        body, name=name,
        grid_spec=pltpu.PrefetchScalarGridSpec(
            num_scalar_prefetch=1, grid=(L, nr),
            in_specs=[pl.BlockSpec((None, tr, C), lambda l, i, k: (l, i, 0))],
            out_specs=pl.BlockSpec((None, tr, C), omap)),
        out_shape=jax.ShapeDtypeStruct(out3, shard.dtype),
        compiler_params=_params(("parallel", "parallel")))(k_idx, shard.reshape(L, r, C))
    return out.reshape(_gathered_shape(sh, kind))


def _comm_gather(shards, fulls, kinds):
    n = len(shards)
    shapes = [s.shape for s in shards]

    def window(ref, a, k, h=None):
        sh, kind = shapes[a], kinds[a]
        r = sh[-2]
        start, size = (0, r) if h is None else (h * (r // 2), r // 2)
        lead = (slice(None),) * (len(sh) - 2)
        if kind == "rows":
            return ref.at[lead + (pl.ds(k * r + start, size), slice(None))]
        if kind == "cols":
            return ref.at[lead + (pl.ds(start, size), pl.ds(pl.multiple_of(k * sh[-1], LANES), sh[-1]))]
        return ref.at[(k,) + lead + (pl.ds(start, size), slice(None))]

    def body(*refs):
        s_refs, o_refs = refs[:n], refs[2 * n:3 * n]
        send_sems, recv_sems = refs[3 * n:]
        x, y, c, chips = _place()
        me, sibling = (x, y, c), (x, y, 1 - c)
        k = 2 * x + y

        def copy(a, j, kk, hh, to, src=None):
            dst = window(o_refs[a], a, kk, hh)
            return pltpu.make_async_remote_copy(
                src_ref=dst if src is None else src, dst_ref=dst, send_sem=send_sems.at[6 * a + j],
                recv_sem=recv_sems.at[6 * a + j], device_id=to, device_id_type=MESH)

        first = []
        for a in range(n):
            r = shapes[a][-2]
            lead = (slice(None),) * (len(shapes[a]) - 2)
            src = s_refs[a].at[lead + (pl.ds(c * (r // 2), r // 2), slice(None))]
            first += [copy(a, j, k, c, (*chip, c), src=src) for j, chip in enumerate(chips)]
        for cp in first:
            cp.start()
        passed = []
        for j, (cx, cy) in enumerate(chips):
            for a in range(n):
                copy(a, j, 2 * cx + cy, c, me).wait_recv()
                fwd = copy(a, 3 + j, 2 * cx + cy, c, sibling)
                fwd.start()
                passed.append(fwd)
        for j, (cx, cy) in enumerate(chips):
            for a in range(n):
                copy(a, 3 + j, 2 * cx + cy, 1 - c, me).wait_recv()
        for cp in first + passed:
            cp.wait_send()

    return pl.pallas_call(
        body, name="comm_gather", in_specs=[ANY] * (2 * n), out_specs=[ANY] * n,
        out_shape=[jax.ShapeDtypeStruct(f.shape, f.dtype) for f in fulls],
        input_output_aliases={n + a: a for a in range(n)},
        scratch_shapes=[pltpu.SemaphoreType.DMA((6 * n,)), pltpu.SemaphoreType.DMA((6 * n,))])(*shards, *fulls)


def _comm_pair_swap(gs):
    n = len(gs)

    def body(*refs):
        g_refs, o_refs, send_sems, recv_sems = refs[:n], refs[n:2 * n], refs[2 * n], refs[2 * n + 1]
        x, y, c, _ = _place()
        half = [g.shape[2] // 2 for g in gs]
        cps = [pltpu.make_async_remote_copy(
            src_ref=g_refs[a].at[:, :, pl.ds(pl.multiple_of((1 - c) * half[a], LANES), half[a])], dst_ref=o_refs[a], send_sem=send_sems.at[a],
            recv_sem=recv_sems.at[a], device_id=(x, y, 1 - c), device_id_type=MESH) for a in range(n)]
        for cp in cps:
            cp.start()
        for cp in cps:
            cp.wait()

    return pl.pallas_call(
        body, name="comm_pair_swap", in_specs=[ANY] * n, out_specs=[ANY] * n,
        out_shape=[jax.ShapeDtypeStruct(g.shape[:2] + (g.shape[2] // 2,), g.dtype) for g in gs],
        scratch_shapes=[pltpu.SemaphoreType.DMA((n,)), pltpu.SemaphoreType.DMA((n,))])(*gs)


def _comm_chip_exchange(hs):
    n = len(hs)

    def body(*refs):
        h_refs, o_refs = refs[:n], refs[n:2 * n]
        send_sems, recv_sems = refs[2 * n:]
        x, y, c, chips = _place()
        k = 2 * x + y

        def copy(a, j, src_slot, dst_slot):
            cx, cy = chips[j]
            return pltpu.make_async_remote_copy(
                src_ref=h_refs[a].at[src_slot], dst_ref=o_refs[a].at[dst_slot], send_sem=send_sems.at[3 * a + j],
                recv_sem=recv_sems.at[3 * a + j], device_id=(cx, cy, c), device_id_type=MESH)

        sends = [copy(a, j, 2 * cx + cy, k) for a in range(n) for j, (cx, cy) in enumerate(chips)]
        for cp in sends:
            cp.start()
        for a in range(n):
            for j, (cx, cy) in enumerate(chips):
                copy(a, j, k, 2 * cx + cy).wait_recv()
        for cp in sends:
            cp.wait_send()

    return pl.pallas_call(
        body, name="comm_chip_exchange", in_specs=[ANY] * n, out_specs=[ANY] * n,
        out_shape=[jax.ShapeDtypeStruct(h.shape, h.dtype) for h in hs],
        scratch_shapes=[pltpu.SemaphoreType.DMA((3 * n,)), pltpu.SemaphoreType.DMA((3 * n,))])(*hs)


def _comm_pair_share(gs):
    n = len(gs)

    def body(*refs):
        g_refs, o_refs, send_sems, recv_sems = refs[:n], refs[n:2 * n], refs[2 * n], refs[2 * n + 1]
        x, y, c, _ = _place()
        cps = [pltpu.make_async_remote_copy(
            src_ref=g_refs[a], dst_ref=o_refs[a], send_sem=send_sems.at[a], recv_sem=recv_sems.at[a],
            device_id=(x, y, 1 - c), device_id_type=MESH) for a in range(n)]
        for cp in cps:
            cp.start()
        for cp in cps:
            cp.wait()

    return pl.pallas_call(
        body, name="comm_pair_share", in_specs=[ANY] * n, out_specs=[ANY] * n,
        out_shape=[jax.ShapeDtypeStruct(g.shape, g.dtype) for g in gs],
        scratch_shapes=[pltpu.SemaphoreType.DMA((n,)), pltpu.SemaphoreType.DMA((n,))])(*gs)


def _pad_rows(flat, unit):
    n = flat.shape[-1]
    pad = (-n) % unit
    if pad:
        flat = jnp.pad(flat, [(0, 0)] * (flat.ndim - 1) + [(0, pad)])
    return flat


def _split_chips(full, axis):
    sh = full.shape
    t = full.reshape(sh[:axis] + (N_CHIPS, sh[axis] // N_CHIPS) + sh[axis + 1:])
    return jnp.moveaxis(t, axis, 0).reshape(N_CHIPS, -1)


def _join_chips(stack, shard_shape, axis):
    t = jnp.moveaxis(stack.reshape((N_CHIPS,) + tuple(shard_shape)), 0, axis)
    sh = t.shape
    return t.reshape(sh[:axis] + (sh[axis] * sh[axis + 1],) + sh[axis + 2:])


def _block_diag(blocks):
    G, r, c = blocks.shape
    eye = jnp.eye(G, dtype=blocks.dtype)
    return (blocks[:, :, None, :] * eye[:, None, :, None]).reshape(G * r, G * c)


def _diag_blocks(m, G):
    r, c = m.shape[0] // G, m.shape[1] // G
    idx = jnp.arange(G)
    return m.reshape(G, r, G, c)[idx, :, idx, :]


def _gather_weights(w, k_idx):
    kinds = {'ev_w_in': "chip", 'ev_w_out': "rows", 's5_w_glu': "rows", 'od_w_in': "cols", 'od_w_out': "rows",
             'ffn_w_up': "cols", 'ffn_w_down': "rows"}
    conv = jnp.concatenate([w[n].reshape(-1) for n in GATHER_F32])
    conv = _pad_rows(conv, 2 * SUBLANES * LANES).reshape(-1, LANES)
    shards = [w[n].astype(bf16) if w[n].shape[0] > 1 else w[n][0].astype(bf16) for n in GATHER_BF16] + [conv]
    kind_list = [kinds[n] for n in GATHER_BF16] + ["chip"]
    fulls = [_place_shard("place_" + n, s, kd, k_idx) for n, s, kd in zip(GATHER_BF16 + ["conv"], shards, kind_list)]
    got = _comm_gather(shards, fulls, kind_list)
    out = {}
    for n, g in zip(GATHER_BF16, got[:-1]):
        if kinds[n] == "chip":
            g = jnp.swapaxes(g, 0, 1).reshape(g.shape[1], -1)
        out[n] = g if w[n].shape[0] > 1 else g[None]
    allc, off = got[-1].reshape(N_CHIPS, -1), 0
    for n in GATHER_F32:
        out[n] = _join_chips(allc[:, off:off + w[n].size], w[n].shape, SHARD_AXIS[n])
        off += w[n].size
    return out


def kernel(x, mix_norm_g, ffn_norm_g, final_norm_g, ev_w_in, ev_w_out, s5_lam_re, s5_lam_im, s5_log_dt, s5_b_re, s5_b_im, s5_c_re, s5_c_im, s5_d, s5_w_glu, s5_b_glu, gm_w_s, gm_b_s, gm_v_g, od_w_in, od_conv_w, od_conv_b, od_w_out, ffn_w_up, ffn_conv_w, ffn_conv_b, ffn_w_down, loss_target, m_mix_norm_g, m_ffn_norm_g, m_final_norm_g, m_ev_w_in, m_ev_w_out, m_s5_lam_re, m_s5_lam_im, m_s5_log_dt, m_s5_b_re, m_s5_b_im, m_s5_c_re, m_s5_c_im, m_s5_d, m_s5_w_glu, m_s5_b_glu, m_gm_w_s, m_gm_b_s, m_gm_v_g, m_od_w_in, m_od_conv_w, m_od_conv_b, m_od_w_out, m_ffn_w_up, m_ffn_conv_w, m_ffn_conv_b, m_ffn_w_down, v_mix_norm_g, v_ffn_norm_g, v_final_norm_g, v_ev_w_in, v_ev_w_out, v_s5_lam_re, v_s5_lam_im, v_s5_log_dt, v_s5_b_re, v_s5_b_im, v_s5_c_re, v_s5_c_im, v_s5_d, v_s5_w_glu, v_s5_b_glu, v_gm_w_s, v_gm_b_s, v_gm_v_g, v_od_w_in, v_od_conv_w, v_od_conv_b, v_od_w_out, v_ffn_w_up, v_ffn_conv_w, v_ffn_conv_b, v_ffn_w_down):
    loc = dict(locals())
    w = {n: loc[n] for n in WEIGHTS}
    mom = {n: loc["m_" + n] for n in WEIGHTS}
    var = {n: loc["v_" + n] for n in WEIGHTS}

    B, S, D = x.shape
    T = B * S
    SW = s5_d.shape[1]
    G = SW // SSM_GROUP
    NS = G * SSM_STATE
    NB = NS // LANES
    tm = min(256, S)
    tt = min(1024, T)
    c_idx = lax.axis_index("c").astype(jnp.int32).reshape(1)
    k_idx = (2 * lax.axis_index("x") + lax.axis_index("y")).astype(jnp.int32).reshape(1)
    full = _gather_weights(w, k_idx)

    h0 = x.reshape(T, D)
    w_ev_in = full['ev_w_in'][0]
    w_ev_out = full['ev_w_out'][0]
    y0, p0 = _norm_mm("ev_in", h0, mix_norm_g[0], w_ev_in, tm)
    PW = p0.shape[1]
    p03 = p0.reshape(B, S, PW)
    lr, li, ldt = s5_lam_re[0], s5_lam_im[0], s5_log_dt[0].reshape(G, 1)
    ar, ai, zr, zi = _s5_prep(lr, li, ldt)
    bre = _block_diag(jnp.swapaxes(s5_b_re[0], 1, 2))
    bim = _block_diag(jnp.swapaxes(s5_b_im[0], 1, 2))
    cbr = _block_diag(jnp.swapaxes(s5_c_re[0], 1, 2)).astype(bf16)
    cbi = _block_diag(jnp.swapaxes(s5_c_im[0], 1, 2)).astype(bf16)
    zr_row, zi_row = zr.reshape(1, NS), zi.reshape(1, NS)
    bbd = _s5_bbd(zr_row, zi_row, bre, bim)
    ar_s, ai_s = ar.reshape(NB, 1, LANES), ai.reshape(NB, 1, LANES)
    xr, xi = _s5_in(p03, bbd, SW, tm)
    hr, hi = _s5_scan("s5_scan", xr, xi, ar_s, ai_s, False)
    w_glu = full['s5_w_glu'][0]
    dsk, bglu = s5_d.reshape(1, SW), s5_b_glu.reshape(1, SW)
    a_out = _s5_out(hr, hi, p03, cbr, cbi, dsk, w_glu, bglu, tm)
    ws, bst, gv = gm_w_s[0], gm_b_s[0].T, gm_v_g.reshape(1, -1)
    mixcat = _gmlp(p0, a_out.reshape(T, SW), ws, bst, gv, SW)
    h1 = _mm_resid("ev_out", mixcat, w_ev_out, h0, tm)

    def ffn_fwd(l, h):
        z, up = _norm_mm(f"ffn_up{l}", h, ffn_norm_g[l], full['ffn_w_up'][l], tm)
        up3 = up.reshape(B, S, -1)
        act = _ffn_act(f"ffn_act{l}", up3, full['ffn_conv_w'][l], ffn_conv_b[l].reshape(1, -1))
        hn = _mm_resid(f"ffn_down{l}", act.reshape(T, -1), full['ffn_w_down'][l], h, tm)
        return hn, (z, up3, act)

    h2, ffn0 = ffn_fwd(0, h1)
    w_od_in, w_od_out = full['od_w_in'][0], full['od_w_out'][0]
    od_cw, od_cb = full['od_conv_w'][0], full['od_conv_b']
    y1, p1 = _norm_mm("od_in", h2, mix_norm_g[1], w_od_in, tm)
    p13 = p1.reshape(B, S, -1)
    sc = _od_act(p13, od_cw, od_cb)
    h3 = _mm_resid("od_out", sc.reshape(T, D), w_od_out, h2, tm)
    h4, ffn1 = ffn_fwd(1, h3)

    dh4, dh4b, loss_part, d_final_g = _final_loss(h4, final_norm_g, loss_target.reshape(T, D), tm)
    loss = lax.psum(loss_part[0, 0], ("x", "y", "c"))

    grads = {}

    def ffn_bwd(l, dh, dhb, h_in, saved):
        z, up3, act = saved
        w_down, w_up = full['ffn_w_down'][l], full['ffn_w_up'][l]
        da = _mm_nt(f"ffn_down_bwd{l}", dhb, w_down, tm)
        g_down = _mm_tn(f"ffn_down_dw{l}", act.reshape(T, -1), dhb, tt)
        dg3, dv3, dcwg, dcwv, dcbg, dcbv = _ffn_act_bwd(
            f"ffn_act_bwd{l}", up3, da.reshape(B, S, -1), full['ffn_conv_w'][l], ffn_conv_b[l].reshape(1, -1))
        dupg, dupv = dg3.reshape(T, -1), dv3.reshape(T, -1)
        F = dupg.shape[1]
        g_up = _mm_tn(f"ffn_up_dw{l}_gate", dupg, z, tt, rows=2 * F)
        g_up = _mm_tn(f"ffn_up_dw{l}_val", dupv, z, tt, rows=2 * F, row_off=F, prev=g_up)
        dh_new, dhb_new, dg = _mm_nt_normbwd(f"ffn_up_bwd{l}", [dupg, dupv], w_up, h_in, ffn_norm_g[l], dh, tm)
        dcw = jnp.concatenate([dcwg, dcwv], axis=1)
        dcb = jnp.concatenate([dcbg, dcbv], axis=1)
        return dh_new, dhb_new, g_down, g_up, dcw, dcb[0], dg[0]

    dh3, dh3b, gd1, gu1, gcw1, gcb1, gng1 = ffn_bwd(1, dh4, dh4b, h3, ffn1)
    dsc = _mm_nt("od_out_bwd", dh3b, w_od_out, tm)
    big = {'od_w_out': _mm_tn("od_out_dw", sc.reshape(T, D), dh3b, tt)}
    dbg3, dcg3, dhx3, d_od_cw, d_od_cb = _od_act_bwd(p13, dsc.reshape(B, S, D), od_cw, od_cb)
    dp1 = [t.reshape(T, D) for t in (dbg3, dcg3, dhx3)]
    g_od_in = None
    for i, piece in enumerate(dp1):
        g_od_in = _mm_tn(f"od_in_dw{i}", piece, y1, tt, rows=3 * D, row_off=i * D, prev=g_od_in)
    big['od_w_in'] = g_od_in
    grads['od_conv_w'] = d_od_cw[None]
    grads['od_conv_b'] = d_od_cb
    dh2, dh2b, gmix1 = _mm_nt_normbwd("od_in_bwd", dp1, w_od_in, h2, mix_norm_g[1], dh3, tm)
    dh1, dh1b, gd0, gu0, gcw0, gcb0, gng0 = ffn_bwd(0, dh2, dh2b, h1, ffn0)
    big.update({'ffn_w_down0': gd0, 'ffn_w_down1': gd1, 'ffn_w_up0': gu0, 'ffn_w_up1': gu1})
    grads['ffn_conv_w'] = jnp.stack([gcw0, gcw1])
    grads['ffn_conv_b'] = jnp.stack([gcb0, gcb1])
    grads['ffn_norm_g'] = jnp.stack([gng0, gng1])
    grads['final_norm_g'] = d_final_g[0]

    dmix = _mm_nt("ev_out_bwd", dh1b, w_ev_out, tm)
    big['ev_w_out'] = _mm_tn("ev_out_dw", mixcat, dh1b, tt)
    duv, d_ws, d_bs, d_gv = _gmlp_bwd(p0, dmix, ws, bst, gv, SW)
    grads['gm_w_s'] = d_ws[None]
    grads['gm_b_s'] = d_bs[:, :, 0][None]
    grads['gm_v_g'] = d_gv
    dhr, dhi, du_skip, d_cbr, d_cbi, d_dsk, d_wglu, d_bglu = _s5_out_bwd(
        hr, hi, p03, dmix.reshape(B, S, D), cbr, cbi, dsk, w_glu, bglu, tm)
    grads['s5_c_re'] = jnp.swapaxes(_diag_blocks(d_cbr, G), 1, 2)[None]
    grads['s5_c_im'] = jnp.swapaxes(_diag_blocks(d_cbi, G), 1, 2)[None]
    grads['s5_d'] = d_dsk
    grads['s5_w_glu'] = d_wglu[None]
    grads['s5_b_glu'] = d_bglu
    gr, gi, dar, dai = _s5_scan("s5_rscan", dhr, dhi, ar_s, ai_s, True, hr, hi)
    dp03, d_bbd = _s5_in_bwd(gr, gi, p03, bbd, du_skip, duv.reshape(B, S, -1), tm)
    d_bre, d_bim, d_zr, d_zi = _s5_bbd_bwd(d_bbd, zr_row, zi_row, bre, bim)
    grads['s5_b_re'] = jnp.swapaxes(_diag_blocks(d_bre, G), 1, 2)[None]
    grads['s5_b_im'] = jnp.swapaxes(_diag_blocks(d_bim, G), 1, 2)[None]
    shp = (-1, G, SSM_STATE)
    d_lr, d_li, d_ldt = _s5_prep_bwd(lr, li, ldt, dar.reshape(shp), dai.reshape(shp), d_zr.reshape(shp),
                                     d_zi.reshape(shp))
    grads['s5_lam_re'] = d_lr[None]
    grads['s5_lam_im'] = d_li[None]
    grads['s5_log_dt'] = d_ldt.reshape(1, G)
    dp0 = dp03.reshape(T, PW)
    big['ev_w_in'] = _mm_tn("ev_in_dw", dp0, y0, tt)
    grad_x, _, gmix0 = _mm_nt_normbwd("ev_in_bwd", [dp0], w_ev_in, h0, mix_norm_g[0], dh1, tm)
    grads['mix_norm_g'] = jnp.concatenate([gmix0, gmix1], axis=0)

    small = [n for n in WEIGHTS if n not in BIG]
    segs = []
    for n in small:
        gfull = grads[n].astype(f32)
        if n in SHARD_AXIS:
            segs.append(_split_chips(gfull, SHARD_AXIS[n]))
        else:
            segs.append(jnp.broadcast_to(gfull.reshape(1, -1), (N_CHIPS, gfull.size)))
    unit = 2 * SUBLANES * D
    gsmall = _pad_rows(jnp.concatenate(segs, axis=1), unit).reshape(N_CHIPS, -1, D)
    names = list(big) + ["small"]
    parts = [big[n].reshape(N_CHIPS, -1, D) for n in big] + [gsmall]
    recv = _comm_pair_swap(parts)
    hsum = [_pair_sum(f"pair_sum_{n}", g, r, c_idx, f32 if n == "small" else bf16)
            for n, g, r in zip(names, parts, recv)]
    r3 = _comm_chip_exchange(hsum)
    mine = [_chip_sum(f"chip_sum_{n}", r, h, k_idx) for n, r, h in zip(names, r3, hsum)]
    theirs = _comm_pair_share(mine)
    halves = {n: (a, b) for n, a, b in zip(names, mine, theirs)}

    out_g, out_d, out_m, out_v = {}, {}, {}, {}

    def update(n, key, lead, transposed, prev=None):
        res = _adamw(f"adamw_{key}", *halves[key], c_idx, w[n], mom[n], var[n], lead, transposed, prev)
        out_g[n], out_d[n], out_m[n], out_v[n] = res
        return res

    update('ev_w_in', 'ev_w_in', 0, True)
    update('ev_w_out', 'ev_w_out', 0, False)
    update('od_w_in', 'od_w_in', 0, True)
    update('od_w_out', 'od_w_out', 0, False)
    update('ffn_w_up', 'ffn_w_up1', 1, True, prev=update('ffn_w_up', 'ffn_w_up0', 0, True))
    update('ffn_w_down', 'ffn_w_down1', 1, False, prev=update('ffn_w_down', 'ffn_w_down0', 0, False))

    def pack_local(d):
        flat = _pad_rows(jnp.concatenate([d[n].astype(f32).reshape(-1) for n in small]), unit)
        return flat.reshape(1, -1, D)

    res = _adamw("adamw_small", *halves['small'], c_idx, pack_local(w), pack_local(mom), pack_local(var), 0, False)
    for dst, p in zip((out_g, out_d, out_m, out_v), res):
        flat, off = p.reshape(-1), 0
        for n in small:
            dst[n] = flat[off:off + w[n].size].reshape(w[n].shape)
            off += w[n].size

    return (loss, grad_x.reshape(B, S, D), *[out_g[n] for n in WEIGHTS], *[out_d[n] for n in WEIGHTS],
            *[out_m[n] for n in WEIGHTS], *[out_v[n] for n in WEIGHTS])
```

```python
import functools
import math

import jax
import jax.numpy as jnp
from jax import lax
from jax.experimental import pallas as pl
from jax.experimental.pallas import tpu as pltpu

f32 = jnp.float32
bf16 = jnp.bfloat16
MESH = pl.DeviceIdType.MESH

SSM_GROUP = 16
SSM_STATE = 64
GMLP_HEAD = 128
CHUNK = 128
EPS = 1e-6
LAMBDA_RE_MAX = -1e-4
ADAM_LR, ADAM_B1, ADAM_B2, ADAM_EPS, ADAM_WD, ADAM_STEP = 0.001, 0.9, 0.999, 1e-08, 0.01, 10

LANES = 128
SUBLANES = 8
NSUB = 32
HALO = 16
VMEM_LIMIT = 56 * 1024 * 1024
N_CHIPS = 4

WEIGHTS = ['mix_norm_g', 'ffn_norm_g', 'final_norm_g', 'ev_w_in', 'ev_w_out', 's5_lam_re', 's5_lam_im', 's5_log_dt',
           's5_b_re', 's5_b_im', 's5_c_re', 's5_c_im', 's5_d', 's5_w_glu', 's5_b_glu', 'gm_w_s', 'gm_b_s', 'gm_v_g',
           'od_w_in', 'od_conv_w', 'od_conv_b', 'od_w_out', 'ffn_w_up', 'ffn_conv_w', 'ffn_conv_b', 'ffn_w_down']
SHARD_AXIS = {'ev_w_in': 2, 'ev_w_out': 1, 's5_w_glu': 1, 'od_w_in': 2, 'od_conv_w': 2, 'od_conv_b': 1, 'od_w_out': 1,
              'ffn_w_up': 2, 'ffn_conv_w': 2, 'ffn_w_down': 1}
GATHER_BF16 = ['ev_w_in', 'ev_w_out', 's5_w_glu', 'od_w_in', 'od_w_out', 'ffn_w_up', 'ffn_w_down']
GATHER_F32 = ['od_conv_w', 'od_conv_b', 'ffn_conv_w']

_GELU_K0 = math.sqrt(2.0 / math.pi)
_GELU_K1 = 0.044715
NT = (((1,), (1,)), ((), ()))
TN = (((0,), (0,)), ((), ()))


def _pick(n, cap):
    if n <= cap:
        return n
    best = None
    for d in range(LANES, cap + 1, LANES):
        if n % d == 0:
            best = d
    assert best is not None, (n, cap)
    return best


def _params(sem=None):
    return pltpu.CompilerParams(dimension_semantics=sem, vmem_limit_bytes=VMEM_LIMIT)


class _Phase:
    def __init__(self, ins, inplace, outs, sems, start, finish):
        self.ins, self.inplace, self.outs, self.sems = list(ins), list(inplace), list(outs), list(sems)
        self.start, self.finish = start, finish


def _call(body, name, grid, in_specs, out_specs, out_shape, args, scratch=(), sem=None, phase=None):
    if phase is None:
        return pl.pallas_call(body, name=name, grid=grid, in_specs=in_specs, out_specs=out_specs, out_shape=out_shape,
                              scratch_shapes=list(scratch), compiler_params=_params(sem))(*args)
    any_spec = pl.BlockSpec(memory_space=pl.ANY)
    n_in, n_out, n_scr = len(args), len(out_shape), len(scratch)
    p_in = phase.ins + phase.inplace
    ci, co = len(p_in), len(phase.outs)

    def wrapped(*refs):
        ins, cins = refs[:n_in], refs[n_in:n_in + len(phase.ins)]
        b = n_in + ci
        outs, couts = refs[b:b + n_out], refs[b + n_out:b + n_out + co]
        d = b + n_out + co
        scr, csem = refs[d:d + n_scr], refs[d + n_scr:]
        ids = [pl.program_id(i) for i in range(len(grid))]
        first = functools.reduce(jnp.logical_and, [i == 0 for i in ids])
        last = functools.reduce(jnp.logical_and, [i == g - 1 for i, g in zip(ids, grid)])

        @pl.when(first)
        def _():
            phase.start(cins, couts, csem)
        body(*ins, *outs, *scr)

        @pl.when(last)
        def _():
            phase.finish(cins, couts, csem)

    res = pl.pallas_call(
        wrapped, name=name, grid=grid, in_specs=list(in_specs) + [any_spec] * ci,
        out_specs=list(out_specs) + [any_spec] * co, out_shape=list(out_shape) + phase.outs,
        scratch_shapes=list(scratch) + phase.sems,
        input_output_aliases={n_in + len(phase.ins) + i: n_out + i for i in range(len(phase.inplace))},
        compiler_params=_params(tuple("arbitrary" for _ in grid)))(*args, *p_in)
    return res[:n_out], res[n_out:]


def _run_phase(name, phase):
    any_spec = pl.BlockSpec(memory_space=pl.ANY)
    ni, ci, co = len(phase.ins), len(phase.ins) + len(phase.inplace), len(phase.outs)

    def body(*refs):
        cins, couts, csem = refs[:ni], refs[ci:ci + co], refs[ci + co:]
        phase.start(cins, couts, csem)
        phase.finish(cins, couts, csem)

    return pl.pallas_call(
        body, name=name, in_specs=[any_spec] * ci, out_specs=[any_spec] * co, out_shape=phase.outs,
        scratch_shapes=phase.sems, input_output_aliases={ni + i: i for i in range(len(phase.inplace))})(
            *phase.ins, *phase.inplace)


def _gelu(x):
    return 0.5 * x * (1.0 + jnp.tanh(_GELU_K0 * (x + _GELU_K1 * x * x * x)))


def _gelu_grad(x):
    t = jnp.tanh(_GELU_K0 * (x + _GELU_K1 * x * x * x))
    return 0.5 * (1.0 + t) + 0.5 * x * (1.0 - t * t) * _GELU_K0 * (1.0 + 3.0 * _GELU_K1 * x * x)


def _rms_stats(x):
    r = lax.rsqrt(jnp.mean(x * x, axis=-1, keepdims=True) + EPS)
    return x * r, r


def _rms_bwd(dy, xh, r, g):
    dxh = dy * g
    dx = r * (dxh - xh * jnp.mean(dxh * xh, axis=-1, keepdims=True))
    return dx, jnp.sum(dy * xh, axis=0, keepdims=True)


def _dot(a, b):
    return jnp.dot(a, b, preferred_element_type=f32)


def _dg(a, b, dims):
    return lax.dot_general(a, b, dims, preferred_element_type=f32)


def _row_fold(z):
    return z.reshape(z.shape[0] // SUBLANES, SUBLANES, z.shape[1]).sum(axis=0)


def _norm_mm(name, h, g, w, tm, phase=None):
    T, D = h.shape
    N = w.shape[1]
    nc = _pick(N, 512)

    def body(h_ref, g_ref, w_ref, y_ref, o_ref):
        xh, _ = _rms_stats(h_ref[...])
        y = (xh * g_ref[...]).astype(bf16)
        y_ref[...] = y
        for j in range(N // nc):
            o_ref[:, j * nc:(j + 1) * nc] = _dot(y, w_ref[:, j * nc:(j + 1) * nc]).astype(bf16)

    return _call(
        body, name, (T // tm,),
        [pl.BlockSpec((tm, D), lambda i: (i, 0)), pl.BlockSpec((1, D), lambda i: (0, 0)),
         pl.BlockSpec((D, N), lambda i: (0, 0))],
        [pl.BlockSpec((tm, D), lambda i: (i, 0)), pl.BlockSpec((tm, N), lambda i: (i, 0))],
        [jax.ShapeDtypeStruct((T, D), bf16), jax.ShapeDtypeStruct((T, N), bf16)],
        [h, g.reshape(1, D), w], sem=("parallel",), phase=phase)


def _mm_resid(name, a, w, resid, tm):
    T, K = a.shape
    N = w.shape[1]

    def body(a_ref, w_ref, r_ref, o_ref):
        o_ref[...] = r_ref[...] + _dot(a_ref[...], w_ref[...])

    return pl.pallas_call(
        body, name=name, grid=(T // tm,),
        in_specs=[pl.BlockSpec((tm, K), lambda i: (i, 0)), pl.BlockSpec((K, N), lambda i: (0, 0)),
                  pl.BlockSpec((tm, N), lambda i: (i, 0))],
        out_specs=pl.BlockSpec((tm, N), lambda i: (i, 0)),
        out_shape=jax.ShapeDtypeStruct((T, N), f32),
        compiler_params=_params(("parallel",)))(a, w, resid)


def _mm_nt(name, dy, w, tm):
    T, N = dy.shape
    K = w.shape[0]
    kc = _pick(K, 512)

    def body(d_ref, w_ref, o_ref):
        d = d_ref[...].astype(bf16)
        for j in range(K // kc):
            o_ref[:, j * kc:(j + 1) * kc] = _dg(d, w_ref[j * kc:(j + 1) * kc, :], NT).astype(bf16)

    return pl.pallas_call(
        body, name=name, grid=(T // tm,),
        in_specs=[pl.BlockSpec((tm, N), lambda i: (i, 0)), pl.BlockSpec((K, N), lambda i: (0, 0))],
        out_specs=pl.BlockSpec((tm, K), lambda i: (i, 0)),
        out_shape=jax.ShapeDtypeStruct((T, K), bf16),
        compiler_params=_params(("parallel",)))(dy, w)


def _mm_nt_normbwd(name, dys, w, h, g, dh_in, tm):
    n = len(dys)
    T = dys[0].shape[0]
    D = w.shape[0]
    widths = [d.shape[1] for d in dys]
    offs = [sum(widths[:i]) for i in range(n)]

    def body(*refs):
        d_refs = refs[:n]
        w_ref, h_ref, g_ref, dh_ref, o_ref, ob_ref, dg_ref = refs[n:]
        dz = _dg(d_refs[0][...], w_ref[:, :widths[0]], NT)
        for i in range(1, n):
            dz += _dg(d_refs[i][...], w_ref[:, offs[i]:offs[i] + widths[i]], NT)
        xh, r = _rms_stats(h_ref[...])
        dx, dg = _rms_bwd(dz, xh, r, g_ref[...])
        out = dh_ref[...] + dx
        o_ref[...] = out
        ob_ref[...] = out.astype(bf16)

        @pl.when(pl.program_id(0) == 0)
        def _():
            dg_ref[...] = jnp.zeros_like(dg_ref)
        dg_ref[...] += dg

    row = lambda c: pl.BlockSpec((tm, c), lambda i: (i, 0))
    return pl.pallas_call(
        body, name=name, grid=(T // tm,),
        in_specs=[row(c) for c in widths] + [pl.BlockSpec((D, sum(widths)), lambda i: (0, 0)), row(D),
                                             pl.BlockSpec((1, D), lambda i: (0, 0)), row(D)],
        out_specs=[row(D), row(D), pl.BlockSpec((1, D), lambda i: (0, 0))],
        out_shape=[jax.ShapeDtypeStruct((T, D), f32), jax.ShapeDtypeStruct((T, D), bf16),
                   jax.ShapeDtypeStruct((1, D), f32)],
        compiler_params=_params(("arbitrary",)))(*dys, w, h, g.reshape(1, D), dh_in)


def _mm_tn(name, a, b, tt, rows=None, row_off=0, prev=None):
    T, K = a.shape
    N = b.shape[1]
    rows = K if rows is None else rows
    tk = _pick(K, 1408)
    tn = _pick(N, 1024)
    assert row_off % tk == 0
    kb = row_off // tk

    def body(a_ref, b_ref, *rest):
        o_ref = rest[-1]

        @pl.when(pl.program_id(2) == 0)
        def _():
            o_ref[...] = jnp.zeros_like(o_ref)
        o_ref[...] += _dg(a_ref[...], b_ref[...], TN)

    in_specs = [pl.BlockSpec((tt, tk), lambda k, n, t: (t, k)), pl.BlockSpec((tt, tn), lambda k, n, t: (t, n))]
    args, aliases = [a, b], {}
    if prev is not None:
        in_specs.append(ANY)
        args.append(prev)
        aliases = {2: 0}
    return pl.pallas_call(
        body, name=name, grid=(K // tk, N // tn, T // tt), in_specs=in_specs,
        out_specs=pl.BlockSpec((tk, tn), lambda k, n, t: (k + kb, n)),
        out_shape=jax.ShapeDtypeStruct((rows, N), f32), input_output_aliases=aliases,
        compiler_params=_params(("parallel", "parallel", "arbitrary")))(*args)


def _final_loss(h, g, tgt, tm):
    T, D = h.shape

    def body(h_ref, g_ref, t_ref, dh_ref, dhb_ref, loss_ref, dg_ref):
        xh, r = _rms_stats(h_ref[...])
        gg = g_ref[...]
        diff = xh * gg - t_ref[...]
        dy = diff * (1.0 / D)
        dx, dg = _rms_bwd(dy, xh, r, gg)
        dh_ref[...] = dx
        dhb_ref[...] = dx.astype(bf16)

        @pl.when(pl.program_id(0) == 0)
        def _():
            dg_ref[...] = jnp.zeros_like(dg_ref)
            loss_ref[...] = jnp.zeros_like(loss_ref)
        dg_ref[...] += dg
        loss_ref[...] += (0.5 / D) * jnp.sum(jnp.sum(diff * diff, axis=1, keepdims=True), axis=0, keepdims=True)

    return pl.pallas_call(
        body, name="final_loss", grid=(T // tm,),
        in_specs=[pl.BlockSpec((tm, D), lambda i: (i, 0)), pl.BlockSpec((1, D), lambda i: (0, 0)),
                  pl.BlockSpec((tm, D), lambda i: (i, 0))],
        out_specs=[pl.BlockSpec((tm, D), lambda i: (i, 0)), pl.BlockSpec((tm, D), lambda i: (i, 0)),
                   pl.BlockSpec((1, 1), lambda i: (0, 0)), pl.BlockSpec((1, D), lambda i: (0, 0))],
        out_shape=[jax.ShapeDtypeStruct((T, D), f32), jax.ShapeDtypeStruct((T, D), bf16),
                   jax.ShapeDtypeStruct((1, 1), f32), jax.ShapeDtypeStruct((1, D), f32)],
        compiler_params=_params(("arbitrary",)))(h, g.reshape(1, D), tgt)


def _taps(load, r0, R):
    main = load(r0, R)
    hs = pl.multiple_of(jnp.maximum(r0 - HALO, 0), HALO)
    halo = load(hs, HALO) * (r0 > 0).astype(f32)
    ext = jnp.concatenate([halo, main], axis=0)
    xm1 = pltpu.roll(ext, 1, 0)[HALO:]
    xm2 = pltpu.roll(ext, 2, 0)[HALO:]
    return xm2, xm1, main


def _conv(w, b, taps):
    return b + w[0:1] * taps[0] + w[1:2] * taps[1] + w[2:3] * taps[2]


def _ref_load(ref):
    return lambda s, n: ref[pl.ds(s, n), :].astype(f32)


def _ffn_act(name, up3, cw, cb, phase=None):
    B, S, F2 = up3.shape
    F = F2 // 2
    cwid = _pick(F, 256)
    nF = F // cwid
    R = min(256, S)

    def body(g_ref, v_ref, wg_ref, wv_ref, bg_ref, bv_ref, o_ref):
        wg, wv, bg, bv = wg_ref[...], wv_ref[...], bg_ref[...], bv_ref[...]

        def chunk(r, c):
            r0 = pl.multiple_of(r * R, R)
            cg = _conv(wg, bg, _taps(_ref_load(g_ref), r0, R))
            cv = _conv(wv, bv, _taps(_ref_load(v_ref), r0, R))
            o_ref[pl.ds(r0, R), :] = (cg * jax.nn.sigmoid(cg) * cv).astype(bf16)
            return c
        lax.fori_loop(0, S // R, chunk, 0)

    blk = lambda off: pl.BlockSpec((None, S, cwid), lambda b, j: (b, 0, off + j))
    wblk = lambda off: pl.BlockSpec((3, cwid), lambda b, j: (0, off + j))
    bblk = lambda off: pl.BlockSpec((1, cwid), lambda b, j: (0, off + j))
    return _call(
        body, name, (B, nF), [blk(0), blk(nF), wblk(0), wblk(nF), bblk(0), bblk(nF)],
        [pl.BlockSpec((None, S, cwid), lambda b, j: (b, 0, j))], [jax.ShapeDtypeStruct((B, S, F), bf16)],
        [up3, up3, cw, cw, cb, cb], sem=("parallel", "parallel"), phase=phase)


def _rev_conv_rows(d, nxt, w):
    R = d.shape[0]
    ext = jnp.concatenate([d, nxt], axis=0)
    n = R + HALO
    xp1 = pltpu.roll(ext, n - 1, 0)[:R]
    xp2 = pltpu.roll(ext, n - 2, 0)[:R]
    return w[2:3] * d + w[1:2] * xp1 + w[0:1] * xp2


def _conv_grad_acc(acc, dc, taps):
    return (acc[0] + _row_fold(dc * taps[0]), acc[1] + _row_fold(dc * taps[1]), acc[2] + _row_fold(dc * taps[2]),
            acc[3] + _row_fold(dc))


def _conv_grad_out(dcw_ref, dcb_ref, acc):
    @pl.when(pl.program_id(1) == 0)
    def _():
        dcw_ref[...] = jnp.zeros_like(dcw_ref)
        dcb_ref[...] = jnp.zeros_like(dcb_ref)
    for k in range(3):
        dcw_ref[k:k + 1, :] += jnp.sum(acc[k], axis=0, keepdims=True)
    dcb_ref[...] += jnp.sum(acc[3], axis=0, keepdims=True)


def _ffn_act_bwd(name, up3, da3, cw, cb, phase=None):
    B, S, F2 = up3.shape
    F = F2 // 2
    cwid = _pick(F, 256)
    nF = F // cwid
    R = min(256, S)
    nR = S // R

    def body(g_ref, v_ref, da_ref, wg_ref, wv_ref, bg_ref, bv_ref,
             dg_ref, dv_ref, dcwg_ref, dcwv_ref, dcbg_ref, dcbv_ref):
        wg, wv, bg, bv = wg_ref[...], wv_ref[...], bg_ref[...], bv_ref[...]

        def step(i, carry):
            ng, nv, accg, accv = carry
            r0 = pl.multiple_of((nR - 1 - i) * R, R)
            tg = _taps(_ref_load(g_ref), r0, R)
            tv = _taps(_ref_load(v_ref), r0, R)
            cg = _conv(wg, bg, tg)
            cv = _conv(wv, bv, tv)
            da = da_ref[pl.ds(r0, R), :].astype(f32)
            sg = jax.nn.sigmoid(cg)
            dgate = da * cv * (sg * (1.0 + cg * (1.0 - sg)))
            dval = da * (cg * sg)
            dg_ref[pl.ds(r0, R), :] = _rev_conv_rows(dgate, ng, wg).astype(bf16)
            dv_ref[pl.ds(r0, R), :] = _rev_conv_rows(dval, nv, wv).astype(bf16)
            return dgate[:HALO], dval[:HALO], _conv_grad_acc(accg, dgate, tg), _conv_grad_acc(accv, dval, tv)
        z = jnp.zeros((SUBLANES, cwid), f32)
        zh = jnp.zeros((HALO, cwid), f32)
        _, _, accg, accv = lax.fori_loop(0, nR, step, (zh, zh, (z, z, z, z), (z, z, z, z)))
        _conv_grad_out(dcwg_ref, dcbg_ref, accg)
        _conv_grad_out(dcwv_ref, dcbv_ref, accv)

    blk = lambda off: pl.BlockSpec((None, S, cwid), lambda j, b: (b, 0, off + j))
    wblk = lambda off: pl.BlockSpec((3, cwid), lambda j, b: (0, off + j))
    bblk = lambda off: pl.BlockSpec((1, cwid), lambda j, b: (0, off + j))
    half = jax.ShapeDtypeStruct((B, S, F), bf16)
    return _call(
        body, name, (nF, B), [blk(0), blk(nF), blk(0), wblk(0), wblk(nF), bblk(0), bblk(nF)],
        [blk(0), blk(0), wblk(0), wblk(0), bblk(0), bblk(0)],
        [half, half, jax.ShapeDtypeStruct((3, F), f32), jax.ShapeDtypeStruct((3, F), f32),
         jax.ShapeDtypeStruct((1, F), f32), jax.ShapeDtypeStruct((1, F), f32)],
        [up3, up3, da3, cw, cw, cb, cb], sem=("parallel", "arbitrary"), phase=phase)


def _od_act(p3, cw, cb):
    B, S, D3 = p3.shape
    D = D3 // 3
    cwid = _pick(D, 256)
    nD = D // cwid
    R = min(256, S)

    def body(bg_ref, cg_ref, hx_ref, w_ref, b_ref, o_ref):
        w, b = w_ref[...], b_ref[...]
        q = lambda s, n: cg_ref[pl.ds(s, n), :].astype(f32) * hx_ref[pl.ds(s, n), :].astype(f32)

        def chunk(r, c):
            r0 = pl.multiple_of(r * R, R)
            cq = _conv(w, b, _taps(q, r0, R))
            o_ref[pl.ds(r0, R), :] = (bg_ref[pl.ds(r0, R), :].astype(f32) * cq).astype(bf16)
            return c
        lax.fori_loop(0, S // R, chunk, 0)

    blk = lambda off: pl.BlockSpec((None, S, cwid), lambda b, j: (b, 0, off + j))
    return pl.pallas_call(
        body, name="od_act", grid=(B, nD),
        in_specs=[blk(0), blk(nD), blk(2 * nD), pl.BlockSpec((3, cwid), lambda b, j: (0, j)),
                  pl.BlockSpec((1, cwid), lambda b, j: (0, j))],
        out_specs=pl.BlockSpec((None, S, cwid), lambda b, j: (b, 0, j)),
        out_shape=jax.ShapeDtypeStruct((B, S, D), bf16),
        compiler_params=_params(("parallel", "parallel")))(p3, p3, p3, cw, cb)


def _od_act_bwd(p3, dsc3, cw, cb):
    B, S, D3 = p3.shape
    D = D3 // 3
    cwid = _pick(D, 256)
    nD = D // cwid
    R = min(256, S)
    nR = S // R

    def body(bg_ref, cg_ref, hx_ref, d_ref, w_ref, b_ref, dbg_ref, dcg_ref, dhx_ref, dcw_ref, dcb_ref):
        w, b = w_ref[...], b_ref[...]
        q = lambda s, n: cg_ref[pl.ds(s, n), :].astype(f32) * hx_ref[pl.ds(s, n), :].astype(f32)

        def step(i, carry):
            nxt, acc = carry
            r0 = pl.multiple_of((nR - 1 - i) * R, R)
            rows = pl.ds(r0, R)
            tq = _taps(q, r0, R)
            cq = _conv(w, b, tq)
            d = d_ref[rows, :].astype(f32)
            dbg_ref[rows, :] = (d * cq).astype(bf16)
            dcq = d * bg_ref[rows, :].astype(f32)
            dq = _rev_conv_rows(dcq, nxt, w)
            dcg_ref[rows, :] = (dq * hx_ref[rows, :].astype(f32)).astype(bf16)
            dhx_ref[rows, :] = (dq * cg_ref[rows, :].astype(f32)).astype(bf16)
            return dcq[:HALO], _conv_grad_acc(acc, dcq, tq)
        z = jnp.zeros((SUBLANES, cwid), f32)
        _, acc = lax.fori_loop(0, nR, step, (jnp.zeros((HALO, cwid), f32), (z, z, z, z)))
        _conv_grad_out(dcw_ref, dcb_ref, acc)

    blk = lambda off: pl.BlockSpec((None, S, cwid), lambda j, b: (b, 0, off + j))
    part = jax.ShapeDtypeStruct((B, S, D), bf16)
    return pl.pallas_call(
        body, name="od_act_bwd", grid=(nD, B),
        in_specs=[blk(0), blk(nD), blk(2 * nD), blk(0), pl.BlockSpec((3, cwid), lambda j, b: (0, j)),
                  pl.BlockSpec((1, cwid), lambda j, b: (0, j))],
        out_specs=[blk(0), blk(0), blk(0), pl.BlockSpec((3, cwid), lambda j, b: (0, j)),
                   pl.BlockSpec((1, cwid), lambda j, b: (0, j))],
        out_shape=[part, part, part, jax.ShapeDtypeStruct((3, D), f32), jax.ShapeDtypeStruct((1, D), f32)],
        compiler_params=_params(("parallel", "arbitrary")))(p3, p3, p3, dsc3, cw, cb)


def _gmlp_parts(p, gv, SW, GW):
    uv = p[:, SW:].astype(f32)
    ge = _gelu(uv)
    u, v = ge[:, :GW], ge[:, GW:]
    vh, r = _rms_stats(v)
    return uv, u, vh, r, vh * gv


def _tril():
    rows = lax.broadcasted_iota(jnp.int32, (CHUNK, CHUNK), 0)
    cols = lax.broadcasted_iota(jnp.int32, (CHUNK, CHUNK), 1)
    return rows >= cols


def _gmlp(p0, a_out, ws, bst, gv, SW):
    T, PW = p0.shape
    GW = (PW - SW) // 2
    H = GW // GMLP_HEAD
    D = SW + GW

    def body(p_ref, a_ref, ws_ref, b_ref, gv_ref, o_ref):
        _, u, _, _, vn = _gmlp_parts(p_ref[...], gv_ref[...], SW, GW)
        tri = _tril()
        o_ref[:, :SW] = a_ref[...]
        for hh in range(H):
            sl = slice(hh * GMLP_HEAD, (hh + 1) * GMLP_HEAD)
            wm = jnp.where(tri, ws_ref[hh], 0.0).astype(bf16)
            gate = _dot(wm, vn[:, sl].astype(bf16)) + b_ref[:, hh:hh + 1]
            o_ref[:, SW + hh * GMLP_HEAD:SW + (hh + 1) * GMLP_HEAD] = (u[:, sl] * gate).astype(bf16)

    return pl.pallas_call(
        body, name="gmlp", grid=(T // CHUNK,),
        in_specs=[pl.BlockSpec((CHUNK, PW), lambda i: (i, 0)), pl.BlockSpec((CHUNK, SW), lambda i: (i, 0)),
                  pl.BlockSpec((H, CHUNK, CHUNK), lambda i: (0, 0, 0)), pl.BlockSpec((CHUNK, H), lambda i: (0, 0)),
                  pl.BlockSpec((1, GW), lambda i: (0, 0))],
        out_specs=pl.BlockSpec((CHUNK, D), lambda i: (i, 0)),
        out_shape=jax.ShapeDtypeStruct((T, D), bf16),
        compiler_params=_params(("parallel",)))(p0, a_out, ws, bst, gv)


def _gmlp_bwd(p0, dmix, ws, bst, gv, SW):
    T, PW = p0.shape
    GW = (PW - SW) // 2
    H = GW // GMLP_HEAD
    D = SW + GW

    def body(p_ref, d_ref, ws_ref, b_ref, gv_ref, duv_ref, dws_ref, dbs_ref, dgv_ref):
        gv_ = gv_ref[...]
        uv, u, vh, r, vn = _gmlp_parts(p_ref[...], gv_, SW, GW)
        dout = d_ref[...][:, SW:].astype(f32)
        tri = _tril()

        @pl.when(pl.program_id(0) == 0)
        def _():
            dws_ref[...] = jnp.zeros_like(dws_ref)
            dbs_ref[...] = jnp.zeros_like(dbs_ref)
            dgv_ref[...] = jnp.zeros_like(dgv_ref)
        du, dvn = [], []
        for hh in range(H):
            sl = slice(hh * GMLP_HEAD, (hh + 1) * GMLP_HEAD)
            wm = jnp.where(tri, ws_ref[hh], 0.0).astype(bf16)
            vnh = vn[:, sl].astype(bf16)
            gate = _dot(wm, vnh) + b_ref[:, hh:hh + 1]
            dgate = dout[:, sl] * u[:, sl]
            du.append(dout[:, sl] * gate)
            dgb = dgate.astype(bf16)
            dws_ref[hh] += jnp.where(tri, _dg(dgb, vnh, NT), 0.0)
            dbs_ref[hh] += jnp.broadcast_to(jnp.sum(dgate, axis=1, keepdims=True), (CHUNK, CHUNK))
            dvn.append(_dg(wm, dgb, TN))
        dvn = jnp.concatenate(dvn, axis=1)
        dv, dgv = _rms_bwd(dvn, vh, r, gv_)
        dgv_ref[...] += dgv
        dge = jnp.concatenate(du + [dv], axis=1)
        duv_ref[...] = (dge * _gelu_grad(uv)).astype(bf16)

    return pl.pallas_call(
        body, name="gmlp_bwd", grid=(T // CHUNK,),
        in_specs=[pl.BlockSpec((CHUNK, PW), lambda i: (i, 0)), pl.BlockSpec((CHUNK, D), lambda i: (i, 0)),
                  pl.BlockSpec((H, CHUNK, CHUNK), lambda i: (0, 0, 0)), pl.BlockSpec((CHUNK, H), lambda i: (0, 0)),
                  pl.BlockSpec((1, GW), lambda i: (0, 0))],
        out_specs=[pl.BlockSpec((CHUNK, 2 * GW), lambda i: (i, 0)), pl.BlockSpec((H, CHUNK, CHUNK), lambda i: (0, 0, 0)),
                   pl.BlockSpec((H, CHUNK, CHUNK), lambda i: (0, 0, 0)), pl.BlockSpec((1, GW), lambda i: (0, 0))],
        out_shape=[jax.ShapeDtypeStruct((T, 2 * GW), bf16), jax.ShapeDtypeStruct((H, CHUNK, CHUNK), f32),
                   jax.ShapeDtypeStruct((H, CHUNK, CHUNK), f32), jax.ShapeDtypeStruct((1, GW), f32)],
        compiler_params=_params(("arbitrary",)))(p0, dmix, ws, bst, gv)


def _s5_disc(lr, li, ldt):
    lr = jnp.minimum(lr, LAMBDA_RE_MAX)
    dt = jnp.exp(ldt)
    mag = jnp.exp(lr * dt)
    ar = mag * jnp.cos(li * dt)
    ai = mag * jnp.sin(li * dt)
    den = lr * lr + li * li
    nr = ar - 1.0
    zr = (nr * lr + ai * li) / den
    zi = (ai * lr - nr * li) / den
    return ar, ai, zr, zi


def _s5_prep(lr, li, ldt):
    G, P = lr.shape

    def body(lr_ref, li_ref, ldt_ref, ar_ref, ai_ref, zr_ref, zi_ref):
        ar, ai, zr, zi = _s5_disc(lr_ref[...], li_ref[...], ldt_ref[...])
        ar_ref[...] = ar
        ai_ref[...] = ai
        zr_ref[...] = zr
        zi_ref[...] = zi

    s = jax.ShapeDtypeStruct((G, P), f32)
    return pl.pallas_call(body, name="s5_prep", out_shape=[s, s, s, s])(lr, li, ldt)


def _s5_prep_bwd(lr, li, ldt, dar, dai, dzr, dzi):
    G, P = lr.shape

    def body(lr_ref, li_ref, ldt_ref, dar_ref, dai_ref, dzr_ref, dzi_ref, o1, o2, o3):
        _, vjp = jax.vjp(_s5_disc, lr_ref[...], li_ref[...], ldt_ref[...])
        cts = tuple(jnp.sum(r[...], axis=0) for r in (dar_ref, dai_ref, dzr_ref, dzi_ref))
        a, b, c = vjp(cts)
        o1[...] = a
        o2[...] = b
        o3[...] = c

    s = jax.ShapeDtypeStruct((G, P), f32)
    return pl.pallas_call(body, name="s5_prep_bwd", out_shape=[s, s, jax.ShapeDtypeStruct((G, 1), f32)])(
        lr, li, ldt, dar, dai, dzr, dzi)


def _s5_bbd(zr, zi, bre, bim):
    SW, NS = bre.shape

    def body(zr_ref, zi_ref, br_ref, bi_ref, o_ref):
        zr_, zi_, br, bi = zr_ref[...], zi_ref[...], br_ref[...], bi_ref[...]
        o_ref[:, :NS] = (zr_ * br - zi_ * bi).astype(bf16)
        o_ref[:, NS:] = (zr_ * bi + zi_ * br).astype(bf16)

    return pl.pallas_call(body, name="s5_bbd", out_shape=jax.ShapeDtypeStruct((SW, 2 * NS), bf16))(zr, zi, bre, bim)


def _s5_bbd_bwd(dbbd, zr, zi, bre, bim):
    SW, NS = bre.shape

    def body(d_ref, zr_ref, zi_ref, br_ref, bi_ref, dbr_ref, dbi_ref, dzr_ref, dzi_ref):
        zr_, zi_, br, bi = zr_ref[...], zi_ref[...], br_ref[...], bi_ref[...]
        dr, di = d_ref[:, :NS], d_ref[:, NS:]
        dbr_ref[...] = zr_ * dr + zi_ * di
        dbi_ref[...] = zr_ * di - zi_ * dr
        dzr_ref[...] = jnp.sum(dr * br + di * bi, axis=0, keepdims=True)
        dzi_ref[...] = jnp.sum(di * br - dr * bi, axis=0, keepdims=True)

    m = jax.ShapeDtypeStruct((SW, NS), f32)
    v = jax.ShapeDtypeStruct((1, NS), f32)
    return pl.pallas_call(body, name="s5_bbd_bwd", out_shape=[m, m, v, v])(dbbd, zr, zi, bre, bim)


def _slab_cat(ref, NB):
    return jnp.concatenate([ref[j] for j in range(NB)], axis=1)


def _s5_in(p3, bbd, SW, tm):
    B, S, PW = p3.shape
    NS = bbd.shape[1] // 2
    NB = NS // LANES

    def body(u_ref, b_ref, xr_ref, xi_ref):
        x = _dot(u_ref[...], b_ref[...])
        for j in range(NB):
            xr_ref[j] = x[:, j * LANES:(j + 1) * LANES]
            xi_ref[j] = x[:, NS + j * LANES:NS + (j + 1) * LANES]

    slab = jax.ShapeDtypeStruct((B, NB, S, LANES), f32)
    sspec = pl.BlockSpec((None, NB, tm, LANES), lambda b, i: (b, 0, i, 0))
    return pl.pallas_call(
        body, name="s5_in", grid=(B, S // tm),
        in_specs=[pl.BlockSpec((None, tm, SW), lambda b, i: (b, i, 0)), pl.BlockSpec((SW, 2 * NS), lambda b, i: (0, 0))],
        out_specs=[sspec, sspec], out_shape=[slab, slab],
        compiler_params=_params(("parallel", "parallel")))(p3, bbd)


def _s5_scan(name, xr, xi, ar, ai, reverse, hr=None, hi=None, phase=None):
    B, NB, S, _ = xr.shape
    L = S // NSUB
    nb = 2 if (hr is None and NB % 2 == 0) else 1
    with_da = hr is not None

    def body(*refs):
        if with_da:
            xr_ref, xi_ref, ar_ref, ai_ref, hr_ref, hi_ref, or_ref, oi_ref, dar_ref, dai_ref, pr_scr, pi_scr = refs
        else:
            xr_ref, xi_ref, ar_ref, ai_ref, or_ref, oi_ref, pr_scr, pi_scr = refs
        sign = -1.0 if reverse else 1.0
        a_r = [jnp.broadcast_to(ar_ref[j], (NSUB, LANES)) for j in range(nb)]
        a_i = [jnp.broadcast_to(ai_ref[j], (NSUB, LANES)) * sign for j in range(nb)]

        def step(t, carry):
            row = (L - 1 - t) if reverse else t
            rows = pl.ds(row, NSUB, stride=L)
            out = []
            for j in range(nb):
                sr, si, pr, pi = carry[j]
                nr = a_r[j] * sr - a_i[j] * si + xr_ref.at[j][rows, :]
                ni = a_r[j] * si + a_i[j] * sr + xi_ref.at[j][rows, :]
                or_ref.at[j][rows, :] = nr
                oi_ref.at[j][rows, :] = ni
                npr = a_r[j] * pr - a_i[j] * pi
                npi = a_r[j] * pi + a_i[j] * pr
                pr_scr[j, pl.ds(row, 1), :] = npr[0:1]
                pi_scr[j, pl.ds(row, 1), :] = npi[0:1]
                out.append((nr, ni, npr, npi))
            return tuple(out)
        z = jnp.zeros((NSUB, LANES), f32)
        one = jnp.ones((NSUB, LANES), f32)
        fin = lax.fori_loop(0, L, step, tuple((z, z, one, z) for _ in range(nb)))

        for j in range(nb):
            sr, si, plr, pli = fin[j]
            plr, pli = plr[0:1], pli[0:1]
            cr = jnp.zeros((1, LANES), f32)
            ci = jnp.zeros((1, LANES), f32)
            order = range(NSUB - 2, -1, -1) if reverse else range(1, NSUB)
            for c in order:
                src = c + 1 if reverse else c - 1
                cr, ci = (sr[src:src + 1] + plr * cr - pli * ci, si[src:src + 1] + plr * ci + pli * cr)
                rows = slice(c * L, (c + 1) * L)
                tr, ti = pr_scr[j], pi_scr[j]
                or_ref[j, rows, :] += tr * cr - ti * ci
                oi_ref[j, rows, :] += tr * ci + ti * cr
            if with_da:
                first = lax.broadcasted_iota(jnp.int32, (L, LANES), 0) == 0
                dar = jnp.zeros((1, LANES), f32)
                dai = jnp.zeros((1, LANES), f32)
                for c in range(NSUB):
                    rows = slice(c * L, (c + 1) * L)
                    if c == 0:
                        lr_, li_ = jnp.zeros((1, LANES), f32), jnp.zeros((1, LANES), f32)
                    else:
                        lr_, li_ = hr_ref[j, c * L - 1:c * L, :], hi_ref[j, c * L - 1:c * L, :]
                    hpr = jnp.where(first, lr_, pltpu.roll(hr_ref[j, rows, :], 1, 0))
                    hpi = jnp.where(first, li_, pltpu.roll(hi_ref[j, rows, :], 1, 0))
                    gr, gi = or_ref[j, rows, :], oi_ref[j, rows, :]
                    dar += jnp.sum(hpr * gr + hpi * gi, axis=0, keepdims=True)
                    dai += jnp.sum(hpr * gi - hpi * gr, axis=0, keepdims=True)
                dar_ref[j] = dar
                dai_ref[j] = dai

    slab = jax.ShapeDtypeStruct((B, NB, S, LANES), f32)
    sspec = pl.BlockSpec((None, nb, S, LANES), lambda b, j: (b, j, 0, 0))
    aspec = pl.BlockSpec((nb, 1, LANES), lambda b, j: (j, 0, 0))
    in_specs = [sspec, sspec, aspec, aspec]
    out_specs = [sspec, sspec]
    out_shape = [slab, slab]
    args = [xr, xi, ar, ai]
    if with_da:
        in_specs += [sspec, sspec]
        args += [hr, hi]
        dspec = pl.BlockSpec((None, nb, 1, LANES), lambda b, j: (b, j, 0, 0))
        out_specs += [dspec, dspec]
        out_shape += [jax.ShapeDtypeStruct((B, NB, 1, LANES), f32)] * 2
    return _call(body, name, (B, NB // nb), in_specs, out_specs, out_shape, args,
                 scratch=[pltpu.VMEM((nb, L, LANES), f32), pltpu.VMEM((nb, L, LANES), f32)],
                 sem=("parallel", "parallel"), phase=phase)


def _s5_out_parts(hr_ref, hi_ref, u_ref, cr_ref, ci_ref, d_ref, wg_ref, bg_ref, NB):
    hcr = _slab_cat(hr_ref, NB).astype(bf16)
    hci = _slab_cat(hi_ref, NB).astype(bf16)
    u = u_ref[...].astype(f32)
    y2 = _dot(hcr, cr_ref[...]) - _dot(hci, ci_ref[...]) + d_ref[...] * u
    yg = _gelu(y2)
    s = jax.nn.sigmoid(_dot(yg.astype(bf16), wg_ref[...]) + bg_ref[...])
    return hcr, hci, u, y2, yg, s


def _s5_out_specs(B, S, NB, NS, SW, tm):
    sspec = pl.BlockSpec((None, NB, tm, LANES), lambda b, i: (b, 0, i, 0))
    full = lambda r, c: pl.BlockSpec((r, c), lambda b, i: (0, 0))
    return sspec, [sspec, sspec, pl.BlockSpec((None, tm, SW), lambda b, i: (b, i, 0)), full(NS, SW), full(NS, SW),
                   full(1, SW), full(SW, SW), full(1, SW)]


def _s5_out(hr, hi, p3, cbr, cbi, dsk, wglu, bglu, tm):
    B, NB, S, _ = hr.shape
    NS, SW = cbr.shape

    def body(hr_ref, hi_ref, u_ref, cr_ref, ci_ref, d_ref, wg_ref, bg_ref, o_ref):
        _, _, _, _, yg, s = _s5_out_parts(hr_ref, hi_ref, u_ref, cr_ref, ci_ref, d_ref, wg_ref, bg_ref, NB)
        o_ref[...] = (yg * s).astype(bf16)

    _, in_specs = _s5_out_specs(B, S, NB, NS, SW, tm)
    return pl.pallas_call(
        body, name="s5_out", grid=(B, S // tm), in_specs=in_specs,
        out_specs=pl.BlockSpec((None, tm, SW), lambda b, i: (b, i, 0)),
        out_shape=jax.ShapeDtypeStruct((B, S, SW), bf16),
        compiler_params=_params(("parallel", "parallel")))(hr, hi, p3, cbr, cbi, dsk, wglu, bglu)


def _s5_out_bwd(hr, hi, p3, dmix3, cbr, cbi, dsk, wglu, bglu, tm):
    B, NB, S, _ = hr.shape
    NS, SW = cbr.shape

    def body(hr_ref, hi_ref, u_ref, cr_ref, ci_ref, d_ref, wg_ref, bg_ref, da_ref,
             dhr_ref, dhi_ref, du_ref, dcr_ref, dci_ref, dd_ref, dwg_ref, dbg_ref):
        hcr, hci, u, y2, yg, s = _s5_out_parts(hr_ref, hi_ref, u_ref, cr_ref, ci_ref, d_ref, wg_ref, bg_ref, NB)
        da = da_ref[...].astype(f32)
        dz = da * yg * s * (1.0 - s)
        dzb = dz.astype(bf16)
        dyg = da * s + _dg(dzb, wg_ref[...], NT)
        dy2 = dyg * _gelu_grad(y2)
        dyb = dy2.astype(bf16)

        @pl.when((pl.program_id(0) == 0) & (pl.program_id(1) == 0))
        def _():
            for r in (dcr_ref, dci_ref, dd_ref, dwg_ref, dbg_ref):
                r[...] = jnp.zeros_like(r)
        dwg_ref[...] += _dg(yg.astype(bf16), dzb, TN)
        dbg_ref[...] += jnp.sum(dz, axis=0, keepdims=True)
        dd_ref[...] += jnp.sum(dy2 * u, axis=0, keepdims=True)
        dcr_ref[...] += _dg(hcr, dyb, TN)
        dci_ref[...] -= _dg(hci, dyb, TN)
        du_ref[...] = dy2 * d_ref[...]
        dhr = _dg(dyb, cr_ref[...], NT)
        dhi = _dg(dyb, ci_ref[...], NT)
        for j in range(NB):
            dhr_ref[j] = dhr[:, j * LANES:(j + 1) * LANES]
            dhi_ref[j] = -dhi[:, j * LANES:(j + 1) * LANES]

    sspec, in_specs = _s5_out_specs(B, S, NB, NS, SW, tm)
    in_specs = in_specs + [pl.BlockSpec((None, tm, SW), lambda b, i: (b, i, 0))]
    full = lambda r, c: pl.BlockSpec((r, c), lambda b, i: (0, 0))
    slab = jax.ShapeDtypeStruct((B, NB, S, LANES), f32)
    mat = lambda r, c: jax.ShapeDtypeStruct((r, c), f32)
    return pl.pallas_call(
        body, name="s5_out_bwd", grid=(B, S // tm), in_specs=in_specs,
        out_specs=[sspec, sspec, pl.BlockSpec((None, tm, SW), lambda b, i: (b, i, 0)), full(NS, SW), full(NS, SW),
                   full(1, SW), full(SW, SW), full(1, SW)],
        out_shape=[slab, slab, jax.ShapeDtypeStruct((B, S, SW), f32), mat(NS, SW), mat(NS, SW), mat(1, SW),
                   mat(SW, SW), mat(1, SW)],
        compiler_params=_params(("arbitrary", "arbitrary")))(hr, hi, p3, cbr, cbi, dsk, wglu, bglu, dmix3)


def _s5_in_bwd(gr, gi, p3, bbd, du_skip, duv3, tm):
    B, NB, S, _ = gr.shape
    SW, NS2 = bbd.shape
    PW = SW + duv3.shape[2]

    def body(gr_ref, gi_ref, u_ref, b_ref, ds_ref, duv_ref, dp_ref, db_ref):
        g = jnp.concatenate([_slab_cat(gr_ref, NB), _slab_cat(gi_ref, NB)], axis=1).astype(bf16)
        du = _dg(g, b_ref[...], NT) + ds_ref[...]
        dp_ref[:, :SW] = du.astype(bf16)
        dp_ref[:, SW:] = duv_ref[...]

        @pl.when((pl.program_id(0) == 0) & (pl.program_id(1) == 0))
        def _():
            db_ref[...] = jnp.zeros_like(db_ref)
        db_ref[...] += _dg(u_ref[...], g, TN)

    sspec = pl.BlockSpec((None, NB, tm, LANES), lambda b, i: (b, 0, i, 0))
    row = lambda c: pl.BlockSpec((None, tm, c), lambda b, i: (b, i, 0))
    return pl.pallas_call(
        body, name="s5_in_bwd", grid=(B, S // tm),
        in_specs=[sspec, sspec, row(SW), pl.BlockSpec((SW, NS2), lambda b, i: (0, 0)), row(SW), row(PW - SW)],
        out_specs=[row(PW), pl.BlockSpec((SW, NS2), lambda b, i: (0, 0))],
        out_shape=[jax.ShapeDtypeStruct((B, S, PW), bf16), jax.ShapeDtypeStruct((SW, NS2), f32)],
        compiler_params=_params(("arbitrary", "arbitrary")))(gr, gi, p3, bbd, du_skip, duv3)


BIG = ['ev_w_in', 'ev_w_out', 'od_w_in', 'od_w_out', 'ffn_w_up', 'ffn_w_down']
ANY = pl.BlockSpec(memory_space=pl.ANY)


def _rtile(rows, mult):
    best = None
    for d in range(mult, min(rows, 512) + 1, mult):
        if rows % d == 0:
            best = d
    assert best is not None, (rows, mult)
    return best


def _pair_sum(name, g, recv, c_idx, out_dtype):
    NCH, R, W = g.shape
    HALF_W = W // 2
    tr = _rtile(R, 16)

    def body(c_ref, a_ref, b_ref, o_ref):
        o_ref[...] = (a_ref[...] + b_ref[...]).astype(out_dtype)

    return pl.pallas_call(
        body, name=name,
        grid_spec=pltpu.PrefetchScalarGridSpec(
            num_scalar_prefetch=1, grid=(NCH, R // tr),
            in_specs=[pl.BlockSpec((None, tr, HALF_W), lambda j, i, c: (j, i, c[0])),
                      pl.BlockSpec((None, tr, HALF_W), lambda j, i, c: (j, i, 0))],
            out_specs=pl.BlockSpec((None, tr, HALF_W), lambda j, i, c: (j, i, 0))),
        out_shape=jax.ShapeDtypeStruct((NCH, R, HALF_W), out_dtype),
        compiler_params=_params(("parallel", "parallel")))(c_idx, g, recv)


def _chip_sum(name, r3, h, k_idx):
    NCH, R, Wh = r3.shape
    tr = _rtile(R, 16)

    def body(k_ref, a_ref, own_ref, o_ref):
        own = own_ref[...].astype(f32)
        t = [jnp.where(k_ref[0] == s, own, a_ref[s].astype(f32)) for s in range(NCH)]
        o_ref[...] = ((t[0] + t[1]) + t[2]) + t[3]

    return pl.pallas_call(
        body, name=name,
        grid_spec=pltpu.PrefetchScalarGridSpec(
            num_scalar_prefetch=1, grid=(R // tr,),
            in_specs=[pl.BlockSpec((NCH, tr, Wh), lambda i, k: (0, i, 0)),
                      pl.BlockSpec((None, tr, Wh), lambda i, k: (k[0], i, 0))],
            out_specs=pl.BlockSpec((tr, Wh), lambda i, k: (i, 0))),
        out_shape=jax.ShapeDtypeStruct((R, Wh), f32),
        compiler_params=_params(("parallel",)))(k_idx, r3, h)


def _adam_math(gg, w, m, v):
    nm = ADAM_B1 * m + (1.0 - ADAM_B1) * gg
    nv = ADAM_B2 * v + (1.0 - ADAM_B2) * jnp.square(gg)
    m_hat = nm / (1.0 - ADAM_B1 ** ADAM_STEP)
    v_hat = nv / (1.0 - ADAM_B2 ** ADAM_STEP)
    return -ADAM_LR * (m_hat / (jnp.sqrt(v_hat) + ADAM_EPS) + ADAM_WD * w), nm, nv


def _adamw(name, mine, theirs, c_idx, w, m, v, lead, transposed, prev=None):
    L, R, W = w.shape
    if transposed:
        bw = LANES if W % LANES == 0 else W
        gspec = pl.BlockSpec((bw, R // 2), lambda i, hf, c: (i, 0))
        wspec = pl.BlockSpec((None, R // 2, bw), lambda i, hf, c: (lead, hf, i))
        grid = (W // bw, 2)
    else:
        tr = _rtile(R, SUBLANES)
        gspec = pl.BlockSpec((tr, W // 2), lambda i, hf, c: (i, 0))
        wspec = pl.BlockSpec((None, tr, W // 2), lambda i, hf, c: (lead, i, hf))
        grid = (R // tr, 2)

    def body(c_ref, a_ref, b_ref, w_ref, m_ref, v_ref, *rest):
        go_ref, d_ref, nm_ref, nv_ref = rest[-4:]
        gg = jnp.where(pl.program_id(1) == c_ref[0], a_ref[...], b_ref[...])
        if transposed:
            gg = gg.T
        d, nm, nv = _adam_math(gg, w_ref[...], m_ref[...], v_ref[...])
        go_ref[...] = gg
        d_ref[...] = d
        nm_ref[...] = nm
        nv_ref[...] = nv

    in_specs = [gspec, gspec, wspec, wspec, wspec]
    args, aliases = [c_idx, mine, theirs, w, m, v], {}
    if prev is not None:
        in_specs += [ANY] * 4
        args += list(prev)
        aliases = {6: 0, 7: 1, 8: 2, 9: 3}
    s = jax.ShapeDtypeStruct((L, R, W), f32)
    return pl.pallas_call(
        body, name=name,
        grid_spec=pltpu.PrefetchScalarGridSpec(num_scalar_prefetch=1, grid=grid, in_specs=in_specs,
                                               out_specs=[wspec] * 4),
        out_shape=[s, s, s, s], input_output_aliases=aliases,
        compiler_params=_params(("parallel", "arbitrary")))(*args)


def _place():
    x, y, c = lax.axis_index("x"), lax.axis_index("y"), lax.axis_index("c")
    return x, y, c, [(1 - x, y), (x, 1 - y), (1 - x, 1 - y)]


def _gathered_shape(sh, kind):
    if kind == "rows":
        return sh[:-2] + (N_CHIPS * sh[-2], sh[-1])
    if kind == "cols":
        return sh[:-1] + (N_CHIPS * sh[-1],)
    return (N_CHIPS,) + sh


def _place_shard(name, shard, kind, k_idx):
    sh = shard.shape
    r, C = sh[-2], sh[-1]
    L = sh[0] if len(sh) == 3 else 1
    tr = _rtile(r, 16)
    nr = r // tr
    if kind == "rows":
        out3, omap = (L, N_CHIPS * r, C), lambda l, i, k: (l, k[0] * nr + i, 0)
    elif kind == "cols":
        out3, omap = (L, r, N_CHIPS * C), lambda l, i, k: (l, i, k[0])
    else:
        out3, omap = (N_CHIPS, r, C), lambda l, i, k: (k[0], i, 0)

    def body(k_ref, s_ref, o_ref):
        o_ref[...] = s_ref[...]

    out = pl.pallas_call(
        body, name=name,
        grid_spec=pltpu.PrefetchScalarGridSpec(
            num_scalar_prefetch=1, grid=(L, nr),
            in_specs=[pl.BlockSpec((None, tr, C), lambda l, i, k: (l, i, 0))],
            out_specs=pl.BlockSpec((None, tr, C), omap)),
        out_shape=jax.ShapeDtypeStruct(out3, shard.dtype),
        compiler_params=_params(("parallel", "parallel")))(k_idx, shard.reshape(L, r, C))
    return out.reshape(_gathered_shape(sh, kind))


def _gather_phase(shards, fulls, kinds):
    n = len(shards)
    shapes = [s.shape for s in shards]

    def window(ref, a, k, h=None):
        sh, kind = shapes[a], kinds[a]
        r = sh[-2]
        start, size = (0, r) if h is None else (h * (r // 2), r // 2)
        lead = (slice(None),) * (len(sh) - 2)
        if kind == "rows":
            return ref.at[lead + (pl.ds(k * r + start, size), slice(None))]
        if kind == "cols":
            return ref.at[lead + (pl.ds(start, size), pl.ds(pl.multiple_of(k * sh[-1], LANES), sh[-1]))]
        return ref.at[(k,) + lead + (pl.ds(start, size), slice(None))]

    def copies(s_refs, o_refs, sems):
        send_sems, recv_sems = sems
        x, y, c, chips = _place()
        k = 2 * x + y

        def copy(a, j, kk, hh, to, src=None):
            dst = window(o_refs[a], a, kk, hh)
            return pltpu.make_async_remote_copy(
                src_ref=dst if src is None else src, dst_ref=dst, send_sem=send_sems.at[6 * a + j],
                recv_sem=recv_sems.at[6 * a + j], device_id=to, device_id_type=MESH)

        first = []
        for a in range(n):
            r = shapes[a][-2]
            lead = (slice(None),) * (len(shapes[a]) - 2)
            src = s_refs[a].at[lead + (pl.ds(c * (r // 2), r // 2), slice(None))]
            first += [copy(a, j, k, c, (*chip, c), src=src) for j, chip in enumerate(chips)]
        return copy, first, (x, y, c), (x, y, 1 - c), c, chips

    def start(s_refs, o_refs, sems):
        for cp in copies(s_refs, o_refs, sems)[1]:
            cp.start()

    def finish(s_refs, o_refs, sems):
        copy, first, me, sibling, c, chips = copies(s_refs, o_refs, sems)
        passed = []
        for j, (cx, cy) in enumerate(chips):
            for a in range(n):
                copy(a, j, 2 * cx + cy, c, me).wait_recv()
                fwd = copy(a, 3 + j, 2 * cx + cy, c, sibling)
                fwd.start()
                passed.append(fwd)
        for j, (cx, cy) in enumerate(chips):
            for a in range(n):
                copy(a, 3 + j, 2 * cx + cy, 1 - c, me).wait_recv()
        for cp in first + passed:
            cp.wait_send()

    return _Phase(shards, fulls, [jax.ShapeDtypeStruct(f.shape, f.dtype) for f in fulls],
                  [pltpu.SemaphoreType.DMA((6 * n,)), pltpu.SemaphoreType.DMA((6 * n,))], start, finish)


def _comm_pair_swap(tag, gs):
    n = len(gs)

    def body(*refs):
        g_refs, o_refs, send_sems, recv_sems = refs[:n], refs[n:2 * n], refs[2 * n], refs[2 * n + 1]
        x, y, c, _ = _place()
        half = [g.shape[2] // 2 for g in gs]
        cps = [pltpu.make_async_remote_copy(
            src_ref=g_refs[a].at[:, :, pl.ds(pl.multiple_of((1 - c) * half[a], LANES), half[a])], dst_ref=o_refs[a], send_sem=send_sems.at[a],
            recv_sem=recv_sems.at[a], device_id=(x, y, 1 - c), device_id_type=MESH) for a in range(n)]
        for cp in cps:
            cp.start()
        for cp in cps:
            cp.wait()

    return pl.pallas_call(
        body, name="comm_pair_swap_" + tag, in_specs=[ANY] * n, out_specs=[ANY] * n,
        out_shape=[jax.ShapeDtypeStruct(g.shape[:2] + (g.shape[2] // 2,), g.dtype) for g in gs],
        scratch_shapes=[pltpu.SemaphoreType.DMA((n,)), pltpu.SemaphoreType.DMA((n,))])(*gs)


def _exchange_phase(hs):
    n = len(hs)

    def copies(h_refs, o_refs, sems):
        send_sems, recv_sems = sems
        x, y, c, chips = _place()
        k = 2 * x + y

        def copy(a, j, src_slot, dst_slot):
            cx, cy = chips[j]
            return pltpu.make_async_remote_copy(
                src_ref=h_refs[a].at[src_slot], dst_ref=o_refs[a].at[dst_slot], send_sem=send_sems.at[3 * a + j],
                recv_sem=recv_sems.at[3 * a + j], device_id=(cx, cy, c), device_id_type=MESH)

        sends = [copy(a, j, 2 * cx + cy, k) for a in range(n) for j, (cx, cy) in enumerate(chips)]
        return copy, sends, k, chips

    def start(h_refs, o_refs, sems):
        for cp in copies(h_refs, o_refs, sems)[1]:
            cp.start()

    def finish(h_refs, o_refs, sems):
        copy, sends, k, chips = copies(h_refs, o_refs, sems)
        for a in range(n):
            for j, (cx, cy) in enumerate(chips):
                copy(a, j, k, 2 * cx + cy).wait_recv()
        for cp in sends:
            cp.wait_send()

    return _Phase(hs, [], [jax.ShapeDtypeStruct(h.shape, h.dtype) for h in hs],
                  [pltpu.SemaphoreType.DMA((3 * n,)), pltpu.SemaphoreType.DMA((3 * n,))], start, finish)


def _comm_pair_share(tag, gs):
    n = len(gs)

    def body(*refs):
        g_refs, o_refs, send_sems, recv_sems = refs[:n], refs[n:2 * n], refs[2 * n], refs[2 * n + 1]
        x, y, c, _ = _place()
        cps = [pltpu.make_async_remote_copy(
            src_ref=g_refs[a], dst_ref=o_refs[a], send_sem=send_sems.at[a], recv_sem=recv_sems.at[a],
            device_id=(x, y, 1 - c), device_id_type=MESH) for a in range(n)]
        for cp in cps:
            cp.start()
        for cp in cps:
            cp.wait()

    return pl.pallas_call(
        body, name="comm_pair_share_" + tag, in_specs=[ANY] * n, out_specs=[ANY] * n,
        out_shape=[jax.ShapeDtypeStruct(g.shape, g.dtype) for g in gs],
        scratch_shapes=[pltpu.SemaphoreType.DMA((n,)), pltpu.SemaphoreType.DMA((n,))])(*gs)


def _pad_rows(flat, unit):
    n = flat.shape[-1]
    pad = (-n) % unit
    if pad:
        flat = jnp.pad(flat, [(0, 0)] * (flat.ndim - 1) + [(0, pad)])
    return flat


def _split_chips(full, axis):
    sh = full.shape
    t = full.reshape(sh[:axis] + (N_CHIPS, sh[axis] // N_CHIPS) + sh[axis + 1:])
    return jnp.moveaxis(t, axis, 0).reshape(N_CHIPS, -1)


def _join_chips(stack, shard_shape, axis):
    t = jnp.moveaxis(stack.reshape((N_CHIPS,) + tuple(shard_shape)), 0, axis)
    sh = t.shape
    return t.reshape(sh[:axis] + (sh[axis] * sh[axis + 1],) + sh[axis + 2:])


def _block_diag(blocks):
    G, r, c = blocks.shape
    eye = jnp.eye(G, dtype=blocks.dtype)
    return (blocks[:, :, None, :] * eye[:, None, :, None]).reshape(G * r, G * c)


def _diag_blocks(m, G):
    r, c = m.shape[0] // G, m.shape[1] // G
    idx = jnp.arange(G)
    return m.reshape(G, r, G, c)[idx, :, idx, :]


def _weight_shards(w):
    conv = jnp.concatenate([w[n].reshape(-1) for n in GATHER_F32])
    conv = _pad_rows(conv, 2 * SUBLANES * LANES).reshape(-1, LANES)
    b16 = lambda a: a.astype(bf16)
    return {'ev_w_in': (b16(w['ev_w_in'][0]), "chip"), 'ev_w_out': (b16(w['ev_w_out'][0]), "rows"),
            's5_w_glu': (b16(w['s5_w_glu'][0]), "rows"), 'conv': (conv, "chip"),
            'od_w_in': (b16(w['od_w_in'][0]), "cols"), 'od_w_out': (b16(w['od_w_out'][0]), "rows"),
            'ffn_w_up0': (b16(w['ffn_w_up'][0]), "cols"), 'ffn_w_up1': (b16(w['ffn_w_up'][1]), "cols"),
            'ffn_w_down0': (b16(w['ffn_w_down'][0]), "rows"), 'ffn_w_down1': (b16(w['ffn_w_down'][1]), "rows")}


def kernel(x, mix_norm_g, ffn_norm_g, final_norm_g, ev_w_in, ev_w_out, s5_lam_re, s5_lam_im, s5_log_dt, s5_b_re, s5_b_im, s5_c_re, s5_c_im, s5_d, s5_w_glu, s5_b_glu, gm_w_s, gm_b_s, gm_v_g, od_w_in, od_conv_w, od_conv_b, od_w_out, ffn_w_up, ffn_conv_w, ffn_conv_b, ffn_w_down, loss_target, m_mix_norm_g, m_ffn_norm_g, m_final_norm_g, m_ev_w_in, m_ev_w_out, m_s5_lam_re, m_s5_lam_im, m_s5_log_dt, m_s5_b_re, m_s5_b_im, m_s5_c_re, m_s5_c_im, m_s5_d, m_s5_w_glu, m_s5_b_glu, m_gm_w_s, m_gm_b_s, m_gm_v_g, m_od_w_in, m_od_conv_w, m_od_conv_b, m_od_w_out, m_ffn_w_up, m_ffn_conv_w, m_ffn_conv_b, m_ffn_w_down, v_mix_norm_g, v_ffn_norm_g, v_final_norm_g, v_ev_w_in, v_ev_w_out, v_s5_lam_re, v_s5_lam_im, v_s5_log_dt, v_s5_b_re, v_s5_b_im, v_s5_c_re, v_s5_c_im, v_s5_d, v_s5_w_glu, v_s5_b_glu, v_gm_w_s, v_gm_b_s, v_gm_v_g, v_od_w_in, v_od_conv_w, v_od_conv_b, v_od_w_out, v_ffn_w_up, v_ffn_conv_w, v_ffn_conv_b, v_ffn_w_down):
    loc = dict(locals())
    w = {n: loc[n] for n in WEIGHTS}
    mom = {n: loc["m_" + n] for n in WEIGHTS}
    var = {n: loc["v_" + n] for n in WEIGHTS}

    B, S, D = x.shape
    T = B * S
    SW = s5_d.shape[1]
    G = SW // SSM_GROUP
    NS = G * SSM_STATE
    NB = NS // LANES
    tm = min(256, S)
    tt = min(1024, T)
    c_idx = lax.axis_index("c").astype(jnp.int32).reshape(1)
    k_idx = (2 * lax.axis_index("x") + lax.axis_index("y")).astype(jnp.int32).reshape(1)
    shards = _weight_shards(w)
    placed = {n: _place_shard("place_" + n, s, kd, k_idx) for n, (s, kd) in shards.items()}

    def gather(names):
        return _gather_phase([shards[n][0] for n in names], [placed[n] for n in names], [shards[n][1] for n in names])

    w_ev_in, w_ev_out, w_glu, conv = _run_phase("comm_gather_mixer0", gather(['ev_w_in', 'ev_w_out', 's5_w_glu', 'conv']))
    w_ev_in = jnp.swapaxes(w_ev_in, 0, 1).reshape(D, -1)
    full, off = {}, 0
    for n in GATHER_F32:
        full[n] = _join_chips(conv.reshape(N_CHIPS, -1)[:, off:off + w[n].size], w[n].shape, SHARD_AXIS[n])
        off += w[n].size

    h0 = x.reshape(T, D)
    y0, p0 = _norm_mm("ev_in", h0, mix_norm_g[0], w_ev_in, tm)
    PW = p0.shape[1]
    p03 = p0.reshape(B, S, PW)
    lr, li, ldt = s5_lam_re[0], s5_lam_im[0], s5_log_dt[0].reshape(G, 1)
    ar, ai, zr, zi = _s5_prep(lr, li, ldt)
    bre = _block_diag(jnp.swapaxes(s5_b_re[0], 1, 2))
    bim = _block_diag(jnp.swapaxes(s5_b_im[0], 1, 2))
    cbr = _block_diag(jnp.swapaxes(s5_c_re[0], 1, 2)).astype(bf16)
    cbi = _block_diag(jnp.swapaxes(s5_c_im[0], 1, 2)).astype(bf16)
    zr_row, zi_row = zr.reshape(1, NS), zi.reshape(1, NS)
    bbd = _s5_bbd(zr_row, zi_row, bre, bim)
    ar_s, ai_s = ar.reshape(NB, 1, LANES), ai.reshape(NB, 1, LANES)
    xr, xi = _s5_in(p03, bbd, SW, tm)
    (hr, hi), (w_up0, w_down0) = _s5_scan("s5_scan", xr, xi, ar_s, ai_s, False,
                                           phase=gather(['ffn_w_up0', 'ffn_w_down0']))
    dsk, bglu = s5_d.reshape(1, SW), s5_b_glu.reshape(1, SW)
    a_out = _s5_out(hr, hi, p03, cbr, cbi, dsk, w_glu, bglu, tm)
    ws, bst, gv = gm_w_s[0], gm_b_s[0].T, gm_v_g.reshape(1, -1)
    mixcat = _gmlp(p0, a_out.reshape(T, SW), ws, bst, gv, SW)
    h1 = _mm_resid("ev_out", mixcat, w_ev_out, h0, tm)

    def ffn_fwd(l, h, w_up, w_down, up_phase=None, act_phase=None):
        res = _norm_mm(f"ffn_up{l}", h, ffn_norm_g[l], w_up, tm, phase=up_phase)
        (z, up), got_up = res if up_phase is not None else (res, None)
        up3 = up.reshape(B, S, -1)
        res = _ffn_act(f"ffn_act{l}", up3, full['ffn_conv_w'][l], ffn_conv_b[l].reshape(1, -1), phase=act_phase)
        (act,), got_act = res if act_phase is not None else (res, None)
        hn = _mm_resid(f"ffn_down{l}", act.reshape(T, -1), w_down, h, tm)
        return hn, (z, up3, act), got_up, got_act

    h2, ffn0, (w_up1, w_down1), (w_od_in, w_od_out) = ffn_fwd(
        0, h1, w_up0, w_down0, gather(['ffn_w_up1', 'ffn_w_down1']), gather(['od_w_in', 'od_w_out']))
    w_ups, w_downs = (w_up0, w_up1), (w_down0, w_down1)
    od_cw, od_cb = full['od_conv_w'][0], full['od_conv_b']
    y1, p1 = _norm_mm("od_in", h2, mix_norm_g[1], w_od_in, tm)
    p13 = p1.reshape(B, S, -1)
    sc = _od_act(p13, od_cw, od_cb)
    h3 = _mm_resid("od_out", sc.reshape(T, D), w_od_out, h2, tm)
    h4, ffn1, _, _ = ffn_fwd(1, h3, w_up1, w_down1)

    dh4, dh4b, loss_part, d_final_g = _final_loss(h4, final_norm_g, loss_target.reshape(T, D), tm)
    loss = lax.psum(loss_part[0, 0], ("x", "y", "c"))

    grads = {}

    halves = {}

    def reduce_begin(tag, names, parts):
        recv = _comm_pair_swap(tag, parts)
        return [_pair_sum(f"pair_sum_{n}", g, r, c_idx, f32 if n == "small" else bf16)
                for n, g, r in zip(names, parts, recv)]

    def reduce_end(tag, names, hsum, r3):
        mine = [_chip_sum(f"chip_sum_{n}", r, h, k_idx) for n, r, h in zip(names, r3, hsum)]
        theirs = _comm_pair_share(tag, mine)
        halves.update({n: (a, b) for n, a, b in zip(names, mine, theirs)})

    def ffn_bwd(l, dh, dhb, h_in, saved, phase=None):
        z, up3, act = saved
        w_down, w_up = w_downs[l], w_ups[l]
        da = _mm_nt(f"ffn_down_bwd{l}", dhb, w_down, tm)
        g_down = _mm_tn(f"ffn_down_dw{l}", act.reshape(T, -1), dhb, tt)
        res = _ffn_act_bwd(f"ffn_act_bwd{l}", up3, da.reshape(B, S, -1), full['ffn_conv_w'][l],
                           ffn_conv_b[l].reshape(1, -1), phase=phase)
        (dg3, dv3, dcwg, dcwv, dcbg, dcbv), got = res if phase is not None else (res, None)
        dupg, dupv = dg3.reshape(T, -1), dv3.reshape(T, -1)
        F = dupg.shape[1]
        g_up = _mm_tn(f"ffn_up_dw{l}_gate", dupg, z, tt, rows=2 * F)
        g_up = _mm_tn(f"ffn_up_dw{l}_val", dupv, z, tt, rows=2 * F, row_off=F, prev=g_up)
        dh_new, dhb_new, dg = _mm_nt_normbwd(f"ffn_up_bwd{l}", [dupg, dupv], w_up, h_in, ffn_norm_g[l], dh, tm)
        dcw = jnp.concatenate([dcwg, dcwv], axis=1)
        dcb = jnp.concatenate([dcbg, dcbv], axis=1)
        return dh_new, dhb_new, g_down, g_up, dcw, dcb[0], dg[0], got

    chips = lambda g: g.reshape(N_CHIPS, -1, D)
    dh3, dh3b, gd1, gu1, gcw1, gcb1, gng1, _ = ffn_bwd(1, dh4, dh4b, h3, ffn1)
    dsc = _mm_nt("od_out_bwd", dh3b, w_od_out, tm)
    g_od_out = _mm_tn("od_out_dw", sc.reshape(T, D), dh3b, tt)
    dbg3, dcg3, dhx3, d_od_cw, d_od_cb = _od_act_bwd(p13, dsc.reshape(B, S, D), od_cw, od_cb)
    dp1 = [t.reshape(T, D) for t in (dbg3, dcg3, dhx3)]
    g_od_in = None
    for i, piece in enumerate(dp1):
        g_od_in = _mm_tn(f"od_in_dw{i}", piece, y1, tt, rows=3 * D, row_off=i * D, prev=g_od_in)
    grads['od_conv_w'] = d_od_cw[None]
    grads['od_conv_b'] = d_od_cb
    dh2, dh2b, gmix1 = _mm_nt_normbwd("od_in_bwd", dp1, w_od_in, h2, mix_norm_g[1], dh3, tm)
    layer1 = ['ffn_w_down1', 'ffn_w_up1', 'od_w_out', 'od_w_in']
    hsum1 = reduce_begin("layer1", layer1, [chips(g) for g in (gd1, gu1, g_od_out, g_od_in)])
    dh1, dh1b, gd0, gu0, gcw0, gcb0, gng0, r3 = ffn_bwd(0, dh2, dh2b, h1, ffn0, phase=_exchange_phase(hsum1))
    reduce_end("layer1", layer1, hsum1, r3)
    ffn0_names = ['ffn_w_down0', 'ffn_w_up0']
    hsum0 = reduce_begin("ffn0", ffn0_names, [chips(gd0), chips(gu0)])
    grads['ffn_conv_w'] = jnp.stack([gcw0, gcw1])
    grads['ffn_conv_b'] = jnp.stack([gcb0, gcb1])
    grads['ffn_norm_g'] = jnp.stack([gng0, gng1])
    grads['final_norm_g'] = d_final_g[0]

    dmix = _mm_nt("ev_out_bwd", dh1b, w_ev_out, tm)
    g_ev_out = _mm_tn("ev_out_dw", mixcat, dh1b, tt)
    duv, d_ws, d_bs, d_gv = _gmlp_bwd(p0, dmix, ws, bst, gv, SW)
    grads['gm_w_s'] = d_ws[None]
    grads['gm_b_s'] = d_bs[:, :, 0][None]
    grads['gm_v_g'] = d_gv
    dhr, dhi, du_skip, d_cbr, d_cbi, d_dsk, d_wglu, d_bglu = _s5_out_bwd(
        hr, hi, p03, dmix.reshape(B, S, D), cbr, cbi, dsk, w_glu, bglu, tm)
    grads['s5_c_re'] = jnp.swapaxes(_diag_blocks(d_cbr, G), 1, 2)[None]
    grads['s5_c_im'] = jnp.swapaxes(_diag_blocks(d_cbi, G), 1, 2)[None]
    grads['s5_d'] = d_dsk
    grads['s5_w_glu'] = d_wglu[None]
    grads['s5_b_glu'] = d_bglu
    (gr, gi, dar, dai), r3 = _s5_scan("s5_rscan", dhr, dhi, ar_s, ai_s, True, hr, hi, phase=_exchange_phase(hsum0))
    reduce_end("ffn0", ffn0_names, hsum0, r3)
    dp03, d_bbd = _s5_in_bwd(gr, gi, p03, bbd, du_skip, duv.reshape(B, S, -1), tm)
    d_bre, d_bim, d_zr, d_zi = _s5_bbd_bwd(d_bbd, zr_row, zi_row, bre, bim)
    grads['s5_b_re'] = jnp.swapaxes(_diag_blocks(d_bre, G), 1, 2)[None]
    grads['s5_b_im'] = jnp.swapaxes(_diag_blocks(d_bim, G), 1, 2)[None]
    shp = (-1, G, SSM_STATE)
    d_lr, d_li, d_ldt = _s5_prep_bwd(lr, li, ldt, dar.reshape(shp), dai.reshape(shp), d_zr.reshape(shp),
                                     d_zi.reshape(shp))
    grads['s5_lam_re'] = d_lr[None]
    grads['s5_lam_im'] = d_li[None]
    grads['s5_log_dt'] = d_ldt.reshape(1, G)
    dp0 = dp03.reshape(T, PW)
    g_ev_in = _mm_tn("ev_in_dw", dp0, y0, tt)
    grad_x, _, gmix0 = _mm_nt_normbwd("ev_in_bwd", [dp0], w_ev_in, h0, mix_norm_g[0], dh1, tm)
    grads['mix_norm_g'] = jnp.concatenate([gmix0, gmix1], axis=0)

    small = [n for n in WEIGHTS if n not in BIG]
    segs = []
    for n in small:
        gfull = grads[n].astype(f32)
        if n in SHARD_AXIS:
            segs.append(_split_chips(gfull, SHARD_AXIS[n]))
        else:
            segs.append(jnp.broadcast_to(gfull.reshape(1, -1), (N_CHIPS, gfull.size)))
    unit = 2 * SUBLANES * D
    gsmall = _pad_rows(jnp.concatenate(segs, axis=1), unit).reshape(N_CHIPS, -1, D)
    mixer0 = ['ev_w_out', 'ev_w_in', 'small']
    hsum = reduce_begin("mixer0", mixer0, [chips(g_ev_out), chips(g_ev_in), gsmall])
    reduce_end("mixer0", mixer0, hsum, _run_phase("comm_exchange_mixer0", _exchange_phase(hsum)))

    out_g, out_d, out_m, out_v = {}, {}, {}, {}

    def update(n, key, lead, transposed, prev=None):
        res = _adamw(f"adamw_{key}", *halves[key], c_idx, w[n], mom[n], var[n], lead, transposed, prev)
        out_g[n], out_d[n], out_m[n], out_v[n] = res
        return res

    update('ev_w_in', 'ev_w_in', 0, True)
    update('ev_w_out', 'ev_w_out', 0, False)
    update('od_w_in', 'od_w_in', 0, True)
    update('od_w_out', 'od_w_out', 0, False)
    update('ffn_w_up', 'ffn_w_up0', 0, True, prev=update('ffn_w_up', 'ffn_w_up1', 1, True))
    update('ffn_w_down', 'ffn_w_down0', 0, False, prev=update('ffn_w_down', 'ffn_w_down1', 1, False))

    def pack_local(d):
        flat = _pad_rows(jnp.concatenate([d[n].astype(f32).reshape(-1) for n in small]), unit)
        return flat.reshape(1, -1, D)

    res = _adamw("adamw_small", *halves['small'], c_idx, pack_local(w), pack_local(mom), pack_local(var), 0, False)
    for dst, p in zip((out_g, out_d, out_m, out_v), res):
        flat, off = p.reshape(-1), 0
        for n in small:
            dst[n] = flat[off:off + w[n].size].reshape(w[n].shape)
            off += w[n].size

    return (loss, grad_x.reshape(B, S, D), *[out_g[n] for n in WEIGHTS], *[out_d[n] for n in WEIGHTS],
            *[out_m[n] for n in WEIGHTS], *[out_v[n] for n in WEIGHTS])
```

```python
import functools
import math

import jax
import jax.numpy as jnp
from jax import lax
from jax.experimental import pallas as pl
from jax.experimental.pallas import tpu as pltpu

f32 = jnp.float32
bf16 = jnp.bfloat16
MESH = pl.DeviceIdType.MESH

SSM_GROUP = 16
SSM_STATE = 64
GMLP_HEAD = 128
CHUNK = 128
EPS = 1e-6
LAMBDA_RE_MAX = -1e-4
ADAM_LR, ADAM_B1, ADAM_B2, ADAM_EPS, ADAM_WD, ADAM_STEP = 0.001, 0.9, 0.999, 1e-08, 0.01, 10

LANES = 128
SUBLANES = 8
NSUB = 32
HALO = 16
VMEM_LIMIT = 56 * 1024 * 1024
N_CHIPS = 4

WEIGHTS = ['mix_norm_g', 'ffn_norm_g', 'final_norm_g', 'ev_w_in', 'ev_w_out', 's5_lam_re', 's5_lam_im', 's5_log_dt',
           's5_b_re', 's5_b_im', 's5_c_re', 's5_c_im', 's5_d', 's5_w_glu', 's5_b_glu', 'gm_w_s', 'gm_b_s', 'gm_v_g',
           'od_w_in', 'od_conv_w', 'od_conv_b', 'od_w_out', 'ffn_w_up', 'ffn_conv_w', 'ffn_conv_b', 'ffn_w_down']
SHARD_AXIS = {'ev_w_in': 2, 'ev_w_out': 1, 's5_w_glu': 1, 'od_w_in': 2, 'od_conv_w': 2, 'od_conv_b': 1, 'od_w_out': 1,
              'ffn_w_up': 2, 'ffn_conv_w': 2, 'ffn_w_down': 1}
GATHER_BF16 = ['ev_w_in', 'ev_w_out', 's5_w_glu', 'od_w_in', 'od_w_out', 'ffn_w_up', 'ffn_w_down']
GATHER_F32 = ['od_conv_w', 'od_conv_b', 'ffn_conv_w']

_GELU_K0 = math.sqrt(2.0 / math.pi)
_GELU_K1 = 0.044715
NT = (((1,), (1,)), ((), ()))
TN = (((0,), (0,)), ((), ()))


def _pick(n, cap):
    if n <= cap:
        return n
    best = None
    for d in range(LANES, cap + 1, LANES):
        if n % d == 0:
            best = d
    assert best is not None, (n, cap)
    return best


def _params(sem=None):
    return pltpu.CompilerParams(dimension_semantics=sem, vmem_limit_bytes=VMEM_LIMIT)


class _Phase:
    def __init__(self, ins, inplace, outs, sems, start, finish):
        self.ins, self.inplace, self.outs, self.sems = list(ins), list(inplace), list(outs), list(sems)
        self.start, self.finish = start, finish


def _call(body, name, grid, in_specs, out_specs, out_shape, args, scratch=(), sem=None, phase=None):
    if phase is None:
        return pl.pallas_call(body, name=name, grid=grid, in_specs=in_specs, out_specs=out_specs, out_shape=out_shape,
                              scratch_shapes=list(scratch), compiler_params=_params(sem))(*args)
    any_spec = pl.BlockSpec(memory_space=pl.ANY)
    n_in, n_out, n_scr = len(args), len(out_shape), len(scratch)
    p_in = phase.ins + phase.inplace
    ci, co = len(p_in), len(phase.outs)

    def wrapped(*refs):
        ins, cins = refs[:n_in], refs[n_in:n_in + len(phase.ins)]
        b = n_in + ci
        outs, couts = refs[b:b + n_out], refs[b + n_out:b + n_out + co]
        d = b + n_out + co
        scr, csem = refs[d:d + n_scr], refs[d + n_scr:]
        ids = [pl.program_id(i) for i in range(len(grid))]
        first = functools.reduce(jnp.logical_and, [i == 0 for i in ids])
        last = functools.reduce(jnp.logical_and, [i == g - 1 for i, g in zip(ids, grid)])

        @pl.when(first)
        def _():
            phase.start(cins, couts, csem)
        body(*ins, *outs, *scr)

        @pl.when(last)
        def _():
            phase.finish(cins, couts, csem)

    res = pl.pallas_call(
        wrapped, name=name, grid=grid, in_specs=list(in_specs) + [any_spec] * ci,
        out_specs=list(out_specs) + [any_spec] * co, out_shape=list(out_shape) + phase.outs,
        scratch_shapes=list(scratch) + phase.sems,
        input_output_aliases={n_in + len(phase.ins) + i: n_out + i for i in range(len(phase.inplace))},
        compiler_params=_params(tuple("arbitrary" for _ in grid)))(*args, *p_in)
    return res[:n_out], res[n_out:]


def _run_phase(name, phase):
    any_spec = pl.BlockSpec(memory_space=pl.ANY)
    ni, ci, co = len(phase.ins), len(phase.ins) + len(phase.inplace), len(phase.outs)

    def body(*refs):
        cins, couts, csem = refs[:ni], refs[ci:ci + co], refs[ci + co:]
        phase.start(cins, couts, csem)
        phase.finish(cins, couts, csem)

    return pl.pallas_call(
        body, name=name, in_specs=[any_spec] * ci, out_specs=[any_spec] * co, out_shape=phase.outs,
        scratch_shapes=phase.sems, input_output_aliases={ni + i: i for i in range(len(phase.inplace))})(
            *phase.ins, *phase.inplace)


def _gelu(x):
    return 0.5 * x * (1.0 + jnp.tanh(_GELU_K0 * (x + _GELU_K1 * x * x * x)))


def _gelu_grad(x):
    t = jnp.tanh(_GELU_K0 * (x + _GELU_K1 * x * x * x))
    return 0.5 * (1.0 + t) + 0.5 * x * (1.0 - t * t) * _GELU_K0 * (1.0 + 3.0 * _GELU_K1 * x * x)


def _rms_stats(x):
    r = lax.rsqrt(jnp.mean(x * x, axis=-1, keepdims=True) + EPS)
    return x * r, r


def _rms_bwd(dy, xh, r, g):
    dxh = dy * g
    dx = r * (dxh - xh * jnp.mean(dxh * xh, axis=-1, keepdims=True))
    return dx, jnp.sum(dy * xh, axis=0, keepdims=True)


def _dot(a, b):
    return jnp.dot(a, b, preferred_element_type=f32)


def _dg(a, b, dims):
    return lax.dot_general(a, b, dims, preferred_element_type=f32)


def _row_fold(z):
    return z.reshape(z.shape[0] // SUBLANES, SUBLANES, z.shape[1]).sum(axis=0)


def _norm_mm(name, h, g, w, tm, phase=None):
    T, D = h.shape
    N = w.shape[1]
    nc = _pick(N, 512)

    def body(h_ref, g_ref, w_ref, y_ref, o_ref):
        xh, _ = _rms_stats(h_ref[...])
        y = (xh * g_ref[...]).astype(bf16)
        y_ref[...] = y
        for j in range(N // nc):
            o_ref[:, j * nc:(j + 1) * nc] = _dot(y, w_ref[:, j * nc:(j + 1) * nc]).astype(bf16)

    return _call(
        body, name, (T // tm,),
        [pl.BlockSpec((tm, D), lambda i: (i, 0)), pl.BlockSpec((1, D), lambda i: (0, 0)),
         pl.BlockSpec((D, N), lambda i: (0, 0))],
        [pl.BlockSpec((tm, D), lambda i: (i, 0)), pl.BlockSpec((tm, N), lambda i: (i, 0))],
        [jax.ShapeDtypeStruct((T, D), bf16), jax.ShapeDtypeStruct((T, N), bf16)],
        [h, g.reshape(1, D), w], sem=("parallel",), phase=phase)


def _mm_resid(name, a, w, resid, tm):
    T, K = a.shape
    N = w.shape[1]

    def body(a_ref, w_ref, r_ref, o_ref):
        o_ref[...] = r_ref[...] + _dot(a_ref[...], w_ref[...])

    return pl.pallas_call(
        body, name=name, grid=(T // tm,),
        in_specs=[pl.BlockSpec((tm, K), lambda i: (i, 0)), pl.BlockSpec((K, N), lambda i: (0, 0)),
                  pl.BlockSpec((tm, N), lambda i: (i, 0))],
        out_specs=pl.BlockSpec((tm, N), lambda i: (i, 0)),
        out_shape=jax.ShapeDtypeStruct((T, N), f32),
        compiler_params=_params(("parallel",)))(a, w, resid)


def _mm_nt(name, dy, w, tm):
    T, N = dy.shape
    K = w.shape[0]
    kc = _pick(K, 512)

    def body(d_ref, w_ref, o_ref):
        d = d_ref[...].astype(bf16)
        for j in range(K // kc):
            o_ref[:, j * kc:(j + 1) * kc] = _dg(d, w_ref[j * kc:(j + 1) * kc, :], NT).astype(bf16)

    return pl.pallas_call(
        body, name=name, grid=(T // tm,),
        in_specs=[pl.BlockSpec((tm, N), lambda i: (i, 0)), pl.BlockSpec((K, N), lambda i: (0, 0))],
        out_specs=pl.BlockSpec((tm, K), lambda i: (i, 0)),
        out_shape=jax.ShapeDtypeStruct((T, K), bf16),
        compiler_params=_params(("parallel",)))(dy, w)


def _mm_nt_normbwd(name, dys, w, h, g, dh_in, tm):
    n = len(dys)
    T = dys[0].shape[0]
    D = w.shape[0]
    widths = [d.shape[1] for d in dys]
    offs = [sum(widths[:i]) for i in range(n)]

    def body(*refs):
        d_refs = refs[:n]
        w_ref, h_ref, g_ref, dh_ref, o_ref, ob_ref, dg_ref = refs[n:]
        dz = _dg(d_refs[0][...], w_ref[:, :widths[0]], NT)
        for i in range(1, n):
            dz += _dg(d_refs[i][...], w_ref[:, offs[i]:offs[i] + widths[i]], NT)
        xh, r = _rms_stats(h_ref[...])
        dx, dg = _rms_bwd(dz, xh, r, g_ref[...])
        out = dh_ref[...] + dx
        o_ref[...] = out
        ob_ref[...] = out.astype(bf16)

        @pl.when(pl.program_id(0) == 0)
        def _():
            dg_ref[...] = jnp.zeros_like(dg_ref)
        dg_ref[...] += dg

    row = lambda c: pl.BlockSpec((tm, c), lambda i: (i, 0))
    return pl.pallas_call(
        body, name=name, grid=(T // tm,),
        in_specs=[row(c) for c in widths] + [pl.BlockSpec((D, sum(widths)), lambda i: (0, 0)), row(D),
                                             pl.BlockSpec((1, D), lambda i: (0, 0)), row(D)],
        out_specs=[row(D), row(D), pl.BlockSpec((1, D), lambda i: (0, 0))],
        out_shape=[jax.ShapeDtypeStruct((T, D), f32), jax.ShapeDtypeStruct((T, D), bf16),
                   jax.ShapeDtypeStruct((1, D), f32)],
        compiler_params=_params(("arbitrary",)))(*dys, w, h, g.reshape(1, D), dh_in)


def _mm_tn(name, a, b, tt, rows=None, row_off=0, prev=None):
    T, K = a.shape
    N = b.shape[1]
    rows = K if rows is None else rows
    tk = _pick(K, 1408)
    tn = _pick(N, 1024)
    assert row_off % tk == 0
    kb = row_off // tk

    def body(a_ref, b_ref, *rest):
        o_ref = rest[-1]

        @pl.when(pl.program_id(2) == 0)
        def _():
            o_ref[...] = jnp.zeros_like(o_ref)
        o_ref[...] += _dg(a_ref[...], b_ref[...], TN)

    in_specs = [pl.BlockSpec((tt, tk), lambda k, n, t: (t, k)), pl.BlockSpec((tt, tn), lambda k, n, t: (t, n))]
    args, aliases = [a, b], {}
    if prev is not None:
        in_specs.append(ANY)
        args.append(prev)
        aliases = {2: 0}
    return pl.pallas_call(
        body, name=name, grid=(K // tk, N // tn, T // tt), in_specs=in_specs,
        out_specs=pl.BlockSpec((tk, tn), lambda k, n, t: (k + kb, n)),
        out_shape=jax.ShapeDtypeStruct((rows, N), f32), input_output_aliases=aliases,
        compiler_params=_params(("parallel", "parallel", "arbitrary")))(*args)


def _final_loss(h, g, tgt, tm):
    T, D = h.shape

    def body(h_ref, g_ref, t_ref, dh_ref, dhb_ref, loss_ref, dg_ref):
        xh, r = _rms_stats(h_ref[...])
        gg = g_ref[...]
        diff = xh * gg - t_ref[...]
        dy = diff * (1.0 / D)
        dx, dg = _rms_bwd(dy, xh, r, gg)
        dh_ref[...] = dx
        dhb_ref[...] = dx.astype(bf16)

        @pl.when(pl.program_id(0) == 0)
        def _():
            dg_ref[...] = jnp.zeros_like(dg_ref)
            loss_ref[...] = jnp.zeros_like(loss_ref)
        dg_ref[...] += dg
        loss_ref[...] += (0.5 / D) * jnp.sum(jnp.sum(diff * diff, axis=1, keepdims=True), axis=0, keepdims=True)

    return pl.pallas_call(
        body, name="final_loss", grid=(T // tm,),
        in_specs=[pl.BlockSpec((tm, D), lambda i: (i, 0)), pl.BlockSpec((1, D), lambda i: (0, 0)),
                  pl.BlockSpec((tm, D), lambda i: (i, 0))],
        out_specs=[pl.BlockSpec((tm, D), lambda i: (i, 0)), pl.BlockSpec((tm, D), lambda i: (i, 0)),
                   pl.BlockSpec((1, 1), lambda i: (0, 0)), pl.BlockSpec((1, D), lambda i: (0, 0))],
        out_shape=[jax.ShapeDtypeStruct((T, D), f32), jax.ShapeDtypeStruct((T, D), bf16),
                   jax.ShapeDtypeStruct((1, 1), f32), jax.ShapeDtypeStruct((1, D), f32)],
        compiler_params=_params(("arbitrary",)))(h, g.reshape(1, D), tgt)


def _taps(load, r0, R):
    main = load(r0, R)
    hs = pl.multiple_of(jnp.maximum(r0 - HALO, 0), HALO)
    halo = load(hs, HALO) * (r0 > 0).astype(f32)
    ext = jnp.concatenate([halo, main], axis=0)
    xm1 = pltpu.roll(ext, 1, 0)[HALO:]
    xm2 = pltpu.roll(ext, 2, 0)[HALO:]
    return xm2, xm1, main


def _conv(w, b, taps):
    return b + w[0:1] * taps[0] + w[1:2] * taps[1] + w[2:3] * taps[2]


def _ref_load(ref):
    return lambda s, n: ref[pl.ds(s, n), :].astype(f32)


def _ffn_down(name, up, cw, cb, w_down, resid, S, tm, phase=None):
    T, F2 = up.shape
    F = F2 // 2
    D = w_down.shape[1]
    cwid = _pick(F, 256)
    per_seq = S // tm

    def body(u_ref, halo_ref, cw_ref, cb_ref, w_ref, r_ref, o_ref):
        keep = (pl.program_id(0) % per_seq > 0).astype(f32)

        def conv(off):
            cols = slice(off, off + cwid)
            main = u_ref[:, cols].astype(f32)
            ext = jnp.concatenate([halo_ref[:, cols].astype(f32) * keep, main], axis=0)
            taps = (pltpu.roll(ext, 2, 0)[HALO:], pltpu.roll(ext, 1, 0)[HALO:], main)
            return _conv(cw_ref[:, cols], cb_ref[:, cols], taps)

        acc = r_ref[...]
        for j in range(F // cwid):
            cg, cv = conv(j * cwid), conv(F + j * cwid)
            a = (cg * jax.nn.sigmoid(cg) * cv).astype(bf16)
            acc = acc + _dot(a, w_ref[j * cwid:(j + 1) * cwid, :])
        o_ref[...] = acc

    full = lambda r, c: pl.BlockSpec((r, c), lambda i: (0, 0))
    return _call(
        body, name, (T // tm,),
        [pl.BlockSpec((tm, F2), lambda i: (i, 0)),
         pl.BlockSpec((HALO, F2), lambda i: (jnp.maximum(i * (tm // HALO) - 1, 0), 0)),
         full(3, F2), full(1, F2), full(F, D), pl.BlockSpec((tm, D), lambda i: (i, 0))],
        [pl.BlockSpec((tm, D), lambda i: (i, 0))], [jax.ShapeDtypeStruct((T, D), f32)],
        [up, up, cw, cb, w_down, resid], sem=("parallel",), phase=phase)


def _rev_conv_rows(d, nxt, w):
    R = d.shape[0]
    ext = jnp.concatenate([d, nxt], axis=0)
    n = R + HALO
    xp1 = pltpu.roll(ext, n - 1, 0)[:R]
    xp2 = pltpu.roll(ext, n - 2, 0)[:R]
    return w[2:3] * d + w[1:2] * xp1 + w[0:1] * xp2


def _conv_grad_acc(acc, dc, taps):
    return (acc[0] + _row_fold(dc * taps[0]), acc[1] + _row_fold(dc * taps[1]), acc[2] + _row_fold(dc * taps[2]),
            acc[3] + _row_fold(dc))


def _conv_grad_out(dcw_ref, dcb_ref, acc):
    @pl.when(pl.program_id(1) == 0)
    def _():
        dcw_ref[...] = jnp.zeros_like(dcw_ref)
        dcb_ref[...] = jnp.zeros_like(dcb_ref)
    for k in range(3):
        dcw_ref[k:k + 1, :] += jnp.sum(acc[k], axis=0, keepdims=True)
    dcb_ref[...] += jnp.sum(acc[3], axis=0, keepdims=True)


def _ffn_act_bwd(name, up3, dh3, w_down, cw, cb, phase=None):
    B, S, F2 = up3.shape
    F = F2 // 2
    D = dh3.shape[2]
    cwid = _pick(F, 256)
    nF = F // cwid
    R = min(256, S)
    nR = S // R

    def body(g_ref, v_ref, dh_ref, wd_ref, wg_ref, wv_ref, bg_ref, bv_ref,
             act_ref, dg_ref, dv_ref, dcwg_ref, dcwv_ref, dcbg_ref, dcbv_ref):
        wg, wv, bg, bv = wg_ref[...], wv_ref[...], bg_ref[...], bv_ref[...]

        def step(i, carry):
            ng, nv, accg, accv = carry
            r0 = pl.multiple_of((nR - 1 - i) * R, R)
            tg = _taps(_ref_load(g_ref), r0, R)
            tv = _taps(_ref_load(v_ref), r0, R)
            cg = _conv(wg, bg, tg)
            cv = _conv(wv, bv, tv)
            da = _dg(dh_ref[pl.ds(r0, R), :], wd_ref[...], NT)
            sg = jax.nn.sigmoid(cg)
            act_ref[pl.ds(r0, R), :] = (cg * sg * cv).astype(bf16)
            dgate = da * cv * (sg * (1.0 + cg * (1.0 - sg)))
            dval = da * (cg * sg)
            dg_ref[pl.ds(r0, R), :] = _rev_conv_rows(dgate, ng, wg).astype(bf16)
            dv_ref[pl.ds(r0, R), :] = _rev_conv_rows(dval, nv, wv).astype(bf16)
            return dgate[:HALO], dval[:HALO], _conv_grad_acc(accg, dgate, tg), _conv_grad_acc(accv, dval, tv)
        z = jnp.zeros((SUBLANES, cwid), f32)
        zh = jnp.zeros((HALO, cwid), f32)
        _, _, accg, accv = lax.fori_loop(0, nR, step, (zh, zh, (z, z, z, z), (z, z, z, z)))
        _conv_grad_out(dcwg_ref, dcbg_ref, accg)
        _conv_grad_out(dcwv_ref, dcbv_ref, accv)

    blk = lambda off: pl.BlockSpec((None, S, cwid), lambda j, b: (b, 0, off + j))
    wblk = lambda off: pl.BlockSpec((3, cwid), lambda j, b: (0, off + j))
    bblk = lambda off: pl.BlockSpec((1, cwid), lambda j, b: (0, off + j))
    half = jax.ShapeDtypeStruct((B, S, F), bf16)
    return _call(
        body, name, (nF, B),
        [blk(0), blk(nF), pl.BlockSpec((None, S, D), lambda j, b: (b, 0, 0)),
         pl.BlockSpec((cwid, D), lambda j, b: (j, 0)), wblk(0), wblk(nF), bblk(0), bblk(nF)],
        [blk(0), blk(0), blk(0), wblk(0), wblk(0), bblk(0), bblk(0)],
        [half, half, half, jax.ShapeDtypeStruct((3, F), f32), jax.ShapeDtypeStruct((3, F), f32),
         jax.ShapeDtypeStruct((1, F), f32), jax.ShapeDtypeStruct((1, F), f32)],
        [up3, up3, dh3, w_down, cw, cw, cb, cb], sem=("parallel", "arbitrary"), phase=phase)


def _od_act(p3, cw, cb):
    B, S, D3 = p3.shape
    D = D3 // 3
    cwid = _pick(D, 256)
    nD = D // cwid
    R = min(256, S)

    def body(bg_ref, cg_ref, hx_ref, w_ref, b_ref, o_ref):
        w, b = w_ref[...], b_ref[...]
        q = lambda s, n: cg_ref[pl.ds(s, n), :].astype(f32) * hx_ref[pl.ds(s, n), :].astype(f32)

        def chunk(r, c):
            r0 = pl.multiple_of(r * R, R)
            cq = _conv(w, b, _taps(q, r0, R))
            o_ref[pl.ds(r0, R), :] = (bg_ref[pl.ds(r0, R), :].astype(f32) * cq).astype(bf16)
            return c
        lax.fori_loop(0, S // R, chunk, 0)

    blk = lambda off: pl.BlockSpec((None, S, cwid), lambda b, j: (b, 0, off + j))
    return pl.pallas_call(
        body, name="od_act", grid=(B, nD),
        in_specs=[blk(0), blk(nD), blk(2 * nD), pl.BlockSpec((3, cwid), lambda b, j: (0, j)),
                  pl.BlockSpec((1, cwid), lambda b, j: (0, j))],
        out_specs=pl.BlockSpec((None, S, cwid), lambda b, j: (b, 0, j)),
        out_shape=jax.ShapeDtypeStruct((B, S, D), bf16),
        compiler_params=_params(("parallel", "parallel")))(p3, p3, p3, cw, cb)


def _od_act_bwd(p3, dsc3, cw, cb):
    B, S, D3 = p3.shape
    D = D3 // 3
    cwid = _pick(D, 256)
    nD = D // cwid
    R = min(256, S)
    nR = S // R

    def body(bg_ref, cg_ref, hx_ref, d_ref, w_ref, b_ref, dbg_ref, dcg_ref, dhx_ref, dcw_ref, dcb_ref):
        w, b = w_ref[...], b_ref[...]
        q = lambda s, n: cg_ref[pl.ds(s, n), :].astype(f32) * hx_ref[pl.ds(s, n), :].astype(f32)

        def step(i, carry):
            nxt, acc = carry
            r0 = pl.multiple_of((nR - 1 - i) * R, R)
            rows = pl.ds(r0, R)
            tq = _taps(q, r0, R)
            cq = _conv(w, b, tq)
            d = d_ref[rows, :].astype(f32)
            dbg_ref[rows, :] = (d * cq).astype(bf16)
            dcq = d * bg_ref[rows, :].astype(f32)
            dq = _rev_conv_rows(dcq, nxt, w)
            dcg_ref[rows, :] = (dq * hx_ref[rows, :].astype(f32)).astype(bf16)
            dhx_ref[rows, :] = (dq * cg_ref[rows, :].astype(f32)).astype(bf16)
            return dcq[:HALO], _conv_grad_acc(acc, dcq, tq)
        z = jnp.zeros((SUBLANES, cwid), f32)
        _, acc = lax.fori_loop(0, nR, step, (jnp.zeros((HALO, cwid), f32), (z, z, z, z)))
        _conv_grad_out(dcw_ref, dcb_ref, acc)

    blk = lambda off: pl.BlockSpec((None, S, cwid), lambda j, b: (b, 0, off + j))
    part = jax.ShapeDtypeStruct((B, S, D), bf16)
    return pl.pallas_call(
        body, name="od_act_bwd", grid=(nD, B),
        in_specs=[blk(0), blk(nD), blk(2 * nD), blk(0), pl.BlockSpec((3, cwid), lambda j, b: (0, j)),
                  pl.BlockSpec((1, cwid), lambda j, b: (0, j))],
        out_specs=[blk(0), blk(0), blk(0), pl.BlockSpec((3, cwid), lambda j, b: (0, j)),
                   pl.BlockSpec((1, cwid), lambda j, b: (0, j))],
        out_shape=[part, part, part, jax.ShapeDtypeStruct((3, D), f32), jax.ShapeDtypeStruct((1, D), f32)],
        compiler_params=_params(("parallel", "arbitrary")))(p3, p3, p3, dsc3, cw, cb)


def _gmlp_parts(p, gv, SW, GW):
    uv = p[:, SW:].astype(f32)
    ge = _gelu(uv)
    u, v = ge[:, :GW], ge[:, GW:]
    vh, r = _rms_stats(v)
    return uv, u, vh, r, vh * gv


def _tril():
    rows = lax.broadcasted_iota(jnp.int32, (CHUNK, CHUNK), 0)
    cols = lax.broadcasted_iota(jnp.int32, (CHUNK, CHUNK), 1)
    return rows >= cols


def _gmlp(p0, a_out, ws, bst, gv, SW):
    T, PW = p0.shape
    GW = (PW - SW) // 2
    H = GW // GMLP_HEAD
    D = SW + GW

    def body(p_ref, a_ref, ws_ref, b_ref, gv_ref, o_ref):
        _, u, _, _, vn = _gmlp_parts(p_ref[...], gv_ref[...], SW, GW)
        tri = _tril()
        o_ref[:, :SW] = a_ref[...]
        for hh in range(H):
            sl = slice(hh * GMLP_HEAD, (hh + 1) * GMLP_HEAD)
            wm = jnp.where(tri, ws_ref[hh], 0.0).astype(bf16)
            gate = _dot(wm, vn[:, sl].astype(bf16)) + b_ref[:, hh:hh + 1]
            o_ref[:, SW + hh * GMLP_HEAD:SW + (hh + 1) * GMLP_HEAD] = (u[:, sl] * gate).astype(bf16)

    return pl.pallas_call(
        body, name="gmlp", grid=(T // CHUNK,),
        in_specs=[pl.BlockSpec((CHUNK, PW), lambda i: (i, 0)), pl.BlockSpec((CHUNK, SW), lambda i: (i, 0)),
                  pl.BlockSpec((H, CHUNK, CHUNK), lambda i: (0, 0, 0)), pl.BlockSpec((CHUNK, H), lambda i: (0, 0)),
                  pl.BlockSpec((1, GW), lambda i: (0, 0))],
        out_specs=pl.BlockSpec((CHUNK, D), lambda i: (i, 0)),
        out_shape=jax.ShapeDtypeStruct((T, D), bf16),
        compiler_params=_params(("parallel",)))(p0, a_out, ws, bst, gv)


def _gmlp_bwd(p0, dmix, ws, bst, gv, SW):
    T, PW = p0.shape
    GW = (PW - SW) // 2
    H = GW // GMLP_HEAD
    D = SW + GW

    def body(p_ref, d_ref, ws_ref, b_ref, gv_ref, duv_ref, dws_ref, dbs_ref, dgv_ref):
        gv_ = gv_ref[...]
        uv, u, vh, r, vn = _gmlp_parts(p_ref[...], gv_, SW, GW)
        dout = d_ref[...][:, SW:].astype(f32)
        tri = _tril()

        @pl.when(pl.program_id(0) == 0)
        def _():
            dws_ref[...] = jnp.zeros_like(dws_ref)
            dbs_ref[...] = jnp.zeros_like(dbs_ref)
            dgv_ref[...] = jnp.zeros_like(dgv_ref)
        du, dvn = [], []
        for hh in range(H):
            sl = slice(hh * GMLP_HEAD, (hh + 1) * GMLP_HEAD)
            wm = jnp.where(tri, ws_ref[hh], 0.0).astype(bf16)
            vnh = vn[:, sl].astype(bf16)
            gate = _dot(wm, vnh) + b_ref[:, hh:hh + 1]
            dgate = dout[:, sl] * u[:, sl]
            du.append(dout[:, sl] * gate)
            dgb = dgate.astype(bf16)
            dws_ref[hh] += jnp.where(tri, _dg(dgb, vnh, NT), 0.0)
            dbs_ref[hh] += jnp.broadcast_to(jnp.sum(dgate, axis=1, keepdims=True), (CHUNK, CHUNK))
            dvn.append(_dg(wm, dgb, TN))
        dvn = jnp.concatenate(dvn, axis=1)
        dv, dgv = _rms_bwd(dvn, vh, r, gv_)
        dgv_ref[...] += dgv
        dge = jnp.concatenate(du + [dv], axis=1)
        duv_ref[...] = (dge * _gelu_grad(uv)).astype(bf16)

    return pl.pallas_call(
        body, name="gmlp_bwd", grid=(T // CHUNK,),
        in_specs=[pl.BlockSpec((CHUNK, PW), lambda i: (i, 0)), pl.BlockSpec((CHUNK, D), lambda i: (i, 0)),
                  pl.BlockSpec((H, CHUNK, CHUNK), lambda i: (0, 0, 0)), pl.BlockSpec((CHUNK, H), lambda i: (0, 0)),
                  pl.BlockSpec((1, GW), lambda i: (0, 0))],
        out_specs=[pl.BlockSpec((CHUNK, 2 * GW), lambda i: (i, 0)), pl.BlockSpec((H, CHUNK, CHUNK), lambda i: (0, 0, 0)),
                   pl.BlockSpec((H, CHUNK, CHUNK), lambda i: (0, 0, 0)), pl.BlockSpec((1, GW), lambda i: (0, 0))],
        out_shape=[jax.ShapeDtypeStruct((T, 2 * GW), bf16), jax.ShapeDtypeStruct((H, CHUNK, CHUNK), f32),
                   jax.ShapeDtypeStruct((H, CHUNK, CHUNK), f32), jax.ShapeDtypeStruct((1, GW), f32)],
        compiler_params=_params(("arbitrary",)))(p0, dmix, ws, bst, gv)


def _s5_disc(lr, li, ldt):
    lr = jnp.minimum(lr, LAMBDA_RE_MAX)
    dt = jnp.exp(ldt)
    mag = jnp.exp(lr * dt)
    ar = mag * jnp.cos(li * dt)
    ai = mag * jnp.sin(li * dt)
    den = lr * lr + li * li
    nr = ar - 1.0
    zr = (nr * lr + ai * li) / den
    zi = (ai * lr - nr * li) / den
    return ar, ai, zr, zi


def _s5_prep(lr, li, ldt):
    G, P = lr.shape

    def body(lr_ref, li_ref, ldt_ref, ar_ref, ai_ref, zr_ref, zi_ref):
        ar, ai, zr, zi = _s5_disc(lr_ref[...], li_ref[...], ldt_ref[...])
        ar_ref[...] = ar
        ai_ref[...] = ai
        zr_ref[...] = zr
        zi_ref[...] = zi

    s = jax.ShapeDtypeStruct((G, P), f32)
    return pl.pallas_call(body, name="s5_prep", out_shape=[s, s, s, s])(lr, li, ldt)


def _s5_prep_bwd(lr, li, ldt, dar, dai, dzr, dzi):
    G, P = lr.shape

    def body(lr_ref, li_ref, ldt_ref, dar_ref, dai_ref, dzr_ref, dzi_ref, o1, o2, o3):
        _, vjp = jax.vjp(_s5_disc, lr_ref[...], li_ref[...], ldt_ref[...])
        cts = tuple(jnp.sum(r[...], axis=0) for r in (dar_ref, dai_ref, dzr_ref, dzi_ref))
        a, b, c = vjp(cts)
        o1[...] = a
        o2[...] = b
        o3[...] = c

    s = jax.ShapeDtypeStruct((G, P), f32)
    return pl.pallas_call(body, name="s5_prep_bwd", out_shape=[s, s, jax.ShapeDtypeStruct((G, 1), f32)])(
        lr, li, ldt, dar, dai, dzr, dzi)


def _s5_bbd(zr, zi, bre, bim):
    SW, NS = bre.shape

    def body(zr_ref, zi_ref, br_ref, bi_ref, o_ref):
        zr_, zi_, br, bi = zr_ref[...], zi_ref[...], br_ref[...], bi_ref[...]
        o_ref[:, :NS] = (zr_ * br - zi_ * bi).astype(bf16)
        o_ref[:, NS:] = (zr_ * bi + zi_ * br).astype(bf16)

    return pl.pallas_call(body, name="s5_bbd", out_shape=jax.ShapeDtypeStruct((SW, 2 * NS), bf16))(zr, zi, bre, bim)


def _s5_bbd_bwd(dbbd, zr, zi, bre, bim):
    SW, NS = bre.shape

    def body(d_ref, zr_ref, zi_ref, br_ref, bi_ref, dbr_ref, dbi_ref, dzr_ref, dzi_ref):
        zr_, zi_, br, bi = zr_ref[...], zi_ref[...], br_ref[...], bi_ref[...]
        dr, di = d_ref[:, :NS], d_ref[:, NS:]
        dbr_ref[...] = zr_ * dr + zi_ * di
        dbi_ref[...] = zr_ * di - zi_ * dr
        dzr_ref[...] = jnp.sum(dr * br + di * bi, axis=0, keepdims=True)
        dzi_ref[...] = jnp.sum(di * br - dr * bi, axis=0, keepdims=True)

    m = jax.ShapeDtypeStruct((SW, NS), f32)
    v = jax.ShapeDtypeStruct((1, NS), f32)
    return pl.pallas_call(body, name="s5_bbd_bwd", out_shape=[m, m, v, v])(dbbd, zr, zi, bre, bim)


def _slab_cat(ref, NB):
    return jnp.concatenate([ref[j] for j in range(NB)], axis=1)


def _s5_in(p3, bbd, SW, tm):
    B, S, PW = p3.shape
    NS = bbd.shape[1] // 2
    NB = NS // LANES

    def body(u_ref, b_ref, xr_ref, xi_ref):
        x = _dot(u_ref[...], b_ref[...])
        for j in range(NB):
            xr_ref[j] = x[:, j * LANES:(j + 1) * LANES]
            xi_ref[j] = x[:, NS + j * LANES:NS + (j + 1) * LANES]

    slab = jax.ShapeDtypeStruct((B, NB, S, LANES), f32)
    sspec = pl.BlockSpec((None, NB, tm, LANES), lambda b, i: (b, 0, i, 0))
    return pl.pallas_call(
        body, name="s5_in", grid=(B, S // tm),
        in_specs=[pl.BlockSpec((None, tm, SW), lambda b, i: (b, i, 0)), pl.BlockSpec((SW, 2 * NS), lambda b, i: (0, 0))],
        out_specs=[sspec, sspec], out_shape=[slab, slab],
        compiler_params=_params(("parallel", "parallel")))(p3, bbd)


def _s5_scan(name, xr, xi, ar, ai, reverse, hr=None, hi=None, phase=None):
    B, NB, S, _ = xr.shape
    L = S // NSUB
    nb = 2 if (hr is None and NB % 2 == 0) else 1
    with_da = hr is not None

    def body(*refs):
        if with_da:
            xr_ref, xi_ref, ar_ref, ai_ref, hr_ref, hi_ref, or_ref, oi_ref, dar_ref, dai_ref, pr_scr, pi_scr = refs
        else:
            xr_ref, xi_ref, ar_ref, ai_ref, or_ref, oi_ref, pr_scr, pi_scr = refs
        sign = -1.0 if reverse else 1.0
        a_r = [jnp.broadcast_to(ar_ref[j], (NSUB, LANES)) for j in range(nb)]
        a_i = [jnp.broadcast_to(ai_ref[j], (NSUB, LANES)) * sign for j in range(nb)]

        def step(t, carry):
            row = (L - 1 - t) if reverse else t
            rows = pl.ds(row, NSUB, stride=L)
            out = []
            for j in range(nb):
                sr, si, pr, pi = carry[j]
                nr = a_r[j] * sr - a_i[j] * si + xr_ref.at[j][rows, :]
                ni = a_r[j] * si + a_i[j] * sr + xi_ref.at[j][rows, :]
                or_ref.at[j][rows, :] = nr
                oi_ref.at[j][rows, :] = ni
                npr = a_r[j] * pr - a_i[j] * pi
                npi = a_r[j] * pi + a_i[j] * pr
                pr_scr[j, pl.ds(row, 1), :] = npr[0:1]
                pi_scr[j, pl.ds(row, 1), :] = npi[0:1]
                out.append((nr, ni, npr, npi))
            return tuple(out)
        z = jnp.zeros((NSUB, LANES), f32)
        one = jnp.ones((NSUB, LANES), f32)
        fin = lax.fori_loop(0, L, step, tuple((z, z, one, z) for _ in range(nb)))

        for j in range(nb):
            sr, si, plr, pli = fin[j]
            plr, pli = plr[0:1], pli[0:1]
            cr = jnp.zeros((1, LANES), f32)
            ci = jnp.zeros((1, LANES), f32)
            order = range(NSUB - 2, -1, -1) if reverse else range(1, NSUB)
            for c in order:
                src = c + 1 if reverse else c - 1
                cr, ci = (sr[src:src + 1] + plr * cr - pli * ci, si[src:src + 1] + plr * ci + pli * cr)
                rows = slice(c * L, (c + 1) * L)
                tr, ti = pr_scr[j], pi_scr[j]
                or_ref[j, rows, :] += tr * cr - ti * ci
                oi_ref[j, rows, :] += tr * ci + ti * cr
            if with_da:
                first = lax.broadcasted_iota(jnp.int32, (L, LANES), 0) == 0
                dar = jnp.zeros((1, LANES), f32)
                dai = jnp.zeros((1, LANES), f32)
                for c in range(NSUB):
                    rows = slice(c * L, (c + 1) * L)
                    if c == 0:
                        lr_, li_ = jnp.zeros((1, LANES), f32), jnp.zeros((1, LANES), f32)
                    else:
                        lr_, li_ = hr_ref[j, c * L - 1:c * L, :], hi_ref[j, c * L - 1:c * L, :]
                    hpr = jnp.where(first, lr_, pltpu.roll(hr_ref[j, rows, :], 1, 0))
                    hpi = jnp.where(first, li_, pltpu.roll(hi_ref[j, rows, :], 1, 0))
                    gr, gi = or_ref[j, rows, :], oi_ref[j, rows, :]
                    dar += jnp.sum(hpr * gr + hpi * gi, axis=0, keepdims=True)
                    dai += jnp.sum(hpr * gi - hpi * gr, axis=0, keepdims=True)
                dar_ref[j] = dar
                dai_ref[j] = dai

    slab = jax.ShapeDtypeStruct((B, NB, S, LANES), f32)
    sspec = pl.BlockSpec((None, nb, S, LANES), lambda b, j: (b, j, 0, 0))
    aspec = pl.BlockSpec((nb, 1, LANES), lambda b, j: (j, 0, 0))
    in_specs = [sspec, sspec, aspec, aspec]
    out_specs = [sspec, sspec]
    out_shape = [slab, slab]
    args = [xr, xi, ar, ai]
    if with_da:
        in_specs += [sspec, sspec]
        args += [hr, hi]
        dspec = pl.BlockSpec((None, nb, 1, LANES), lambda b, j: (b, j, 0, 0))
        out_specs += [dspec, dspec]
        out_shape += [jax.ShapeDtypeStruct((B, NB, 1, LANES), f32)] * 2
    return _call(body, name, (B, NB // nb), in_specs, out_specs, out_shape, args,
                 scratch=[pltpu.VMEM((nb, L, LANES), f32), pltpu.VMEM((nb, L, LANES), f32)],
                 sem=("parallel", "parallel"), phase=phase)


def _s5_out_parts(hr_ref, hi_ref, u_ref, cr_ref, ci_ref, d_ref, wg_ref, bg_ref, NB):
    hcr = _slab_cat(hr_ref, NB).astype(bf16)
    hci = _slab_cat(hi_ref, NB).astype(bf16)
    u = u_ref[...].astype(f32)
    y2 = _dot(hcr, cr_ref[...]) - _dot(hci, ci_ref[...]) + d_ref[...] * u
    yg = _gelu(y2)
    s = jax.nn.sigmoid(_dot(yg.astype(bf16), wg_ref[...]) + bg_ref[...])
    return hcr, hci, u, y2, yg, s


def _s5_out_specs(B, S, NB, NS, SW, tm):
    sspec = pl.BlockSpec((None, NB, tm, LANES), lambda b, i: (b, 0, i, 0))
    full = lambda r, c: pl.BlockSpec((r, c), lambda b, i: (0, 0))
    return sspec, [sspec, sspec, pl.BlockSpec((None, tm, SW), lambda b, i: (b, i, 0)), full(NS, SW), full(NS, SW),
                   full(1, SW), full(SW, SW), full(1, SW)]


def _s5_out(hr, hi, p3, cbr, cbi, dsk, wglu, bglu, tm):
    B, NB, S, _ = hr.shape
    NS, SW = cbr.shape

    def body(hr_ref, hi_ref, u_ref, cr_ref, ci_ref, d_ref, wg_ref, bg_ref, o_ref):
        _, _, _, _, yg, s = _s5_out_parts(hr_ref, hi_ref, u_ref, cr_ref, ci_ref, d_ref, wg_ref, bg_ref, NB)
        o_ref[...] = (yg * s).astype(bf16)

    _, in_specs = _s5_out_specs(B, S, NB, NS, SW, tm)
    return pl.pallas_call(
        body, name="s5_out", grid=(B, S // tm), in_specs=in_specs,
        out_specs=pl.BlockSpec((None, tm, SW), lambda b, i: (b, i, 0)),
        out_shape=jax.ShapeDtypeStruct((B, S, SW), bf16),
        compiler_params=_params(("parallel", "parallel")))(hr, hi, p3, cbr, cbi, dsk, wglu, bglu)


def _s5_out_bwd(hr, hi, p3, dmix3, cbr, cbi, dsk, wglu, bglu, tm):
    B, NB, S, _ = hr.shape
    NS, SW = cbr.shape

    def body(hr_ref, hi_ref, u_ref, cr_ref, ci_ref, d_ref, wg_ref, bg_ref, da_ref,
             dhr_ref, dhi_ref, du_ref, dcr_ref, dci_ref, dd_ref, dwg_ref, dbg_ref):
        hcr, hci, u, y2, yg, s = _s5_out_parts(hr_ref, hi_ref, u_ref, cr_ref, ci_ref, d_ref, wg_ref, bg_ref, NB)
        da = da_ref[...].astype(f32)
        dz = da * yg * s * (1.0 - s)
        dzb = dz.astype(bf16)
        dyg = da * s + _dg(dzb, wg_ref[...], NT)
        dy2 = dyg * _gelu_grad(y2)
        dyb = dy2.astype(bf16)

        @pl.when((pl.program_id(0) == 0) & (pl.program_id(1) == 0))
        def _():
            for r in (dcr_ref, dci_ref, dd_ref, dwg_ref, dbg_ref):
                r[...] = jnp.zeros_like(r)
        dwg_ref[...] += _dg(yg.astype(bf16), dzb, TN)
        dbg_ref[...] += jnp.sum(dz, axis=0, keepdims=True)
        dd_ref[...] += jnp.sum(dy2 * u, axis=0, keepdims=True)
        dcr_ref[...] += _dg(hcr, dyb, TN)
        dci_ref[...] -= _dg(hci, dyb, TN)
        du_ref[...] = dy2 * d_ref[...]
        dhr = _dg(dyb, cr_ref[...], NT)
        dhi = _dg(dyb, ci_ref[...], NT)
        for j in range(NB):
            dhr_ref[j] = dhr[:, j * LANES:(j + 1) * LANES]
            dhi_ref[j] = -dhi[:, j * LANES:(j + 1) * LANES]

    sspec, in_specs = _s5_out_specs(B, S, NB, NS, SW, tm)
    in_specs = in_specs + [pl.BlockSpec((None, tm, SW), lambda b, i: (b, i, 0))]
    full = lambda r, c: pl.BlockSpec((r, c), lambda b, i: (0, 0))
    slab = jax.ShapeDtypeStruct((B, NB, S, LANES), f32)
    mat = lambda r, c: jax.ShapeDtypeStruct((r, c), f32)
    return pl.pallas_call(
        body, name="s5_out_bwd", grid=(B, S // tm), in_specs=in_specs,
        out_specs=[sspec, sspec, pl.BlockSpec((None, tm, SW), lambda b, i: (b, i, 0)), full(NS, SW), full(NS, SW),
                   full(1, SW), full(SW, SW), full(1, SW)],
        out_shape=[slab, slab, jax.ShapeDtypeStruct((B, S, SW), f32), mat(NS, SW), mat(NS, SW), mat(1, SW),
                   mat(SW, SW), mat(1, SW)],
        compiler_params=_params(("arbitrary", "arbitrary")))(hr, hi, p3, cbr, cbi, dsk, wglu, bglu, dmix3)


def _s5_in_bwd(gr, gi, p3, bbd, du_skip, duv3, tm):
    B, NB, S, _ = gr.shape
    SW, NS2 = bbd.shape
    PW = SW + duv3.shape[2]

    def body(gr_ref, gi_ref, u_ref, b_ref, ds_ref, duv_ref, dp_ref, db_ref):
        g = jnp.concatenate([_slab_cat(gr_ref, NB), _slab_cat(gi_ref, NB)], axis=1).astype(bf16)
        du = _dg(g, b_ref[...], NT) + ds_ref[...]
        dp_ref[:, :SW] = du.astype(bf16)
        dp_ref[:, SW:] = duv_ref[...]

        @pl.when((pl.program_id(0) == 0) & (pl.program_id(1) == 0))
        def _():
            db_ref[...] = jnp.zeros_like(db_ref)
        db_ref[...] += _dg(u_ref[...], g, TN)

    sspec = pl.BlockSpec((None, NB, tm, LANES), lambda b, i: (b, 0, i, 0))
    row = lambda c: pl.BlockSpec((None, tm, c), lambda b, i: (b, i, 0))
    return pl.pallas_call(
        body, name="s5_in_bwd", grid=(B, S // tm),
        in_specs=[sspec, sspec, row(SW), pl.BlockSpec((SW, NS2), lambda b, i: (0, 0)), row(SW), row(PW - SW)],
        out_specs=[row(PW), pl.BlockSpec((SW, NS2), lambda b, i: (0, 0))],
        out_shape=[jax.ShapeDtypeStruct((B, S, PW), bf16), jax.ShapeDtypeStruct((SW, NS2), f32)],
        compiler_params=_params(("arbitrary", "arbitrary")))(gr, gi, p3, bbd, du_skip, duv3)


BIG = ['ev_w_in', 'ev_w_out', 'od_w_in', 'od_w_out', 'ffn_w_up', 'ffn_w_down']
ANY = pl.BlockSpec(memory_space=pl.ANY)


def _rtile(rows, mult):
    best = None
    for d in range(mult, min(rows, 512) + 1, mult):
        if rows % d == 0:
            best = d
    assert best is not None, (rows, mult)
    return best


def _pair_sum(name, g, recv, c_idx, out_dtype):
    NCH, R, W = g.shape
    HALF_W = W // 2
    tr = _rtile(R, 16)

    def body(c_ref, a_ref, b_ref, o_ref):
        o_ref[...] = (a_ref[...] + b_ref[...]).astype(out_dtype)

    return pl.pallas_call(
        body, name=name,
        grid_spec=pltpu.PrefetchScalarGridSpec(
            num_scalar_prefetch=1, grid=(NCH, R // tr),
            in_specs=[pl.BlockSpec((None, tr, HALF_W), lambda j, i, c: (j, i, c[0])),
                      pl.BlockSpec((None, tr, HALF_W), lambda j, i, c: (j, i, 0))],
            out_specs=pl.BlockSpec((None, tr, HALF_W), lambda j, i, c: (j, i, 0))),
        out_shape=jax.ShapeDtypeStruct((NCH, R, HALF_W), out_dtype),
        compiler_params=_params(("parallel", "parallel")))(c_idx, g, recv)


def _chip_sum(name, r3, h, k_idx):
    NCH, R, Wh = r3.shape
    tr = _rtile(R, 16)

    def body(k_ref, a_ref, own_ref, o_ref):
        own = own_ref[...].astype(f32)
        t = [jnp.where(k_ref[0] == s, own, a_ref[s].astype(f32)) for s in range(NCH)]
        o_ref[...] = ((t[0] + t[1]) + t[2]) + t[3]

    return pl.pallas_call(
        body, name=name,
        grid_spec=pltpu.PrefetchScalarGridSpec(
            num_scalar_prefetch=1, grid=(R // tr,),
            in_specs=[pl.BlockSpec((NCH, tr, Wh), lambda i, k: (0, i, 0)),
                      pl.BlockSpec((None, tr, Wh), lambda i, k: (k[0], i, 0))],
            out_specs=pl.BlockSpec((tr, Wh), lambda i, k: (i, 0))),
        out_shape=jax.ShapeDtypeStruct((R, Wh), f32),
        compiler_params=_params(("parallel",)))(k_idx, r3, h)


def _adam_math(gg, w, m, v):
    nm = ADAM_B1 * m + (1.0 - ADAM_B1) * gg
    nv = ADAM_B2 * v + (1.0 - ADAM_B2) * jnp.square(gg)
    m_hat = nm / (1.0 - ADAM_B1 ** ADAM_STEP)
    v_hat = nv / (1.0 - ADAM_B2 ** ADAM_STEP)
    return -ADAM_LR * (m_hat / (jnp.sqrt(v_hat) + ADAM_EPS) + ADAM_WD * w), nm, nv


def _adamw(name, mine, theirs, c_idx, w, m, v, lead, transposed, prev=None):
    L, R, W = w.shape
    if transposed:
        bw = LANES if W % LANES == 0 else W
        gspec = pl.BlockSpec((bw, R // 2), lambda i, hf, c: (i, 0))
        wspec = pl.BlockSpec((None, R // 2, bw), lambda i, hf, c: (lead, hf, i))
        grid = (W // bw, 2)
    else:
        tr = _rtile(R, SUBLANES)
        gspec = pl.BlockSpec((tr, W // 2), lambda i, hf, c: (i, 0))
        wspec = pl.BlockSpec((None, tr, W // 2), lambda i, hf, c: (lead, i, hf))
        grid = (R // tr, 2)

    def body(c_ref, a_ref, b_ref, w_ref, m_ref, v_ref, *rest):
        go_ref, d_ref, nm_ref, nv_ref = rest[-4:]
        gg = jnp.where(pl.program_id(1) == c_ref[0], a_ref[...], b_ref[...])
        if transposed:
            gg = gg.T
        d, nm, nv = _adam_math(gg, w_ref[...], m_ref[...], v_ref[...])
        go_ref[...] = gg
        d_ref[...] = d
        nm_ref[...] = nm
        nv_ref[...] = nv

    in_specs = [gspec, gspec, wspec, wspec, wspec]
    args, aliases = [c_idx, mine, theirs, w, m, v], {}
    if prev is not None:
        in_specs += [ANY] * 4
        args += list(prev)
        aliases = {6: 0, 7: 1, 8: 2, 9: 3}
    s = jax.ShapeDtypeStruct((L, R, W), f32)
    return pl.pallas_call(
        body, name=name,
        grid_spec=pltpu.PrefetchScalarGridSpec(num_scalar_prefetch=1, grid=grid, in_specs=in_specs,
                                               out_specs=[wspec] * 4),
        out_shape=[s, s, s, s], input_output_aliases=aliases,
        compiler_params=_params(("parallel", "arbitrary")))(*args)


def _place():
    x, y, c = lax.axis_index("x"), lax.axis_index("y"), lax.axis_index("c")
    return x, y, c, [(1 - x, y), (x, 1 - y), (1 - x, 1 - y)]


def _gathered_shape(sh, kind):
    if kind == "rows":
        return sh[:-2] + (N_CHIPS * sh[-2], sh[-1])
    if kind == "cols":
        return sh[:-1] + (N_CHIPS * sh[-1],)
    return (N_CHIPS,) + sh


def _place_shard(name, shard, kind, k_idx):
    sh = shard.shape
    r, C = sh[-2], sh[-1]
    L = sh[0] if len(sh) == 3 else 1
    tr = _rtile(r, 16)
    nr = r // tr
    if kind == "rows":
        out3, omap = (L, N_CHIPS * r, C), lambda l, i, k: (l, k[0] * nr + i, 0)
    elif kind == "cols":
        out3, omap = (L, r, N_CHIPS * C), lambda l, i, k: (l, i, k[0])
    else:
        out3, omap = (N_CHIPS, r, C), lambda l, i, k: (k[0], i, 0)

    def body(k_ref, s_ref, o_ref):
        o_ref[...] = s_ref[...]

    out = pl.pallas_call(
        body, name=name,
        grid_spec=pltpu.PrefetchScalarGridSpec(
            num_scalar_prefetch=1, grid=(L, nr),
            in_specs=[pl.BlockSpec((None, tr, C), lambda l, i, k: (l, i, 0))],
            out_specs=pl.BlockSpec((None, tr, C), omap)),
        out_shape=jax.ShapeDtypeStruct(out3, shard.dtype),
        compiler_params=_params(("parallel", "parallel")))(k_idx, shard.reshape(L, r, C))
    return out.reshape(_gathered_shape(sh, kind))


def _gather_phase(shards, fulls, kinds):
    n = len(shards)
    shapes = [s.shape for s in shards]

    def window(ref, a, k, h=None):
        sh, kind = shapes[a], kinds[a]
        r = sh[-2]
        start, size = (0, r) if h is None else (h * (r // 2), r // 2)
        lead = (slice(None),) * (len(sh) - 2)
        if kind == "rows":
            return ref.at[lead + (pl.ds(k * r + start, size), slice(None))]
        if kind == "cols":
            return ref.at[lead + (pl.ds(start, size), pl.ds(pl.multiple_of(k * sh[-1], LANES), sh[-1]))]
        return ref.at[(k,) + lead + (pl.ds(start, size), slice(None))]

    def copies(s_refs, o_refs, sems):
        send_sems, recv_sems = sems
        x, y, c, chips = _place()
        k = 2 * x + y

        def copy(a, j, kk, hh, to, src=None):
            dst = window(o_refs[a], a, kk, hh)
            return pltpu.make_async_remote_copy(
                src_ref=dst if src is None else src, dst_ref=dst, send_sem=send_sems.at[6 * a + j],
                recv_sem=recv_sems.at[6 * a + j], device_id=to, device_id_type=MESH)

        first = []
        for a in range(n):
            r = shapes[a][-2]
            lead = (slice(None),) * (len(shapes[a]) - 2)
            src = s_refs[a].at[lead + (pl.ds(c * (r // 2), r // 2), slice(None))]
            first += [copy(a, j, k, c, (*chip, c), src=src) for j, chip in enumerate(chips)]
        return copy, first, (x, y, c), (x, y, 1 - c), c, chips

    def start(s_refs, o_refs, sems):
        for cp in copies(s_refs, o_refs, sems)[1]:
            cp.start()

    def finish(s_refs, o_refs, sems):
        copy, first, me, sibling, c, chips = copies(s_refs, o_refs, sems)
        passed = []
        for j, (cx, cy) in enumerate(chips):
            for a in range(n):
                copy(a, j, 2 * cx + cy, c, me).wait_recv()
                fwd = copy(a, 3 + j, 2 * cx + cy, c, sibling)
                fwd.start()
                passed.append(fwd)
        for j, (cx, cy) in enumerate(chips):
            for a in range(n):
                copy(a, 3 + j, 2 * cx + cy, 1 - c, me).wait_recv()
        for cp in first + passed:
            cp.wait_send()

    return _Phase(shards, fulls, [jax.ShapeDtypeStruct(f.shape, f.dtype) for f in fulls],
                  [pltpu.SemaphoreType.DMA((6 * n,)), pltpu.SemaphoreType.DMA((6 * n,))], start, finish)


def _comm_pair_swap(tag, gs):
    n = len(gs)

    def body(*refs):
        g_refs, o_refs, send_sems, recv_sems = refs[:n], refs[n:2 * n], refs[2 * n], refs[2 * n + 1]
        x, y, c, _ = _place()
        half = [g.shape[2] // 2 for g in gs]
        cps = [pltpu.make_async_remote_copy(
            src_ref=g_refs[a].at[:, :, pl.ds(pl.multiple_of((1 - c) * half[a], LANES), half[a])], dst_ref=o_refs[a], send_sem=send_sems.at[a],
            recv_sem=recv_sems.at[a], device_id=(x, y, 1 - c), device_id_type=MESH) for a in range(n)]
        for cp in cps:
            cp.start()
        for cp in cps:
            cp.wait()

    return pl.pallas_call(
        body, name="comm_pair_swap_" + tag, in_specs=[ANY] * n, out_specs=[ANY] * n,
        out_shape=[jax.ShapeDtypeStruct(g.shape[:2] + (g.shape[2] // 2,), g.dtype) for g in gs],
        scratch_shapes=[pltpu.SemaphoreType.DMA((n,)), pltpu.SemaphoreType.DMA((n,))])(*gs)


def _exchange_phase(hs):
    n = len(hs)

    def copies(h_refs, o_refs, sems):
        send_sems, recv_sems = sems
        x, y, c, chips = _place()
        k = 2 * x + y

        def copy(a, j, src_slot, dst_slot):
            cx, cy = chips[j]
            return pltpu.make_async_remote_copy(
                src_ref=h_refs[a].at[src_slot], dst_ref=o_refs[a].at[dst_slot], send_sem=send_sems.at[3 * a + j],
                recv_sem=recv_sems.at[3 * a + j], device_id=(cx, cy, c), device_id_type=MESH)

        sends = [copy(a, j, 2 * cx + cy, k) for a in range(n) for j, (cx, cy) in enumerate(chips)]
        return copy, sends, k, chips

    def start(h_refs, o_refs, sems):
        for cp in copies(h_refs, o_refs, sems)[1]:
            cp.start()

    def finish(h_refs, o_refs, sems):
        copy, sends, k, chips = copies(h_refs, o_refs, sems)
        for a in range(n):
            for j, (cx, cy) in enumerate(chips):
                copy(a, j, k, 2 * cx + cy).wait_recv()
        for cp in sends:
            cp.wait_send()

    return _Phase(hs, [], [jax.ShapeDtypeStruct(h.shape, h.dtype) for h in hs],
                  [pltpu.SemaphoreType.DMA((3 * n,)), pltpu.SemaphoreType.DMA((3 * n,))], start, finish)


def _comm_pair_share(tag, gs):
    n = len(gs)

    def body(*refs):
        g_refs, o_refs, send_sems, recv_sems = refs[:n], refs[n:2 * n], refs[2 * n], refs[2 * n + 1]
        x, y, c, _ = _place()
        cps = [pltpu.make_async_remote_copy(
            src_ref=g_refs[a], dst_ref=o_refs[a], send_sem=send_sems.at[a], recv_sem=recv_sems.at[a],
            device_id=(x, y, 1 - c), device_id_type=MESH) for a in range(n)]
        for cp in cps:
            cp.start()
        for cp in cps:
            cp.wait()

    return pl.pallas_call(
        body, name="comm_pair_share_" + tag, in_specs=[ANY] * n, out_specs=[ANY] * n,
        out_shape=[jax.ShapeDtypeStruct(g.shape, g.dtype) for g in gs],
        scratch_shapes=[pltpu.SemaphoreType.DMA((n,)), pltpu.SemaphoreType.DMA((n,))])(*gs)


def _pad_rows(flat, unit):
    n = flat.shape[-1]
    pad = (-n) % unit
    if pad:
        flat = jnp.pad(flat, [(0, 0)] * (flat.ndim - 1) + [(0, pad)])
    return flat


def _split_chips(full, axis):
    sh = full.shape
    t = full.reshape(sh[:axis] + (N_CHIPS, sh[axis] // N_CHIPS) + sh[axis + 1:])
    return jnp.moveaxis(t, axis, 0).reshape(N_CHIPS, -1)


def _join_chips(stack, shard_shape, axis):
    t = jnp.moveaxis(stack.reshape((N_CHIPS,) + tuple(shard_shape)), 0, axis)
    sh = t.shape
    return t.reshape(sh[:axis] + (sh[axis] * sh[axis + 1],) + sh[axis + 2:])


def _block_diag(blocks):
    G, r, c = blocks.shape
    eye = jnp.eye(G, dtype=blocks.dtype)
    return (blocks[:, :, None, :] * eye[:, None, :, None]).reshape(G * r, G * c)


def _diag_blocks(m, G):
    r, c = m.shape[0] // G, m.shape[1] // G
    idx = jnp.arange(G)
    return m.reshape(G, r, G, c)[idx, :, idx, :]


def _weight_shards(w):
    conv = jnp.concatenate([w[n].reshape(-1) for n in GATHER_F32])
    conv = _pad_rows(conv, 2 * SUBLANES * LANES).reshape(-1, LANES)
    b16 = lambda a: a.astype(bf16)
    return {'ev_w_in': (b16(w['ev_w_in'][0]), "chip"), 'ev_w_out': (b16(w['ev_w_out'][0]), "rows"),
            's5_w_glu': (b16(w['s5_w_glu'][0]), "rows"), 'conv': (conv, "chip"),
            'od_w_in': (b16(w['od_w_in'][0]), "cols"), 'od_w_out': (b16(w['od_w_out'][0]), "rows"),
            'ffn_w_up0': (b16(w['ffn_w_up'][0]), "cols"), 'ffn_w_up1': (b16(w['ffn_w_up'][1]), "cols"),
            'ffn_w_down0': (b16(w['ffn_w_down'][0]), "rows"), 'ffn_w_down1': (b16(w['ffn_w_down'][1]), "rows")}


def kernel(x, mix_norm_g, ffn_norm_g, final_norm_g, ev_w_in, ev_w_out, s5_lam_re, s5_lam_im, s5_log_dt, s5_b_re, s5_b_im, s5_c_re, s5_c_im, s5_d, s5_w_glu, s5_b_glu, gm_w_s, gm_b_s, gm_v_g, od_w_in, od_conv_w, od_conv_b, od_w_out, ffn_w_up, ffn_conv_w, ffn_conv_b, ffn_w_down, loss_target, m_mix_norm_g, m_ffn_norm_g, m_final_norm_g, m_ev_w_in, m_ev_w_out, m_s5_lam_re, m_s5_lam_im, m_s5_log_dt, m_s5_b_re, m_s5_b_im, m_s5_c_re, m_s5_c_im, m_s5_d, m_s5_w_glu, m_s5_b_glu, m_gm_w_s, m_gm_b_s, m_gm_v_g, m_od_w_in, m_od_conv_w, m_od_conv_b, m_od_w_out, m_ffn_w_up, m_ffn_conv_w, m_ffn_conv_b, m_ffn_w_down, v_mix_norm_g, v_ffn_norm_g, v_final_norm_g, v_ev_w_in, v_ev_w_out, v_s5_lam_re, v_s5_lam_im, v_s5_log_dt, v_s5_b_re, v_s5_b_im, v_s5_c_re, v_s5_c_im, v_s5_d, v_s5_w_glu, v_s5_b_glu, v_gm_w_s, v_gm_b_s, v_gm_v_g, v_od_w_in, v_od_conv_w, v_od_conv_b, v_od_w_out, v_ffn_w_up, v_ffn_conv_w, v_ffn_conv_b, v_ffn_w_down):
    loc = dict(locals())
    w = {n: loc[n] for n in WEIGHTS}
    mom = {n: loc["m_" + n] for n in WEIGHTS}
    var = {n: loc["v_" + n] for n in WEIGHTS}

    B, S, D = x.shape
    T = B * S
    SW = s5_d.shape[1]
    G = SW // SSM_GROUP
    NS = G * SSM_STATE
    NB = NS // LANES
    tm = min(512, S)
    tt = min(1024, T)
    c_idx = lax.axis_index("c").astype(jnp.int32).reshape(1)
    k_idx = (2 * lax.axis_index("x") + lax.axis_index("y")).astype(jnp.int32).reshape(1)
    shards = _weight_shards(w)
    placed = {n: _place_shard("place_" + n, s, kd, k_idx) for n, (s, kd) in shards.items()}

    def gather(names):
        return _gather_phase([shards[n][0] for n in names], [placed[n] for n in names], [shards[n][1] for n in names])

    w_ev_in, w_ev_out, w_glu, conv = _run_phase("comm_gather_mixer0", gather(['ev_w_in', 'ev_w_out', 's5_w_glu', 'conv']))
    w_ev_in = jnp.swapaxes(w_ev_in, 0, 1).reshape(D, -1)
    full, off = {}, 0
    for n in GATHER_F32:
        full[n] = _join_chips(conv.reshape(N_CHIPS, -1)[:, off:off + w[n].size], w[n].shape, SHARD_AXIS[n])
        off += w[n].size

    h0 = x.reshape(T, D)
    y0, p0 = _norm_mm("ev_in", h0, mix_norm_g[0], w_ev_in, tm)
    PW = p0.shape[1]
    p03 = p0.reshape(B, S, PW)
    lr, li, ldt = s5_lam_re[0], s5_lam_im[0], s5_log_dt[0].reshape(G, 1)
    ar, ai, zr, zi = _s5_prep(lr, li, ldt)
    bre = _block_diag(jnp.swapaxes(s5_b_re[0], 1, 2))
    bim = _block_diag(jnp.swapaxes(s5_b_im[0], 1, 2))
    cbr = _block_diag(jnp.swapaxes(s5_c_re[0], 1, 2)).astype(bf16)
    cbi = _block_diag(jnp.swapaxes(s5_c_im[0], 1, 2)).astype(bf16)
    zr_row, zi_row = zr.reshape(1, NS), zi.reshape(1, NS)
    bbd = _s5_bbd(zr_row, zi_row, bre, bim)
    ar_s, ai_s = ar.reshape(NB, 1, LANES), ai.reshape(NB, 1, LANES)
    xr, xi = _s5_in(p03, bbd, SW, tm)
    (hr, hi), (w_up0, w_down0) = _s5_scan("s5_scan", xr, xi, ar_s, ai_s, False,
                                           phase=gather(['ffn_w_up0', 'ffn_w_down0']))
    dsk, bglu = s5_d.reshape(1, SW), s5_b_glu.reshape(1, SW)
    a_out = _s5_out(hr, hi, p03, cbr, cbi, dsk, w_glu, bglu, tm)
    ws, bst, gv = gm_w_s[0], gm_b_s[0].T, gm_v_g.reshape(1, -1)
    mixcat = _gmlp(p0, a_out.reshape(T, SW), ws, bst, gv, SW)
    h1 = _mm_resid("ev_out", mixcat, w_ev_out, h0, tm)

    def ffn_fwd(l, h, w_up, w_down, up_phase=None, down_phase=None):
        res = _norm_mm(f"ffn_up{l}", h, ffn_norm_g[l], w_up, tm, phase=up_phase)
        (z, up), got_up = res if up_phase is not None else (res, None)
        res = _ffn_down(f"ffn_down{l}", up, full['ffn_conv_w'][l], ffn_conv_b[l].reshape(1, -1), w_down, h, S, tm,
                        phase=down_phase)
        (hn,), got_down = res if down_phase is not None else (res, None)
        return hn, (z, up.reshape(B, S, -1)), got_up, got_down

    h2, ffn0, (w_up1, w_down1), (w_od_in, w_od_out) = ffn_fwd(
        0, h1, w_up0, w_down0, gather(['ffn_w_up1', 'ffn_w_down1']), gather(['od_w_in', 'od_w_out']))
    w_ups, w_downs = (w_up0, w_up1), (w_down0, w_down1)
    od_cw, od_cb = full['od_conv_w'][0], full['od_conv_b']
    y1, p1 = _norm_mm("od_in", h2, mix_norm_g[1], w_od_in, tm)
    p13 = p1.reshape(B, S, -1)
    sc = _od_act(p13, od_cw, od_cb)
    h3 = _mm_resid("od_out", sc.reshape(T, D), w_od_out, h2, tm)
    h4, ffn1, _, _ = ffn_fwd(1, h3, w_up1, w_down1)

    dh4, dh4b, loss_part, d_final_g = _final_loss(h4, final_norm_g, loss_target.reshape(T, D), tm)
    loss = lax.psum(loss_part[0, 0], ("x", "y", "c"))

    grads = {}

    halves = {}

    def reduce_begin(tag, names, parts):
        recv = _comm_pair_swap(tag, parts)
        return [_pair_sum(f"pair_sum_{n}", g, r, c_idx, f32 if n == "small" else bf16)
                for n, g, r in zip(names, parts, recv)]

    def reduce_end(tag, names, hsum, r3):
        mine = [_chip_sum(f"chip_sum_{n}", r, h, k_idx) for n, r, h in zip(names, r3, hsum)]
        theirs = _comm_pair_share(tag, mine)
        halves.update({n: (a, b) for n, a, b in zip(names, mine, theirs)})

    def ffn_bwd(l, dh, dhb, h_in, saved, phase=None):
        z, up3 = saved
        w_down, w_up = w_downs[l], w_ups[l]
        res = _ffn_act_bwd(f"ffn_act_bwd{l}", up3, dhb.reshape(B, S, D), w_down, full['ffn_conv_w'][l],
                           ffn_conv_b[l].reshape(1, -1), phase=phase)
        (act, dg3, dv3, dcwg, dcwv, dcbg, dcbv), got = res if phase is not None else (res, None)
        g_down = _mm_tn(f"ffn_down_dw{l}", act.reshape(T, -1), dhb, tt)
        dupg, dupv = dg3.reshape(T, -1), dv3.reshape(T, -1)
        F = dupg.shape[1]
        g_up = _mm_tn(f"ffn_up_dw{l}_gate", dupg, z, tt, rows=2 * F)
        g_up = _mm_tn(f"ffn_up_dw{l}_val", dupv, z, tt, rows=2 * F, row_off=F, prev=g_up)
        dh_new, dhb_new, dg = _mm_nt_normbwd(f"ffn_up_bwd{l}", [dupg, dupv], w_up, h_in, ffn_norm_g[l], dh, tm)
        dcw = jnp.concatenate([dcwg, dcwv], axis=1)
        dcb = jnp.concatenate([dcbg, dcbv], axis=1)
        return dh_new, dhb_new, g_down, g_up, dcw, dcb[0], dg[0], got

    chips = lambda g: g.reshape(N_CHIPS, -1, D)
    dh3, dh3b, gd1, gu1, gcw1, gcb1, gng1, _ = ffn_bwd(1, dh4, dh4b, h3, ffn1)
    dsc = _mm_nt("od_out_bwd", dh3b, w_od_out, tm)
    g_od_out = _mm_tn("od_out_dw", sc.reshape(T, D), dh3b, tt)
    dbg3, dcg3, dhx3, d_od_cw, d_od_cb = _od_act_bwd(p13, dsc.reshape(B, S, D), od_cw, od_cb)
    dp1 = [t.reshape(T, D) for t in (dbg3, dcg3, dhx3)]
    g_od_in = None
    for i, piece in enumerate(dp1):
        g_od_in = _mm_tn(f"od_in_dw{i}", piece, y1, tt, rows=3 * D, row_off=i * D, prev=g_od_in)
    grads['od_conv_w'] = d_od_cw[None]
    grads['od_conv_b'] = d_od_cb
    dh2, dh2b, gmix1 = _mm_nt_normbwd("od_in_bwd", dp1, w_od_in, h2, mix_norm_g[1], dh3, tm)
    layer1 = ['ffn_w_down1', 'ffn_w_up1', 'od_w_out', 'od_w_in']
    hsum1 = reduce_begin("layer1", layer1, [chips(g) for g in (gd1, gu1, g_od_out, g_od_in)])
    dh1, dh1b, gd0, gu0, gcw0, gcb0, gng0, r3 = ffn_bwd(0, dh2, dh2b, h1, ffn0, phase=_exchange_phase(hsum1))
    reduce_end("layer1", layer1, hsum1, r3)
    ffn0_names = ['ffn_w_down0', 'ffn_w_up0']
    hsum0 = reduce_begin("ffn0", ffn0_names, [chips(gd0), chips(gu0)])
    grads['ffn_conv_w'] = jnp.stack([gcw0, gcw1])
    grads['ffn_conv_b'] = jnp.stack([gcb0, gcb1])
    grads['ffn_norm_g'] = jnp.stack([gng0, gng1])
    grads['final_norm_g'] = d_final_g[0]

    dmix = _mm_nt("ev_out_bwd", dh1b, w_ev_out, tm)
    g_ev_out = _mm_tn("ev_out_dw", mixcat, dh1b, tt)
    duv, d_ws, d_bs, d_gv = _gmlp_bwd(p0, dmix, ws, bst, gv, SW)
    grads['gm_w_s'] = d_ws[None]
    grads['gm_b_s'] = d_bs[:, :, 0][None]
    grads['gm_v_g'] = d_gv
    dhr, dhi, du_skip, d_cbr, d_cbi, d_dsk, d_wglu, d_bglu = _s5_out_bwd(
        hr, hi, p03, dmix.reshape(B, S, D), cbr, cbi, dsk, w_glu, bglu, tm)
    grads['s5_c_re'] = jnp.swapaxes(_diag_blocks(d_cbr, G), 1, 2)[None]
    grads['s5_c_im'] = jnp.swapaxes(_diag_blocks(d_cbi, G), 1, 2)[None]
    grads['s5_d'] = d_dsk
    grads['s5_w_glu'] = d_wglu[None]
    grads['s5_b_glu'] = d_bglu
    (gr, gi, dar, dai), r3 = _s5_scan("s5_rscan", dhr, dhi, ar_s, ai_s, True, hr, hi, phase=_exchange_phase(hsum0))
    reduce_end("ffn0", ffn0_names, hsum0, r3)
    dp03, d_bbd = _s5_in_bwd(gr, gi, p03, bbd, du_skip, duv.reshape(B, S, -1), tm)
    d_bre, d_bim, d_zr, d_zi = _s5_bbd_bwd(d_bbd, zr_row, zi_row, bre, bim)
    grads['s5_b_re'] = jnp.swapaxes(_diag_blocks(d_bre, G), 1, 2)[None]
    grads['s5_b_im'] = jnp.swapaxes(_diag_blocks(d_bim, G), 1, 2)[None]
    shp = (-1, G, SSM_STATE)
    d_lr, d_li, d_ldt = _s5_prep_bwd(lr, li, ldt, dar.reshape(shp), dai.reshape(shp), d_zr.reshape(shp),
                                     d_zi.reshape(shp))
    grads['s5_lam_re'] = d_lr[None]
    grads['s5_lam_im'] = d_li[None]
    grads['s5_log_dt'] = d_ldt.reshape(1, G)
    dp0 = dp03.reshape(T, PW)
    g_ev_in = _mm_tn("ev_in_dw", dp0, y0, tt)
    grad_x, _, gmix0 = _mm_nt_normbwd("ev_in_bwd", [dp0], w_ev_in, h0, mix_norm_g[0], dh1, tm)
    grads['mix_norm_g'] = jnp.concatenate([gmix0, gmix1], axis=0)

    small = [n for n in WEIGHTS if n not in BIG]
    segs = []
    for n in small:
        gfull = grads[n].astype(f32)
        if n in SHARD_AXIS:
            segs.append(_split_chips(gfull, SHARD_AXIS[n]))
        else:
            segs.append(jnp.broadcast_to(gfull.reshape(1, -1), (N_CHIPS, gfull.size)))
    unit = 2 * SUBLANES * D
    gsmall = _pad_rows(jnp.concatenate(segs, axis=1), unit).reshape(N_CHIPS, -1, D)
    mixer0 = ['ev_w_out', 'ev_w_in', 'small']
    hsum = reduce_begin("mixer0", mixer0, [chips(g_ev_out), chips(g_ev_in), gsmall])
    reduce_end("mixer0", mixer0, hsum, _run_phase("comm_exchange_mixer0", _exchange_phase(hsum)))

    out_g, out_d, out_m, out_v = {}, {}, {}, {}

    def update(n, key, lead, transposed, prev=None):
        res = _adamw(f"adamw_{key}", *halves[key], c_idx, w[n], mom[n], var[n], lead, transposed, prev)
        out_g[n], out_d[n], out_m[n], out_v[n] = res
        return res

    update('ev_w_in', 'ev_w_in', 0, True)
    update('ev_w_out', 'ev_w_out', 0, False)
    update('od_w_in', 'od_w_in', 0, True)
    update('od_w_out', 'od_w_out', 0, False)
    update('ffn_w_up', 'ffn_w_up0', 0, True, prev=update('ffn_w_up', 'ffn_w_up1', 1, True))
    update('ffn_w_down', 'ffn_w_down0', 0, False, prev=update('ffn_w_down', 'ffn_w_down1', 1, False))

    def pack_local(d):
        flat = _pad_rows(jnp.concatenate([d[n].astype(f32).reshape(-1) for n in small]), unit)
        return flat.reshape(1, -1, D)

    res = _adamw("adamw_small", *halves['small'], c_idx, pack_local(w), pack_local(mom), pack_local(var), 0, False)
    for dst, p in zip((out_g, out_d, out_m, out_v), res):
        flat, off = p.reshape(-1), 0
        for n in small:
            dst[n] = flat[off:off + w[n].size].reshape(w[n].shape)
            off += w[n].size

    return (loss, grad_x.reshape(B, S, D), *[out_g[n] for n in WEIGHTS], *[out_d[n] for n in WEIGHTS],
            *[out_m[n] for n in WEIGHTS], *[out_v[n] for n in WEIGHTS])
```

```python
import functools
import math

import jax
import jax.numpy as jnp
from jax import lax
from jax.experimental import pallas as pl
from jax.experimental.pallas import tpu as pltpu

f32 = jnp.float32
bf16 = jnp.bfloat16
MESH = pl.DeviceIdType.MESH

SSM_GROUP = 16
SSM_STATE = 64
GMLP_HEAD = 128
CHUNK = 128
EPS = 1e-6
LAMBDA_RE_MAX = -1e-4
ADAM_LR, ADAM_B1, ADAM_B2, ADAM_EPS, ADAM_WD, ADAM_STEP = 0.001, 0.9, 0.999, 1e-08, 0.01, 10

LANES = 128
SUBLANES = 8
NSUB = 32
HALO = 16
VMEM_LIMIT = 56 * 1024 * 1024
N_CHIPS = 4

WEIGHTS = ['mix_norm_g', 'ffn_norm_g', 'final_norm_g', 'ev_w_in', 'ev_w_out', 's5_lam_re', 's5_lam_im', 's5_log_dt',
           's5_b_re', 's5_b_im', 's5_c_re', 's5_c_im', 's5_d', 's5_w_glu', 's5_b_glu', 'gm_w_s', 'gm_b_s', 'gm_v_g',
           'od_w_in', 'od_conv_w', 'od_conv_b', 'od_w_out', 'ffn_w_up', 'ffn_conv_w', 'ffn_conv_b', 'ffn_w_down']
SHARD_AXIS = {'ev_w_in': 2, 'ev_w_out': 1, 's5_w_glu': 1, 'od_w_in': 2, 'od_conv_w': 2, 'od_conv_b': 1, 'od_w_out': 1,
              'ffn_w_up': 2, 'ffn_conv_w': 2, 'ffn_w_down': 1}
GATHER_BF16 = ['ev_w_in', 'ev_w_out', 's5_w_glu', 'od_w_in', 'od_w_out', 'ffn_w_up', 'ffn_w_down']
GATHER_F32 = ['od_conv_w', 'od_conv_b', 'ffn_conv_w']

_GELU_K0 = math.sqrt(2.0 / math.pi)
_GELU_K1 = 0.044715
NT = (((1,), (1,)), ((), ()))
TN = (((0,), (0,)), ((), ()))


def _pick(n, cap):
    if n <= cap:
        return n
    best = None
    for d in range(LANES, cap + 1, LANES):
        if n % d == 0:
            best = d
    assert best is not None, (n, cap)
    return best


def _params(sem=None):
    return pltpu.CompilerParams(dimension_semantics=sem, vmem_limit_bytes=VMEM_LIMIT)


class _Phase:
    def __init__(self, ins, inplace, outs, sems, start, finish):
        self.ins, self.inplace, self.outs, self.sems = list(ins), list(inplace), list(outs), list(sems)
        self.start, self.finish = start, finish


def _call(body, name, grid, in_specs, out_specs, out_shape, args, scratch=(), sem=None, phase=None):
    if phase is None:
        return pl.pallas_call(body, name=name, grid=grid, in_specs=in_specs, out_specs=out_specs, out_shape=out_shape,
                              scratch_shapes=list(scratch), compiler_params=_params(sem))(*args)
    any_spec = pl.BlockSpec(memory_space=pl.ANY)
    n_in, n_out, n_scr = len(args), len(out_shape), len(scratch)
    p_in = phase.ins + phase.inplace
    ci, co = len(p_in), len(phase.outs)

    def wrapped(*refs):
        ins, cins = refs[:n_in], refs[n_in:n_in + len(phase.ins)]
        b = n_in + ci
        outs, couts = refs[b:b + n_out], refs[b + n_out:b + n_out + co]
        d = b + n_out + co
        scr, csem = refs[d:d + n_scr], refs[d + n_scr:]
        ids = [pl.program_id(i) for i in range(len(grid))]
        first = functools.reduce(jnp.logical_and, [i == 0 for i in ids])
        last = functools.reduce(jnp.logical_and, [i == g - 1 for i, g in zip(ids, grid)])

        @pl.when(first)
        def _():
            phase.start(cins, couts, csem)
        body(*ins, *outs, *scr)

        @pl.when(last)
        def _():
            phase.finish(cins, couts, csem)

    res = pl.pallas_call(
        wrapped, name=name, grid=grid, in_specs=list(in_specs) + [any_spec] * ci,
        out_specs=list(out_specs) + [any_spec] * co, out_shape=list(out_shape) + phase.outs,
        scratch_shapes=list(scratch) + phase.sems,
        input_output_aliases={n_in + len(phase.ins) + i: n_out + i for i in range(len(phase.inplace))},
        compiler_params=_params(tuple("arbitrary" for _ in grid)))(*args, *p_in)
    return res[:n_out], res[n_out:]


def _run_phase(name, phase):
    any_spec = pl.BlockSpec(memory_space=pl.ANY)
    ni, ci, co = len(phase.ins), len(phase.ins) + len(phase.inplace), len(phase.outs)

    def body(*refs):
        cins, couts, csem = refs[:ni], refs[ci:ci + co], refs[ci + co:]
        phase.start(cins, couts, csem)
        phase.finish(cins, couts, csem)

    return pl.pallas_call(
        body, name=name, in_specs=[any_spec] * ci, out_specs=[any_spec] * co, out_shape=phase.outs,
        scratch_shapes=phase.sems, input_output_aliases={ni + i: i for i in range(len(phase.inplace))})(
            *phase.ins, *phase.inplace)


def _gelu(x):
    return 0.5 * x * (1.0 + jnp.tanh(_GELU_K0 * (x + _GELU_K1 * x * x * x)))


def _gelu_grad(x):
    t = jnp.tanh(_GELU_K0 * (x + _GELU_K1 * x * x * x))
    return 0.5 * (1.0 + t) + 0.5 * x * (1.0 - t * t) * _GELU_K0 * (1.0 + 3.0 * _GELU_K1 * x * x)


def _rms_stats(x):
    r = lax.rsqrt(jnp.mean(x * x, axis=-1, keepdims=True) + EPS)
    return x * r, r


def _rms_bwd(dy, xh, r, g):
    dxh = dy * g
    dx = r * (dxh - xh * jnp.mean(dxh * xh, axis=-1, keepdims=True))
    return dx, jnp.sum(dy * xh, axis=0, keepdims=True)


def _dot(a, b):
    return jnp.dot(a, b, preferred_element_type=f32)


def _dg(a, b, dims):
    return lax.dot_general(a, b, dims, preferred_element_type=f32)


def _row_fold(z):
    return z.reshape(z.shape[0] // SUBLANES, SUBLANES, z.shape[1]).sum(axis=0)


def _norm_mm(name, h, g, w, tm, phase=None):
    T, D = h.shape
    N = w.shape[1]
    nc = _pick(N, 512)

    def body(h_ref, g_ref, w_ref, y_ref, o_ref):
        xh, _ = _rms_stats(h_ref[...])
        y = (xh * g_ref[...]).astype(bf16)
        y_ref[...] = y
        for j in range(N // nc):
            o_ref[:, j * nc:(j + 1) * nc] = _dot(y, w_ref[:, j * nc:(j + 1) * nc]).astype(bf16)

    return _call(
        body, name, (T // tm,),
        [pl.BlockSpec((tm, D), lambda i: (i, 0)), pl.BlockSpec((1, D), lambda i: (0, 0)),
         pl.BlockSpec((D, N), lambda i: (0, 0))],
        [pl.BlockSpec((tm, D), lambda i: (i, 0)), pl.BlockSpec((tm, N), lambda i: (i, 0))],
        [jax.ShapeDtypeStruct((T, D), bf16), jax.ShapeDtypeStruct((T, N), bf16)],
        [h, g.reshape(1, D), w], sem=("parallel",), phase=phase)


def _mm_resid(name, a, w, resid, tm):
    T, K = a.shape
    N = w.shape[1]

    def body(a_ref, w_ref, r_ref, o_ref):
        o_ref[...] = r_ref[...] + _dot(a_ref[...], w_ref[...])

    return pl.pallas_call(
        body, name=name, grid=(T // tm,),
        in_specs=[pl.BlockSpec((tm, K), lambda i: (i, 0)), pl.BlockSpec((K, N), lambda i: (0, 0)),
                  pl.BlockSpec((tm, N), lambda i: (i, 0))],
        out_specs=pl.BlockSpec((tm, N), lambda i: (i, 0)),
        out_shape=jax.ShapeDtypeStruct((T, N), f32),
        compiler_params=_params(("parallel",)))(a, w, resid)


def _mm_nt(name, dy, w, tm):
    T, N = dy.shape
    K = w.shape[0]
    kc = _pick(K, 512)

    def body(d_ref, w_ref, o_ref):
        d = d_ref[...].astype(bf16)
        for j in range(K // kc):
            o_ref[:, j * kc:(j + 1) * kc] = _dg(d, w_ref[j * kc:(j + 1) * kc, :], NT).astype(bf16)

    return pl.pallas_call(
        body, name=name, grid=(T // tm,),
        in_specs=[pl.BlockSpec((tm, N), lambda i: (i, 0)), pl.BlockSpec((K, N), lambda i: (0, 0))],
        out_specs=pl.BlockSpec((tm, K), lambda i: (i, 0)),
        out_shape=jax.ShapeDtypeStruct((T, K), bf16),
        compiler_params=_params(("parallel",)))(dy, w)


def _mm_nt_normbwd(name, dys, w, h, g, dh_in, tm):
    n = len(dys)
    T = dys[0].shape[0]
    D = w.shape[0]
    widths = [d.shape[1] for d in dys]
    offs = [sum(widths[:i]) for i in range(n)]

    def body(*refs):
        d_refs = refs[:n]
        w_ref, h_ref, g_ref, dh_ref, o_ref, ob_ref, dg_ref = refs[n:]
        dz = _dg(d_refs[0][...], w_ref[:, :widths[0]], NT)
        for i in range(1, n):
            dz += _dg(d_refs[i][...], w_ref[:, offs[i]:offs[i] + widths[i]], NT)
        xh, r = _rms_stats(h_ref[...])
        dx, dg = _rms_bwd(dz, xh, r, g_ref[...])
        out = dh_ref[...] + dx
        o_ref[...] = out
        ob_ref[...] = out.astype(bf16)

        @pl.when(pl.program_id(0) == 0)
        def _():
            dg_ref[...] = jnp.zeros_like(dg_ref)
        dg_ref[...] += dg

    row = lambda c: pl.BlockSpec((tm, c), lambda i: (i, 0))
    return pl.pallas_call(
        body, name=name, grid=(T // tm,),
        in_specs=[row(c) for c in widths] + [pl.BlockSpec((D, sum(widths)), lambda i: (0, 0)), row(D),
                                             pl.BlockSpec((1, D), lambda i: (0, 0)), row(D)],
        out_specs=[row(D), row(D), pl.BlockSpec((1, D), lambda i: (0, 0))],
        out_shape=[jax.ShapeDtypeStruct((T, D), f32), jax.ShapeDtypeStruct((T, D), bf16),
                   jax.ShapeDtypeStruct((1, D), f32)],
        compiler_params=_params(("arbitrary",)))(*dys, w, h, g.reshape(1, D), dh_in)


def _mm_tn(name, a, b, tt, rows=None, row_off=0, prev=None):
    T, K = a.shape
    N = b.shape[1]
    rows = K if rows is None else rows
    tk = _pick(K, 1408)
    tn = _pick(N, 1024)
    assert row_off % tk == 0
    kb = row_off // tk

    def body(a_ref, b_ref, *rest):
        o_ref = rest[-1]

        @pl.when(pl.program_id(2) == 0)
        def _():
            o_ref[...] = jnp.zeros_like(o_ref)
        o_ref[...] += _dg(a_ref[...], b_ref[...], TN)

    in_specs = [pl.BlockSpec((tt, tk), lambda k, n, t: (t, k)), pl.BlockSpec((tt, tn), lambda k, n, t: (t, n))]
    args, aliases = [a, b], {}
    if prev is not None:
        in_specs.append(ANY)
        args.append(prev)
        aliases = {2: 0}
    return pl.pallas_call(
        body, name=name, grid=(K // tk, N // tn, T // tt), in_specs=in_specs,
        out_specs=pl.BlockSpec((tk, tn), lambda k, n, t: (k + kb, n)),
        out_shape=jax.ShapeDtypeStruct((rows, N), f32), input_output_aliases=aliases,
        compiler_params=_params(("parallel", "parallel", "arbitrary")))(*args)


def _final_loss(h, g, tgt, tm):
    T, D = h.shape

    def body(h_ref, g_ref, t_ref, dh_ref, dhb_ref, loss_ref, dg_ref):
        xh, r = _rms_stats(h_ref[...])
        gg = g_ref[...]
        diff = xh * gg - t_ref[...]
        dy = diff * (1.0 / D)
        dx, dg = _rms_bwd(dy, xh, r, gg)
        dh_ref[...] = dx
        dhb_ref[...] = dx.astype(bf16)

        @pl.when(pl.program_id(0) == 0)
        def _():
            dg_ref[...] = jnp.zeros_like(dg_ref)
            loss_ref[...] = jnp.zeros_like(loss_ref)
        dg_ref[...] += dg
        loss_ref[...] += (0.5 / D) * jnp.sum(jnp.sum(diff * diff, axis=1, keepdims=True), axis=0, keepdims=True)

    return pl.pallas_call(
        body, name="final_loss", grid=(T // tm,),
        in_specs=[pl.BlockSpec((tm, D), lambda i: (i, 0)), pl.BlockSpec((1, D), lambda i: (0, 0)),
                  pl.BlockSpec((tm, D), lambda i: (i, 0))],
        out_specs=[pl.BlockSpec((tm, D), lambda i: (i, 0)), pl.BlockSpec((tm, D), lambda i: (i, 0)),
                   pl.BlockSpec((1, 1), lambda i: (0, 0)), pl.BlockSpec((1, D), lambda i: (0, 0))],
        out_shape=[jax.ShapeDtypeStruct((T, D), f32), jax.ShapeDtypeStruct((T, D), bf16),
                   jax.ShapeDtypeStruct((1, 1), f32), jax.ShapeDtypeStruct((1, D), f32)],
        compiler_params=_params(("arbitrary",)))(h, g.reshape(1, D), tgt)


def _taps(load, r0, R):
    main = load(r0, R)
    hs = pl.multiple_of(jnp.maximum(r0 - HALO, 0), HALO)
    halo = load(hs, HALO) * (r0 > 0).astype(f32)
    ext = jnp.concatenate([halo, main], axis=0)
    xm1 = pltpu.roll(ext, 1, 0)[HALO:]
    xm2 = pltpu.roll(ext, 2, 0)[HALO:]
    return xm2, xm1, main


def _conv(w, b, taps):
    return b + w[0:1] * taps[0] + w[1:2] * taps[1] + w[2:3] * taps[2]


def _ref_load(ref):
    return lambda s, n: ref[pl.ds(s, n), :].astype(f32)


def _ffn_down(name, up, cw, cb, w_down, resid, S, tm, phase=None):
    T, F2 = up.shape
    F = F2 // 2
    D = w_down.shape[1]
    cwid = _pick(F, 256)
    per_seq = S // tm

    def body(u_ref, halo_ref, cw_ref, cb_ref, w_ref, r_ref, o_ref):
        keep = (pl.program_id(0) % per_seq > 0).astype(f32)

        def conv(off):
            cols = slice(off, off + cwid)
            main = u_ref[:, cols].astype(f32)
            ext = jnp.concatenate([halo_ref[:, cols].astype(f32) * keep, main], axis=0)
            taps = (pltpu.roll(ext, 2, 0)[HALO:], pltpu.roll(ext, 1, 0)[HALO:], main)
            return _conv(cw_ref[:, cols], cb_ref[:, cols], taps)

        acc = r_ref[...]
        for j in range(F // cwid):
            cg, cv = conv(j * cwid), conv(F + j * cwid)
            a = (cg * jax.nn.sigmoid(cg) * cv).astype(bf16)
            acc = acc + _dot(a, w_ref[j * cwid:(j + 1) * cwid, :])
        o_ref[...] = acc

    full = lambda r, c: pl.BlockSpec((r, c), lambda i: (0, 0))
    return _call(
        body, name, (T // tm,),
        [pl.BlockSpec((tm, F2), lambda i: (i, 0)),
         pl.BlockSpec((HALO, F2), lambda i: (jnp.maximum(i * (tm // HALO) - 1, 0), 0)),
         full(3, F2), full(1, F2), full(F, D), pl.BlockSpec((tm, D), lambda i: (i, 0))],
        [pl.BlockSpec((tm, D), lambda i: (i, 0))], [jax.ShapeDtypeStruct((T, D), f32)],
        [up, up, cw, cb, w_down, resid], sem=("parallel",), phase=phase)


def _rev_conv_rows(d, nxt, w):
    R = d.shape[0]
    ext = jnp.concatenate([d, nxt], axis=0)
    n = R + HALO
    xp1 = pltpu.roll(ext, n - 1, 0)[:R]
    xp2 = pltpu.roll(ext, n - 2, 0)[:R]
    return w[2:3] * d + w[1:2] * xp1 + w[0:1] * xp2


def _conv_grad_acc(acc, dc, taps):
    return (acc[0] + _row_fold(dc * taps[0]), acc[1] + _row_fold(dc * taps[1]), acc[2] + _row_fold(dc * taps[2]),
            acc[3] + _row_fold(dc))


def _conv_grad_out(dcw_ref, dcb_ref, acc):
    @pl.when(pl.program_id(1) == 0)
    def _():
        dcw_ref[...] = jnp.zeros_like(dcw_ref)
        dcb_ref[...] = jnp.zeros_like(dcb_ref)
    for k in range(3):
        dcw_ref[k:k + 1, :] += jnp.sum(acc[k], axis=0, keepdims=True)
    dcb_ref[...] += jnp.sum(acc[3], axis=0, keepdims=True)


def _ffn_act_bwd(name, up3, dh3, w_down_t, cw, cb, phase=None):
    B, S, F2 = up3.shape
    F = F2 // 2
    D = dh3.shape[2]
    cwid = _pick(F, 256)
    nF = F // cwid
    R = min(256, S)
    nR = S // R

    def body(g_ref, v_ref, dh_ref, wd_ref, wg_ref, wv_ref, bg_ref, bv_ref,
             act_ref, dg_ref, dv_ref, dcwg_ref, dcwv_ref, dcbg_ref, dcbv_ref, sum_scr):
        wg, wv, bg, bv = wg_ref[...], wv_ref[...], bg_ref[...], bv_ref[...]

        def step(i, carry):
            ng, nv, accg, accv = carry
            r0 = pl.multiple_of((nR - 1 - i) * R, R)
            tg = _taps(_ref_load(g_ref), r0, R)
            tv = _taps(_ref_load(v_ref), r0, R)
            cg = _conv(wg, bg, tg)
            cv = _conv(wv, bv, tv)
            da = _dot(dh_ref[pl.ds(r0, R), :], wd_ref[...])
            sg = jax.nn.sigmoid(cg)
            act_ref[pl.ds(r0, R), :] = (cg * sg * cv).astype(bf16)
            dgate = da * cv * (sg * (1.0 + cg * (1.0 - sg)))
            dval = da * (cg * sg)
            dg_ref[pl.ds(r0, R), :] = _rev_conv_rows(dgate, ng, wg).astype(bf16)
            dv_ref[pl.ds(r0, R), :] = _rev_conv_rows(dval, nv, wv).astype(bf16)
            return dgate[:HALO], dval[:HALO], _conv_grad_acc(accg, dgate, tg), _conv_grad_acc(accv, dval, tv)
        z = jnp.zeros((SUBLANES, cwid), f32)
        zh = jnp.zeros((HALO, cwid), f32)
        _, _, accg, accv = lax.fori_loop(0, nR, step, (zh, zh, (z, z, z, z), (z, z, z, z)))
        j = pl.program_id(1)
        for half_i, (acc, dcw_ref, dcb_ref) in enumerate(((accg, dcwg_ref, dcbg_ref), (accv, dcwv_ref, dcbv_ref))):
            @pl.when(pl.program_id(0) == 0)
            def _():
                sum_scr[half_i, j] = jnp.zeros((SUBLANES, cwid), f32)
            for k in range(4):
                sum_scr[half_i, j, k:k + 1, :] += jnp.sum(acc[k], axis=0, keepdims=True)
            dcw_ref[...] = sum_scr[half_i, j, 0:3, :]
            dcb_ref[...] = sum_scr[half_i, j, 3:4, :]

    blk = lambda off: pl.BlockSpec((None, S, cwid), lambda b, j: (b, 0, off + j))
    wblk = lambda off: pl.BlockSpec((3, cwid), lambda b, j: (0, off + j))
    bblk = lambda off: pl.BlockSpec((1, cwid), lambda b, j: (0, off + j))
    sums = lambda r: pl.BlockSpec((r, cwid), lambda b, j: (0, jnp.where(b == B - 1, j, nF)))
    half = jax.ShapeDtypeStruct((B, S, F), bf16)
    return _call(
        body, name, (B, nF),
        [blk(0), blk(nF), pl.BlockSpec((None, S, D), lambda b, j: (b, 0, 0)),
         pl.BlockSpec((D, cwid), lambda b, j: (0, j)), wblk(0), wblk(nF), bblk(0), bblk(nF)],
        [blk(0), blk(0), blk(0), sums(3), sums(3), sums(1), sums(1)],
        [half, half, half, jax.ShapeDtypeStruct((3, F + cwid), f32), jax.ShapeDtypeStruct((3, F + cwid), f32),
         jax.ShapeDtypeStruct((1, F + cwid), f32), jax.ShapeDtypeStruct((1, F + cwid), f32)],
        [up3, up3, dh3, w_down_t, cw, cw, cb, cb], scratch=[pltpu.VMEM((2, nF, SUBLANES, cwid), f32)],
        sem=("arbitrary", "arbitrary"), phase=phase)


def _od_act(p3, cw, cb):
    B, S, D3 = p3.shape
    D = D3 // 3
    cwid = _pick(D, 256)
    nD = D // cwid
    R = min(256, S)

    def body(bg_ref, cg_ref, hx_ref, w_ref, b_ref, o_ref):
        w, b = w_ref[...], b_ref[...]
        q = lambda s, n: cg_ref[pl.ds(s, n), :].astype(f32) * hx_ref[pl.ds(s, n), :].astype(f32)

        def chunk(r, c):
            r0 = pl.multiple_of(r * R, R)
            cq = _conv(w, b, _taps(q, r0, R))
            o_ref[pl.ds(r0, R), :] = (bg_ref[pl.ds(r0, R), :].astype(f32) * cq).astype(bf16)
            return c
        lax.fori_loop(0, S // R, chunk, 0)

    blk = lambda off: pl.BlockSpec((None, S, cwid), lambda b, j: (b, 0, off + j))
    return pl.pallas_call(
        body, name="od_act", grid=(B, nD),
        in_specs=[blk(0), blk(nD), blk(2 * nD), pl.BlockSpec((3, cwid), lambda b, j: (0, j)),
                  pl.BlockSpec((1, cwid), lambda b, j: (0, j))],
        out_specs=pl.BlockSpec((None, S, cwid), lambda b, j: (b, 0, j)),
        out_shape=jax.ShapeDtypeStruct((B, S, D), bf16),
        compiler_params=_params(("parallel", "parallel")))(p3, p3, p3, cw, cb)


def _od_act_bwd(p3, dsc3, cw, cb):
    B, S, D3 = p3.shape
    D = D3 // 3
    cwid = _pick(D, 256)
    nD = D // cwid
    R = min(256, S)
    nR = S // R

    def body(bg_ref, cg_ref, hx_ref, d_ref, w_ref, b_ref, dbg_ref, dcg_ref, dhx_ref, dcw_ref, dcb_ref):
        w, b = w_ref[...], b_ref[...]
        q = lambda s, n: cg_ref[pl.ds(s, n), :].astype(f32) * hx_ref[pl.ds(s, n), :].astype(f32)

        def step(i, carry):
            nxt, acc = carry
            r0 = pl.multiple_of((nR - 1 - i) * R, R)
            rows = pl.ds(r0, R)
            tq = _taps(q, r0, R)
            cq = _conv(w, b, tq)
            d = d_ref[rows, :].astype(f32)
            dbg_ref[rows, :] = (d * cq).astype(bf16)
            dcq = d * bg_ref[rows, :].astype(f32)
            dq = _rev_conv_rows(dcq, nxt, w)
            dcg_ref[rows, :] = (dq * hx_ref[rows, :].astype(f32)).astype(bf16)
            dhx_ref[rows, :] = (dq * cg_ref[rows, :].astype(f32)).astype(bf16)
            return dcq[:HALO], _conv_grad_acc(acc, dcq, tq)
        z = jnp.zeros((SUBLANES, cwid), f32)
        _, acc = lax.fori_loop(0, nR, step, (jnp.zeros((HALO, cwid), f32), (z, z, z, z)))
        _conv_grad_out(dcw_ref, dcb_ref, acc)

    blk = lambda off: pl.BlockSpec((None, S, cwid), lambda j, b: (b, 0, off + j))
    part = jax.ShapeDtypeStruct((B, S, D), bf16)
    return pl.pallas_call(
        body, name="od_act_bwd", grid=(nD, B),
        in_specs=[blk(0), blk(nD), blk(2 * nD), blk(0), pl.BlockSpec((3, cwid), lambda j, b: (0, j)),
                  pl.BlockSpec((1, cwid), lambda j, b: (0, j))],
        out_specs=[blk(0), blk(0), blk(0), pl.BlockSpec((3, cwid), lambda j, b: (0, j)),
                   pl.BlockSpec((1, cwid), lambda j, b: (0, j))],
        out_shape=[part, part, part, jax.ShapeDtypeStruct((3, D), f32), jax.ShapeDtypeStruct((1, D), f32)],
        compiler_params=_params(("parallel", "arbitrary")))(p3, p3, p3, dsc3, cw, cb)


def _gmlp_parts(p, gv, SW, GW):
    uv = p[:, SW:].astype(f32)
    ge = _gelu(uv)
    u, v = ge[:, :GW], ge[:, GW:]
    vh, r = _rms_stats(v)
    return uv, u, vh, r, vh * gv


def _tril():
    rows = lax.broadcasted_iota(jnp.int32, (CHUNK, CHUNK), 0)
    cols = lax.broadcasted_iota(jnp.int32, (CHUNK, CHUNK), 1)
    return rows >= cols


def _gmlp(p0, a_out, ws, bst, gv, SW):
    T, PW = p0.shape
    GW = (PW - SW) // 2
    H = GW // GMLP_HEAD
    D = SW + GW

    def body(p_ref, a_ref, ws_ref, b_ref, gv_ref, o_ref):
        _, u, _, _, vn = _gmlp_parts(p_ref[...], gv_ref[...], SW, GW)
        tri = _tril()
        o_ref[:, :SW] = a_ref[...]
        for hh in range(H):
            sl = slice(hh * GMLP_HEAD, (hh + 1) * GMLP_HEAD)
            wm = jnp.where(tri, ws_ref[hh], 0.0).astype(bf16)
            gate = _dot(wm, vn[:, sl].astype(bf16)) + b_ref[:, hh:hh + 1]
            o_ref[:, SW + hh * GMLP_HEAD:SW + (hh + 1) * GMLP_HEAD] = (u[:, sl] * gate).astype(bf16)

    return pl.pallas_call(
        body, name="gmlp", grid=(T // CHUNK,),
        in_specs=[pl.BlockSpec((CHUNK, PW), lambda i: (i, 0)), pl.BlockSpec((CHUNK, SW), lambda i: (i, 0)),
                  pl.BlockSpec((H, CHUNK, CHUNK), lambda i: (0, 0, 0)), pl.BlockSpec((CHUNK, H), lambda i: (0, 0)),
                  pl.BlockSpec((1, GW), lambda i: (0, 0))],
        out_specs=pl.BlockSpec((CHUNK, D), lambda i: (i, 0)),
        out_shape=jax.ShapeDtypeStruct((T, D), bf16),
        compiler_params=_params(("parallel",)))(p0, a_out, ws, bst, gv)


def _gmlp_bwd(p0, dmix, ws, bst, gv, SW):
    T, PW = p0.shape
    GW = (PW - SW) // 2
    H = GW // GMLP_HEAD
    D = SW + GW

    def body(p_ref, d_ref, ws_ref, b_ref, gv_ref, duv_ref, dws_ref, dbs_ref, dgv_ref):
        gv_ = gv_ref[...]
        uv, u, vh, r, vn = _gmlp_parts(p_ref[...], gv_, SW, GW)
        dout = d_ref[...][:, SW:].astype(f32)
        tri = _tril()

        @pl.when(pl.program_id(0) == 0)
        def _():
            dws_ref[...] = jnp.zeros_like(dws_ref)
            dbs_ref[...] = jnp.zeros_like(dbs_ref)
            dgv_ref[...] = jnp.zeros_like(dgv_ref)
        du, dvn = [], []
        for hh in range(H):
            sl = slice(hh * GMLP_HEAD, (hh + 1) * GMLP_HEAD)
            wm = jnp.where(tri, ws_ref[hh], 0.0).astype(bf16)
            vnh = vn[:, sl].astype(bf16)
            gate = _dot(wm, vnh) + b_ref[:, hh:hh + 1]
            dgate = dout[:, sl] * u[:, sl]
            du.append(dout[:, sl] * gate)
            dgb = dgate.astype(bf16)
            dws_ref[hh] += jnp.where(tri, _dg(dgb, vnh, NT), 0.0)
            dbs_ref[hh] += jnp.broadcast_to(jnp.sum(dgate, axis=1, keepdims=True), (CHUNK, CHUNK))
            dvn.append(_dg(wm, dgb, TN))
        dvn = jnp.concatenate(dvn, axis=1)
        dv, dgv = _rms_bwd(dvn, vh, r, gv_)
        dgv_ref[...] += dgv
        dge = jnp.concatenate(du + [dv], axis=1)
        duv_ref[...] = (dge * _gelu_grad(uv)).astype(bf16)

    return pl.pallas_call(
        body, name="gmlp_bwd", grid=(T // CHUNK,),
        in_specs=[pl.BlockSpec((CHUNK, PW), lambda i: (i, 0)), pl.BlockSpec((CHUNK, D), lambda i: (i, 0)),
                  pl.BlockSpec((H, CHUNK, CHUNK), lambda i: (0, 0, 0)), pl.BlockSpec((CHUNK, H), lambda i: (0, 0)),
                  pl.BlockSpec((1, GW), lambda i: (0, 0))],
        out_specs=[pl.BlockSpec((CHUNK, 2 * GW), lambda i: (i, 0)), pl.BlockSpec((H, CHUNK, CHUNK), lambda i: (0, 0, 0)),
                   pl.BlockSpec((H, CHUNK, CHUNK), lambda i: (0, 0, 0)), pl.BlockSpec((1, GW), lambda i: (0, 0))],
        out_shape=[jax.ShapeDtypeStruct((T, 2 * GW), bf16), jax.ShapeDtypeStruct((H, CHUNK, CHUNK), f32),
                   jax.ShapeDtypeStruct((H, CHUNK, CHUNK), f32), jax.ShapeDtypeStruct((1, GW), f32)],
        compiler_params=_params(("arbitrary",)))(p0, dmix, ws, bst, gv)


def _s5_disc(lr, li, ldt):
    lr = jnp.minimum(lr, LAMBDA_RE_MAX)
    dt = jnp.exp(ldt)
    mag = jnp.exp(lr * dt)
    ar = mag * jnp.cos(li * dt)
    ai = mag * jnp.sin(li * dt)
    den = lr * lr + li * li
    nr = ar - 1.0
    zr = (nr * lr + ai * li) / den
    zi = (ai * lr - nr * li) / den
    return ar, ai, zr, zi


def _s5_prep(lr, li, ldt):
    G, P = lr.shape

    def body(lr_ref, li_ref, ldt_ref, ar_ref, ai_ref, zr_ref, zi_ref):
        ar, ai, zr, zi = _s5_disc(lr_ref[...], li_ref[...], ldt_ref[...])
        ar_ref[...] = ar
        ai_ref[...] = ai
        zr_ref[...] = zr
        zi_ref[...] = zi

    s = jax.ShapeDtypeStruct((G, P), f32)
    return pl.pallas_call(body, name="s5_prep", out_shape=[s, s, s, s])(lr, li, ldt)


def _s5_prep_bwd(lr, li, ldt, dar, dai, dzr, dzi):
    G, P = lr.shape

    def body(lr_ref, li_ref, ldt_ref, dar_ref, dai_ref, dzr_ref, dzi_ref, o1, o2, o3):
        _, vjp = jax.vjp(_s5_disc, lr_ref[...], li_ref[...], ldt_ref[...])
        cts = tuple(jnp.sum(r[...], axis=0) for r in (dar_ref, dai_ref, dzr_ref, dzi_ref))
        a, b, c = vjp(cts)
        o1[...] = a
        o2[...] = b
        o3[...] = c

    s = jax.ShapeDtypeStruct((G, P), f32)
    return pl.pallas_call(body, name="s5_prep_bwd", out_shape=[s, s, jax.ShapeDtypeStruct((G, 1), f32)])(
        lr, li, ldt, dar, dai, dzr, dzi)


def _s5_bbd(zr, zi, bre, bim):
    SW, NS = bre.shape

    def body(zr_ref, zi_ref, br_ref, bi_ref, o_ref):
        zr_, zi_, br, bi = zr_ref[...], zi_ref[...], br_ref[...], bi_ref[...]
        o_ref[:, :NS] = (zr_ * br - zi_ * bi).astype(bf16)
        o_ref[:, NS:] = (zr_ * bi + zi_ * br).astype(bf16)

    return pl.pallas_call(body, name="s5_bbd", out_shape=jax.ShapeDtypeStruct((SW, 2 * NS), bf16))(zr, zi, bre, bim)


def _s5_bbd_bwd(dbbd, zr, zi, bre, bim):
    SW, NS = bre.shape

    def body(d_ref, zr_ref, zi_ref, br_ref, bi_ref, dbr_ref, dbi_ref, dzr_ref, dzi_ref):
        zr_, zi_, br, bi = zr_ref[...], zi_ref[...], br_ref[...], bi_ref[...]
        dr, di = d_ref[:, :NS], d_ref[:, NS:]
        dbr_ref[...] = zr_ * dr + zi_ * di
        dbi_ref[...] = zr_ * di - zi_ * dr
        dzr_ref[...] = jnp.sum(dr * br + di * bi, axis=0, keepdims=True)
        dzi_ref[...] = jnp.sum(di * br - dr * bi, axis=0, keepdims=True)

    m = jax.ShapeDtypeStruct((SW, NS), f32)
    v = jax.ShapeDtypeStruct((1, NS), f32)
    return pl.pallas_call(body, name="s5_bbd_bwd", out_shape=[m, m, v, v])(dbbd, zr, zi, bre, bim)


def _slab_cat(ref, NB):
    return jnp.concatenate([ref[j] for j in range(NB)], axis=1)


def _s5_in(p3, bbd, SW, tm):
    B, S, PW = p3.shape
    NS = bbd.shape[1] // 2
    NB = NS // LANES

    def body(u_ref, b_ref, xr_ref, xi_ref):
        x = _dot(u_ref[...], b_ref[...])
        for j in range(NB):
            xr_ref[j] = x[:, j * LANES:(j + 1) * LANES]
            xi_ref[j] = x[:, NS + j * LANES:NS + (j + 1) * LANES]

    slab = jax.ShapeDtypeStruct((B, NB, S, LANES), f32)
    sspec = pl.BlockSpec((None, NB, tm, LANES), lambda b, i: (b, 0, i, 0))
    return pl.pallas_call(
        body, name="s5_in", grid=(B, S // tm),
        in_specs=[pl.BlockSpec((None, tm, SW), lambda b, i: (b, i, 0)), pl.BlockSpec((SW, 2 * NS), lambda b, i: (0, 0))],
        out_specs=[sspec, sspec], out_shape=[slab, slab],
        compiler_params=_params(("parallel", "parallel")))(p3, bbd)


def _s5_scan(name, xr, xi, ar, ai, reverse, hr=None, hi=None, phase=None):
    B, NB, S, _ = xr.shape
    L = S // NSUB
    nb = 2 if (hr is None and NB % 2 == 0) else 1
    with_da = hr is not None

    def body(*refs):
        if with_da:
            xr_ref, xi_ref, ar_ref, ai_ref, hr_ref, hi_ref, or_ref, oi_ref, dar_ref, dai_ref, pr_scr, pi_scr = refs
        else:
            xr_ref, xi_ref, ar_ref, ai_ref, or_ref, oi_ref, pr_scr, pi_scr = refs
        sign = -1.0 if reverse else 1.0
        a_r = [jnp.broadcast_to(ar_ref[j], (NSUB, LANES)) for j in range(nb)]
        a_i = [jnp.broadcast_to(ai_ref[j], (NSUB, LANES)) * sign for j in range(nb)]

        def step(t, carry):
            row = (L - 1 - t) if reverse else t
            rows = pl.ds(row, NSUB, stride=L)
            out = []
            for j in range(nb):
                sr, si, pr, pi = carry[j]
                nr = a_r[j] * sr - a_i[j] * si + xr_ref.at[j][rows, :]
                ni = a_r[j] * si + a_i[j] * sr + xi_ref.at[j][rows, :]
                or_ref.at[j][rows, :] = nr
                oi_ref.at[j][rows, :] = ni
                npr = a_r[j] * pr - a_i[j] * pi
                npi = a_r[j] * pi + a_i[j] * pr
                pr_scr[j, pl.ds(row, 1), :] = npr[0:1]
                pi_scr[j, pl.ds(row, 1), :] = npi[0:1]
                out.append((nr, ni, npr, npi))
            return tuple(out)
        z = jnp.zeros((NSUB, LANES), f32)
        one = jnp.ones((NSUB, LANES), f32)
        fin = lax.fori_loop(0, L, step, tuple((z, z, one, z) for _ in range(nb)))

        for j in range(nb):
            sr, si, plr, pli = fin[j]
            plr, pli = plr[0:1], pli[0:1]
            cr = jnp.zeros((1, LANES), f32)
            ci = jnp.zeros((1, LANES), f32)
            order = range(NSUB - 2, -1, -1) if reverse else range(1, NSUB)
            for c in order:
                src = c + 1 if reverse else c - 1
                cr, ci = (sr[src:src + 1] + plr * cr - pli * ci, si[src:src + 1] + plr * ci + pli * cr)
                rows = slice(c * L, (c + 1) * L)
                tr, ti = pr_scr[j], pi_scr[j]
                or_ref[j, rows, :] += tr * cr - ti * ci
                oi_ref[j, rows, :] += tr * ci + ti * cr
            if with_da:
                first = lax.broadcasted_iota(jnp.int32, (L, LANES), 0) == 0
                dar = jnp.zeros((1, LANES), f32)
                dai = jnp.zeros((1, LANES), f32)
                for c in range(NSUB):
                    rows = slice(c * L, (c + 1) * L)
                    if c == 0:
                        lr_, li_ = jnp.zeros((1, LANES), f32), jnp.zeros((1, LANES), f32)
                    else:
                        lr_, li_ = hr_ref[j, c * L - 1:c * L, :], hi_ref[j, c * L - 1:c * L, :]
                    hpr = jnp.where(first, lr_, pltpu.roll(hr_ref[j, rows, :], 1, 0))
                    hpi = jnp.where(first, li_, pltpu.roll(hi_ref[j, rows, :], 1, 0))
                    gr, gi = or_ref[j, rows, :], oi_ref[j, rows, :]
                    dar += jnp.sum(hpr * gr + hpi * gi, axis=0, keepdims=True)
                    dai += jnp.sum(hpr * gi - hpi * gr, axis=0, keepdims=True)
                dar_ref[j] = dar
                dai_ref[j] = dai

    slab = jax.ShapeDtypeStruct((B, NB, S, LANES), f32)
    sspec = pl.BlockSpec((None, nb, S, LANES), lambda b, j: (b, j, 0, 0))
    aspec = pl.BlockSpec((nb, 1, LANES), lambda b, j: (j, 0, 0))
    in_specs = [sspec, sspec, aspec, aspec]
    out_specs = [sspec, sspec]
    out_shape = [slab, slab]
    args = [xr, xi, ar, ai]
    if with_da:
        in_specs += [sspec, sspec]
        args += [hr, hi]
        dspec = pl.BlockSpec((None, nb, 1, LANES), lambda b, j: (b, j, 0, 0))
        out_specs += [dspec, dspec]
        out_shape += [jax.ShapeDtypeStruct((B, NB, 1, LANES), f32)] * 2
    return _call(body, name, (B, NB // nb), in_specs, out_specs, out_shape, args,
                 scratch=[pltpu.VMEM((nb, L, LANES), f32), pltpu.VMEM((nb, L, LANES), f32)],
                 sem=("parallel", "parallel"), phase=phase)


def _s5_out_parts(hr_ref, hi_ref, u_ref, cr_ref, ci_ref, d_ref, wg_ref, bg_ref, NB):
    hcr = _slab_cat(hr_ref, NB).astype(bf16)
    hci = _slab_cat(hi_ref, NB).astype(bf16)
    u = u_ref[...].astype(f32)
    y2 = _dot(hcr, cr_ref[...]) - _dot(hci, ci_ref[...]) + d_ref[...] * u
    yg = _gelu(y2)
    s = jax.nn.sigmoid(_dot(yg.astype(bf16), wg_ref[...]) + bg_ref[...])
    return hcr, hci, u, y2, yg, s


def _s5_out_specs(B, S, NB, NS, SW, tm):
    sspec = pl.BlockSpec((None, NB, tm, LANES), lambda b, i: (b, 0, i, 0))
    full = lambda r, c: pl.BlockSpec((r, c), lambda b, i: (0, 0))
    return sspec, [sspec, sspec, pl.BlockSpec((None, tm, SW), lambda b, i: (b, i, 0)), full(NS, SW), full(NS, SW),
                   full(1, SW), full(SW, SW), full(1, SW)]


def _s5_out(hr, hi, p3, cbr, cbi, dsk, wglu, bglu, tm):
    B, NB, S, _ = hr.shape
    NS, SW = cbr.shape

    def body(hr_ref, hi_ref, u_ref, cr_ref, ci_ref, d_ref, wg_ref, bg_ref, o_ref):
        _, _, _, _, yg, s = _s5_out_parts(hr_ref, hi_ref, u_ref, cr_ref, ci_ref, d_ref, wg_ref, bg_ref, NB)
        o_ref[...] = (yg * s).astype(bf16)

    _, in_specs = _s5_out_specs(B, S, NB, NS, SW, tm)
    return pl.pallas_call(
        body, name="s5_out", grid=(B, S // tm), in_specs=in_specs,
        out_specs=pl.BlockSpec((None, tm, SW), lambda b, i: (b, i, 0)),
        out_shape=jax.ShapeDtypeStruct((B, S, SW), bf16),
        compiler_params=_params(("parallel", "parallel")))(hr, hi, p3, cbr, cbi, dsk, wglu, bglu)


def _s5_out_bwd(hr, hi, p3, dmix3, cbr, cbi, dsk, wglu, bglu, tm):
    B, NB, S, _ = hr.shape
    NS, SW = cbr.shape

    def body(hr_ref, hi_ref, u_ref, cr_ref, ci_ref, d_ref, wg_ref, bg_ref, da_ref,
             dhr_ref, dhi_ref, du_ref, dcr_ref, dci_ref, dd_ref, dwg_ref, dbg_ref):
        hcr, hci, u, y2, yg, s = _s5_out_parts(hr_ref, hi_ref, u_ref, cr_ref, ci_ref, d_ref, wg_ref, bg_ref, NB)
        da = da_ref[...].astype(f32)
        dz = da * yg * s * (1.0 - s)
        dzb = dz.astype(bf16)
        dyg = da * s + _dg(dzb, wg_ref[...], NT)
        dy2 = dyg * _gelu_grad(y2)
        dyb = dy2.astype(bf16)

        @pl.when((pl.program_id(0) == 0) & (pl.program_id(1) == 0))
        def _():
            for r in (dcr_ref, dci_ref, dd_ref, dwg_ref, dbg_ref):
                r[...] = jnp.zeros_like(r)
        dwg_ref[...] += _dg(yg.astype(bf16), dzb, TN)
        dbg_ref[...] += jnp.sum(dz, axis=0, keepdims=True)
        dd_ref[...] += jnp.sum(dy2 * u, axis=0, keepdims=True)
        dcr_ref[...] += _dg(hcr, dyb, TN)
        dci_ref[...] -= _dg(hci, dyb, TN)
        du_ref[...] = dy2 * d_ref[...]
        dhr = _dg(dyb, cr_ref[...], NT)
        dhi = _dg(dyb, ci_ref[...], NT)
        for j in range(NB):
            dhr_ref[j] = dhr[:, j * LANES:(j + 1) * LANES]
            dhi_ref[j] = -dhi[:, j * LANES:(j + 1) * LANES]

    sspec, in_specs = _s5_out_specs(B, S, NB, NS, SW, tm)
    in_specs = in_specs + [pl.BlockSpec((None, tm, SW), lambda b, i: (b, i, 0))]
    full = lambda r, c: pl.BlockSpec((r, c), lambda b, i: (0, 0))
    slab = jax.ShapeDtypeStruct((B, NB, S, LANES), f32)
    mat = lambda r, c: jax.ShapeDtypeStruct((r, c), f32)
    return pl.pallas_call(
        body, name="s5_out_bwd", grid=(B, S // tm), in_specs=in_specs,
        out_specs=[sspec, sspec, pl.BlockSpec((None, tm, SW), lambda b, i: (b, i, 0)), full(NS, SW), full(NS, SW),
                   full(1, SW), full(SW, SW), full(1, SW)],
        out_shape=[slab, slab, jax.ShapeDtypeStruct((B, S, SW), f32), mat(NS, SW), mat(NS, SW), mat(1, SW),
                   mat(SW, SW), mat(1, SW)],
        compiler_params=_params(("arbitrary", "arbitrary")))(hr, hi, p3, cbr, cbi, dsk, wglu, bglu, dmix3)


def _s5_in_bwd(gr, gi, p3, bbd, du_skip, duv3, tm):
    B, NB, S, _ = gr.shape
    SW, NS2 = bbd.shape
    PW = SW + duv3.shape[2]

    def body(gr_ref, gi_ref, u_ref, b_ref, ds_ref, duv_ref, dp_ref, db_ref):
        g = jnp.concatenate([_slab_cat(gr_ref, NB), _slab_cat(gi_ref, NB)], axis=1).astype(bf16)
        du = _dg(g, b_ref[...], NT) + ds_ref[...]
        dp_ref[:, :SW] = du.astype(bf16)
        dp_ref[:, SW:] = duv_ref[...]

        @pl.when((pl.program_id(0) == 0) & (pl.program_id(1) == 0))
        def _():
            db_ref[...] = jnp.zeros_like(db_ref)
        db_ref[...] += _dg(u_ref[...], g, TN)

    sspec = pl.BlockSpec((None, NB, tm, LANES), lambda b, i: (b, 0, i, 0))
    row = lambda c: pl.BlockSpec((None, tm, c), lambda b, i: (b, i, 0))
    return pl.pallas_call(
        body, name="s5_in_bwd", grid=(B, S // tm),
        in_specs=[sspec, sspec, row(SW), pl.BlockSpec((SW, NS2), lambda b, i: (0, 0)), row(SW), row(PW - SW)],
        out_specs=[row(PW), pl.BlockSpec((SW, NS2), lambda b, i: (0, 0))],
        out_shape=[jax.ShapeDtypeStruct((B, S, PW), bf16), jax.ShapeDtypeStruct((SW, NS2), f32)],
        compiler_params=_params(("arbitrary", "arbitrary")))(gr, gi, p3, bbd, du_skip, duv3)


BIG = ['ev_w_in', 'ev_w_out', 'od_w_in', 'od_w_out', 'ffn_w_up', 'ffn_w_down']
ANY = pl.BlockSpec(memory_space=pl.ANY)


def _rtile(rows, mult):
    best = None
    for d in range(mult, min(rows, 512) + 1, mult):
        if rows % d == 0:
            best = d
    assert best is not None, (rows, mult)
    return best


def _pair_sum(name, g, recv, c_idx, out_dtype):
    NCH, R, W = g.shape
    HALF_W = W // 2
    tr = _rtile(R, 16)

    def body(c_ref, a_ref, b_ref, o_ref):
        o_ref[...] = (a_ref[...] + b_ref[...]).astype(out_dtype)

    return pl.pallas_call(
        body, name=name,
        grid_spec=pltpu.PrefetchScalarGridSpec(
            num_scalar_prefetch=1, grid=(NCH, R // tr),
            in_specs=[pl.BlockSpec((None, tr, HALF_W), lambda j, i, c: (j, i, c[0])),
                      pl.BlockSpec((None, tr, HALF_W), lambda j, i, c: (j, i, 0))],
            out_specs=pl.BlockSpec((None, tr, HALF_W), lambda j, i, c: (j, i, 0))),
        out_shape=jax.ShapeDtypeStruct((NCH, R, HALF_W), out_dtype),
        compiler_params=_params(("parallel", "parallel")))(c_idx, g, recv)


def _chip_sum(name, r3, h, k_idx):
    NCH, R, Wh = r3.shape
    tr = _rtile(R, 16)

    def body(k_ref, a_ref, own_ref, o_ref):
        own = own_ref[...].astype(f32)
        t = [jnp.where(k_ref[0] == s, own, a_ref[s].astype(f32)) for s in range(NCH)]
        o_ref[...] = ((t[0] + t[1]) + t[2]) + t[3]

    return pl.pallas_call(
        body, name=name,
        grid_spec=pltpu.PrefetchScalarGridSpec(
            num_scalar_prefetch=1, grid=(R // tr,),
            in_specs=[pl.BlockSpec((NCH, tr, Wh), lambda i, k: (0, i, 0)),
                      pl.BlockSpec((None, tr, Wh), lambda i, k: (k[0], i, 0))],
            out_specs=pl.BlockSpec((tr, Wh), lambda i, k: (i, 0))),
        out_shape=jax.ShapeDtypeStruct((R, Wh), f32),
        compiler_params=_params(("parallel",)))(k_idx, r3, h)


def _adam_math(gg, w, m, v):
    nm = ADAM_B1 * m + (1.0 - ADAM_B1) * gg
    nv = ADAM_B2 * v + (1.0 - ADAM_B2) * jnp.square(gg)
    m_hat = nm / (1.0 - ADAM_B1 ** ADAM_STEP)
    v_hat = nv / (1.0 - ADAM_B2 ** ADAM_STEP)
    return -ADAM_LR * (m_hat / (jnp.sqrt(v_hat) + ADAM_EPS) + ADAM_WD * w), nm, nv


def _adamw(name, mine, theirs, c_idx, w, m, v, lead, transposed, prev=None):
    L, R, W = w.shape
    if transposed:
        bw = LANES if W % LANES == 0 else W
        gspec = pl.BlockSpec((bw, R // 2), lambda i, hf, c: (i, 0))
        wspec = pl.BlockSpec((None, R // 2, bw), lambda i, hf, c: (lead, hf, i))
        grid = (W // bw, 2)
    else:
        tr = _rtile(R, SUBLANES)
        gspec = pl.BlockSpec((tr, W // 2), lambda i, hf, c: (i, 0))
        wspec = pl.BlockSpec((None, tr, W // 2), lambda i, hf, c: (lead, i, hf))
        grid = (R // tr, 2)

    def body(c_ref, a_ref, b_ref, w_ref, m_ref, v_ref, *rest):
        go_ref, d_ref, nm_ref, nv_ref = rest[-4:]
        gg = jnp.where(pl.program_id(1) == c_ref[0], a_ref[...], b_ref[...])
        if transposed:
            gg = gg.T
        d, nm, nv = _adam_math(gg, w_ref[...], m_ref[...], v_ref[...])
        go_ref[...] = gg
        d_ref[...] = d
        nm_ref[...] = nm
        nv_ref[...] = nv

    in_specs = [gspec, gspec, wspec, wspec, wspec]
    args, aliases = [c_idx, mine, theirs, w, m, v], {}
    if prev is not None:
        in_specs += [ANY] * 4
        args += list(prev)
        aliases = {6: 0, 7: 1, 8: 2, 9: 3}
    s = jax.ShapeDtypeStruct((L, R, W), f32)
    return pl.pallas_call(
        body, name=name,
        grid_spec=pltpu.PrefetchScalarGridSpec(num_scalar_prefetch=1, grid=grid, in_specs=in_specs,
                                               out_specs=[wspec] * 4),
        out_shape=[s, s, s, s], input_output_aliases=aliases,
        compiler_params=_params(("parallel", "arbitrary")))(*args)


def _place():
    x, y, c = lax.axis_index("x"), lax.axis_index("y"), lax.axis_index("c")
    return x, y, c, [(1 - x, y), (x, 1 - y), (1 - x, 1 - y)]


def _gathered_shape(sh, kind):
    if kind == "rows":
        return sh[:-2] + (N_CHIPS * sh[-2], sh[-1])
    if kind == "cols":
        return sh[:-1] + (N_CHIPS * sh[-1],)
    return (N_CHIPS,) + sh


def _place_shard(name, shard, kind, k_idx):
    sh = shard.shape
    r, C = sh[-2], sh[-1]
    L = sh[0] if len(sh) == 3 else 1
    tr = _rtile(r, 16)
    nr = r // tr
    if kind == "rows":
        out3, omap = (L, N_CHIPS * r, C), lambda l, i, k: (l, k[0] * nr + i, 0)
    elif kind == "cols":
        out3, omap = (L, r, N_CHIPS * C), lambda l, i, k: (l, i, k[0])
    else:
        out3, omap = (N_CHIPS, r, C), lambda l, i, k: (k[0], i, 0)

    def body(k_ref, s_ref, o_ref):
        o_ref[...] = s_ref[...]

    out = pl.pallas_call(
        body, name=name,
        grid_spec=pltpu.PrefetchScalarGridSpec(
            num_scalar_prefetch=1, grid=(L, nr),
            in_specs=[pl.BlockSpec((None, tr, C), lambda l, i, k: (l, i, 0))],
            out_specs=pl.BlockSpec((None, tr, C), omap)),
        out_shape=jax.ShapeDtypeStruct(out3, shard.dtype),
        compiler_params=_params(("parallel", "parallel")))(k_idx, shard.reshape(L, r, C))
    return out.reshape(_gathered_shape(sh, kind))


def _gather_phase(shards, fulls, kinds):
    n = len(shards)
    shapes = [s.shape for s in shards]

    def window(ref, a, k, h=None):
        sh, kind = shapes[a], kinds[a]
        r = sh[-2]
        start, size = (0, r) if h is None else (h * (r // 2), r // 2)
        lead = (slice(None),) * (len(sh) - 2)
        if kind == "rows":
            return ref.at[lead + (pl.ds(k * r + start, size), slice(None))]
        if kind == "cols":
            return ref.at[lead + (pl.ds(start, size), pl.ds(pl.multiple_of(k * sh[-1], LANES), sh[-1]))]
        return ref.at[(k,) + lead + (pl.ds(start, size), slice(None))]

    def copies(s_refs, o_refs, sems):
        send_sems, recv_sems = sems
        x, y, c, chips = _place()
        k = 2 * x + y

        def copy(a, j, kk, hh, to, src=None):
            dst = window(o_refs[a], a, kk, hh)
            return pltpu.make_async_remote_copy(
                src_ref=dst if src is None else src, dst_ref=dst, send_sem=send_sems.at[6 * a + j],
                recv_sem=recv_sems.at[6 * a + j], device_id=to, device_id_type=MESH)

        first = []
        for a in range(n):
            r = shapes[a][-2]
            lead = (slice(None),) * (len(shapes[a]) - 2)
            src = s_refs[a].at[lead + (pl.ds(c * (r // 2), r // 2), slice(None))]
            first += [copy(a, j, k, c, (*chip, c), src=src) for j, chip in enumerate(chips)]
        return copy, first, (x, y, c), (x, y, 1 - c), c, chips

    def start(s_refs, o_refs, sems):
        for cp in copies(s_refs, o_refs, sems)[1]:
            cp.start()

    def finish(s_refs, o_refs, sems):
        copy, first, me, sibling, c, chips = copies(s_refs, o_refs, sems)
        passed = []
        for j, (cx, cy) in enumerate(chips):
            for a in range(n):
                copy(a, j, 2 * cx + cy, c, me).wait_recv()
                fwd = copy(a, 3 + j, 2 * cx + cy, c, sibling)
                fwd.start()
                passed.append(fwd)
        for j, (cx, cy) in enumerate(chips):
            for a in range(n):
                copy(a, 3 + j, 2 * cx + cy, 1 - c, me).wait_recv()
        for cp in first + passed:
            cp.wait_send()

    return _Phase(shards, fulls, [jax.ShapeDtypeStruct(f.shape, f.dtype) for f in fulls],
                  [pltpu.SemaphoreType.DMA((6 * n,)), pltpu.SemaphoreType.DMA((6 * n,))], start, finish)


def _comm_pair_swap(tag, gs):
    n = len(gs)

    def body(*refs):
        g_refs, o_refs, send_sems, recv_sems = refs[:n], refs[n:2 * n], refs[2 * n], refs[2 * n + 1]
        x, y, c, _ = _place()
        half = [g.shape[2] // 2 for g in gs]
        cps = [pltpu.make_async_remote_copy(
            src_ref=g_refs[a].at[:, :, pl.ds(pl.multiple_of((1 - c) * half[a], LANES), half[a])], dst_ref=o_refs[a], send_sem=send_sems.at[a],
            recv_sem=recv_sems.at[a], device_id=(x, y, 1 - c), device_id_type=MESH) for a in range(n)]
        for cp in cps:
            cp.start()
        for cp in cps:
            cp.wait()

    return pl.pallas_call(
        body, name="comm_pair_swap_" + tag, in_specs=[ANY] * n, out_specs=[ANY] * n,
        out_shape=[jax.ShapeDtypeStruct(g.shape[:2] + (g.shape[2] // 2,), g.dtype) for g in gs],
        scratch_shapes=[pltpu.SemaphoreType.DMA((n,)), pltpu.SemaphoreType.DMA((n,))])(*gs)


def _exchange_phase(hs):
    n = len(hs)

    def copies(h_refs, o_refs, sems):
        send_sems, recv_sems = sems
        x, y, c, chips = _place()
        k = 2 * x + y

        def copy(a, j, src_slot, dst_slot):
            cx, cy = chips[j]
            return pltpu.make_async_remote_copy(
                src_ref=h_refs[a].at[src_slot], dst_ref=o_refs[a].at[dst_slot], send_sem=send_sems.at[3 * a + j],
                recv_sem=recv_sems.at[3 * a + j], device_id=(cx, cy, c), device_id_type=MESH)

        sends = [copy(a, j, 2 * cx + cy, k) for a in range(n) for j, (cx, cy) in enumerate(chips)]
        return copy, sends, k, chips

    def start(h_refs, o_refs, sems):
        for cp in copies(h_refs, o_refs, sems)[1]:
            cp.start()

    def finish(h_refs, o_refs, sems):
        copy, sends, k, chips = copies(h_refs, o_refs, sems)
        for a in range(n):
            for j, (cx, cy) in enumerate(chips):
                copy(a, j, k, 2 * cx + cy).wait_recv()
        for cp in sends:
            cp.wait_send()

    return _Phase(hs, [], [jax.ShapeDtypeStruct(h.shape, h.dtype) for h in hs],
                  [pltpu.SemaphoreType.DMA((3 * n,)), pltpu.SemaphoreType.DMA((3 * n,))], start, finish)


def _comm_pair_share(tag, gs):
    n = len(gs)

    def body(*refs):
        g_refs, o_refs, send_sems, recv_sems = refs[:n], refs[n:2 * n], refs[2 * n], refs[2 * n + 1]
        x, y, c, _ = _place()
        cps = [pltpu.make_async_remote_copy(
            src_ref=g_refs[a], dst_ref=o_refs[a], send_sem=send_sems.at[a], recv_sem=recv_sems.at[a],
            device_id=(x, y, 1 - c), device_id_type=MESH) for a in range(n)]
        for cp in cps:
            cp.start()
        for cp in cps:
            cp.wait()

    return pl.pallas_call(
        body, name="comm_pair_share_" + tag, in_specs=[ANY] * n, out_specs=[ANY] * n,
        out_shape=[jax.ShapeDtypeStruct(g.shape, g.dtype) for g in gs],
        scratch_shapes=[pltpu.SemaphoreType.DMA((n,)), pltpu.SemaphoreType.DMA((n,))])(*gs)


def _pad_rows(flat, unit):
    n = flat.shape[-1]
    pad = (-n) % unit
    if pad:
        flat = jnp.pad(flat, [(0, 0)] * (flat.ndim - 1) + [(0, pad)])
    return flat


def _split_chips(full, axis):
    sh = full.shape
    t = full.reshape(sh[:axis] + (N_CHIPS, sh[axis] // N_CHIPS) + sh[axis + 1:])
    return jnp.moveaxis(t, axis, 0).reshape(N_CHIPS, -1)


def _join_chips(stack, shard_shape, axis):
    t = jnp.moveaxis(stack.reshape((N_CHIPS,) + tuple(shard_shape)), 0, axis)
    sh = t.shape
    return t.reshape(sh[:axis] + (sh[axis] * sh[axis + 1],) + sh[axis + 2:])


def _block_diag(blocks):
    G, r, c = blocks.shape
    eye = jnp.eye(G, dtype=blocks.dtype)
    return (blocks[:, :, None, :] * eye[:, None, :, None]).reshape(G * r, G * c)


def _diag_blocks(m, G):
    r, c = m.shape[0] // G, m.shape[1] // G
    idx = jnp.arange(G)
    return m.reshape(G, r, G, c)[idx, :, idx, :]


def _weight_shards(w):
    conv = jnp.concatenate([w[n].reshape(-1) for n in GATHER_F32])
    conv = _pad_rows(conv, 2 * SUBLANES * LANES).reshape(-1, LANES)
    b16 = lambda a: a.astype(bf16)
    return {'ev_w_in': (b16(w['ev_w_in'][0]), "chip"), 'ev_w_out': (b16(w['ev_w_out'][0]), "rows"),
            's5_w_glu': (b16(w['s5_w_glu'][0]), "rows"), 'conv': (conv, "chip"),
            'od_w_in': (b16(w['od_w_in'][0]), "cols"), 'od_w_out': (b16(w['od_w_out'][0]), "rows"),
            'ffn_w_up0': (b16(w['ffn_w_up'][0]), "cols"), 'ffn_w_up1': (b16(w['ffn_w_up'][1]), "cols"),
            'ffn_w_down0': (b16(w['ffn_w_down'][0]), "rows"), 'ffn_w_down1': (b16(w['ffn_w_down'][1]), "rows")}


def kernel(x, mix_norm_g, ffn_norm_g, final_norm_g, ev_w_in, ev_w_out, s5_lam_re, s5_lam_im, s5_log_dt, s5_b_re, s5_b_im, s5_c_re, s5_c_im, s5_d, s5_w_glu, s5_b_glu, gm_w_s, gm_b_s, gm_v_g, od_w_in, od_conv_w, od_conv_b, od_w_out, ffn_w_up, ffn_conv_w, ffn_conv_b, ffn_w_down, loss_target, m_mix_norm_g, m_ffn_norm_g, m_final_norm_g, m_ev_w_in, m_ev_w_out, m_s5_lam_re, m_s5_lam_im, m_s5_log_dt, m_s5_b_re, m_s5_b_im, m_s5_c_re, m_s5_c_im, m_s5_d, m_s5_w_glu, m_s5_b_glu, m_gm_w_s, m_gm_b_s, m_gm_v_g, m_od_w_in, m_od_conv_w, m_od_conv_b, m_od_w_out, m_ffn_w_up, m_ffn_conv_w, m_ffn_conv_b, m_ffn_w_down, v_mix_norm_g, v_ffn_norm_g, v_final_norm_g, v_ev_w_in, v_ev_w_out, v_s5_lam_re, v_s5_lam_im, v_s5_log_dt, v_s5_b_re, v_s5_b_im, v_s5_c_re, v_s5_c_im, v_s5_d, v_s5_w_glu, v_s5_b_glu, v_gm_w_s, v_gm_b_s, v_gm_v_g, v_od_w_in, v_od_conv_w, v_od_conv_b, v_od_w_out, v_ffn_w_up, v_ffn_conv_w, v_ffn_conv_b, v_ffn_w_down):
    loc = dict(locals())
    w = {n: loc[n] for n in WEIGHTS}
    mom = {n: loc["m_" + n] for n in WEIGHTS}
    var = {n: loc["v_" + n] for n in WEIGHTS}

    B, S, D = x.shape
    T = B * S
    SW = s5_d.shape[1]
    G = SW // SSM_GROUP
    NS = G * SSM_STATE
    NB = NS // LANES
    tm = min(512, S)
    tt = min(1024, T)
    c_idx = lax.axis_index("c").astype(jnp.int32).reshape(1)
    k_idx = (2 * lax.axis_index("x") + lax.axis_index("y")).astype(jnp.int32).reshape(1)
    shards = _weight_shards(w)
    placed = {n: _place_shard("place_" + n, s, kd, k_idx) for n, (s, kd) in shards.items()}

    def gather(names):
        return _gather_phase([shards[n][0] for n in names], [placed[n] for n in names], [shards[n][1] for n in names])

    w_ev_in, w_ev_out, w_glu, conv = _run_phase("comm_gather_mixer0", gather(['ev_w_in', 'ev_w_out', 's5_w_glu', 'conv']))
    w_ev_in = jnp.swapaxes(w_ev_in, 0, 1).reshape(D, -1)
    full, off = {}, 0
    for n in GATHER_F32:
        full[n] = _join_chips(conv.reshape(N_CHIPS, -1)[:, off:off + w[n].size], w[n].shape, SHARD_AXIS[n])
        off += w[n].size

    h0 = x.reshape(T, D)
    y0, p0 = _norm_mm("ev_in", h0, mix_norm_g[0], w_ev_in, tm)
    PW = p0.shape[1]
    p03 = p0.reshape(B, S, PW)
    lr, li, ldt = s5_lam_re[0], s5_lam_im[0], s5_log_dt[0].reshape(G, 1)
    ar, ai, zr, zi = _s5_prep(lr, li, ldt)
    bre = _block_diag(jnp.swapaxes(s5_b_re[0], 1, 2))
    bim = _block_diag(jnp.swapaxes(s5_b_im[0], 1, 2))
    cbr = _block_diag(jnp.swapaxes(s5_c_re[0], 1, 2)).astype(bf16)
    cbi = _block_diag(jnp.swapaxes(s5_c_im[0], 1, 2)).astype(bf16)
    zr_row, zi_row = zr.reshape(1, NS), zi.reshape(1, NS)
    bbd = _s5_bbd(zr_row, zi_row, bre, bim)
    ar_s, ai_s = ar.reshape(NB, 1, LANES), ai.reshape(NB, 1, LANES)
    xr, xi = _s5_in(p03, bbd, SW, tm)
    (hr, hi), (w_up0, w_down0) = _s5_scan("s5_scan", xr, xi, ar_s, ai_s, False,
                                           phase=gather(['ffn_w_up0', 'ffn_w_down0']))
    dsk, bglu = s5_d.reshape(1, SW), s5_b_glu.reshape(1, SW)
    a_out = _s5_out(hr, hi, p03, cbr, cbi, dsk, w_glu, bglu, tm)
    ws, bst, gv = gm_w_s[0], gm_b_s[0].T, gm_v_g.reshape(1, -1)
    mixcat = _gmlp(p0, a_out.reshape(T, SW), ws, bst, gv, SW)
    h1 = _mm_resid("ev_out", mixcat, w_ev_out, h0, tm)

    def ffn_fwd(l, h, w_up, w_down, up_phase=None, down_phase=None):
        res = _norm_mm(f"ffn_up{l}", h, ffn_norm_g[l], w_up, tm, phase=up_phase)
        (z, up), got_up = res if up_phase is not None else (res, None)
        res = _ffn_down(f"ffn_down{l}", up, full['ffn_conv_w'][l], ffn_conv_b[l].reshape(1, -1), w_down, h, S, tm,
                        phase=down_phase)
        (hn,), got_down = res if down_phase is not None else (res, None)
        return hn, (z, up.reshape(B, S, -1)), got_up, got_down

    h2, ffn0, (w_up1, w_down1), (w_od_in, w_od_out) = ffn_fwd(
        0, h1, w_up0, w_down0, gather(['ffn_w_up1', 'ffn_w_down1']), gather(['od_w_in', 'od_w_out']))
    w_ups, w_downs = (w_up0, w_up1), (w_down0, w_down1)
    od_cw, od_cb = full['od_conv_w'][0], full['od_conv_b']
    y1, p1 = _norm_mm("od_in", h2, mix_norm_g[1], w_od_in, tm)
    p13 = p1.reshape(B, S, -1)
    sc = _od_act(p13, od_cw, od_cb)
    h3 = _mm_resid("od_out", sc.reshape(T, D), w_od_out, h2, tm)
    h4, ffn1, _, _ = ffn_fwd(1, h3, w_up1, w_down1)

    dh4, dh4b, loss_part, d_final_g = _final_loss(h4, final_norm_g, loss_target.reshape(T, D), tm)
    loss = lax.psum(loss_part[0, 0], ("x", "y", "c"))

    grads = {}

    halves = {}

    def reduce_begin(tag, names, parts):
        recv = _comm_pair_swap(tag, parts)
        return [_pair_sum(f"pair_sum_{n}", g, r, c_idx, f32 if n == "small" else bf16)
                for n, g, r in zip(names, parts, recv)]

    def reduce_end(tag, names, hsum, r3):
        mine = [_chip_sum(f"chip_sum_{n}", r, h, k_idx) for n, r, h in zip(names, r3, hsum)]
        theirs = _comm_pair_share(tag, mine)
        halves.update({n: (a, b) for n, a, b in zip(names, mine, theirs)})

    def ffn_bwd(l, dh, dhb, h_in, saved, phase=None):
        z, up3 = saved
        w_down, w_up = w_downs[l], w_ups[l]
        res = _ffn_act_bwd(f"ffn_act_bwd{l}", up3, dhb.reshape(B, S, D), w_down.T, full['ffn_conv_w'][l],
                           ffn_conv_b[l].reshape(1, -1), phase=phase)
        (act, dg3, dv3, dcwg, dcwv, dcbg, dcbv), got = res if phase is not None else (res, None)
        g_down = _mm_tn(f"ffn_down_dw{l}", act.reshape(T, -1), dhb, tt)
        dupg, dupv = dg3.reshape(T, -1), dv3.reshape(T, -1)
        F = dupg.shape[1]
        g_up = _mm_tn(f"ffn_up_dw{l}_gate", dupg, z, tt, rows=2 * F)
        g_up = _mm_tn(f"ffn_up_dw{l}_val", dupv, z, tt, rows=2 * F, row_off=F, prev=g_up)
        dh_new, dhb_new, dg = _mm_nt_normbwd(f"ffn_up_bwd{l}", [dupg, dupv], w_up, h_in, ffn_norm_g[l], dh, tm)
        F = dg3.shape[2]
        dcw = jnp.concatenate([dcwg[:, :F], dcwv[:, :F]], axis=1)
        dcb = jnp.concatenate([dcbg[:, :F], dcbv[:, :F]], axis=1)
        return dh_new, dhb_new, g_down, g_up, dcw, dcb[0], dg[0], got

    chips = lambda g: g.reshape(N_CHIPS, -1, D)
    dh3, dh3b, gd1, gu1, gcw1, gcb1, gng1, _ = ffn_bwd(1, dh4, dh4b, h3, ffn1)
    dsc = _mm_nt("od_out_bwd", dh3b, w_od_out, tm)
    g_od_out = _mm_tn("od_out_dw", sc.reshape(T, D), dh3b, tt)
    dbg3, dcg3, dhx3, d_od_cw, d_od_cb = _od_act_bwd(p13, dsc.reshape(B, S, D), od_cw, od_cb)
    dp1 = [t.reshape(T, D) for t in (dbg3, dcg3, dhx3)]
    g_od_in = None
    for i, piece in enumerate(dp1):
        g_od_in = _mm_tn(f"od_in_dw{i}", piece, y1, tt, rows=3 * D, row_off=i * D, prev=g_od_in)
    grads['od_conv_w'] = d_od_cw[None]
    grads['od_conv_b'] = d_od_cb
    dh2, dh2b, gmix1 = _mm_nt_normbwd("od_in_bwd", dp1, w_od_in, h2, mix_norm_g[1], dh3, tm)
    layer1 = ['ffn_w_down1', 'ffn_w_up1', 'od_w_out', 'od_w_in']
    hsum1 = reduce_begin("layer1", layer1, [chips(g) for g in (gd1, gu1, g_od_out, g_od_in)])
    dh1, dh1b, gd0, gu0, gcw0, gcb0, gng0, r3 = ffn_bwd(0, dh2, dh2b, h1, ffn0, phase=_exchange_phase(hsum1))
    reduce_end("layer1", layer1, hsum1, r3)
    ffn0_names = ['ffn_w_down0', 'ffn_w_up0']
    hsum0 = reduce_begin("ffn0", ffn0_names, [chips(gd0), chips(gu0)])
    grads['ffn_conv_w'] = jnp.stack([gcw0, gcw1])
    grads['ffn_conv_b'] = jnp.stack([gcb0, gcb1])
    grads['ffn_norm_g'] = jnp.stack([gng0, gng1])
    grads['final_norm_g'] = d_final_g[0]

    dmix = _mm_nt("ev_out_bwd", dh1b, w_ev_out, tm)
    g_ev_out = _mm_tn("ev_out_dw", mixcat, dh1b, tt)
    duv, d_ws, d_bs, d_gv = _gmlp_bwd(p0, dmix, ws, bst, gv, SW)
    grads['gm_w_s'] = d_ws[None]
    grads['gm_b_s'] = d_bs[:, :, 0][None]
    grads['gm_v_g'] = d_gv
    dhr, dhi, du_skip, d_cbr, d_cbi, d_dsk, d_wglu, d_bglu = _s5_out_bwd(
        hr, hi, p03, dmix.reshape(B, S, D), cbr, cbi, dsk, w_glu, bglu, tm)
    grads['s5_c_re'] = jnp.swapaxes(_diag_blocks(d_cbr, G), 1, 2)[None]
    grads['s5_c_im'] = jnp.swapaxes(_diag_blocks(d_cbi, G), 1, 2)[None]
    grads['s5_d'] = d_dsk
    grads['s5_w_glu'] = d_wglu[None]
    grads['s5_b_glu'] = d_bglu
    (gr, gi, dar, dai), r3 = _s5_scan("s5_rscan", dhr, dhi, ar_s, ai_s, True, hr, hi, phase=_exchange_phase(hsum0))
    reduce_end("ffn0", ffn0_names, hsum0, r3)
    dp03, d_bbd = _s5_in_bwd(gr, gi, p03, bbd, du_skip, duv.reshape(B, S, -1), tm)
    d_bre, d_bim, d_zr, d_zi = _s5_bbd_bwd(d_bbd, zr_row, zi_row, bre, bim)
    grads['s5_b_re'] = jnp.swapaxes(_diag_blocks(d_bre, G), 1, 2)[None]
    grads['s5_b_im'] = jnp.swapaxes(_diag_blocks(d_bim, G), 1, 2)[None]
    shp = (-1, G, SSM_STATE)
    d_lr, d_li, d_ldt = _s5_prep_bwd(lr, li, ldt, dar.reshape(shp), dai.reshape(shp), d_zr.reshape(shp),
                                     d_zi.reshape(shp))
    grads['s5_lam_re'] = d_lr[None]
    grads['s5_lam_im'] = d_li[None]
    grads['s5_log_dt'] = d_ldt.reshape(1, G)
    dp0 = dp03.reshape(T, PW)
    g_ev_in = _mm_tn("ev_in_dw", dp0, y0, tt)
    grad_x, _, gmix0 = _mm_nt_normbwd("ev_in_bwd", [dp0], w_ev_in, h0, mix_norm_g[0], dh1, tm)
    grads['mix_norm_g'] = jnp.concatenate([gmix0, gmix1], axis=0)

    small = [n for n in WEIGHTS if n not in BIG]
    segs = []
    for n in small:
        gfull = grads[n].astype(f32)
        if n in SHARD_AXIS:
            segs.append(_split_chips(gfull, SHARD_AXIS[n]))
        else:
            segs.append(jnp.broadcast_to(gfull.reshape(1, -1), (N_CHIPS, gfull.size)))
    unit = 2 * SUBLANES * D
    gsmall = _pad_rows(jnp.concatenate(segs, axis=1), unit).reshape(N_CHIPS, -1, D)
    mixer0 = ['ev_w_out', 'ev_w_in', 'small']
    hsum = reduce_begin("mixer0", mixer0, [chips(g_ev_out), chips(g_ev_in), gsmall])
    reduce_end("mixer0", mixer0, hsum, _run_phase("comm_exchange_mixer0", _exchange_phase(hsum)))

    out_g, out_d, out_m, out_v = {}, {}, {}, {}

    def update(n, key, lead, transposed, prev=None):
        res = _adamw(f"adamw_{key}", *halves[key], c_idx, w[n], mom[n], var[n], lead, transposed, prev)
        out_g[n], out_d[n], out_m[n], out_v[n] = res
        return res

    update('ev_w_in', 'ev_w_in', 0, True)
    update('ev_w_out', 'ev_w_out', 0, False)
    update('od_w_in', 'od_w_in', 0, True)
    update('od_w_out', 'od_w_out', 0, False)
    update('ffn_w_up', 'ffn_w_up0', 0, True, prev=update('ffn_w_up', 'ffn_w_up1', 1, True))
    update('ffn_w_down', 'ffn_w_down0', 0, False, prev=update('ffn_w_down', 'ffn_w_down1', 1, False))

    def pack_local(d):
        flat = _pad_rows(jnp.concatenate([d[n].astype(f32).reshape(-1) for n in small]), unit)
        return flat.reshape(1, -1, D)

    res = _adamw("adamw_small", *halves['small'], c_idx, pack_local(w), pack_local(mom), pack_local(var), 0, False)
    for dst, p in zip((out_g, out_d, out_m, out_v), res):
        flat, off = p.reshape(-1), 0
        for n in small:
            dst[n] = flat[off:off + w[n].size].reshape(w[n].shape)
            off += w[n].size

    return (loss, grad_x.reshape(B, S, D), *[out_g[n] for n in WEIGHTS], *[out_d[n] for n in WEIGHTS],
            *[out_m[n] for n in WEIGHTS], *[out_v[n] for n in WEIGHTS])
```

```python
import functools
import math

import jax
import jax.numpy as jnp
from jax import lax
from jax.experimental import pallas as pl
from jax.experimental.pallas import tpu as pltpu

f32 = jnp.float32
bf16 = jnp.bfloat16
MESH = pl.DeviceIdType.MESH

SSM_GROUP = 16
SSM_STATE = 64
GMLP_HEAD = 128
CHUNK = 128
EPS = 1e-6
LAMBDA_RE_MAX = -1e-4
ADAM_LR, ADAM_B1, ADAM_B2, ADAM_EPS, ADAM_WD, ADAM_STEP = 0.001, 0.9, 0.999, 1e-08, 0.01, 10

LANES = 128
SUBLANES = 8
NSUB = 32
HALO = 16
VMEM_LIMIT = 56 * 1024 * 1024
N_CHIPS = 4

WEIGHTS = ['mix_norm_g', 'ffn_norm_g', 'final_norm_g', 'ev_w_in', 'ev_w_out', 's5_lam_re', 's5_lam_im', 's5_log_dt',
           's5_b_re', 's5_b_im', 's5_c_re', 's5_c_im', 's5_d', 's5_w_glu', 's5_b_glu', 'gm_w_s', 'gm_b_s', 'gm_v_g',
           'od_w_in', 'od_conv_w', 'od_conv_b', 'od_w_out', 'ffn_w_up', 'ffn_conv_w', 'ffn_conv_b', 'ffn_w_down']
SHARD_AXIS = {'ev_w_in': 2, 'ev_w_out': 1, 's5_w_glu': 1, 'od_w_in': 2, 'od_conv_w': 2, 'od_conv_b': 1, 'od_w_out': 1,
              'ffn_w_up': 2, 'ffn_conv_w': 2, 'ffn_w_down': 1}
GATHER_BF16 = ['ev_w_in', 'ev_w_out', 's5_w_glu', 'od_w_in', 'od_w_out', 'ffn_w_up', 'ffn_w_down']
GATHER_F32 = ['od_conv_w', 'od_conv_b', 'ffn_conv_w']

_GELU_K0 = math.sqrt(2.0 / math.pi)
_GELU_K1 = 0.044715
NT = (((1,), (1,)), ((), ()))
TN = (((0,), (0,)), ((), ()))


def _pick(n, cap):
    if n <= cap:
        return n
    best = None
    for d in range(LANES, cap + 1, LANES):
        if n % d == 0:
            best = d
    assert best is not None, (n, cap)
    return best


def _params(sem=None):
    return pltpu.CompilerParams(dimension_semantics=sem, vmem_limit_bytes=VMEM_LIMIT)


class _Phase:
    def __init__(self, ins, inplace, outs, sems, start, finish):
        self.ins, self.inplace, self.outs, self.sems = list(ins), list(inplace), list(outs), list(sems)
        self.start, self.finish = start, finish


def _call(body, name, grid, in_specs, out_specs, out_shape, args, scratch=(), sem=None, phase=None):
    if phase is None:
        return pl.pallas_call(body, name=name, grid=grid, in_specs=in_specs, out_specs=out_specs, out_shape=out_shape,
                              scratch_shapes=list(scratch), compiler_params=_params(sem))(*args)
    any_spec = pl.BlockSpec(memory_space=pl.ANY)
    n_in, n_out, n_scr = len(args), len(out_shape), len(scratch)
    p_in = phase.ins + phase.inplace
    ci, co = len(p_in), len(phase.outs)

    def wrapped(*refs):
        ins, cins = refs[:n_in], refs[n_in:n_in + len(phase.ins)]
        b = n_in + ci
        outs, couts = refs[b:b + n_out], refs[b + n_out:b + n_out + co]
        d = b + n_out + co
        scr, csem = refs[d:d + n_scr], refs[d + n_scr:]
        ids = [pl.program_id(i) for i in range(len(grid))]
        first = functools.reduce(jnp.logical_and, [i == 0 for i in ids])
        last = functools.reduce(jnp.logical_and, [i == g - 1 for i, g in zip(ids, grid)])

        @pl.when(first)
        def _():
            phase.start(cins, couts, csem)
        body(*ins, *outs, *scr)

        @pl.when(last)
        def _():
            phase.finish(cins, couts, csem)

    res = pl.pallas_call(
        wrapped, name=name, grid=grid, in_specs=list(in_specs) + [any_spec] * ci,
        out_specs=list(out_specs) + [any_spec] * co, out_shape=list(out_shape) + phase.outs,
        scratch_shapes=list(scratch) + phase.sems,
        input_output_aliases={n_in + len(phase.ins) + i: n_out + i for i in range(len(phase.inplace))},
        compiler_params=_params(tuple("arbitrary" for _ in grid)))(*args, *p_in)
    return res[:n_out], res[n_out:]


def _run_phase(name, phase):
    any_spec = pl.BlockSpec(memory_space=pl.ANY)
    ni, ci, co = len(phase.ins), len(phase.ins) + len(phase.inplace), len(phase.outs)

    def body(*refs):
        cins, couts, csem = refs[:ni], refs[ci:ci + co], refs[ci + co:]
        phase.start(cins, couts, csem)
        phase.finish(cins, couts, csem)

    return pl.pallas_call(
        body, name=name, in_specs=[any_spec] * ci, out_specs=[any_spec] * co, out_shape=phase.outs,
        scratch_shapes=phase.sems, input_output_aliases={ni + i: i for i in range(len(phase.inplace))})(
            *phase.ins, *phase.inplace)


def _gelu(x):
    return 0.5 * x * (1.0 + jnp.tanh(_GELU_K0 * (x + _GELU_K1 * x * x * x)))


def _gelu_grad(x):
    t = jnp.tanh(_GELU_K0 * (x + _GELU_K1 * x * x * x))
    return 0.5 * (1.0 + t) + 0.5 * x * (1.0 - t * t) * _GELU_K0 * (1.0 + 3.0 * _GELU_K1 * x * x)


def _rms_stats(x):
    r = lax.rsqrt(jnp.mean(x * x, axis=-1, keepdims=True) + EPS)
    return x * r, r


def _rms_bwd(dy, xh, r, g):
    dxh = dy * g
    dx = r * (dxh - xh * jnp.mean(dxh * xh, axis=-1, keepdims=True))
    return dx, jnp.sum(dy * xh, axis=0, keepdims=True)


def _dot(a, b):
    return jnp.dot(a, b, preferred_element_type=f32)


def _dg(a, b, dims):
    return lax.dot_general(a, b, dims, preferred_element_type=f32)


def _row_fold(z):
    return z.reshape(z.shape[0] // SUBLANES, SUBLANES, z.shape[1]).sum(axis=0)


def _norm_mm(name, h, g, w, tm, phase=None):
    T, D = h.shape
    N = w.shape[1]
    nc = _pick(N, 512)

    def body(h_ref, g_ref, w_ref, y_ref, o_ref):
        xh, _ = _rms_stats(h_ref[...])
        y = (xh * g_ref[...]).astype(bf16)
        y_ref[...] = y
        for j in range(N // nc):
            o_ref[:, j * nc:(j + 1) * nc] = _dot(y, w_ref[:, j * nc:(j + 1) * nc]).astype(bf16)

    return _call(
        body, name, (T // tm,),
        [pl.BlockSpec((tm, D), lambda i: (i, 0)), pl.BlockSpec((1, D), lambda i: (0, 0)),
         pl.BlockSpec((D, N), lambda i: (0, 0))],
        [pl.BlockSpec((tm, D), lambda i: (i, 0)), pl.BlockSpec((tm, N), lambda i: (i, 0))],
        [jax.ShapeDtypeStruct((T, D), bf16), jax.ShapeDtypeStruct((T, N), bf16)],
        [h, g.reshape(1, D), w], sem=("parallel",), phase=phase)


def _mm_resid(name, a, w, resid, tm):
    T, K = a.shape
    N = w.shape[1]

    def body(a_ref, w_ref, r_ref, o_ref):
        o_ref[...] = r_ref[...] + _dot(a_ref[...], w_ref[...])

    return pl.pallas_call(
        body, name=name, grid=(T // tm,),
        in_specs=[pl.BlockSpec((tm, K), lambda i: (i, 0)), pl.BlockSpec((K, N), lambda i: (0, 0)),
                  pl.BlockSpec((tm, N), lambda i: (i, 0))],
        out_specs=pl.BlockSpec((tm, N), lambda i: (i, 0)),
        out_shape=jax.ShapeDtypeStruct((T, N), f32),
        compiler_params=_params(("parallel",)))(a, w, resid)


def _mm_nt(name, dy, w, tm):
    T, N = dy.shape
    K = w.shape[0]
    kc = _pick(K, 512)

    def body(d_ref, w_ref, o_ref):
        d = d_ref[...].astype(bf16)
        for j in range(K // kc):
            o_ref[:, j * kc:(j + 1) * kc] = _dg(d, w_ref[j * kc:(j + 1) * kc, :], NT).astype(bf16)

    return pl.pallas_call(
        body, name=name, grid=(T // tm,),
        in_specs=[pl.BlockSpec((tm, N), lambda i: (i, 0)), pl.BlockSpec((K, N), lambda i: (0, 0))],
        out_specs=pl.BlockSpec((tm, K), lambda i: (i, 0)),
        out_shape=jax.ShapeDtypeStruct((T, K), bf16),
        compiler_params=_params(("parallel",)))(dy, w)


def _mm_nt_normbwd(name, dys, w, h, g, dh_in, tm, phase=None):
    n = len(dys)
    T = dys[0].shape[0]
    D = w.shape[0]
    widths = [d.shape[1] for d in dys]
    offs = [sum(widths[:i]) for i in range(n)]

    def body(*refs):
        d_refs = refs[:n]
        w_ref, h_ref, g_ref, dh_ref, o_ref, ob_ref, dg_ref = refs[n:]
        dz = _dg(d_refs[0][...], w_ref[:, :widths[0]], NT)
        for i in range(1, n):
            dz += _dg(d_refs[i][...], w_ref[:, offs[i]:offs[i] + widths[i]], NT)
        xh, r = _rms_stats(h_ref[...])
        dx, dg = _rms_bwd(dz, xh, r, g_ref[...])
        out = dh_ref[...] + dx
        o_ref[...] = out
        ob_ref[...] = out.astype(bf16)

        @pl.when(pl.program_id(0) == 0)
        def _():
            dg_ref[...] = jnp.zeros_like(dg_ref)
        dg_ref[...] += dg

    row = lambda c: pl.BlockSpec((tm, c), lambda i: (i, 0))
    return _call(
        body, name, (T // tm,),
        [row(c) for c in widths] + [pl.BlockSpec((D, sum(widths)), lambda i: (0, 0)), row(D),
                                    pl.BlockSpec((1, D), lambda i: (0, 0)), row(D)],
        [row(D), row(D), pl.BlockSpec((1, D), lambda i: (0, 0))],
        [jax.ShapeDtypeStruct((T, D), f32), jax.ShapeDtypeStruct((T, D), bf16), jax.ShapeDtypeStruct((1, D), f32)],
        [*dys, w, h, g.reshape(1, D), dh_in], sem=("arbitrary",), phase=phase)


def _mm_tn(name, a, b, tt, rows=None, row_off=0, prev=None):
    T, K = a.shape
    N = b.shape[1]
    rows = K if rows is None else rows
    tk = _pick(K, 1408)
    tn = _pick(N, 1024)
    assert row_off % tk == 0
    kb = row_off // tk

    def body(a_ref, b_ref, *rest):
        o_ref = rest[-1]

        @pl.when(pl.program_id(2) == 0)
        def _():
            o_ref[...] = jnp.zeros_like(o_ref)
        o_ref[...] += _dg(a_ref[...], b_ref[...], TN)

    in_specs = [pl.BlockSpec((tt, tk), lambda k, n, t: (t, k)), pl.BlockSpec((tt, tn), lambda k, n, t: (t, n))]
    args, aliases = [a, b], {}
    if prev is not None:
        in_specs.append(ANY)
        args.append(prev)
        aliases = {2: 0}
    return pl.pallas_call(
        body, name=name, grid=(K // tk, N // tn, T // tt), in_specs=in_specs,
        out_specs=pl.BlockSpec((tk, tn), lambda k, n, t: (k + kb, n)),
        out_shape=jax.ShapeDtypeStruct((rows, N), f32), input_output_aliases=aliases,
        compiler_params=_params(("parallel", "parallel", "arbitrary")))(*args)


def _final_loss(h, g, tgt, tm):
    T, D = h.shape

    def body(h_ref, g_ref, t_ref, dh_ref, dhb_ref, loss_ref, dg_ref):
        xh, r = _rms_stats(h_ref[...])
        gg = g_ref[...]
        diff = xh * gg - t_ref[...]
        dy = diff * (1.0 / D)
        dx, dg = _rms_bwd(dy, xh, r, gg)
        dh_ref[...] = dx
        dhb_ref[...] = dx.astype(bf16)

        @pl.when(pl.program_id(0) == 0)
        def _():
            dg_ref[...] = jnp.zeros_like(dg_ref)
            loss_ref[...] = jnp.zeros_like(loss_ref)
        dg_ref[...] += dg
        loss_ref[...] += (0.5 / D) * jnp.sum(jnp.sum(diff * diff, axis=1, keepdims=True), axis=0, keepdims=True)

    return pl.pallas_call(
        body, name="final_loss", grid=(T // tm,),
        in_specs=[pl.BlockSpec((tm, D), lambda i: (i, 0)), pl.BlockSpec((1, D), lambda i: (0, 0)),
                  pl.BlockSpec((tm, D), lambda i: (i, 0))],
        out_specs=[pl.BlockSpec((tm, D), lambda i: (i, 0)), pl.BlockSpec((tm, D), lambda i: (i, 0)),
                   pl.BlockSpec((1, 1), lambda i: (0, 0)), pl.BlockSpec((1, D), lambda i: (0, 0))],
        out_shape=[jax.ShapeDtypeStruct((T, D), f32), jax.ShapeDtypeStruct((T, D), bf16),
                   jax.ShapeDtypeStruct((1, 1), f32), jax.ShapeDtypeStruct((1, D), f32)],
        compiler_params=_params(("arbitrary",)))(h, g.reshape(1, D), tgt)


def _taps(load, r0, R):
    main = load(r0, R)
    hs = pl.multiple_of(jnp.maximum(r0 - HALO, 0), HALO)
    halo = load(hs, HALO) * (r0 > 0).astype(f32)
    ext = jnp.concatenate([halo, main], axis=0)
    xm1 = pltpu.roll(ext, 1, 0)[HALO:]
    xm2 = pltpu.roll(ext, 2, 0)[HALO:]
    return xm2, xm1, main


def _conv(w, b, taps):
    return b + w[0:1] * taps[0] + w[1:2] * taps[1] + w[2:3] * taps[2]


def _ref_load(ref):
    return lambda s, n: ref[pl.ds(s, n), :].astype(f32)


def _ffn_down(name, up, cw, cb, w_down, resid, S, tm, phase=None):
    T, F2 = up.shape
    F = F2 // 2
    D = w_down.shape[1]
    cwid = _pick(F, 256)
    per_seq = S // tm

    def body(u_ref, halo_ref, cw_ref, cb_ref, w_ref, r_ref, o_ref):
        keep = (pl.program_id(0) % per_seq > 0).astype(f32)

        def conv(off):
            cols = slice(off, off + cwid)
            main = u_ref[:, cols].astype(f32)
            ext = jnp.concatenate([halo_ref[:, cols].astype(f32) * keep, main], axis=0)
            taps = (pltpu.roll(ext, 2, 0)[HALO:], pltpu.roll(ext, 1, 0)[HALO:], main)
            return _conv(cw_ref[:, cols], cb_ref[:, cols], taps)

        acc = r_ref[...]
        for j in range(F // cwid):
            cg, cv = conv(j * cwid), conv(F + j * cwid)
            a = (cg * jax.nn.sigmoid(cg) * cv).astype(bf16)
            acc = acc + _dot(a, w_ref[j * cwid:(j + 1) * cwid, :])
        o_ref[...] = acc

    full = lambda r, c: pl.BlockSpec((r, c), lambda i: (0, 0))
    return _call(
        body, name, (T // tm,),
        [pl.BlockSpec((tm, F2), lambda i: (i, 0)),
         pl.BlockSpec((HALO, F2), lambda i: (jnp.maximum(i * (tm // HALO) - 1, 0), 0)),
         full(3, F2), full(1, F2), full(F, D), pl.BlockSpec((tm, D), lambda i: (i, 0))],
        [pl.BlockSpec((tm, D), lambda i: (i, 0))], [jax.ShapeDtypeStruct((T, D), f32)],
        [up, up, cw, cb, w_down, resid], sem=("parallel",), phase=phase)


def _rev_conv_rows(d, nxt, w):
    R = d.shape[0]
    ext = jnp.concatenate([d, nxt], axis=0)
    n = R + HALO
    xp1 = pltpu.roll(ext, n - 1, 0)[:R]
    xp2 = pltpu.roll(ext, n - 2, 0)[:R]
    return w[2:3] * d + w[1:2] * xp1 + w[0:1] * xp2


def _conv_grad_acc(acc, dc, taps):
    return (acc[0] + _row_fold(dc * taps[0]), acc[1] + _row_fold(dc * taps[1]), acc[2] + _row_fold(dc * taps[2]),
            acc[3] + _row_fold(dc))


def _conv_grad_out(dcw_ref, dcb_ref, acc):
    @pl.when(pl.program_id(1) == 0)
    def _():
        dcw_ref[...] = jnp.zeros_like(dcw_ref)
        dcb_ref[...] = jnp.zeros_like(dcb_ref)
    for k in range(3):
        dcw_ref[k:k + 1, :] += jnp.sum(acc[k], axis=0, keepdims=True)
    dcb_ref[...] += jnp.sum(acc[3], axis=0, keepdims=True)


def _ffn_act_bwd(name, up3, da3, cw, cb, phase=None):
    B, S, F2 = up3.shape
    F = F2 // 2
    cwid = _pick(F, 256)
    nF = F // cwid
    R = min(256, S)
    nR = S // R

    def body(g_ref, v_ref, da_ref, wg_ref, wv_ref, bg_ref, bv_ref,
             act_ref, dg_ref, dv_ref, dcwg_ref, dcwv_ref, dcbg_ref, dcbv_ref, sum_scr):
        wg, wv, bg, bv = wg_ref[...], wv_ref[...], bg_ref[...], bv_ref[...]

        def step(i, carry):
            ng, nv, accg, accv = carry
            r0 = pl.multiple_of((nR - 1 - i) * R, R)
            tg = _taps(_ref_load(g_ref), r0, R)
            tv = _taps(_ref_load(v_ref), r0, R)
            cg = _conv(wg, bg, tg)
            cv = _conv(wv, bv, tv)
            da = da_ref[pl.ds(r0, R), :].astype(f32)
            sg = jax.nn.sigmoid(cg)
            act_ref[pl.ds(r0, R), :] = (cg * sg * cv).astype(bf16)
            dgate = da * cv * (sg * (1.0 + cg * (1.0 - sg)))
            dval = da * (cg * sg)
            dg_ref[pl.ds(r0, R), :] = _rev_conv_rows(dgate, ng, wg).astype(bf16)
            dv_ref[pl.ds(r0, R), :] = _rev_conv_rows(dval, nv, wv).astype(bf16)
            return dgate[:HALO], dval[:HALO], _conv_grad_acc(accg, dgate, tg), _conv_grad_acc(accv, dval, tv)
        z = jnp.zeros((SUBLANES, cwid), f32)
        zh = jnp.zeros((HALO, cwid), f32)
        _, _, accg, accv = lax.fori_loop(0, nR, step, (zh, zh, (z, z, z, z), (z, z, z, z)))
        j = pl.program_id(1)
        for half_i, (acc, dcw_ref, dcb_ref) in enumerate(((accg, dcwg_ref, dcbg_ref), (accv, dcwv_ref, dcbv_ref))):
            @pl.when(pl.program_id(0) == 0)
            def _():
                sum_scr[half_i, j] = jnp.zeros((SUBLANES, cwid), f32)
            for k in range(4):
                sum_scr[half_i, j, k:k + 1, :] += jnp.sum(acc[k], axis=0, keepdims=True)
            dcw_ref[...] = sum_scr[half_i, j, 0:3, :]
            dcb_ref[...] = sum_scr[half_i, j, 3:4, :]

    blk = lambda off: pl.BlockSpec((None, S, cwid), lambda b, j: (b, 0, off + j))
    wblk = lambda off: pl.BlockSpec((3, cwid), lambda b, j: (0, off + j))
    bblk = lambda off: pl.BlockSpec((1, cwid), lambda b, j: (0, off + j))
    sums = lambda r: pl.BlockSpec((r, cwid), lambda b, j: (0, jnp.where(b == B - 1, j, nF)))
    half = jax.ShapeDtypeStruct((B, S, F), bf16)
    return _call(
        body, name, (B, nF),
        [blk(0), blk(nF), blk(0), wblk(0), wblk(nF), bblk(0), bblk(nF)],
        [blk(0), blk(0), blk(0), sums(3), sums(3), sums(1), sums(1)],
        [half, half, half, jax.ShapeDtypeStruct((3, F + cwid), f32), jax.ShapeDtypeStruct((3, F + cwid), f32),
         jax.ShapeDtypeStruct((1, F + cwid), f32), jax.ShapeDtypeStruct((1, F + cwid), f32)],
        [up3, up3, da3, cw, cw, cb, cb], scratch=[pltpu.VMEM((2, nF, SUBLANES, cwid), f32)],
        sem=("arbitrary", "arbitrary"), phase=phase)


def _od_act(p3, cw, cb):
    B, S, D3 = p3.shape
    D = D3 // 3
    cwid = _pick(D, 256)
    nD = D // cwid
    R = min(256, S)

    def body(bg_ref, cg_ref, hx_ref, w_ref, b_ref, o_ref):
        w, b = w_ref[...], b_ref[...]
        q = lambda s, n: cg_ref[pl.ds(s, n), :].astype(f32) * hx_ref[pl.ds(s, n), :].astype(f32)

        def chunk(r, c):
            r0 = pl.multiple_of(r * R, R)
            cq = _conv(w, b, _taps(q, r0, R))
            o_ref[pl.ds(r0, R), :] = (bg_ref[pl.ds(r0, R), :].astype(f32) * cq).astype(bf16)
            return c
        lax.fori_loop(0, S // R, chunk, 0)

    blk = lambda off: pl.BlockSpec((None, S, cwid), lambda b, j: (b, 0, off + j))
    return pl.pallas_call(
        body, name="od_act", grid=(B, nD),
        in_specs=[blk(0), blk(nD), blk(2 * nD), pl.BlockSpec((3, cwid), lambda b, j: (0, j)),
                  pl.BlockSpec((1, cwid), lambda b, j: (0, j))],
        out_specs=pl.BlockSpec((None, S, cwid), lambda b, j: (b, 0, j)),
        out_shape=jax.ShapeDtypeStruct((B, S, D), bf16),
        compiler_params=_params(("parallel", "parallel")))(p3, p3, p3, cw, cb)


def _od_act_bwd(p3, dsc3, cw, cb):
    B, S, D3 = p3.shape
    D = D3 // 3
    cwid = _pick(D, 256)
    nD = D // cwid
    R = min(256, S)
    nR = S // R

    def body(bg_ref, cg_ref, hx_ref, d_ref, w_ref, b_ref, dbg_ref, dcg_ref, dhx_ref, dcw_ref, dcb_ref):
        w, b = w_ref[...], b_ref[...]
        q = lambda s, n: cg_ref[pl.ds(s, n), :].astype(f32) * hx_ref[pl.ds(s, n), :].astype(f32)

        def step(i, carry):
            nxt, acc = carry
            r0 = pl.multiple_of((nR - 1 - i) * R, R)
            rows = pl.ds(r0, R)
            tq = _taps(q, r0, R)
            cq = _conv(w, b, tq)
            d = d_ref[rows, :].astype(f32)
            dbg_ref[rows, :] = (d * cq).astype(bf16)
            dcq = d * bg_ref[rows, :].astype(f32)
            dq = _rev_conv_rows(dcq, nxt, w)
            dcg_ref[rows, :] = (dq * hx_ref[rows, :].astype(f32)).astype(bf16)
            dhx_ref[rows, :] = (dq * cg_ref[rows, :].astype(f32)).astype(bf16)
            return dcq[:HALO], _conv_grad_acc(acc, dcq, tq)
        z = jnp.zeros((SUBLANES, cwid), f32)
        _, acc = lax.fori_loop(0, nR, step, (jnp.zeros((HALO, cwid), f32), (z, z, z, z)))
        _conv_grad_out(dcw_ref, dcb_ref, acc)

    blk = lambda off: pl.BlockSpec((None, S, cwid), lambda j, b: (b, 0, off + j))
    part = jax.ShapeDtypeStruct((B, S, D), bf16)
    return pl.pallas_call(
        body, name="od_act_bwd", grid=(nD, B),
        in_specs=[blk(0), blk(nD), blk(2 * nD), blk(0), pl.BlockSpec((3, cwid), lambda j, b: (0, j)),
                  pl.BlockSpec((1, cwid), lambda j, b: (0, j))],
        out_specs=[blk(0), blk(0), blk(0), pl.BlockSpec((3, cwid), lambda j, b: (0, j)),
                   pl.BlockSpec((1, cwid), lambda j, b: (0, j))],
        out_shape=[part, part, part, jax.ShapeDtypeStruct((3, D), f32), jax.ShapeDtypeStruct((1, D), f32)],
        compiler_params=_params(("parallel", "arbitrary")))(p3, p3, p3, dsc3, cw, cb)


def _gmlp_parts(p, gv, SW, GW):
    uv = p[:, SW:].astype(f32)
    ge = _gelu(uv)
    u, v = ge[:, :GW], ge[:, GW:]
    vh, r = _rms_stats(v)
    return uv, u, vh, r, vh * gv


def _tril():
    rows = lax.broadcasted_iota(jnp.int32, (CHUNK, CHUNK), 0)
    cols = lax.broadcasted_iota(jnp.int32, (CHUNK, CHUNK), 1)
    return rows >= cols


def _gmlp(p0, a_out, ws, bst, gv, SW):
    T, PW = p0.shape
    GW = (PW - SW) // 2
    H = GW // GMLP_HEAD
    D = SW + GW

    def body(p_ref, a_ref, ws_ref, b_ref, gv_ref, o_ref):
        _, u, _, _, vn = _gmlp_parts(p_ref[...], gv_ref[...], SW, GW)
        tri = _tril()
        o_ref[:, :SW] = a_ref[...]
        for hh in range(H):
            sl = slice(hh * GMLP_HEAD, (hh + 1) * GMLP_HEAD)
            wm = jnp.where(tri, ws_ref[hh], 0.0).astype(bf16)
            gate = _dot(wm, vn[:, sl].astype(bf16)) + b_ref[:, hh:hh + 1]
            o_ref[:, SW + hh * GMLP_HEAD:SW + (hh + 1) * GMLP_HEAD] = (u[:, sl] * gate).astype(bf16)

    return pl.pallas_call(
        body, name="gmlp", grid=(T // CHUNK,),
        in_specs=[pl.BlockSpec((CHUNK, PW), lambda i: (i, 0)), pl.BlockSpec((CHUNK, SW), lambda i: (i, 0)),
                  pl.BlockSpec((H, CHUNK, CHUNK), lambda i: (0, 0, 0)), pl.BlockSpec((CHUNK, H), lambda i: (0, 0)),
                  pl.BlockSpec((1, GW), lambda i: (0, 0))],
        out_specs=pl.BlockSpec((CHUNK, D), lambda i: (i, 0)),
        out_shape=jax.ShapeDtypeStruct((T, D), bf16),
        compiler_params=_params(("parallel",)))(p0, a_out, ws, bst, gv)


def _gmlp_bwd(p0, dmix, ws, bst, gv, SW):
    T, PW = p0.shape
    GW = (PW - SW) // 2
    H = GW // GMLP_HEAD
    D = SW + GW

    def body(p_ref, d_ref, ws_ref, b_ref, gv_ref, duv_ref, dws_ref, dbs_ref, dgv_ref):
        gv_ = gv_ref[...]
        uv, u, vh, r, vn = _gmlp_parts(p_ref[...], gv_, SW, GW)
        dout = d_ref[...][:, SW:].astype(f32)
        tri = _tril()

        @pl.when(pl.program_id(0) == 0)
        def _():
            dws_ref[...] = jnp.zeros_like(dws_ref)
            dbs_ref[...] = jnp.zeros_like(dbs_ref)
            dgv_ref[...] = jnp.zeros_like(dgv_ref)
        du, dvn = [], []
        for hh in range(H):
            sl = slice(hh * GMLP_HEAD, (hh + 1) * GMLP_HEAD)
            wm = jnp.where(tri, ws_ref[hh], 0.0).astype(bf16)
            vnh = vn[:, sl].astype(bf16)
            gate = _dot(wm, vnh) + b_ref[:, hh:hh + 1]
            dgate = dout[:, sl] * u[:, sl]
            du.append(dout[:, sl] * gate)
            dgb = dgate.astype(bf16)
            dws_ref[hh] += jnp.where(tri, _dg(dgb, vnh, NT), 0.0)
            dbs_ref[hh] += jnp.broadcast_to(jnp.sum(dgate, axis=1, keepdims=True), (CHUNK, CHUNK))
            dvn.append(_dg(wm, dgb, TN))
        dvn = jnp.concatenate(dvn, axis=1)
        dv, dgv = _rms_bwd(dvn, vh, r, gv_)
        dgv_ref[...] += dgv
        dge = jnp.concatenate(du + [dv], axis=1)
        duv_ref[...] = (dge * _gelu_grad(uv)).astype(bf16)

    return pl.pallas_call(
        body, name="gmlp_bwd", grid=(T // CHUNK,),
        in_specs=[pl.BlockSpec((CHUNK, PW), lambda i: (i, 0)), pl.BlockSpec((CHUNK, D), lambda i: (i, 0)),
                  pl.BlockSpec((H, CHUNK, CHUNK), lambda i: (0, 0, 0)), pl.BlockSpec((CHUNK, H), lambda i: (0, 0)),
                  pl.BlockSpec((1, GW), lambda i: (0, 0))],
        out_specs=[pl.BlockSpec((CHUNK, 2 * GW), lambda i: (i, 0)), pl.BlockSpec((H, CHUNK, CHUNK), lambda i: (0, 0, 0)),
                   pl.BlockSpec((H, CHUNK, CHUNK), lambda i: (0, 0, 0)), pl.BlockSpec((1, GW), lambda i: (0, 0))],
        out_shape=[jax.ShapeDtypeStruct((T, 2 * GW), bf16), jax.ShapeDtypeStruct((H, CHUNK, CHUNK), f32),
                   jax.ShapeDtypeStruct((H, CHUNK, CHUNK), f32), jax.ShapeDtypeStruct((1, GW), f32)],
        compiler_params=_params(("arbitrary",)))(p0, dmix, ws, bst, gv)


def _s5_disc(lr, li, ldt):
    lr = jnp.minimum(lr, LAMBDA_RE_MAX)
    dt = jnp.exp(ldt)
    mag = jnp.exp(lr * dt)
    ar = mag * jnp.cos(li * dt)
    ai = mag * jnp.sin(li * dt)
    den = lr * lr + li * li
    nr = ar - 1.0
    zr = (nr * lr + ai * li) / den
    zi = (ai * lr - nr * li) / den
    return ar, ai, zr, zi


def _s5_prep(lr, li, ldt):
    G, P = lr.shape

    def body(lr_ref, li_ref, ldt_ref, ar_ref, ai_ref, zr_ref, zi_ref):
        ar, ai, zr, zi = _s5_disc(lr_ref[...], li_ref[...], ldt_ref[...])
        ar_ref[...] = ar
        ai_ref[...] = ai
        zr_ref[...] = zr
        zi_ref[...] = zi

    s = jax.ShapeDtypeStruct((G, P), f32)
    return pl.pallas_call(body, name="s5_prep", out_shape=[s, s, s, s])(lr, li, ldt)


def _s5_prep_bwd(lr, li, ldt, dar, dai, dzr, dzi):
    G, P = lr.shape

    def body(lr_ref, li_ref, ldt_ref, dar_ref, dai_ref, dzr_ref, dzi_ref, o1, o2, o3):
        _, vjp = jax.vjp(_s5_disc, lr_ref[...], li_ref[...], ldt_ref[...])
        cts = tuple(jnp.sum(r[...], axis=0) for r in (dar_ref, dai_ref, dzr_ref, dzi_ref))
        a, b, c = vjp(cts)
        o1[...] = a
        o2[...] = b
        o3[...] = c

    s = jax.ShapeDtypeStruct((G, P), f32)
    return pl.pallas_call(body, name="s5_prep_bwd", out_shape=[s, s, jax.ShapeDtypeStruct((G, 1), f32)])(
        lr, li, ldt, dar, dai, dzr, dzi)


def _s5_bbd(zr, zi, bre, bim):
    SW, NS = bre.shape

    def body(zr_ref, zi_ref, br_ref, bi_ref, o_ref):
        zr_, zi_, br, bi = zr_ref[...], zi_ref[...], br_ref[...], bi_ref[...]
        o_ref[:, :NS] = (zr_ * br - zi_ * bi).astype(bf16)
        o_ref[:, NS:] = (zr_ * bi + zi_ * br).astype(bf16)

    return pl.pallas_call(body, name="s5_bbd", out_shape=jax.ShapeDtypeStruct((SW, 2 * NS), bf16))(zr, zi, bre, bim)


def _s5_bbd_bwd(dbbd, zr, zi, bre, bim):
    SW, NS = bre.shape

    def body(d_ref, zr_ref, zi_ref, br_ref, bi_ref, dbr_ref, dbi_ref, dzr_ref, dzi_ref):
        zr_, zi_, br, bi = zr_ref[...], zi_ref[...], br_ref[...], bi_ref[...]
        dr, di = d_ref[:, :NS], d_ref[:, NS:]
        dbr_ref[...] = zr_ * dr + zi_ * di
        dbi_ref[...] = zr_ * di - zi_ * dr
        dzr_ref[...] = jnp.sum(dr * br + di * bi, axis=0, keepdims=True)
        dzi_ref[...] = jnp.sum(di * br - dr * bi, axis=0, keepdims=True)

    m = jax.ShapeDtypeStruct((SW, NS), f32)
    v = jax.ShapeDtypeStruct((1, NS), f32)
    return pl.pallas_call(body, name="s5_bbd_bwd", out_shape=[m, m, v, v])(dbbd, zr, zi, bre, bim)


def _slab_cat(ref, NB):
    return jnp.concatenate([ref[j] for j in range(NB)], axis=1)


def _s5_in(p3, bbd, SW, tm):
    B, S, PW = p3.shape
    NS = bbd.shape[1] // 2
    NB = NS // LANES

    def body(u_ref, b_ref, xr_ref, xi_ref):
        x = _dot(u_ref[...], b_ref[...])
        for j in range(NB):
            xr_ref[j] = x[:, j * LANES:(j + 1) * LANES]
            xi_ref[j] = x[:, NS + j * LANES:NS + (j + 1) * LANES]

    slab = jax.ShapeDtypeStruct((B, NB, S, LANES), f32)
    sspec = pl.BlockSpec((None, NB, tm, LANES), lambda b, i: (b, 0, i, 0))
    return pl.pallas_call(
        body, name="s5_in", grid=(B, S // tm),
        in_specs=[pl.BlockSpec((None, tm, SW), lambda b, i: (b, i, 0)), pl.BlockSpec((SW, 2 * NS), lambda b, i: (0, 0))],
        out_specs=[sspec, sspec], out_shape=[slab, slab],
        compiler_params=_params(("parallel", "parallel")))(p3, bbd)


def _s5_scan(name, xr, xi, ar, ai, reverse, hr=None, hi=None, phase=None):
    B, NB, S, _ = xr.shape
    L = S // NSUB
    nb = 2 if (hr is None and NB % 2 == 0) else 1
    with_da = hr is not None

    def body(*refs):
        if with_da:
            xr_ref, xi_ref, ar_ref, ai_ref, hr_ref, hi_ref, or_ref, oi_ref, dar_ref, dai_ref, pr_scr, pi_scr = refs
        else:
            xr_ref, xi_ref, ar_ref, ai_ref, or_ref, oi_ref, pr_scr, pi_scr = refs
        sign = -1.0 if reverse else 1.0
        a_r = [jnp.broadcast_to(ar_ref[j], (NSUB, LANES)) for j in range(nb)]
        a_i = [jnp.broadcast_to(ai_ref[j], (NSUB, LANES)) * sign for j in range(nb)]

        def step(t, carry):
            row = (L - 1 - t) if reverse else t
            rows = pl.ds(row, NSUB, stride=L)
            out = []
            for j in range(nb):
                sr, si, pr, pi = carry[j]
                nr = a_r[j] * sr - a_i[j] * si + xr_ref.at[j][rows, :]
                ni = a_r[j] * si + a_i[j] * sr + xi_ref.at[j][rows, :]
                or_ref.at[j][rows, :] = nr
                oi_ref.at[j][rows, :] = ni
                npr = a_r[j] * pr - a_i[j] * pi
                npi = a_r[j] * pi + a_i[j] * pr
                pr_scr[j, pl.ds(row, 1), :] = npr[0:1]
                pi_scr[j, pl.ds(row, 1), :] = npi[0:1]
                out.append((nr, ni, npr, npi))
            return tuple(out)
        z = jnp.zeros((NSUB, LANES), f32)
        one = jnp.ones((NSUB, LANES), f32)
        fin = lax.fori_loop(0, L, step, tuple((z, z, one, z) for _ in range(nb)))

        for j in range(nb):
            sr, si, plr, pli = fin[j]
            plr, pli = plr[0:1], pli[0:1]
            cr = jnp.zeros((1, LANES), f32)
            ci = jnp.zeros((1, LANES), f32)
            order = range(NSUB - 2, -1, -1) if reverse else range(1, NSUB)
            for c in order:
                src = c + 1 if reverse else c - 1
                cr, ci = (sr[src:src + 1] + plr * cr - pli * ci, si[src:src + 1] + plr * ci + pli * cr)
                rows = slice(c * L, (c + 1) * L)
                tr, ti = pr_scr[j], pi_scr[j]
                or_ref[j, rows, :] += tr * cr - ti * ci
                oi_ref[j, rows, :] += tr * ci + ti * cr
            if with_da:
                first = lax.broadcasted_iota(jnp.int32, (L, LANES), 0) == 0
                dar = jnp.zeros((1, LANES), f32)
                dai = jnp.zeros((1, LANES), f32)
                for c in range(NSUB):
                    rows = slice(c * L, (c + 1) * L)
                    if c == 0:
                        lr_, li_ = jnp.zeros((1, LANES), f32), jnp.zeros((1, LANES), f32)
                    else:
                        lr_, li_ = hr_ref[j, c * L - 1:c * L, :], hi_ref[j, c * L - 1:c * L, :]
                    hpr = jnp.where(first, lr_, pltpu.roll(hr_ref[j, rows, :], 1, 0))
                    hpi = jnp.where(first, li_, pltpu.roll(hi_ref[j, rows, :], 1, 0))
                    gr, gi = or_ref[j, rows, :], oi_ref[j, rows, :]
                    dar += jnp.sum(hpr * gr + hpi * gi, axis=0, keepdims=True)
                    dai += jnp.sum(hpr * gi - hpi * gr, axis=0, keepdims=True)
                dar_ref[j] = dar
                dai_ref[j] = dai

    slab = jax.ShapeDtypeStruct((B, NB, S, LANES), f32)
    sspec = pl.BlockSpec((None, nb, S, LANES), lambda b, j: (b, j, 0, 0))
    aspec = pl.BlockSpec((nb, 1, LANES), lambda b, j: (j, 0, 0))
    in_specs = [sspec, sspec, aspec, aspec]
    out_specs = [sspec, sspec]
    out_shape = [slab, slab]
    args = [xr, xi, ar, ai]
    if with_da:
        in_specs += [sspec, sspec]
        args += [hr, hi]
        dspec = pl.BlockSpec((None, nb, 1, LANES), lambda b, j: (b, j, 0, 0))
        out_specs += [dspec, dspec]
        out_shape += [jax.ShapeDtypeStruct((B, NB, 1, LANES), f32)] * 2
    return _call(body, name, (B, NB // nb), in_specs, out_specs, out_shape, args,
                 scratch=[pltpu.VMEM((nb, L, LANES), f32), pltpu.VMEM((nb, L, LANES), f32)],
                 sem=("parallel", "parallel"), phase=phase)


def _s5_out_parts(hr_ref, hi_ref, u_ref, cr_ref, ci_ref, d_ref, wg_ref, bg_ref, NB):
    hcr = _slab_cat(hr_ref, NB).astype(bf16)
    hci = _slab_cat(hi_ref, NB).astype(bf16)
    u = u_ref[...].astype(f32)
    y2 = _dot(hcr, cr_ref[...]) - _dot(hci, ci_ref[...]) + d_ref[...] * u
    yg = _gelu(y2)
    s = jax.nn.sigmoid(_dot(yg.astype(bf16), wg_ref[...]) + bg_ref[...])
    return hcr, hci, u, y2, yg, s


def _s5_out_specs(B, S, NB, NS, SW, tm):
    sspec = pl.BlockSpec((None, NB, tm, LANES), lambda b, i: (b, 0, i, 0))
    full = lambda r, c: pl.BlockSpec((r, c), lambda b, i: (0, 0))
    return sspec, [sspec, sspec, pl.BlockSpec((None, tm, SW), lambda b, i: (b, i, 0)), full(NS, SW), full(NS, SW),
                   full(1, SW), full(SW, SW), full(1, SW)]


def _s5_out(hr, hi, p3, cbr, cbi, dsk, wglu, bglu, tm):
    B, NB, S, _ = hr.shape
    NS, SW = cbr.shape

    def body(hr_ref, hi_ref, u_ref, cr_ref, ci_ref, d_ref, wg_ref, bg_ref, o_ref):
        _, _, _, _, yg, s = _s5_out_parts(hr_ref, hi_ref, u_ref, cr_ref, ci_ref, d_ref, wg_ref, bg_ref, NB)
        o_ref[...] = (yg * s).astype(bf16)

    _, in_specs = _s5_out_specs(B, S, NB, NS, SW, tm)
    return pl.pallas_call(
        body, name="s5_out", grid=(B, S // tm), in_specs=in_specs,
        out_specs=pl.BlockSpec((None, tm, SW), lambda b, i: (b, i, 0)),
        out_shape=jax.ShapeDtypeStruct((B, S, SW), bf16),
        compiler_params=_params(("parallel", "parallel")))(hr, hi, p3, cbr, cbi, dsk, wglu, bglu)


def _s5_out_bwd(hr, hi, p3, dmix3, cbr, cbi, dsk, wglu, bglu, tm):
    B, NB, S, _ = hr.shape
    NS, SW = cbr.shape

    def body(hr_ref, hi_ref, u_ref, cr_ref, ci_ref, d_ref, wg_ref, bg_ref, da_ref,
             dhr_ref, dhi_ref, du_ref, dcr_ref, dci_ref, dd_ref, dwg_ref, dbg_ref):
        hcr, hci, u, y2, yg, s = _s5_out_parts(hr_ref, hi_ref, u_ref, cr_ref, ci_ref, d_ref, wg_ref, bg_ref, NB)
        da = da_ref[...].astype(f32)
        dz = da * yg * s * (1.0 - s)
        dzb = dz.astype(bf16)
        dyg = da * s + _dg(dzb, wg_ref[...], NT)
        dy2 = dyg * _gelu_grad(y2)
        dyb = dy2.astype(bf16)

        @pl.when((pl.program_id(0) == 0) & (pl.program_id(1) == 0))
        def _():
            for r in (dcr_ref, dci_ref, dd_ref, dwg_ref, dbg_ref):
                r[...] = jnp.zeros_like(r)
        dwg_ref[...] += _dg(yg.astype(bf16), dzb, TN)
        dbg_ref[...] += jnp.sum(dz, axis=0, keepdims=True)
        dd_ref[...] += jnp.sum(dy2 * u, axis=0, keepdims=True)
        dcr_ref[...] += _dg(hcr, dyb, TN)
        dci_ref[...] -= _dg(hci, dyb, TN)
        du_ref[...] = dy2 * d_ref[...]
        dhr = _dg(dyb, cr_ref[...], NT)
        dhi = _dg(dyb, ci_ref[...], NT)
        for j in range(NB):
            dhr_ref[j] = dhr[:, j * LANES:(j + 1) * LANES]
            dhi_ref[j] = -dhi[:, j * LANES:(j + 1) * LANES]

    sspec, in_specs = _s5_out_specs(B, S, NB, NS, SW, tm)
    in_specs = in_specs + [pl.BlockSpec((None, tm, SW), lambda b, i: (b, i, 0))]
    full = lambda r, c: pl.BlockSpec((r, c), lambda b, i: (0, 0))
    slab = jax.ShapeDtypeStruct((B, NB, S, LANES), f32)
    mat = lambda r, c: jax.ShapeDtypeStruct((r, c), f32)
    return pl.pallas_call(
        body, name="s5_out_bwd", grid=(B, S // tm), in_specs=in_specs,
        out_specs=[sspec, sspec, pl.BlockSpec((None, tm, SW), lambda b, i: (b, i, 0)), full(NS, SW), full(NS, SW),
                   full(1, SW), full(SW, SW), full(1, SW)],
        out_shape=[slab, slab, jax.ShapeDtypeStruct((B, S, SW), f32), mat(NS, SW), mat(NS, SW), mat(1, SW),
                   mat(SW, SW), mat(1, SW)],
        compiler_params=_params(("arbitrary", "arbitrary")))(hr, hi, p3, cbr, cbi, dsk, wglu, bglu, dmix3)


def _s5_in_bwd(gr, gi, p3, bbd, du_skip, duv3, tm):
    B, NB, S, _ = gr.shape
    SW, NS2 = bbd.shape
    PW = SW + duv3.shape[2]

    def body(gr_ref, gi_ref, u_ref, b_ref, ds_ref, duv_ref, dp_ref, db_ref):
        g = jnp.concatenate([_slab_cat(gr_ref, NB), _slab_cat(gi_ref, NB)], axis=1).astype(bf16)
        du = _dg(g, b_ref[...], NT) + ds_ref[...]
        dp_ref[:, :SW] = du.astype(bf16)
        dp_ref[:, SW:] = duv_ref[...]

        @pl.when((pl.program_id(0) == 0) & (pl.program_id(1) == 0))
        def _():
            db_ref[...] = jnp.zeros_like(db_ref)
        db_ref[...] += _dg(u_ref[...], g, TN)

    sspec = pl.BlockSpec((None, NB, tm, LANES), lambda b, i: (b, 0, i, 0))
    row = lambda c: pl.BlockSpec((None, tm, c), lambda b, i: (b, i, 0))
    return pl.pallas_call(
        body, name="s5_in_bwd", grid=(B, S // tm),
        in_specs=[sspec, sspec, row(SW), pl.BlockSpec((SW, NS2), lambda b, i: (0, 0)), row(SW), row(PW - SW)],
        out_specs=[row(PW), pl.BlockSpec((SW, NS2), lambda b, i: (0, 0))],
        out_shape=[jax.ShapeDtypeStruct((B, S, PW), bf16), jax.ShapeDtypeStruct((SW, NS2), f32)],
        compiler_params=_params(("arbitrary", "arbitrary")))(gr, gi, p3, bbd, du_skip, duv3)


BIG = ['ev_w_in', 'ev_w_out', 'od_w_in', 'od_w_out', 'ffn_w_up', 'ffn_w_down']
ANY = pl.BlockSpec(memory_space=pl.ANY)


def _rtile(rows, mult):
    best = None
    for d in range(mult, min(rows, 512) + 1, mult):
        if rows % d == 0:
            best = d
    assert best is not None, (rows, mult)
    return best


def _pair_sum(name, g, recv, c_idx, out_dtype):
    NCH, R, W = g.shape
    HALF_W = W // 2
    tr = _rtile(R, 16)

    def body(c_ref, a_ref, b_ref, o_ref):
        o_ref[...] = (a_ref[...] + b_ref[...]).astype(out_dtype)

    return pl.pallas_call(
        body, name=name,
        grid_spec=pltpu.PrefetchScalarGridSpec(
            num_scalar_prefetch=1, grid=(NCH, R // tr),
            in_specs=[pl.BlockSpec((None, tr, HALF_W), lambda j, i, c: (j, i, c[0])),
                      pl.BlockSpec((None, tr, HALF_W), lambda j, i, c: (j, i, 0))],
            out_specs=pl.BlockSpec((None, tr, HALF_W), lambda j, i, c: (j, i, 0))),
        out_shape=jax.ShapeDtypeStruct((NCH, R, HALF_W), out_dtype),
        compiler_params=_params(("parallel", "parallel")))(c_idx, g, recv)


def _chip_sum(name, r3, h, k_idx):
    NCH, R, Wh = r3.shape
    tr = _rtile(R, 16)

    def body(k_ref, a_ref, own_ref, o_ref):
        own = own_ref[...].astype(f32)
        t = [jnp.where(k_ref[0] == s, own, a_ref[s].astype(f32)) for s in range(NCH)]
        o_ref[...] = ((t[0] + t[1]) + t[2]) + t[3]

    return pl.pallas_call(
        body, name=name,
        grid_spec=pltpu.PrefetchScalarGridSpec(
            num_scalar_prefetch=1, grid=(R // tr,),
            in_specs=[pl.BlockSpec((NCH, tr, Wh), lambda i, k: (0, i, 0)),
                      pl.BlockSpec((None, tr, Wh), lambda i, k: (k[0], i, 0))],
            out_specs=pl.BlockSpec((tr, Wh), lambda i, k: (i, 0))),
        out_shape=jax.ShapeDtypeStruct((R, Wh), f32),
        compiler_params=_params(("parallel",)))(k_idx, r3, h)


def _adam_math(gg, w, m, v):
    nm = ADAM_B1 * m + (1.0 - ADAM_B1) * gg
    nv = ADAM_B2 * v + (1.0 - ADAM_B2) * jnp.square(gg)
    m_hat = nm / (1.0 - ADAM_B1 ** ADAM_STEP)
    v_hat = nv / (1.0 - ADAM_B2 ** ADAM_STEP)
    return -ADAM_LR * (m_hat / (jnp.sqrt(v_hat) + ADAM_EPS) + ADAM_WD * w), nm, nv


def _adamw(name, mine, theirs, c_idx, w, m, v, lead, transposed, prev=None):
    L, R, W = w.shape
    if transposed:
        bw = LANES if W % LANES == 0 else W
        gspec = pl.BlockSpec((bw, R // 2), lambda i, hf, c: (i, 0))
        wspec = pl.BlockSpec((None, R // 2, bw), lambda i, hf, c: (lead, hf, i))
        grid = (W // bw, 2)
    else:
        tr = _rtile(R, SUBLANES)
        gspec = pl.BlockSpec((tr, W // 2), lambda i, hf, c: (i, 0))
        wspec = pl.BlockSpec((None, tr, W // 2), lambda i, hf, c: (lead, i, hf))
        grid = (R // tr, 2)

    def body(c_ref, a_ref, b_ref, w_ref, m_ref, v_ref, *rest):
        go_ref, d_ref, nm_ref, nv_ref = rest[-4:]
        gg = jnp.where(pl.program_id(1) == c_ref[0], a_ref[...], b_ref[...])
        if transposed:
            gg = gg.T
        d, nm, nv = _adam_math(gg, w_ref[...], m_ref[...], v_ref[...])
        go_ref[...] = gg
        d_ref[...] = d
        nm_ref[...] = nm
        nv_ref[...] = nv

    in_specs = [gspec, gspec, wspec, wspec, wspec]
    args, aliases = [c_idx, mine, theirs, w, m, v], {}
    if prev is not None:
        in_specs += [ANY] * 4
        args += list(prev)
        aliases = {6: 0, 7: 1, 8: 2, 9: 3}
    s = jax.ShapeDtypeStruct((L, R, W), f32)
    return pl.pallas_call(
        body, name=name,
        grid_spec=pltpu.PrefetchScalarGridSpec(num_scalar_prefetch=1, grid=grid, in_specs=in_specs,
                                               out_specs=[wspec] * 4),
        out_shape=[s, s, s, s], input_output_aliases=aliases,
        compiler_params=_params(("parallel", "arbitrary")))(*args)


def _place():
    x, y, c = lax.axis_index("x"), lax.axis_index("y"), lax.axis_index("c")
    return x, y, c, [(1 - x, y), (x, 1 - y), (1 - x, 1 - y)]


def _gathered_shape(sh, kind):
    if kind == "rows":
        return sh[:-2] + (N_CHIPS * sh[-2], sh[-1])
    if kind == "cols":
        return sh[:-1] + (N_CHIPS * sh[-1],)
    return (N_CHIPS,) + sh


def _place_shard(name, shard, kind, k_idx):
    sh = shard.shape
    r, C = sh[-2], sh[-1]
    L = sh[0] if len(sh) == 3 else 1
    tr = _rtile(r, 16)
    nr = r // tr
    if kind == "rows":
        out3, omap = (L, N_CHIPS * r, C), lambda l, i, k: (l, k[0] * nr + i, 0)
    elif kind == "cols":
        out3, omap = (L, r, N_CHIPS * C), lambda l, i, k: (l, i, k[0])
    else:
        out3, omap = (N_CHIPS, r, C), lambda l, i, k: (k[0], i, 0)

    def body(k_ref, s_ref, o_ref):
        o_ref[...] = s_ref[...]

    out = pl.pallas_call(
        body, name=name,
        grid_spec=pltpu.PrefetchScalarGridSpec(
            num_scalar_prefetch=1, grid=(L, nr),
            in_specs=[pl.BlockSpec((None, tr, C), lambda l, i, k: (l, i, 0))],
            out_specs=pl.BlockSpec((None, tr, C), omap)),
        out_shape=jax.ShapeDtypeStruct(out3, shard.dtype),
        compiler_params=_params(("parallel", "parallel")))(k_idx, shard.reshape(L, r, C))
    return out.reshape(_gathered_shape(sh, kind))


def _gather_phase(shards, fulls, kinds):
    n = len(shards)
    shapes = [s.shape for s in shards]

    def window(ref, a, k, h=None):
        sh, kind = shapes[a], kinds[a]
        r = sh[-2]
        start, size = (0, r) if h is None else (h * (r // 2), r // 2)
        lead = (slice(None),) * (len(sh) - 2)
        if kind == "rows":
            return ref.at[lead + (pl.ds(k * r + start, size), slice(None))]
        if kind == "cols":
            return ref.at[lead + (pl.ds(start, size), pl.ds(pl.multiple_of(k * sh[-1], LANES), sh[-1]))]
        return ref.at[(k,) + lead + (pl.ds(start, size), slice(None))]

    def copies(s_refs, o_refs, sems):
        send_sems, recv_sems = sems
        x, y, c, chips = _place()
        k = 2 * x + y

        def copy(a, j, kk, hh, to, src=None):
            dst = window(o_refs[a], a, kk, hh)
            return pltpu.make_async_remote_copy(
                src_ref=dst if src is None else src, dst_ref=dst, send_sem=send_sems.at[6 * a + j],
                recv_sem=recv_sems.at[6 * a + j], device_id=to, device_id_type=MESH)

        first = []
        for a in range(n):
            r = shapes[a][-2]
            lead = (slice(None),) * (len(shapes[a]) - 2)
            src = s_refs[a].at[lead + (pl.ds(c * (r // 2), r // 2), slice(None))]
            first += [copy(a, j, k, c, (*chip, c), src=src) for j, chip in enumerate(chips)]
        return copy, first, (x, y, c), (x, y, 1 - c), c, chips

    def start(s_refs, o_refs, sems):
        for cp in copies(s_refs, o_refs, sems)[1]:
            cp.start()

    def finish(s_refs, o_refs, sems):
        copy, first, me, sibling, c, chips = copies(s_refs, o_refs, sems)
        passed = []
        for j, (cx, cy) in enumerate(chips):
            for a in range(n):
                copy(a, j, 2 * cx + cy, c, me).wait_recv()
                fwd = copy(a, 3 + j, 2 * cx + cy, c, sibling)
                fwd.start()
                passed.append(fwd)
        for j, (cx, cy) in enumerate(chips):
            for a in range(n):
                copy(a, 3 + j, 2 * cx + cy, 1 - c, me).wait_recv()
        for cp in first + passed:
            cp.wait_send()

    return _Phase(shards, fulls, [jax.ShapeDtypeStruct(f.shape, f.dtype) for f in fulls],
                  [pltpu.SemaphoreType.DMA((6 * n,)), pltpu.SemaphoreType.DMA((6 * n,))], start, finish)


def _swap_phase(gs):
    n = len(gs)

    def copies(g_refs, o_refs, sems):
        send_sems, recv_sems = sems
        x, y, c, _ = _place()
        half = [g.shape[2] // 2 for g in gs]
        return [pltpu.make_async_remote_copy(
            src_ref=g_refs[a].at[:, :, pl.ds(pl.multiple_of((1 - c) * half[a], LANES), half[a])], dst_ref=o_refs[a],
            send_sem=send_sems.at[a], recv_sem=recv_sems.at[a], device_id=(x, y, 1 - c), device_id_type=MESH)
            for a in range(n)]

    def start(g_refs, o_refs, sems):
        for cp in copies(g_refs, o_refs, sems):
            cp.start()

    def finish(g_refs, o_refs, sems):
        for cp in copies(g_refs, o_refs, sems):
            cp.wait()

    return _Phase(gs, [], [jax.ShapeDtypeStruct(g.shape[:2] + (g.shape[2] // 2,), g.dtype) for g in gs],
                  [pltpu.SemaphoreType.DMA((n,)), pltpu.SemaphoreType.DMA((n,))], start, finish)


def _exchange_phase(hs):
    n = len(hs)

    def copies(h_refs, o_refs, sems):
        send_sems, recv_sems = sems
        x, y, c, chips = _place()
        k = 2 * x + y

        def copy(a, j, src_slot, dst_slot):
            cx, cy = chips[j]
            return pltpu.make_async_remote_copy(
                src_ref=h_refs[a].at[src_slot], dst_ref=o_refs[a].at[dst_slot], send_sem=send_sems.at[3 * a + j],
                recv_sem=recv_sems.at[3 * a + j], device_id=(cx, cy, c), device_id_type=MESH)

        sends = [copy(a, j, 2 * cx + cy, k) for a in range(n) for j, (cx, cy) in enumerate(chips)]
        return copy, sends, k, chips

    def start(h_refs, o_refs, sems):
        for cp in copies(h_refs, o_refs, sems)[1]:
            cp.start()

    def finish(h_refs, o_refs, sems):
        copy, sends, k, chips = copies(h_refs, o_refs, sems)
        for a in range(n):
            for j, (cx, cy) in enumerate(chips):
                copy(a, j, k, 2 * cx + cy).wait_recv()
        for cp in sends:
            cp.wait_send()

    return _Phase(hs, [], [jax.ShapeDtypeStruct(h.shape, h.dtype) for h in hs],
                  [pltpu.SemaphoreType.DMA((3 * n,)), pltpu.SemaphoreType.DMA((3 * n,))], start, finish)


def _comm_pair_share(tag, gs):
    n = len(gs)

    def body(*refs):
        g_refs, o_refs, send_sems, recv_sems = refs[:n], refs[n:2 * n], refs[2 * n], refs[2 * n + 1]
        x, y, c, _ = _place()
        cps = [pltpu.make_async_remote_copy(
            src_ref=g_refs[a], dst_ref=o_refs[a], send_sem=send_sems.at[a], recv_sem=recv_sems.at[a],
            device_id=(x, y, 1 - c), device_id_type=MESH) for a in range(n)]
        for cp in cps:
            cp.start()
        for cp in cps:
            cp.wait()

    return pl.pallas_call(
        body, name="comm_pair_share_" + tag, in_specs=[ANY] * n, out_specs=[ANY] * n,
        out_shape=[jax.ShapeDtypeStruct(g.shape, g.dtype) for g in gs],
        scratch_shapes=[pltpu.SemaphoreType.DMA((n,)), pltpu.SemaphoreType.DMA((n,))])(*gs)


def _pad_rows(flat, unit):
    n = flat.shape[-1]
    pad = (-n) % unit
    if pad:
        flat = jnp.pad(flat, [(0, 0)] * (flat.ndim - 1) + [(0, pad)])
    return flat


def _split_chips(full, axis):
    sh = full.shape
    t = full.reshape(sh[:axis] + (N_CHIPS, sh[axis] // N_CHIPS) + sh[axis + 1:])
    return jnp.moveaxis(t, axis, 0).reshape(N_CHIPS, -1)


def _join_chips(stack, shard_shape, axis):
    t = jnp.moveaxis(stack.reshape((N_CHIPS,) + tuple(shard_shape)), 0, axis)
    sh = t.shape
    return t.reshape(sh[:axis] + (sh[axis] * sh[axis + 1],) + sh[axis + 2:])


def _block_diag(blocks):
    G, r, c = blocks.shape
    eye = jnp.eye(G, dtype=blocks.dtype)
    return (blocks[:, :, None, :] * eye[:, None, :, None]).reshape(G * r, G * c)


def _diag_blocks(m, G):
    r, c = m.shape[0] // G, m.shape[1] // G
    idx = jnp.arange(G)
    return m.reshape(G, r, G, c)[idx, :, idx, :]


def _weight_shards(w):
    conv = jnp.concatenate([w[n].reshape(-1) for n in GATHER_F32])
    conv = _pad_rows(conv, 2 * SUBLANES * LANES).reshape(-1, LANES)
    b16 = lambda a: a.astype(bf16)
    return {'ev_w_in': (b16(w['ev_w_in'][0]), "chip"), 'ev_w_out': (b16(w['ev_w_out'][0]), "rows"),
            's5_w_glu': (b16(w['s5_w_glu'][0]), "rows"), 'conv': (conv, "chip"),
            'od_w_in': (b16(w['od_w_in'][0]), "cols"), 'od_w_out': (b16(w['od_w_out'][0]), "rows"),
            'ffn_w_up0': (b16(w['ffn_w_up'][0]), "cols"), 'ffn_w_up1': (b16(w['ffn_w_up'][1]), "cols"),
            'ffn_w_down0': (b16(w['ffn_w_down'][0]), "rows"), 'ffn_w_down1': (b16(w['ffn_w_down'][1]), "rows")}


def kernel(x, mix_norm_g, ffn_norm_g, final_norm_g, ev_w_in, ev_w_out, s5_lam_re, s5_lam_im, s5_log_dt, s5_b_re, s5_b_im, s5_c_re, s5_c_im, s5_d, s5_w_glu, s5_b_glu, gm_w_s, gm_b_s, gm_v_g, od_w_in, od_conv_w, od_conv_b, od_w_out, ffn_w_up, ffn_conv_w, ffn_conv_b, ffn_w_down, loss_target, m_mix_norm_g, m_ffn_norm_g, m_final_norm_g, m_ev_w_in, m_ev_w_out, m_s5_lam_re, m_s5_lam_im, m_s5_log_dt, m_s5_b_re, m_s5_b_im, m_s5_c_re, m_s5_c_im, m_s5_d, m_s5_w_glu, m_s5_b_glu, m_gm_w_s, m_gm_b_s, m_gm_v_g, m_od_w_in, m_od_conv_w, m_od_conv_b, m_od_w_out, m_ffn_w_up, m_ffn_conv_w, m_ffn_conv_b, m_ffn_w_down, v_mix_norm_g, v_ffn_norm_g, v_final_norm_g, v_ev_w_in, v_ev_w_out, v_s5_lam_re, v_s5_lam_im, v_s5_log_dt, v_s5_b_re, v_s5_b_im, v_s5_c_re, v_s5_c_im, v_s5_d, v_s5_w_glu, v_s5_b_glu, v_gm_w_s, v_gm_b_s, v_gm_v_g, v_od_w_in, v_od_conv_w, v_od_conv_b, v_od_w_out, v_ffn_w_up, v_ffn_conv_w, v_ffn_conv_b, v_ffn_w_down):
    loc = dict(locals())
    w = {n: loc[n] for n in WEIGHTS}
    mom = {n: loc["m_" + n] for n in WEIGHTS}
    var = {n: loc["v_" + n] for n in WEIGHTS}

    B, S, D = x.shape
    T = B * S
    SW = s5_d.shape[1]
    G = SW // SSM_GROUP
    NS = G * SSM_STATE
    NB = NS // LANES
    tm = min(512, S)
    tt = min(1024, T)
    c_idx = lax.axis_index("c").astype(jnp.int32).reshape(1)
    k_idx = (2 * lax.axis_index("x") + lax.axis_index("y")).astype(jnp.int32).reshape(1)
    shards = _weight_shards(w)
    placed = {n: _place_shard("place_" + n, s, kd, k_idx) for n, (s, kd) in shards.items()}

    def gather(names):
        return _gather_phase([shards[n][0] for n in names], [placed[n] for n in names], [shards[n][1] for n in names])

    (w_ev_in,) = _run_phase("comm_gather_ev_in", gather(['ev_w_in']))
    w_ev_in = jnp.swapaxes(w_ev_in, 0, 1).reshape(D, -1)

    h0 = x.reshape(T, D)
    (y0, p0), (w_ev_out, w_glu, conv) = _norm_mm("ev_in", h0, mix_norm_g[0], w_ev_in, tm,
                                                 phase=gather(['ev_w_out', 's5_w_glu', 'conv']))
    full, off = {}, 0
    for n in GATHER_F32:
        full[n] = _join_chips(conv.reshape(N_CHIPS, -1)[:, off:off + w[n].size], w[n].shape, SHARD_AXIS[n])
        off += w[n].size
    PW = p0.shape[1]
    p03 = p0.reshape(B, S, PW)
    lr, li, ldt = s5_lam_re[0], s5_lam_im[0], s5_log_dt[0].reshape(G, 1)
    ar, ai, zr, zi = _s5_prep(lr, li, ldt)
    bre = _block_diag(jnp.swapaxes(s5_b_re[0], 1, 2))
    bim = _block_diag(jnp.swapaxes(s5_b_im[0], 1, 2))
    cbr = _block_diag(jnp.swapaxes(s5_c_re[0], 1, 2)).astype(bf16)
    cbi = _block_diag(jnp.swapaxes(s5_c_im[0], 1, 2)).astype(bf16)
    zr_row, zi_row = zr.reshape(1, NS), zi.reshape(1, NS)
    bbd = _s5_bbd(zr_row, zi_row, bre, bim)
    ar_s, ai_s = ar.reshape(NB, 1, LANES), ai.reshape(NB, 1, LANES)
    xr, xi = _s5_in(p03, bbd, SW, tm)
    (hr, hi), (w_up0, w_down0) = _s5_scan("s5_scan", xr, xi, ar_s, ai_s, False,
                                           phase=gather(['ffn_w_up0', 'ffn_w_down0']))
    dsk, bglu = s5_d.reshape(1, SW), s5_b_glu.reshape(1, SW)
    a_out = _s5_out(hr, hi, p03, cbr, cbi, dsk, w_glu, bglu, tm)
    ws, bst, gv = gm_w_s[0], gm_b_s[0].T, gm_v_g.reshape(1, -1)
    mixcat = _gmlp(p0, a_out.reshape(T, SW), ws, bst, gv, SW)
    h1 = _mm_resid("ev_out", mixcat, w_ev_out, h0, tm)

    def ffn_fwd(l, h, w_up, w_down, up_phase=None, down_phase=None):
        res = _norm_mm(f"ffn_up{l}", h, ffn_norm_g[l], w_up, tm, phase=up_phase)
        (z, up), got_up = res if up_phase is not None else (res, None)
        res = _ffn_down(f"ffn_down{l}", up, full['ffn_conv_w'][l], ffn_conv_b[l].reshape(1, -1), w_down, h, S, tm,
                        phase=down_phase)
        (hn,), got_down = res if down_phase is not None else (res, None)
        return hn, (z, up.reshape(B, S, -1)), got_up, got_down

    h2, ffn0, (w_up1, w_down1), (w_od_in, w_od_out) = ffn_fwd(
        0, h1, w_up0, w_down0, gather(['ffn_w_up1', 'ffn_w_down1']), gather(['od_w_in', 'od_w_out']))
    w_ups, w_downs = (w_up0, w_up1), (w_down0, w_down1)
    od_cw, od_cb = full['od_conv_w'][0], full['od_conv_b']
    y1, p1 = _norm_mm("od_in", h2, mix_norm_g[1], w_od_in, tm)
    p13 = p1.reshape(B, S, -1)
    sc = _od_act(p13, od_cw, od_cb)
    h3 = _mm_resid("od_out", sc.reshape(T, D), w_od_out, h2, tm)
    h4, ffn1, _, _ = ffn_fwd(1, h3, w_up1, w_down1)

    dh4, dh4b, loss_part, d_final_g = _final_loss(h4, final_norm_g, loss_target.reshape(T, D), tm)
    loss = lax.psum(loss_part[0, 0], ("x", "y", "c"))

    grads = {}

    halves = {}
    chips = lambda g: g.reshape(N_CHIPS, -1, D)

    def pair_sums(names, parts, recv):
        return [_pair_sum(f"pair_sum_{n}", g, r, c_idx, f32 if n == "small" else bf16)
                for n, g, r in zip(names, parts, recv)]

    def reduce_end(tag, names, hsum, r3):
        mine = [_chip_sum(f"chip_sum_{n}", r, h, k_idx) for n, r, h in zip(names, r3, hsum)]
        theirs = _comm_pair_share(tag, mine)
        halves.update({n: (a, b) for n, a, b in zip(names, mine, theirs)})

    def ffn_bwd(l, dh, dhb, h_in, saved, phase=None, swap=False):
        z, up3 = saved
        w_down, w_up = w_downs[l], w_ups[l]
        da = _mm_nt(f"ffn_down_bwd{l}", dhb, w_down, tm)
        res = _ffn_act_bwd(f"ffn_act_bwd{l}", up3, da.reshape(B, S, -1), full['ffn_conv_w'][l],
                           ffn_conv_b[l].reshape(1, -1), phase=phase)
        (act, dg3, dv3, dcwg, dcwv, dcbg, dcbv), got = res if phase is not None else (res, None)
        g_down = _mm_tn(f"ffn_down_dw{l}", act.reshape(T, -1), dhb, tt)
        dupg, dupv = dg3.reshape(T, -1), dv3.reshape(T, -1)
        F = dupg.shape[1]
        g_up = _mm_tn(f"ffn_up_dw{l}_gate", dupg, z, tt, rows=2 * F)
        g_up = _mm_tn(f"ffn_up_dw{l}_val", dupv, z, tt, rows=2 * F, row_off=F, prev=g_up)
        parts = [chips(g_down), chips(g_up)]
        res = _mm_nt_normbwd(f"ffn_up_bwd{l}", [dupg, dupv], w_up, h_in, ffn_norm_g[l], dh, tm,
                             phase=_swap_phase(parts) if swap else None)
        (dh_new, dhb_new, dg), recv = res if swap else (res, None)
        F = dg3.shape[2]
        dcw = jnp.concatenate([dcwg[:, :F], dcwv[:, :F]], axis=1)
        dcb = jnp.concatenate([dcbg[:, :F], dcbv[:, :F]], axis=1)
        return dh_new, dhb_new, g_down, g_up, dcw, dcb[0], dg[0], got, parts, recv

    dh3, dh3b, gd1, gu1, gcw1, gcb1, gng1, _, _, _ = ffn_bwd(1, dh4, dh4b, h3, ffn1)
    dsc = _mm_nt("od_out_bwd", dh3b, w_od_out, tm)
    g_od_out = _mm_tn("od_out_dw", sc.reshape(T, D), dh3b, tt)
    dbg3, dcg3, dhx3, d_od_cw, d_od_cb = _od_act_bwd(p13, dsc.reshape(B, S, D), od_cw, od_cb)
    dp1 = [t.reshape(T, D) for t in (dbg3, dcg3, dhx3)]
    g_od_in = None
    for i, piece in enumerate(dp1):
        g_od_in = _mm_tn(f"od_in_dw{i}", piece, y1, tt, rows=3 * D, row_off=i * D, prev=g_od_in)
    grads['od_conv_w'] = d_od_cw[None]
    grads['od_conv_b'] = d_od_cb
    layer1 = ['ffn_w_down1', 'ffn_w_up1', 'od_w_out', 'od_w_in']
    parts1 = [chips(g) for g in (gd1, gu1, g_od_out, g_od_in)]
    (dh2, dh2b, gmix1), recv1 = _mm_nt_normbwd("od_in_bwd", dp1, w_od_in, h2, mix_norm_g[1], dh3, tm,
                                               phase=_swap_phase(parts1))
    hsum1 = pair_sums(layer1, parts1, recv1)
    dh1, dh1b, gd0, gu0, gcw0, gcb0, gng0, r3, parts0, recv0 = ffn_bwd(
        0, dh2, dh2b, h1, ffn0, phase=_exchange_phase(hsum1), swap=True)
    reduce_end("layer1", layer1, hsum1, r3)
    ffn0_names = ['ffn_w_down0', 'ffn_w_up0']
    hsum0 = pair_sums(ffn0_names, parts0, recv0)
    grads['ffn_conv_w'] = jnp.stack([gcw0, gcw1])
    grads['ffn_conv_b'] = jnp.stack([gcb0, gcb1])
    grads['ffn_norm_g'] = jnp.stack([gng0, gng1])
    grads['final_norm_g'] = d_final_g[0]

    dmix = _mm_nt("ev_out_bwd", dh1b, w_ev_out, tm)
    g_ev_out = _mm_tn("ev_out_dw", mixcat, dh1b, tt)
    duv, d_ws, d_bs, d_gv = _gmlp_bwd(p0, dmix, ws, bst, gv, SW)
    grads['gm_w_s'] = d_ws[None]
    grads['gm_b_s'] = d_bs[:, :, 0][None]
    grads['gm_v_g'] = d_gv
    dhr, dhi, du_skip, d_cbr, d_cbi, d_dsk, d_wglu, d_bglu = _s5_out_bwd(
        hr, hi, p03, dmix.reshape(B, S, D), cbr, cbi, dsk, w_glu, bglu, tm)
    grads['s5_c_re'] = jnp.swapaxes(_diag_blocks(d_cbr, G), 1, 2)[None]
    grads['s5_c_im'] = jnp.swapaxes(_diag_blocks(d_cbi, G), 1, 2)[None]
    grads['s5_d'] = d_dsk
    grads['s5_w_glu'] = d_wglu[None]
    grads['s5_b_glu'] = d_bglu
    (gr, gi, dar, dai), r3 = _s5_scan("s5_rscan", dhr, dhi, ar_s, ai_s, True, hr, hi, phase=_exchange_phase(hsum0))
    reduce_end("ffn0", ffn0_names, hsum0, r3)
    dp03, d_bbd = _s5_in_bwd(gr, gi, p03, bbd, du_skip, duv.reshape(B, S, -1), tm)
    d_bre, d_bim, d_zr, d_zi = _s5_bbd_bwd(d_bbd, zr_row, zi_row, bre, bim)
    grads['s5_b_re'] = jnp.swapaxes(_diag_blocks(d_bre, G), 1, 2)[None]
    grads['s5_b_im'] = jnp.swapaxes(_diag_blocks(d_bim, G), 1, 2)[None]
    shp = (-1, G, SSM_STATE)
    d_lr, d_li, d_ldt = _s5_prep_bwd(lr, li, ldt, dar.reshape(shp), dai.reshape(shp), d_zr.reshape(shp),
                                     d_zi.reshape(shp))
    grads['s5_lam_re'] = d_lr[None]
    grads['s5_lam_im'] = d_li[None]
    grads['s5_log_dt'] = d_ldt.reshape(1, G)
    dp0 = dp03.reshape(T, PW)
    g_ev_in = _mm_tn("ev_in_dw", dp0, y0, tt)
    grad_x, _, gmix0 = _mm_nt_normbwd("ev_in_bwd", [dp0], w_ev_in, h0, mix_norm_g[0], dh1, tm)
    grads['mix_norm_g'] = jnp.concatenate([gmix0, gmix1], axis=0)

    small = [n for n in WEIGHTS if n not in BIG]
    segs = []
    for n in small:
        gfull = grads[n].astype(f32)
        if n in SHARD_AXIS:
            segs.append(_split_chips(gfull, SHARD_AXIS[n]))
        else:
            segs.append(jnp.broadcast_to(gfull.reshape(1, -1), (N_CHIPS, gfull.size)))
    unit = 2 * SUBLANES * D
    gsmall = _pad_rows(jnp.concatenate(segs, axis=1), unit).reshape(N_CHIPS, -1, D)
    mixer0 = ['ev_w_out', 'ev_w_in', 'small']
    parts = [chips(g_ev_out), chips(g_ev_in), gsmall]
    hsum = pair_sums(mixer0, parts, _run_phase("comm_pair_swap_mixer0", _swap_phase(parts)))
    reduce_end("mixer0", mixer0, hsum, _run_phase("comm_exchange_mixer0", _exchange_phase(hsum)))

    out_g, out_d, out_m, out_v = {}, {}, {}, {}

    def update(n, key, lead, transposed, prev=None):
        res = _adamw(f"adamw_{key}", *halves[key], c_idx, w[n], mom[n], var[n], lead, transposed, prev)
        out_g[n], out_d[n], out_m[n], out_v[n] = res
        return res

    update('ev_w_in', 'ev_w_in', 0, True)
    update('ev_w_out', 'ev_w_out', 0, False)
    update('od_w_in', 'od_w_in', 0, True)
    update('od_w_out', 'od_w_out', 0, False)
    update('ffn_w_up', 'ffn_w_up0', 0, True, prev=update('ffn_w_up', 'ffn_w_up1', 1, True))
    update('ffn_w_down', 'ffn_w_down0', 0, False, prev=update('ffn_w_down', 'ffn_w_down1', 1, False))

    def pack_local(d):
        flat = _pad_rows(jnp.concatenate([d[n].astype(f32).reshape(-1) for n in small]), unit)
        return flat.reshape(1, -1, D)

    res = _adamw("adamw_small", *halves['small'], c_idx, pack_local(w), pack_local(mom), pack_local(var), 0, False)
    for dst, p in zip((out_g, out_d, out_m, out_v), res):
        flat, off = p.reshape(-1), 0
        for n in small:
            dst[n] = flat[off:off + w[n].size].reshape(w[n].shape)
            off += w[n].size

    return (loss, grad_x.reshape(B, S, D), *[out_g[n] for n in WEIGHTS], *[out_d[n] for n in WEIGHTS],
            *[out_m[n] for n in WEIGHTS], *[out_v[n] for n in WEIGHTS])
```

```python
import functools
import math

import jax
import jax.numpy as jnp
from jax import lax
from jax.experimental import pallas as pl
from jax.experimental.pallas import tpu as pltpu

f32 = jnp.float32
bf16 = jnp.bfloat16
MESH = pl.DeviceIdType.MESH

SSM_GROUP = 16
SSM_STATE = 64
GMLP_HEAD = 128
CHUNK = 128
EPS = 1e-6
LAMBDA_RE_MAX = -1e-4
ADAM_LR, ADAM_B1, ADAM_B2, ADAM_EPS, ADAM_WD, ADAM_STEP = 0.001, 0.9, 0.999, 1e-08, 0.01, 10

LANES = 128
SUBLANES = 8
NSUB = 32
HALO = 16
VMEM_LIMIT = 56 * 1024 * 1024
N_CHIPS = 4

WEIGHTS = ['mix_norm_g', 'ffn_norm_g', 'final_norm_g', 'ev_w_in', 'ev_w_out', 's5_lam_re', 's5_lam_im', 's5_log_dt',
           's5_b_re', 's5_b_im', 's5_c_re', 's5_c_im', 's5_d', 's5_w_glu', 's5_b_glu', 'gm_w_s', 'gm_b_s', 'gm_v_g',
           'od_w_in', 'od_conv_w', 'od_conv_b', 'od_w_out', 'ffn_w_up', 'ffn_conv_w', 'ffn_conv_b', 'ffn_w_down']
SHARD_AXIS = {'ev_w_in': 2, 'ev_w_out': 1, 's5_w_glu': 1, 'od_w_in': 2, 'od_conv_w': 2, 'od_conv_b': 1, 'od_w_out': 1,
              'ffn_w_up': 2, 'ffn_conv_w': 2, 'ffn_w_down': 1}
GATHER_BF16 = ['ev_w_in', 'ev_w_out', 's5_w_glu', 'od_w_in', 'od_w_out', 'ffn_w_up', 'ffn_w_down']
GATHER_F32 = ['od_conv_w', 'od_conv_b', 'ffn_conv_w']

_GELU_K0 = math.sqrt(2.0 / math.pi)
_GELU_K1 = 0.044715
NT = (((1,), (1,)), ((), ()))
TN = (((0,), (0,)), ((), ()))


def _pick(n, cap):
    if n <= cap:
        return n
    best = None
    for d in range(LANES, cap + 1, LANES):
        if n % d == 0:
            best = d
    assert best is not None, (n, cap)
    return best


def _params(sem=None):
    return pltpu.CompilerParams(dimension_semantics=sem, vmem_limit_bytes=VMEM_LIMIT)


class _Phase:
    def __init__(self, ins, inplace, outs, sems, start, finish):
        self.ins, self.inplace, self.outs, self.sems = list(ins), list(inplace), list(outs), list(sems)
        self.start, self.finish = start, finish


def _call(body, name, grid, in_specs, out_specs, out_shape, args, scratch=(), sem=None, phase=None):
    if phase is None:
        return pl.pallas_call(body, name=name, grid=grid, in_specs=in_specs, out_specs=out_specs, out_shape=out_shape,
                              scratch_shapes=list(scratch), compiler_params=_params(sem))(*args)
    any_spec = pl.BlockSpec(memory_space=pl.ANY)
    n_in, n_out, n_scr = len(args), len(out_shape), len(scratch)
    p_in = phase.ins + phase.inplace
    ci, co = len(p_in), len(phase.outs)

    def wrapped(*refs):
        ins, cins = refs[:n_in], refs[n_in:n_in + len(phase.ins)]
        b = n_in + ci
        outs, couts = refs[b:b + n_out], refs[b + n_out:b + n_out + co]
        d = b + n_out + co
        scr, csem = refs[d:d + n_scr], refs[d + n_scr:]
        ids = [pl.program_id(i) for i in range(len(grid))]
        first = functools.reduce(jnp.logical_and, [i == 0 for i in ids])
        last = functools.reduce(jnp.logical_and, [i == g - 1 for i, g in zip(ids, grid)])

        @pl.when(first)
        def _():
            phase.start(cins, couts, csem)
        body(*ins, *outs, *scr)

        @pl.when(last)
        def _():
            phase.finish(cins, couts, csem)

    res = pl.pallas_call(
        wrapped, name=name, grid=grid, in_specs=list(in_specs) + [any_spec] * ci,
        out_specs=list(out_specs) + [any_spec] * co, out_shape=list(out_shape) + phase.outs,
        scratch_shapes=list(scratch) + phase.sems,
        input_output_aliases={n_in + len(phase.ins) + i: n_out + i for i in range(len(phase.inplace))},
        compiler_params=_params(tuple("arbitrary" for _ in grid)))(*args, *p_in)
    return res[:n_out], res[n_out:]


def _run_phase(name, phase):
    any_spec = pl.BlockSpec(memory_space=pl.ANY)
    ni, ci, co = len(phase.ins), len(phase.ins) + len(phase.inplace), len(phase.outs)

    def body(*refs):
        cins, couts, csem = refs[:ni], refs[ci:ci + co], refs[ci + co:]
        phase.start(cins, couts, csem)
        phase.finish(cins, couts, csem)

    return pl.pallas_call(
        body, name=name, in_specs=[any_spec] * ci, out_specs=[any_spec] * co, out_shape=phase.outs,
        scratch_shapes=phase.sems, input_output_aliases={ni + i: i for i in range(len(phase.inplace))})(
            *phase.ins, *phase.inplace)


def _gelu(x):
    return 0.5 * x * (1.0 + jnp.tanh(_GELU_K0 * (x + _GELU_K1 * x * x * x)))


def _gelu_grad(x):
    t = jnp.tanh(_GELU_K0 * (x + _GELU_K1 * x * x * x))
    return 0.5 * (1.0 + t) + 0.5 * x * (1.0 - t * t) * _GELU_K0 * (1.0 + 3.0 * _GELU_K1 * x * x)


def _rms_stats(x):
    r = lax.rsqrt(jnp.mean(x * x, axis=-1, keepdims=True) + EPS)
    return x * r, r


def _rms_bwd(dy, xh, r, g):
    dxh = dy * g
    dx = r * (dxh - xh * jnp.mean(dxh * xh, axis=-1, keepdims=True))
    return dx, jnp.sum(dy * xh, axis=0, keepdims=True)


def _dot(a, b):
    return jnp.dot(a, b, preferred_element_type=f32)


def _dg(a, b, dims):
    return lax.dot_general(a, b, dims, preferred_element_type=f32)


def _row_fold(z):
    return z.reshape(z.shape[0] // SUBLANES, SUBLANES, z.shape[1]).sum(axis=0)


def _norm_mm(name, h, g, w, tm, phase=None):
    T, D = h.shape
    N = w.shape[1]
    nc = _pick(N, 512)

    def body(h_ref, g_ref, w_ref, y_ref, o_ref):
        xh, _ = _rms_stats(h_ref[...])
        y = (xh * g_ref[...]).astype(bf16)
        y_ref[...] = y
        for j in range(N // nc):
            o_ref[:, j * nc:(j + 1) * nc] = _dot(y, w_ref[:, j * nc:(j + 1) * nc]).astype(bf16)

    return _call(
        body, name, (T // tm,),
        [pl.BlockSpec((tm, D), lambda i: (i, 0)), pl.BlockSpec((1, D), lambda i: (0, 0)),
         pl.BlockSpec((D, N), lambda i: (0, 0))],
        [pl.BlockSpec((tm, D), lambda i: (i, 0)), pl.BlockSpec((tm, N), lambda i: (i, 0))],
        [jax.ShapeDtypeStruct((T, D), bf16), jax.ShapeDtypeStruct((T, N), bf16)],
        [h, g.reshape(1, D), w], sem=("parallel",), phase=phase)


def _mm_resid(name, a, w, resid, tm):
    T, K = a.shape
    N = w.shape[1]

    def body(a_ref, w_ref, r_ref, o_ref):
        o_ref[...] = r_ref[...] + _dot(a_ref[...], w_ref[...])

    return pl.pallas_call(
        body, name=name, grid=(T // tm,),
        in_specs=[pl.BlockSpec((tm, K), lambda i: (i, 0)), pl.BlockSpec((K, N), lambda i: (0, 0)),
                  pl.BlockSpec((tm, N), lambda i: (i, 0))],
        out_specs=pl.BlockSpec((tm, N), lambda i: (i, 0)),
        out_shape=jax.ShapeDtypeStruct((T, N), f32),
        compiler_params=_params(("parallel",)))(a, w, resid)


def _mm_nt(name, dy, w, tm):
    T, N = dy.shape
    K = w.shape[0]
    kc = _pick(K, 512)

    def body(d_ref, w_ref, o_ref):
        d = d_ref[...].astype(bf16)
        for j in range(K // kc):
            o_ref[:, j * kc:(j + 1) * kc] = _dg(d, w_ref[j * kc:(j + 1) * kc, :], NT).astype(bf16)

    return pl.pallas_call(
        body, name=name, grid=(T // tm,),
        in_specs=[pl.BlockSpec((tm, N), lambda i: (i, 0)), pl.BlockSpec((K, N), lambda i: (0, 0))],
        out_specs=pl.BlockSpec((tm, K), lambda i: (i, 0)),
        out_shape=jax.ShapeDtypeStruct((T, K), bf16),
        compiler_params=_params(("parallel",)))(dy, w)


def _mm_nt_normbwd(name, dys, w, h, g, dh_in, tm, phase=None):
    n = len(dys)
    T = dys[0].shape[0]
    D = w.shape[0]
    widths = [d.shape[1] for d in dys]
    offs = [sum(widths[:i]) for i in range(n)]

    def body(*refs):
        d_refs = refs[:n]
        w_ref, h_ref, g_ref, dh_ref, o_ref, ob_ref, dg_ref = refs[n:]
        dz = _dg(d_refs[0][...], w_ref[:, :widths[0]], NT)
        for i in range(1, n):
            dz += _dg(d_refs[i][...], w_ref[:, offs[i]:offs[i] + widths[i]], NT)
        xh, r = _rms_stats(h_ref[...])
        dx, dg = _rms_bwd(dz, xh, r, g_ref[...])
        out = dh_ref[...] + dx
        o_ref[...] = out
        ob_ref[...] = out.astype(bf16)

        @pl.when(pl.program_id(0) == 0)
        def _():
            dg_ref[...] = jnp.zeros_like(dg_ref)
        dg_ref[...] += dg

    row = lambda c: pl.BlockSpec((tm, c), lambda i: (i, 0))
    return _call(
        body, name, (T // tm,),
        [row(c) for c in widths] + [pl.BlockSpec((D, sum(widths)), lambda i: (0, 0)), row(D),
                                    pl.BlockSpec((1, D), lambda i: (0, 0)), row(D)],
        [row(D), row(D), pl.BlockSpec((1, D), lambda i: (0, 0))],
        [jax.ShapeDtypeStruct((T, D), f32), jax.ShapeDtypeStruct((T, D), bf16), jax.ShapeDtypeStruct((1, D), f32)],
        [*dys, w, h, g.reshape(1, D), dh_in], sem=("arbitrary",), phase=phase)


def _mm_tn(name, a, b, tt, rows=None, row_off=0, prev=None):
    T, K = a.shape
    N = b.shape[1]
    rows = K if rows is None else rows
    tk = _pick(K, 1408)
    tn = _pick(N, 1024)
    assert row_off % tk == 0
    kb = row_off // tk

    def body(a_ref, b_ref, *rest):
        o_ref = rest[-1]

        @pl.when(pl.program_id(2) == 0)
        def _():
            o_ref[...] = jnp.zeros_like(o_ref)
        o_ref[...] += _dg(a_ref[...], b_ref[...], TN)

    in_specs = [pl.BlockSpec((tt, tk), lambda k, n, t: (t, k)), pl.BlockSpec((tt, tn), lambda k, n, t: (t, n))]
    args, aliases = [a, b], {}
    if prev is not None:
        in_specs.append(ANY)
        args.append(prev)
        aliases = {2: 0}
    return pl.pallas_call(
        body, name=name, grid=(K // tk, N // tn, T // tt), in_specs=in_specs,
        out_specs=pl.BlockSpec((tk, tn), lambda k, n, t: (k + kb, n)),
        out_shape=jax.ShapeDtypeStruct((rows, N), f32), input_output_aliases=aliases,
        compiler_params=_params(("parallel", "parallel", "arbitrary")))(*args)


def _final_loss(h, g, tgt, tm):
    T, D = h.shape

    def body(h_ref, g_ref, t_ref, dh_ref, dhb_ref, loss_ref, dg_ref):
        xh, r = _rms_stats(h_ref[...])
        gg = g_ref[...]
        diff = xh * gg - t_ref[...]
        dy = diff * (1.0 / D)
        dx, dg = _rms_bwd(dy, xh, r, gg)
        dh_ref[...] = dx
        dhb_ref[...] = dx.astype(bf16)

        @pl.when(pl.program_id(0) == 0)
        def _():
            dg_ref[...] = jnp.zeros_like(dg_ref)
            loss_ref[...] = jnp.zeros_like(loss_ref)
        dg_ref[...] += dg
        loss_ref[...] += (0.5 / D) * jnp.sum(jnp.sum(diff * diff, axis=1, keepdims=True), axis=0, keepdims=True)

    return pl.pallas_call(
        body, name="final_loss", grid=(T // tm,),
        in_specs=[pl.BlockSpec((tm, D), lambda i: (i, 0)), pl.BlockSpec((1, D), lambda i: (0, 0)),
                  pl.BlockSpec((tm, D), lambda i: (i, 0))],
        out_specs=[pl.BlockSpec((tm, D), lambda i: (i, 0)), pl.BlockSpec((tm, D), lambda i: (i, 0)),
                   pl.BlockSpec((1, 1), lambda i: (0, 0)), pl.BlockSpec((1, D), lambda i: (0, 0))],
        out_shape=[jax.ShapeDtypeStruct((T, D), f32), jax.ShapeDtypeStruct((T, D), bf16),
                   jax.ShapeDtypeStruct((1, 1), f32), jax.ShapeDtypeStruct((1, D), f32)],
        compiler_params=_params(("arbitrary",)))(h, g.reshape(1, D), tgt)


def _taps(load, r0, R):
    main = load(r0, R)
    hs = pl.multiple_of(jnp.maximum(r0 - HALO, 0), HALO)
    halo = load(hs, HALO) * (r0 > 0).astype(f32)
    ext = jnp.concatenate([halo, main], axis=0)
    xm1 = pltpu.roll(ext, 1, 0)[HALO:]
    xm2 = pltpu.roll(ext, 2, 0)[HALO:]
    return xm2, xm1, main


def _conv(w, b, taps):
    return b + w[0:1] * taps[0] + w[1:2] * taps[1] + w[2:3] * taps[2]


def _ref_load(ref):
    return lambda s, n: ref[pl.ds(s, n), :].astype(f32)


def _ffn_down(name, up, cw, cb, w_down, resid, S, tm, phase=None):
    T, F2 = up.shape
    F = F2 // 2
    D = w_down.shape[1]
    cwid = _pick(F, 256)
    per_seq = S // tm

    def body(u_ref, halo_ref, cw_ref, cb_ref, w_ref, r_ref, o_ref):
        keep = (pl.program_id(0) % per_seq > 0).astype(f32)

        def conv(off):
            cols = slice(off, off + cwid)
            main = u_ref[:, cols].astype(f32)
            ext = jnp.concatenate([halo_ref[:, cols].astype(f32) * keep, main], axis=0)
            taps = (pltpu.roll(ext, 2, 0)[HALO:], pltpu.roll(ext, 1, 0)[HALO:], main)
            return _conv(cw_ref[:, cols], cb_ref[:, cols], taps)

        acc = r_ref[...]
        for j in range(F // cwid):
            cg, cv = conv(j * cwid), conv(F + j * cwid)
            a = (cg * jax.nn.sigmoid(cg) * cv).astype(bf16)
            acc = acc + _dot(a, w_ref[j * cwid:(j + 1) * cwid, :])
        o_ref[...] = acc

    full = lambda r, c: pl.BlockSpec((r, c), lambda i: (0, 0))
    return _call(
        body, name, (T // tm,),
        [pl.BlockSpec((tm, F2), lambda i: (i, 0)),
         pl.BlockSpec((HALO, F2), lambda i: (jnp.maximum(i * (tm // HALO) - 1, 0), 0)),
         full(3, F2), full(1, F2), full(F, D), pl.BlockSpec((tm, D), lambda i: (i, 0))],
        [pl.BlockSpec((tm, D), lambda i: (i, 0))], [jax.ShapeDtypeStruct((T, D), f32)],
        [up, up, cw, cb, w_down, resid], sem=("parallel",), phase=phase)


def _rev_conv_rows(d, nxt, w):
    R = d.shape[0]
    ext = jnp.concatenate([d, nxt], axis=0)
    n = R + HALO
    xp1 = pltpu.roll(ext, n - 1, 0)[:R]
    xp2 = pltpu.roll(ext, n - 2, 0)[:R]
    return w[2:3] * d + w[1:2] * xp1 + w[0:1] * xp2


def _conv_grad_acc(acc, dc, taps):
    return (acc[0] + _row_fold(dc * taps[0]), acc[1] + _row_fold(dc * taps[1]), acc[2] + _row_fold(dc * taps[2]),
            acc[3] + _row_fold(dc))


def _conv_grad_out(dcw_ref, dcb_ref, acc):
    @pl.when(pl.program_id(1) == 0)
    def _():
        dcw_ref[...] = jnp.zeros_like(dcw_ref)
        dcb_ref[...] = jnp.zeros_like(dcb_ref)
    for k in range(3):
        dcw_ref[k:k + 1, :] += jnp.sum(acc[k], axis=0, keepdims=True)
    dcb_ref[...] += jnp.sum(acc[3], axis=0, keepdims=True)


def _ffn_act_bwd(name, up3, da3, cw, cb, phase=None):
    B, S, F2 = up3.shape
    F = F2 // 2
    cwid = _pick(F, 256)
    nF = F // cwid
    R = min(256, S)
    nR = S // R

    def body(g_ref, v_ref, da_ref, wg_ref, wv_ref, bg_ref, bv_ref,
             act_ref, dg_ref, dv_ref, dcwg_ref, dcwv_ref, dcbg_ref, dcbv_ref, sum_scr):
        wg, wv, bg, bv = wg_ref[...], wv_ref[...], bg_ref[...], bv_ref[...]

        def step(i, carry):
            ng, nv, accg, accv = carry
            r0 = pl.multiple_of((nR - 1 - i) * R, R)
            tg = _taps(_ref_load(g_ref), r0, R)
            tv = _taps(_ref_load(v_ref), r0, R)
            cg = _conv(wg, bg, tg)
            cv = _conv(wv, bv, tv)
            da = da_ref[pl.ds(r0, R), :].astype(f32)
            sg = jax.nn.sigmoid(cg)
            act_ref[pl.ds(r0, R), :] = (cg * sg * cv).astype(bf16)
            dgate = da * cv * (sg * (1.0 + cg * (1.0 - sg)))
            dval = da * (cg * sg)
            dg_ref[pl.ds(r0, R), :] = _rev_conv_rows(dgate, ng, wg).astype(bf16)
            dv_ref[pl.ds(r0, R), :] = _rev_conv_rows(dval, nv, wv).astype(bf16)
            return dgate[:HALO], dval[:HALO], _conv_grad_acc(accg, dgate, tg), _conv_grad_acc(accv, dval, tv)
        z = jnp.zeros((SUBLANES, cwid), f32)
        zh = jnp.zeros((HALO, cwid), f32)
        _, _, accg, accv = lax.fori_loop(0, nR, step, (zh, zh, (z, z, z, z), (z, z, z, z)))
        j = pl.program_id(1)
        for half_i, (acc, dcw_ref, dcb_ref) in enumerate(((accg, dcwg_ref, dcbg_ref), (accv, dcwv_ref, dcbv_ref))):
            @pl.when(pl.program_id(0) == 0)
            def _():
                sum_scr[half_i, j] = jnp.zeros((SUBLANES, cwid), f32)
            for k in range(4):
                sum_scr[half_i, j, k:k + 1, :] += jnp.sum(acc[k], axis=0, keepdims=True)
            dcw_ref[...] = sum_scr[half_i, j, 0:3, :]
            dcb_ref[...] = sum_scr[half_i, j, 3:4, :]

    blk = lambda off: pl.BlockSpec((None, S, cwid), lambda b, j: (b, 0, off + j))
    wblk = lambda off: pl.BlockSpec((3, cwid), lambda b, j: (0, off + j))
    bblk = lambda off: pl.BlockSpec((1, cwid), lambda b, j: (0, off + j))
    sums = lambda r: pl.BlockSpec((r, cwid), lambda b, j: (0, jnp.where(b == B - 1, j, nF)))
    half = jax.ShapeDtypeStruct((B, S, F), bf16)
    return _call(
        body, name, (B, nF),
        [blk(0), blk(nF), blk(0), wblk(0), wblk(nF), bblk(0), bblk(nF)],
        [blk(0), blk(0), blk(0), sums(3), sums(3), sums(1), sums(1)],
        [half, half, half, jax.ShapeDtypeStruct((3, F + cwid), f32), jax.ShapeDtypeStruct((3, F + cwid), f32),
         jax.ShapeDtypeStruct((1, F + cwid), f32), jax.ShapeDtypeStruct((1, F + cwid), f32)],
        [up3, up3, da3, cw, cw, cb, cb], scratch=[pltpu.VMEM((2, nF, SUBLANES, cwid), f32)],
        sem=("arbitrary", "arbitrary"), phase=phase)


def _od_act(p3, cw, cb):
    B, S, D3 = p3.shape
    D = D3 // 3
    cwid = _pick(D, 256)
    nD = D // cwid
    R = min(256, S)

    def body(bg_ref, cg_ref, hx_ref, w_ref, b_ref, o_ref):
        w, b = w_ref[...], b_ref[...]
        q = lambda s, n: cg_ref[pl.ds(s, n), :].astype(f32) * hx_ref[pl.ds(s, n), :].astype(f32)

        def chunk(r, c):
            r0 = pl.multiple_of(r * R, R)
            cq = _conv(w, b, _taps(q, r0, R))
            o_ref[pl.ds(r0, R), :] = (bg_ref[pl.ds(r0, R), :].astype(f32) * cq).astype(bf16)
            return c
        lax.fori_loop(0, S // R, chunk, 0)

    blk = lambda off: pl.BlockSpec((None, S, cwid), lambda b, j: (b, 0, off + j))
    return pl.pallas_call(
        body, name="od_act", grid=(B, nD),
        in_specs=[blk(0), blk(nD), blk(2 * nD), pl.BlockSpec((3, cwid), lambda b, j: (0, j)),
                  pl.BlockSpec((1, cwid), lambda b, j: (0, j))],
        out_specs=pl.BlockSpec((None, S, cwid), lambda b, j: (b, 0, j)),
        out_shape=jax.ShapeDtypeStruct((B, S, D), bf16),
        compiler_params=_params(("parallel", "parallel")))(p3, p3, p3, cw, cb)


def _od_act_bwd(p3, dsc3, cw, cb):
    B, S, D3 = p3.shape
    D = D3 // 3
    cwid = _pick(D, 256)
    nD = D // cwid
    R = min(256, S)
    nR = S // R

    def body(bg_ref, cg_ref, hx_ref, d_ref, w_ref, b_ref, dbg_ref, dcg_ref, dhx_ref, dcw_ref, dcb_ref):
        w, b = w_ref[...], b_ref[...]
        q = lambda s, n: cg_ref[pl.ds(s, n), :].astype(f32) * hx_ref[pl.ds(s, n), :].astype(f32)

        def step(i, carry):
            nxt, acc = carry
            r0 = pl.multiple_of((nR - 1 - i) * R, R)
            rows = pl.ds(r0, R)
            tq = _taps(q, r0, R)
            cq = _conv(w, b, tq)
            d = d_ref[rows, :].astype(f32)
            dbg_ref[rows, :] = (d * cq).astype(bf16)
            dcq = d * bg_ref[rows, :].astype(f32)
            dq = _rev_conv_rows(dcq, nxt, w)
            dcg_ref[rows, :] = (dq * hx_ref[rows, :].astype(f32)).astype(bf16)
            dhx_ref[rows, :] = (dq * cg_ref[rows, :].astype(f32)).astype(bf16)
            return dcq[:HALO], _conv_grad_acc(acc, dcq, tq)
        z = jnp.zeros((SUBLANES, cwid), f32)
        _, acc = lax.fori_loop(0, nR, step, (jnp.zeros((HALO, cwid), f32), (z, z, z, z)))
        _conv_grad_out(dcw_ref, dcb_ref, acc)

    blk = lambda off: pl.BlockSpec((None, S, cwid), lambda j, b: (b, 0, off + j))
    part = jax.ShapeDtypeStruct((B, S, D), bf16)
    return pl.pallas_call(
        body, name="od_act_bwd", grid=(nD, B),
        in_specs=[blk(0), blk(nD), blk(2 * nD), blk(0), pl.BlockSpec((3, cwid), lambda j, b: (0, j)),
                  pl.BlockSpec((1, cwid), lambda j, b: (0, j))],
        out_specs=[blk(0), blk(0), blk(0), pl.BlockSpec((3, cwid), lambda j, b: (0, j)),
                   pl.BlockSpec((1, cwid), lambda j, b: (0, j))],
        out_shape=[part, part, part, jax.ShapeDtypeStruct((3, D), f32), jax.ShapeDtypeStruct((1, D), f32)],
        compiler_params=_params(("parallel", "arbitrary")))(p3, p3, p3, dsc3, cw, cb)


def _gmlp_parts(p, gv, SW, GW):
    uv = p[:, SW:].astype(f32)
    ge = _gelu(uv)
    u, v = ge[:, :GW], ge[:, GW:]
    vh, r = _rms_stats(v)
    return uv, u, vh, r, vh * gv


def _tril():
    rows = lax.broadcasted_iota(jnp.int32, (CHUNK, CHUNK), 0)
    cols = lax.broadcasted_iota(jnp.int32, (CHUNK, CHUNK), 1)
    return rows >= cols


def _gmlp(p0, a_out, ws, bst, gv, SW):
    T, PW = p0.shape
    GW = (PW - SW) // 2
    H = GW // GMLP_HEAD
    D = SW + GW

    def body(p_ref, a_ref, ws_ref, b_ref, gv_ref, o_ref):
        _, u, _, _, vn = _gmlp_parts(p_ref[...], gv_ref[...], SW, GW)
        tri = _tril()
        o_ref[:, :SW] = a_ref[...]
        for hh in range(H):
            sl = slice(hh * GMLP_HEAD, (hh + 1) * GMLP_HEAD)
            wm = jnp.where(tri, ws_ref[hh], 0.0).astype(bf16)
            gate = _dot(wm, vn[:, sl].astype(bf16)) + b_ref[:, hh:hh + 1]
            o_ref[:, SW + hh * GMLP_HEAD:SW + (hh + 1) * GMLP_HEAD] = (u[:, sl] * gate).astype(bf16)

    return pl.pallas_call(
        body, name="gmlp", grid=(T // CHUNK,),
        in_specs=[pl.BlockSpec((CHUNK, PW), lambda i: (i, 0)), pl.BlockSpec((CHUNK, SW), lambda i: (i, 0)),
                  pl.BlockSpec((H, CHUNK, CHUNK), lambda i: (0, 0, 0)), pl.BlockSpec((CHUNK, H), lambda i: (0, 0)),
                  pl.BlockSpec((1, GW), lambda i: (0, 0))],
        out_specs=pl.BlockSpec((CHUNK, D), lambda i: (i, 0)),
        out_shape=jax.ShapeDtypeStruct((T, D), bf16),
        compiler_params=_params(("parallel",)))(p0, a_out, ws, bst, gv)


def _gmlp_bwd(p0, dmix, ws, bst, gv, SW):
    T, PW = p0.shape
    GW = (PW - SW) // 2
    H = GW // GMLP_HEAD
    D = SW + GW

    def body(p_ref, d_ref, ws_ref, b_ref, gv_ref, duv_ref, dws_ref, dbs_ref, dgv_ref):
        gv_ = gv_ref[...]
        uv, u, vh, r, vn = _gmlp_parts(p_ref[...], gv_, SW, GW)
        dout = d_ref[...][:, SW:].astype(f32)
        tri = _tril()

        @pl.when(pl.program_id(0) == 0)
        def _():
            dws_ref[...] = jnp.zeros_like(dws_ref)
            dbs_ref[...] = jnp.zeros_like(dbs_ref)
            dgv_ref[...] = jnp.zeros_like(dgv_ref)
        du, dvn = [], []
        for hh in range(H):
            sl = slice(hh * GMLP_HEAD, (hh + 1) * GMLP_HEAD)
            wm = jnp.where(tri, ws_ref[hh], 0.0).astype(bf16)
            vnh = vn[:, sl].astype(bf16)
            gate = _dot(wm, vnh) + b_ref[:, hh:hh + 1]
            dgate = dout[:, sl] * u[:, sl]
            du.append(dout[:, sl] * gate)
            dgb = dgate.astype(bf16)
            dws_ref[hh] += jnp.where(tri, _dg(dgb, vnh, NT), 0.0)
            dbs_ref[hh] += jnp.broadcast_to(jnp.sum(dgate, axis=1, keepdims=True), (CHUNK, CHUNK))
            dvn.append(_dg(wm, dgb, TN))
        dvn = jnp.concatenate(dvn, axis=1)
        dv, dgv = _rms_bwd(dvn, vh, r, gv_)
        dgv_ref[...] += dgv
        dge = jnp.concatenate(du + [dv], axis=1)
        duv_ref[...] = (dge * _gelu_grad(uv)).astype(bf16)

    return pl.pallas_call(
        body, name="gmlp_bwd", grid=(T // CHUNK,),
        in_specs=[pl.BlockSpec((CHUNK, PW), lambda i: (i, 0)), pl.BlockSpec((CHUNK, D), lambda i: (i, 0)),
                  pl.BlockSpec((H, CHUNK, CHUNK), lambda i: (0, 0, 0)), pl.BlockSpec((CHUNK, H), lambda i: (0, 0)),
                  pl.BlockSpec((1, GW), lambda i: (0, 0))],
        out_specs=[pl.BlockSpec((CHUNK, 2 * GW), lambda i: (i, 0)), pl.BlockSpec((H, CHUNK, CHUNK), lambda i: (0, 0, 0)),
                   pl.BlockSpec((H, CHUNK, CHUNK), lambda i: (0, 0, 0)), pl.BlockSpec((1, GW), lambda i: (0, 0))],
        out_shape=[jax.ShapeDtypeStruct((T, 2 * GW), bf16), jax.ShapeDtypeStruct((H, CHUNK, CHUNK), f32),
                   jax.ShapeDtypeStruct((H, CHUNK, CHUNK), f32), jax.ShapeDtypeStruct((1, GW), f32)],
        compiler_params=_params(("arbitrary",)))(p0, dmix, ws, bst, gv)


def _s5_disc(lr, li, ldt):
    lr = jnp.minimum(lr, LAMBDA_RE_MAX)
    dt = jnp.exp(ldt)
    mag = jnp.exp(lr * dt)
    ar = mag * jnp.cos(li * dt)
    ai = mag * jnp.sin(li * dt)
    den = lr * lr + li * li
    nr = ar - 1.0
    zr = (nr * lr + ai * li) / den
    zi = (ai * lr - nr * li) / den
    return ar, ai, zr, zi


def _s5_prep(lr, li, ldt):
    G, P = lr.shape

    def body(lr_ref, li_ref, ldt_ref, ar_ref, ai_ref, zr_ref, zi_ref):
        ar, ai, zr, zi = _s5_disc(lr_ref[...], li_ref[...], ldt_ref[...])
        ar_ref[...] = ar
        ai_ref[...] = ai
        zr_ref[...] = zr
        zi_ref[...] = zi

    s = jax.ShapeDtypeStruct((G, P), f32)
    return pl.pallas_call(body, name="s5_prep", out_shape=[s, s, s, s])(lr, li, ldt)


def _s5_prep_bwd(lr, li, ldt, dar, dai, dzr, dzi):
    G, P = lr.shape

    def body(lr_ref, li_ref, ldt_ref, dar_ref, dai_ref, dzr_ref, dzi_ref, o1, o2, o3):
        _, vjp = jax.vjp(_s5_disc, lr_ref[...], li_ref[...], ldt_ref[...])
        cts = tuple(jnp.sum(r[...], axis=0) for r in (dar_ref, dai_ref, dzr_ref, dzi_ref))
        a, b, c = vjp(cts)
        o1[...] = a
        o2[...] = b
        o3[...] = c

    s = jax.ShapeDtypeStruct((G, P), f32)
    return pl.pallas_call(body, name="s5_prep_bwd", out_shape=[s, s, jax.ShapeDtypeStruct((G, 1), f32)])(
        lr, li, ldt, dar, dai, dzr, dzi)


def _s5_bbd(zr, zi, bre, bim):
    SW, NS = bre.shape

    def body(zr_ref, zi_ref, br_ref, bi_ref, o_ref):
        zr_, zi_, br, bi = zr_ref[...], zi_ref[...], br_ref[...], bi_ref[...]
        o_ref[:, :NS] = (zr_ * br - zi_ * bi).astype(bf16)
        o_ref[:, NS:] = (zr_ * bi + zi_ * br).astype(bf16)

    return pl.pallas_call(body, name="s5_bbd", out_shape=jax.ShapeDtypeStruct((SW, 2 * NS), bf16))(zr, zi, bre, bim)


def _s5_bbd_bwd(dbbd, zr, zi, bre, bim):
    SW, NS = bre.shape

    def body(d_ref, zr_ref, zi_ref, br_ref, bi_ref, dbr_ref, dbi_ref, dzr_ref, dzi_ref):
        zr_, zi_, br, bi = zr_ref[...], zi_ref[...], br_ref[...], bi_ref[...]
        dr, di = d_ref[:, :NS], d_ref[:, NS:]
        dbr_ref[...] = zr_ * dr + zi_ * di
        dbi_ref[...] = zr_ * di - zi_ * dr
        dzr_ref[...] = jnp.sum(dr * br + di * bi, axis=0, keepdims=True)
        dzi_ref[...] = jnp.sum(di * br - dr * bi, axis=0, keepdims=True)

    m = jax.ShapeDtypeStruct((SW, NS), f32)
    v = jax.ShapeDtypeStruct((1, NS), f32)
    return pl.pallas_call(body, name="s5_bbd_bwd", out_shape=[m, m, v, v])(dbbd, zr, zi, bre, bim)


def _slab_cat(ref, NB):
    return jnp.concatenate([ref[j] for j in range(NB)], axis=1)


def _s5_in(p3, bbd, SW, tm):
    B, S, PW = p3.shape
    NS = bbd.shape[1] // 2
    NB = NS // LANES

    def body(u_ref, b_ref, xr_ref, xi_ref):
        x = _dot(u_ref[...], b_ref[...])
        for j in range(NB):
            xr_ref[j] = x[:, j * LANES:(j + 1) * LANES]
            xi_ref[j] = x[:, NS + j * LANES:NS + (j + 1) * LANES]

    slab = jax.ShapeDtypeStruct((B, NB, S, LANES), f32)
    sspec = pl.BlockSpec((None, NB, tm, LANES), lambda b, i: (b, 0, i, 0))
    return pl.pallas_call(
        body, name="s5_in", grid=(B, S // tm),
        in_specs=[pl.BlockSpec((None, tm, SW), lambda b, i: (b, i, 0)), pl.BlockSpec((SW, 2 * NS), lambda b, i: (0, 0))],
        out_specs=[sspec, sspec], out_shape=[slab, slab],
        compiler_params=_params(("parallel", "parallel")))(p3, bbd)


def _s5_scan(name, xr, xi, ar, ai, reverse, hr=None, hi=None, phase=None):
    B, NB, S, _ = xr.shape
    L = S // NSUB
    nb = 2 if (hr is None and NB % 2 == 0) else 1
    with_da = hr is not None

    def body(*refs):
        if with_da:
            xr_ref, xi_ref, ar_ref, ai_ref, hr_ref, hi_ref, or_ref, oi_ref, dar_ref, dai_ref, pr_scr, pi_scr = refs
        else:
            xr_ref, xi_ref, ar_ref, ai_ref, or_ref, oi_ref, pr_scr, pi_scr = refs
        sign = -1.0 if reverse else 1.0
        a_r = [jnp.broadcast_to(ar_ref[j], (NSUB, LANES)) for j in range(nb)]
        a_i = [jnp.broadcast_to(ai_ref[j], (NSUB, LANES)) * sign for j in range(nb)]

        def step(t, carry):
            row = (L - 1 - t) if reverse else t
            rows = pl.ds(row, NSUB, stride=L)
            out = []
            for j in range(nb):
                sr, si, pr, pi = carry[j]
                nr = a_r[j] * sr - a_i[j] * si + xr_ref.at[j][rows, :]
                ni = a_r[j] * si + a_i[j] * sr + xi_ref.at[j][rows, :]
                or_ref.at[j][rows, :] = nr
                oi_ref.at[j][rows, :] = ni
                npr = a_r[j] * pr - a_i[j] * pi
                npi = a_r[j] * pi + a_i[j] * pr
                pr_scr[j, pl.ds(row, 1), :] = npr[0:1]
                pi_scr[j, pl.ds(row, 1), :] = npi[0:1]
                out.append((nr, ni, npr, npi))
            return tuple(out)
        z = jnp.zeros((NSUB, LANES), f32)
        one = jnp.ones((NSUB, LANES), f32)
        fin = lax.fori_loop(0, L, step, tuple((z, z, one, z) for _ in range(nb)))

        for j in range(nb):
            sr, si, plr, pli = fin[j]
            plr, pli = plr[0:1], pli[0:1]
            cr = jnp.zeros((1, LANES), f32)
            ci = jnp.zeros((1, LANES), f32)
            order = range(NSUB - 2, -1, -1) if reverse else range(1, NSUB)
            for c in order:
                src = c + 1 if reverse else c - 1
                cr, ci = (sr[src:src + 1] + plr * cr - pli * ci, si[src:src + 1] + plr * ci + pli * cr)
                rows = slice(c * L, (c + 1) * L)
                tr, ti = pr_scr[j], pi_scr[j]
                or_ref[j, rows, :] += tr * cr - ti * ci
                oi_ref[j, rows, :] += tr * ci + ti * cr
            if with_da:
                first = lax.broadcasted_iota(jnp.int32, (L, LANES), 0) == 0
                dar = jnp.zeros((1, LANES), f32)
                dai = jnp.zeros((1, LANES), f32)
                for c in range(NSUB):
                    rows = slice(c * L, (c + 1) * L)
                    if c == 0:
                        lr_, li_ = jnp.zeros((1, LANES), f32), jnp.zeros((1, LANES), f32)
                    else:
                        lr_, li_ = hr_ref[j, c * L - 1:c * L, :], hi_ref[j, c * L - 1:c * L, :]
                    hpr = jnp.where(first, lr_, pltpu.roll(hr_ref[j, rows, :], 1, 0))
                    hpi = jnp.where(first, li_, pltpu.roll(hi_ref[j, rows, :], 1, 0))
                    gr, gi = or_ref[j, rows, :], oi_ref[j, rows, :]
                    dar += jnp.sum(hpr * gr + hpi * gi, axis=0, keepdims=True)
                    dai += jnp.sum(hpr * gi - hpi * gr, axis=0, keepdims=True)
                dar_ref[j] = dar
                dai_ref[j] = dai

    slab = jax.ShapeDtypeStruct((B, NB, S, LANES), f32)
    sspec = pl.BlockSpec((None, nb, S, LANES), lambda b, j: (b, j, 0, 0))
    aspec = pl.BlockSpec((nb, 1, LANES), lambda b, j: (j, 0, 0))
    in_specs = [sspec, sspec, aspec, aspec]
    out_specs = [sspec, sspec]
    out_shape = [slab, slab]
    args = [xr, xi, ar, ai]
    if with_da:
        in_specs += [sspec, sspec]
        args += [hr, hi]
        dspec = pl.BlockSpec((None, nb, 1, LANES), lambda b, j: (b, j, 0, 0))
        out_specs += [dspec, dspec]
        out_shape += [jax.ShapeDtypeStruct((B, NB, 1, LANES), f32)] * 2
    return _call(body, name, (B, NB // nb), in_specs, out_specs, out_shape, args,
                 scratch=[pltpu.VMEM((nb, L, LANES), f32), pltpu.VMEM((nb, L, LANES), f32)],
                 sem=("parallel", "parallel"), phase=phase)


def _s5_out_parts(hr_ref, hi_ref, u_ref, cr_ref, ci_ref, d_ref, wg_ref, bg_ref, NB):
    hcr = _slab_cat(hr_ref, NB).astype(bf16)
    hci = _slab_cat(hi_ref, NB).astype(bf16)
    u = u_ref[...].astype(f32)
    y2 = _dot(hcr, cr_ref[...]) - _dot(hci, ci_ref[...]) + d_ref[...] * u
    yg = _gelu(y2)
    s = jax.nn.sigmoid(_dot(yg.astype(bf16), wg_ref[...]) + bg_ref[...])
    return hcr, hci, u, y2, yg, s


def _s5_out_specs(B, S, NB, NS, SW, tm):
    sspec = pl.BlockSpec((None, NB, tm, LANES), lambda b, i: (b, 0, i, 0))
    full = lambda r, c: pl.BlockSpec((r, c), lambda b, i: (0, 0))
    return sspec, [sspec, sspec, pl.BlockSpec((None, tm, SW), lambda b, i: (b, i, 0)), full(NS, SW), full(NS, SW),
                   full(1, SW), full(SW, SW), full(1, SW)]


def _s5_out(hr, hi, p3, cbr, cbi, dsk, wglu, bglu, tm):
    B, NB, S, _ = hr.shape
    NS, SW = cbr.shape

    def body(hr_ref, hi_ref, u_ref, cr_ref, ci_ref, d_ref, wg_ref, bg_ref, o_ref):
        _, _, _, _, yg, s = _s5_out_parts(hr_ref, hi_ref, u_ref, cr_ref, ci_ref, d_ref, wg_ref, bg_ref, NB)
        o_ref[...] = (yg * s).astype(bf16)

    _, in_specs = _s5_out_specs(B, S, NB, NS, SW, tm)
    return pl.pallas_call(
        body, name="s5_out", grid=(B, S // tm), in_specs=in_specs,
        out_specs=pl.BlockSpec((None, tm, SW), lambda b, i: (b, i, 0)),
        out_shape=jax.ShapeDtypeStruct((B, S, SW), bf16),
        compiler_params=_params(("parallel", "parallel")))(hr, hi, p3, cbr, cbi, dsk, wglu, bglu)


def _s5_out_bwd(hr, hi, p3, dmix3, cbr, cbi, dsk, wglu, bglu, tm):
    B, NB, S, _ = hr.shape
    NS, SW = cbr.shape

    def body(hr_ref, hi_ref, u_ref, cr_ref, ci_ref, d_ref, wg_ref, bg_ref, da_ref,
             dhr_ref, dhi_ref, du_ref, dcr_ref, dci_ref, dd_ref, dwg_ref, dbg_ref):
        hcr, hci, u, y2, yg, s = _s5_out_parts(hr_ref, hi_ref, u_ref, cr_ref, ci_ref, d_ref, wg_ref, bg_ref, NB)
        da = da_ref[...].astype(f32)
        dz = da * yg * s * (1.0 - s)
        dzb = dz.astype(bf16)
        dyg = da * s + _dg(dzb, wg_ref[...], NT)
        dy2 = dyg * _gelu_grad(y2)
        dyb = dy2.astype(bf16)

        @pl.when((pl.program_id(0) == 0) & (pl.program_id(1) == 0))
        def _():
            for r in (dcr_ref, dci_ref, dd_ref, dwg_ref, dbg_ref):
                r[...] = jnp.zeros_like(r)
        dwg_ref[...] += _dg(yg.astype(bf16), dzb, TN)
        dbg_ref[...] += jnp.sum(dz, axis=0, keepdims=True)
        dd_ref[...] += jnp.sum(dy2 * u, axis=0, keepdims=True)
        dcr_ref[...] += _dg(hcr, dyb, TN)
        dci_ref[...] -= _dg(hci, dyb, TN)
        du_ref[...] = dy2 * d_ref[...]
        dhr = _dg(dyb, cr_ref[...], NT)
        dhi = _dg(dyb, ci_ref[...], NT)
        for j in range(NB):
            dhr_ref[j] = dhr[:, j * LANES:(j + 1) * LANES]
            dhi_ref[j] = -dhi[:, j * LANES:(j + 1) * LANES]

    sspec, in_specs = _s5_out_specs(B, S, NB, NS, SW, tm)
    in_specs = in_specs + [pl.BlockSpec((None, tm, SW), lambda b, i: (b, i, 0))]
    full = lambda r, c: pl.BlockSpec((r, c), lambda b, i: (0, 0))
    slab = jax.ShapeDtypeStruct((B, NB, S, LANES), f32)
    mat = lambda r, c: jax.ShapeDtypeStruct((r, c), f32)
    return pl.pallas_call(
        body, name="s5_out_bwd", grid=(B, S // tm), in_specs=in_specs,
        out_specs=[sspec, sspec, pl.BlockSpec((None, tm, SW), lambda b, i: (b, i, 0)), full(NS, SW), full(NS, SW),
                   full(1, SW), full(SW, SW), full(1, SW)],
        out_shape=[slab, slab, jax.ShapeDtypeStruct((B, S, SW), f32), mat(NS, SW), mat(NS, SW), mat(1, SW),
                   mat(SW, SW), mat(1, SW)],
        compiler_params=_params(("arbitrary", "arbitrary")))(hr, hi, p3, cbr, cbi, dsk, wglu, bglu, dmix3)


def _s5_in_bwd(gr, gi, p3, bbd, du_skip, duv3, tm):
    B, NB, S, _ = gr.shape
    SW, NS2 = bbd.shape
    PW = SW + duv3.shape[2]

    def body(gr_ref, gi_ref, u_ref, b_ref, ds_ref, duv_ref, dp_ref, db_ref):
        g = jnp.concatenate([_slab_cat(gr_ref, NB), _slab_cat(gi_ref, NB)], axis=1).astype(bf16)
        du = _dg(g, b_ref[...], NT) + ds_ref[...]
        dp_ref[:, :SW] = du.astype(bf16)
        dp_ref[:, SW:] = duv_ref[...]

        @pl.when((pl.program_id(0) == 0) & (pl.program_id(1) == 0))
        def _():
            db_ref[...] = jnp.zeros_like(db_ref)
        db_ref[...] += _dg(u_ref[...], g, TN)

    sspec = pl.BlockSpec((None, NB, tm, LANES), lambda b, i: (b, 0, i, 0))
    row = lambda c: pl.BlockSpec((None, tm, c), lambda b, i: (b, i, 0))
    return pl.pallas_call(
        body, name="s5_in_bwd", grid=(B, S // tm),
        in_specs=[sspec, sspec, row(SW), pl.BlockSpec((SW, NS2), lambda b, i: (0, 0)), row(SW), row(PW - SW)],
        out_specs=[row(PW), pl.BlockSpec((SW, NS2), lambda b, i: (0, 0))],
        out_shape=[jax.ShapeDtypeStruct((B, S, PW), bf16), jax.ShapeDtypeStruct((SW, NS2), f32)],
        compiler_params=_params(("arbitrary", "arbitrary")))(gr, gi, p3, bbd, du_skip, duv3)


BIG = ['ev_w_in', 'ev_w_out', 'od_w_in', 'od_w_out', 'ffn_w_up', 'ffn_w_down']
ANY = pl.BlockSpec(memory_space=pl.ANY)


def _rtile(rows, mult):
    best = None
    for d in range(mult, min(rows, 512) + 1, mult):
        if rows % d == 0:
            best = d
    assert best is not None, (rows, mult)
    return best


def _pair_sum(name, g, recv, c_idx, out_dtype):
    NCH, R, W = g.shape
    HALF_W = W // 2
    tr = _rtile(R, 16)

    def body(c_ref, a_ref, b_ref, o_ref):
        o_ref[...] = (a_ref[...] + b_ref[...]).astype(out_dtype)

    return pl.pallas_call(
        body, name=name,
        grid_spec=pltpu.PrefetchScalarGridSpec(
            num_scalar_prefetch=1, grid=(NCH, R // tr),
            in_specs=[pl.BlockSpec((None, tr, HALF_W), lambda j, i, c: (j, i, c[0])),
                      pl.BlockSpec((None, tr, HALF_W), lambda j, i, c: (j, i, 0))],
            out_specs=pl.BlockSpec((None, tr, HALF_W), lambda j, i, c: (j, i, 0))),
        out_shape=jax.ShapeDtypeStruct((NCH, R, HALF_W), out_dtype),
        compiler_params=_params(("parallel", "parallel")))(c_idx, g, recv)


def _chip_sum(name, r3, h, k_idx):
    NCH, R, Wh = r3.shape
    tr = _rtile(R, 16)

    def body(k_ref, a_ref, own_ref, o_ref):
        own = own_ref[...].astype(f32)
        t = [jnp.where(k_ref[0] == s, own, a_ref[s].astype(f32)) for s in range(NCH)]
        o_ref[...] = ((t[0] + t[1]) + t[2]) + t[3]

    return pl.pallas_call(
        body, name=name,
        grid_spec=pltpu.PrefetchScalarGridSpec(
            num_scalar_prefetch=1, grid=(R // tr,),
            in_specs=[pl.BlockSpec((NCH, tr, Wh), lambda i, k: (0, i, 0)),
                      pl.BlockSpec((None, tr, Wh), lambda i, k: (k[0], i, 0))],
            out_specs=pl.BlockSpec((tr, Wh), lambda i, k: (i, 0))),
        out_shape=jax.ShapeDtypeStruct((R, Wh), f32),
        compiler_params=_params(("parallel",)))(k_idx, r3, h)


def _adam_math(gg, w, m, v):
    nm = ADAM_B1 * m + (1.0 - ADAM_B1) * gg
    nv = ADAM_B2 * v + (1.0 - ADAM_B2) * jnp.square(gg)
    m_hat = nm / (1.0 - ADAM_B1 ** ADAM_STEP)
    v_hat = nv / (1.0 - ADAM_B2 ** ADAM_STEP)
    return -ADAM_LR * (m_hat / (jnp.sqrt(v_hat) + ADAM_EPS) + ADAM_WD * w), nm, nv


def _adamw(name, mine, theirs, c_idx, w, m, v, lead, transposed, prev=None):
    L, R, W = w.shape
    if transposed:
        bw = LANES if W % LANES == 0 else W
        gspec = pl.BlockSpec((bw, R // 2), lambda i, hf, c: (i, 0))
        wspec = pl.BlockSpec((None, R // 2, bw), lambda i, hf, c: (lead, hf, i))
        grid = (W // bw, 2)
    else:
        tr = _rtile(R, SUBLANES)
        gspec = pl.BlockSpec((tr, W // 2), lambda i, hf, c: (i, 0))
        wspec = pl.BlockSpec((None, tr, W // 2), lambda i, hf, c: (lead, i, hf))
        grid = (R // tr, 2)

    def body(c_ref, a_ref, b_ref, w_ref, m_ref, v_ref, *rest):
        go_ref, d_ref, nm_ref, nv_ref = rest[-4:]
        gg = jnp.where(pl.program_id(1) == c_ref[0], a_ref[...], b_ref[...])
        if transposed:
            gg = gg.T
        d, nm, nv = _adam_math(gg, w_ref[...], m_ref[...], v_ref[...])
        go_ref[...] = gg
        d_ref[...] = d
        nm_ref[...] = nm
        nv_ref[...] = nv

    in_specs = [gspec, gspec, wspec, wspec, wspec]
    args, aliases = [c_idx, mine, theirs, w, m, v], {}
    if prev is not None:
        in_specs += [ANY] * 4
        args += list(prev)
        aliases = {6: 0, 7: 1, 8: 2, 9: 3}
    s = jax.ShapeDtypeStruct((L, R, W), f32)
    return pl.pallas_call(
        body, name=name,
        grid_spec=pltpu.PrefetchScalarGridSpec(num_scalar_prefetch=1, grid=grid, in_specs=in_specs,
                                               out_specs=[wspec] * 4),
        out_shape=[s, s, s, s], input_output_aliases=aliases,
        compiler_params=_params(("parallel", "arbitrary")))(*args)


def _adamw_small(gs, ws, ms, vs):
    n = len(gs)

    def body(*refs):
        for i in range(n):
            d, nm, nv = _adam_math(refs[i][...], refs[n + i][...], refs[2 * n + i][...], refs[3 * n + i][...])
            refs[4 * n + i][...] = d
            refs[5 * n + i][...] = nm
            refs[6 * n + i][...] = nv

    return pl.pallas_call(body, name="adamw_small",
                          out_shape=[jax.ShapeDtypeStruct(t.shape, f32) for t in ws] * 3)(*gs, *ws, *ms, *vs)


def _place():
    x, y, c = lax.axis_index("x"), lax.axis_index("y"), lax.axis_index("c")
    return x, y, c, [(1 - x, y), (x, 1 - y), (1 - x, 1 - y)]


def _gathered_shape(sh, kind):
    if kind == "rows":
        return sh[:-2] + (N_CHIPS * sh[-2], sh[-1])
    if kind == "cols":
        return sh[:-1] + (N_CHIPS * sh[-1],)
    return (N_CHIPS,) + sh


def _place_shard(name, shard, kind, k_idx):
    sh = shard.shape
    r, C = sh[-2], sh[-1]
    L = sh[0] if len(sh) == 3 else 1
    tr = _rtile(r, 16)
    nr = r // tr
    if kind == "rows":
        out3, omap = (L, N_CHIPS * r, C), lambda l, i, k: (l, k[0] * nr + i, 0)
    elif kind == "cols":
        out3, omap = (L, r, N_CHIPS * C), lambda l, i, k: (l, i, k[0])
    else:
        out3, omap = (N_CHIPS, r, C), lambda l, i, k: (k[0], i, 0)

    def body(k_ref, s_ref, o_ref):
        o_ref[...] = s_ref[...]

    out = pl.pallas_call(
        body, name=name,
        grid_spec=pltpu.PrefetchScalarGridSpec(
            num_scalar_prefetch=1, grid=(L, nr),
            in_specs=[pl.BlockSpec((None, tr, C), lambda l, i, k: (l, i, 0))],
            out_specs=pl.BlockSpec((None, tr, C), omap)),
        out_shape=jax.ShapeDtypeStruct(out3, shard.dtype),
        compiler_params=_params(("parallel", "parallel")))(k_idx, shard.reshape(L, r, C))
    return out.reshape(_gathered_shape(sh, kind))


def _gather_phase(shards, fulls, kinds):
    n = len(shards)
    shapes = [s.shape for s in shards]

    def window(ref, a, k, h=None):
        sh, kind = shapes[a], kinds[a]
        r = sh[-2]
        start, size = (0, r) if h is None else (h * (r // 2), r // 2)
        lead = (slice(None),) * (len(sh) - 2)
        if kind == "rows":
            return ref.at[lead + (pl.ds(k * r + start, size), slice(None))]
        if kind == "cols":
            return ref.at[lead + (pl.ds(start, size), pl.ds(pl.multiple_of(k * sh[-1], LANES), sh[-1]))]
        return ref.at[(k,) + lead + (pl.ds(start, size), slice(None))]

    def copies(s_refs, o_refs, sems):
        send_sems, recv_sems = sems
        x, y, c, chips = _place()
        k = 2 * x + y

        def copy(a, j, kk, hh, to, src=None):
            dst = window(o_refs[a], a, kk, hh)
            return pltpu.make_async_remote_copy(
                src_ref=dst if src is None else src, dst_ref=dst, send_sem=send_sems.at[6 * a + j],
                recv_sem=recv_sems.at[6 * a + j], device_id=to, device_id_type=MESH)

        first = []
        for a in range(n):
            r = shapes[a][-2]
            lead = (slice(None),) * (len(shapes[a]) - 2)
            src = s_refs[a].at[lead + (pl.ds(c * (r // 2), r // 2), slice(None))]
            first += [copy(a, j, k, c, (*chip, c), src=src) for j, chip in enumerate(chips)]
        return copy, first, (x, y, c), (x, y, 1 - c), c, chips

    def start(s_refs, o_refs, sems):
        for cp in copies(s_refs, o_refs, sems)[1]:
            cp.start()

    def finish(s_refs, o_refs, sems):
        copy, first, me, sibling, c, chips = copies(s_refs, o_refs, sems)
        passed = []
        for j, (cx, cy) in enumerate(chips):
            for a in range(n):
                copy(a, j, 2 * cx + cy, c, me).wait_recv()
                fwd = copy(a, 3 + j, 2 * cx + cy, c, sibling)
                fwd.start()
                passed.append(fwd)
        for j, (cx, cy) in enumerate(chips):
            for a in range(n):
                copy(a, 3 + j, 2 * cx + cy, 1 - c, me).wait_recv()
        for cp in first + passed:
            cp.wait_send()

    return _Phase(shards, fulls, [jax.ShapeDtypeStruct(f.shape, f.dtype) for f in fulls],
                  [pltpu.SemaphoreType.DMA((6 * n,)), pltpu.SemaphoreType.DMA((6 * n,))], start, finish)


def _swap_phase(gs):
    n = len(gs)

    def copies(g_refs, o_refs, sems):
        send_sems, recv_sems = sems
        x, y, c, _ = _place()
        half = [g.shape[2] // 2 for g in gs]
        return [pltpu.make_async_remote_copy(
            src_ref=g_refs[a].at[:, :, pl.ds(pl.multiple_of((1 - c) * half[a], LANES), half[a])], dst_ref=o_refs[a],
            send_sem=send_sems.at[a], recv_sem=recv_sems.at[a], device_id=(x, y, 1 - c), device_id_type=MESH)
            for a in range(n)]

    def start(g_refs, o_refs, sems):
        for cp in copies(g_refs, o_refs, sems):
            cp.start()

    def finish(g_refs, o_refs, sems):
        for cp in copies(g_refs, o_refs, sems):
            cp.wait()

    return _Phase(gs, [], [jax.ShapeDtypeStruct(g.shape[:2] + (g.shape[2] // 2,), g.dtype) for g in gs],
                  [pltpu.SemaphoreType.DMA((n,)), pltpu.SemaphoreType.DMA((n,))], start, finish)


def _exchange_phase(hs):
    n = len(hs)

    def copies(h_refs, o_refs, sems):
        send_sems, recv_sems = sems
        x, y, c, chips = _place()
        k = 2 * x + y

        def copy(a, j, src_slot, dst_slot):
            cx, cy = chips[j]
            return pltpu.make_async_remote_copy(
                src_ref=h_refs[a].at[src_slot], dst_ref=o_refs[a].at[dst_slot], send_sem=send_sems.at[3 * a + j],
                recv_sem=recv_sems.at[3 * a + j], device_id=(cx, cy, c), device_id_type=MESH)

        sends = [copy(a, j, 2 * cx + cy, k) for a in range(n) for j, (cx, cy) in enumerate(chips)]
        return copy, sends, k, chips

    def start(h_refs, o_refs, sems):
        for cp in copies(h_refs, o_refs, sems)[1]:
            cp.start()

    def finish(h_refs, o_refs, sems):
        copy, sends, k, chips = copies(h_refs, o_refs, sems)
        for a in range(n):
            for j, (cx, cy) in enumerate(chips):
                copy(a, j, k, 2 * cx + cy).wait_recv()
        for cp in sends:
            cp.wait_send()

    return _Phase(hs, [], [jax.ShapeDtypeStruct(h.shape, h.dtype) for h in hs],
                  [pltpu.SemaphoreType.DMA((3 * n,)), pltpu.SemaphoreType.DMA((3 * n,))], start, finish)


def _comm_pair_share(tag, gs):
    n = len(gs)

    def body(*refs):
        g_refs, o_refs, send_sems, recv_sems = refs[:n], refs[n:2 * n], refs[2 * n], refs[2 * n + 1]
        x, y, c, _ = _place()
        cps = [pltpu.make_async_remote_copy(
            src_ref=g_refs[a], dst_ref=o_refs[a], send_sem=send_sems.at[a], recv_sem=recv_sems.at[a],
            device_id=(x, y, 1 - c), device_id_type=MESH) for a in range(n)]
        for cp in cps:
            cp.start()
        for cp in cps:
            cp.wait()

    return pl.pallas_call(
        body, name="comm_pair_share_" + tag, in_specs=[ANY] * n, out_specs=[ANY] * n,
        out_shape=[jax.ShapeDtypeStruct(g.shape, g.dtype) for g in gs],
        scratch_shapes=[pltpu.SemaphoreType.DMA((n,)), pltpu.SemaphoreType.DMA((n,))])(*gs)


def _pad_rows(flat, unit):
    n = flat.shape[-1]
    pad = (-n) % unit
    if pad:
        flat = jnp.pad(flat, [(0, 0)] * (flat.ndim - 1) + [(0, pad)])
    return flat


def _split_chips(full, axis):
    sh = full.shape
    t = full.reshape(sh[:axis] + (N_CHIPS, sh[axis] // N_CHIPS) + sh[axis + 1:])
    return jnp.moveaxis(t, axis, 0).reshape(N_CHIPS, -1)


def _join_chips(stack, shard_shape, axis):
    t = jnp.moveaxis(stack.reshape((N_CHIPS,) + tuple(shard_shape)), 0, axis)
    sh = t.shape
    return t.reshape(sh[:axis] + (sh[axis] * sh[axis + 1],) + sh[axis + 2:])


def _block_diag(blocks):
    G, r, c = blocks.shape
    eye = jnp.eye(G, dtype=blocks.dtype)
    return (blocks[:, :, None, :] * eye[:, None, :, None]).reshape(G * r, G * c)


def _diag_blocks(m, G):
    r, c = m.shape[0] // G, m.shape[1] // G
    idx = jnp.arange(G)
    return m.reshape(G, r, G, c)[idx, :, idx, :]


def _weight_shards(w):
    conv = jnp.concatenate([w[n].reshape(-1) for n in GATHER_F32])
    conv = _pad_rows(conv, 2 * SUBLANES * LANES).reshape(-1, LANES)
    b16 = lambda a: a.astype(bf16)
    return {'ev_w_in': (b16(w['ev_w_in'][0]), "chip"), 'ev_w_out': (b16(w['ev_w_out'][0]), "rows"),
            's5_w_glu': (b16(w['s5_w_glu'][0]), "rows"), 'conv': (conv, "chip"),
            'od_w_in': (b16(w['od_w_in'][0]), "cols"), 'od_w_out': (b16(w['od_w_out'][0]), "rows"),
            'ffn_w_up0': (b16(w['ffn_w_up'][0]), "cols"), 'ffn_w_up1': (b16(w['ffn_w_up'][1]), "cols"),
            'ffn_w_down0': (b16(w['ffn_w_down'][0]), "rows"), 'ffn_w_down1': (b16(w['ffn_w_down'][1]), "rows")}


def kernel(x, mix_norm_g, ffn_norm_g, final_norm_g, ev_w_in, ev_w_out, s5_lam_re, s5_lam_im, s5_log_dt, s5_b_re, s5_b_im, s5_c_re, s5_c_im, s5_d, s5_w_glu, s5_b_glu, gm_w_s, gm_b_s, gm_v_g, od_w_in, od_conv_w, od_conv_b, od_w_out, ffn_w_up, ffn_conv_w, ffn_conv_b, ffn_w_down, loss_target, m_mix_norm_g, m_ffn_norm_g, m_final_norm_g, m_ev_w_in, m_ev_w_out, m_s5_lam_re, m_s5_lam_im, m_s5_log_dt, m_s5_b_re, m_s5_b_im, m_s5_c_re, m_s5_c_im, m_s5_d, m_s5_w_glu, m_s5_b_glu, m_gm_w_s, m_gm_b_s, m_gm_v_g, m_od_w_in, m_od_conv_w, m_od_conv_b, m_od_w_out, m_ffn_w_up, m_ffn_conv_w, m_ffn_conv_b, m_ffn_w_down, v_mix_norm_g, v_ffn_norm_g, v_final_norm_g, v_ev_w_in, v_ev_w_out, v_s5_lam_re, v_s5_lam_im, v_s5_log_dt, v_s5_b_re, v_s5_b_im, v_s5_c_re, v_s5_c_im, v_s5_d, v_s5_w_glu, v_s5_b_glu, v_gm_w_s, v_gm_b_s, v_gm_v_g, v_od_w_in, v_od_conv_w, v_od_conv_b, v_od_w_out, v_ffn_w_up, v_ffn_conv_w, v_ffn_conv_b, v_ffn_w_down):
    loc = dict(locals())
    w = {n: loc[n] for n in WEIGHTS}
    mom = {n: loc["m_" + n] for n in WEIGHTS}
    var = {n: loc["v_" + n] for n in WEIGHTS}

    B, S, D = x.shape
    T = B * S
    SW = s5_d.shape[1]
    G = SW // SSM_GROUP
    NS = G * SSM_STATE
    NB = NS // LANES
    tm = min(512, S)
    tt = min(1024, T)
    c_idx = lax.axis_index("c").astype(jnp.int32).reshape(1)
    k_idx = (2 * lax.axis_index("x") + lax.axis_index("y")).astype(jnp.int32).reshape(1)
    shards = _weight_shards(w)
    placed = {n: _place_shard("place_" + n, s, kd, k_idx) for n, (s, kd) in shards.items()}

    def gather(names):
        return _gather_phase([shards[n][0] for n in names], [placed[n] for n in names], [shards[n][1] for n in names])

    (w_ev_in,) = _run_phase("comm_gather_ev_in", gather(['ev_w_in']))
    w_ev_in = jnp.swapaxes(w_ev_in, 0, 1).reshape(D, -1)

    h0 = x.reshape(T, D)
    (y0, p0), (w_ev_out, w_glu, conv) = _norm_mm("ev_in", h0, mix_norm_g[0], w_ev_in, tm,
                                                 phase=gather(['ev_w_out', 's5_w_glu', 'conv']))
    full, off = {}, 0
    for n in GATHER_F32:
        full[n] = _join_chips(conv.reshape(N_CHIPS, -1)[:, off:off + w[n].size], w[n].shape, SHARD_AXIS[n])
        off += w[n].size
    PW = p0.shape[1]
    p03 = p0.reshape(B, S, PW)
    lr, li, ldt = s5_lam_re[0], s5_lam_im[0], s5_log_dt[0].reshape(G, 1)
    ar, ai, zr, zi = _s5_prep(lr, li, ldt)
    bre = _block_diag(jnp.swapaxes(s5_b_re[0], 1, 2))
    bim = _block_diag(jnp.swapaxes(s5_b_im[0], 1, 2))
    cbr = _block_diag(jnp.swapaxes(s5_c_re[0], 1, 2)).astype(bf16)
    cbi = _block_diag(jnp.swapaxes(s5_c_im[0], 1, 2)).astype(bf16)
    zr_row, zi_row = zr.reshape(1, NS), zi.reshape(1, NS)
    bbd = _s5_bbd(zr_row, zi_row, bre, bim)
    ar_s, ai_s = ar.reshape(NB, 1, LANES), ai.reshape(NB, 1, LANES)
    xr, xi = _s5_in(p03, bbd, SW, tm)
    (hr, hi), (w_up0, w_down0) = _s5_scan("s5_scan", xr, xi, ar_s, ai_s, False,
                                           phase=gather(['ffn_w_up0', 'ffn_w_down0']))
    dsk, bglu = s5_d.reshape(1, SW), s5_b_glu.reshape(1, SW)
    a_out = _s5_out(hr, hi, p03, cbr, cbi, dsk, w_glu, bglu, tm)
    ws, bst, gv = gm_w_s[0], gm_b_s[0].T, gm_v_g.reshape(1, -1)
    mixcat = _gmlp(p0, a_out.reshape(T, SW), ws, bst, gv, SW)
    h1 = _mm_resid("ev_out", mixcat, w_ev_out, h0, tm)

    def ffn_fwd(l, h, w_up, w_down, up_phase=None, down_phase=None):
        res = _norm_mm(f"ffn_up{l}", h, ffn_norm_g[l], w_up, tm, phase=up_phase)
        (z, up), got_up = res if up_phase is not None else (res, None)
        res = _ffn_down(f"ffn_down{l}", up, full['ffn_conv_w'][l], ffn_conv_b[l].reshape(1, -1), w_down, h, S, tm,
                        phase=down_phase)
        (hn,), got_down = res if down_phase is not None else (res, None)
        return hn, (z, up.reshape(B, S, -1)), got_up, got_down

    h2, ffn0, (w_up1, w_down1), (w_od_in, w_od_out) = ffn_fwd(
        0, h1, w_up0, w_down0, gather(['ffn_w_up1', 'ffn_w_down1']), gather(['od_w_in', 'od_w_out']))
    w_ups, w_downs = (w_up0, w_up1), (w_down0, w_down1)
    od_cw, od_cb = full['od_conv_w'][0], full['od_conv_b']
    y1, p1 = _norm_mm("od_in", h2, mix_norm_g[1], w_od_in, tm)
    p13 = p1.reshape(B, S, -1)
    sc = _od_act(p13, od_cw, od_cb)
    h3 = _mm_resid("od_out", sc.reshape(T, D), w_od_out, h2, tm)
    h4, ffn1, _, _ = ffn_fwd(1, h3, w_up1, w_down1)

    dh4, dh4b, loss_part, d_final_g = _final_loss(h4, final_norm_g, loss_target.reshape(T, D), tm)
    loss = lax.psum(loss_part[0, 0], ("x", "y", "c"))

    grads = {}

    halves = {}
    chips = lambda g: g.reshape(N_CHIPS, -1, D)

    def pair_sums(names, parts, recv):
        return [_pair_sum(f"pair_sum_{n}", g, r, c_idx, f32 if n == "small" else bf16)
                for n, g, r in zip(names, parts, recv)]

    def reduce_end(tag, names, hsum, r3):
        mine = [_chip_sum(f"chip_sum_{n}", r, h, k_idx) for n, r, h in zip(names, r3, hsum)]
        theirs = _comm_pair_share(tag, mine)
        halves.update({n: (a, b) for n, a, b in zip(names, mine, theirs)})

    def ffn_bwd(l, dh, dhb, h_in, saved, phase=None, swap=False):
        z, up3 = saved
        w_down, w_up = w_downs[l], w_ups[l]
        da = _mm_nt(f"ffn_down_bwd{l}", dhb, w_down, tm)
        res = _ffn_act_bwd(f"ffn_act_bwd{l}", up3, da.reshape(B, S, -1), full['ffn_conv_w'][l],
                           ffn_conv_b[l].reshape(1, -1), phase=phase)
        (act, dg3, dv3, dcwg, dcwv, dcbg, dcbv), got = res if phase is not None else (res, None)
        g_down = _mm_tn(f"ffn_down_dw{l}", act.reshape(T, -1), dhb, tt)
        dupg, dupv = dg3.reshape(T, -1), dv3.reshape(T, -1)
        F = dupg.shape[1]
        g_up = _mm_tn(f"ffn_up_dw{l}_gate", dupg, z, tt, rows=2 * F)
        g_up = _mm_tn(f"ffn_up_dw{l}_val", dupv, z, tt, rows=2 * F, row_off=F, prev=g_up)
        parts = [chips(g_down), chips(g_up)]
        res = _mm_nt_normbwd(f"ffn_up_bwd{l}", [dupg, dupv], w_up, h_in, ffn_norm_g[l], dh, tm,
                             phase=_swap_phase(parts) if swap else None)
        (dh_new, dhb_new, dg), recv = res if swap else (res, None)
        F = dg3.shape[2]
        dcw = jnp.concatenate([dcwg[:, :F], dcwv[:, :F]], axis=1)
        dcb = jnp.concatenate([dcbg[:, :F], dcbv[:, :F]], axis=1)
        return dh_new, dhb_new, g_down, g_up, dcw, dcb[0], dg[0], got, parts, recv

    dh3, dh3b, gd1, gu1, gcw1, gcb1, gng1, _, _, _ = ffn_bwd(1, dh4, dh4b, h3, ffn1)
    dsc = _mm_nt("od_out_bwd", dh3b, w_od_out, tm)
    g_od_out = _mm_tn("od_out_dw", sc.reshape(T, D), dh3b, tt)
    dbg3, dcg3, dhx3, d_od_cw, d_od_cb = _od_act_bwd(p13, dsc.reshape(B, S, D), od_cw, od_cb)
    dp1 = [t.reshape(T, D) for t in (dbg3, dcg3, dhx3)]
    g_od_in = None
    for i, piece in enumerate(dp1):
        g_od_in = _mm_tn(f"od_in_dw{i}", piece, y1, tt, rows=3 * D, row_off=i * D, prev=g_od_in)
    grads['od_conv_w'] = d_od_cw[None]
    grads['od_conv_b'] = d_od_cb
    layer1 = ['ffn_w_down1', 'ffn_w_up1', 'od_w_out', 'od_w_in']
    parts1 = [chips(g) for g in (gd1, gu1, g_od_out, g_od_in)]
    (dh2, dh2b, gmix1), recv1 = _mm_nt_normbwd("od_in_bwd", dp1, w_od_in, h2, mix_norm_g[1], dh3, tm,
                                               phase=_swap_phase(parts1))
    hsum1 = pair_sums(layer1, parts1, recv1)
    dh1, dh1b, gd0, gu0, gcw0, gcb0, gng0, r3, parts0, recv0 = ffn_bwd(
        0, dh2, dh2b, h1, ffn0, phase=_exchange_phase(hsum1), swap=True)
    reduce_end("layer1", layer1, hsum1, r3)
    ffn0_names = ['ffn_w_down0', 'ffn_w_up0']
    hsum0 = pair_sums(ffn0_names, parts0, recv0)
    grads['ffn_conv_w'] = jnp.stack([gcw0, gcw1])
    grads['ffn_conv_b'] = jnp.stack([gcb0, gcb1])
    grads['ffn_norm_g'] = jnp.stack([gng0, gng1])
    grads['final_norm_g'] = d_final_g[0]

    dmix = _mm_nt("ev_out_bwd", dh1b, w_ev_out, tm)
    g_ev_out = _mm_tn("ev_out_dw", mixcat, dh1b, tt)
    duv, d_ws, d_bs, d_gv = _gmlp_bwd(p0, dmix, ws, bst, gv, SW)
    grads['gm_w_s'] = d_ws[None]
    grads['gm_b_s'] = d_bs[:, :, 0][None]
    grads['gm_v_g'] = d_gv
    dhr, dhi, du_skip, d_cbr, d_cbi, d_dsk, d_wglu, d_bglu = _s5_out_bwd(
        hr, hi, p03, dmix.reshape(B, S, D), cbr, cbi, dsk, w_glu, bglu, tm)
    grads['s5_c_re'] = jnp.swapaxes(_diag_blocks(d_cbr, G), 1, 2)[None]
    grads['s5_c_im'] = jnp.swapaxes(_diag_blocks(d_cbi, G), 1, 2)[None]
    grads['s5_d'] = d_dsk
    grads['s5_w_glu'] = d_wglu[None]
    grads['s5_b_glu'] = d_bglu
    (gr, gi, dar, dai), r3 = _s5_scan("s5_rscan", dhr, dhi, ar_s, ai_s, True, hr, hi, phase=_exchange_phase(hsum0))
    reduce_end("ffn0", ffn0_names, hsum0, r3)
    dp03, d_bbd = _s5_in_bwd(gr, gi, p03, bbd, du_skip, duv.reshape(B, S, -1), tm)
    d_bre, d_bim, d_zr, d_zi = _s5_bbd_bwd(d_bbd, zr_row, zi_row, bre, bim)
    grads['s5_b_re'] = jnp.swapaxes(_diag_blocks(d_bre, G), 1, 2)[None]
    grads['s5_b_im'] = jnp.swapaxes(_diag_blocks(d_bim, G), 1, 2)[None]
    shp = (-1, G, SSM_STATE)
    d_lr, d_li, d_ldt = _s5_prep_bwd(lr, li, ldt, dar.reshape(shp), dai.reshape(shp), d_zr.reshape(shp),
                                     d_zi.reshape(shp))
    grads['s5_lam_re'] = d_lr[None]
    grads['s5_lam_im'] = d_li[None]
    grads['s5_log_dt'] = d_ldt.reshape(1, G)
    dp0 = dp03.reshape(T, PW)
    g_ev_in = _mm_tn("ev_in_dw", dp0, y0, tt)
    grad_x, _, gmix0 = _mm_nt_normbwd("ev_in_bwd", [dp0], w_ev_in, h0, mix_norm_g[0], dh1, tm)
    grads['mix_norm_g'] = jnp.concatenate([gmix0, gmix1], axis=0)

    small = [n for n in WEIGHTS if n not in BIG]
    segs = []
    for n in small:
        gfull = grads[n].astype(f32)
        if n in SHARD_AXIS:
            segs.append(_split_chips(gfull, SHARD_AXIS[n]))
        else:
            segs.append(jnp.broadcast_to(gfull.reshape(1, -1), (N_CHIPS, gfull.size)))
    unit = 2 * SUBLANES * D
    gsmall = _pad_rows(jnp.concatenate(segs, axis=1), unit).reshape(N_CHIPS, -1, D)
    mixer0 = ['ev_w_out', 'ev_w_in', 'small']
    parts = [chips(g_ev_out), chips(g_ev_in), gsmall]
    hsum = pair_sums(mixer0, parts, _run_phase("comm_pair_swap_mixer0", _swap_phase(parts)))
    reduce_end("mixer0", mixer0, hsum, _run_phase("comm_exchange_mixer0", _exchange_phase(hsum)))

    out_g, out_d, out_m, out_v = {}, {}, {}, {}

    def update(n, key, lead, transposed, prev=None):
        res = _adamw(f"adamw_{key}", *halves[key], c_idx, w[n], mom[n], var[n], lead, transposed, prev)
        out_g[n], out_d[n], out_m[n], out_v[n] = res
        return res

    update('ev_w_in', 'ev_w_in', 0, True)
    update('ev_w_out', 'ev_w_out', 0, False)
    update('od_w_in', 'od_w_in', 0, True)
    update('od_w_out', 'od_w_out', 0, False)
    update('ffn_w_up', 'ffn_w_up0', 0, True, prev=update('ffn_w_up', 'ffn_w_up1', 1, True))
    update('ffn_w_down', 'ffn_w_down0', 0, False, prev=update('ffn_w_down', 'ffn_w_down1', 1, False))

    mine, theirs = halves['small']
    first = lax.axis_index("c") == 0
    flat = jnp.concatenate([jnp.where(first, mine, theirs), jnp.where(first, theirs, mine)], axis=1).reshape(-1)
    off = 0
    for n in small:
        out_g[n] = flat[off:off + w[n].size].reshape(w[n].shape)
        off += w[n].size
    res = _adamw_small([out_g[n] for n in small], [w[n] for n in small], [mom[n] for n in small],
                       [var[n] for n in small])
    for i, n in enumerate(small):
        out_d[n], out_m[n], out_v[n] = res[i], res[len(small) + i], res[2 * len(small) + i]

    return (loss, grad_x.reshape(B, S, D), *[out_g[n] for n in WEIGHTS], *[out_d[n] for n in WEIGHTS],
            *[out_m[n] for n in WEIGHTS], *[out_v[n] for n in WEIGHTS])
```

```python
import functools
import math

import jax
import jax.numpy as jnp
from jax import lax
from jax.experimental import pallas as pl
from jax.experimental.pallas import tpu as pltpu

f32 = jnp.float32
bf16 = jnp.bfloat16
MESH = pl.DeviceIdType.MESH

SSM_GROUP = 16
SSM_STATE = 64
GMLP_HEAD = 128
CHUNK = 128
EPS = 1e-6
LAMBDA_RE_MAX = -1e-4
ADAM_LR, ADAM_B1, ADAM_B2, ADAM_EPS, ADAM_WD, ADAM_STEP = 0.001, 0.9, 0.999, 1e-08, 0.01, 10

LANES = 128
SUBLANES = 8
NSUB = 32
HALO = 16
VMEM_LIMIT = 56 * 1024 * 1024
N_CHIPS = 4

WEIGHTS = ['mix_norm_g', 'ffn_norm_g', 'final_norm_g', 'ev_w_in', 'ev_w_out', 's5_lam_re', 's5_lam_im', 's5_log_dt',
           's5_b_re', 's5_b_im', 's5_c_re', 's5_c_im', 's5_d', 's5_w_glu', 's5_b_glu', 'gm_w_s', 'gm_b_s', 'gm_v_g',
           'od_w_in', 'od_conv_w', 'od_conv_b', 'od_w_out', 'ffn_w_up', 'ffn_conv_w', 'ffn_conv_b', 'ffn_w_down']
SHARD_AXIS = {'ev_w_in': 2, 'ev_w_out': 1, 's5_w_glu': 1, 'od_w_in': 2, 'od_conv_w': 2, 'od_conv_b': 1, 'od_w_out': 1,
              'ffn_w_up': 2, 'ffn_conv_w': 2, 'ffn_w_down': 1}
GATHER_BF16 = ['ev_w_in', 'ev_w_out', 's5_w_glu', 'od_w_in', 'od_w_out', 'ffn_w_up', 'ffn_w_down']
GATHER_F32 = ['od_conv_w', 'od_conv_b', 'ffn_conv_w']

_GELU_K0 = math.sqrt(2.0 / math.pi)
_GELU_K1 = 0.044715
NT = (((1,), (1,)), ((), ()))
TN = (((0,), (0,)), ((), ()))


def _pick(n, cap):
    if n <= cap:
        return n
    best = None
    for d in range(LANES, cap + 1, LANES):
        if n % d == 0:
            best = d
    assert best is not None, (n, cap)
    return best


def _params(sem=None):
    return pltpu.CompilerParams(dimension_semantics=sem, vmem_limit_bytes=VMEM_LIMIT)


class _Phase:
    def __init__(self, ins, inplace, outs, sems, start, finish):
        self.ins, self.inplace, self.outs, self.sems = list(ins), list(inplace), list(outs), list(sems)
        self.start, self.finish = start, finish


def _call(body, name, grid, in_specs, out_specs, out_shape, args, scratch=(), sem=None, phase=None):
    if phase is None:
        return pl.pallas_call(body, name=name, grid=grid, in_specs=in_specs, out_specs=out_specs, out_shape=out_shape,
                              scratch_shapes=list(scratch), compiler_params=_params(sem))(*args)
    any_spec = pl.BlockSpec(memory_space=pl.ANY)
    n_in, n_out, n_scr = len(args), len(out_shape), len(scratch)
    p_in = phase.ins + phase.inplace
    ci, co = len(p_in), len(phase.outs)

    def wrapped(*refs):
        ins, cins = refs[:n_in], refs[n_in:n_in + len(phase.ins)]
        b = n_in + ci
        outs, couts = refs[b:b + n_out], refs[b + n_out:b + n_out + co]
        d = b + n_out + co
        scr, csem = refs[d:d + n_scr], refs[d + n_scr:]
        ids = [pl.program_id(i) for i in range(len(grid))]
        first = functools.reduce(jnp.logical_and, [i == 0 for i in ids])
        last = functools.reduce(jnp.logical_and, [i == g - 1 for i, g in zip(ids, grid)])

        @pl.when(first)
        def _():
            phase.start(cins, couts, csem)
        body(*ins, *outs, *scr)

        @pl.when(last)
        def _():
            phase.finish(cins, couts, csem)

    res = pl.pallas_call(
        wrapped, name=name, grid=grid, in_specs=list(in_specs) + [any_spec] * ci,
        out_specs=list(out_specs) + [any_spec] * co, out_shape=list(out_shape) + phase.outs,
        scratch_shapes=list(scratch) + phase.sems,
        input_output_aliases={n_in + len(phase.ins) + i: n_out + i for i in range(len(phase.inplace))},
        compiler_params=_params(tuple("arbitrary" for _ in grid)))(*args, *p_in)
    return res[:n_out], res[n_out:]


def _run_phase(name, phase):
    any_spec = pl.BlockSpec(memory_space=pl.ANY)
    ni, ci, co = len(phase.ins), len(phase.ins) + len(phase.inplace), len(phase.outs)

    def body(*refs):
        cins, couts, csem = refs[:ni], refs[ci:ci + co], refs[ci + co:]
        phase.start(cins, couts, csem)
        phase.finish(cins, couts, csem)

    return pl.pallas_call(
        body, name=name, in_specs=[any_spec] * ci, out_specs=[any_spec] * co, out_shape=phase.outs,
        scratch_shapes=phase.sems, input_output_aliases={ni + i: i for i in range(len(phase.inplace))})(
            *phase.ins, *phase.inplace)


def _gelu(x):
    return 0.5 * x * (1.0 + jnp.tanh(_GELU_K0 * (x + _GELU_K1 * x * x * x)))


def _gelu_grad(x):
    t = jnp.tanh(_GELU_K0 * (x + _GELU_K1 * x * x * x))
    return 0.5 * (1.0 + t) + 0.5 * x * (1.0 - t * t) * _GELU_K0 * (1.0 + 3.0 * _GELU_K1 * x * x)


def _rms_stats(x):
    r = lax.rsqrt(jnp.mean(x * x, axis=-1, keepdims=True) + EPS)
    return x * r, r


def _rms_bwd(dy, xh, r, g):
    dxh = dy * g
    dx = r * (dxh - xh * jnp.mean(dxh * xh, axis=-1, keepdims=True))
    return dx, jnp.sum(dy * xh, axis=0, keepdims=True)


def _dot(a, b):
    return jnp.dot(a, b, preferred_element_type=f32)


def _dg(a, b, dims):
    return lax.dot_general(a, b, dims, preferred_element_type=f32)


def _row_fold(z):
    return z.reshape(z.shape[0] // SUBLANES, SUBLANES, z.shape[1]).sum(axis=0)


def _norm_mm(name, h, g, w, tm, phase=None):
    T, D = h.shape
    N = w.shape[1]
    nc = _pick(N, 512)

    def body(h_ref, g_ref, w_ref, y_ref, o_ref):
        xh, _ = _rms_stats(h_ref[...])
        y = (xh * g_ref[...]).astype(bf16)
        y_ref[...] = y
        for j in range(N // nc):
            o_ref[:, j * nc:(j + 1) * nc] = _dot(y, w_ref[:, j * nc:(j + 1) * nc]).astype(bf16)

    return _call(
        body, name, (T // tm,),
        [pl.BlockSpec((tm, D), lambda i: (i, 0)), pl.BlockSpec((1, D), lambda i: (0, 0)),
         pl.BlockSpec((D, N), lambda i: (0, 0))],
        [pl.BlockSpec((tm, D), lambda i: (i, 0)), pl.BlockSpec((tm, N), lambda i: (i, 0))],
        [jax.ShapeDtypeStruct((T, D), bf16), jax.ShapeDtypeStruct((T, N), bf16)],
        [h, g.reshape(1, D), w], sem=("parallel",), phase=phase)


def _mm_resid(name, a, w, resid, tm):
    T, K = a.shape
    N = w.shape[1]

    def body(a_ref, w_ref, r_ref, o_ref):
        o_ref[...] = r_ref[...] + _dot(a_ref[...], w_ref[...])

    return pl.pallas_call(
        body, name=name, grid=(T // tm,),
        in_specs=[pl.BlockSpec((tm, K), lambda i: (i, 0)), pl.BlockSpec((K, N), lambda i: (0, 0)),
                  pl.BlockSpec((tm, N), lambda i: (i, 0))],
        out_specs=pl.BlockSpec((tm, N), lambda i: (i, 0)),
        out_shape=jax.ShapeDtypeStruct((T, N), f32),
        compiler_params=_params(("parallel",)))(a, w, resid)


def _mm_nt(name, dy, w, tm):
    T, N = dy.shape
    K = w.shape[0]
    kc = _pick(K, 512)

    def body(d_ref, w_ref, o_ref):
        d = d_ref[...].astype(bf16)
        for j in range(K // kc):
            o_ref[:, j * kc:(j + 1) * kc] = _dg(d, w_ref[j * kc:(j + 1) * kc, :], NT).astype(bf16)

    return pl.pallas_call(
        body, name=name, grid=(T // tm,),
        in_specs=[pl.BlockSpec((tm, N), lambda i: (i, 0)), pl.BlockSpec((K, N), lambda i: (0, 0))],
        out_specs=pl.BlockSpec((tm, K), lambda i: (i, 0)),
        out_shape=jax.ShapeDtypeStruct((T, K), bf16),
        compiler_params=_params(("parallel",)))(dy, w)


def _mm_nt_normbwd(name, dys, w, h, g, dh_in, tm, phase=None):
    n = len(dys)
    T = dys[0].shape[0]
    D = w.shape[0]
    widths = [d.shape[1] for d in dys]
    offs = [sum(widths[:i]) for i in range(n)]

    def body(*refs):
        d_refs = refs[:n]
        w_ref, h_ref, g_ref, dh_ref, o_ref, ob_ref, dg_ref = refs[n:]
        dz = _dg(d_refs[0][...], w_ref[:, :widths[0]], NT)
        for i in range(1, n):
            dz += _dg(d_refs[i][...], w_ref[:, offs[i]:offs[i] + widths[i]], NT)
        xh, r = _rms_stats(h_ref[...])
        dx, dg = _rms_bwd(dz, xh, r, g_ref[...])
        out = dh_ref[...] + dx
        o_ref[...] = out
        ob_ref[...] = out.astype(bf16)

        @pl.when(pl.program_id(0) == 0)
        def _():
            dg_ref[...] = jnp.zeros_like(dg_ref)
        dg_ref[...] += dg

    row = lambda c: pl.BlockSpec((tm, c), lambda i: (i, 0))
    return _call(
        body, name, (T // tm,),
        [row(c) for c in widths] + [pl.BlockSpec((D, sum(widths)), lambda i: (0, 0)), row(D),
                                    pl.BlockSpec((1, D), lambda i: (0, 0)), row(D)],
        [row(D), row(D), pl.BlockSpec((1, D), lambda i: (0, 0))],
        [jax.ShapeDtypeStruct((T, D), f32), jax.ShapeDtypeStruct((T, D), bf16), jax.ShapeDtypeStruct((1, D), f32)],
        [*dys, w, h, g.reshape(1, D), dh_in], sem=("arbitrary",), phase=phase)


def _mm_tn(name, a, b, tt, rows=None, row_off=0, prev=None):
    T, K = a.shape
    N = b.shape[1]
    rows = K if rows is None else rows
    tk = _pick(K, 1408)
    tn = _pick(N, 1024)
    assert row_off % tk == 0
    kb = row_off // tk

    def body(a_ref, b_ref, *rest):
        o_ref = rest[-1]

        @pl.when(pl.program_id(2) == 0)
        def _():
            o_ref[...] = jnp.zeros_like(o_ref)
        o_ref[...] += _dg(a_ref[...], b_ref[...], TN)

    in_specs = [pl.BlockSpec((tt, tk), lambda k, n, t: (t, k)), pl.BlockSpec((tt, tn), lambda k, n, t: (t, n))]
    args, aliases = [a, b], {}
    if prev is not None:
        in_specs.append(ANY)
        args.append(prev)
        aliases = {2: 0}
    return pl.pallas_call(
        body, name=name, grid=(K // tk, N // tn, T // tt), in_specs=in_specs,
        out_specs=pl.BlockSpec((tk, tn), lambda k, n, t: (k + kb, n)),
        out_shape=jax.ShapeDtypeStruct((rows, N), f32), input_output_aliases=aliases,
        compiler_params=_params(("parallel", "parallel", "arbitrary")))(*args)


def _final_loss(h, g, tgt, tm):
    T, D = h.shape

    def body(h_ref, g_ref, t_ref, dh_ref, dhb_ref, loss_ref, dg_ref):
        xh, r = _rms_stats(h_ref[...])
        gg = g_ref[...]
        diff = xh * gg - t_ref[...]
        dy = diff * (1.0 / D)
        dx, dg = _rms_bwd(dy, xh, r, gg)
        dh_ref[...] = dx
        dhb_ref[...] = dx.astype(bf16)

        @pl.when(pl.program_id(0) == 0)
        def _():
            dg_ref[...] = jnp.zeros_like(dg_ref)
            loss_ref[...] = jnp.zeros_like(loss_ref)
        dg_ref[...] += dg
        loss_ref[...] += (0.5 / D) * jnp.sum(jnp.sum(diff * diff, axis=1, keepdims=True), axis=0, keepdims=True)

    return pl.pallas_call(
        body, name="final_loss", grid=(T // tm,),
        in_specs=[pl.BlockSpec((tm, D), lambda i: (i, 0)), pl.BlockSpec((1, D), lambda i: (0, 0)),
                  pl.BlockSpec((tm, D), lambda i: (i, 0))],
        out_specs=[pl.BlockSpec((tm, D), lambda i: (i, 0)), pl.BlockSpec((tm, D), lambda i: (i, 0)),
                   pl.BlockSpec((1, 1), lambda i: (0, 0)), pl.BlockSpec((1, D), lambda i: (0, 0))],
        out_shape=[jax.ShapeDtypeStruct((T, D), f32), jax.ShapeDtypeStruct((T, D), bf16),
                   jax.ShapeDtypeStruct((1, 1), f32), jax.ShapeDtypeStruct((1, D), f32)],
        compiler_params=_params(("arbitrary",)))(h, g.reshape(1, D), tgt)


def _taps(load, r0, R):
    main = load(r0, R)
    hs = pl.multiple_of(jnp.maximum(r0 - HALO, 0), HALO)
    halo = load(hs, HALO) * (r0 > 0).astype(f32)
    ext = jnp.concatenate([halo, main], axis=0)
    xm1 = pltpu.roll(ext, 1, 0)[HALO:]
    xm2 = pltpu.roll(ext, 2, 0)[HALO:]
    return xm2, xm1, main


def _conv(w, b, taps):
    return b + w[0:1] * taps[0] + w[1:2] * taps[1] + w[2:3] * taps[2]


def _ref_load(ref):
    return lambda s, n: ref[pl.ds(s, n), :].astype(f32)


def _ffn_down(name, up, cw, cb, w_down, resid, S, tm, phase=None):
    T, F2 = up.shape
    F = F2 // 2
    D = w_down.shape[1]
    cwid = _pick(F, 256)
    per_seq = S // tm

    def body(u_ref, halo_ref, cw_ref, cb_ref, w_ref, r_ref, o_ref, c_ref):
        keep = (pl.program_id(0) % per_seq > 0).astype(f32)

        def conv(off):
            cols = slice(off, off + cwid)
            main = u_ref[:, cols].astype(f32)
            ext = jnp.concatenate([halo_ref[:, cols].astype(f32) * keep, main], axis=0)
            taps = (pltpu.roll(ext, 2, 0)[HALO:], pltpu.roll(ext, 1, 0)[HALO:], main)
            return _conv(cw_ref[:, cols], cb_ref[:, cols], taps)

        acc = r_ref[...]
        for j in range(F // cwid):
            cg, cv = conv(j * cwid), conv(F + j * cwid)
            c_ref[:, j * cwid:(j + 1) * cwid] = cg.astype(bf16)
            c_ref[:, F + j * cwid:F + (j + 1) * cwid] = cv.astype(bf16)
            a = (cg * jax.nn.sigmoid(cg) * cv).astype(bf16)
            acc = acc + _dot(a, w_ref[j * cwid:(j + 1) * cwid, :])
        o_ref[...] = acc

    full = lambda r, c: pl.BlockSpec((r, c), lambda i: (0, 0))
    return _call(
        body, name, (T // tm,),
        [pl.BlockSpec((tm, F2), lambda i: (i, 0)),
         pl.BlockSpec((HALO, F2), lambda i: (jnp.maximum(i * (tm // HALO) - 1, 0), 0)),
         full(3, F2), full(1, F2), full(F, D), pl.BlockSpec((tm, D), lambda i: (i, 0))],
        [pl.BlockSpec((tm, D), lambda i: (i, 0)), pl.BlockSpec((tm, F2), lambda i: (i, 0))],
        [jax.ShapeDtypeStruct((T, D), f32), jax.ShapeDtypeStruct((T, F2), bf16)],
        [up, up, cw, cb, w_down, resid], sem=("parallel",), phase=phase)


def _rev_conv_rows(d, nxt, w):
    R = d.shape[0]
    ext = jnp.concatenate([d, nxt], axis=0)
    n = R + HALO
    xp1 = pltpu.roll(ext, n - 1, 0)[:R]
    xp2 = pltpu.roll(ext, n - 2, 0)[:R]
    return w[2:3] * d + w[1:2] * xp1 + w[0:1] * xp2, xp1, xp2


def _conv_grad_acc(acc, dc, taps):
    return (acc[0] + _row_fold(dc * taps[0]), acc[1] + _row_fold(dc * taps[1]), acc[2] + _row_fold(dc * taps[2]),
            acc[3] + _row_fold(dc))


def _conv_grad_out(dcw_ref, dcb_ref, acc):
    @pl.when(pl.program_id(1) == 0)
    def _():
        dcw_ref[...] = jnp.zeros_like(dcw_ref)
        dcb_ref[...] = jnp.zeros_like(dcb_ref)
    for k in range(3):
        dcw_ref[k:k + 1, :] += jnp.sum(acc[k], axis=0, keepdims=True)
    dcb_ref[...] += jnp.sum(acc[3], axis=0, keepdims=True)


def _ffn_act_bwd(name, up3, c3, da3, cw, phase=None):
    B, S, F2 = up3.shape
    F = F2 // 2
    cwid = _pick(F, 256)
    nF = F // cwid
    R = min(256, S)
    nR = S // R

    def body(xg_ref, xv_ref, cg_ref, cv_ref, da_ref, wg_ref, wv_ref,
             act_ref, dg_ref, dv_ref, dcwg_ref, dcwv_ref, dcbg_ref, dcbv_ref, sum_scr):
        wg, wv = wg_ref[...], wv_ref[...]

        def half(d, nxt, w, x_ref, rows, acc, out_ref):
            out, xp1, xp2 = _rev_conv_rows(d, nxt, w)
            out_ref[rows, :] = out.astype(bf16)
            x = x_ref[rows, :].astype(f32)
            return (acc[0] + _row_fold(xp2 * x), acc[1] + _row_fold(xp1 * x), acc[2] + _row_fold(d * x),
                    acc[3] + _row_fold(d))

        def step(i, carry):
            ng, nv, accg, accv = carry
            rows = pl.ds(pl.multiple_of((nR - 1 - i) * R, R), R)
            cg, cv = cg_ref[rows, :].astype(f32), cv_ref[rows, :].astype(f32)
            da = da_ref[rows, :].astype(f32)
            sg = jax.nn.sigmoid(cg)
            act_ref[rows, :] = (cg * sg * cv).astype(bf16)
            dgate = da * cv * (sg * (1.0 + cg * (1.0 - sg)))
            dval = da * (cg * sg)
            accg = half(dgate, ng, wg, xg_ref, rows, accg, dg_ref)
            accv = half(dval, nv, wv, xv_ref, rows, accv, dv_ref)
            return dgate[:HALO], dval[:HALO], accg, accv
        z = jnp.zeros((SUBLANES, cwid), f32)
        zh = jnp.zeros((HALO, cwid), f32)
        _, _, accg, accv = lax.fori_loop(0, nR, step, (zh, zh, (z, z, z, z), (z, z, z, z)))
        j = pl.program_id(1)
        for half_i, (acc, dcw_ref, dcb_ref) in enumerate(((accg, dcwg_ref, dcbg_ref), (accv, dcwv_ref, dcbv_ref))):
            @pl.when(pl.program_id(0) == 0)
            def _():
                sum_scr[half_i, j] = jnp.zeros((SUBLANES, cwid), f32)
            for k in range(4):
                sum_scr[half_i, j, k:k + 1, :] += jnp.sum(acc[k], axis=0, keepdims=True)
            dcw_ref[...] = sum_scr[half_i, j, 0:3, :]
            dcb_ref[...] = sum_scr[half_i, j, 3:4, :]

    blk = lambda off: pl.BlockSpec((None, S, cwid), lambda b, j: (b, 0, off + j))
    wblk = lambda off: pl.BlockSpec((3, cwid), lambda b, j: (0, off + j))
    sums = lambda r: pl.BlockSpec((r, cwid), lambda b, j: (0, jnp.where(b == B - 1, j, nF)))
    half_shape = jax.ShapeDtypeStruct((B, S, F), bf16)
    return _call(
        body, name, (B, nF),
        [blk(0), blk(nF), blk(0), blk(nF), blk(0), wblk(0), wblk(nF)],
        [blk(0), blk(0), blk(0), sums(3), sums(3), sums(1), sums(1)],
        [half_shape, half_shape, half_shape, jax.ShapeDtypeStruct((3, F + cwid), f32),
         jax.ShapeDtypeStruct((3, F + cwid), f32), jax.ShapeDtypeStruct((1, F + cwid), f32),
         jax.ShapeDtypeStruct((1, F + cwid), f32)],
        [up3, up3, c3, c3, da3, cw, cw], scratch=[pltpu.VMEM((2, nF, SUBLANES, cwid), f32)],
        sem=("arbitrary", "arbitrary"), phase=phase)


def _od_act(p3, cw, cb):
    B, S, D3 = p3.shape
    D = D3 // 3
    cwid = _pick(D, 256)
    nD = D // cwid
    R = min(256, S)

    def body(bg_ref, cg_ref, hx_ref, w_ref, b_ref, o_ref):
        w, b = w_ref[...], b_ref[...]
        q = lambda s, n: cg_ref[pl.ds(s, n), :].astype(f32) * hx_ref[pl.ds(s, n), :].astype(f32)

        def chunk(r, c):
            r0 = pl.multiple_of(r * R, R)
            cq = _conv(w, b, _taps(q, r0, R))
            o_ref[pl.ds(r0, R), :] = (bg_ref[pl.ds(r0, R), :].astype(f32) * cq).astype(bf16)
            return c
        lax.fori_loop(0, S // R, chunk, 0)

    blk = lambda off: pl.BlockSpec((None, S, cwid), lambda b, j: (b, 0, off + j))
    return pl.pallas_call(
        body, name="od_act", grid=(B, nD),
        in_specs=[blk(0), blk(nD), blk(2 * nD), pl.BlockSpec((3, cwid), lambda b, j: (0, j)),
                  pl.BlockSpec((1, cwid), lambda b, j: (0, j))],
        out_specs=pl.BlockSpec((None, S, cwid), lambda b, j: (b, 0, j)),
        out_shape=jax.ShapeDtypeStruct((B, S, D), bf16),
        compiler_params=_params(("parallel", "parallel")))(p3, p3, p3, cw, cb)


def _od_act_bwd(p3, dsc3, cw, cb):
    B, S, D3 = p3.shape
    D = D3 // 3
    cwid = _pick(D, 256)
    nD = D // cwid
    R = min(256, S)
    nR = S // R

    def body(bg_ref, cg_ref, hx_ref, d_ref, w_ref, b_ref, dbg_ref, dcg_ref, dhx_ref, dcw_ref, dcb_ref):
        w, b = w_ref[...], b_ref[...]
        q = lambda s, n: cg_ref[pl.ds(s, n), :].astype(f32) * hx_ref[pl.ds(s, n), :].astype(f32)

        def step(i, carry):
            nxt, acc = carry
            r0 = pl.multiple_of((nR - 1 - i) * R, R)
            rows = pl.ds(r0, R)
            tq = _taps(q, r0, R)
            cq = _conv(w, b, tq)
            d = d_ref[rows, :].astype(f32)
            dbg_ref[rows, :] = (d * cq).astype(bf16)
            dcq = d * bg_ref[rows, :].astype(f32)
            dq, _, _ = _rev_conv_rows(dcq, nxt, w)
            dcg_ref[rows, :] = (dq * hx_ref[rows, :].astype(f32)).astype(bf16)
            dhx_ref[rows, :] = (dq * cg_ref[rows, :].astype(f32)).astype(bf16)
            return dcq[:HALO], _conv_grad_acc(acc, dcq, tq)
        z = jnp.zeros((SUBLANES, cwid), f32)
        _, acc = lax.fori_loop(0, nR, step, (jnp.zeros((HALO, cwid), f32), (z, z, z, z)))
        _conv_grad_out(dcw_ref, dcb_ref, acc)

    blk = lambda off: pl.BlockSpec((None, S, cwid), lambda j, b: (b, 0, off + j))
    part = jax.ShapeDtypeStruct((B, S, D), bf16)
    return pl.pallas_call(
        body, name="od_act_bwd", grid=(nD, B),
        in_specs=[blk(0), blk(nD), blk(2 * nD), blk(0), pl.BlockSpec((3, cwid), lambda j, b: (0, j)),
                  pl.BlockSpec((1, cwid), lambda j, b: (0, j))],
        out_specs=[blk(0), blk(0), blk(0), pl.BlockSpec((3, cwid), lambda j, b: (0, j)),
                   pl.BlockSpec((1, cwid), lambda j, b: (0, j))],
        out_shape=[part, part, part, jax.ShapeDtypeStruct((3, D), f32), jax.ShapeDtypeStruct((1, D), f32)],
        compiler_params=_params(("parallel", "arbitrary")))(p3, p3, p3, dsc3, cw, cb)


def _gmlp_parts(p, gv, SW, GW):
    uv = p[:, SW:].astype(f32)
    ge = _gelu(uv)
    u, v = ge[:, :GW], ge[:, GW:]
    vh, r = _rms_stats(v)
    return uv, u, vh, r, vh * gv


def _tril():
    rows = lax.broadcasted_iota(jnp.int32, (CHUNK, CHUNK), 0)
    cols = lax.broadcasted_iota(jnp.int32, (CHUNK, CHUNK), 1)
    return rows >= cols


def _gmlp(p0, a_out, ws, bst, gv, SW):
    T, PW = p0.shape
    GW = (PW - SW) // 2
    H = GW // GMLP_HEAD
    D = SW + GW

    def body(p_ref, a_ref, ws_ref, b_ref, gv_ref, o_ref):
        _, u, _, _, vn = _gmlp_parts(p_ref[...], gv_ref[...], SW, GW)
        tri = _tril()
        o_ref[:, :SW] = a_ref[...]
        for hh in range(H):
            sl = slice(hh * GMLP_HEAD, (hh + 1) * GMLP_HEAD)
            wm = jnp.where(tri, ws_ref[hh], 0.0).astype(bf16)
            gate = _dot(wm, vn[:, sl].astype(bf16)) + b_ref[:, hh:hh + 1]
            o_ref[:, SW + hh * GMLP_HEAD:SW + (hh + 1) * GMLP_HEAD] = (u[:, sl] * gate).astype(bf16)

    return pl.pallas_call(
        body, name="gmlp", grid=(T // CHUNK,),
        in_specs=[pl.BlockSpec((CHUNK, PW), lambda i: (i, 0)), pl.BlockSpec((CHUNK, SW), lambda i: (i, 0)),
                  pl.BlockSpec((H, CHUNK, CHUNK), lambda i: (0, 0, 0)), pl.BlockSpec((CHUNK, H), lambda i: (0, 0)),
                  pl.BlockSpec((1, GW), lambda i: (0, 0))],
        out_specs=pl.BlockSpec((CHUNK, D), lambda i: (i, 0)),
        out_shape=jax.ShapeDtypeStruct((T, D), bf16),
        compiler_params=_params(("parallel",)))(p0, a_out, ws, bst, gv)


def _gmlp_bwd(p0, dmix, ws, bst, gv, SW):
    T, PW = p0.shape
    GW = (PW - SW) // 2
    H = GW // GMLP_HEAD
    D = SW + GW

    def body(p_ref, d_ref, ws_ref, b_ref, gv_ref, duv_ref, dws_ref, dbs_ref, dgv_ref):
        gv_ = gv_ref[...]
        uv, u, vh, r, vn = _gmlp_parts(p_ref[...], gv_, SW, GW)
        dout = d_ref[...][:, SW:].astype(f32)
        tri = _tril()

        @pl.when(pl.program_id(0) == 0)
        def _():
            dws_ref[...] = jnp.zeros_like(dws_ref)
            dbs_ref[...] = jnp.zeros_like(dbs_ref)
            dgv_ref[...] = jnp.zeros_like(dgv_ref)
        du, dvn = [], []
        for hh in range(H):
            sl = slice(hh * GMLP_HEAD, (hh + 1) * GMLP_HEAD)
            wm = jnp.where(tri, ws_ref[hh], 0.0).astype(bf16)
            vnh = vn[:, sl].astype(bf16)
            gate = _dot(wm, vnh) + b_ref[:, hh:hh + 1]
            dgate = dout[:, sl] * u[:, sl]
            du.append(dout[:, sl] * gate)
            dgb = dgate.astype(bf16)
            dws_ref[hh] += jnp.where(tri, _dg(dgb, vnh, NT), 0.0)
            dbs_ref[hh] += jnp.broadcast_to(jnp.sum(dgate, axis=1, keepdims=True), (CHUNK, CHUNK))
            dvn.append(_dg(wm, dgb, TN))
        dvn = jnp.concatenate(dvn, axis=1)
        dv, dgv = _rms_bwd(dvn, vh, r, gv_)
        dgv_ref[...] += dgv
        dge = jnp.concatenate(du + [dv], axis=1)
        duv_ref[...] = (dge * _gelu_grad(uv)).astype(bf16)

    return pl.pallas_call(
        body, name="gmlp_bwd", grid=(T // CHUNK,),
        in_specs=[pl.BlockSpec((CHUNK, PW), lambda i: (i, 0)), pl.BlockSpec((CHUNK, D), lambda i: (i, 0)),
                  pl.BlockSpec((H, CHUNK, CHUNK), lambda i: (0, 0, 0)), pl.BlockSpec((CHUNK, H), lambda i: (0, 0)),
                  pl.BlockSpec((1, GW), lambda i: (0, 0))],
        out_specs=[pl.BlockSpec((CHUNK, 2 * GW), lambda i: (i, 0)), pl.BlockSpec((H, CHUNK, CHUNK), lambda i: (0, 0, 0)),
                   pl.BlockSpec((H, CHUNK, CHUNK), lambda i: (0, 0, 0)), pl.BlockSpec((1, GW), lambda i: (0, 0))],
        out_shape=[jax.ShapeDtypeStruct((T, 2 * GW), bf16), jax.ShapeDtypeStruct((H, CHUNK, CHUNK), f32),
                   jax.ShapeDtypeStruct((H, CHUNK, CHUNK), f32), jax.ShapeDtypeStruct((1, GW), f32)],
        compiler_params=_params(("arbitrary",)))(p0, dmix, ws, bst, gv)


def _s5_disc(lr, li, ldt):
    lr = jnp.minimum(lr, LAMBDA_RE_MAX)
    dt = jnp.exp(ldt)
    mag = jnp.exp(lr * dt)
    ar = mag * jnp.cos(li * dt)
    ai = mag * jnp.sin(li * dt)
    den = lr * lr + li * li
    nr = ar - 1.0
    zr = (nr * lr + ai * li) / den
    zi = (ai * lr - nr * li) / den
    return ar, ai, zr, zi


def _s5_prep(lr, li, ldt):
    G, P = lr.shape

    def body(lr_ref, li_ref, ldt_ref, ar_ref, ai_ref, zr_ref, zi_ref):
        ar, ai, zr, zi = _s5_disc(lr_ref[...], li_ref[...], ldt_ref[...])
        ar_ref[...] = ar
        ai_ref[...] = ai
        zr_ref[...] = zr
        zi_ref[...] = zi

    s = jax.ShapeDtypeStruct((G, P), f32)
    return pl.pallas_call(body, name="s5_prep", out_shape=[s, s, s, s])(lr, li, ldt)


def _s5_prep_bwd(lr, li, ldt, dar, dai, dzr, dzi):
    G, P = lr.shape

    def body(lr_ref, li_ref, ldt_ref, dar_ref, dai_ref, dzr_ref, dzi_ref, o1, o2, o3):
        _, vjp = jax.vjp(_s5_disc, lr_ref[...], li_ref[...], ldt_ref[...])
        cts = tuple(jnp.sum(r[...], axis=0) for r in (dar_ref, dai_ref, dzr_ref, dzi_ref))
        a, b, c = vjp(cts)
        o1[...] = a
        o2[...] = b
        o3[...] = c

    s = jax.ShapeDtypeStruct((G, P), f32)
    return pl.pallas_call(body, name="s5_prep_bwd", out_shape=[s, s, jax.ShapeDtypeStruct((G, 1), f32)])(
        lr, li, ldt, dar, dai, dzr, dzi)


def _s5_bbd(zr, zi, bre, bim):
    SW, NS = bre.shape

    def body(zr_ref, zi_ref, br_ref, bi_ref, o_ref):
        zr_, zi_, br, bi = zr_ref[...], zi_ref[...], br_ref[...], bi_ref[...]
        o_ref[:, :NS] = (zr_ * br - zi_ * bi).astype(bf16)
        o_ref[:, NS:] = (zr_ * bi + zi_ * br).astype(bf16)

    return pl.pallas_call(body, name="s5_bbd", out_shape=jax.ShapeDtypeStruct((SW, 2 * NS), bf16))(zr, zi, bre, bim)


def _s5_bbd_bwd(dbbd, zr, zi, bre, bim):
    SW, NS = bre.shape

    def body(d_ref, zr_ref, zi_ref, br_ref, bi_ref, dbr_ref, dbi_ref, dzr_ref, dzi_ref):
        zr_, zi_, br, bi = zr_ref[...], zi_ref[...], br_ref[...], bi_ref[...]
        dr, di = d_ref[:, :NS], d_ref[:, NS:]
        dbr_ref[...] = zr_ * dr + zi_ * di
        dbi_ref[...] = zr_ * di - zi_ * dr
        dzr_ref[...] = jnp.sum(dr * br + di * bi, axis=0, keepdims=True)
        dzi_ref[...] = jnp.sum(di * br - dr * bi, axis=0, keepdims=True)

    m = jax.ShapeDtypeStruct((SW, NS), f32)
    v = jax.ShapeDtypeStruct((1, NS), f32)
    return pl.pallas_call(body, name="s5_bbd_bwd", out_shape=[m, m, v, v])(dbbd, zr, zi, bre, bim)


def _slab_cat(ref, NB):
    return jnp.concatenate([ref[j] for j in range(NB)], axis=1)


def _s5_in(p3, bbd, SW, tm):
    B, S, PW = p3.shape
    NS = bbd.shape[1] // 2
    NB = NS // LANES

    def body(u_ref, b_ref, xr_ref, xi_ref):
        x = _dot(u_ref[...], b_ref[...])
        for j in range(NB):
            xr_ref[j] = x[:, j * LANES:(j + 1) * LANES]
            xi_ref[j] = x[:, NS + j * LANES:NS + (j + 1) * LANES]

    slab = jax.ShapeDtypeStruct((B, NB, S, LANES), f32)
    sspec = pl.BlockSpec((None, NB, tm, LANES), lambda b, i: (b, 0, i, 0))
    return pl.pallas_call(
        body, name="s5_in", grid=(B, S // tm),
        in_specs=[pl.BlockSpec((None, tm, SW), lambda b, i: (b, i, 0)), pl.BlockSpec((SW, 2 * NS), lambda b, i: (0, 0))],
        out_specs=[sspec, sspec], out_shape=[slab, slab],
        compiler_params=_params(("parallel", "parallel")))(p3, bbd)


def _s5_scan(name, xr, xi, ar, ai, reverse, hr=None, hi=None, phase=None):
    B, NB, S, _ = xr.shape
    L = S // NSUB
    nb = 2 if (hr is None and NB % 2 == 0) else 1
    with_da = hr is not None

    def body(*refs):
        if with_da:
            xr_ref, xi_ref, ar_ref, ai_ref, hr_ref, hi_ref, or_ref, oi_ref, dar_ref, dai_ref, pr_scr, pi_scr = refs
        else:
            xr_ref, xi_ref, ar_ref, ai_ref, or_ref, oi_ref, pr_scr, pi_scr = refs
        sign = -1.0 if reverse else 1.0
        a_r = [jnp.broadcast_to(ar_ref[j], (NSUB, LANES)) for j in range(nb)]
        a_i = [jnp.broadcast_to(ai_ref[j], (NSUB, LANES)) * sign for j in range(nb)]

        def step(t, carry):
            row = (L - 1 - t) if reverse else t
            rows = pl.ds(row, NSUB, stride=L)
            out = []
            for j in range(nb):
                sr, si, pr, pi = carry[j]
                nr = a_r[j] * sr - a_i[j] * si + xr_ref.at[j][rows, :]
                ni = a_r[j] * si + a_i[j] * sr + xi_ref.at[j][rows, :]
                or_ref.at[j][rows, :] = nr
                oi_ref.at[j][rows, :] = ni
                npr = a_r[j] * pr - a_i[j] * pi
                npi = a_r[j] * pi + a_i[j] * pr
                pr_scr[j, pl.ds(row, 1), :] = npr[0:1]
                pi_scr[j, pl.ds(row, 1), :] = npi[0:1]
                out.append((nr, ni, npr, npi))
            return tuple(out)
        z = jnp.zeros((NSUB, LANES), f32)
        one = jnp.ones((NSUB, LANES), f32)
        fin = lax.fori_loop(0, L, step, tuple((z, z, one, z) for _ in range(nb)))

        for j in range(nb):
            sr, si, plr, pli = fin[j]
            plr, pli = plr[0:1], pli[0:1]
            cr = jnp.zeros((1, LANES), f32)
            ci = jnp.zeros((1, LANES), f32)
            order = range(NSUB - 2, -1, -1) if reverse else range(1, NSUB)
            for c in order:
                src = c + 1 if reverse else c - 1
                cr, ci = (sr[src:src + 1] + plr * cr - pli * ci, si[src:src + 1] + plr * ci + pli * cr)
                rows = slice(c * L, (c + 1) * L)
                tr, ti = pr_scr[j], pi_scr[j]
                or_ref[j, rows, :] += tr * cr - ti * ci
                oi_ref[j, rows, :] += tr * ci + ti * cr
            if with_da:
                first = lax.broadcasted_iota(jnp.int32, (L, LANES), 0) == 0
                dar = jnp.zeros((1, LANES), f32)
                dai = jnp.zeros((1, LANES), f32)
                for c in range(NSUB):
                    rows = slice(c * L, (c + 1) * L)
                    if c == 0:
                        lr_, li_ = jnp.zeros((1, LANES), f32), jnp.zeros((1, LANES), f32)
                    else:
                        lr_, li_ = hr_ref[j, c * L - 1:c * L, :], hi_ref[j, c * L - 1:c * L, :]
                    hpr = jnp.where(first, lr_, pltpu.roll(hr_ref[j, rows, :], 1, 0))
                    hpi = jnp.where(first, li_, pltpu.roll(hi_ref[j, rows, :], 1, 0))
                    gr, gi = or_ref[j, rows, :], oi_ref[j, rows, :]
                    dar += jnp.sum(hpr * gr + hpi * gi, axis=0, keepdims=True)
                    dai += jnp.sum(hpr * gi - hpi * gr, axis=0, keepdims=True)
                dar_ref[j] = dar
                dai_ref[j] = dai

    slab = jax.ShapeDtypeStruct((B, NB, S, LANES), f32)
    sspec = pl.BlockSpec((None, nb, S, LANES), lambda b, j: (b, j, 0, 0))
    aspec = pl.BlockSpec((nb, 1, LANES), lambda b, j: (j, 0, 0))
    in_specs = [sspec, sspec, aspec, aspec]
    out_specs = [sspec, sspec]
    out_shape = [slab, slab]
    args = [xr, xi, ar, ai]
    if with_da:
        in_specs += [sspec, sspec]
        args += [hr, hi]
        dspec = pl.BlockSpec((None, nb, 1, LANES), lambda b, j: (b, j, 0, 0))
        out_specs += [dspec, dspec]
        out_shape += [jax.ShapeDtypeStruct((B, NB, 1, LANES), f32)] * 2
    return _call(body, name, (B, NB // nb), in_specs, out_specs, out_shape, args,
                 scratch=[pltpu.VMEM((nb, L, LANES), f32), pltpu.VMEM((nb, L, LANES), f32)],
                 sem=("parallel", "parallel"), phase=phase)


def _s5_out_parts(hr_ref, hi_ref, u_ref, cr_ref, ci_ref, d_ref, wg_ref, bg_ref, NB):
    hcr = _slab_cat(hr_ref, NB).astype(bf16)
    hci = _slab_cat(hi_ref, NB).astype(bf16)
    u = u_ref[...].astype(f32)
    y2 = _dot(hcr, cr_ref[...]) - _dot(hci, ci_ref[...]) + d_ref[...] * u
    yg = _gelu(y2)
    s = jax.nn.sigmoid(_dot(yg.astype(bf16), wg_ref[...]) + bg_ref[...])
    return hcr, hci, u, y2, yg, s


def _s5_out_specs(B, S, NB, NS, SW, tm):
    sspec = pl.BlockSpec((None, NB, tm, LANES), lambda b, i: (b, 0, i, 0))
    full = lambda r, c: pl.BlockSpec((r, c), lambda b, i: (0, 0))
    return sspec, [sspec, sspec, pl.BlockSpec((None, tm, SW), lambda b, i: (b, i, 0)), full(NS, SW), full(NS, SW),
                   full(1, SW), full(SW, SW), full(1, SW)]


def _s5_out(hr, hi, p3, cbr, cbi, dsk, wglu, bglu, tm):
    B, NB, S, _ = hr.shape
    NS, SW = cbr.shape

    def body(hr_ref, hi_ref, u_ref, cr_ref, ci_ref, d_ref, wg_ref, bg_ref, o_ref):
        _, _, _, _, yg, s = _s5_out_parts(hr_ref, hi_ref, u_ref, cr_ref, ci_ref, d_ref, wg_ref, bg_ref, NB)
        o_ref[...] = (yg * s).astype(bf16)

    _, in_specs = _s5_out_specs(B, S, NB, NS, SW, tm)
    return pl.pallas_call(
        body, name="s5_out", grid=(B, S // tm), in_specs=in_specs,
        out_specs=pl.BlockSpec((None, tm, SW), lambda b, i: (b, i, 0)),
        out_shape=jax.ShapeDtypeStruct((B, S, SW), bf16),
        compiler_params=_params(("parallel", "parallel")))(hr, hi, p3, cbr, cbi, dsk, wglu, bglu)


def _s5_out_bwd(hr, hi, p3, dmix3, cbr, cbi, dsk, wglu, bglu, tm):
    B, NB, S, _ = hr.shape
    NS, SW = cbr.shape

    def body(hr_ref, hi_ref, u_ref, cr_ref, ci_ref, d_ref, wg_ref, bg_ref, da_ref,
             dhr_ref, dhi_ref, du_ref, dcr_ref, dci_ref, dd_ref, dwg_ref, dbg_ref):
        hcr, hci, u, y2, yg, s = _s5_out_parts(hr_ref, hi_ref, u_ref, cr_ref, ci_ref, d_ref, wg_ref, bg_ref, NB)
        da = da_ref[...].astype(f32)
        dz = da * yg * s * (1.0 - s)
        dzb = dz.astype(bf16)
        dyg = da * s + _dg(dzb, wg_ref[...], NT)
        dy2 = dyg * _gelu_grad(y2)
        dyb = dy2.astype(bf16)

        @pl.when((pl.program_id(0) == 0) & (pl.program_id(1) == 0))
        def _():
            for r in (dcr_ref, dci_ref, dd_ref, dwg_ref, dbg_ref):
                r[...] = jnp.zeros_like(r)
        dwg_ref[...] += _dg(yg.astype(bf16), dzb, TN)
        dbg_ref[...] += jnp.sum(dz, axis=0, keepdims=True)
        dd_ref[...] += jnp.sum(dy2 * u, axis=0, keepdims=True)
        dcr_ref[...] += _dg(hcr, dyb, TN)
        dci_ref[...] -= _dg(hci, dyb, TN)
        du_ref[...] = dy2 * d_ref[...]
        dhr = _dg(dyb, cr_ref[...], NT)
        dhi = _dg(dyb, ci_ref[...], NT)
        for j in range(NB):
            dhr_ref[j] = dhr[:, j * LANES:(j + 1) * LANES]
            dhi_ref[j] = -dhi[:, j * LANES:(j + 1) * LANES]

    sspec, in_specs = _s5_out_specs(B, S, NB, NS, SW, tm)
    in_specs = in_specs + [pl.BlockSpec((None, tm, SW), lambda b, i: (b, i, 0))]
    full = lambda r, c: pl.BlockSpec((r, c), lambda b, i: (0, 0))
    slab = jax.ShapeDtypeStruct((B, NB, S, LANES), f32)
    mat = lambda r, c: jax.ShapeDtypeStruct((r, c), f32)
    return pl.pallas_call(
        body, name="s5_out_bwd", grid=(B, S // tm), in_specs=in_specs,
        out_specs=[sspec, sspec, pl.BlockSpec((None, tm, SW), lambda b, i: (b, i, 0)), full(NS, SW), full(NS, SW),
                   full(1, SW), full(SW, SW), full(1, SW)],
        out_shape=[slab, slab, jax.ShapeDtypeStruct((B, S, SW), f32), mat(NS, SW), mat(NS, SW), mat(1, SW),
                   mat(SW, SW), mat(1, SW)],
        compiler_params=_params(("arbitrary", "arbitrary")))(hr, hi, p3, cbr, cbi, dsk, wglu, bglu, dmix3)


def _s5_in_bwd(gr, gi, p3, bbd, du_skip, duv3, tm):
    B, NB, S, _ = gr.shape
    SW, NS2 = bbd.shape
    PW = SW + duv3.shape[2]

    def body(gr_ref, gi_ref, u_ref, b_ref, ds_ref, duv_ref, dp_ref, db_ref):
        g = jnp.concatenate([_slab_cat(gr_ref, NB), _slab_cat(gi_ref, NB)], axis=1).astype(bf16)
        du = _dg(g, b_ref[...], NT) + ds_ref[...]
        dp_ref[:, :SW] = du.astype(bf16)
        dp_ref[:, SW:] = duv_ref[...]

        @pl.when((pl.program_id(0) == 0) & (pl.program_id(1) == 0))
        def _():
            db_ref[...] = jnp.zeros_like(db_ref)
        db_ref[...] += _dg(u_ref[...], g, TN)

    sspec = pl.BlockSpec((None, NB, tm, LANES), lambda b, i: (b, 0, i, 0))
    row = lambda c: pl.BlockSpec((None, tm, c), lambda b, i: (b, i, 0))
    return pl.pallas_call(
        body, name="s5_in_bwd", grid=(B, S // tm),
        in_specs=[sspec, sspec, row(SW), pl.BlockSpec((SW, NS2), lambda b, i: (0, 0)), row(SW), row(PW - SW)],
        out_specs=[row(PW), pl.BlockSpec((SW, NS2), lambda b, i: (0, 0))],
        out_shape=[jax.ShapeDtypeStruct((B, S, PW), bf16), jax.ShapeDtypeStruct((SW, NS2), f32)],
        compiler_params=_params(("arbitrary", "arbitrary")))(gr, gi, p3, bbd, du_skip, duv3)


BIG = ['ev_w_in', 'ev_w_out', 'od_w_in', 'od_w_out', 'ffn_w_up', 'ffn_w_down']
ANY = pl.BlockSpec(memory_space=pl.ANY)


def _rtile(rows, mult):
    best = None
    for d in range(mult, min(rows, 512) + 1, mult):
        if rows % d == 0:
            best = d
    assert best is not None, (rows, mult)
    return best


def _pair_sum(name, g, recv, c_idx, out_dtype):
    NCH, R, W = g.shape
    HALF_W = W // 2
    tr = _rtile(R, 16)

    def body(c_ref, a_ref, b_ref, o_ref):
        o_ref[...] = (a_ref[...] + b_ref[...]).astype(out_dtype)

    return pl.pallas_call(
        body, name=name,
        grid_spec=pltpu.PrefetchScalarGridSpec(
            num_scalar_prefetch=1, grid=(NCH, R // tr),
            in_specs=[pl.BlockSpec((None, tr, HALF_W), lambda j, i, c: (j, i, c[0])),
                      pl.BlockSpec((None, tr, HALF_W), lambda j, i, c: (j, i, 0))],
            out_specs=pl.BlockSpec((None, tr, HALF_W), lambda j, i, c: (j, i, 0))),
        out_shape=jax.ShapeDtypeStruct((NCH, R, HALF_W), out_dtype),
        compiler_params=_params(("parallel", "parallel")))(c_idx, g, recv)


def _chip_sum(name, r3, h, k_idx):
    NCH, R, Wh = r3.shape
    tr = _rtile(R, 16)

    def body(k_ref, a_ref, own_ref, o_ref):
        own = own_ref[...].astype(f32)
        t = [jnp.where(k_ref[0] == s, own, a_ref[s].astype(f32)) for s in range(NCH)]
        o_ref[...] = ((t[0] + t[1]) + t[2]) + t[3]

    return pl.pallas_call(
        body, name=name,
        grid_spec=pltpu.PrefetchScalarGridSpec(
            num_scalar_prefetch=1, grid=(R // tr,),
            in_specs=[pl.BlockSpec((NCH, tr, Wh), lambda i, k: (0, i, 0)),
                      pl.BlockSpec((None, tr, Wh), lambda i, k: (k[0], i, 0))],
            out_specs=pl.BlockSpec((tr, Wh), lambda i, k: (i, 0))),
        out_shape=jax.ShapeDtypeStruct((R, Wh), f32),
        compiler_params=_params(("parallel",)))(k_idx, r3, h)


def _adam_math(gg, w, m, v):
    nm = ADAM_B1 * m + (1.0 - ADAM_B1) * gg
    nv = ADAM_B2 * v + (1.0 - ADAM_B2) * jnp.square(gg)
    m_hat = nm / (1.0 - ADAM_B1 ** ADAM_STEP)
    v_hat = nv / (1.0 - ADAM_B2 ** ADAM_STEP)
    return -ADAM_LR * (m_hat / (jnp.sqrt(v_hat) + ADAM_EPS) + ADAM_WD * w), nm, nv


def _adamw(name, mine, theirs, c_idx, w, m, v, lead, transposed, prev=None):
    L, R, W = w.shape
    if transposed:
        bw = LANES if W % LANES == 0 else W
        gspec = pl.BlockSpec((bw, R // 2), lambda i, hf, c: (i, 0))
        wspec = pl.BlockSpec((None, R // 2, bw), lambda i, hf, c: (lead, hf, i))
        grid = (W // bw, 2)
    else:
        tr = _rtile(R, SUBLANES)
        gspec = pl.BlockSpec((tr, W // 2), lambda i, hf, c: (i, 0))
        wspec = pl.BlockSpec((None, tr, W // 2), lambda i, hf, c: (lead, i, hf))
        grid = (R // tr, 2)

    def body(c_ref, a_ref, b_ref, w_ref, m_ref, v_ref, *rest):
        go_ref, d_ref, nm_ref, nv_ref = rest[-4:]
        gg = jnp.where(pl.program_id(1) == c_ref[0], a_ref[...], b_ref[...])
        if transposed:
            gg = gg.T
        d, nm, nv = _adam_math(gg, w_ref[...], m_ref[...], v_ref[...])
        go_ref[...] = gg
        d_ref[...] = d
        nm_ref[...] = nm
        nv_ref[...] = nv

    in_specs = [gspec, gspec, wspec, wspec, wspec]
    args, aliases = [c_idx, mine, theirs, w, m, v], {}
    if prev is not None:
        in_specs += [ANY] * 4
        args += list(prev)
        aliases = {6: 0, 7: 1, 8: 2, 9: 3}
    s = jax.ShapeDtypeStruct((L, R, W), f32)
    return pl.pallas_call(
        body, name=name,
        grid_spec=pltpu.PrefetchScalarGridSpec(num_scalar_prefetch=1, grid=grid, in_specs=in_specs,
                                               out_specs=[wspec] * 4),
        out_shape=[s, s, s, s], input_output_aliases=aliases,
        compiler_params=_params(("parallel", "arbitrary")))(*args)


def _adamw_small(gs, ws, ms, vs):
    n = len(gs)

    def body(*refs):
        for i in range(n):
            d, nm, nv = _adam_math(refs[i][...], refs[n + i][...], refs[2 * n + i][...], refs[3 * n + i][...])
            refs[4 * n + i][...] = d
            refs[5 * n + i][...] = nm
            refs[6 * n + i][...] = nv

    return pl.pallas_call(body, name="adamw_small",
                          out_shape=[jax.ShapeDtypeStruct(t.shape, f32) for t in ws] * 3)(*gs, *ws, *ms, *vs)


def _place():
    x, y, c = lax.axis_index("x"), lax.axis_index("y"), lax.axis_index("c")
    return x, y, c, [(1 - x, y), (x, 1 - y), (1 - x, 1 - y)]


def _gathered_shape(sh, kind):
    if kind == "rows":
        return sh[:-2] + (N_CHIPS * sh[-2], sh[-1])
    if kind == "cols":
        return sh[:-1] + (N_CHIPS * sh[-1],)
    return (N_CHIPS,) + sh


def _place_shard(name, shard, kind, k_idx):
    sh = shard.shape
    r, C = sh[-2], sh[-1]
    L = sh[0] if len(sh) == 3 else 1
    tr = _rtile(r, 16)
    nr = r // tr
    if kind == "rows":
        out3, omap = (L, N_CHIPS * r, C), lambda l, i, k: (l, k[0] * nr + i, 0)
    elif kind == "cols":
        out3, omap = (L, r, N_CHIPS * C), lambda l, i, k: (l, i, k[0])
    else:
        out3, omap = (N_CHIPS, r, C), lambda l, i, k: (k[0], i, 0)

    def body(k_ref, s_ref, o_ref):
        o_ref[...] = s_ref[...]

    out = pl.pallas_call(
        body, name=name,
        grid_spec=pltpu.PrefetchScalarGridSpec(
            num_scalar_prefetch=1, grid=(L, nr),
            in_specs=[pl.BlockSpec((None, tr, C), lambda l, i, k: (l, i, 0))],
            out_specs=pl.BlockSpec((None, tr, C), omap)),
        out_shape=jax.ShapeDtypeStruct(out3, shard.dtype),
        compiler_params=_params(("parallel", "parallel")))(k_idx, shard.reshape(L, r, C))
    return out.reshape(_gathered_shape(sh, kind))


def _gather_phase(shards, fulls, kinds):
    n = len(shards)
    shapes = [s.shape for s in shards]

    def window(ref, a, k, h=None):
        sh, kind = shapes[a], kinds[a]
        r = sh[-2]
        start, size = (0, r) if h is None else (h * (r // 2), r // 2)
        lead = (slice(None),) * (len(sh) - 2)
        if kind == "rows":
            return ref.at[lead + (pl.ds(k * r + start, size), slice(None))]
        if kind == "cols":
            return ref.at[lead + (pl.ds(start, size), pl.ds(pl.multiple_of(k * sh[-1], LANES), sh[-1]))]
        return ref.at[(k,) + lead + (pl.ds(start, size), slice(None))]

    def copies(s_refs, o_refs, sems):
        send_sems, recv_sems = sems
        x, y, c, chips = _place()
        k = 2 * x + y

        def copy(a, j, kk, hh, to, src=None):
            dst = window(o_refs[a], a, kk, hh)
            return pltpu.make_async_remote_copy(
                src_ref=dst if src is None else src, dst_ref=dst, send_sem=send_sems.at[6 * a + j],
                recv_sem=recv_sems.at[6 * a + j], device_id=to, device_id_type=MESH)

        first = []
        for a in range(n):
            r = shapes[a][-2]
            lead = (slice(None),) * (len(shapes[a]) - 2)
            src = s_refs[a].at[lead + (pl.ds(c * (r // 2), r // 2), slice(None))]
            first += [copy(a, j, k, c, (*chip, c), src=src) for j, chip in enumerate(chips)]
        return copy, first, (x, y, c), (x, y, 1 - c), c, chips

    def start(s_refs, o_refs, sems):
        for cp in copies(s_refs, o_refs, sems)[1]:
            cp.start()

    def finish(s_refs, o_refs, sems):
        copy, first, me, sibling, c, chips = copies(s_refs, o_refs, sems)
        passed = []
        for j, (cx, cy) in enumerate(chips):
            for a in range(n):
                copy(a, j, 2 * cx + cy, c, me).wait_recv()
                fwd = copy(a, 3 + j, 2 * cx + cy, c, sibling)
                fwd.start()
                passed.append(fwd)
        for j, (cx, cy) in enumerate(chips):
            for a in range(n):
                copy(a, 3 + j, 2 * cx + cy, 1 - c, me).wait_recv()
        for cp in first + passed:
            cp.wait_send()

    return _Phase(shards, fulls, [jax.ShapeDtypeStruct(f.shape, f.dtype) for f in fulls],
                  [pltpu.SemaphoreType.DMA((6 * n,)), pltpu.SemaphoreType.DMA((6 * n,))], start, finish)


def _swap_phase(gs):
    n = len(gs)

    def copies(g_refs, o_refs, sems):
        send_sems, recv_sems = sems
        x, y, c, _ = _place()
        half = [g.shape[2] // 2 for g in gs]
        return [pltpu.make_async_remote_copy(
            src_ref=g_refs[a].at[:, :, pl.ds(pl.multiple_of((1 - c) * half[a], LANES), half[a])], dst_ref=o_refs[a],
            send_sem=send_sems.at[a], recv_sem=recv_sems.at[a], device_id=(x, y, 1 - c), device_id_type=MESH)
            for a in range(n)]

    def start(g_refs, o_refs, sems):
        for cp in copies(g_refs, o_refs, sems):
            cp.start()

    def finish(g_refs, o_refs, sems):
        for cp in copies(g_refs, o_refs, sems):
            cp.wait()

    return _Phase(gs, [], [jax.ShapeDtypeStruct(g.shape[:2] + (g.shape[2] // 2,), g.dtype) for g in gs],
                  [pltpu.SemaphoreType.DMA((n,)), pltpu.SemaphoreType.DMA((n,))], start, finish)


def _exchange_phase(hs):
    n = len(hs)

    def copies(h_refs, o_refs, sems):
        send_sems, recv_sems = sems
        x, y, c, chips = _place()
        k = 2 * x + y

        def copy(a, j, src_slot, dst_slot):
            cx, cy = chips[j]
            return pltpu.make_async_remote_copy(
                src_ref=h_refs[a].at[src_slot], dst_ref=o_refs[a].at[dst_slot], send_sem=send_sems.at[3 * a + j],
                recv_sem=recv_sems.at[3 * a + j], device_id=(cx, cy, c), device_id_type=MESH)

        sends = [copy(a, j, 2 * cx + cy, k) for a in range(n) for j, (cx, cy) in enumerate(chips)]
        return copy, sends, k, chips

    def start(h_refs, o_refs, sems):
        for cp in copies(h_refs, o_refs, sems)[1]:
            cp.start()

    def finish(h_refs, o_refs, sems):
        copy, sends, k, chips = copies(h_refs, o_refs, sems)
        for a in range(n):
            for j, (cx, cy) in enumerate(chips):
                copy(a, j, k, 2 * cx + cy).wait_recv()
        for cp in sends:
            cp.wait_send()

    return _Phase(hs, [], [jax.ShapeDtypeStruct(h.shape, h.dtype) for h in hs],
                  [pltpu.SemaphoreType.DMA((3 * n,)), pltpu.SemaphoreType.DMA((3 * n,))], start, finish)


def _comm_pair_share(tag, gs):
    n = len(gs)

    def body(*refs):
        g_refs, o_refs, send_sems, recv_sems = refs[:n], refs[n:2 * n], refs[2 * n], refs[2 * n + 1]
        x, y, c, _ = _place()
        cps = [pltpu.make_async_remote_copy(
            src_ref=g_refs[a], dst_ref=o_refs[a], send_sem=send_sems.at[a], recv_sem=recv_sems.at[a],
            device_id=(x, y, 1 - c), device_id_type=MESH) for a in range(n)]
        for cp in cps:
            cp.start()
        for cp in cps:
            cp.wait()

    return pl.pallas_call(
        body, name="comm_pair_share_" + tag, in_specs=[ANY] * n, out_specs=[ANY] * n,
        out_shape=[jax.ShapeDtypeStruct(g.shape, g.dtype) for g in gs],
        scratch_shapes=[pltpu.SemaphoreType.DMA((n,)), pltpu.SemaphoreType.DMA((n,))])(*gs)


def _pad_rows(flat, unit):
    n = flat.shape[-1]
    pad = (-n) % unit
    if pad:
        flat = jnp.pad(flat, [(0, 0)] * (flat.ndim - 1) + [(0, pad)])
    return flat


def _split_chips(full, axis):
    sh = full.shape
    t = full.reshape(sh[:axis] + (N_CHIPS, sh[axis] // N_CHIPS) + sh[axis + 1:])
    return jnp.moveaxis(t, axis, 0).reshape(N_CHIPS, -1)


def _join_chips(stack, shard_shape, axis):
    t = jnp.moveaxis(stack.reshape((N_CHIPS,) + tuple(shard_shape)), 0, axis)
    sh = t.shape
    return t.reshape(sh[:axis] + (sh[axis] * sh[axis + 1],) + sh[axis + 2:])


def _block_diag(blocks):
    G, r, c = blocks.shape
    eye = jnp.eye(G, dtype=blocks.dtype)
    return (blocks[:, :, None, :] * eye[:, None, :, None]).reshape(G * r, G * c)


def _diag_blocks(m, G):
    r, c = m.shape[0] // G, m.shape[1] // G
    idx = jnp.arange(G)
    return m.reshape(G, r, G, c)[idx, :, idx, :]


def _weight_shards(w):
    conv = jnp.concatenate([w[n].reshape(-1) for n in GATHER_F32])
    conv = _pad_rows(conv, 2 * SUBLANES * LANES).reshape(-1, LANES)
    b16 = lambda a: a.astype(bf16)
    return {'ev_w_in': (b16(w['ev_w_in'][0]), "chip"), 'ev_w_out': (b16(w['ev_w_out'][0]), "rows"),
            's5_w_glu': (b16(w['s5_w_glu'][0]), "rows"), 'conv': (conv, "chip"),
            'od_w_in': (b16(w['od_w_in'][0]), "cols"), 'od_w_out': (b16(w['od_w_out'][0]), "rows"),
            'ffn_w_up0': (b16(w['ffn_w_up'][0]), "cols"), 'ffn_w_up1': (b16(w['ffn_w_up'][1]), "cols"),
            'ffn_w_down0': (b16(w['ffn_w_down'][0]), "rows"), 'ffn_w_down1': (b16(w['ffn_w_down'][1]), "rows")}


def kernel(x, mix_norm_g, ffn_norm_g, final_norm_g, ev_w_in, ev_w_out, s5_lam_re, s5_lam_im, s5_log_dt, s5_b_re, s5_b_im, s5_c_re, s5_c_im, s5_d, s5_w_glu, s5_b_glu, gm_w_s, gm_b_s, gm_v_g, od_w_in, od_conv_w, od_conv_b, od_w_out, ffn_w_up, ffn_conv_w, ffn_conv_b, ffn_w_down, loss_target, m_mix_norm_g, m_ffn_norm_g, m_final_norm_g, m_ev_w_in, m_ev_w_out, m_s5_lam_re, m_s5_lam_im, m_s5_log_dt, m_s5_b_re, m_s5_b_im, m_s5_c_re, m_s5_c_im, m_s5_d, m_s5_w_glu, m_s5_b_glu, m_gm_w_s, m_gm_b_s, m_gm_v_g, m_od_w_in, m_od_conv_w, m_od_conv_b, m_od_w_out, m_ffn_w_up, m_ffn_conv_w, m_ffn_conv_b, m_ffn_w_down, v_mix_norm_g, v_ffn_norm_g, v_final_norm_g, v_ev_w_in, v_ev_w_out, v_s5_lam_re, v_s5_lam_im, v_s5_log_dt, v_s5_b_re, v_s5_b_im, v_s5_c_re, v_s5_c_im, v_s5_d, v_s5_w_glu, v_s5_b_glu, v_gm_w_s, v_gm_b_s, v_gm_v_g, v_od_w_in, v_od_conv_w, v_od_conv_b, v_od_w_out, v_ffn_w_up, v_ffn_conv_w, v_ffn_conv_b, v_ffn_w_down):
    loc = dict(locals())
    w = {n: loc[n] for n in WEIGHTS}
    mom = {n: loc["m_" + n] for n in WEIGHTS}
    var = {n: loc["v_" + n] for n in WEIGHTS}

    B, S, D = x.shape
    T = B * S
    SW = s5_d.shape[1]
    G = SW // SSM_GROUP
    NS = G * SSM_STATE
    NB = NS // LANES
    tm = min(512, S)
    tt = min(1024, T)
    c_idx = lax.axis_index("c").astype(jnp.int32).reshape(1)
    k_idx = (2 * lax.axis_index("x") + lax.axis_index("y")).astype(jnp.int32).reshape(1)
    shards = _weight_shards(w)
    placed = {n: _place_shard("place_" + n, s, kd, k_idx) for n, (s, kd) in shards.items()}

    def gather(names):
        return _gather_phase([shards[n][0] for n in names], [placed[n] for n in names], [shards[n][1] for n in names])

    (w_ev_in,) = _run_phase("comm_gather_ev_in", gather(['ev_w_in']))
    w_ev_in = jnp.swapaxes(w_ev_in, 0, 1).reshape(D, -1)

    h0 = x.reshape(T, D)
    (y0, p0), (w_ev_out, w_glu, conv) = _norm_mm("ev_in", h0, mix_norm_g[0], w_ev_in, tm,
                                                 phase=gather(['ev_w_out', 's5_w_glu', 'conv']))
    full, off = {}, 0
    for n in GATHER_F32:
        full[n] = _join_chips(conv.reshape(N_CHIPS, -1)[:, off:off + w[n].size], w[n].shape, SHARD_AXIS[n])
        off += w[n].size
    PW = p0.shape[1]
    p03 = p0.reshape(B, S, PW)
    lr, li, ldt = s5_lam_re[0], s5_lam_im[0], s5_log_dt[0].reshape(G, 1)
    ar, ai, zr, zi = _s5_prep(lr, li, ldt)
    bre = _block_diag(jnp.swapaxes(s5_b_re[0], 1, 2))
    bim = _block_diag(jnp.swapaxes(s5_b_im[0], 1, 2))
    cbr = _block_diag(jnp.swapaxes(s5_c_re[0], 1, 2)).astype(bf16)
    cbi = _block_diag(jnp.swapaxes(s5_c_im[0], 1, 2)).astype(bf16)
    zr_row, zi_row = zr.reshape(1, NS), zi.reshape(1, NS)
    bbd = _s5_bbd(zr_row, zi_row, bre, bim)
    ar_s, ai_s = ar.reshape(NB, 1, LANES), ai.reshape(NB, 1, LANES)
    xr, xi = _s5_in(p03, bbd, SW, tm)
    (hr, hi), (w_up0, w_down0) = _s5_scan("s5_scan", xr, xi, ar_s, ai_s, False,
                                           phase=gather(['ffn_w_up0', 'ffn_w_down0']))
    dsk, bglu = s5_d.reshape(1, SW), s5_b_glu.reshape(1, SW)
    a_out = _s5_out(hr, hi, p03, cbr, cbi, dsk, w_glu, bglu, tm)
    ws, bst, gv = gm_w_s[0], gm_b_s[0].T, gm_v_g.reshape(1, -1)
    mixcat = _gmlp(p0, a_out.reshape(T, SW), ws, bst, gv, SW)
    h1 = _mm_resid("ev_out", mixcat, w_ev_out, h0, tm)

    def ffn_fwd(l, h, w_up, w_down, up_phase=None, down_phase=None):
        res = _norm_mm(f"ffn_up{l}", h, ffn_norm_g[l], w_up, tm, phase=up_phase)
        (z, up), got_up = res if up_phase is not None else (res, None)
        res = _ffn_down(f"ffn_down{l}", up, full['ffn_conv_w'][l], ffn_conv_b[l].reshape(1, -1), w_down, h, S, tm,
                        phase=down_phase)
        (hn, c), got_down = res if down_phase is not None else (res, None)
        return hn, (z, up.reshape(B, S, -1), c.reshape(B, S, -1)), got_up, got_down

    h2, ffn0, (w_up1, w_down1), (w_od_in, w_od_out) = ffn_fwd(
        0, h1, w_up0, w_down0, gather(['ffn_w_up1', 'ffn_w_down1']), gather(['od_w_in', 'od_w_out']))
    w_ups, w_downs = (w_up0, w_up1), (w_down0, w_down1)
    od_cw, od_cb = full['od_conv_w'][0], full['od_conv_b']
    y1, p1 = _norm_mm("od_in", h2, mix_norm_g[1], w_od_in, tm)
    p13 = p1.reshape(B, S, -1)
    sc = _od_act(p13, od_cw, od_cb)
    h3 = _mm_resid("od_out", sc.reshape(T, D), w_od_out, h2, tm)
    h4, ffn1, _, _ = ffn_fwd(1, h3, w_up1, w_down1)

    dh4, dh4b, loss_part, d_final_g = _final_loss(h4, final_norm_g, loss_target.reshape(T, D), tm)
    loss = lax.psum(loss_part[0, 0], ("x", "y", "c"))

    grads = {}

    halves = {}
    chips = lambda g: g.reshape(N_CHIPS, -1, D)

    def pair_sums(names, parts, recv):
        return [_pair_sum(f"pair_sum_{n}", g, r, c_idx, f32 if n == "small" else bf16)
                for n, g, r in zip(names, parts, recv)]

    def reduce_end(tag, names, hsum, r3):
        mine = [_chip_sum(f"chip_sum_{n}", r, h, k_idx) for n, r, h in zip(names, r3, hsum)]
        theirs = _comm_pair_share(tag, mine)
        halves.update({n: (a, b) for n, a, b in zip(names, mine, theirs)})

    def ffn_bwd(l, dh, dhb, h_in, saved, phase=None, swap=False):
        z, up3, c3 = saved
        w_down, w_up = w_downs[l], w_ups[l]
        da = _mm_nt(f"ffn_down_bwd{l}", dhb, w_down, tm)
        res = _ffn_act_bwd(f"ffn_act_bwd{l}", up3, c3, da.reshape(B, S, -1), full['ffn_conv_w'][l], phase=phase)
        (act, dg3, dv3, dcwg, dcwv, dcbg, dcbv), got = res if phase is not None else (res, None)
        g_down = _mm_tn(f"ffn_down_dw{l}", act.reshape(T, -1), dhb, tt)
        dupg, dupv = dg3.reshape(T, -1), dv3.reshape(T, -1)
        F = dupg.shape[1]
        g_up = _mm_tn(f"ffn_up_dw{l}_gate", dupg, z, tt, rows=2 * F)
        g_up = _mm_tn(f"ffn_up_dw{l}_val", dupv, z, tt, rows=2 * F, row_off=F, prev=g_up)
        parts = [chips(g_down), chips(g_up)]
        res = _mm_nt_normbwd(f"ffn_up_bwd{l}", [dupg, dupv], w_up, h_in, ffn_norm_g[l], dh, tm,
                             phase=_swap_phase(parts) if swap else None)
        (dh_new, dhb_new, dg), recv = res if swap else (res, None)
        F = dg3.shape[2]
        dcw = jnp.concatenate([dcwg[:, :F], dcwv[:, :F]], axis=1)
        dcb = jnp.concatenate([dcbg[:, :F], dcbv[:, :F]], axis=1)
        return dh_new, dhb_new, g_down, g_up, dcw, dcb[0], dg[0], got, parts, recv

    dh3, dh3b, gd1, gu1, gcw1, gcb1, gng1, _, _, _ = ffn_bwd(1, dh4, dh4b, h3, ffn1)
    dsc = _mm_nt("od_out_bwd", dh3b, w_od_out, tm)
    g_od_out = _mm_tn("od_out_dw", sc.reshape(T, D), dh3b, tt)
    dbg3, dcg3, dhx3, d_od_cw, d_od_cb = _od_act_bwd(p13, dsc.reshape(B, S, D), od_cw, od_cb)
    dp1 = [t.reshape(T, D) for t in (dbg3, dcg3, dhx3)]
    g_od_in = None
    for i, piece in enumerate(dp1):
        g_od_in = _mm_tn(f"od_in_dw{i}", piece, y1, tt, rows=3 * D, row_off=i * D, prev=g_od_in)
    grads['od_conv_w'] = d_od_cw[None]
    grads['od_conv_b'] = d_od_cb
    layer1 = ['ffn_w_down1', 'ffn_w_up1', 'od_w_out', 'od_w_in']
    parts1 = [chips(g) for g in (gd1, gu1, g_od_out, g_od_in)]
    (dh2, dh2b, gmix1), recv1 = _mm_nt_normbwd("od_in_bwd", dp1, w_od_in, h2, mix_norm_g[1], dh3, tm,
                                               phase=_swap_phase(parts1))
    hsum1 = pair_sums(layer1, parts1, recv1)
    dh1, dh1b, gd0, gu0, gcw0, gcb0, gng0, r3, parts0, recv0 = ffn_bwd(
        0, dh2, dh2b, h1, ffn0, phase=_exchange_phase(hsum1), swap=True)
    reduce_end("layer1", layer1, hsum1, r3)
    ffn0_names = ['ffn_w_down0', 'ffn_w_up0']
    hsum0 = pair_sums(ffn0_names, parts0, recv0)
    grads['ffn_conv_w'] = jnp.stack([gcw0, gcw1])
    grads['ffn_conv_b'] = jnp.stack([gcb0, gcb1])
    grads['ffn_norm_g'] = jnp.stack([gng0, gng1])
    grads['final_norm_g'] = d_final_g[0]

    dmix = _mm_nt("ev_out_bwd", dh1b, w_ev_out, tm)
    g_ev_out = _mm_tn("ev_out_dw", mixcat, dh1b, tt)
    duv, d_ws, d_bs, d_gv = _gmlp_bwd(p0, dmix, ws, bst, gv, SW)
    grads['gm_w_s'] = d_ws[None]
    grads['gm_b_s'] = d_bs[:, :, 0][None]
    grads['gm_v_g'] = d_gv
    dhr, dhi, du_skip, d_cbr, d_cbi, d_dsk, d_wglu, d_bglu = _s5_out_bwd(
        hr, hi, p03, dmix.reshape(B, S, D), cbr, cbi, dsk, w_glu, bglu, tm)
    grads['s5_c_re'] = jnp.swapaxes(_diag_blocks(d_cbr, G), 1, 2)[None]
    grads['s5_c_im'] = jnp.swapaxes(_diag_blocks(d_cbi, G), 1, 2)[None]
    grads['s5_d'] = d_dsk
    grads['s5_w_glu'] = d_wglu[None]
    grads['s5_b_glu'] = d_bglu
    (gr, gi, dar, dai), r3 = _s5_scan("s5_rscan", dhr, dhi, ar_s, ai_s, True, hr, hi, phase=_exchange_phase(hsum0))
    reduce_end("ffn0", ffn0_names, hsum0, r3)
    dp03, d_bbd = _s5_in_bwd(gr, gi, p03, bbd, du_skip, duv.reshape(B, S, -1), tm)
    d_bre, d_bim, d_zr, d_zi = _s5_bbd_bwd(d_bbd, zr_row, zi_row, bre, bim)
    grads['s5_b_re'] = jnp.swapaxes(_diag_blocks(d_bre, G), 1, 2)[None]
    grads['s5_b_im'] = jnp.swapaxes(_diag_blocks(d_bim, G), 1, 2)[None]
    shp = (-1, G, SSM_STATE)
    d_lr, d_li, d_ldt = _s5_prep_bwd(lr, li, ldt, dar.reshape(shp), dai.reshape(shp), d_zr.reshape(shp),
                                     d_zi.reshape(shp))
    grads['s5_lam_re'] = d_lr[None]
    grads['s5_lam_im'] = d_li[None]
    grads['s5_log_dt'] = d_ldt.reshape(1, G)
    dp0 = dp03.reshape(T, PW)
    g_ev_in = _mm_tn("ev_in_dw", dp0, y0, tt)
    grad_x, _, gmix0 = _mm_nt_normbwd("ev_in_bwd", [dp0], w_ev_in, h0, mix_norm_g[0], dh1, tm)
    grads['mix_norm_g'] = jnp.concatenate([gmix0, gmix1], axis=0)

    small = [n for n in WEIGHTS if n not in BIG]
    segs = []
    for n in small:
        gfull = grads[n].astype(f32)
        if n in SHARD_AXIS:
            segs.append(_split_chips(gfull, SHARD_AXIS[n]))
        else:
            segs.append(jnp.broadcast_to(gfull.reshape(1, -1), (N_CHIPS, gfull.size)))
    unit = 2 * SUBLANES * D
    gsmall = _pad_rows(jnp.concatenate(segs, axis=1), unit).reshape(N_CHIPS, -1, D)
    mixer0 = ['ev_w_out', 'ev_w_in', 'small']
    parts = [chips(g_ev_out), chips(g_ev_in), gsmall]
    hsum = pair_sums(mixer0, parts, _run_phase("comm_pair_swap_mixer0", _swap_phase(parts)))
    reduce_end("mixer0", mixer0, hsum, _run_phase("comm_exchange_mixer0", _exchange_phase(hsum)))

    out_g, out_d, out_m, out_v = {}, {}, {}, {}

    def update(n, key, lead, transposed, prev=None):
        res = _adamw(f"adamw_{key}", *halves[key], c_idx, w[n], mom[n], var[n], lead, transposed, prev)
        out_g[n], out_d[n], out_m[n], out_v[n] = res
        return res

    update('ev_w_in', 'ev_w_in', 0, True)
    update('ev_w_out', 'ev_w_out', 0, False)
    update('od_w_in', 'od_w_in', 0, True)
    update('od_w_out', 'od_w_out', 0, False)
    update('ffn_w_up', 'ffn_w_up0', 0, True, prev=update('ffn_w_up', 'ffn_w_up1', 1, True))
    update('ffn_w_down', 'ffn_w_down0', 0, False, prev=update('ffn_w_down', 'ffn_w_down1', 1, False))

    mine, theirs = halves['small']
    first = lax.axis_index("c") == 0
    flat = jnp.concatenate([jnp.where(first, mine, theirs), jnp.where(first, theirs, mine)], axis=1).reshape(-1)
    off = 0
    for n in small:
        out_g[n] = flat[off:off + w[n].size].reshape(w[n].shape)
        off += w[n].size
    res = _adamw_small([out_g[n] for n in small], [w[n] for n in small], [mom[n] for n in small],
                       [var[n] for n in small])
    for i, n in enumerate(small):
        out_d[n], out_m[n], out_v[n] = res[i], res[len(small) + i], res[2 * len(small) + i]

    return (loss, grad_x.reshape(B, S, D), *[out_g[n] for n in WEIGHTS], *[out_d[n] for n in WEIGHTS],
            *[out_m[n] for n in WEIGHTS], *[out_v[n] for n in WEIGHTS])
```

```python
import functools
import math

import jax
import jax.numpy as jnp
from jax import lax
from jax.experimental import pallas as pl
from jax.experimental.pallas import tpu as pltpu

f32 = jnp.float32
bf16 = jnp.bfloat16
MESH = pl.DeviceIdType.MESH

SSM_GROUP = 16
SSM_STATE = 64
GMLP_HEAD = 128
CHUNK = 128
EPS = 1e-6
LAMBDA_RE_MAX = -1e-4
ADAM_LR, ADAM_B1, ADAM_B2, ADAM_EPS, ADAM_WD, ADAM_STEP = 0.001, 0.9, 0.999, 1e-08, 0.01, 10

LANES = 128
SUBLANES = 8
NSUB = 32
HALO = 16
VMEM_LIMIT = 56 * 1024 * 1024
N_CHIPS = 4

WEIGHTS = ['mix_norm_g', 'ffn_norm_g', 'final_norm_g', 'ev_w_in', 'ev_w_out', 's5_lam_re', 's5_lam_im', 's5_log_dt',
           's5_b_re', 's5_b_im', 's5_c_re', 's5_c_im', 's5_d', 's5_w_glu', 's5_b_glu', 'gm_w_s', 'gm_b_s', 'gm_v_g',
           'od_w_in', 'od_conv_w', 'od_conv_b', 'od_w_out', 'ffn_w_up', 'ffn_conv_w', 'ffn_conv_b', 'ffn_w_down']
SHARD_AXIS = {'ev_w_in': 2, 'ev_w_out': 1, 's5_w_glu': 1, 'od_w_in': 2, 'od_conv_w': 2, 'od_conv_b': 1, 'od_w_out': 1,
              'ffn_w_up': 2, 'ffn_conv_w': 2, 'ffn_w_down': 1}
GATHER_BF16 = ['ev_w_in', 'ev_w_out', 's5_w_glu', 'od_w_in', 'od_w_out', 'ffn_w_up', 'ffn_w_down']
GATHER_F32 = ['od_conv_w', 'od_conv_b', 'ffn_conv_w']

_GELU_K0 = math.sqrt(2.0 / math.pi)
_GELU_K1 = 0.044715
NT = (((1,), (1,)), ((), ()))
TN = (((0,), (0,)), ((), ()))


def _pick(n, cap):
    if n <= cap:
        return n
    best = None
    for d in range(LANES, cap + 1, LANES):
        if n % d == 0:
            best = d
    assert best is not None, (n, cap)
    return best


def _params(sem=None):
    return pltpu.CompilerParams(dimension_semantics=sem, vmem_limit_bytes=VMEM_LIMIT)


class _Phase:
    def __init__(self, ins, inplace, outs, sems, start, finish):
        self.ins, self.inplace, self.outs, self.sems = list(ins), list(inplace), list(outs), list(sems)
        self.start, self.finish = start, finish


def _call(body, name, grid, in_specs, out_specs, out_shape, args, scratch=(), sem=None, phase=None):
    if phase is None:
        return pl.pallas_call(body, name=name, grid=grid, in_specs=in_specs, out_specs=out_specs, out_shape=out_shape,
                              scratch_shapes=list(scratch), compiler_params=_params(sem))(*args)
    any_spec = pl.BlockSpec(memory_space=pl.ANY)
    n_in, n_out, n_scr = len(args), len(out_shape), len(scratch)
    p_in = phase.ins + phase.inplace
    ci, co = len(p_in), len(phase.outs)

    def wrapped(*refs):
        ins, cins = refs[:n_in], refs[n_in:n_in + len(phase.ins)]
        b = n_in + ci
        outs, couts = refs[b:b + n_out], refs[b + n_out:b + n_out + co]
        d = b + n_out + co
        scr, csem = refs[d:d + n_scr], refs[d + n_scr:]
        ids = [pl.program_id(i) for i in range(len(grid))]
        first = functools.reduce(jnp.logical_and, [i == 0 for i in ids])
        last = functools.reduce(jnp.logical_and, [i == g - 1 for i, g in zip(ids, grid)])

        @pl.when(first)
        def _():
            phase.start(cins, couts, csem)
        body(*ins, *outs, *scr)

        @pl.when(last)
        def _():
            phase.finish(cins, couts, csem)

    res = pl.pallas_call(
        wrapped, name=name, grid=grid, in_specs=list(in_specs) + [any_spec] * ci,
        out_specs=list(out_specs) + [any_spec] * co, out_shape=list(out_shape) + phase.outs,
        scratch_shapes=list(scratch) + phase.sems,
        input_output_aliases={n_in + len(phase.ins) + i: n_out + i for i in range(len(phase.inplace))},
        compiler_params=_params(tuple("arbitrary" for _ in grid)))(*args, *p_in)
    return res[:n_out], res[n_out:]


def _run_phase(name, phase):
    any_spec = pl.BlockSpec(memory_space=pl.ANY)
    ni, ci, co = len(phase.ins), len(phase.ins) + len(phase.inplace), len(phase.outs)

    def body(*refs):
        cins, couts, csem = refs[:ni], refs[ci:ci + co], refs[ci + co:]
        phase.start(cins, couts, csem)
        phase.finish(cins, couts, csem)

    return pl.pallas_call(
        body, name=name, in_specs=[any_spec] * ci, out_specs=[any_spec] * co, out_shape=phase.outs,
        scratch_shapes=phase.sems, input_output_aliases={ni + i: i for i in range(len(phase.inplace))})(
            *phase.ins, *phase.inplace)


def _gelu(x):
    return 0.5 * x * (1.0 + jnp.tanh(_GELU_K0 * (x + _GELU_K1 * x * x * x)))


def _gelu_grad(x):
    t = jnp.tanh(_GELU_K0 * (x + _GELU_K1 * x * x * x))
    return 0.5 * (1.0 + t) + 0.5 * x * (1.0 - t * t) * _GELU_K0 * (1.0 + 3.0 * _GELU_K1 * x * x)


def _rms_stats(x):
    r = lax.rsqrt(jnp.mean(x * x, axis=-1, keepdims=True) + EPS)
    return x * r, r


def _rms_bwd(dy, xh, r, g):
    dxh = dy * g
    dx = r * (dxh - xh * jnp.mean(dxh * xh, axis=-1, keepdims=True))
    return dx, jnp.sum(dy * xh, axis=0, keepdims=True)


def _dot(a, b):
    return jnp.dot(a, b, preferred_element_type=f32)


def _dg(a, b, dims):
    return lax.dot_general(a, b, dims, preferred_element_type=f32)


def _row_fold(z):
    return z.reshape(z.shape[0] // SUBLANES, SUBLANES, z.shape[1]).sum(axis=0)


def _norm_mm(name, h, g, w, tm, phase=None):
    T, D = h.shape
    N = w.shape[1]
    nc = _pick(N, 512)

    def body(h_ref, g_ref, w_ref, y_ref, o_ref):
        xh, _ = _rms_stats(h_ref[...])
        y = (xh * g_ref[...]).astype(bf16)
        y_ref[...] = y
        for j in range(N // nc):
            o_ref[:, j * nc:(j + 1) * nc] = _dot(y, w_ref[:, j * nc:(j + 1) * nc]).astype(bf16)

    return _call(
        body, name, (T // tm,),
        [pl.BlockSpec((tm, D), lambda i: (i, 0)), pl.BlockSpec((1, D), lambda i: (0, 0)),
         pl.BlockSpec((D, N), lambda i: (0, 0))],
        [pl.BlockSpec((tm, D), lambda i: (i, 0)), pl.BlockSpec((tm, N), lambda i: (i, 0))],
        [jax.ShapeDtypeStruct((T, D), bf16), jax.ShapeDtypeStruct((T, N), bf16)],
        [h, g.reshape(1, D), w], sem=("parallel",), phase=phase)


def _mm_resid(name, a, w, resid, tm):
    T, K = a.shape
    N = w.shape[1]

    def body(a_ref, w_ref, r_ref, o_ref):
        o_ref[...] = r_ref[...] + _dot(a_ref[...], w_ref[...])

    return pl.pallas_call(
        body, name=name, grid=(T // tm,),
        in_specs=[pl.BlockSpec((tm, K), lambda i: (i, 0)), pl.BlockSpec((K, N), lambda i: (0, 0)),
                  pl.BlockSpec((tm, N), lambda i: (i, 0))],
        out_specs=pl.BlockSpec((tm, N), lambda i: (i, 0)),
        out_shape=jax.ShapeDtypeStruct((T, N), f32),
        compiler_params=_params(("parallel",)))(a, w, resid)


def _mm_nt(name, dy, w, tm):
    T, N = dy.shape
    K = w.shape[0]
    kc = _pick(K, 512)

    def body(d_ref, w_ref, o_ref):
        d = d_ref[...].astype(bf16)
        for j in range(K // kc):
            o_ref[:, j * kc:(j + 1) * kc] = _dg(d, w_ref[j * kc:(j + 1) * kc, :], NT).astype(bf16)

    return pl.pallas_call(
        body, name=name, grid=(T // tm,),
        in_specs=[pl.BlockSpec((tm, N), lambda i: (i, 0)), pl.BlockSpec((K, N), lambda i: (0, 0))],
        out_specs=pl.BlockSpec((tm, K), lambda i: (i, 0)),
        out_shape=jax.ShapeDtypeStruct((T, K), bf16),
        compiler_params=_params(("parallel",)))(dy, w)


def _mm_nt_normbwd(name, dys, w, h, g, dh_in, tm, phase=None):
    n = len(dys)
    T = dys[0].shape[0]
    D = w.shape[0]
    widths = [d.shape[1] for d in dys]
    offs = [sum(widths[:i]) for i in range(n)]

    def body(*refs):
        d_refs = refs[:n]
        w_ref, h_ref, g_ref, dh_ref, o_ref, ob_ref, dg_ref = refs[n:]
        dz = _dg(d_refs[0][...], w_ref[:, :widths[0]], NT)
        for i in range(1, n):
            dz += _dg(d_refs[i][...], w_ref[:, offs[i]:offs[i] + widths[i]], NT)
        xh, r = _rms_stats(h_ref[...])
        dx, dg = _rms_bwd(dz, xh, r, g_ref[...])
        out = dh_ref[...] + dx
        o_ref[...] = out
        ob_ref[...] = out.astype(bf16)

        @pl.when(pl.program_id(0) == 0)
        def _():
            dg_ref[...] = jnp.zeros_like(dg_ref)
        dg_ref[...] += dg

    row = lambda c: pl.BlockSpec((tm, c), lambda i: (i, 0))
    return _call(
        body, name, (T // tm,),
        [row(c) for c in widths] + [pl.BlockSpec((D, sum(widths)), lambda i: (0, 0)), row(D),
                                    pl.BlockSpec((1, D), lambda i: (0, 0)), row(D)],
        [row(D), row(D), pl.BlockSpec((1, D), lambda i: (0, 0))],
        [jax.ShapeDtypeStruct((T, D), f32), jax.ShapeDtypeStruct((T, D), bf16), jax.ShapeDtypeStruct((1, D), f32)],
        [*dys, w, h, g.reshape(1, D), dh_in], sem=("arbitrary",), phase=phase)


def _mm_tn(name, a, b, tt, rows=None, row_off=0, prev=None):
    T, K = a.shape
    N = b.shape[1]
    rows = K if rows is None else rows
    tk = _pick(K, 1408)
    tn = _pick(N, 1024)
    assert row_off % tk == 0
    kb = row_off // tk

    def body(a_ref, b_ref, *rest):
        o_ref = rest[-1]

        @pl.when(pl.program_id(2) == 0)
        def _():
            o_ref[...] = jnp.zeros_like(o_ref)
        o_ref[...] += _dg(a_ref[...], b_ref[...], TN)

    in_specs = [pl.BlockSpec((tt, tk), lambda k, n, t: (t, k)), pl.BlockSpec((tt, tn), lambda k, n, t: (t, n))]
    args, aliases = [a, b], {}
    if prev is not None:
        in_specs.append(ANY)
        args.append(prev)
        aliases = {2: 0}
    return pl.pallas_call(
        body, name=name, grid=(K // tk, N // tn, T // tt), in_specs=in_specs,
        out_specs=pl.BlockSpec((tk, tn), lambda k, n, t: (k + kb, n)),
        out_shape=jax.ShapeDtypeStruct((rows, N), f32), input_output_aliases=aliases,
        compiler_params=_params(("parallel", "parallel", "arbitrary")))(*args)


def _final_loss(h, g, tgt, tm):
    T, D = h.shape

    def body(h_ref, g_ref, t_ref, dh_ref, dhb_ref, loss_ref, dg_ref):
        xh, r = _rms_stats(h_ref[...])
        gg = g_ref[...]
        diff = xh * gg - t_ref[...]
        dy = diff * (1.0 / D)
        dx, dg = _rms_bwd(dy, xh, r, gg)
        dh_ref[...] = dx
        dhb_ref[...] = dx.astype(bf16)

        @pl.when(pl.program_id(0) == 0)
        def _():
            dg_ref[...] = jnp.zeros_like(dg_ref)
            loss_ref[...] = jnp.zeros_like(loss_ref)
        dg_ref[...] += dg
        loss_ref[...] += (0.5 / D) * jnp.sum(jnp.sum(diff * diff, axis=1, keepdims=True), axis=0, keepdims=True)

    return pl.pallas_call(
        body, name="final_loss", grid=(T // tm,),
        in_specs=[pl.BlockSpec((tm, D), lambda i: (i, 0)), pl.BlockSpec((1, D), lambda i: (0, 0)),
                  pl.BlockSpec((tm, D), lambda i: (i, 0))],
        out_specs=[pl.BlockSpec((tm, D), lambda i: (i, 0)), pl.BlockSpec((tm, D), lambda i: (i, 0)),
                   pl.BlockSpec((1, 1), lambda i: (0, 0)), pl.BlockSpec((1, D), lambda i: (0, 0))],
        out_shape=[jax.ShapeDtypeStruct((T, D), f32), jax.ShapeDtypeStruct((T, D), bf16),
                   jax.ShapeDtypeStruct((1, 1), f32), jax.ShapeDtypeStruct((1, D), f32)],
        compiler_params=_params(("arbitrary",)))(h, g.reshape(1, D), tgt)


def _taps(load, r0, R):
    main = load(r0, R)
    hs = pl.multiple_of(jnp.maximum(r0 - HALO, 0), HALO)
    halo = load(hs, HALO) * (r0 > 0).astype(f32)
    ext = jnp.concatenate([halo, main], axis=0)
    xm1 = pltpu.roll(ext, 1, 0)[HALO:]
    xm2 = pltpu.roll(ext, 2, 0)[HALO:]
    return xm2, xm1, main


def _conv(w, b, taps):
    return b + w[0:1] * taps[0] + w[1:2] * taps[1] + w[2:3] * taps[2]


def _ref_load(ref):
    return lambda s, n: ref[pl.ds(s, n), :].astype(f32)


def _ffn_down(name, up, cw, cb, w_down, resid, S, tm, phase=None):
    T, F2 = up.shape
    F = F2 // 2
    D = w_down.shape[1]
    cwid = _pick(F, 256)
    per_seq = S // tm

    def body(u_ref, halo_ref, cw_ref, cb_ref, w_ref, r_ref, o_ref, c_ref):
        keep = (pl.program_id(0) % per_seq > 0).astype(f32)

        def conv(off):
            cols = slice(off, off + cwid)
            main = u_ref[:, cols].astype(f32)
            ext = jnp.concatenate([halo_ref[:, cols].astype(f32) * keep, main], axis=0)
            taps = (pltpu.roll(ext, 2, 0)[HALO:], pltpu.roll(ext, 1, 0)[HALO:], main)
            return _conv(cw_ref[:, cols], cb_ref[:, cols], taps)

        acc = r_ref[...]
        for j in range(F // cwid):
            cg, cv = conv(j * cwid), conv(F + j * cwid)
            c_ref[:, j * cwid:(j + 1) * cwid] = cg.astype(bf16)
            c_ref[:, F + j * cwid:F + (j + 1) * cwid] = cv.astype(bf16)
            a = (cg * jax.nn.sigmoid(cg) * cv).astype(bf16)
            acc = acc + _dot(a, w_ref[j * cwid:(j + 1) * cwid, :])
        o_ref[...] = acc

    full = lambda r, c: pl.BlockSpec((r, c), lambda i: (0, 0))
    return _call(
        body, name, (T // tm,),
        [pl.BlockSpec((tm, F2), lambda i: (i, 0)),
         pl.BlockSpec((HALO, F2), lambda i: (jnp.maximum(i * (tm // HALO) - 1, 0), 0)),
         full(3, F2), full(1, F2), full(F, D), pl.BlockSpec((tm, D), lambda i: (i, 0))],
        [pl.BlockSpec((tm, D), lambda i: (i, 0)), pl.BlockSpec((tm, F2), lambda i: (i, 0))],
        [jax.ShapeDtypeStruct((T, D), f32), jax.ShapeDtypeStruct((T, F2), bf16)],
        [up, up, cw, cb, w_down, resid], sem=("parallel",), phase=phase)


def _rev_conv_rows(d, nxt, w):
    R = d.shape[0]
    ext = jnp.concatenate([d, nxt], axis=0)
    n = R + HALO
    xp1 = pltpu.roll(ext, n - 1, 0)[:R]
    xp2 = pltpu.roll(ext, n - 2, 0)[:R]
    return w[2:3] * d + w[1:2] * xp1 + w[0:1] * xp2, xp1, xp2


def _conv_grad_acc(acc, dc, taps):
    return (acc[0] + _row_fold(dc * taps[0]), acc[1] + _row_fold(dc * taps[1]), acc[2] + _row_fold(dc * taps[2]),
            acc[3] + _row_fold(dc))


def _conv_grad_out(dcw_ref, dcb_ref, acc):
    @pl.when(pl.program_id(1) == 0)
    def _():
        dcw_ref[...] = jnp.zeros_like(dcw_ref)
        dcb_ref[...] = jnp.zeros_like(dcb_ref)
    for k in range(3):
        dcw_ref[k:k + 1, :] += jnp.sum(acc[k], axis=0, keepdims=True)
    dcb_ref[...] += jnp.sum(acc[3], axis=0, keepdims=True)


def _ffn_act_bwd(name, up3, c3, da3, cw, phase=None):
    B, S, F2 = up3.shape
    F = F2 // 2
    cwid = _pick(F, 256)
    nF = F // cwid
    R = min(256, S)
    nR = S // R

    def body(xg_ref, xv_ref, cg_ref, cv_ref, da_ref, wg_ref, wv_ref,
             act_ref, dg_ref, dv_ref, dcwg_ref, dcwv_ref, dcbg_ref, dcbv_ref, sum_scr):
        wg, wv = wg_ref[...], wv_ref[...]

        def half(d, nxt, w, x_ref, rows, acc, out_ref):
            out, xp1, xp2 = _rev_conv_rows(d, nxt, w)
            out_ref[rows, :] = out.astype(bf16)
            x = x_ref[rows, :].astype(f32)
            return (acc[0] + _row_fold(xp2 * x), acc[1] + _row_fold(xp1 * x), acc[2] + _row_fold(d * x),
                    acc[3] + _row_fold(d))

        def step(i, carry):
            ng, nv, accg, accv = carry
            rows = pl.ds(pl.multiple_of((nR - 1 - i) * R, R), R)
            cg, cv = cg_ref[rows, :].astype(f32), cv_ref[rows, :].astype(f32)
            da = da_ref[rows, :].astype(f32)
            sg = jax.nn.sigmoid(cg)
            act_ref[rows, :] = (cg * sg * cv).astype(bf16)
            dgate = da * cv * (sg * (1.0 + cg * (1.0 - sg)))
            dval = da * (cg * sg)
            accg = half(dgate, ng, wg, xg_ref, rows, accg, dg_ref)
            accv = half(dval, nv, wv, xv_ref, rows, accv, dv_ref)
            return dgate[:HALO], dval[:HALO], accg, accv
        z = jnp.zeros((SUBLANES, cwid), f32)
        zh = jnp.zeros((HALO, cwid), f32)
        _, _, accg, accv = lax.fori_loop(0, nR, step, (zh, zh, (z, z, z, z), (z, z, z, z)))
        j = pl.program_id(1)
        for half_i, (acc, dcw_ref, dcb_ref) in enumerate(((accg, dcwg_ref, dcbg_ref), (accv, dcwv_ref, dcbv_ref))):
            @pl.when(pl.program_id(0) == 0)
            def _():
                sum_scr[half_i, j] = jnp.zeros((SUBLANES, cwid), f32)
            for k in range(4):
                sum_scr[half_i, j, k:k + 1, :] += jnp.sum(acc[k], axis=0, keepdims=True)
            dcw_ref[...] = sum_scr[half_i, j, 0:3, :]
            dcb_ref[...] = sum_scr[half_i, j, 3:4, :]

    blk = lambda off: pl.BlockSpec((None, S, cwid), lambda b, j: (b, 0, off + j))
    wblk = lambda off: pl.BlockSpec((3, cwid), lambda b, j: (0, off + j))
    sums = lambda r: pl.BlockSpec((r, cwid), lambda b, j: (0, jnp.where(b == B - 1, j, nF)))
    half_shape = jax.ShapeDtypeStruct((B, S, F), bf16)
    return _call(
        body, name, (B, nF),
        [blk(0), blk(nF), blk(0), blk(nF), blk(0), wblk(0), wblk(nF)],
        [blk(0), blk(0), blk(0), sums(3), sums(3), sums(1), sums(1)],
        [half_shape, half_shape, half_shape, jax.ShapeDtypeStruct((3, F + cwid), f32),
         jax.ShapeDtypeStruct((3, F + cwid), f32), jax.ShapeDtypeStruct((1, F + cwid), f32),
         jax.ShapeDtypeStruct((1, F + cwid), f32)],
        [up3, up3, c3, c3, da3, cw, cw], scratch=[pltpu.VMEM((2, nF, SUBLANES, cwid), f32)],
        sem=("arbitrary", "arbitrary"), phase=phase)


def _od_act(p3, cw, cb):
    B, S, D3 = p3.shape
    D = D3 // 3
    cwid = _pick(D, 256)
    nD = D // cwid
    R = min(256, S)

    def body(bg_ref, cg_ref, hx_ref, w_ref, b_ref, o_ref):
        w, b = w_ref[...], b_ref[...]
        q = lambda s, n: cg_ref[pl.ds(s, n), :].astype(f32) * hx_ref[pl.ds(s, n), :].astype(f32)

        def chunk(r, c):
            r0 = pl.multiple_of(r * R, R)
            cq = _conv(w, b, _taps(q, r0, R))
            o_ref[pl.ds(r0, R), :] = (bg_ref[pl.ds(r0, R), :].astype(f32) * cq).astype(bf16)
            return c
        lax.fori_loop(0, S // R, chunk, 0)

    blk = lambda off: pl.BlockSpec((None, S, cwid), lambda b, j: (b, 0, off + j))
    return pl.pallas_call(
        body, name="od_act", grid=(B, nD),
        in_specs=[blk(0), blk(nD), blk(2 * nD), pl.BlockSpec((3, cwid), lambda b, j: (0, j)),
                  pl.BlockSpec((1, cwid), lambda b, j: (0, j))],
        out_specs=pl.BlockSpec((None, S, cwid), lambda b, j: (b, 0, j)),
        out_shape=jax.ShapeDtypeStruct((B, S, D), bf16),
        compiler_params=_params(("parallel", "parallel")))(p3, p3, p3, cw, cb)


def _od_act_bwd(p3, dsc3, cw, cb):
    B, S, D3 = p3.shape
    D = D3 // 3
    cwid = _pick(D, 256)
    nD = D // cwid
    R = min(256, S)
    nR = S // R

    def body(bg_ref, cg_ref, hx_ref, d_ref, w_ref, b_ref, dbg_ref, dcg_ref, dhx_ref, dcw_ref, dcb_ref):
        w, b = w_ref[...], b_ref[...]
        q = lambda s, n: cg_ref[pl.ds(s, n), :].astype(f32) * hx_ref[pl.ds(s, n), :].astype(f32)

        def step(i, carry):
            nxt, acc = carry
            r0 = pl.multiple_of((nR - 1 - i) * R, R)
            rows = pl.ds(r0, R)
            tq = _taps(q, r0, R)
            cq = _conv(w, b, tq)
            d = d_ref[rows, :].astype(f32)
            dbg_ref[rows, :] = (d * cq).astype(bf16)
            dcq = d * bg_ref[rows, :].astype(f32)
            dq, _, _ = _rev_conv_rows(dcq, nxt, w)
            dcg_ref[rows, :] = (dq * hx_ref[rows, :].astype(f32)).astype(bf16)
            dhx_ref[rows, :] = (dq * cg_ref[rows, :].astype(f32)).astype(bf16)
            return dcq[:HALO], _conv_grad_acc(acc, dcq, tq)
        z = jnp.zeros((SUBLANES, cwid), f32)
        _, acc = lax.fori_loop(0, nR, step, (jnp.zeros((HALO, cwid), f32), (z, z, z, z)))
        _conv_grad_out(dcw_ref, dcb_ref, acc)

    blk = lambda off: pl.BlockSpec((None, S, cwid), lambda j, b: (b, 0, off + j))
    part = jax.ShapeDtypeStruct((B, S, D), bf16)
    return pl.pallas_call(
        body, name="od_act_bwd", grid=(nD, B),
        in_specs=[blk(0), blk(nD), blk(2 * nD), blk(0), pl.BlockSpec((3, cwid), lambda j, b: (0, j)),
                  pl.BlockSpec((1, cwid), lambda j, b: (0, j))],
        out_specs=[blk(0), blk(0), blk(0), pl.BlockSpec((3, cwid), lambda j, b: (0, j)),
                   pl.BlockSpec((1, cwid), lambda j, b: (0, j))],
        out_shape=[part, part, part, jax.ShapeDtypeStruct((3, D), f32), jax.ShapeDtypeStruct((1, D), f32)],
        compiler_params=_params(("parallel", "arbitrary")))(p3, p3, p3, dsc3, cw, cb)


def _gmlp_parts(p, gv, SW, GW):
    uv = p[:, SW:].astype(f32)
    ge = _gelu(uv)
    u, v = ge[:, :GW], ge[:, GW:]
    vh, r = _rms_stats(v)
    return uv, u, vh, r, vh * gv


def _tril():
    rows = lax.broadcasted_iota(jnp.int32, (CHUNK, CHUNK), 0)
    cols = lax.broadcasted_iota(jnp.int32, (CHUNK, CHUNK), 1)
    return rows >= cols


def _chunks_per_step(T):
    return 4 if T % (4 * CHUNK) == 0 else 1


def _gmlp(p0, a_out, ws, bst, gv, SW):
    T, PW = p0.shape
    GW = (PW - SW) // 2
    H = GW // GMLP_HEAD
    D = SW + GW

    kc = _chunks_per_step(T)
    rb = kc * CHUNK

    def body(p_ref, a_ref, ws_ref, b_ref, gv_ref, o_ref):
        tri = _tril()
        o_ref[:, :SW] = a_ref[...]
        wm = [jnp.where(tri, ws_ref[hh], 0.0).astype(bf16) for hh in range(H)]
        for q in range(kc):
            rows = slice(q * CHUNK, (q + 1) * CHUNK)
            _, u, _, _, vn = _gmlp_parts(p_ref[rows, :], gv_ref[...], SW, GW)
            for hh in range(H):
                sl = slice(hh * GMLP_HEAD, (hh + 1) * GMLP_HEAD)
                gate = _dot(wm[hh], vn[:, sl].astype(bf16)) + b_ref[:, hh:hh + 1]
                o_ref[rows, SW + hh * GMLP_HEAD:SW + (hh + 1) * GMLP_HEAD] = (u[:, sl] * gate).astype(bf16)

    return pl.pallas_call(
        body, name="gmlp", grid=(T // rb,),
        in_specs=[pl.BlockSpec((rb, PW), lambda i: (i, 0)), pl.BlockSpec((rb, SW), lambda i: (i, 0)),
                  pl.BlockSpec((H, CHUNK, CHUNK), lambda i: (0, 0, 0)), pl.BlockSpec((CHUNK, H), lambda i: (0, 0)),
                  pl.BlockSpec((1, GW), lambda i: (0, 0))],
        out_specs=pl.BlockSpec((rb, D), lambda i: (i, 0)),
        out_shape=jax.ShapeDtypeStruct((T, D), bf16),
        compiler_params=_params(("parallel",)))(p0, a_out, ws, bst, gv)


def _gmlp_bwd(p0, dmix, ws, bst, gv, SW, phase=None):
    T, PW = p0.shape
    GW = (PW - SW) // 2
    H = GW // GMLP_HEAD
    D = SW + GW

    kc = _chunks_per_step(T)
    rb = kc * CHUNK

    def body(p_ref, d_ref, ws_ref, b_ref, gv_ref, duv_ref, dws_ref, dbs_ref, dgv_ref):
        gv_ = gv_ref[...]
        tri = _tril()

        @pl.when(pl.program_id(0) == 0)
        def _():
            dws_ref[...] = jnp.zeros_like(dws_ref)
            dbs_ref[...] = jnp.zeros_like(dbs_ref)
            dgv_ref[...] = jnp.zeros_like(dgv_ref)
        wm = [jnp.where(tri, ws_ref[hh], 0.0).astype(bf16) for hh in range(H)]
        for q in range(kc):
            rows = slice(q * CHUNK, (q + 1) * CHUNK)
            uv, u, vh, r, vn = _gmlp_parts(p_ref[rows, :], gv_, SW, GW)
            dout = d_ref[rows, SW:].astype(f32)
            du, dvn = [], []
            for hh in range(H):
                sl = slice(hh * GMLP_HEAD, (hh + 1) * GMLP_HEAD)
                vnh = vn[:, sl].astype(bf16)
                gate = _dot(wm[hh], vnh) + b_ref[:, hh:hh + 1]
                dgate = dout[:, sl] * u[:, sl]
                du.append(dout[:, sl] * gate)
                dgb = dgate.astype(bf16)
                dws_ref[hh] += jnp.where(tri, _dg(dgb, vnh, NT), 0.0)
                dbs_ref[hh] += jnp.broadcast_to(jnp.sum(dgate, axis=1, keepdims=True), (CHUNK, CHUNK))
                dvn.append(_dg(wm[hh], dgb, TN))
            dvn = jnp.concatenate(dvn, axis=1)
            dv, dgv = _rms_bwd(dvn, vh, r, gv_)
            dgv_ref[...] += dgv
            dge = jnp.concatenate(du + [dv], axis=1)
            duv_ref[rows, :] = (dge * _gelu_grad(uv)).astype(bf16)

    fixed = pl.BlockSpec((H, CHUNK, CHUNK), lambda i: (0, 0, 0))
    return _call(
        body, "gmlp_bwd", (T // rb,),
        [pl.BlockSpec((rb, PW), lambda i: (i, 0)), pl.BlockSpec((rb, D), lambda i: (i, 0)), fixed,
         pl.BlockSpec((CHUNK, H), lambda i: (0, 0)), pl.BlockSpec((1, GW), lambda i: (0, 0))],
        [pl.BlockSpec((rb, 2 * GW), lambda i: (i, 0)), fixed, fixed, pl.BlockSpec((1, GW), lambda i: (0, 0))],
        [jax.ShapeDtypeStruct((T, 2 * GW), bf16), jax.ShapeDtypeStruct((H, CHUNK, CHUNK), f32),
         jax.ShapeDtypeStruct((H, CHUNK, CHUNK), f32), jax.ShapeDtypeStruct((1, GW), f32)],
        [p0, dmix, ws, bst, gv], sem=("arbitrary",), phase=phase)


def _s5_disc(lr, li, ldt):
    lr = jnp.minimum(lr, LAMBDA_RE_MAX)
    dt = jnp.exp(ldt)
    mag = jnp.exp(lr * dt)
    ar = mag * jnp.cos(li * dt)
    ai = mag * jnp.sin(li * dt)
    den = lr * lr + li * li
    nr = ar - 1.0
    zr = (nr * lr + ai * li) / den
    zi = (ai * lr - nr * li) / den
    return ar, ai, zr, zi


def _s5_prep(lr, li, ldt):
    G, P = lr.shape

    def body(lr_ref, li_ref, ldt_ref, ar_ref, ai_ref, zr_ref, zi_ref):
        ar, ai, zr, zi = _s5_disc(lr_ref[...], li_ref[...], ldt_ref[...])
        ar_ref[...] = ar
        ai_ref[...] = ai
        zr_ref[...] = zr
        zi_ref[...] = zi

    s = jax.ShapeDtypeStruct((G, P), f32)
    return pl.pallas_call(body, name="s5_prep", out_shape=[s, s, s, s])(lr, li, ldt)


def _s5_prep_bwd(lr, li, ldt, dar, dai, dzr, dzi):
    G, P = lr.shape

    def body(lr_ref, li_ref, ldt_ref, dar_ref, dai_ref, dzr_ref, dzi_ref, o1, o2, o3):
        _, vjp = jax.vjp(_s5_disc, lr_ref[...], li_ref[...], ldt_ref[...])
        cts = tuple(jnp.sum(r[...], axis=0) for r in (dar_ref, dai_ref, dzr_ref, dzi_ref))
        a, b, c = vjp(cts)
        o1[...] = a
        o2[...] = b
        o3[...] = c

    s = jax.ShapeDtypeStruct((G, P), f32)
    return pl.pallas_call(body, name="s5_prep_bwd", out_shape=[s, s, jax.ShapeDtypeStruct((G, 1), f32)])(
        lr, li, ldt, dar, dai, dzr, dzi)


def _s5_bbd(zr, zi, bre, bim):
    SW, NS = bre.shape

    def body(zr_ref, zi_ref, br_ref, bi_ref, o_ref):
        zr_, zi_, br, bi = zr_ref[...], zi_ref[...], br_ref[...], bi_ref[...]
        o_ref[:, :NS] = (zr_ * br - zi_ * bi).astype(bf16)
        o_ref[:, NS:] = (zr_ * bi + zi_ * br).astype(bf16)

    return pl.pallas_call(body, name="s5_bbd", out_shape=jax.ShapeDtypeStruct((SW, 2 * NS), bf16))(zr, zi, bre, bim)


def _s5_bbd_bwd(dbbd, zr, zi, bre, bim):
    SW, NS = bre.shape

    def body(d_ref, zr_ref, zi_ref, br_ref, bi_ref, dbr_ref, dbi_ref, dzr_ref, dzi_ref):
        zr_, zi_, br, bi = zr_ref[...], zi_ref[...], br_ref[...], bi_ref[...]
        dr, di = d_ref[:, :NS], d_ref[:, NS:]
        dbr_ref[...] = zr_ * dr + zi_ * di
        dbi_ref[...] = zr_ * di - zi_ * dr
        dzr_ref[...] = jnp.sum(dr * br + di * bi, axis=0, keepdims=True)
        dzi_ref[...] = jnp.sum(di * br - dr * bi, axis=0, keepdims=True)

    m = jax.ShapeDtypeStruct((SW, NS), f32)
    v = jax.ShapeDtypeStruct((1, NS), f32)
    return pl.pallas_call(body, name="s5_bbd_bwd", out_shape=[m, m, v, v])(dbbd, zr, zi, bre, bim)


def _slab_cat(ref, NB):
    return jnp.concatenate([ref[j] for j in range(NB)], axis=1)


def _s5_in(p3, bbd, SW, tm):
    B, S, PW = p3.shape
    NS = bbd.shape[1] // 2
    NB = NS // LANES

    def body(u_ref, b_ref, xr_ref, xi_ref):
        x = _dot(u_ref[...], b_ref[...])
        for j in range(NB):
            xr_ref[j] = x[:, j * LANES:(j + 1) * LANES]
            xi_ref[j] = x[:, NS + j * LANES:NS + (j + 1) * LANES]

    slab = jax.ShapeDtypeStruct((B, NB, S, LANES), f32)
    sspec = pl.BlockSpec((None, NB, tm, LANES), lambda b, i: (b, 0, i, 0))
    return pl.pallas_call(
        body, name="s5_in", grid=(B, S // tm),
        in_specs=[pl.BlockSpec((None, tm, SW), lambda b, i: (b, i, 0)), pl.BlockSpec((SW, 2 * NS), lambda b, i: (0, 0))],
        out_specs=[sspec, sspec], out_shape=[slab, slab],
        compiler_params=_params(("parallel", "parallel")))(p3, bbd)


def _s5_scan(name, xr, xi, ar, ai, reverse, hr=None, hi=None, phase=None):
    B, NB, S, _ = xr.shape
    L = S // NSUB
    nb = 2 if (hr is None and NB % 2 == 0) else 1
    with_da = hr is not None

    def body(*refs):
        if with_da:
            xr_ref, xi_ref, ar_ref, ai_ref, hr_ref, hi_ref, or_ref, oi_ref, dar_ref, dai_ref, pr_scr, pi_scr = refs
        else:
            xr_ref, xi_ref, ar_ref, ai_ref, or_ref, oi_ref, pr_scr, pi_scr = refs
        sign = -1.0 if reverse else 1.0
        a_r = [jnp.broadcast_to(ar_ref[j], (NSUB, LANES)) for j in range(nb)]
        a_i = [jnp.broadcast_to(ai_ref[j], (NSUB, LANES)) * sign for j in range(nb)]

        def step(t, carry):
            row = (L - 1 - t) if reverse else t
            rows = pl.ds(row, NSUB, stride=L)
            out = []
            for j in range(nb):
                sr, si, pr, pi = carry[j]
                nr = a_r[j] * sr - a_i[j] * si + xr_ref.at[j][rows, :]
                ni = a_r[j] * si + a_i[j] * sr + xi_ref.at[j][rows, :]
                or_ref.at[j][rows, :] = nr
                oi_ref.at[j][rows, :] = ni
                npr = a_r[j] * pr - a_i[j] * pi
                npi = a_r[j] * pi + a_i[j] * pr
                pr_scr[j, pl.ds(row, 1), :] = npr[0:1]
                pi_scr[j, pl.ds(row, 1), :] = npi[0:1]
                out.append((nr, ni, npr, npi))
            return tuple(out)
        z = jnp.zeros((NSUB, LANES), f32)
        one = jnp.ones((NSUB, LANES), f32)
        fin = lax.fori_loop(0, L, step, tuple((z, z, one, z) for _ in range(nb)))

        for j in range(nb):
            sr, si, plr, pli = fin[j]
            plr, pli = plr[0:1], pli[0:1]
            cr = jnp.zeros((1, LANES), f32)
            ci = jnp.zeros((1, LANES), f32)
            order = range(NSUB - 2, -1, -1) if reverse else range(1, NSUB)
            for c in order:
                src = c + 1 if reverse else c - 1
                cr, ci = (sr[src:src + 1] + plr * cr - pli * ci, si[src:src + 1] + plr * ci + pli * cr)
                rows = slice(c * L, (c + 1) * L)
                tr, ti = pr_scr[j], pi_scr[j]
                or_ref[j, rows, :] += tr * cr - ti * ci
                oi_ref[j, rows, :] += tr * ci + ti * cr
            if with_da:
                first = lax.broadcasted_iota(jnp.int32, (L, LANES), 0) == 0
                dar = jnp.zeros((1, LANES), f32)
                dai = jnp.zeros((1, LANES), f32)
                for c in range(NSUB):
                    rows = slice(c * L, (c + 1) * L)
                    if c == 0:
                        lr_, li_ = jnp.zeros((1, LANES), f32), jnp.zeros((1, LANES), f32)
                    else:
                        lr_, li_ = hr_ref[j, c * L - 1:c * L, :], hi_ref[j, c * L - 1:c * L, :]
                    hpr = jnp.where(first, lr_, pltpu.roll(hr_ref[j, rows, :], 1, 0))
                    hpi = jnp.where(first, li_, pltpu.roll(hi_ref[j, rows, :], 1, 0))
                    gr, gi = or_ref[j, rows, :], oi_ref[j, rows, :]
                    dar += jnp.sum(hpr * gr + hpi * gi, axis=0, keepdims=True)
                    dai += jnp.sum(hpr * gi - hpi * gr, axis=0, keepdims=True)
                dar_ref[j] = dar
                dai_ref[j] = dai

    slab = jax.ShapeDtypeStruct((B, NB, S, LANES), f32)
    sspec = pl.BlockSpec((None, nb, S, LANES), lambda b, j: (b, j, 0, 0))
    aspec = pl.BlockSpec((nb, 1, LANES), lambda b, j: (j, 0, 0))
    in_specs = [sspec, sspec, aspec, aspec]
    out_specs = [sspec, sspec]
    out_shape = [slab, slab]
    args = [xr, xi, ar, ai]
    if with_da:
        in_specs += [sspec, sspec]
        args += [hr, hi]
        dspec = pl.BlockSpec((None, nb, 1, LANES), lambda b, j: (b, j, 0, 0))
        out_specs += [dspec, dspec]
        out_shape += [jax.ShapeDtypeStruct((B, NB, 1, LANES), f32)] * 2
    return _call(body, name, (B, NB // nb), in_specs, out_specs, out_shape, args,
                 scratch=[pltpu.VMEM((nb, L, LANES), f32), pltpu.VMEM((nb, L, LANES), f32)],
                 sem=("parallel", "parallel"), phase=phase)


def _s5_out_parts(hr_ref, hi_ref, u_ref, cr_ref, ci_ref, d_ref, wg_ref, bg_ref, NB):
    hcr = _slab_cat(hr_ref, NB).astype(bf16)
    hci = _slab_cat(hi_ref, NB).astype(bf16)
    u = u_ref[...].astype(f32)
    y2 = _dot(hcr, cr_ref[...]) - _dot(hci, ci_ref[...]) + d_ref[...] * u
    yg = _gelu(y2)
    s = jax.nn.sigmoid(_dot(yg.astype(bf16), wg_ref[...]) + bg_ref[...])
    return hcr, hci, u, y2, yg, s


def _s5_out_specs(B, S, NB, NS, SW, tm):
    sspec = pl.BlockSpec((None, NB, tm, LANES), lambda b, i: (b, 0, i, 0))
    full = lambda r, c: pl.BlockSpec((r, c), lambda b, i: (0, 0))
    return sspec, [sspec, sspec, pl.BlockSpec((None, tm, SW), lambda b, i: (b, i, 0)), full(NS, SW), full(NS, SW),
                   full(1, SW), full(SW, SW), full(1, SW)]


def _s5_out(hr, hi, p3, cbr, cbi, dsk, wglu, bglu, tm):
    B, NB, S, _ = hr.shape
    NS, SW = cbr.shape

    def body(hr_ref, hi_ref, u_ref, cr_ref, ci_ref, d_ref, wg_ref, bg_ref, o_ref):
        _, _, _, _, yg, s = _s5_out_parts(hr_ref, hi_ref, u_ref, cr_ref, ci_ref, d_ref, wg_ref, bg_ref, NB)
        o_ref[...] = (yg * s).astype(bf16)

    _, in_specs = _s5_out_specs(B, S, NB, NS, SW, tm)
    return pl.pallas_call(
        body, name="s5_out", grid=(B, S // tm), in_specs=in_specs,
        out_specs=pl.BlockSpec((None, tm, SW), lambda b, i: (b, i, 0)),
        out_shape=jax.ShapeDtypeStruct((B, S, SW), bf16),
        compiler_params=_params(("parallel", "parallel")))(hr, hi, p3, cbr, cbi, dsk, wglu, bglu)


def _s5_out_bwd(hr, hi, p3, dmix3, cbr, cbi, dsk, wglu, bglu, tm):
    B, NB, S, _ = hr.shape
    NS, SW = cbr.shape

    def body(hr_ref, hi_ref, u_ref, cr_ref, ci_ref, d_ref, wg_ref, bg_ref, da_ref,
             dhr_ref, dhi_ref, du_ref, dcr_ref, dci_ref, dd_ref, dwg_ref, dbg_ref):
        hcr, hci, u, y2, yg, s = _s5_out_parts(hr_ref, hi_ref, u_ref, cr_ref, ci_ref, d_ref, wg_ref, bg_ref, NB)
        da = da_ref[...].astype(f32)
        dz = da * yg * s * (1.0 - s)
        dzb = dz.astype(bf16)
        dyg = da * s + _dg(dzb, wg_ref[...], NT)
        dy2 = dyg * _gelu_grad(y2)
        dyb = dy2.astype(bf16)

        @pl.when((pl.program_id(0) == 0) & (pl.program_id(1) == 0))
        def _():
            for r in (dcr_ref, dci_ref, dd_ref, dwg_ref, dbg_ref):
                r[...] = jnp.zeros_like(r)
        dwg_ref[...] += _dg(yg.astype(bf16), dzb, TN)
        dbg_ref[...] += jnp.sum(dz, axis=0, keepdims=True)
        dd_ref[...] += jnp.sum(dy2 * u, axis=0, keepdims=True)
        dcr_ref[...] += _dg(hcr, dyb, TN)
        dci_ref[...] -= _dg(hci, dyb, TN)
        du_ref[...] = dy2 * d_ref[...]
        dhr = _dg(dyb, cr_ref[...], NT)
        dhi = _dg(dyb, ci_ref[...], NT)
        for j in range(NB):
            dhr_ref[j] = dhr[:, j * LANES:(j + 1) * LANES]
            dhi_ref[j] = -dhi[:, j * LANES:(j + 1) * LANES]

    sspec, in_specs = _s5_out_specs(B, S, NB, NS, SW, tm)
    in_specs = in_specs + [pl.BlockSpec((None, tm, SW), lambda b, i: (b, i, 0))]
    full = lambda r, c: pl.BlockSpec((r, c), lambda b, i: (0, 0))
    slab = jax.ShapeDtypeStruct((B, NB, S, LANES), f32)
    mat = lambda r, c: jax.ShapeDtypeStruct((r, c), f32)
    return pl.pallas_call(
        body, name="s5_out_bwd", grid=(B, S // tm), in_specs=in_specs,
        out_specs=[sspec, sspec, pl.BlockSpec((None, tm, SW), lambda b, i: (b, i, 0)), full(NS, SW), full(NS, SW),
                   full(1, SW), full(SW, SW), full(1, SW)],
        out_shape=[slab, slab, jax.ShapeDtypeStruct((B, S, SW), f32), mat(NS, SW), mat(NS, SW), mat(1, SW),
                   mat(SW, SW), mat(1, SW)],
        compiler_params=_params(("arbitrary", "arbitrary")))(hr, hi, p3, cbr, cbi, dsk, wglu, bglu, dmix3)


def _s5_in_bwd(gr, gi, p3, bbd, du_skip, duv3, tm):
    B, NB, S, _ = gr.shape
    SW, NS2 = bbd.shape
    PW = SW + duv3.shape[2]

    def body(gr_ref, gi_ref, u_ref, b_ref, ds_ref, duv_ref, dp_ref, db_ref):
        g = jnp.concatenate([_slab_cat(gr_ref, NB), _slab_cat(gi_ref, NB)], axis=1).astype(bf16)
        du = _dg(g, b_ref[...], NT) + ds_ref[...]
        dp_ref[:, :SW] = du.astype(bf16)
        dp_ref[:, SW:] = duv_ref[...]

        @pl.when((pl.program_id(0) == 0) & (pl.program_id(1) == 0))
        def _():
            db_ref[...] = jnp.zeros_like(db_ref)
        db_ref[...] += _dg(u_ref[...], g, TN)

    sspec = pl.BlockSpec((None, NB, tm, LANES), lambda b, i: (b, 0, i, 0))
    row = lambda c: pl.BlockSpec((None, tm, c), lambda b, i: (b, i, 0))
    return pl.pallas_call(
        body, name="s5_in_bwd", grid=(B, S // tm),
        in_specs=[sspec, sspec, row(SW), pl.BlockSpec((SW, NS2), lambda b, i: (0, 0)), row(SW), row(PW - SW)],
        out_specs=[row(PW), pl.BlockSpec((SW, NS2), lambda b, i: (0, 0))],
        out_shape=[jax.ShapeDtypeStruct((B, S, PW), bf16), jax.ShapeDtypeStruct((SW, NS2), f32)],
        compiler_params=_params(("arbitrary", "arbitrary")))(gr, gi, p3, bbd, du_skip, duv3)


BIG = ['ev_w_in', 'ev_w_out', 'od_w_in', 'od_w_out', 'ffn_w_up', 'ffn_w_down']
ANY = pl.BlockSpec(memory_space=pl.ANY)


def _rtile(rows, mult):
    best = None
    for d in range(mult, min(rows, 512) + 1, mult):
        if rows % d == 0:
            best = d
    assert best is not None, (rows, mult)
    return best


def _pair_sum(name, g, recv, c_idx, out_dtype):
    NCH, R, W = g.shape
    HALF_W = W // 2
    tr = _rtile(R, 16)

    def body(c_ref, a_ref, b_ref, o_ref):
        o_ref[...] = (a_ref[...] + b_ref[...]).astype(out_dtype)

    return pl.pallas_call(
        body, name=name,
        grid_spec=pltpu.PrefetchScalarGridSpec(
            num_scalar_prefetch=1, grid=(NCH, R // tr),
            in_specs=[pl.BlockSpec((None, tr, HALF_W), lambda j, i, c: (j, i, c[0])),
                      pl.BlockSpec((None, tr, HALF_W), lambda j, i, c: (j, i, 0))],
            out_specs=pl.BlockSpec((None, tr, HALF_W), lambda j, i, c: (j, i, 0))),
        out_shape=jax.ShapeDtypeStruct((NCH, R, HALF_W), out_dtype),
        compiler_params=_params(("parallel", "parallel")))(c_idx, g, recv)


def _chip_sum(name, r3, h, k_idx):
    NCH, R, Wh = r3.shape
    tr = _rtile(R, 16)

    def body(k_ref, a_ref, own_ref, o_ref):
        own = own_ref[...].astype(f32)
        t = [jnp.where(k_ref[0] == s, own, a_ref[s].astype(f32)) for s in range(NCH)]
        o_ref[...] = ((t[0] + t[1]) + t[2]) + t[3]

    return pl.pallas_call(
        body, name=name,
        grid_spec=pltpu.PrefetchScalarGridSpec(
            num_scalar_prefetch=1, grid=(R // tr,),
            in_specs=[pl.BlockSpec((NCH, tr, Wh), lambda i, k: (0, i, 0)),
                      pl.BlockSpec((None, tr, Wh), lambda i, k: (k[0], i, 0))],
            out_specs=pl.BlockSpec((tr, Wh), lambda i, k: (i, 0))),
        out_shape=jax.ShapeDtypeStruct((R, Wh), f32),
        compiler_params=_params(("parallel",)))(k_idx, r3, h)


def _adam_math(gg, w, m, v):
    nm = ADAM_B1 * m + (1.0 - ADAM_B1) * gg
    nv = ADAM_B2 * v + (1.0 - ADAM_B2) * jnp.square(gg)
    m_hat = nm / (1.0 - ADAM_B1 ** ADAM_STEP)
    v_hat = nv / (1.0 - ADAM_B2 ** ADAM_STEP)
    return -ADAM_LR * (m_hat / (jnp.sqrt(v_hat) + ADAM_EPS) + ADAM_WD * w), nm, nv


def _adamw(name, mine, theirs, c_idx, w, m, v, lead, transposed, prev=None):
    L, R, W = w.shape
    if transposed:
        bw = LANES if W % LANES == 0 else W
        gspec = pl.BlockSpec((bw, R // 2), lambda i, hf, c: (i, 0))
        wspec = pl.BlockSpec((None, R // 2, bw), lambda i, hf, c: (lead, hf, i))
        grid = (W // bw, 2)
    else:
        tr = _rtile(R, SUBLANES)
        gspec = pl.BlockSpec((tr, W // 2), lambda i, hf, c: (i, 0))
        wspec = pl.BlockSpec((None, tr, W // 2), lambda i, hf, c: (lead, i, hf))
        grid = (R // tr, 2)

    def body(c_ref, a_ref, b_ref, w_ref, m_ref, v_ref, *rest):
        go_ref, d_ref, nm_ref, nv_ref = rest[-4:]
        gg = jnp.where(pl.program_id(1) == c_ref[0], a_ref[...], b_ref[...])
        if transposed:
            gg = gg.T
        d, nm, nv = _adam_math(gg, w_ref[...], m_ref[...], v_ref[...])
        go_ref[...] = gg
        d_ref[...] = d
        nm_ref[...] = nm
        nv_ref[...] = nv

    in_specs = [gspec, gspec, wspec, wspec, wspec]
    args, aliases = [c_idx, mine, theirs, w, m, v], {}
    if prev is not None:
        in_specs += [ANY] * 4
        args += list(prev)
        aliases = {6: 0, 7: 1, 8: 2, 9: 3}
    s = jax.ShapeDtypeStruct((L, R, W), f32)
    return pl.pallas_call(
        body, name=name,
        grid_spec=pltpu.PrefetchScalarGridSpec(num_scalar_prefetch=1, grid=grid, in_specs=in_specs,
                                               out_specs=[wspec] * 4),
        out_shape=[s, s, s, s], input_output_aliases=aliases,
        compiler_params=_params(("parallel", "arbitrary")))(*args)


def _adamw_small(gs, ws, ms, vs):
    n = len(gs)

    def body(*refs):
        for i in range(n):
            d, nm, nv = _adam_math(refs[i][...], refs[n + i][...], refs[2 * n + i][...], refs[3 * n + i][...])
            refs[4 * n + i][...] = d
            refs[5 * n + i][...] = nm
            refs[6 * n + i][...] = nv

    return pl.pallas_call(body, name="adamw_small",
                          out_shape=[jax.ShapeDtypeStruct(t.shape, f32) for t in ws] * 3)(*gs, *ws, *ms, *vs)


def _place():
    x, y, c = lax.axis_index("x"), lax.axis_index("y"), lax.axis_index("c")
    return x, y, c, [(1 - x, y), (x, 1 - y), (1 - x, 1 - y)]


def _gathered_shape(sh, kind):
    if kind == "rows":
        return sh[:-2] + (N_CHIPS * sh[-2], sh[-1])
    if kind == "cols":
        return sh[:-1] + (N_CHIPS * sh[-1],)
    return (N_CHIPS,) + sh


def _place_shard(name, shard, kind, k_idx):
    sh = shard.shape
    r, C = sh[-2], sh[-1]
    L = sh[0] if len(sh) == 3 else 1
    tr = _rtile(r, 16)
    nr = r // tr
    if kind == "rows":
        out3, omap = (L, N_CHIPS * r, C), lambda l, i, k: (l, k[0] * nr + i, 0)
    elif kind == "cols":
        out3, omap = (L, r, N_CHIPS * C), lambda l, i, k: (l, i, k[0])
    else:
        out3, omap = (N_CHIPS, r, C), lambda l, i, k: (k[0], i, 0)

    def body(k_ref, s_ref, o_ref):
        o_ref[...] = s_ref[...]

    out = pl.pallas_call(
        body, name=name,
        grid_spec=pltpu.PrefetchScalarGridSpec(
            num_scalar_prefetch=1, grid=(L, nr),
            in_specs=[pl.BlockSpec((None, tr, C), lambda l, i, k: (l, i, 0))],
            out_specs=pl.BlockSpec((None, tr, C), omap)),
        out_shape=jax.ShapeDtypeStruct(out3, shard.dtype),
        compiler_params=_params(("parallel", "parallel")))(k_idx, shard.reshape(L, r, C))
    return out.reshape(_gathered_shape(sh, kind))


def _gather_phase(shards, fulls, kinds):
    n = len(shards)
    shapes = [s.shape for s in shards]

    def window(ref, a, k, h=None):
        sh, kind = shapes[a], kinds[a]
        r = sh[-2]
        start, size = (0, r) if h is None else (h * (r // 2), r // 2)
        lead = (slice(None),) * (len(sh) - 2)
        if kind == "rows":
            return ref.at[lead + (pl.ds(k * r + start, size), slice(None))]
        if kind == "cols":
            return ref.at[lead + (pl.ds(start, size), pl.ds(pl.multiple_of(k * sh[-1], LANES), sh[-1]))]
        return ref.at[(k,) + lead + (pl.ds(start, size), slice(None))]

    def copies(s_refs, o_refs, sems):
        send_sems, recv_sems = sems
        x, y, c, chips = _place()
        k = 2 * x + y

        def copy(a, j, kk, hh, to, src=None):
            dst = window(o_refs[a], a, kk, hh)
            return pltpu.make_async_remote_copy(
                src_ref=dst if src is None else src, dst_ref=dst, send_sem=send_sems.at[6 * a + j],
                recv_sem=recv_sems.at[6 * a + j], device_id=to, device_id_type=MESH)

        first = []
        for a in range(n):
            r = shapes[a][-2]
            lead = (slice(None),) * (len(shapes[a]) - 2)
            src = s_refs[a].at[lead + (pl.ds(c * (r // 2), r // 2), slice(None))]
            first += [copy(a, j, k, c, (*chip, c), src=src) for j, chip in enumerate(chips)]
        return copy, first, (x, y, c), (x, y, 1 - c), c, chips

    def start(s_refs, o_refs, sems):
        for cp in copies(s_refs, o_refs, sems)[1]:
            cp.start()

    def finish(s_refs, o_refs, sems):
        copy, first, me, sibling, c, chips = copies(s_refs, o_refs, sems)
        passed = []
        for j, (cx, cy) in enumerate(chips):
            for a in range(n):
                copy(a, j, 2 * cx + cy, c, me).wait_recv()
                fwd = copy(a, 3 + j, 2 * cx + cy, c, sibling)
                fwd.start()
                passed.append(fwd)
        for j, (cx, cy) in enumerate(chips):
            for a in range(n):
                copy(a, 3 + j, 2 * cx + cy, 1 - c, me).wait_recv()
        for cp in first + passed:
            cp.wait_send()

    return _Phase(shards, fulls, [jax.ShapeDtypeStruct(f.shape, f.dtype) for f in fulls],
                  [pltpu.SemaphoreType.DMA((6 * n,)), pltpu.SemaphoreType.DMA((6 * n,))], start, finish)


def _swap_phase(gs):
    n = len(gs)

    def copies(g_refs, o_refs, sems):
        send_sems, recv_sems = sems
        x, y, c, _ = _place()
        half = [g.shape[2] // 2 for g in gs]
        return [pltpu.make_async_remote_copy(
            src_ref=g_refs[a].at[:, :, pl.ds(pl.multiple_of((1 - c) * half[a], LANES), half[a])], dst_ref=o_refs[a],
            send_sem=send_sems.at[a], recv_sem=recv_sems.at[a], device_id=(x, y, 1 - c), device_id_type=MESH)
            for a in range(n)]

    def start(g_refs, o_refs, sems):
        for cp in copies(g_refs, o_refs, sems):
            cp.start()

    def finish(g_refs, o_refs, sems):
        for cp in copies(g_refs, o_refs, sems):
            cp.wait()

    return _Phase(gs, [], [jax.ShapeDtypeStruct(g.shape[:2] + (g.shape[2] // 2,), g.dtype) for g in gs],
                  [pltpu.SemaphoreType.DMA((n,)), pltpu.SemaphoreType.DMA((n,))], start, finish)


def _exchange_phase(hs):
    n = len(hs)

    def copies(h_refs, o_refs, sems):
        send_sems, recv_sems = sems
        x, y, c, chips = _place()
        k = 2 * x + y

        def copy(a, j, src_slot, dst_slot):
            cx, cy = chips[j]
            return pltpu.make_async_remote_copy(
                src_ref=h_refs[a].at[src_slot], dst_ref=o_refs[a].at[dst_slot], send_sem=send_sems.at[3 * a + j],
                recv_sem=recv_sems.at[3 * a + j], device_id=(cx, cy, c), device_id_type=MESH)

        sends = [copy(a, j, 2 * cx + cy, k) for a in range(n) for j, (cx, cy) in enumerate(chips)]
        return copy, sends, k, chips

    def start(h_refs, o_refs, sems):
        for cp in copies(h_refs, o_refs, sems)[1]:
            cp.start()

    def finish(h_refs, o_refs, sems):
        copy, sends, k, chips = copies(h_refs, o_refs, sems)
        for a in range(n):
            for j, (cx, cy) in enumerate(chips):
                copy(a, j, k, 2 * cx + cy).wait_recv()
        for cp in sends:
            cp.wait_send()

    return _Phase(hs, [], [jax.ShapeDtypeStruct(h.shape, h.dtype) for h in hs],
                  [pltpu.SemaphoreType.DMA((3 * n,)), pltpu.SemaphoreType.DMA((3 * n,))], start, finish)


def _comm_pair_share(tag, gs):
    n = len(gs)

    def body(*refs):
        g_refs, o_refs, send_sems, recv_sems = refs[:n], refs[n:2 * n], refs[2 * n], refs[2 * n + 1]
        x, y, c, _ = _place()
        cps = [pltpu.make_async_remote_copy(
            src_ref=g_refs[a], dst_ref=o_refs[a], send_sem=send_sems.at[a], recv_sem=recv_sems.at[a],
            device_id=(x, y, 1 - c), device_id_type=MESH) for a in range(n)]
        for cp in cps:
            cp.start()
        for cp in cps:
            cp.wait()

    return pl.pallas_call(
        body, name="comm_pair_share_" + tag, in_specs=[ANY] * n, out_specs=[ANY] * n,
        out_shape=[jax.ShapeDtypeStruct(g.shape, g.dtype) for g in gs],
        scratch_shapes=[pltpu.SemaphoreType.DMA((n,)), pltpu.SemaphoreType.DMA((n,))])(*gs)


def _pad_rows(flat, unit):
    n = flat.shape[-1]
    pad = (-n) % unit
    if pad:
        flat = jnp.pad(flat, [(0, 0)] * (flat.ndim - 1) + [(0, pad)])
    return flat


def _split_chips(full, axis):
    sh = full.shape
    t = full.reshape(sh[:axis] + (N_CHIPS, sh[axis] // N_CHIPS) + sh[axis + 1:])
    return jnp.moveaxis(t, axis, 0).reshape(N_CHIPS, -1)


def _join_chips(stack, shard_shape, axis):
    t = jnp.moveaxis(stack.reshape((N_CHIPS,) + tuple(shard_shape)), 0, axis)
    sh = t.shape
    return t.reshape(sh[:axis] + (sh[axis] * sh[axis + 1],) + sh[axis + 2:])


def _block_diag(blocks):
    G, r, c = blocks.shape
    eye = jnp.eye(G, dtype=blocks.dtype)
    return (blocks[:, :, None, :] * eye[:, None, :, None]).reshape(G * r, G * c)


def _diag_blocks(m, G):
    r, c = m.shape[0] // G, m.shape[1] // G
    idx = jnp.arange(G)
    return m.reshape(G, r, G, c)[idx, :, idx, :]


def _weight_shards(w):
    conv = jnp.concatenate([w[n].reshape(-1) for n in GATHER_F32])
    conv = _pad_rows(conv, 2 * SUBLANES * LANES).reshape(-1, LANES)
    b16 = lambda a: a.astype(bf16)
    return {'ev_w_in': (b16(w['ev_w_in'][0]), "chip"), 'ev_w_out': (b16(w['ev_w_out'][0]), "rows"),
            's5_w_glu': (b16(w['s5_w_glu'][0]), "rows"), 'conv': (conv, "chip"),
            'od_w_in': (b16(w['od_w_in'][0]), "cols"), 'od_w_out': (b16(w['od_w_out'][0]), "rows"),
            'ffn_w_up0': (b16(w['ffn_w_up'][0]), "cols"), 'ffn_w_up1': (b16(w['ffn_w_up'][1]), "cols"),
            'ffn_w_down0': (b16(w['ffn_w_down'][0]), "rows"), 'ffn_w_down1': (b16(w['ffn_w_down'][1]), "rows")}


def kernel(x, mix_norm_g, ffn_norm_g, final_norm_g, ev_w_in, ev_w_out, s5_lam_re, s5_lam_im, s5_log_dt, s5_b_re, s5_b_im, s5_c_re, s5_c_im, s5_d, s5_w_glu, s5_b_glu, gm_w_s, gm_b_s, gm_v_g, od_w_in, od_conv_w, od_conv_b, od_w_out, ffn_w_up, ffn_conv_w, ffn_conv_b, ffn_w_down, loss_target, m_mix_norm_g, m_ffn_norm_g, m_final_norm_g, m_ev_w_in, m_ev_w_out, m_s5_lam_re, m_s5_lam_im, m_s5_log_dt, m_s5_b_re, m_s5_b_im, m_s5_c_re, m_s5_c_im, m_s5_d, m_s5_w_glu, m_s5_b_glu, m_gm_w_s, m_gm_b_s, m_gm_v_g, m_od_w_in, m_od_conv_w, m_od_conv_b, m_od_w_out, m_ffn_w_up, m_ffn_conv_w, m_ffn_conv_b, m_ffn_w_down, v_mix_norm_g, v_ffn_norm_g, v_final_norm_g, v_ev_w_in, v_ev_w_out, v_s5_lam_re, v_s5_lam_im, v_s5_log_dt, v_s5_b_re, v_s5_b_im, v_s5_c_re, v_s5_c_im, v_s5_d, v_s5_w_glu, v_s5_b_glu, v_gm_w_s, v_gm_b_s, v_gm_v_g, v_od_w_in, v_od_conv_w, v_od_conv_b, v_od_w_out, v_ffn_w_up, v_ffn_conv_w, v_ffn_conv_b, v_ffn_w_down):
    loc = dict(locals())
    w = {n: loc[n] for n in WEIGHTS}
    mom = {n: loc["m_" + n] for n in WEIGHTS}
    var = {n: loc["v_" + n] for n in WEIGHTS}

    B, S, D = x.shape
    T = B * S
    SW = s5_d.shape[1]
    G = SW // SSM_GROUP
    NS = G * SSM_STATE
    NB = NS // LANES
    tm = min(512, S)
    tt = min(1024, T)
    c_idx = lax.axis_index("c").astype(jnp.int32).reshape(1)
    k_idx = (2 * lax.axis_index("x") + lax.axis_index("y")).astype(jnp.int32).reshape(1)
    shards = _weight_shards(w)
    placed = {n: _place_shard("place_" + n, s, kd, k_idx) for n, (s, kd) in shards.items()}

    def gather(names):
        return _gather_phase([shards[n][0] for n in names], [placed[n] for n in names], [shards[n][1] for n in names])

    (w_ev_in,) = _run_phase("comm_gather_ev_in", gather(['ev_w_in']))
    w_ev_in = jnp.swapaxes(w_ev_in, 0, 1).reshape(D, -1)

    h0 = x.reshape(T, D)
    (y0, p0), (w_ev_out, w_glu, conv) = _norm_mm("ev_in", h0, mix_norm_g[0], w_ev_in, tm,
                                                 phase=gather(['ev_w_out', 's5_w_glu', 'conv']))
    full, off = {}, 0
    for n in GATHER_F32:
        full[n] = _join_chips(conv.reshape(N_CHIPS, -1)[:, off:off + w[n].size], w[n].shape, SHARD_AXIS[n])
        off += w[n].size
    PW = p0.shape[1]
    p03 = p0.reshape(B, S, PW)
    lr, li, ldt = s5_lam_re[0], s5_lam_im[0], s5_log_dt[0].reshape(G, 1)
    ar, ai, zr, zi = _s5_prep(lr, li, ldt)
    bre = _block_diag(jnp.swapaxes(s5_b_re[0], 1, 2))
    bim = _block_diag(jnp.swapaxes(s5_b_im[0], 1, 2))
    cbr = _block_diag(jnp.swapaxes(s5_c_re[0], 1, 2)).astype(bf16)
    cbi = _block_diag(jnp.swapaxes(s5_c_im[0], 1, 2)).astype(bf16)
    zr_row, zi_row = zr.reshape(1, NS), zi.reshape(1, NS)
    bbd = _s5_bbd(zr_row, zi_row, bre, bim)
    ar_s, ai_s = ar.reshape(NB, 1, LANES), ai.reshape(NB, 1, LANES)
    xr, xi = _s5_in(p03, bbd, SW, tm)
    (hr, hi), (w_up0, w_down0) = _s5_scan("s5_scan", xr, xi, ar_s, ai_s, False,
                                           phase=gather(['ffn_w_up0', 'ffn_w_down0']))
    dsk, bglu = s5_d.reshape(1, SW), s5_b_glu.reshape(1, SW)
    a_out = _s5_out(hr, hi, p03, cbr, cbi, dsk, w_glu, bglu, tm)
    ws, bst, gv = gm_w_s[0], gm_b_s[0].T, gm_v_g.reshape(1, -1)
    mixcat = _gmlp(p0, a_out.reshape(T, SW), ws, bst, gv, SW)
    h1 = _mm_resid("ev_out", mixcat, w_ev_out, h0, tm)

    def ffn_fwd(l, h, w_up, w_down, up_phase=None, down_phase=None):
        res = _norm_mm(f"ffn_up{l}", h, ffn_norm_g[l], w_up, tm, phase=up_phase)
        (z, up), got_up = res if up_phase is not None else (res, None)
        res = _ffn_down(f"ffn_down{l}", up, full['ffn_conv_w'][l], ffn_conv_b[l].reshape(1, -1), w_down, h, S, tm,
                        phase=down_phase)
        (hn, c), got_down = res if down_phase is not None else (res, None)
        return hn, (z, up.reshape(B, S, -1), c.reshape(B, S, -1)), got_up, got_down

    h2, ffn0, (w_up1, w_down1), (w_od_in, w_od_out) = ffn_fwd(
        0, h1, w_up0, w_down0, gather(['ffn_w_up1', 'ffn_w_down1']), gather(['od_w_in', 'od_w_out']))
    w_ups, w_downs = (w_up0, w_up1), (w_down0, w_down1)
    od_cw, od_cb = full['od_conv_w'][0], full['od_conv_b']
    y1, p1 = _norm_mm("od_in", h2, mix_norm_g[1], w_od_in, tm)
    p13 = p1.reshape(B, S, -1)
    sc = _od_act(p13, od_cw, od_cb)
    h3 = _mm_resid("od_out", sc.reshape(T, D), w_od_out, h2, tm)
    h4, ffn1, _, _ = ffn_fwd(1, h3, w_up1, w_down1)

    dh4, dh4b, loss_part, d_final_g = _final_loss(h4, final_norm_g, loss_target.reshape(T, D), tm)
    loss = lax.psum(loss_part[0, 0], ("x", "y", "c"))

    grads = {}

    halves = {}
    chips = lambda g: g.reshape(N_CHIPS, -1, D)

    def pair_sums(names, parts, recv):
        return [_pair_sum(f"pair_sum_{n}", g, r, c_idx, f32 if n == "small" else bf16)
                for n, g, r in zip(names, parts, recv)]

    def reduce_end(tag, names, hsum, r3):
        mine = [_chip_sum(f"chip_sum_{n}", r, h, k_idx) for n, r, h in zip(names, r3, hsum)]
        theirs = _comm_pair_share(tag, mine)
        halves.update({n: (a, b) for n, a, b in zip(names, mine, theirs)})

    def ffn_bwd(l, dh, dhb, h_in, saved, phase=None, swap=False):
        z, up3, c3 = saved
        w_down, w_up = w_downs[l], w_ups[l]
        da = _mm_nt(f"ffn_down_bwd{l}", dhb, w_down, tm)
        res = _ffn_act_bwd(f"ffn_act_bwd{l}", up3, c3, da.reshape(B, S, -1), full['ffn_conv_w'][l], phase=phase)
        (act, dg3, dv3, dcwg, dcwv, dcbg, dcbv), got = res if phase is not None else (res, None)
        g_down = _mm_tn(f"ffn_down_dw{l}", act.reshape(T, -1), dhb, tt)
        dupg, dupv = dg3.reshape(T, -1), dv3.reshape(T, -1)
        F = dupg.shape[1]
        g_up = _mm_tn(f"ffn_up_dw{l}_gate", dupg, z, tt, rows=2 * F)
        g_up = _mm_tn(f"ffn_up_dw{l}_val", dupv, z, tt, rows=2 * F, row_off=F, prev=g_up)
        parts = [chips(g_down), chips(g_up)]
        res = _mm_nt_normbwd(f"ffn_up_bwd{l}", [dupg, dupv], w_up, h_in, ffn_norm_g[l], dh, tm,
                             phase=_swap_phase(parts) if swap else None)
        (dh_new, dhb_new, dg), recv = res if swap else (res, None)
        F = dg3.shape[2]
        dcw = jnp.concatenate([dcwg[:, :F], dcwv[:, :F]], axis=1)
        dcb = jnp.concatenate([dcbg[:, :F], dcbv[:, :F]], axis=1)
        return dh_new, dhb_new, g_down, g_up, dcw, dcb[0], dg[0], got, parts, recv

    dh3, dh3b, gd1, gu1, gcw1, gcb1, gng1, _, _, _ = ffn_bwd(1, dh4, dh4b, h3, ffn1)
    dsc = _mm_nt("od_out_bwd", dh3b, w_od_out, tm)
    g_od_out = _mm_tn("od_out_dw", sc.reshape(T, D), dh3b, tt)
    dbg3, dcg3, dhx3, d_od_cw, d_od_cb = _od_act_bwd(p13, dsc.reshape(B, S, D), od_cw, od_cb)
    dp1 = [t.reshape(T, D) for t in (dbg3, dcg3, dhx3)]
    g_od_in = None
    for i, piece in enumerate(dp1):
        g_od_in = _mm_tn(f"od_in_dw{i}", piece, y1, tt, rows=3 * D, row_off=i * D, prev=g_od_in)
    grads['od_conv_w'] = d_od_cw[None]
    grads['od_conv_b'] = d_od_cb
    layer1 = ['ffn_w_down1', 'ffn_w_up1', 'od_w_out', 'od_w_in']
    parts1 = [chips(g) for g in (gd1, gu1, g_od_out, g_od_in)]
    (dh2, dh2b, gmix1), recv1 = _mm_nt_normbwd("od_in_bwd", dp1, w_od_in, h2, mix_norm_g[1], dh3, tm,
                                               phase=_swap_phase(parts1))
    hsum1 = pair_sums(layer1, parts1, recv1)
    dh1, dh1b, gd0, gu0, gcw0, gcb0, gng0, r3, parts0, recv0 = ffn_bwd(
        0, dh2, dh2b, h1, ffn0, phase=_exchange_phase(hsum1), swap=True)
    reduce_end("layer1", layer1, hsum1, r3)
    ffn0_names = ['ffn_w_down0', 'ffn_w_up0']
    hsum0 = pair_sums(ffn0_names, parts0, recv0)
    grads['ffn_conv_w'] = jnp.stack([gcw0, gcw1])
    grads['ffn_conv_b'] = jnp.stack([gcb0, gcb1])
    grads['ffn_norm_g'] = jnp.stack([gng0, gng1])
    grads['final_norm_g'] = d_final_g[0]

    dmix = _mm_nt("ev_out_bwd", dh1b, w_ev_out, tm)
    g_ev_out = _mm_tn("ev_out_dw", mixcat, dh1b, tt)
    part_evo = [chips(g_ev_out)]
    (duv, d_ws, d_bs, d_gv), recv_evo = _gmlp_bwd(p0, dmix, ws, bst, gv, SW, phase=_swap_phase(part_evo))
    hsum0 = hsum0 + pair_sums(['ev_w_out'], part_evo, recv_evo)
    ffn0_names = ffn0_names + ['ev_w_out']
    grads['gm_w_s'] = d_ws[None]
    grads['gm_b_s'] = d_bs[:, :, 0][None]
    grads['gm_v_g'] = d_gv
    dhr, dhi, du_skip, d_cbr, d_cbi, d_dsk, d_wglu, d_bglu = _s5_out_bwd(
        hr, hi, p03, dmix.reshape(B, S, D), cbr, cbi, dsk, w_glu, bglu, tm)
    grads['s5_c_re'] = jnp.swapaxes(_diag_blocks(d_cbr, G), 1, 2)[None]
    grads['s5_c_im'] = jnp.swapaxes(_diag_blocks(d_cbi, G), 1, 2)[None]
    grads['s5_d'] = d_dsk
    grads['s5_w_glu'] = d_wglu[None]
    grads['s5_b_glu'] = d_bglu
    (gr, gi, dar, dai), r3 = _s5_scan("s5_rscan", dhr, dhi, ar_s, ai_s, True, hr, hi, phase=_exchange_phase(hsum0))
    reduce_end("ffn0", ffn0_names, hsum0, r3)
    dp03, d_bbd = _s5_in_bwd(gr, gi, p03, bbd, du_skip, duv.reshape(B, S, -1), tm)
    d_bre, d_bim, d_zr, d_zi = _s5_bbd_bwd(d_bbd, zr_row, zi_row, bre, bim)
    grads['s5_b_re'] = jnp.swapaxes(_diag_blocks(d_bre, G), 1, 2)[None]
    grads['s5_b_im'] = jnp.swapaxes(_diag_blocks(d_bim, G), 1, 2)[None]
    shp = (-1, G, SSM_STATE)
    d_lr, d_li, d_ldt = _s5_prep_bwd(lr, li, ldt, dar.reshape(shp), dai.reshape(shp), d_zr.reshape(shp),
                                     d_zi.reshape(shp))
    grads['s5_lam_re'] = d_lr[None]
    grads['s5_lam_im'] = d_li[None]
    grads['s5_log_dt'] = d_ldt.reshape(1, G)
    dp0 = dp03.reshape(T, PW)
    g_ev_in = _mm_tn("ev_in_dw", dp0, y0, tt)
    grad_x, _, gmix0 = _mm_nt_normbwd("ev_in_bwd", [dp0], w_ev_in, h0, mix_norm_g[0], dh1, tm)
    grads['mix_norm_g'] = jnp.concatenate([gmix0, gmix1], axis=0)

    small = [n for n in WEIGHTS if n not in BIG]
    segs = []
    for n in small:
        gfull = grads[n].astype(f32)
        if n in SHARD_AXIS:
            segs.append(_split_chips(gfull, SHARD_AXIS[n]))
        else:
            segs.append(jnp.broadcast_to(gfull.reshape(1, -1), (N_CHIPS, gfull.size)))
    unit = 2 * SUBLANES * D
    gsmall = _pad_rows(jnp.concatenate(segs, axis=1), unit).reshape(N_CHIPS, -1, D)
    mixer0 = ['ev_w_in', 'small']
    parts = [chips(g_ev_in), gsmall]
    hsum = pair_sums(mixer0, parts, _run_phase("comm_pair_swap_mixer0", _swap_phase(parts)))
    reduce_end("mixer0", mixer0, hsum, _run_phase("comm_exchange_mixer0", _exchange_phase(hsum)))

    out_g, out_d, out_m, out_v = {}, {}, {}, {}

    def update(n, key, lead, transposed, prev=None):
        res = _adamw(f"adamw_{key}", *halves[key], c_idx, w[n], mom[n], var[n], lead, transposed, prev)
        out_g[n], out_d[n], out_m[n], out_v[n] = res
        return res

    update('ev_w_in', 'ev_w_in', 0, True)
    update('ev_w_out', 'ev_w_out', 0, False)
    update('od_w_in', 'od_w_in', 0, True)
    update('od_w_out', 'od_w_out', 0, False)
    update('ffn_w_up', 'ffn_w_up0', 0, True, prev=update('ffn_w_up', 'ffn_w_up1', 1, True))
    update('ffn_w_down', 'ffn_w_down0', 0, False, prev=update('ffn_w_down', 'ffn_w_down1', 1, False))

    mine, theirs = halves['small']
    first = lax.axis_index("c") == 0
    flat = jnp.concatenate([jnp.where(first, mine, theirs), jnp.where(first, theirs, mine)], axis=1).reshape(-1)
    off = 0
    for n in small:
        out_g[n] = flat[off:off + w[n].size].reshape(w[n].shape)
        off += w[n].size
    res = _adamw_small([out_g[n] for n in small], [w[n] for n in small], [mom[n] for n in small],
                       [var[n] for n in small])
    for i, n in enumerate(small):
        out_d[n], out_m[n], out_v[n] = res[i], res[len(small) + i], res[2 * len(small) + i]

    return (loss, grad_x.reshape(B, S, D), *[out_g[n] for n in WEIGHTS], *[out_d[n] for n in WEIGHTS],
            *[out_m[n] for n in WEIGHTS], *[out_v[n] for n in WEIGHTS])
```

```python
import functools
import math

import jax
import jax.numpy as jnp
from jax import lax
from jax.experimental import pallas as pl
from jax.experimental.pallas import tpu as pltpu

f32 = jnp.float32
bf16 = jnp.bfloat16
MESH = pl.DeviceIdType.MESH

SSM_GROUP = 16
SSM_STATE = 64
GMLP_HEAD = 128
CHUNK = 128
EPS = 1e-6
LAMBDA_RE_MAX = -1e-4
ADAM_LR, ADAM_B1, ADAM_B2, ADAM_EPS, ADAM_WD, ADAM_STEP = 0.001, 0.9, 0.999, 1e-08, 0.01, 10

LANES = 128
SUBLANES = 8
NSUB = 32
HALO = 16
VMEM_LIMIT = 56 * 1024 * 1024
N_CHIPS = 4

WEIGHTS = ['mix_norm_g', 'ffn_norm_g', 'final_norm_g', 'ev_w_in', 'ev_w_out', 's5_lam_re', 's5_lam_im', 's5_log_dt',
           's5_b_re', 's5_b_im', 's5_c_re', 's5_c_im', 's5_d', 's5_w_glu', 's5_b_glu', 'gm_w_s', 'gm_b_s', 'gm_v_g',
           'od_w_in', 'od_conv_w', 'od_conv_b', 'od_w_out', 'ffn_w_up', 'ffn_conv_w', 'ffn_conv_b', 'ffn_w_down']
SHARD_AXIS = {'ev_w_in': 2, 'ev_w_out': 1, 's5_w_glu': 1, 'od_w_in': 2, 'od_conv_w': 2, 'od_conv_b': 1, 'od_w_out': 1,
              'ffn_w_up': 2, 'ffn_conv_w': 2, 'ffn_w_down': 1}
GATHER_BF16 = ['ev_w_in', 'ev_w_out', 's5_w_glu', 'od_w_in', 'od_w_out', 'ffn_w_up', 'ffn_w_down']
GATHER_F32 = ['od_conv_w', 'od_conv_b', 'ffn_conv_w']

_GELU_K0 = math.sqrt(2.0 / math.pi)
_GELU_K1 = 0.044715
NT = (((1,), (1,)), ((), ()))
TN = (((0,), (0,)), ((), ()))


def _pick(n, cap):
    if n <= cap:
        return n
    best = None
    for d in range(LANES, cap + 1, LANES):
        if n % d == 0:
            best = d
    assert best is not None, (n, cap)
    return best


def _params(sem=None):
    return pltpu.CompilerParams(dimension_semantics=sem, vmem_limit_bytes=VMEM_LIMIT)


class _Phase:
    def __init__(self, ins, inplace, outs, sems, start, finish):
        self.ins, self.inplace, self.outs, self.sems = list(ins), list(inplace), list(outs), list(sems)
        self.start, self.finish = start, finish


def _call(body, name, grid, in_specs, out_specs, out_shape, args, scratch=(), sem=None, phase=None):
    if phase is None:
        return pl.pallas_call(body, name=name, grid=grid, in_specs=in_specs, out_specs=out_specs, out_shape=out_shape,
                              scratch_shapes=list(scratch), compiler_params=_params(sem))(*args)
    any_spec = pl.BlockSpec(memory_space=pl.ANY)
    n_in, n_out, n_scr = len(args), len(out_shape), len(scratch)
    p_in = phase.ins + phase.inplace
    ci, co = len(p_in), len(phase.outs)

    def wrapped(*refs):
        ins, cins = refs[:n_in], refs[n_in:n_in + len(phase.ins)]
        b = n_in + ci
        outs, couts = refs[b:b + n_out], refs[b + n_out:b + n_out + co]
        d = b + n_out + co
        scr, csem = refs[d:d + n_scr], refs[d + n_scr:]
        ids = [pl.program_id(i) for i in range(len(grid))]
        first = functools.reduce(jnp.logical_and, [i == 0 for i in ids])
        last = functools.reduce(jnp.logical_and, [i == g - 1 for i, g in zip(ids, grid)])

        @pl.when(first)
        def _():
            phase.start(cins, couts, csem)
        body(*ins, *outs, *scr)

        @pl.when(last)
        def _():
            phase.finish(cins, couts, csem)

    res = pl.pallas_call(
        wrapped, name=name, grid=grid, in_specs=list(in_specs) + [any_spec] * ci,
        out_specs=list(out_specs) + [any_spec] * co, out_shape=list(out_shape) + phase.outs,
        scratch_shapes=list(scratch) + phase.sems,
        input_output_aliases={n_in + len(phase.ins) + i: n_out + i for i in range(len(phase.inplace))},
        compiler_params=_params(tuple("arbitrary" for _ in grid)))(*args, *p_in)
    return res[:n_out], res[n_out:]


def _run_phase(name, phase):
    any_spec = pl.BlockSpec(memory_space=pl.ANY)
    ni, ci, co = len(phase.ins), len(phase.ins) + len(phase.inplace), len(phase.outs)

    def body(*refs):
        cins, couts, csem = refs[:ni], refs[ci:ci + co], refs[ci + co:]
        phase.start(cins, couts, csem)
        phase.finish(cins, couts, csem)

    return pl.pallas_call(
        body, name=name, in_specs=[any_spec] * ci, out_specs=[any_spec] * co, out_shape=phase.outs,
        scratch_shapes=phase.sems, input_output_aliases={ni + i: i for i in range(len(phase.inplace))})(
            *phase.ins, *phase.inplace)


def _gelu(x):
    return 0.5 * x * (1.0 + jnp.tanh(_GELU_K0 * (x + _GELU_K1 * x * x * x)))


def _gelu_grad(x):
    t = jnp.tanh(_GELU_K0 * (x + _GELU_K1 * x * x * x))
    return 0.5 * (1.0 + t) + 0.5 * x * (1.0 - t * t) * _GELU_K0 * (1.0 + 3.0 * _GELU_K1 * x * x)


def _rms_stats(x):
    r = lax.rsqrt(jnp.mean(x * x, axis=-1, keepdims=True) + EPS)
    return x * r, r


def _rms_bwd(dy, xh, r, g):
    dxh = dy * g
    dx = r * (dxh - xh * jnp.mean(dxh * xh, axis=-1, keepdims=True))
    return dx, jnp.sum(dy * xh, axis=0, keepdims=True)


def _dot(a, b):
    return jnp.dot(a, b, preferred_element_type=f32)


def _dg(a, b, dims):
    return lax.dot_general(a, b, dims, preferred_element_type=f32)


def _row_fold(z):
    return z.reshape(z.shape[0] // SUBLANES, SUBLANES, z.shape[1]).sum(axis=0)


def _norm_mm(name, h, g, w, tm, phase=None):
    T, D = h.shape
    N = w.shape[1]
    nc = _pick(N, 512)

    def body(h_ref, g_ref, w_ref, y_ref, o_ref):
        xh, _ = _rms_stats(h_ref[...])
        y = (xh * g_ref[...]).astype(bf16)
        y_ref[...] = y
        for j in range(N // nc):
            o_ref[:, j * nc:(j + 1) * nc] = _dot(y, w_ref[:, j * nc:(j + 1) * nc]).astype(bf16)

    return _call(
        body, name, (T // tm,),
        [pl.BlockSpec((tm, D), lambda i: (i, 0)), pl.BlockSpec((1, D), lambda i: (0, 0)),
         pl.BlockSpec((D, N), lambda i: (0, 0))],
        [pl.BlockSpec((tm, D), lambda i: (i, 0)), pl.BlockSpec((tm, N), lambda i: (i, 0))],
        [jax.ShapeDtypeStruct((T, D), bf16), jax.ShapeDtypeStruct((T, N), bf16)],
        [h, g.reshape(1, D), w], sem=("parallel",), phase=phase)


def _mm_resid(name, a, w, resid, tm):
    T, K = a.shape
    N = w.shape[1]

    def body(a_ref, w_ref, r_ref, o_ref):
        o_ref[...] = r_ref[...] + _dot(a_ref[...], w_ref[...])

    return pl.pallas_call(
        body, name=name, grid=(T // tm,),
        in_specs=[pl.BlockSpec((tm, K), lambda i: (i, 0)), pl.BlockSpec((K, N), lambda i: (0, 0)),
                  pl.BlockSpec((tm, N), lambda i: (i, 0))],
        out_specs=pl.BlockSpec((tm, N), lambda i: (i, 0)),
        out_shape=jax.ShapeDtypeStruct((T, N), f32),
        compiler_params=_params(("parallel",)))(a, w, resid)


def _mm_nt(name, dy, w, tm):
    T, N = dy.shape
    K = w.shape[0]
    kc = _pick(K, 512)

    def body(d_ref, w_ref, o_ref):
        d = d_ref[...].astype(bf16)
        for j in range(K // kc):
            o_ref[:, j * kc:(j + 1) * kc] = _dg(d, w_ref[j * kc:(j + 1) * kc, :], NT).astype(bf16)

    return pl.pallas_call(
        body, name=name, grid=(T // tm,),
        in_specs=[pl.BlockSpec((tm, N), lambda i: (i, 0)), pl.BlockSpec((K, N), lambda i: (0, 0))],
        out_specs=pl.BlockSpec((tm, K), lambda i: (i, 0)),
        out_shape=jax.ShapeDtypeStruct((T, K), bf16),
        compiler_params=_params(("parallel",)))(dy, w)


def _mm_nt_normbwd(name, dys, w, h, g, dh_in, tm, phase=None):
    n = len(dys)
    T = dys[0].shape[0]
    D = w.shape[0]
    widths = [d.shape[1] for d in dys]
    offs = [sum(widths[:i]) for i in range(n)]

    def body(*refs):
        d_refs = refs[:n]
        w_ref, h_ref, g_ref, dh_ref, o_ref, ob_ref, dg_ref = refs[n:]
        dz = _dg(d_refs[0][...], w_ref[:, :widths[0]], NT)
        for i in range(1, n):
            dz += _dg(d_refs[i][...], w_ref[:, offs[i]:offs[i] + widths[i]], NT)
        xh, r = _rms_stats(h_ref[...])
        dx, dg = _rms_bwd(dz, xh, r, g_ref[...])
        out = dh_ref[...] + dx
        o_ref[...] = out
        ob_ref[...] = out.astype(bf16)

        @pl.when(pl.program_id(0) == 0)
        def _():
            dg_ref[...] = jnp.zeros_like(dg_ref)
        dg_ref[...] += dg

    row = lambda c: pl.BlockSpec((tm, c), lambda i: (i, 0))
    return _call(
        body, name, (T // tm,),
        [row(c) for c in widths] + [pl.BlockSpec((D, sum(widths)), lambda i: (0, 0)), row(D),
                                    pl.BlockSpec((1, D), lambda i: (0, 0)), row(D)],
        [row(D), row(D), pl.BlockSpec((1, D), lambda i: (0, 0))],
        [jax.ShapeDtypeStruct((T, D), f32), jax.ShapeDtypeStruct((T, D), bf16), jax.ShapeDtypeStruct((1, D), f32)],
        [*dys, w, h, g.reshape(1, D), dh_in], sem=("arbitrary",), phase=phase)


def _mm_tn(name, a, b, tt, rows=None, row_off=0, prev=None):
    T, K = a.shape
    N = b.shape[1]
    rows = K if rows is None else rows
    tk = _pick(K, 1408)
    tn = _pick(N, 1024)
    assert row_off % tk == 0
    kb = row_off // tk

    def body(a_ref, b_ref, *rest):
        o_ref = rest[-1]

        @pl.when(pl.program_id(2) == 0)
        def _():
            o_ref[...] = jnp.zeros_like(o_ref)
        o_ref[...] += _dg(a_ref[...], b_ref[...], TN)

    in_specs = [pl.BlockSpec((tt, tk), lambda k, n, t: (t, k)), pl.BlockSpec((tt, tn), lambda k, n, t: (t, n))]
    args, aliases = [a, b], {}
    if prev is not None:
        in_specs.append(ANY)
        args.append(prev)
        aliases = {2: 0}
    return pl.pallas_call(
        body, name=name, grid=(K // tk, N // tn, T // tt), in_specs=in_specs,
        out_specs=pl.BlockSpec((tk, tn), lambda k, n, t: (k + kb, n)),
        out_shape=jax.ShapeDtypeStruct((rows, N), f32), input_output_aliases=aliases,
        compiler_params=_params(("parallel", "parallel", "arbitrary")))(*args)


def _final_loss(h, g, tgt, tm):
    T, D = h.shape

    def body(h_ref, g_ref, t_ref, dh_ref, dhb_ref, loss_ref, dg_ref):
        xh, r = _rms_stats(h_ref[...])
        gg = g_ref[...]
        diff = xh * gg - t_ref[...]
        dy = diff * (1.0 / D)
        dx, dg = _rms_bwd(dy, xh, r, gg)
        dh_ref[...] = dx
        dhb_ref[...] = dx.astype(bf16)

        @pl.when(pl.program_id(0) == 0)
        def _():
            dg_ref[...] = jnp.zeros_like(dg_ref)
            loss_ref[...] = jnp.zeros_like(loss_ref)
        dg_ref[...] += dg
        loss_ref[...] += (0.5 / D) * jnp.sum(jnp.sum(diff * diff, axis=1, keepdims=True), axis=0, keepdims=True)

    return pl.pallas_call(
        body, name="final_loss", grid=(T // tm,),
        in_specs=[pl.BlockSpec((tm, D), lambda i: (i, 0)), pl.BlockSpec((1, D), lambda i: (0, 0)),
                  pl.BlockSpec((tm, D), lambda i: (i, 0))],
        out_specs=[pl.BlockSpec((tm, D), lambda i: (i, 0)), pl.BlockSpec((tm, D), lambda i: (i, 0)),
                   pl.BlockSpec((1, 1), lambda i: (0, 0)), pl.BlockSpec((1, D), lambda i: (0, 0))],
        out_shape=[jax.ShapeDtypeStruct((T, D), f32), jax.ShapeDtypeStruct((T, D), bf16),
                   jax.ShapeDtypeStruct((1, 1), f32), jax.ShapeDtypeStruct((1, D), f32)],
        compiler_params=_params(("arbitrary",)))(h, g.reshape(1, D), tgt)


def _taps(load, r0, R):
    main = load(r0, R)
    hs = pl.multiple_of(jnp.maximum(r0 - HALO, 0), HALO)
    halo = load(hs, HALO) * (r0 > 0).astype(f32)
    ext = jnp.concatenate([halo, main], axis=0)
    xm1 = pltpu.roll(ext, 1, 0)[HALO:]
    xm2 = pltpu.roll(ext, 2, 0)[HALO:]
    return xm2, xm1, main


def _conv(w, b, taps):
    return b + w[0:1] * taps[0] + w[1:2] * taps[1] + w[2:3] * taps[2]


def _ref_load(ref):
    return lambda s, n: ref[pl.ds(s, n), :].astype(f32)


def _ffn_down(name, up, cw, cb, w_down, resid, S, tm, phase=None):
    T, F2 = up.shape
    F = F2 // 2
    D = w_down.shape[1]
    cwid = _pick(F, 256)
    per_seq = S // tm

    def body(u_ref, halo_ref, cw_ref, cb_ref, w_ref, r_ref, o_ref, c_ref):
        keep = (pl.program_id(0) % per_seq > 0).astype(f32)

        def conv(off):
            cols = slice(off, off + cwid)
            main = u_ref[:, cols].astype(f32)
            ext = jnp.concatenate([halo_ref[:, cols].astype(f32) * keep, main], axis=0)
            taps = (pltpu.roll(ext, 2, 0)[HALO:], pltpu.roll(ext, 1, 0)[HALO:], main)
            return _conv(cw_ref[:, cols], cb_ref[:, cols], taps)

        acc = r_ref[...]
        for j in range(F // cwid):
            cg, cv = conv(j * cwid), conv(F + j * cwid)
            c_ref[:, j * cwid:(j + 1) * cwid] = cg.astype(bf16)
            c_ref[:, F + j * cwid:F + (j + 1) * cwid] = cv.astype(bf16)
            a = (cg * jax.nn.sigmoid(cg) * cv).astype(bf16)
            acc = acc + _dot(a, w_ref[j * cwid:(j + 1) * cwid, :])
        o_ref[...] = acc

    full = lambda r, c: pl.BlockSpec((r, c), lambda i: (0, 0))
    return _call(
        body, name, (T // tm,),
        [pl.BlockSpec((tm, F2), lambda i: (i, 0)),
         pl.BlockSpec((HALO, F2), lambda i: (jnp.maximum(i * (tm // HALO) - 1, 0), 0)),
         full(3, F2), full(1, F2), full(F, D), pl.BlockSpec((tm, D), lambda i: (i, 0))],
        [pl.BlockSpec((tm, D), lambda i: (i, 0)), pl.BlockSpec((tm, F2), lambda i: (i, 0))],
        [jax.ShapeDtypeStruct((T, D), f32), jax.ShapeDtypeStruct((T, F2), bf16)],
        [up, up, cw, cb, w_down, resid], sem=("parallel",), phase=phase)


def _rev_conv_rows(d, nxt, w):
    R = d.shape[0]
    ext = jnp.concatenate([d, nxt], axis=0)
    n = R + HALO
    xp1 = pltpu.roll(ext, n - 1, 0)[:R]
    xp2 = pltpu.roll(ext, n - 2, 0)[:R]
    return w[2:3] * d + w[1:2] * xp1 + w[0:1] * xp2, xp1, xp2


def _conv_grad_acc(acc, dc, taps):
    return (acc[0] + _row_fold(dc * taps[0]), acc[1] + _row_fold(dc * taps[1]), acc[2] + _row_fold(dc * taps[2]),
            acc[3] + _row_fold(dc))


def _conv_grad_out(dcw_ref, dcb_ref, acc):
    @pl.when(pl.program_id(1) == 0)
    def _():
        dcw_ref[...] = jnp.zeros_like(dcw_ref)
        dcb_ref[...] = jnp.zeros_like(dcb_ref)
    for k in range(3):
        dcw_ref[k:k + 1, :] += jnp.sum(acc[k], axis=0, keepdims=True)
    dcb_ref[...] += jnp.sum(acc[3], axis=0, keepdims=True)


def _ffn_act_bwd(name, up3, c3, da3, cw, phase=None):
    B, S, F2 = up3.shape
    F = F2 // 2
    cwid = _pick(F, 256)
    nF = F // cwid
    R = min(256, S)
    nR = S // R

    def body(xg_ref, xv_ref, cg_ref, cv_ref, da_ref, wg_ref, wv_ref,
             act_ref, dg_ref, dv_ref, dcwg_ref, dcwv_ref, dcbg_ref, dcbv_ref, sum_scr):
        wg, wv = wg_ref[...], wv_ref[...]

        def half(d, nxt, w, x_ref, rows, acc, out_ref):
            out, xp1, xp2 = _rev_conv_rows(d, nxt, w)
            out_ref[rows, :] = out.astype(bf16)
            x = x_ref[rows, :].astype(f32)
            return (acc[0] + _row_fold(xp2 * x), acc[1] + _row_fold(xp1 * x), acc[2] + _row_fold(d * x),
                    acc[3] + _row_fold(d))

        def step(i, carry):
            ng, nv, accg, accv = carry
            rows = pl.ds(pl.multiple_of((nR - 1 - i) * R, R), R)
            cg, cv = cg_ref[rows, :].astype(f32), cv_ref[rows, :].astype(f32)
            da = da_ref[rows, :].astype(f32)
            sg = jax.nn.sigmoid(cg)
            act_ref[rows, :] = (cg * sg * cv).astype(bf16)
            dgate = da * cv * (sg * (1.0 + cg * (1.0 - sg)))
            dval = da * (cg * sg)
            accg = half(dgate, ng, wg, xg_ref, rows, accg, dg_ref)
            accv = half(dval, nv, wv, xv_ref, rows, accv, dv_ref)
            return dgate[:HALO], dval[:HALO], accg, accv
        z = jnp.zeros((SUBLANES, cwid), f32)
        zh = jnp.zeros((HALO, cwid), f32)
        _, _, accg, accv = lax.fori_loop(0, nR, step, (zh, zh, (z, z, z, z), (z, z, z, z)))
        j = pl.program_id(1)
        for half_i, (acc, dcw_ref, dcb_ref) in enumerate(((accg, dcwg_ref, dcbg_ref), (accv, dcwv_ref, dcbv_ref))):
            @pl.when(pl.program_id(0) == 0)
            def _():
                sum_scr[half_i, j] = jnp.zeros((SUBLANES, cwid), f32)
            for k in range(4):
                sum_scr[half_i, j, k:k + 1, :] += jnp.sum(acc[k], axis=0, keepdims=True)
            dcw_ref[...] = sum_scr[half_i, j, 0:3, :]
            dcb_ref[...] = sum_scr[half_i, j, 3:4, :]

    blk = lambda off: pl.BlockSpec((None, S, cwid), lambda b, j: (b, 0, off + j))
    wblk = lambda off: pl.BlockSpec((3, cwid), lambda b, j: (0, off + j))
    sums = lambda r: pl.BlockSpec((r, cwid), lambda b, j: (0, jnp.where(b == B - 1, j, nF)))
    half_shape = jax.ShapeDtypeStruct((B, S, F), bf16)
    return _call(
        body, name, (B, nF),
        [blk(0), blk(nF), blk(0), blk(nF), blk(0), wblk(0), wblk(nF)],
        [blk(0), blk(0), blk(0), sums(3), sums(3), sums(1), sums(1)],
        [half_shape, half_shape, half_shape, jax.ShapeDtypeStruct((3, F + cwid), f32),
         jax.ShapeDtypeStruct((3, F + cwid), f32), jax.ShapeDtypeStruct((1, F + cwid), f32),
         jax.ShapeDtypeStruct((1, F + cwid), f32)],
        [up3, up3, c3, c3, da3, cw, cw], scratch=[pltpu.VMEM((2, nF, SUBLANES, cwid), f32)],
        sem=("arbitrary", "arbitrary"), phase=phase)


def _od_act(p3, cw, cb):
    B, S, D3 = p3.shape
    D = D3 // 3
    cwid = _pick(D, 256)
    nD = D // cwid
    R = min(256, S)

    def body(bg_ref, cg_ref, hx_ref, w_ref, b_ref, o_ref):
        w, b = w_ref[...], b_ref[...]
        q = lambda s, n: cg_ref[pl.ds(s, n), :].astype(f32) * hx_ref[pl.ds(s, n), :].astype(f32)

        def chunk(r, c):
            r0 = pl.multiple_of(r * R, R)
            cq = _conv(w, b, _taps(q, r0, R))
            o_ref[pl.ds(r0, R), :] = (bg_ref[pl.ds(r0, R), :].astype(f32) * cq).astype(bf16)
            return c
        lax.fori_loop(0, S // R, chunk, 0)

    blk = lambda off: pl.BlockSpec((None, S, cwid), lambda b, j: (b, 0, off + j))
    return pl.pallas_call(
        body, name="od_act", grid=(B, nD),
        in_specs=[blk(0), blk(nD), blk(2 * nD), pl.BlockSpec((3, cwid), lambda b, j: (0, j)),
                  pl.BlockSpec((1, cwid), lambda b, j: (0, j))],
        out_specs=pl.BlockSpec((None, S, cwid), lambda b, j: (b, 0, j)),
        out_shape=jax.ShapeDtypeStruct((B, S, D), bf16),
        compiler_params=_params(("parallel", "parallel")))(p3, p3, p3, cw, cb)


def _od_act_bwd(p3, dsc3, cw, cb):
    B, S, D3 = p3.shape
    D = D3 // 3
    cwid = _pick(D, 256)
    nD = D // cwid
    R = min(256, S)
    nR = S // R

    def body(bg_ref, cg_ref, hx_ref, d_ref, w_ref, b_ref, dbg_ref, dcg_ref, dhx_ref, dcw_ref, dcb_ref):
        w, b = w_ref[...], b_ref[...]
        q = lambda s, n: cg_ref[pl.ds(s, n), :].astype(f32) * hx_ref[pl.ds(s, n), :].astype(f32)

        def step(i, carry):
            nxt, acc = carry
            r0 = pl.multiple_of((nR - 1 - i) * R, R)
            rows = pl.ds(r0, R)
            tq = _taps(q, r0, R)
            cq = _conv(w, b, tq)
            d = d_ref[rows, :].astype(f32)
            dbg_ref[rows, :] = (d * cq).astype(bf16)
            dcq = d * bg_ref[rows, :].astype(f32)
            dq, _, _ = _rev_conv_rows(dcq, nxt, w)
            dcg_ref[rows, :] = (dq * hx_ref[rows, :].astype(f32)).astype(bf16)
            dhx_ref[rows, :] = (dq * cg_ref[rows, :].astype(f32)).astype(bf16)
            return dcq[:HALO], _conv_grad_acc(acc, dcq, tq)
        z = jnp.zeros((SUBLANES, cwid), f32)
        _, acc = lax.fori_loop(0, nR, step, (jnp.zeros((HALO, cwid), f32), (z, z, z, z)))
        _conv_grad_out(dcw_ref, dcb_ref, acc)

    blk = lambda off: pl.BlockSpec((None, S, cwid), lambda j, b: (b, 0, off + j))
    part = jax.ShapeDtypeStruct((B, S, D), bf16)
    return pl.pallas_call(
        body, name="od_act_bwd", grid=(nD, B),
        in_specs=[blk(0), blk(nD), blk(2 * nD), blk(0), pl.BlockSpec((3, cwid), lambda j, b: (0, j)),
                  pl.BlockSpec((1, cwid), lambda j, b: (0, j))],
        out_specs=[blk(0), blk(0), blk(0), pl.BlockSpec((3, cwid), lambda j, b: (0, j)),
                   pl.BlockSpec((1, cwid), lambda j, b: (0, j))],
        out_shape=[part, part, part, jax.ShapeDtypeStruct((3, D), f32), jax.ShapeDtypeStruct((1, D), f32)],
        compiler_params=_params(("parallel", "arbitrary")))(p3, p3, p3, dsc3, cw, cb)


def _gmlp_parts(p, gv, SW, GW):
    uv = p[:, SW:].astype(f32)
    ge = _gelu(uv)
    u, v = ge[:, :GW], ge[:, GW:]
    vh, r = _rms_stats(v)
    return uv, u, vh, r, vh * gv


def _tril():
    rows = lax.broadcasted_iota(jnp.int32, (CHUNK, CHUNK), 0)
    cols = lax.broadcasted_iota(jnp.int32, (CHUNK, CHUNK), 1)
    return rows >= cols


def _chunks_per_step(T):
    return 4 if T % (4 * CHUNK) == 0 else 1


def _gmlp(p0, a_out, ws, bst, gv, SW):
    T, PW = p0.shape
    GW = (PW - SW) // 2
    H = GW // GMLP_HEAD
    D = SW + GW

    kc = _chunks_per_step(T)
    rb = kc * CHUNK

    def body(p_ref, a_ref, ws_ref, b_ref, gv_ref, o_ref):
        tri = _tril()
        o_ref[:, :SW] = a_ref[...]
        wm = [jnp.where(tri, ws_ref[hh], 0.0).astype(bf16) for hh in range(H)]
        for q in range(kc):
            rows = slice(q * CHUNK, (q + 1) * CHUNK)
            _, u, _, _, vn = _gmlp_parts(p_ref[rows, :], gv_ref[...], SW, GW)
            for hh in range(H):
                sl = slice(hh * GMLP_HEAD, (hh + 1) * GMLP_HEAD)
                gate = _dot(wm[hh], vn[:, sl].astype(bf16)) + b_ref[:, hh:hh + 1]
                o_ref[rows, SW + hh * GMLP_HEAD:SW + (hh + 1) * GMLP_HEAD] = (u[:, sl] * gate).astype(bf16)

    return pl.pallas_call(
        body, name="gmlp", grid=(T // rb,),
        in_specs=[pl.BlockSpec((rb, PW), lambda i: (i, 0)), pl.BlockSpec((rb, SW), lambda i: (i, 0)),
                  pl.BlockSpec((H, CHUNK, CHUNK), lambda i: (0, 0, 0)), pl.BlockSpec((CHUNK, H), lambda i: (0, 0)),
                  pl.BlockSpec((1, GW), lambda i: (0, 0))],
        out_specs=pl.BlockSpec((rb, D), lambda i: (i, 0)),
        out_shape=jax.ShapeDtypeStruct((T, D), bf16),
        compiler_params=_params(("parallel",)))(p0, a_out, ws, bst, gv)


def _gmlp_bwd(p0, dmix, ws, bst, gv, SW, phase=None):
    T, PW = p0.shape
    GW = (PW - SW) // 2
    H = GW // GMLP_HEAD
    D = SW + GW

    kc = _chunks_per_step(T)
    rb = kc * CHUNK

    def body(p_ref, d_ref, ws_ref, b_ref, gv_ref, duv_ref, dws_ref, dbs_ref, dgv_ref):
        gv_ = gv_ref[...]
        tri = _tril()

        @pl.when(pl.program_id(0) == 0)
        def _():
            dws_ref[...] = jnp.zeros_like(dws_ref)
            dbs_ref[...] = jnp.zeros_like(dbs_ref)
            dgv_ref[...] = jnp.zeros_like(dgv_ref)
        wm = [jnp.where(tri, ws_ref[hh], 0.0).astype(bf16) for hh in range(H)]
        for q in range(kc):
            rows = slice(q * CHUNK, (q + 1) * CHUNK)
            uv, u, vh, r, vn = _gmlp_parts(p_ref[rows, :], gv_, SW, GW)
            dout = d_ref[rows, SW:].astype(f32)
            du, dvn = [], []
            for hh in range(H):
                sl = slice(hh * GMLP_HEAD, (hh + 1) * GMLP_HEAD)
                vnh = vn[:, sl].astype(bf16)
                gate = _dot(wm[hh], vnh) + b_ref[:, hh:hh + 1]
                dgate = dout[:, sl] * u[:, sl]
                du.append(dout[:, sl] * gate)
                dgb = dgate.astype(bf16)
                dws_ref[hh] += jnp.where(tri, _dg(dgb, vnh, NT), 0.0)
                dbs_ref[hh] += jnp.broadcast_to(jnp.sum(dgate, axis=1, keepdims=True), (CHUNK, CHUNK))
                dvn.append(_dg(wm[hh], dgb, TN))
            dvn = jnp.concatenate(dvn, axis=1)
            dv, dgv = _rms_bwd(dvn, vh, r, gv_)
            dgv_ref[...] += dgv
            dge = jnp.concatenate(du + [dv], axis=1)
            duv_ref[rows, :] = (dge * _gelu_grad(uv)).astype(bf16)

    fixed = pl.BlockSpec((H, CHUNK, CHUNK), lambda i: (0, 0, 0))
    return _call(
        body, "gmlp_bwd", (T // rb,),
        [pl.BlockSpec((rb, PW), lambda i: (i, 0)), pl.BlockSpec((rb, D), lambda i: (i, 0)), fixed,
         pl.BlockSpec((CHUNK, H), lambda i: (0, 0)), pl.BlockSpec((1, GW), lambda i: (0, 0))],
        [pl.BlockSpec((rb, 2 * GW), lambda i: (i, 0)), fixed, fixed, pl.BlockSpec((1, GW), lambda i: (0, 0))],
        [jax.ShapeDtypeStruct((T, 2 * GW), bf16), jax.ShapeDtypeStruct((H, CHUNK, CHUNK), f32),
         jax.ShapeDtypeStruct((H, CHUNK, CHUNK), f32), jax.ShapeDtypeStruct((1, GW), f32)],
        [p0, dmix, ws, bst, gv], sem=("arbitrary",), phase=phase)


def _s5_disc(lr, li, ldt):
    lr = jnp.minimum(lr, LAMBDA_RE_MAX)
    dt = jnp.exp(ldt)
    mag = jnp.exp(lr * dt)
    ar = mag * jnp.cos(li * dt)
    ai = mag * jnp.sin(li * dt)
    den = lr * lr + li * li
    nr = ar - 1.0
    zr = (nr * lr + ai * li) / den
    zi = (ai * lr - nr * li) / den
    return ar, ai, zr, zi


def _s5_prep(lr, li, ldt):
    G, P = lr.shape

    def body(lr_ref, li_ref, ldt_ref, ar_ref, ai_ref, zr_ref, zi_ref):
        ar, ai, zr, zi = _s5_disc(lr_ref[...], li_ref[...], ldt_ref[...])
        ar_ref[...] = ar
        ai_ref[...] = ai
        zr_ref[...] = zr
        zi_ref[...] = zi

    s = jax.ShapeDtypeStruct((G, P), f32)
    return pl.pallas_call(body, name="s5_prep", out_shape=[s, s, s, s])(lr, li, ldt)


def _s5_prep_bwd(lr, li, ldt, dar, dai, dzr, dzi):
    G, P = lr.shape

    def body(lr_ref, li_ref, ldt_ref, dar_ref, dai_ref, dzr_ref, dzi_ref, o1, o2, o3):
        _, vjp = jax.vjp(_s5_disc, lr_ref[...], li_ref[...], ldt_ref[...])
        cts = tuple(jnp.sum(r[...], axis=0) for r in (dar_ref, dai_ref, dzr_ref, dzi_ref))
        a, b, c = vjp(cts)
        o1[...] = a
        o2[...] = b
        o3[...] = c

    s = jax.ShapeDtypeStruct((G, P), f32)
    return pl.pallas_call(body, name="s5_prep_bwd", out_shape=[s, s, jax.ShapeDtypeStruct((G, 1), f32)])(
        lr, li, ldt, dar, dai, dzr, dzi)


def _s5_bbd(zr, zi, bre, bim):
    SW, NS = bre.shape

    def body(zr_ref, zi_ref, br_ref, bi_ref, o_ref):
        zr_, zi_, br, bi = zr_ref[...], zi_ref[...], br_ref[...], bi_ref[...]
        o_ref[:, :NS] = (zr_ * br - zi_ * bi).astype(bf16)
        o_ref[:, NS:] = (zr_ * bi + zi_ * br).astype(bf16)

    return pl.pallas_call(body, name="s5_bbd", out_shape=jax.ShapeDtypeStruct((SW, 2 * NS), bf16))(zr, zi, bre, bim)


def _s5_bbd_bwd(dbbd, zr, zi, bre, bim):
    SW, NS = bre.shape

    def body(d_ref, zr_ref, zi_ref, br_ref, bi_ref, dbr_ref, dbi_ref, dzr_ref, dzi_ref):
        zr_, zi_, br, bi = zr_ref[...], zi_ref[...], br_ref[...], bi_ref[...]
        dr, di = d_ref[:, :NS], d_ref[:, NS:]
        dbr_ref[...] = zr_ * dr + zi_ * di
        dbi_ref[...] = zr_ * di - zi_ * dr
        dzr_ref[...] = jnp.sum(dr * br + di * bi, axis=0, keepdims=True)
        dzi_ref[...] = jnp.sum(di * br - dr * bi, axis=0, keepdims=True)

    m = jax.ShapeDtypeStruct((SW, NS), f32)
    v = jax.ShapeDtypeStruct((1, NS), f32)
    return pl.pallas_call(body, name="s5_bbd_bwd", out_shape=[m, m, v, v])(dbbd, zr, zi, bre, bim)


def _slab_cat(ref, NB):
    return jnp.concatenate([ref[j] for j in range(NB)], axis=1)


def _s5_in(p3, bbd, SW, tm):
    B, S, PW = p3.shape
    NS = bbd.shape[1] // 2
    NB = NS // LANES

    def body(u_ref, b_ref, xr_ref, xi_ref):
        x = _dot(u_ref[...], b_ref[...])
        for j in range(NB):
            xr_ref[j] = x[:, j * LANES:(j + 1) * LANES].astype(bf16)
            xi_ref[j] = x[:, NS + j * LANES:NS + (j + 1) * LANES].astype(bf16)

    slab = jax.ShapeDtypeStruct((B, NB, S, LANES), bf16)
    sspec = pl.BlockSpec((None, NB, tm, LANES), lambda b, i: (b, 0, i, 0))
    return pl.pallas_call(
        body, name="s5_in", grid=(B, S // tm),
        in_specs=[pl.BlockSpec((None, tm, SW), lambda b, i: (b, i, 0)), pl.BlockSpec((SW, 2 * NS), lambda b, i: (0, 0))],
        out_specs=[sspec, sspec], out_shape=[slab, slab],
        compiler_params=_params(("parallel", "parallel")))(p3, bbd)


def _s5_scan(name, xr, xi, ar, ai, reverse, hr=None, hi=None, phase=None):
    B, NB, S, _ = xr.shape
    L = S // NSUB
    nb = 2 if NB % 2 == 0 else 1
    with_da = hr is not None
    CR = min(512, S)

    def body(*refs):
        if with_da:
            (xr_ref, xi_ref, ar_ref, ai_ref, hr_ref, hi_ref, or_ref, oi_ref, dar_ref, dai_ref,
             wr, wi, pr_scr, pi_scr) = refs
        else:
            xr_ref, xi_ref, ar_ref, ai_ref, or_ref, oi_ref, wr, wi, pr_scr, pi_scr = refs

        def load(k, c):
            rows = pl.ds(pl.multiple_of(k * CR, CR), CR)
            for j in range(nb):
                wr[j, rows, :] = xr_ref[j, rows, :].astype(f32)
                wi[j, rows, :] = xi_ref[j, rows, :].astype(f32)
            return c
        lax.fori_loop(0, S // CR, load, 0)

        sign = -1.0 if reverse else 1.0
        a_r = [jnp.broadcast_to(ar_ref[j], (NSUB, LANES)) for j in range(nb)]
        a_i = [jnp.broadcast_to(ai_ref[j], (NSUB, LANES)) * sign for j in range(nb)]

        def step(t, carry):
            row = (L - 1 - t) if reverse else t
            rows = pl.ds(row, NSUB, stride=L)
            out = []
            for j in range(nb):
                sr, si, pr, pi = carry[j]
                nr = a_r[j] * sr - a_i[j] * si + wr.at[j][rows, :]
                ni = a_r[j] * si + a_i[j] * sr + wi.at[j][rows, :]
                wr.at[j][rows, :] = nr
                wi.at[j][rows, :] = ni
                npr = a_r[j] * pr - a_i[j] * pi
                npi = a_r[j] * pi + a_i[j] * pr
                pr_scr[j, pl.ds(row, 1), :] = npr[0:1]
                pi_scr[j, pl.ds(row, 1), :] = npi[0:1]
                out.append((nr, ni, npr, npi))
            return tuple(out)
        z = jnp.zeros((NSUB, LANES), f32)
        one = jnp.ones((NSUB, LANES), f32)
        fin = lax.fori_loop(0, L, step, tuple((z, z, one, z) for _ in range(nb)))

        for j in range(nb):
            sr, si, plr, pli = fin[j]
            plr, pli = plr[0:1], pli[0:1]
            cr = jnp.zeros((1, LANES), f32)
            ci = jnp.zeros((1, LANES), f32)
            order = range(NSUB - 2, -1, -1) if reverse else range(1, NSUB)
            for c in order:
                src = c + 1 if reverse else c - 1
                cr, ci = (sr[src:src + 1] + plr * cr - pli * ci, si[src:src + 1] + plr * ci + pli * cr)
                rows = slice(c * L, (c + 1) * L)
                tr, ti = pr_scr[j], pi_scr[j]
                wr[j, rows, :] += tr * cr - ti * ci
                wi[j, rows, :] += tr * ci + ti * cr
            if with_da:
                first = lax.broadcasted_iota(jnp.int32, (L, LANES), 0) == 0
                dar = jnp.zeros((1, LANES), f32)
                dai = jnp.zeros((1, LANES), f32)
                for c in range(NSUB):
                    rows = slice(c * L, (c + 1) * L)
                    if c == 0:
                        lr_, li_ = jnp.zeros((1, LANES), f32), jnp.zeros((1, LANES), f32)
                    else:
                        before = slice(c * L - HALO, c * L)
                        lr_ = hr_ref[j, before, :].astype(f32)[HALO - 1:]
                        li_ = hi_ref[j, before, :].astype(f32)[HALO - 1:]
                    hpr = jnp.where(first, lr_, pltpu.roll(hr_ref[j, rows, :].astype(f32), 1, 0))
                    hpi = jnp.where(first, li_, pltpu.roll(hi_ref[j, rows, :].astype(f32), 1, 0))
                    gr, gi = wr[j, rows, :], wi[j, rows, :]
                    dar += jnp.sum(hpr * gr + hpi * gi, axis=0, keepdims=True)
                    dai += jnp.sum(hpr * gi - hpi * gr, axis=0, keepdims=True)
                dar_ref[j] = dar
                dai_ref[j] = dai

        def store(k, c):
            rows = pl.ds(pl.multiple_of(k * CR, CR), CR)
            for j in range(nb):
                or_ref[j, rows, :] = wr[j, rows, :].astype(bf16)
                oi_ref[j, rows, :] = wi[j, rows, :].astype(bf16)
            return c
        lax.fori_loop(0, S // CR, store, 0)

    slab = jax.ShapeDtypeStruct((B, NB, S, LANES), bf16)
    sspec = pl.BlockSpec((None, nb, S, LANES), lambda b, j: (b, j, 0, 0))
    aspec = pl.BlockSpec((nb, 1, LANES), lambda b, j: (j, 0, 0))
    in_specs = [sspec, sspec, aspec, aspec]
    out_specs = [sspec, sspec]
    out_shape = [slab, slab]
    args = [xr, xi, ar, ai]
    if with_da:
        in_specs += [sspec, sspec]
        args += [hr, hi]
        dspec = pl.BlockSpec((None, nb, 1, LANES), lambda b, j: (b, j, 0, 0))
        out_specs += [dspec, dspec]
        out_shape += [jax.ShapeDtypeStruct((B, NB, 1, LANES), f32)] * 2
    work = pltpu.VMEM((nb, S, LANES), f32)
    table = pltpu.VMEM((nb, L, LANES), f32)
    return _call(body, name, (B, NB // nb), in_specs, out_specs, out_shape, args,
                 scratch=[work, work, table, table], sem=("parallel", "parallel"), phase=phase)


def _s5_out_parts(hr_ref, hi_ref, u_ref, cr_ref, ci_ref, d_ref, wg_ref, bg_ref, NB):
    hcr = _slab_cat(hr_ref, NB).astype(bf16)
    hci = _slab_cat(hi_ref, NB).astype(bf16)
    u = u_ref[...].astype(f32)
    y2 = _dot(hcr, cr_ref[...]) - _dot(hci, ci_ref[...]) + d_ref[...] * u
    yg = _gelu(y2)
    s = jax.nn.sigmoid(_dot(yg.astype(bf16), wg_ref[...]) + bg_ref[...])
    return hcr, hci, u, y2, yg, s


def _s5_out_specs(B, S, NB, NS, SW, tm):
    sspec = pl.BlockSpec((None, NB, tm, LANES), lambda b, i: (b, 0, i, 0))
    full = lambda r, c: pl.BlockSpec((r, c), lambda b, i: (0, 0))
    return sspec, [sspec, sspec, pl.BlockSpec((None, tm, SW), lambda b, i: (b, i, 0)), full(NS, SW), full(NS, SW),
                   full(1, SW), full(SW, SW), full(1, SW)]


def _s5_out(hr, hi, p3, cbr, cbi, dsk, wglu, bglu, tm):
    B, NB, S, _ = hr.shape
    NS, SW = cbr.shape

    def body(hr_ref, hi_ref, u_ref, cr_ref, ci_ref, d_ref, wg_ref, bg_ref, o_ref):
        _, _, _, _, yg, s = _s5_out_parts(hr_ref, hi_ref, u_ref, cr_ref, ci_ref, d_ref, wg_ref, bg_ref, NB)
        o_ref[...] = (yg * s).astype(bf16)

    _, in_specs = _s5_out_specs(B, S, NB, NS, SW, tm)
    return pl.pallas_call(
        body, name="s5_out", grid=(B, S // tm), in_specs=in_specs,
        out_specs=pl.BlockSpec((None, tm, SW), lambda b, i: (b, i, 0)),
        out_shape=jax.ShapeDtypeStruct((B, S, SW), bf16),
        compiler_params=_params(("parallel", "parallel")))(hr, hi, p3, cbr, cbi, dsk, wglu, bglu)


def _s5_out_bwd(hr, hi, p3, dmix3, cbr, cbi, dsk, wglu, bglu, tm):
    B, NB, S, _ = hr.shape
    NS, SW = cbr.shape

    def body(hr_ref, hi_ref, u_ref, cr_ref, ci_ref, d_ref, wg_ref, bg_ref, da_ref,
             dhr_ref, dhi_ref, du_ref, dcr_ref, dci_ref, dd_ref, dwg_ref, dbg_ref):
        hcr, hci, u, y2, yg, s = _s5_out_parts(hr_ref, hi_ref, u_ref, cr_ref, ci_ref, d_ref, wg_ref, bg_ref, NB)
        da = da_ref[...].astype(f32)
        dz = da * yg * s * (1.0 - s)
        dzb = dz.astype(bf16)
        dyg = da * s + _dg(dzb, wg_ref[...], NT)
        dy2 = dyg * _gelu_grad(y2)
        dyb = dy2.astype(bf16)

        @pl.when((pl.program_id(0) == 0) & (pl.program_id(1) == 0))
        def _():
            for r in (dcr_ref, dci_ref, dd_ref, dwg_ref, dbg_ref):
                r[...] = jnp.zeros_like(r)
        dwg_ref[...] += _dg(yg.astype(bf16), dzb, TN)
        dbg_ref[...] += jnp.sum(dz, axis=0, keepdims=True)
        dd_ref[...] += jnp.sum(dy2 * u, axis=0, keepdims=True)
        dcr_ref[...] += _dg(hcr, dyb, TN)
        dci_ref[...] -= _dg(hci, dyb, TN)
        du_ref[...] = dy2 * d_ref[...]
        dhr = _dg(dyb, cr_ref[...], NT)
        dhi = _dg(dyb, ci_ref[...], NT)
        for j in range(NB):
            dhr_ref[j] = dhr[:, j * LANES:(j + 1) * LANES].astype(bf16)
            dhi_ref[j] = (-dhi[:, j * LANES:(j + 1) * LANES]).astype(bf16)

    sspec, in_specs = _s5_out_specs(B, S, NB, NS, SW, tm)
    in_specs = in_specs + [pl.BlockSpec((None, tm, SW), lambda b, i: (b, i, 0))]
    full = lambda r, c: pl.BlockSpec((r, c), lambda b, i: (0, 0))
    slab = jax.ShapeDtypeStruct((B, NB, S, LANES), bf16)
    mat = lambda r, c: jax.ShapeDtypeStruct((r, c), f32)
    return pl.pallas_call(
        body, name="s5_out_bwd", grid=(B, S // tm), in_specs=in_specs,
        out_specs=[sspec, sspec, pl.BlockSpec((None, tm, SW), lambda b, i: (b, i, 0)), full(NS, SW), full(NS, SW),
                   full(1, SW), full(SW, SW), full(1, SW)],
        out_shape=[slab, slab, jax.ShapeDtypeStruct((B, S, SW), f32), mat(NS, SW), mat(NS, SW), mat(1, SW),
                   mat(SW, SW), mat(1, SW)],
        compiler_params=_params(("arbitrary", "arbitrary")))(hr, hi, p3, cbr, cbi, dsk, wglu, bglu, dmix3)


def _s5_in_bwd(gr, gi, p3, bbd, du_skip, duv3, tm):
    B, NB, S, _ = gr.shape
    SW, NS2 = bbd.shape
    PW = SW + duv3.shape[2]

    def body(gr_ref, gi_ref, u_ref, b_ref, ds_ref, duv_ref, dp_ref, db_ref):
        g = jnp.concatenate([_slab_cat(gr_ref, NB), _slab_cat(gi_ref, NB)], axis=1).astype(bf16)
        du = _dg(g, b_ref[...], NT) + ds_ref[...]
        dp_ref[:, :SW] = du.astype(bf16)
        dp_ref[:, SW:] = duv_ref[...]

        @pl.when((pl.program_id(0) == 0) & (pl.program_id(1) == 0))
        def _():
            db_ref[...] = jnp.zeros_like(db_ref)
        db_ref[...] += _dg(u_ref[...], g, TN)

    sspec = pl.BlockSpec((None, NB, tm, LANES), lambda b, i: (b, 0, i, 0))
    row = lambda c: pl.BlockSpec((None, tm, c), lambda b, i: (b, i, 0))
    return pl.pallas_call(
        body, name="s5_in_bwd", grid=(B, S // tm),
        in_specs=[sspec, sspec, row(SW), pl.BlockSpec((SW, NS2), lambda b, i: (0, 0)), row(SW), row(PW - SW)],
        out_specs=[row(PW), pl.BlockSpec((SW, NS2), lambda b, i: (0, 0))],
        out_shape=[jax.ShapeDtypeStruct((B, S, PW), bf16), jax.ShapeDtypeStruct((SW, NS2), f32)],
        compiler_params=_params(("arbitrary", "arbitrary")))(gr, gi, p3, bbd, du_skip, duv3)


BIG = ['ev_w_in', 'ev_w_out', 'od_w_in', 'od_w_out', 'ffn_w_up', 'ffn_w_down']
ANY = pl.BlockSpec(memory_space=pl.ANY)


def _rtile(rows, mult):
    best = None
    for d in range(mult, min(rows, 512) + 1, mult):
        if rows % d == 0:
            best = d
    assert best is not None, (rows, mult)
    return best


def _pair_sum(name, g, recv, c_idx, out_dtype):
    NCH, R, W = g.shape
    HALF_W = W // 2
    tr = _rtile(R, 16)

    def body(c_ref, a_ref, b_ref, o_ref):
        o_ref[...] = (a_ref[...] + b_ref[...]).astype(out_dtype)

    return pl.pallas_call(
        body, name=name,
        grid_spec=pltpu.PrefetchScalarGridSpec(
            num_scalar_prefetch=1, grid=(NCH, R // tr),
            in_specs=[pl.BlockSpec((None, tr, HALF_W), lambda j, i, c: (j, i, c[0])),
                      pl.BlockSpec((None, tr, HALF_W), lambda j, i, c: (j, i, 0))],
            out_specs=pl.BlockSpec((None, tr, HALF_W), lambda j, i, c: (j, i, 0))),
        out_shape=jax.ShapeDtypeStruct((NCH, R, HALF_W), out_dtype),
        compiler_params=_params(("parallel", "parallel")))(c_idx, g, recv)


def _chip_sum(name, r3, h, k_idx):
    NCH, R, Wh = r3.shape
    tr = _rtile(R, 16)

    def body(k_ref, a_ref, own_ref, o_ref):
        own = own_ref[...].astype(f32)
        t = [jnp.where(k_ref[0] == s, own, a_ref[s].astype(f32)) for s in range(NCH)]
        o_ref[...] = ((t[0] + t[1]) + t[2]) + t[3]

    return pl.pallas_call(
        body, name=name,
        grid_spec=pltpu.PrefetchScalarGridSpec(
            num_scalar_prefetch=1, grid=(R // tr,),
            in_specs=[pl.BlockSpec((NCH, tr, Wh), lambda i, k: (0, i, 0)),
                      pl.BlockSpec((None, tr, Wh), lambda i, k: (k[0], i, 0))],
            out_specs=pl.BlockSpec((tr, Wh), lambda i, k: (i, 0))),
        out_shape=jax.ShapeDtypeStruct((R, Wh), f32),
        compiler_params=_params(("parallel",)))(k_idx, r3, h)


def _adam_math(gg, w, m, v):
    nm = ADAM_B1 * m + (1.0 - ADAM_B1) * gg
    nv = ADAM_B2 * v + (1.0 - ADAM_B2) * jnp.square(gg)
    m_hat = nm / (1.0 - ADAM_B1 ** ADAM_STEP)
    v_hat = nv / (1.0 - ADAM_B2 ** ADAM_STEP)
    return -ADAM_LR * (m_hat / (jnp.sqrt(v_hat) + ADAM_EPS) + ADAM_WD * w), nm, nv


def _adamw(name, mine, theirs, c_idx, w, m, v, lead, transposed, prev=None):
    L, R, W = w.shape
    if transposed:
        bw = LANES if W % LANES == 0 else W
        gspec = pl.BlockSpec((bw, R // 2), lambda i, hf, c: (i, 0))
        wspec = pl.BlockSpec((None, R // 2, bw), lambda i, hf, c: (lead, hf, i))
        grid = (W // bw, 2)
    else:
        tr = _rtile(R, SUBLANES)
        gspec = pl.BlockSpec((tr, W // 2), lambda i, hf, c: (i, 0))
        wspec = pl.BlockSpec((None, tr, W // 2), lambda i, hf, c: (lead, i, hf))
        grid = (R // tr, 2)

    def body(c_ref, a_ref, b_ref, w_ref, m_ref, v_ref, *rest):
        go_ref, d_ref, nm_ref, nv_ref = rest[-4:]
        gg = jnp.where(pl.program_id(1) == c_ref[0], a_ref[...], b_ref[...])
        if transposed:
            gg = gg.T
        d, nm, nv = _adam_math(gg, w_ref[...], m_ref[...], v_ref[...])
        go_ref[...] = gg
        d_ref[...] = d
        nm_ref[...] = nm
        nv_ref[...] = nv

    in_specs = [gspec, gspec, wspec, wspec, wspec]
    args, aliases = [c_idx, mine, theirs, w, m, v], {}
    if prev is not None:
        in_specs += [ANY] * 4
        args += list(prev)
        aliases = {6: 0, 7: 1, 8: 2, 9: 3}
    s = jax.ShapeDtypeStruct((L, R, W), f32)
    return pl.pallas_call(
        body, name=name,
        grid_spec=pltpu.PrefetchScalarGridSpec(num_scalar_prefetch=1, grid=grid, in_specs=in_specs,
                                               out_specs=[wspec] * 4),
        out_shape=[s, s, s, s], input_output_aliases=aliases,
        compiler_params=_params(("parallel", "arbitrary")))(*args)


def _adamw_small(gs, ws, ms, vs):
    n = len(gs)

    def body(*refs):
        for i in range(n):
            d, nm, nv = _adam_math(refs[i][...], refs[n + i][...], refs[2 * n + i][...], refs[3 * n + i][...])
            refs[4 * n + i][...] = d
            refs[5 * n + i][...] = nm
            refs[6 * n + i][...] = nv

    return pl.pallas_call(body, name="adamw_small",
                          out_shape=[jax.ShapeDtypeStruct(t.shape, f32) for t in ws] * 3)(*gs, *ws, *ms, *vs)


def _place():
    x, y, c = lax.axis_index("x"), lax.axis_index("y"), lax.axis_index("c")
    return x, y, c, [(1 - x, y), (x, 1 - y), (1 - x, 1 - y)]


def _gathered_shape(sh, kind):
    if kind == "rows":
        return sh[:-2] + (N_CHIPS * sh[-2], sh[-1])
    if kind == "cols":
        return sh[:-1] + (N_CHIPS * sh[-1],)
    return (N_CHIPS,) + sh


def _place_shard(name, shard, kind, k_idx):
    sh = shard.shape
    r, C = sh[-2], sh[-1]
    L = sh[0] if len(sh) == 3 else 1
    tr = _rtile(r, 16)
    nr = r // tr
    if kind == "rows":
        out3, omap = (L, N_CHIPS * r, C), lambda l, i, k: (l, k[0] * nr + i, 0)
    elif kind == "cols":
        out3, omap = (L, r, N_CHIPS * C), lambda l, i, k: (l, i, k[0])
    else:
        out3, omap = (N_CHIPS, r, C), lambda l, i, k: (k[0], i, 0)

    def body(k_ref, s_ref, o_ref):
        o_ref[...] = s_ref[...]

    out = pl.pallas_call(
        body, name=name,
        grid_spec=pltpu.PrefetchScalarGridSpec(
            num_scalar_prefetch=1, grid=(L, nr),
            in_specs=[pl.BlockSpec((None, tr, C), lambda l, i, k: (l, i, 0))],
            out_specs=pl.BlockSpec((None, tr, C), omap)),
        out_shape=jax.ShapeDtypeStruct(out3, shard.dtype),
        compiler_params=_params(("parallel", "parallel")))(k_idx, shard.reshape(L, r, C))
    return out.reshape(_gathered_shape(sh, kind))


def _gather_phase(shards, fulls, kinds):
    n = len(shards)
    shapes = [s.shape for s in shards]

    def window(ref, a, k, h=None):
        sh, kind = shapes[a], kinds[a]
        r = sh[-2]
        start, size = (0, r) if h is None else (h * (r // 2), r // 2)
        lead = (slice(None),) * (len(sh) - 2)
        if kind == "rows":
            return ref.at[lead + (pl.ds(k * r + start, size), slice(None))]
        if kind == "cols":
            return ref.at[lead + (pl.ds(start, size), pl.ds(pl.multiple_of(k * sh[-1], LANES), sh[-1]))]
        return ref.at[(k,) + lead + (pl.ds(start, size), slice(None))]

    def copies(s_refs, o_refs, sems):
        send_sems, recv_sems = sems
        x, y, c, chips = _place()
        k = 2 * x + y

        def copy(a, j, kk, hh, to, src=None):
            dst = window(o_refs[a], a, kk, hh)
            return pltpu.make_async_remote_copy(
                src_ref=dst if src is None else src, dst_ref=dst, send_sem=send_sems.at[6 * a + j],
                recv_sem=recv_sems.at[6 * a + j], device_id=to, device_id_type=MESH)

        first = []
        for a in range(n):
            r = shapes[a][-2]
            lead = (slice(None),) * (len(shapes[a]) - 2)
            src = s_refs[a].at[lead + (pl.ds(c * (r // 2), r // 2), slice(None))]
            first += [copy(a, j, k, c, (*chip, c), src=src) for j, chip in enumerate(chips)]
        return copy, first, (x, y, c), (x, y, 1 - c), c, chips

    def start(s_refs, o_refs, sems):
        for cp in copies(s_refs, o_refs, sems)[1]:
            cp.start()

    def finish(s_refs, o_refs, sems):
        copy, first, me, sibling, c, chips = copies(s_refs, o_refs, sems)
        passed = []
        for j, (cx, cy) in enumerate(chips):
            for a in range(n):
                copy(a, j, 2 * cx + cy, c, me).wait_recv()
                fwd = copy(a, 3 + j, 2 * cx + cy, c, sibling)
                fwd.start()
                passed.append(fwd)
        for j, (cx, cy) in enumerate(chips):
            for a in range(n):
                copy(a, 3 + j, 2 * cx + cy, 1 - c, me).wait_recv()
        for cp in first + passed:
            cp.wait_send()

    return _Phase(shards, fulls, [jax.ShapeDtypeStruct(f.shape, f.dtype) for f in fulls],
                  [pltpu.SemaphoreType.DMA((6 * n,)), pltpu.SemaphoreType.DMA((6 * n,))], start, finish)


def _swap_phase(gs):
    n = len(gs)

    def copies(g_refs, o_refs, sems):
        send_sems, recv_sems = sems
        x, y, c, _ = _place()
        half = [g.shape[2] // 2 for g in gs]
        return [pltpu.make_async_remote_copy(
            src_ref=g_refs[a].at[:, :, pl.ds(pl.multiple_of((1 - c) * half[a], LANES), half[a])], dst_ref=o_refs[a],
            send_sem=send_sems.at[a], recv_sem=recv_sems.at[a], device_id=(x, y, 1 - c), device_id_type=MESH)
            for a in range(n)]

    def start(g_refs, o_refs, sems):
        for cp in copies(g_refs, o_refs, sems):
            cp.start()

    def finish(g_refs, o_refs, sems):
        for cp in copies(g_refs, o_refs, sems):
            cp.wait()

    return _Phase(gs, [], [jax.ShapeDtypeStruct(g.shape[:2] + (g.shape[2] // 2,), g.dtype) for g in gs],
                  [pltpu.SemaphoreType.DMA((n,)), pltpu.SemaphoreType.DMA((n,))], start, finish)


def _exchange_phase(hs):
    n = len(hs)

    def copies(h_refs, o_refs, sems):
        send_sems, recv_sems = sems
        x, y, c, chips = _place()
        k = 2 * x + y

        def copy(a, j, src_slot, dst_slot):
            cx, cy = chips[j]
            return pltpu.make_async_remote_copy(
                src_ref=h_refs[a].at[src_slot], dst_ref=o_refs[a].at[dst_slot], send_sem=send_sems.at[3 * a + j],
                recv_sem=recv_sems.at[3 * a + j], device_id=(cx, cy, c), device_id_type=MESH)

        sends = [copy(a, j, 2 * cx + cy, k) for a in range(n) for j, (cx, cy) in enumerate(chips)]
        return copy, sends, k, chips

    def start(h_refs, o_refs, sems):
        for cp in copies(h_refs, o_refs, sems)[1]:
            cp.start()

    def finish(h_refs, o_refs, sems):
        copy, sends, k, chips = copies(h_refs, o_refs, sems)
        for a in range(n):
            for j, (cx, cy) in enumerate(chips):
                copy(a, j, k, 2 * cx + cy).wait_recv()
        for cp in sends:
            cp.wait_send()

    return _Phase(hs, [], [jax.ShapeDtypeStruct(h.shape, h.dtype) for h in hs],
                  [pltpu.SemaphoreType.DMA((3 * n,)), pltpu.SemaphoreType.DMA((3 * n,))], start, finish)


def _comm_pair_share(tag, gs):
    n = len(gs)

    def body(*refs):
        g_refs, o_refs, send_sems, recv_sems = refs[:n], refs[n:2 * n], refs[2 * n], refs[2 * n + 1]
        x, y, c, _ = _place()
        cps = [pltpu.make_async_remote_copy(
            src_ref=g_refs[a], dst_ref=o_refs[a], send_sem=send_sems.at[a], recv_sem=recv_sems.at[a],
            device_id=(x, y, 1 - c), device_id_type=MESH) for a in range(n)]
        for cp in cps:
            cp.start()
        for cp in cps:
            cp.wait()

    return pl.pallas_call(
        body, name="comm_pair_share_" + tag, in_specs=[ANY] * n, out_specs=[ANY] * n,
        out_shape=[jax.ShapeDtypeStruct(g.shape, g.dtype) for g in gs],
        scratch_shapes=[pltpu.SemaphoreType.DMA((n,)), pltpu.SemaphoreType.DMA((n,))])(*gs)


def _pad_rows(flat, unit):
    n = flat.shape[-1]
    pad = (-n) % unit
    if pad:
        flat = jnp.pad(flat, [(0, 0)] * (flat.ndim - 1) + [(0, pad)])
    return flat


def _split_chips(full, axis):
    sh = full.shape
    t = full.reshape(sh[:axis] + (N_CHIPS, sh[axis] // N_CHIPS) + sh[axis + 1:])
    return jnp.moveaxis(t, axis, 0).reshape(N_CHIPS, -1)


def _join_chips(stack, shard_shape, axis):
    t = jnp.moveaxis(stack.reshape((N_CHIPS,) + tuple(shard_shape)), 0, axis)
    sh = t.shape
    return t.reshape(sh[:axis] + (sh[axis] * sh[axis + 1],) + sh[axis + 2:])


def _block_diag(blocks):
    G, r, c = blocks.shape
    eye = jnp.eye(G, dtype=blocks.dtype)
    return (blocks[:, :, None, :] * eye[:, None, :, None]).reshape(G * r, G * c)


def _diag_blocks(m, G):
    r, c = m.shape[0] // G, m.shape[1] // G
    idx = jnp.arange(G)
    return m.reshape(G, r, G, c)[idx, :, idx, :]


def _weight_shards(w):
    conv = jnp.concatenate([w[n].reshape(-1) for n in GATHER_F32])
    conv = _pad_rows(conv, 2 * SUBLANES * LANES).reshape(-1, LANES)
    b16 = lambda a: a.astype(bf16)
    return {'ev_w_in': (b16(w['ev_w_in'][0]), "chip"), 'ev_w_out': (b16(w['ev_w_out'][0]), "rows"),
            's5_w_glu': (b16(w['s5_w_glu'][0]), "rows"), 'conv': (conv, "chip"),
            'od_w_in': (b16(w['od_w_in'][0]), "cols"), 'od_w_out': (b16(w['od_w_out'][0]), "rows"),
            'ffn_w_up0': (b16(w['ffn_w_up'][0]), "cols"), 'ffn_w_up1': (b16(w['ffn_w_up'][1]), "cols"),
            'ffn_w_down0': (b16(w['ffn_w_down'][0]), "rows"), 'ffn_w_down1': (b16(w['ffn_w_down'][1]), "rows")}


def kernel(x, mix_norm_g, ffn_norm_g, final_norm_g, ev_w_in, ev_w_out, s5_lam_re, s5_lam_im, s5_log_dt, s5_b_re, s5_b_im, s5_c_re, s5_c_im, s5_d, s5_w_glu, s5_b_glu, gm_w_s, gm_b_s, gm_v_g, od_w_in, od_conv_w, od_conv_b, od_w_out, ffn_w_up, ffn_conv_w, ffn_conv_b, ffn_w_down, loss_target, m_mix_norm_g, m_ffn_norm_g, m_final_norm_g, m_ev_w_in, m_ev_w_out, m_s5_lam_re, m_s5_lam_im, m_s5_log_dt, m_s5_b_re, m_s5_b_im, m_s5_c_re, m_s5_c_im, m_s5_d, m_s5_w_glu, m_s5_b_glu, m_gm_w_s, m_gm_b_s, m_gm_v_g, m_od_w_in, m_od_conv_w, m_od_conv_b, m_od_w_out, m_ffn_w_up, m_ffn_conv_w, m_ffn_conv_b, m_ffn_w_down, v_mix_norm_g, v_ffn_norm_g, v_final_norm_g, v_ev_w_in, v_ev_w_out, v_s5_lam_re, v_s5_lam_im, v_s5_log_dt, v_s5_b_re, v_s5_b_im, v_s5_c_re, v_s5_c_im, v_s5_d, v_s5_w_glu, v_s5_b_glu, v_gm_w_s, v_gm_b_s, v_gm_v_g, v_od_w_in, v_od_conv_w, v_od_conv_b, v_od_w_out, v_ffn_w_up, v_ffn_conv_w, v_ffn_conv_b, v_ffn_w_down):
    loc = dict(locals())
    w = {n: loc[n] for n in WEIGHTS}
    mom = {n: loc["m_" + n] for n in WEIGHTS}
    var = {n: loc["v_" + n] for n in WEIGHTS}

    B, S, D = x.shape
    T = B * S
    SW = s5_d.shape[1]
    G = SW // SSM_GROUP
    NS = G * SSM_STATE
    NB = NS // LANES
    tm = min(512, S)
    tt = min(1024, T)
    c_idx = lax.axis_index("c").astype(jnp.int32).reshape(1)
    k_idx = (2 * lax.axis_index("x") + lax.axis_index("y")).astype(jnp.int32).reshape(1)
    shards = _weight_shards(w)
    placed = {n: _place_shard("place_" + n, s, kd, k_idx) for n, (s, kd) in shards.items()}

    def gather(names):
        return _gather_phase([shards[n][0] for n in names], [placed[n] for n in names], [shards[n][1] for n in names])

    (w_ev_in,) = _run_phase("comm_gather_ev_in", gather(['ev_w_in']))
    w_ev_in = jnp.swapaxes(w_ev_in, 0, 1).reshape(D, -1)

    h0 = x.reshape(T, D)
    (y0, p0), (w_ev_out, w_glu, conv) = _norm_mm("ev_in", h0, mix_norm_g[0], w_ev_in, tm,
                                                 phase=gather(['ev_w_out', 's5_w_glu', 'conv']))
    full, off = {}, 0
    for n in GATHER_F32:
        full[n] = _join_chips(conv.reshape(N_CHIPS, -1)[:, off:off + w[n].size], w[n].shape, SHARD_AXIS[n])
        off += w[n].size
    PW = p0.shape[1]
    p03 = p0.reshape(B, S, PW)
    lr, li, ldt = s5_lam_re[0], s5_lam_im[0], s5_log_dt[0].reshape(G, 1)
    ar, ai, zr, zi = _s5_prep(lr, li, ldt)
    bre = _block_diag(jnp.swapaxes(s5_b_re[0], 1, 2))
    bim = _block_diag(jnp.swapaxes(s5_b_im[0], 1, 2))
    cbr = _block_diag(jnp.swapaxes(s5_c_re[0], 1, 2)).astype(bf16)
    cbi = _block_diag(jnp.swapaxes(s5_c_im[0], 1, 2)).astype(bf16)
    zr_row, zi_row = zr.reshape(1, NS), zi.reshape(1, NS)
    bbd = _s5_bbd(zr_row, zi_row, bre, bim)
    ar_s, ai_s = ar.reshape(NB, 1, LANES), ai.reshape(NB, 1, LANES)
    xr, xi = _s5_in(p03, bbd, SW, tm)
    (hr, hi), (w_up0, w_down0) = _s5_scan("s5_scan", xr, xi, ar_s, ai_s, False,
                                           phase=gather(['ffn_w_up0', 'ffn_w_down0']))
    dsk, bglu = s5_d.reshape(1, SW), s5_b_glu.reshape(1, SW)
    a_out = _s5_out(hr, hi, p03, cbr, cbi, dsk, w_glu, bglu, tm)
    ws, bst, gv = gm_w_s[0], gm_b_s[0].T, gm_v_g.reshape(1, -1)
    mixcat = _gmlp(p0, a_out.reshape(T, SW), ws, bst, gv, SW)
    h1 = _mm_resid("ev_out", mixcat, w_ev_out, h0, tm)

    def ffn_fwd(l, h, w_up, w_down, up_phase=None, down_phase=None):
        res = _norm_mm(f"ffn_up{l}", h, ffn_norm_g[l], w_up, tm, phase=up_phase)
        (z, up), got_up = res if up_phase is not None else (res, None)
        res = _ffn_down(f"ffn_down{l}", up, full['ffn_conv_w'][l], ffn_conv_b[l].reshape(1, -1), w_down, h, S, tm,
                        phase=down_phase)
        (hn, c), got_down = res if down_phase is not None else (res, None)
        return hn, (z, up.reshape(B, S, -1), c.reshape(B, S, -1)), got_up, got_down

    h2, ffn0, (w_up1, w_down1), (w_od_in, w_od_out) = ffn_fwd(
        0, h1, w_up0, w_down0, gather(['ffn_w_up1', 'ffn_w_down1']), gather(['od_w_in', 'od_w_out']))
    w_ups, w_downs = (w_up0, w_up1), (w_down0, w_down1)
    od_cw, od_cb = full['od_conv_w'][0], full['od_conv_b']
    y1, p1 = _norm_mm("od_in", h2, mix_norm_g[1], w_od_in, tm)
    p13 = p1.reshape(B, S, -1)
    sc = _od_act(p13, od_cw, od_cb)
    h3 = _mm_resid("od_out", sc.reshape(T, D), w_od_out, h2, tm)
    h4, ffn1, _, _ = ffn_fwd(1, h3, w_up1, w_down1)

    dh4, dh4b, loss_part, d_final_g = _final_loss(h4, final_norm_g, loss_target.reshape(T, D), tm)
    loss = lax.psum(loss_part[0, 0], ("x", "y", "c"))

    grads = {}

    halves = {}
    chips = lambda g: g.reshape(N_CHIPS, -1, D)

    def pair_sums(names, parts, recv):
        return [_pair_sum(f"pair_sum_{n}", g, r, c_idx, f32 if n == "small" else bf16)
                for n, g, r in zip(names, parts, recv)]

    def reduce_end(tag, names, hsum, r3):
        mine = [_chip_sum(f"chip_sum_{n}", r, h, k_idx) for n, r, h in zip(names, r3, hsum)]
        theirs = _comm_pair_share(tag, mine)
        halves.update({n: (a, b) for n, a, b in zip(names, mine, theirs)})

    def ffn_bwd(l, dh, dhb, h_in, saved, phase=None, swap=False):
        z, up3, c3 = saved
        w_down, w_up = w_downs[l], w_ups[l]
        da = _mm_nt(f"ffn_down_bwd{l}", dhb, w_down, tm)
        res = _ffn_act_bwd(f"ffn_act_bwd{l}", up3, c3, da.reshape(B, S, -1), full['ffn_conv_w'][l], phase=phase)
        (act, dg3, dv3, dcwg, dcwv, dcbg, dcbv), got = res if phase is not None else (res, None)
        g_down = _mm_tn(f"ffn_down_dw{l}", act.reshape(T, -1), dhb, tt)
        dupg, dupv = dg3.reshape(T, -1), dv3.reshape(T, -1)
        F = dupg.shape[1]
        g_up = _mm_tn(f"ffn_up_dw{l}_gate", dupg, z, tt, rows=2 * F)
        g_up = _mm_tn(f"ffn_up_dw{l}_val", dupv, z, tt, rows=2 * F, row_off=F, prev=g_up)
        parts = [chips(g_down), chips(g_up)]
        res = _mm_nt_normbwd(f"ffn_up_bwd{l}", [dupg, dupv], w_up, h_in, ffn_norm_g[l], dh, tm,
                             phase=_swap_phase(parts) if swap else None)
        (dh_new, dhb_new, dg), recv = res if swap else (res, None)
        F = dg3.shape[2]
        dcw = jnp.concatenate([dcwg[:, :F], dcwv[:, :F]], axis=1)
        dcb = jnp.concatenate([dcbg[:, :F], dcbv[:, :F]], axis=1)
        return dh_new, dhb_new, g_down, g_up, dcw, dcb[0], dg[0], got, parts, recv

    dh3, dh3b, gd1, gu1, gcw1, gcb1, gng1, _, _, _ = ffn_bwd(1, dh4, dh4b, h3, ffn1)
    dsc = _mm_nt("od_out_bwd", dh3b, w_od_out, tm)
    g_od_out = _mm_tn("od_out_dw", sc.reshape(T, D), dh3b, tt)
    dbg3, dcg3, dhx3, d_od_cw, d_od_cb = _od_act_bwd(p13, dsc.reshape(B, S, D), od_cw, od_cb)
    dp1 = [t.reshape(T, D) for t in (dbg3, dcg3, dhx3)]
    g_od_in = None
    for i, piece in enumerate(dp1):
        g_od_in = _mm_tn(f"od_in_dw{i}", piece, y1, tt, rows=3 * D, row_off=i * D, prev=g_od_in)
    grads['od_conv_w'] = d_od_cw[None]
    grads['od_conv_b'] = d_od_cb
    layer1 = ['ffn_w_down1', 'ffn_w_up1', 'od_w_out', 'od_w_in']
    parts1 = [chips(g) for g in (gd1, gu1, g_od_out, g_od_in)]
    (dh2, dh2b, gmix1), recv1 = _mm_nt_normbwd("od_in_bwd", dp1, w_od_in, h2, mix_norm_g[1], dh3, tm,
                                               phase=_swap_phase(parts1))
    hsum1 = pair_sums(layer1, parts1, recv1)
    dh1, dh1b, gd0, gu0, gcw0, gcb0, gng0, r3, parts0, recv0 = ffn_bwd(
        0, dh2, dh2b, h1, ffn0, phase=_exchange_phase(hsum1), swap=True)
    reduce_end("layer1", layer1, hsum1, r3)
    ffn0_names = ['ffn_w_down0', 'ffn_w_up0']
    hsum0 = pair_sums(ffn0_names, parts0, recv0)
    grads['ffn_conv_w'] = jnp.stack([gcw0, gcw1])
    grads['ffn_conv_b'] = jnp.stack([gcb0, gcb1])
    grads['ffn_norm_g'] = jnp.stack([gng0, gng1])
    grads['final_norm_g'] = d_final_g[0]

    dmix = _mm_nt("ev_out_bwd", dh1b, w_ev_out, tm)
    g_ev_out = _mm_tn("ev_out_dw", mixcat, dh1b, tt)
    part_evo = [chips(g_ev_out)]
    (duv, d_ws, d_bs, d_gv), recv_evo = _gmlp_bwd(p0, dmix, ws, bst, gv, SW, phase=_swap_phase(part_evo))
    hsum0 = hsum0 + pair_sums(['ev_w_out'], part_evo, recv_evo)
    ffn0_names = ffn0_names + ['ev_w_out']
    grads['gm_w_s'] = d_ws[None]
    grads['gm_b_s'] = d_bs[:, :, 0][None]
    grads['gm_v_g'] = d_gv
    dhr, dhi, du_skip, d_cbr, d_cbi, d_dsk, d_wglu, d_bglu = _s5_out_bwd(
        hr, hi, p03, dmix.reshape(B, S, D), cbr, cbi, dsk, w_glu, bglu, tm)
    grads['s5_c_re'] = jnp.swapaxes(_diag_blocks(d_cbr, G), 1, 2)[None]
    grads['s5_c_im'] = jnp.swapaxes(_diag_blocks(d_cbi, G), 1, 2)[None]
    grads['s5_d'] = d_dsk
    grads['s5_w_glu'] = d_wglu[None]
    grads['s5_b_glu'] = d_bglu
    (gr, gi, dar, dai), r3 = _s5_scan("s5_rscan", dhr, dhi, ar_s, ai_s, True, hr, hi, phase=_exchange_phase(hsum0))
    reduce_end("ffn0", ffn0_names, hsum0, r3)
    dp03, d_bbd = _s5_in_bwd(gr, gi, p03, bbd, du_skip, duv.reshape(B, S, -1), tm)
    d_bre, d_bim, d_zr, d_zi = _s5_bbd_bwd(d_bbd, zr_row, zi_row, bre, bim)
    grads['s5_b_re'] = jnp.swapaxes(_diag_blocks(d_bre, G), 1, 2)[None]
    grads['s5_b_im'] = jnp.swapaxes(_diag_blocks(d_bim, G), 1, 2)[None]
    shp = (-1, G, SSM_STATE)
    d_lr, d_li, d_ldt = _s5_prep_bwd(lr, li, ldt, dar.reshape(shp), dai.reshape(shp), d_zr.reshape(shp),
                                     d_zi.reshape(shp))
    grads['s5_lam_re'] = d_lr[None]
    grads['s5_lam_im'] = d_li[None]
    grads['s5_log_dt'] = d_ldt.reshape(1, G)
    dp0 = dp03.reshape(T, PW)
    g_ev_in = _mm_tn("ev_in_dw", dp0, y0, tt)
    grad_x, _, gmix0 = _mm_nt_normbwd("ev_in_bwd", [dp0], w_ev_in, h0, mix_norm_g[0], dh1, tm)
    grads['mix_norm_g'] = jnp.concatenate([gmix0, gmix1], axis=0)

    small = [n for n in WEIGHTS if n not in BIG]
    segs = []
    for n in small:
        gfull = grads[n].astype(f32)
        if n in SHARD_AXIS:
            segs.append(_split_chips(gfull, SHARD_AXIS[n]))
        else:
            segs.append(jnp.broadcast_to(gfull.reshape(1, -1), (N_CHIPS, gfull.size)))
    unit = 2 * SUBLANES * D
    gsmall = _pad_rows(jnp.concatenate(segs, axis=1), unit).reshape(N_CHIPS, -1, D)
    mixer0 = ['ev_w_in', 'small']
    parts = [chips(g_ev_in), gsmall]
    hsum = pair_sums(mixer0, parts, _run_phase("comm_pair_swap_mixer0", _swap_phase(parts)))
    reduce_end("mixer0", mixer0, hsum, _run_phase("comm_exchange_mixer0", _exchange_phase(hsum)))

    out_g, out_d, out_m, out_v = {}, {}, {}, {}

    def update(n, key, lead, transposed, prev=None):
        res = _adamw(f"adamw_{key}", *halves[key], c_idx, w[n], mom[n], var[n], lead, transposed, prev)
        out_g[n], out_d[n], out_m[n], out_v[n] = res
        return res

    update('ev_w_in', 'ev_w_in', 0, True)
    update('ev_w_out', 'ev_w_out', 0, False)
    update('od_w_in', 'od_w_in', 0, True)
    update('od_w_out', 'od_w_out', 0, False)
    update('ffn_w_up', 'ffn_w_up0', 0, True, prev=update('ffn_w_up', 'ffn_w_up1', 1, True))
    update('ffn_w_down', 'ffn_w_down0', 0, False, prev=update('ffn_w_down', 'ffn_w_down1', 1, False))

    mine, theirs = halves['small']
    first = lax.axis_index("c") == 0
    flat = jnp.concatenate([jnp.where(first, mine, theirs), jnp.where(first, theirs, mine)], axis=1).reshape(-1)
    off = 0
    for n in small:
        out_g[n] = flat[off:off + w[n].size].reshape(w[n].shape)
        off += w[n].size
    res = _adamw_small([out_g[n] for n in small], [w[n] for n in small], [mom[n] for n in small],
                       [var[n] for n in small])
    for i, n in enumerate(small):
        out_d[n], out_m[n], out_v[n] = res[i], res[len(small) + i], res[2 * len(small) + i]

    return (loss, grad_x.reshape(B, S, D), *[out_g[n] for n in WEIGHTS], *[out_d[n] for n in WEIGHTS],
            *[out_m[n] for n in WEIGHTS], *[out_v[n] for n in WEIGHTS])
```

```python
import functools
import math

import jax
import jax.numpy as jnp
from jax import lax
from jax.experimental import pallas as pl
from jax.experimental.pallas import tpu as pltpu

f32 = jnp.float32
bf16 = jnp.bfloat16
MESH = pl.DeviceIdType.MESH

SSM_GROUP = 16
SSM_STATE = 64
GMLP_HEAD = 128
CHUNK = 128
EPS = 1e-6
LAMBDA_RE_MAX = -1e-4
ADAM_LR, ADAM_B1, ADAM_B2, ADAM_EPS, ADAM_WD, ADAM_STEP = 0.001, 0.9, 0.999, 1e-08, 0.01, 10

LANES = 128
SUBLANES = 8
NSUB = 32
HALO = 16
VMEM_LIMIT = 56 * 1024 * 1024
N_CHIPS = 4

WEIGHTS = ['mix_norm_g', 'ffn_norm_g', 'final_norm_g', 'ev_w_in', 'ev_w_out', 's5_lam_re', 's5_lam_im', 's5_log_dt',
           's5_b_re', 's5_b_im', 's5_c_re', 's5_c_im', 's5_d', 's5_w_glu', 's5_b_glu', 'gm_w_s', 'gm_b_s', 'gm_v_g',
           'od_w_in', 'od_conv_w', 'od_conv_b', 'od_w_out', 'ffn_w_up', 'ffn_conv_w', 'ffn_conv_b', 'ffn_w_down']
SHARD_AXIS = {'ev_w_in': 2, 'ev_w_out': 1, 's5_w_glu': 1, 'od_w_in': 2, 'od_conv_w': 2, 'od_conv_b': 1, 'od_w_out': 1,
              'ffn_w_up': 2, 'ffn_conv_w': 2, 'ffn_w_down': 1}
GATHER_BF16 = ['ev_w_in', 'ev_w_out', 's5_w_glu', 'od_w_in', 'od_w_out', 'ffn_w_up', 'ffn_w_down']
GATHER_F32 = ['od_conv_w', 'od_conv_b', 'ffn_conv_w']

_GELU_K0 = math.sqrt(2.0 / math.pi)
_GELU_K1 = 0.044715
NT = (((1,), (1,)), ((), ()))
TN = (((0,), (0,)), ((), ()))


def _pick(n, cap):
    if n <= cap:
        return n
    best = None
    for d in range(LANES, cap + 1, LANES):
        if n % d == 0:
            best = d
    assert best is not None, (n, cap)
    return best


def _params(sem=None):
    return pltpu.CompilerParams(dimension_semantics=sem, vmem_limit_bytes=VMEM_LIMIT)


class _Phase:
    def __init__(self, ins, inplace, outs, sems, start, finish):
        self.ins, self.inplace, self.outs, self.sems = list(ins), list(inplace), list(outs), list(sems)
        self.start, self.finish = start, finish


def _call(body, name, grid, in_specs, out_specs, out_shape, args, scratch=(), sem=None, phase=None):
    if phase is None:
        return pl.pallas_call(body, name=name, grid=grid, in_specs=in_specs, out_specs=out_specs, out_shape=out_shape,
                              scratch_shapes=list(scratch), compiler_params=_params(sem))(*args)
    any_spec = pl.BlockSpec(memory_space=pl.ANY)
    n_in, n_out, n_scr = len(args), len(out_shape), len(scratch)
    p_in = phase.ins + phase.inplace
    ci, co = len(p_in), len(phase.outs)

    def wrapped(*refs):
        ins, cins = refs[:n_in], refs[n_in:n_in + len(phase.ins)]
        b = n_in + ci
        outs, couts = refs[b:b + n_out], refs[b + n_out:b + n_out + co]
        d = b + n_out + co
        scr, csem = refs[d:d + n_scr], refs[d + n_scr:]
        ids = [pl.program_id(i) for i in range(len(grid))]
        first = functools.reduce(jnp.logical_and, [i == 0 for i in ids])
        last = functools.reduce(jnp.logical_and, [i == g - 1 for i, g in zip(ids, grid)])

        @pl.when(first)
        def _():
            phase.start(cins, couts, csem)
        body(*ins, *outs, *scr)

        @pl.when(last)
        def _():
            phase.finish(cins, couts, csem)

    res = pl.pallas_call(
        wrapped, name=name, grid=grid, in_specs=list(in_specs) + [any_spec] * ci,
        out_specs=list(out_specs) + [any_spec] * co, out_shape=list(out_shape) + phase.outs,
        scratch_shapes=list(scratch) + phase.sems,
        input_output_aliases={n_in + len(phase.ins) + i: n_out + i for i in range(len(phase.inplace))},
        compiler_params=_params(tuple("arbitrary" for _ in grid)))(*args, *p_in)
    return res[:n_out], res[n_out:]


def _run_phase(name, phase):
    any_spec = pl.BlockSpec(memory_space=pl.ANY)
    ni, ci, co = len(phase.ins), len(phase.ins) + len(phase.inplace), len(phase.outs)

    def body(*refs):
        cins, couts, csem = refs[:ni], refs[ci:ci + co], refs[ci + co:]
        phase.start(cins, couts, csem)
        phase.finish(cins, couts, csem)

    return pl.pallas_call(
        body, name=name, in_specs=[any_spec] * ci, out_specs=[any_spec] * co, out_shape=phase.outs,
        scratch_shapes=phase.sems, input_output_aliases={ni + i: i for i in range(len(phase.inplace))})(
            *phase.ins, *phase.inplace)


def _gelu(x):
    return 0.5 * x * (1.0 + jnp.tanh(_GELU_K0 * (x + _GELU_K1 * x * x * x)))


def _gelu_grad(x):
    t = jnp.tanh(_GELU_K0 * (x + _GELU_K1 * x * x * x))
    return 0.5 * (1.0 + t) + 0.5 * x * (1.0 - t * t) * _GELU_K0 * (1.0 + 3.0 * _GELU_K1 * x * x)


def _rms_stats(x):
    r = lax.rsqrt(jnp.mean(x * x, axis=-1, keepdims=True) + EPS)
    return x * r, r


def _rms_bwd(dy, xh, r, g):
    dxh = dy * g
    dx = r * (dxh - xh * jnp.mean(dxh * xh, axis=-1, keepdims=True))
    return dx, jnp.sum(dy * xh, axis=0, keepdims=True)


def _dot(a, b):
    return jnp.dot(a, b, preferred_element_type=f32)


def _dg(a, b, dims):
    return lax.dot_general(a, b, dims, preferred_element_type=f32)


def _row_fold(z):
    return z.reshape(z.shape[0] // SUBLANES, SUBLANES, z.shape[1]).sum(axis=0)


def _norm_mm(name, h, g, w, tm, phase=None):
    T, D = h.shape
    N = w.shape[1]
    nc = _pick(N, 512)

    def body(h_ref, g_ref, w_ref, y_ref, o_ref):
        xh, _ = _rms_stats(h_ref[...])
        y = (xh * g_ref[...]).astype(bf16)
        y_ref[...] = y
        for j in range(N // nc):
            o_ref[:, j * nc:(j + 1) * nc] = _dot(y, w_ref[:, j * nc:(j + 1) * nc]).astype(bf16)

    return _call(
        body, name, (T // tm,),
        [pl.BlockSpec((tm, D), lambda i: (i, 0)), pl.BlockSpec((1, D), lambda i: (0, 0)),
         pl.BlockSpec((D, N), lambda i: (0, 0))],
        [pl.BlockSpec((tm, D), lambda i: (i, 0)), pl.BlockSpec((tm, N), lambda i: (i, 0))],
        [jax.ShapeDtypeStruct((T, D), bf16), jax.ShapeDtypeStruct((T, N), bf16)],
        [h, g.reshape(1, D), w], sem=("parallel",), phase=phase)


def _mm_resid(name, a, w, resid, tm):
    T, K = a.shape
    N = w.shape[1]

    def body(a_ref, w_ref, r_ref, o_ref):
        o_ref[...] = r_ref[...] + _dot(a_ref[...], w_ref[...])

    return pl.pallas_call(
        body, name=name, grid=(T // tm,),
        in_specs=[pl.BlockSpec((tm, K), lambda i: (i, 0)), pl.BlockSpec((K, N), lambda i: (0, 0)),
                  pl.BlockSpec((tm, N), lambda i: (i, 0))],
        out_specs=pl.BlockSpec((tm, N), lambda i: (i, 0)),
        out_shape=jax.ShapeDtypeStruct((T, N), f32),
        compiler_params=_params(("parallel",)))(a, w, resid)


def _mm_nt(name, dy, w, tm):
    T, N = dy.shape
    K = w.shape[0]
    kc = _pick(K, 512)

    def body(d_ref, w_ref, o_ref):
        d = d_ref[...].astype(bf16)
        for j in range(K // kc):
            o_ref[:, j * kc:(j + 1) * kc] = _dg(d, w_ref[j * kc:(j + 1) * kc, :], NT).astype(bf16)

    return pl.pallas_call(
        body, name=name, grid=(T // tm,),
        in_specs=[pl.BlockSpec((tm, N), lambda i: (i, 0)), pl.BlockSpec((K, N), lambda i: (0, 0))],
        out_specs=pl.BlockSpec((tm, K), lambda i: (i, 0)),
        out_shape=jax.ShapeDtypeStruct((T, K), bf16),
        compiler_params=_params(("parallel",)))(dy, w)


def _mm_nt_normbwd(name, dys, w, h, g, dh_in, tm, phase=None):
    n = len(dys)
    T = dys[0].shape[0]
    D = w.shape[0]
    widths = [d.shape[1] for d in dys]
    offs = [sum(widths[:i]) for i in range(n)]

    def body(*refs):
        d_refs = refs[:n]
        w_ref, h_ref, g_ref, dh_ref, o_ref, ob_ref, dg_ref = refs[n:]
        dz = _dg(d_refs[0][...], w_ref[:, :widths[0]], NT)
        for i in range(1, n):
            dz += _dg(d_refs[i][...], w_ref[:, offs[i]:offs[i] + widths[i]], NT)
        xh, r = _rms_stats(h_ref[...])
        dx, dg = _rms_bwd(dz, xh, r, g_ref[...])
        out = dh_ref[...] + dx
        o_ref[...] = out
        ob_ref[...] = out.astype(bf16)

        @pl.when(pl.program_id(0) == 0)
        def _():
            dg_ref[...] = jnp.zeros_like(dg_ref)
        dg_ref[...] += dg

    row = lambda c: pl.BlockSpec((tm, c), lambda i: (i, 0))
    return _call(
        body, name, (T // tm,),
        [row(c) for c in widths] + [pl.BlockSpec((D, sum(widths)), lambda i: (0, 0)), row(D),
                                    pl.BlockSpec((1, D), lambda i: (0, 0)), row(D)],
        [row(D), row(D), pl.BlockSpec((1, D), lambda i: (0, 0))],
        [jax.ShapeDtypeStruct((T, D), f32), jax.ShapeDtypeStruct((T, D), bf16), jax.ShapeDtypeStruct((1, D), f32)],
        [*dys, w, h, g.reshape(1, D), dh_in], sem=("arbitrary",), phase=phase)


def _mm_tn(name, a, b, tt, rows=None, row_off=0, prev=None):
    T, K = a.shape
    N = b.shape[1]
    rows = K if rows is None else rows
    tk = _pick(K, 1408)
    tn = _pick(N, 1024)
    assert row_off % tk == 0
    kb = row_off // tk

    def body(a_ref, b_ref, *rest):
        o_ref = rest[-1]

        @pl.when(pl.program_id(2) == 0)
        def _():
            o_ref[...] = jnp.zeros_like(o_ref)
        o_ref[...] += _dg(a_ref[...], b_ref[...], TN)

    in_specs = [pl.BlockSpec((tt, tk), lambda k, n, t: (t, k)), pl.BlockSpec((tt, tn), lambda k, n, t: (t, n))]
    args, aliases = [a, b], {}
    if prev is not None:
        in_specs.append(ANY)
        args.append(prev)
        aliases = {2: 0}
    return pl.pallas_call(
        body, name=name, grid=(K // tk, N // tn, T // tt), in_specs=in_specs,
        out_specs=pl.BlockSpec((tk, tn), lambda k, n, t: (k + kb, n)),
        out_shape=jax.ShapeDtypeStruct((rows, N), f32), input_output_aliases=aliases,
        compiler_params=_params(("parallel", "parallel", "arbitrary")))(*args)


def _final_loss(h, g, tgt, tm):
    T, D = h.shape

    def body(h_ref, g_ref, t_ref, dh_ref, dhb_ref, loss_ref, dg_ref):
        xh, r = _rms_stats(h_ref[...])
        gg = g_ref[...]
        diff = xh * gg - t_ref[...]
        dy = diff * (1.0 / D)
        dx, dg = _rms_bwd(dy, xh, r, gg)
        dh_ref[...] = dx
        dhb_ref[...] = dx.astype(bf16)

        @pl.when(pl.program_id(0) == 0)
        def _():
            dg_ref[...] = jnp.zeros_like(dg_ref)
            loss_ref[...] = jnp.zeros_like(loss_ref)
        dg_ref[...] += dg
        loss_ref[...] += (0.5 / D) * jnp.sum(jnp.sum(diff * diff, axis=1, keepdims=True), axis=0, keepdims=True)

    return pl.pallas_call(
        body, name="final_loss", grid=(T // tm,),
        in_specs=[pl.BlockSpec((tm, D), lambda i: (i, 0)), pl.BlockSpec((1, D), lambda i: (0, 0)),
                  pl.BlockSpec((tm, D), lambda i: (i, 0))],
        out_specs=[pl.BlockSpec((tm, D), lambda i: (i, 0)), pl.BlockSpec((tm, D), lambda i: (i, 0)),
                   pl.BlockSpec((1, 1), lambda i: (0, 0)), pl.BlockSpec((1, D), lambda i: (0, 0))],
        out_shape=[jax.ShapeDtypeStruct((T, D), f32), jax.ShapeDtypeStruct((T, D), bf16),
                   jax.ShapeDtypeStruct((1, 1), f32), jax.ShapeDtypeStruct((1, D), f32)],
        compiler_params=_params(("arbitrary",)))(h, g.reshape(1, D), tgt)


def _taps(load, r0, R):
    main = load(r0, R)
    hs = pl.multiple_of(jnp.maximum(r0 - HALO, 0), HALO)
    halo = load(hs, HALO) * (r0 > 0).astype(f32)
    ext = jnp.concatenate([halo, main], axis=0)
    xm1 = pltpu.roll(ext, 1, 0)[HALO:]
    xm2 = pltpu.roll(ext, 2, 0)[HALO:]
    return xm2, xm1, main


def _conv(w, b, taps):
    return b + w[0:1] * taps[0] + w[1:2] * taps[1] + w[2:3] * taps[2]


def _ref_load(ref):
    return lambda s, n: ref[pl.ds(s, n), :].astype(f32)


def _ffn_down(name, up, cw, cb, w_down, resid, S, tm, phase=None):
    T, F2 = up.shape
    F = F2 // 2
    D = w_down.shape[1]
    cwid = _pick(F, 256)
    per_seq = S // tm

    def body(u_ref, halo_ref, cw_ref, cb_ref, w_ref, r_ref, o_ref, c_ref):
        keep = (pl.program_id(0) % per_seq > 0).astype(f32)

        def conv(off):
            cols = slice(off, off + cwid)
            main = u_ref[:, cols].astype(f32)
            ext = jnp.concatenate([halo_ref[:, cols].astype(f32) * keep, main], axis=0)
            taps = (pltpu.roll(ext, 2, 0)[HALO:], pltpu.roll(ext, 1, 0)[HALO:], main)
            return _conv(cw_ref[:, cols], cb_ref[:, cols], taps)

        acc = r_ref[...]
        for j in range(F // cwid):
            cg, cv = conv(j * cwid), conv(F + j * cwid)
            c_ref[:, j * cwid:(j + 1) * cwid] = cg.astype(bf16)
            c_ref[:, F + j * cwid:F + (j + 1) * cwid] = cv.astype(bf16)
            a = (cg * jax.nn.sigmoid(cg) * cv).astype(bf16)
            acc = acc + _dot(a, w_ref[j * cwid:(j + 1) * cwid, :])
        o_ref[...] = acc

    full = lambda r, c: pl.BlockSpec((r, c), lambda i: (0, 0))
    return _call(
        body, name, (T // tm,),
        [pl.BlockSpec((tm, F2), lambda i: (i, 0)),
         pl.BlockSpec((HALO, F2), lambda i: (jnp.maximum(i * (tm // HALO) - 1, 0), 0)),
         full(3, F2), full(1, F2), full(F, D), pl.BlockSpec((tm, D), lambda i: (i, 0))],
        [pl.BlockSpec((tm, D), lambda i: (i, 0)), pl.BlockSpec((tm, F2), lambda i: (i, 0))],
        [jax.ShapeDtypeStruct((T, D), f32), jax.ShapeDtypeStruct((T, F2), bf16)],
        [up, up, cw, cb, w_down, resid], sem=("parallel",), phase=phase)


def _rev_conv_rows(d, nxt, w):
    R = d.shape[0]
    ext = jnp.concatenate([d, nxt], axis=0)
    n = R + HALO
    xp1 = pltpu.roll(ext, n - 1, 0)[:R]
    xp2 = pltpu.roll(ext, n - 2, 0)[:R]
    return w[2:3] * d + w[1:2] * xp1 + w[0:1] * xp2, xp1, xp2


def _conv_grad_acc(acc, dc, taps):
    return (acc[0] + _row_fold(dc * taps[0]), acc[1] + _row_fold(dc * taps[1]), acc[2] + _row_fold(dc * taps[2]),
            acc[3] + _row_fold(dc))


def _conv_grad_out(dcw_ref, dcb_ref, acc):
    @pl.when(pl.program_id(1) == 0)
    def _():
        dcw_ref[...] = jnp.zeros_like(dcw_ref)
        dcb_ref[...] = jnp.zeros_like(dcb_ref)
    for k in range(3):
        dcw_ref[k:k + 1, :] += jnp.sum(acc[k], axis=0, keepdims=True)
    dcb_ref[...] += jnp.sum(acc[3], axis=0, keepdims=True)


def _ffn_act_bwd(name, up3, c3, da3, cw, phase=None):
    B, S, F2 = up3.shape
    F = F2 // 2
    cwid = _pick(F, 256)
    nF = F // cwid
    R = min(256, S)
    nR = S // R

    def body(xg_ref, xv_ref, cg_ref, cv_ref, da_ref, wg_ref, wv_ref,
             act_ref, dg_ref, dv_ref, dcwg_ref, dcwv_ref, dcbg_ref, dcbv_ref, sum_scr):
        wg, wv = wg_ref[...], wv_ref[...]

        def half(d, nxt, w, x_ref, rows, acc, out_ref):
            out, xp1, xp2 = _rev_conv_rows(d, nxt, w)
            out_ref[rows, :] = out.astype(bf16)
            x = x_ref[rows, :].astype(f32)
            return (acc[0] + _row_fold(xp2 * x), acc[1] + _row_fold(xp1 * x), acc[2] + _row_fold(d * x),
                    acc[3] + _row_fold(d))

        def step(i, carry):
            ng, nv, accg, accv = carry
            rows = pl.ds(pl.multiple_of((nR - 1 - i) * R, R), R)
            cg, cv = cg_ref[rows, :].astype(f32), cv_ref[rows, :].astype(f32)
            da = da_ref[rows, :].astype(f32)
            sg = jax.nn.sigmoid(cg)
            act_ref[rows, :] = (cg * sg * cv).astype(bf16)
            dgate = da * cv * (sg * (1.0 + cg * (1.0 - sg)))
            dval = da * (cg * sg)
            accg = half(dgate, ng, wg, xg_ref, rows, accg, dg_ref)
            accv = half(dval, nv, wv, xv_ref, rows, accv, dv_ref)
            return dgate[:HALO], dval[:HALO], accg, accv
        z = jnp.zeros((SUBLANES, cwid), f32)
        zh = jnp.zeros((HALO, cwid), f32)
        _, _, accg, accv = lax.fori_loop(0, nR, step, (zh, zh, (z, z, z, z), (z, z, z, z)))
        j = pl.program_id(1)
        for half_i, (acc, dcw_ref, dcb_ref) in enumerate(((accg, dcwg_ref, dcbg_ref), (accv, dcwv_ref, dcbv_ref))):
            @pl.when(pl.program_id(0) == 0)
            def _():
                sum_scr[half_i, j] = jnp.zeros((SUBLANES, cwid), f32)
            for k in range(4):
                sum_scr[half_i, j, k:k + 1, :] += jnp.sum(acc[k], axis=0, keepdims=True)
            dcw_ref[...] = sum_scr[half_i, j, 0:3, :]
            dcb_ref[...] = sum_scr[half_i, j, 3:4, :]

    blk = lambda off: pl.BlockSpec((None, S, cwid), lambda b, j: (b, 0, off + j))
    wblk = lambda off: pl.BlockSpec((3, cwid), lambda b, j: (0, off + j))
    sums = lambda r: pl.BlockSpec((r, cwid), lambda b, j: (0, jnp.where(b == B - 1, j, nF)))
    half_shape = jax.ShapeDtypeStruct((B, S, F), bf16)
    return _call(
        body, name, (B, nF),
        [blk(0), blk(nF), blk(0), blk(nF), blk(0), wblk(0), wblk(nF)],
        [blk(0), blk(0), blk(0), sums(3), sums(3), sums(1), sums(1)],
        [half_shape, half_shape, half_shape, jax.ShapeDtypeStruct((3, F + cwid), f32),
         jax.ShapeDtypeStruct((3, F + cwid), f32), jax.ShapeDtypeStruct((1, F + cwid), f32),
         jax.ShapeDtypeStruct((1, F + cwid), f32)],
        [up3, up3, c3, c3, da3, cw, cw], scratch=[pltpu.VMEM((2, nF, SUBLANES, cwid), f32)],
        sem=("arbitrary", "arbitrary"), phase=phase)


def _od_act(p3, cw, cb):
    B, S, D3 = p3.shape
    D = D3 // 3
    cwid = _pick(D, 256)
    nD = D // cwid
    R = min(256, S)

    def body(bg_ref, cg_ref, hx_ref, w_ref, b_ref, o_ref):
        w, b = w_ref[...], b_ref[...]
        q = lambda s, n: cg_ref[pl.ds(s, n), :].astype(f32) * hx_ref[pl.ds(s, n), :].astype(f32)

        def chunk(r, c):
            r0 = pl.multiple_of(r * R, R)
            cq = _conv(w, b, _taps(q, r0, R))
            o_ref[pl.ds(r0, R), :] = (bg_ref[pl.ds(r0, R), :].astype(f32) * cq).astype(bf16)
            return c
        lax.fori_loop(0, S // R, chunk, 0)

    blk = lambda off: pl.BlockSpec((None, S, cwid), lambda b, j: (b, 0, off + j))
    return pl.pallas_call(
        body, name="od_act", grid=(B, nD),
        in_specs=[blk(0), blk(nD), blk(2 * nD), pl.BlockSpec((3, cwid), lambda b, j: (0, j)),
                  pl.BlockSpec((1, cwid), lambda b, j: (0, j))],
        out_specs=pl.BlockSpec((None, S, cwid), lambda b, j: (b, 0, j)),
        out_shape=jax.ShapeDtypeStruct((B, S, D), bf16),
        compiler_params=_params(("parallel", "parallel")))(p3, p3, p3, cw, cb)


def _od_act_bwd(p3, dsc3, cw, cb):
    B, S, D3 = p3.shape
    D = D3 // 3
    cwid = _pick(D, 256)
    nD = D // cwid
    R = min(256, S)
    nR = S // R

    def body(bg_ref, cg_ref, hx_ref, d_ref, w_ref, b_ref, dbg_ref, dcg_ref, dhx_ref, dcw_ref, dcb_ref):
        w, b = w_ref[...], b_ref[...]
        q = lambda s, n: cg_ref[pl.ds(s, n), :].astype(f32) * hx_ref[pl.ds(s, n), :].astype(f32)

        def step(i, carry):
            nxt, acc = carry
            r0 = pl.multiple_of((nR - 1 - i) * R, R)
            rows = pl.ds(r0, R)
            tq = _taps(q, r0, R)
            cq = _conv(w, b, tq)
            d = d_ref[rows, :].astype(f32)
            dbg_ref[rows, :] = (d * cq).astype(bf16)
            dcq = d * bg_ref[rows, :].astype(f32)
            dq, _, _ = _rev_conv_rows(dcq, nxt, w)
            dcg_ref[rows, :] = (dq * hx_ref[rows, :].astype(f32)).astype(bf16)
            dhx_ref[rows, :] = (dq * cg_ref[rows, :].astype(f32)).astype(bf16)
            return dcq[:HALO], _conv_grad_acc(acc, dcq, tq)
        z = jnp.zeros((SUBLANES, cwid), f32)
        _, acc = lax.fori_loop(0, nR, step, (jnp.zeros((HALO, cwid), f32), (z, z, z, z)))
        _conv_grad_out(dcw_ref, dcb_ref, acc)

    blk = lambda off: pl.BlockSpec((None, S, cwid), lambda j, b: (b, 0, off + j))
    part = jax.ShapeDtypeStruct((B, S, D), bf16)
    return pl.pallas_call(
        body, name="od_act_bwd", grid=(nD, B),
        in_specs=[blk(0), blk(nD), blk(2 * nD), blk(0), pl.BlockSpec((3, cwid), lambda j, b: (0, j)),
                  pl.BlockSpec((1, cwid), lambda j, b: (0, j))],
        out_specs=[blk(0), blk(0), blk(0), pl.BlockSpec((3, cwid), lambda j, b: (0, j)),
                   pl.BlockSpec((1, cwid), lambda j, b: (0, j))],
        out_shape=[part, part, part, jax.ShapeDtypeStruct((3, D), f32), jax.ShapeDtypeStruct((1, D), f32)],
        compiler_params=_params(("parallel", "arbitrary")))(p3, p3, p3, dsc3, cw, cb)


def _gmlp_parts(p, gv, SW, GW):
    uv = p[:, SW:].astype(f32)
    ge = _gelu(uv)
    u, v = ge[:, :GW], ge[:, GW:]
    vh, r = _rms_stats(v)
    return uv, u, vh, r, vh * gv


def _tril():
    rows = lax.broadcasted_iota(jnp.int32, (CHUNK, CHUNK), 0)
    cols = lax.broadcasted_iota(jnp.int32, (CHUNK, CHUNK), 1)
    return rows >= cols


def _chunks_per_step(T):
    return 4 if T % (4 * CHUNK) == 0 else 1


def _gmlp(p0, a_out, ws, bst, gv, SW):
    T, PW = p0.shape
    GW = (PW - SW) // 2
    H = GW // GMLP_HEAD
    D = SW + GW

    kc = _chunks_per_step(T)
    rb = kc * CHUNK

    def body(p_ref, a_ref, ws_ref, b_ref, gv_ref, o_ref):
        tri = _tril()
        o_ref[:, :SW] = a_ref[...]
        wm = [jnp.where(tri, ws_ref[hh], 0.0).astype(bf16) for hh in range(H)]
        for q in range(kc):
            rows = slice(q * CHUNK, (q + 1) * CHUNK)
            _, u, _, _, vn = _gmlp_parts(p_ref[rows, :], gv_ref[...], SW, GW)
            for hh in range(H):
                sl = slice(hh * GMLP_HEAD, (hh + 1) * GMLP_HEAD)
                gate = _dot(wm[hh], vn[:, sl].astype(bf16)) + b_ref[:, hh:hh + 1]
                o_ref[rows, SW + hh * GMLP_HEAD:SW + (hh + 1) * GMLP_HEAD] = (u[:, sl] * gate).astype(bf16)

    return pl.pallas_call(
        body, name="gmlp", grid=(T // rb,),
        in_specs=[pl.BlockSpec((rb, PW), lambda i: (i, 0)), pl.BlockSpec((rb, SW), lambda i: (i, 0)),
                  pl.BlockSpec((H, CHUNK, CHUNK), lambda i: (0, 0, 0)), pl.BlockSpec((CHUNK, H), lambda i: (0, 0)),
                  pl.BlockSpec((1, GW), lambda i: (0, 0))],
        out_specs=pl.BlockSpec((rb, D), lambda i: (i, 0)),
        out_shape=jax.ShapeDtypeStruct((T, D), bf16),
        compiler_params=_params(("parallel",)))(p0, a_out, ws, bst, gv)


def _gmlp_bwd(p0, dmix, ws, bst, gv, SW, phase=None):
    T, PW = p0.shape
    GW = (PW - SW) // 2
    H = GW // GMLP_HEAD
    D = SW + GW

    kc = _chunks_per_step(T)
    rb = kc * CHUNK

    def body(p_ref, d_ref, ws_ref, b_ref, gv_ref, duv_ref, dws_ref, dbs_ref, dgv_ref):
        gv_ = gv_ref[...]
        tri = _tril()

        @pl.when(pl.program_id(0) == 0)
        def _():
            dws_ref[...] = jnp.zeros_like(dws_ref)
            dbs_ref[...] = jnp.zeros_like(dbs_ref)
            dgv_ref[...] = jnp.zeros_like(dgv_ref)
        wm = [jnp.where(tri, ws_ref[hh], 0.0).astype(bf16) for hh in range(H)]
        for q in range(kc):
            rows = slice(q * CHUNK, (q + 1) * CHUNK)
            uv, u, vh, r, vn = _gmlp_parts(p_ref[rows, :], gv_, SW, GW)
            dout = d_ref[rows, SW:].astype(f32)
            du, dvn = [], []
            for hh in range(H):
                sl = slice(hh * GMLP_HEAD, (hh + 1) * GMLP_HEAD)
                vnh = vn[:, sl].astype(bf16)
                gate = _dot(wm[hh], vnh) + b_ref[:, hh:hh + 1]
                dgate = dout[:, sl] * u[:, sl]
                du.append(dout[:, sl] * gate)
                dgb = dgate.astype(bf16)
                dws_ref[hh] += jnp.where(tri, _dg(dgb, vnh, NT), 0.0)
                dbs_ref[hh] += jnp.broadcast_to(jnp.sum(dgate, axis=1, keepdims=True), (CHUNK, CHUNK))
                dvn.append(_dg(wm[hh], dgb, TN))
            dvn = jnp.concatenate(dvn, axis=1)
            dv, dgv = _rms_bwd(dvn, vh, r, gv_)
            dgv_ref[...] += dgv
            dge = jnp.concatenate(du + [dv], axis=1)
            duv_ref[rows, :] = (dge * _gelu_grad(uv)).astype(bf16)

    fixed = pl.BlockSpec((H, CHUNK, CHUNK), lambda i: (0, 0, 0))
    return _call(
        body, "gmlp_bwd", (T // rb,),
        [pl.BlockSpec((rb, PW), lambda i: (i, 0)), pl.BlockSpec((rb, D), lambda i: (i, 0)), fixed,
         pl.BlockSpec((CHUNK, H), lambda i: (0, 0)), pl.BlockSpec((1, GW), lambda i: (0, 0))],
        [pl.BlockSpec((rb, 2 * GW), lambda i: (i, 0)), fixed, fixed, pl.BlockSpec((1, GW), lambda i: (0, 0))],
        [jax.ShapeDtypeStruct((T, 2 * GW), bf16), jax.ShapeDtypeStruct((H, CHUNK, CHUNK), f32),
         jax.ShapeDtypeStruct((H, CHUNK, CHUNK), f32), jax.ShapeDtypeStruct((1, GW), f32)],
        [p0, dmix, ws, bst, gv], sem=("arbitrary",), phase=phase)


def _s5_disc(lr, li, ldt):
    lr = jnp.minimum(lr, LAMBDA_RE_MAX)
    dt = jnp.exp(ldt)
    mag = jnp.exp(lr * dt)
    ar = mag * jnp.cos(li * dt)
    ai = mag * jnp.sin(li * dt)
    den = lr * lr + li * li
    nr = ar - 1.0
    zr = (nr * lr + ai * li) / den
    zi = (ai * lr - nr * li) / den
    return ar, ai, zr, zi


def _s5_prep(lr, li, ldt):
    G, P = lr.shape

    def body(lr_ref, li_ref, ldt_ref, ar_ref, ai_ref, zr_ref, zi_ref):
        ar, ai, zr, zi = _s5_disc(lr_ref[...], li_ref[...], ldt_ref[...])
        ar_ref[...] = ar
        ai_ref[...] = ai
        zr_ref[...] = zr
        zi_ref[...] = zi

    s = jax.ShapeDtypeStruct((G, P), f32)
    return pl.pallas_call(body, name="s5_prep", out_shape=[s, s, s, s])(lr, li, ldt)


def _s5_prep_bwd(lr, li, ldt, dar, dai, dzr, dzi):
    G, P = lr.shape

    def body(lr_ref, li_ref, ldt_ref, dar_ref, dai_ref, dzr_ref, dzi_ref, o1, o2, o3):
        _, vjp = jax.vjp(_s5_disc, lr_ref[...], li_ref[...], ldt_ref[...])
        cts = tuple(jnp.sum(r[...], axis=0) for r in (dar_ref, dai_ref, dzr_ref, dzi_ref))
        a, b, c = vjp(cts)
        o1[...] = a
        o2[...] = b
        o3[...] = c

    s = jax.ShapeDtypeStruct((G, P), f32)
    return pl.pallas_call(body, name="s5_prep_bwd", out_shape=[s, s, jax.ShapeDtypeStruct((G, 1), f32)])(
        lr, li, ldt, dar, dai, dzr, dzi)


def _s5_bbd(zr, zi, bre, bim):
    SW, NS = bre.shape

    def body(zr_ref, zi_ref, br_ref, bi_ref, o_ref):
        zr_, zi_, br, bi = zr_ref[...], zi_ref[...], br_ref[...], bi_ref[...]
        o_ref[:, :NS] = (zr_ * br - zi_ * bi).astype(bf16)
        o_ref[:, NS:] = (zr_ * bi + zi_ * br).astype(bf16)

    return pl.pallas_call(body, name="s5_bbd", out_shape=jax.ShapeDtypeStruct((SW, 2 * NS), bf16))(zr, zi, bre, bim)


def _s5_bbd_bwd(dbbd, zr, zi, bre, bim):
    SW, NS = bre.shape

    def body(d_ref, zr_ref, zi_ref, br_ref, bi_ref, dbr_ref, dbi_ref, dzr_ref, dzi_ref):
        zr_, zi_, br, bi = zr_ref[...], zi_ref[...], br_ref[...], bi_ref[...]
        dr, di = d_ref[:, :NS], d_ref[:, NS:]
        dbr_ref[...] = zr_ * dr + zi_ * di
        dbi_ref[...] = zr_ * di - zi_ * dr
        dzr_ref[...] = jnp.sum(dr * br + di * bi, axis=0, keepdims=True)
        dzi_ref[...] = jnp.sum(di * br - dr * bi, axis=0, keepdims=True)

    m = jax.ShapeDtypeStruct((SW, NS), f32)
    v = jax.ShapeDtypeStruct((1, NS), f32)
    return pl.pallas_call(body, name="s5_bbd_bwd", out_shape=[m, m, v, v])(dbbd, zr, zi, bre, bim)


def _slab_cat(ref, NB):
    return jnp.concatenate([ref[j] for j in range(NB)], axis=1)


def _s5_in(p3, bbd, SW, tm):
    B, S, PW = p3.shape
    NS = bbd.shape[1] // 2
    NB = NS // LANES

    def body(u_ref, b_ref, xr_ref, xi_ref):
        x = _dot(u_ref[...], b_ref[...])
        for j in range(NB):
            xr_ref[j] = x[:, j * LANES:(j + 1) * LANES].astype(bf16)
            xi_ref[j] = x[:, NS + j * LANES:NS + (j + 1) * LANES].astype(bf16)

    slab = jax.ShapeDtypeStruct((B, NB, S, LANES), bf16)
    sspec = pl.BlockSpec((None, NB, tm, LANES), lambda b, i: (b, 0, i, 0))
    return pl.pallas_call(
        body, name="s5_in", grid=(B, S // tm),
        in_specs=[pl.BlockSpec((None, tm, SW), lambda b, i: (b, i, 0)), pl.BlockSpec((SW, 2 * NS), lambda b, i: (0, 0))],
        out_specs=[sspec, sspec], out_shape=[slab, slab],
        compiler_params=_params(("parallel", "parallel")))(p3, bbd)


def _s5_scan(name, xr, xi, ar, ai, reverse, hr=None, hi=None, phase=None):
    B, NB, S, _ = xr.shape
    L = S // NSUB
    nb = 2 if NB % 2 == 0 else 1
    with_da = hr is not None
    CR = min(512, S)

    def body(*refs):
        if with_da:
            (xr_ref, xi_ref, ar_ref, ai_ref, hr_ref, hi_ref, or_ref, oi_ref, dar_ref, dai_ref,
             vr, vi, wr, wi, pr_scr, pi_scr) = refs
        else:
            xr_ref, xi_ref, ar_ref, ai_ref, or_ref, oi_ref, vr, vi, wr, wi, pr_scr, pi_scr = refs

        def load(k, c):
            rows = pl.ds(pl.multiple_of(k * CR, CR), CR)
            for j in range(nb):
                vr[j, rows, :] = xr_ref[j, rows, :].astype(f32)
                vi[j, rows, :] = xi_ref[j, rows, :].astype(f32)
            return c
        lax.fori_loop(0, S // CR, load, 0)

        sign = -1.0 if reverse else 1.0
        a_r = [jnp.broadcast_to(ar_ref[j], (NSUB, LANES)) for j in range(nb)]
        a_i = [jnp.broadcast_to(ai_ref[j], (NSUB, LANES)) * sign for j in range(nb)]

        def step(t, carry):
            row = (L - 1 - t) if reverse else t
            rows = pl.ds(row, NSUB, stride=L)
            out = []
            for j in range(nb):
                sr, si, pr, pi = carry[j]
                nr = a_r[j] * sr - a_i[j] * si + vr.at[j][rows, :]
                ni = a_r[j] * si + a_i[j] * sr + vi.at[j][rows, :]
                wr.at[j][rows, :] = nr
                wi.at[j][rows, :] = ni
                npr = a_r[j] * pr - a_i[j] * pi
                npi = a_r[j] * pi + a_i[j] * pr
                pr_scr[j, pl.ds(row, 1), :] = npr[0:1]
                pi_scr[j, pl.ds(row, 1), :] = npi[0:1]
                out.append((nr, ni, npr, npi))
            return tuple(out)
        z = jnp.zeros((NSUB, LANES), f32)
        one = jnp.ones((NSUB, LANES), f32)
        fin = lax.fori_loop(0, L, step, tuple((z, z, one, z) for _ in range(nb)))

        for j in range(nb):
            sr, si, plr, pli = fin[j]
            plr, pli = plr[0:1], pli[0:1]
            cr = jnp.zeros((1, LANES), f32)
            ci = jnp.zeros((1, LANES), f32)
            order = range(NSUB - 2, -1, -1) if reverse else range(1, NSUB)
            for c in order:
                src = c + 1 if reverse else c - 1
                cr, ci = (sr[src:src + 1] + plr * cr - pli * ci, si[src:src + 1] + plr * ci + pli * cr)
                rows = slice(c * L, (c + 1) * L)
                tr, ti = pr_scr[j], pi_scr[j]
                wr[j, rows, :] += tr * cr - ti * ci
                wi[j, rows, :] += tr * ci + ti * cr
            if with_da:
                first = lax.broadcasted_iota(jnp.int32, (L, LANES), 0) == 0
                dar = jnp.zeros((1, LANES), f32)
                dai = jnp.zeros((1, LANES), f32)
                for c in range(NSUB):
                    rows = slice(c * L, (c + 1) * L)
                    if c == 0:
                        lr_, li_ = jnp.zeros((1, LANES), f32), jnp.zeros((1, LANES), f32)
                    else:
                        before = slice(c * L - HALO, c * L)
                        lr_ = hr_ref[j, before, :].astype(f32)[HALO - 1:]
                        li_ = hi_ref[j, before, :].astype(f32)[HALO - 1:]
                    hpr = jnp.where(first, lr_, pltpu.roll(hr_ref[j, rows, :].astype(f32), 1, 0))
                    hpi = jnp.where(first, li_, pltpu.roll(hi_ref[j, rows, :].astype(f32), 1, 0))
                    gr, gi = wr[j, rows, :], wi[j, rows, :]
                    dar += jnp.sum(hpr * gr + hpi * gi, axis=0, keepdims=True)
                    dai += jnp.sum(hpr * gi - hpi * gr, axis=0, keepdims=True)
                dar_ref[j] = dar
                dai_ref[j] = dai

        def store(k, c):
            rows = pl.ds(pl.multiple_of(k * CR, CR), CR)
            for j in range(nb):
                or_ref[j, rows, :] = wr[j, rows, :].astype(bf16)
                oi_ref[j, rows, :] = wi[j, rows, :].astype(bf16)
            return c
        lax.fori_loop(0, S // CR, store, 0)

    slab = jax.ShapeDtypeStruct((B, NB, S, LANES), bf16)
    sspec = pl.BlockSpec((None, nb, S, LANES), lambda b, j: (b, j, 0, 0))
    aspec = pl.BlockSpec((nb, 1, LANES), lambda b, j: (j, 0, 0))
    in_specs = [sspec, sspec, aspec, aspec]
    out_specs = [sspec, sspec]
    out_shape = [slab, slab]
    args = [xr, xi, ar, ai]
    if with_da:
        in_specs += [sspec, sspec]
        args += [hr, hi]
        dspec = pl.BlockSpec((None, nb, 1, LANES), lambda b, j: (b, j, 0, 0))
        out_specs += [dspec, dspec]
        out_shape += [jax.ShapeDtypeStruct((B, NB, 1, LANES), f32)] * 2
    work = pltpu.VMEM((nb, S, LANES), f32)
    table = pltpu.VMEM((nb, L, LANES), f32)
    return _call(body, name, (B, NB // nb), in_specs, out_specs, out_shape, args,
                 scratch=[work, work, work, work, table, table], sem=("parallel", "parallel"), phase=phase)


def _s5_out_parts(hr_ref, hi_ref, u_ref, cr_ref, ci_ref, d_ref, wg_ref, bg_ref, NB):
    hcr = _slab_cat(hr_ref, NB).astype(bf16)
    hci = _slab_cat(hi_ref, NB).astype(bf16)
    u = u_ref[...].astype(f32)
    y2 = _dot(hcr, cr_ref[...]) - _dot(hci, ci_ref[...]) + d_ref[...] * u
    yg = _gelu(y2)
    s = jax.nn.sigmoid(_dot(yg.astype(bf16), wg_ref[...]) + bg_ref[...])
    return hcr, hci, u, y2, yg, s


def _s5_out_specs(B, S, NB, NS, SW, tm):
    sspec = pl.BlockSpec((None, NB, tm, LANES), lambda b, i: (b, 0, i, 0))
    full = lambda r, c: pl.BlockSpec((r, c), lambda b, i: (0, 0))
    return sspec, [sspec, sspec, pl.BlockSpec((None, tm, SW), lambda b, i: (b, i, 0)), full(NS, SW), full(NS, SW),
                   full(1, SW), full(SW, SW), full(1, SW)]


def _s5_out(hr, hi, p3, cbr, cbi, dsk, wglu, bglu, tm):
    B, NB, S, _ = hr.shape
    NS, SW = cbr.shape

    def body(hr_ref, hi_ref, u_ref, cr_ref, ci_ref, d_ref, wg_ref, bg_ref, o_ref):
        _, _, _, _, yg, s = _s5_out_parts(hr_ref, hi_ref, u_ref, cr_ref, ci_ref, d_ref, wg_ref, bg_ref, NB)
        o_ref[...] = (yg * s).astype(bf16)

    _, in_specs = _s5_out_specs(B, S, NB, NS, SW, tm)
    return pl.pallas_call(
        body, name="s5_out", grid=(B, S // tm), in_specs=in_specs,
        out_specs=pl.BlockSpec((None, tm, SW), lambda b, i: (b, i, 0)),
        out_shape=jax.ShapeDtypeStruct((B, S, SW), bf16),
        compiler_params=_params(("parallel", "parallel")))(hr, hi, p3, cbr, cbi, dsk, wglu, bglu)


def _s5_out_bwd(hr, hi, p3, dmix3, cbr, cbi, dsk, wglu, bglu, tm):
    B, NB, S, _ = hr.shape
    NS, SW = cbr.shape

    def body(hr_ref, hi_ref, u_ref, cr_ref, ci_ref, d_ref, wg_ref, bg_ref, da_ref,
             dhr_ref, dhi_ref, du_ref, dcr_ref, dci_ref, dd_ref, dwg_ref, dbg_ref):
        hcr, hci, u, y2, yg, s = _s5_out_parts(hr_ref, hi_ref, u_ref, cr_ref, ci_ref, d_ref, wg_ref, bg_ref, NB)
        da = da_ref[...].astype(f32)
        dz = da * yg * s * (1.0 - s)
        dzb = dz.astype(bf16)
        dyg = da * s + _dg(dzb, wg_ref[...], NT)
        dy2 = dyg * _gelu_grad(y2)
        dyb = dy2.astype(bf16)

        @pl.when((pl.program_id(0) == 0) & (pl.program_id(1) == 0))
        def _():
            for r in (dcr_ref, dci_ref, dd_ref, dwg_ref, dbg_ref):
                r[...] = jnp.zeros_like(r)
        dwg_ref[...] += _dg(yg.astype(bf16), dzb, TN)
        dbg_ref[...] += jnp.sum(dz, axis=0, keepdims=True)
        dd_ref[...] += jnp.sum(dy2 * u, axis=0, keepdims=True)
        dcr_ref[...] += _dg(hcr, dyb, TN)
        dci_ref[...] -= _dg(hci, dyb, TN)
        du_ref[...] = dy2 * d_ref[...]
        dhr = _dg(dyb, cr_ref[...], NT)
        dhi = _dg(dyb, ci_ref[...], NT)
        for j in range(NB):
            dhr_ref[j] = dhr[:, j * LANES:(j + 1) * LANES].astype(bf16)
            dhi_ref[j] = (-dhi[:, j * LANES:(j + 1) * LANES]).astype(bf16)

    sspec, in_specs = _s5_out_specs(B, S, NB, NS, SW, tm)
    in_specs = in_specs + [pl.BlockSpec((None, tm, SW), lambda b, i: (b, i, 0))]
    full = lambda r, c: pl.BlockSpec((r, c), lambda b, i: (0, 0))
    slab = jax.ShapeDtypeStruct((B, NB, S, LANES), bf16)
    mat = lambda r, c: jax.ShapeDtypeStruct((r, c), f32)
    return pl.pallas_call(
        body, name="s5_out_bwd", grid=(B, S // tm), in_specs=in_specs,
        out_specs=[sspec, sspec, pl.BlockSpec((None, tm, SW), lambda b, i: (b, i, 0)), full(NS, SW), full(NS, SW),
                   full(1, SW), full(SW, SW), full(1, SW)],
        out_shape=[slab, slab, jax.ShapeDtypeStruct((B, S, SW), f32), mat(NS, SW), mat(NS, SW), mat(1, SW),
                   mat(SW, SW), mat(1, SW)],
        compiler_params=_params(("arbitrary", "arbitrary")))(hr, hi, p3, cbr, cbi, dsk, wglu, bglu, dmix3)


def _s5_in_bwd(gr, gi, p3, bbd, du_skip, duv3, tm):
    B, NB, S, _ = gr.shape
    SW, NS2 = bbd.shape
    PW = SW + duv3.shape[2]

    def body(gr_ref, gi_ref, u_ref, b_ref, ds_ref, duv_ref, dp_ref, db_ref):
        g = jnp.concatenate([_slab_cat(gr_ref, NB), _slab_cat(gi_ref, NB)], axis=1).astype(bf16)
        du = _dg(g, b_ref[...], NT) + ds_ref[...]
        dp_ref[:, :SW] = du.astype(bf16)
        dp_ref[:, SW:] = duv_ref[...]

        @pl.when((pl.program_id(0) == 0) & (pl.program_id(1) == 0))
        def _():
            db_ref[...] = jnp.zeros_like(db_ref)
        db_ref[...] += _dg(u_ref[...], g, TN)

    sspec = pl.BlockSpec((None, NB, tm, LANES), lambda b, i: (b, 0, i, 0))
    row = lambda c: pl.BlockSpec((None, tm, c), lambda b, i: (b, i, 0))
    return pl.pallas_call(
        body, name="s5_in_bwd", grid=(B, S // tm),
        in_specs=[sspec, sspec, row(SW), pl.BlockSpec((SW, NS2), lambda b, i: (0, 0)), row(SW), row(PW - SW)],
        out_specs=[row(PW), pl.BlockSpec((SW, NS2), lambda b, i: (0, 0))],
        out_shape=[jax.ShapeDtypeStruct((B, S, PW), bf16), jax.ShapeDtypeStruct((SW, NS2), f32)],
        compiler_params=_params(("arbitrary", "arbitrary")))(gr, gi, p3, bbd, du_skip, duv3)


BIG = ['ev_w_in', 'ev_w_out', 'od_w_in', 'od_w_out', 'ffn_w_up', 'ffn_w_down']
ANY = pl.BlockSpec(memory_space=pl.ANY)


def _rtile(rows, mult):
    best = None
    for d in range(mult, min(rows, 512) + 1, mult):
        if rows % d == 0:
            best = d
    assert best is not None, (rows, mult)
    return best


def _pair_sum(name, g, recv, c_idx, out_dtype):
    NCH, R, W = g.shape
    HALF_W = W // 2
    tr = _rtile(R, 16)

    def body(c_ref, a_ref, b_ref, o_ref):
        o_ref[...] = (a_ref[...] + b_ref[...]).astype(out_dtype)

    return pl.pallas_call(
        body, name=name,
        grid_spec=pltpu.PrefetchScalarGridSpec(
            num_scalar_prefetch=1, grid=(NCH, R // tr),
            in_specs=[pl.BlockSpec((None, tr, HALF_W), lambda j, i, c: (j, i, c[0])),
                      pl.BlockSpec((None, tr, HALF_W), lambda j, i, c: (j, i, 0))],
            out_specs=pl.BlockSpec((None, tr, HALF_W), lambda j, i, c: (j, i, 0))),
        out_shape=jax.ShapeDtypeStruct((NCH, R, HALF_W), out_dtype),
        compiler_params=_params(("parallel", "parallel")))(c_idx, g, recv)


def _chip_sum(name, r3, h, k_idx):
    NCH, R, Wh = r3.shape
    tr = _rtile(R, 16)

    def body(k_ref, a_ref, own_ref, o_ref):
        own = own_ref[...].astype(f32)
        t = [jnp.where(k_ref[0] == s, own, a_ref[s].astype(f32)) for s in range(NCH)]
        o_ref[...] = ((t[0] + t[1]) + t[2]) + t[3]

    return pl.pallas_call(
        body, name=name,
        grid_spec=pltpu.PrefetchScalarGridSpec(
            num_scalar_prefetch=1, grid=(R // tr,),
            in_specs=[pl.BlockSpec((NCH, tr, Wh), lambda i, k: (0, i, 0)),
                      pl.BlockSpec((None, tr, Wh), lambda i, k: (k[0], i, 0))],
            out_specs=pl.BlockSpec((tr, Wh), lambda i, k: (i, 0))),
        out_shape=jax.ShapeDtypeStruct((R, Wh), f32),
        compiler_params=_params(("parallel",)))(k_idx, r3, h)


def _adam_math(gg, w, m, v):
    nm = ADAM_B1 * m + (1.0 - ADAM_B1) * gg
    nv = ADAM_B2 * v + (1.0 - ADAM_B2) * jnp.square(gg)
    m_hat = nm / (1.0 - ADAM_B1 ** ADAM_STEP)
    v_hat = nv / (1.0 - ADAM_B2 ** ADAM_STEP)
    return -ADAM_LR * (m_hat / (jnp.sqrt(v_hat) + ADAM_EPS) + ADAM_WD * w), nm, nv


def _adamw(name, mine, theirs, c_idx, w, m, v, lead, transposed, prev=None):
    L, R, W = w.shape
    if transposed:
        bw = LANES if W % LANES == 0 else W
        gspec = pl.BlockSpec((bw, R // 2), lambda i, hf, c: (i, 0))
        wspec = pl.BlockSpec((None, R // 2, bw), lambda i, hf, c: (lead, hf, i))
        grid = (W // bw, 2)
    else:
        tr = _rtile(R, SUBLANES)
        gspec = pl.BlockSpec((tr, W // 2), lambda i, hf, c: (i, 0))
        wspec = pl.BlockSpec((None, tr, W // 2), lambda i, hf, c: (lead, i, hf))
        grid = (R // tr, 2)

    def body(c_ref, a_ref, b_ref, w_ref, m_ref, v_ref, *rest):
        go_ref, d_ref, nm_ref, nv_ref = rest[-4:]
        gg = jnp.where(pl.program_id(1) == c_ref[0], a_ref[...], b_ref[...])
        if transposed:
            gg = gg.T
        d, nm, nv = _adam_math(gg, w_ref[...], m_ref[...], v_ref[...])
        go_ref[...] = gg
        d_ref[...] = d
        nm_ref[...] = nm
        nv_ref[...] = nv

    in_specs = [gspec, gspec, wspec, wspec, wspec]
    args, aliases = [c_idx, mine, theirs, w, m, v], {}
    if prev is not None:
        in_specs += [ANY] * 4
        args += list(prev)
        aliases = {6: 0, 7: 1, 8: 2, 9: 3}
    s = jax.ShapeDtypeStruct((L, R, W), f32)
    return pl.pallas_call(
        body, name=name,
        grid_spec=pltpu.PrefetchScalarGridSpec(num_scalar_prefetch=1, grid=grid, in_specs=in_specs,
                                               out_specs=[wspec] * 4),
        out_shape=[s, s, s, s], input_output_aliases=aliases,
        compiler_params=_params(("parallel", "arbitrary")))(*args)


def _adamw_small(gs, ws, ms, vs):
    n = len(gs)

    def body(*refs):
        for i in range(n):
            d, nm, nv = _adam_math(refs[i][...], refs[n + i][...], refs[2 * n + i][...], refs[3 * n + i][...])
            refs[4 * n + i][...] = d
            refs[5 * n + i][...] = nm
            refs[6 * n + i][...] = nv

    return pl.pallas_call(body, name="adamw_small",
                          out_shape=[jax.ShapeDtypeStruct(t.shape, f32) for t in ws] * 3)(*gs, *ws, *ms, *vs)


def _place():
    x, y, c = lax.axis_index("x"), lax.axis_index("y"), lax.axis_index("c")
    return x, y, c, [(1 - x, y), (x, 1 - y), (1 - x, 1 - y)]


def _gathered_shape(sh, kind):
    if kind == "rows":
        return sh[:-2] + (N_CHIPS * sh[-2], sh[-1])
    if kind == "cols":
        return sh[:-1] + (N_CHIPS * sh[-1],)
    return (N_CHIPS,) + sh


def _place_shard(name, shard, kind, k_idx):
    sh = shard.shape
    r, C = sh[-2], sh[-1]
    L = sh[0] if len(sh) == 3 else 1
    tr = _rtile(r, 16)
    nr = r // tr
    if kind == "rows":
        out3, omap = (L, N_CHIPS * r, C), lambda l, i, k: (l, k[0] * nr + i, 0)
    elif kind == "cols":
        out3, omap = (L, r, N_CHIPS * C), lambda l, i, k: (l, i, k[0])
    else:
        out3, omap = (N_CHIPS, r, C), lambda l, i, k: (k[0], i, 0)

    def body(k_ref, s_ref, o_ref):
        o_ref[...] = s_ref[...]

    out = pl.pallas_call(
        body, name=name,
        grid_spec=pltpu.PrefetchScalarGridSpec(
            num_scalar_prefetch=1, grid=(L, nr),
            in_specs=[pl.BlockSpec((None, tr, C), lambda l, i, k: (l, i, 0))],
            out_specs=pl.BlockSpec((None, tr, C), omap)),
        out_shape=jax.ShapeDtypeStruct(out3, shard.dtype),
        compiler_params=_params(("parallel", "parallel")))(k_idx, shard.reshape(L, r, C))
    return out.reshape(_gathered_shape(sh, kind))


def _gather_phase(shards, fulls, kinds):
    n = len(shards)
    shapes = [s.shape for s in shards]

    def window(ref, a, k, h=None):
        sh, kind = shapes[a], kinds[a]
        r = sh[-2]
        start, size = (0, r) if h is None else (h * (r // 2), r // 2)
        lead = (slice(None),) * (len(sh) - 2)
        if kind == "rows":
            return ref.at[lead + (pl.ds(k * r + start, size), slice(None))]
        if kind == "cols":
            return ref.at[lead + (pl.ds(start, size), pl.ds(pl.multiple_of(k * sh[-1], LANES), sh[-1]))]
        return ref.at[(k,) + lead + (pl.ds(start, size), slice(None))]

    def copies(s_refs, o_refs, sems):
        send_sems, recv_sems = sems
        x, y, c, chips = _place()
        k = 2 * x + y

        def copy(a, j, kk, hh, to, src=None):
            dst = window(o_refs[a], a, kk, hh)
            return pltpu.make_async_remote_copy(
                src_ref=dst if src is None else src, dst_ref=dst, send_sem=send_sems.at[6 * a + j],
                recv_sem=recv_sems.at[6 * a + j], device_id=to, device_id_type=MESH)

        first = []
        for a in range(n):
            r = shapes[a][-2]
            lead = (slice(None),) * (len(shapes[a]) - 2)
            src = s_refs[a].at[lead + (pl.ds(c * (r // 2), r // 2), slice(None))]
            first += [copy(a, j, k, c, (*chip, c), src=src) for j, chip in enumerate(chips)]
        return copy, first, (x, y, c), (x, y, 1 - c), c, chips

    def start(s_refs, o_refs, sems):
        for cp in copies(s_refs, o_refs, sems)[1]:
            cp.start()

    def finish(s_refs, o_refs, sems):
        copy, first, me, sibling, c, chips = copies(s_refs, o_refs, sems)
        passed = []
        for j, (cx, cy) in enumerate(chips):
            for a in range(n):
                copy(a, j, 2 * cx + cy, c, me).wait_recv()
                fwd = copy(a, 3 + j, 2 * cx + cy, c, sibling)
                fwd.start()
                passed.append(fwd)
        for j, (cx, cy) in enumerate(chips):
            for a in range(n):
                copy(a, 3 + j, 2 * cx + cy, 1 - c, me).wait_recv()
        for cp in first + passed:
            cp.wait_send()

    return _Phase(shards, fulls, [jax.ShapeDtypeStruct(f.shape, f.dtype) for f in fulls],
                  [pltpu.SemaphoreType.DMA((6 * n,)), pltpu.SemaphoreType.DMA((6 * n,))], start, finish)


def _swap_phase(gs):
    n = len(gs)

    def copies(g_refs, o_refs, sems):
        send_sems, recv_sems = sems
        x, y, c, _ = _place()
        half = [g.shape[2] // 2 for g in gs]
        return [pltpu.make_async_remote_copy(
            src_ref=g_refs[a].at[:, :, pl.ds(pl.multiple_of((1 - c) * half[a], LANES), half[a])], dst_ref=o_refs[a],
            send_sem=send_sems.at[a], recv_sem=recv_sems.at[a], device_id=(x, y, 1 - c), device_id_type=MESH)
            for a in range(n)]

    def start(g_refs, o_refs, sems):
        for cp in copies(g_refs, o_refs, sems):
            cp.start()

    def finish(g_refs, o_refs, sems):
        for cp in copies(g_refs, o_refs, sems):
            cp.wait()

    return _Phase(gs, [], [jax.ShapeDtypeStruct(g.shape[:2] + (g.shape[2] // 2,), g.dtype) for g in gs],
                  [pltpu.SemaphoreType.DMA((n,)), pltpu.SemaphoreType.DMA((n,))], start, finish)


def _exchange_phase(hs):
    n = len(hs)

    def copies(h_refs, o_refs, sems):
        send_sems, recv_sems = sems
        x, y, c, chips = _place()
        k = 2 * x + y

        def copy(a, j, src_slot, dst_slot):
            cx, cy = chips[j]
            return pltpu.make_async_remote_copy(
                src_ref=h_refs[a].at[src_slot], dst_ref=o_refs[a].at[dst_slot], send_sem=send_sems.at[3 * a + j],
                recv_sem=recv_sems.at[3 * a + j], device_id=(cx, cy, c), device_id_type=MESH)

        sends = [copy(a, j, 2 * cx + cy, k) for a in range(n) for j, (cx, cy) in enumerate(chips)]
        return copy, sends, k, chips

    def start(h_refs, o_refs, sems):
        for cp in copies(h_refs, o_refs, sems)[1]:
            cp.start()

    def finish(h_refs, o_refs, sems):
        copy, sends, k, chips = copies(h_refs, o_refs, sems)
        for a in range(n):
            for j, (cx, cy) in enumerate(chips):
                copy(a, j, k, 2 * cx + cy).wait_recv()
        for cp in sends:
            cp.wait_send()

    return _Phase(hs, [], [jax.ShapeDtypeStruct(h.shape, h.dtype) for h in hs],
                  [pltpu.SemaphoreType.DMA((3 * n,)), pltpu.SemaphoreType.DMA((3 * n,))], start, finish)


def _comm_pair_share(tag, gs):
    n = len(gs)

    def body(*refs):
        g_refs, o_refs, send_sems, recv_sems = refs[:n], refs[n:2 * n], refs[2 * n], refs[2 * n + 1]
        x, y, c, _ = _place()
        cps = [pltpu.make_async_remote_copy(
            src_ref=g_refs[a], dst_ref=o_refs[a], send_sem=send_sems.at[a], recv_sem=recv_sems.at[a],
            device_id=(x, y, 1 - c), device_id_type=MESH) for a in range(n)]
        for cp in cps:
            cp.start()
        for cp in cps:
            cp.wait()

    return pl.pallas_call(
        body, name="comm_pair_share_" + tag, in_specs=[ANY] * n, out_specs=[ANY] * n,
        out_shape=[jax.ShapeDtypeStruct(g.shape, g.dtype) for g in gs],
        scratch_shapes=[pltpu.SemaphoreType.DMA((n,)), pltpu.SemaphoreType.DMA((n,))])(*gs)


def _pad_rows(flat, unit):
    n = flat.shape[-1]
    pad = (-n) % unit
    if pad:
        flat = jnp.pad(flat, [(0, 0)] * (flat.ndim - 1) + [(0, pad)])
    return flat


def _split_chips(full, axis):
    sh = full.shape
    t = full.reshape(sh[:axis] + (N_CHIPS, sh[axis] // N_CHIPS) + sh[axis + 1:])
    return jnp.moveaxis(t, axis, 0).reshape(N_CHIPS, -1)


def _join_chips(stack, shard_shape, axis):
    t = jnp.moveaxis(stack.reshape((N_CHIPS,) + tuple(shard_shape)), 0, axis)
    sh = t.shape
    return t.reshape(sh[:axis] + (sh[axis] * sh[axis + 1],) + sh[axis + 2:])


def _block_diag(blocks):
    G, r, c = blocks.shape
    eye = jnp.eye(G, dtype=blocks.dtype)
    return (blocks[:, :, None, :] * eye[:, None, :, None]).reshape(G * r, G * c)


def _diag_blocks(m, G):
    r, c = m.shape[0] // G, m.shape[1] // G
    idx = jnp.arange(G)
    return m.reshape(G, r, G, c)[idx, :, idx, :]


def _weight_shards(w):
    conv = jnp.concatenate([w[n].reshape(-1) for n in GATHER_F32])
    conv = _pad_rows(conv, 2 * SUBLANES * LANES).reshape(-1, LANES)
    b16 = lambda a: a.astype(bf16)
    return {'ev_w_in': (b16(w['ev_w_in'][0]), "chip"), 'ev_w_out': (b16(w['ev_w_out'][0]), "rows"),
            's5_w_glu': (b16(w['s5_w_glu'][0]), "rows"), 'conv': (conv, "chip"),
            'od_w_in': (b16(w['od_w_in'][0]), "cols"), 'od_w_out': (b16(w['od_w_out'][0]), "rows"),
            'ffn_w_up0': (b16(w['ffn_w_up'][0]), "cols"), 'ffn_w_up1': (b16(w['ffn_w_up'][1]), "cols"),
            'ffn_w_down0': (b16(w['ffn_w_down'][0]), "rows"), 'ffn_w_down1': (b16(w['ffn_w_down'][1]), "rows")}


def kernel(x, mix_norm_g, ffn_norm_g, final_norm_g, ev_w_in, ev_w_out, s5_lam_re, s5_lam_im, s5_log_dt, s5_b_re, s5_b_im, s5_c_re, s5_c_im, s5_d, s5_w_glu, s5_b_glu, gm_w_s, gm_b_s, gm_v_g, od_w_in, od_conv_w, od_conv_b, od_w_out, ffn_w_up, ffn_conv_w, ffn_conv_b, ffn_w_down, loss_target, m_mix_norm_g, m_ffn_norm_g, m_final_norm_g, m_ev_w_in, m_ev_w_out, m_s5_lam_re, m_s5_lam_im, m_s5_log_dt, m_s5_b_re, m_s5_b_im, m_s5_c_re, m_s5_c_im, m_s5_d, m_s5_w_glu, m_s5_b_glu, m_gm_w_s, m_gm_b_s, m_gm_v_g, m_od_w_in, m_od_conv_w, m_od_conv_b, m_od_w_out, m_ffn_w_up, m_ffn_conv_w, m_ffn_conv_b, m_ffn_w_down, v_mix_norm_g, v_ffn_norm_g, v_final_norm_g, v_ev_w_in, v_ev_w_out, v_s5_lam_re, v_s5_lam_im, v_s5_log_dt, v_s5_b_re, v_s5_b_im, v_s5_c_re, v_s5_c_im, v_s5_d, v_s5_w_glu, v_s5_b_glu, v_gm_w_s, v_gm_b_s, v_gm_v_g, v_od_w_in, v_od_conv_w, v_od_conv_b, v_od_w_out, v_ffn_w_up, v_ffn_conv_w, v_ffn_conv_b, v_ffn_w_down):
    loc = dict(locals())
    w = {n: loc[n] for n in WEIGHTS}
    mom = {n: loc["m_" + n] for n in WEIGHTS}
    var = {n: loc["v_" + n] for n in WEIGHTS}

    B, S, D = x.shape
    T = B * S
    SW = s5_d.shape[1]
    G = SW // SSM_GROUP
    NS = G * SSM_STATE
    NB = NS // LANES
    tm = min(512, S)
    tt = min(1024, T)
    c_idx = lax.axis_index("c").astype(jnp.int32).reshape(1)
    k_idx = (2 * lax.axis_index("x") + lax.axis_index("y")).astype(jnp.int32).reshape(1)
    shards = _weight_shards(w)
    placed = {n: _place_shard("place_" + n, s, kd, k_idx) for n, (s, kd) in shards.items()}

    def gather(names):
        return _gather_phase([shards[n][0] for n in names], [placed[n] for n in names], [shards[n][1] for n in names])

    (w_ev_in,) = _run_phase("comm_gather_ev_in", gather(['ev_w_in']))
    w_ev_in = jnp.swapaxes(w_ev_in, 0, 1).reshape(D, -1)

    h0 = x.reshape(T, D)
    (y0, p0), (w_ev_out, w_glu, conv) = _norm_mm("ev_in", h0, mix_norm_g[0], w_ev_in, tm,
                                                 phase=gather(['ev_w_out', 's5_w_glu', 'conv']))
    full, off = {}, 0
    for n in GATHER_F32:
        full[n] = _join_chips(conv.reshape(N_CHIPS, -1)[:, off:off + w[n].size], w[n].shape, SHARD_AXIS[n])
        off += w[n].size
    PW = p0.shape[1]
    p03 = p0.reshape(B, S, PW)
    lr, li, ldt = s5_lam_re[0], s5_lam_im[0], s5_log_dt[0].reshape(G, 1)
    ar, ai, zr, zi = _s5_prep(lr, li, ldt)
    bre = _block_diag(jnp.swapaxes(s5_b_re[0], 1, 2))
    bim = _block_diag(jnp.swapaxes(s5_b_im[0], 1, 2))
    cbr = _block_diag(jnp.swapaxes(s5_c_re[0], 1, 2)).astype(bf16)
    cbi = _block_diag(jnp.swapaxes(s5_c_im[0], 1, 2)).astype(bf16)
    zr_row, zi_row = zr.reshape(1, NS), zi.reshape(1, NS)
    bbd = _s5_bbd(zr_row, zi_row, bre, bim)
    ar_s, ai_s = ar.reshape(NB, 1, LANES), ai.reshape(NB, 1, LANES)
    xr, xi = _s5_in(p03, bbd, SW, tm)
    (hr, hi), (w_up0, w_down0) = _s5_scan("s5_scan", xr, xi, ar_s, ai_s, False,
                                           phase=gather(['ffn_w_up0', 'ffn_w_down0']))
    dsk, bglu = s5_d.reshape(1, SW), s5_b_glu.reshape(1, SW)
    a_out = _s5_out(hr, hi, p03, cbr, cbi, dsk, w_glu, bglu, tm)
    ws, bst, gv = gm_w_s[0], gm_b_s[0].T, gm_v_g.reshape(1, -1)
    mixcat = _gmlp(p0, a_out.reshape(T, SW), ws, bst, gv, SW)
    h1 = _mm_resid("ev_out", mixcat, w_ev_out, h0, tm)

    def ffn_fwd(l, h, w_up, w_down, up_phase=None, down_phase=None):
        res = _norm_mm(f"ffn_up{l}", h, ffn_norm_g[l], w_up, tm, phase=up_phase)
        (z, up), got_up = res if up_phase is not None else (res, None)
        res = _ffn_down(f"ffn_down{l}", up, full['ffn_conv_w'][l], ffn_conv_b[l].reshape(1, -1), w_down, h, S, tm,
                        phase=down_phase)
        (hn, c), got_down = res if down_phase is not None else (res, None)
        return hn, (z, up.reshape(B, S, -1), c.reshape(B, S, -1)), got_up, got_down

    h2, ffn0, (w_up1, w_down1), (w_od_in, w_od_out) = ffn_fwd(
        0, h1, w_up0, w_down0, gather(['ffn_w_up1', 'ffn_w_down1']), gather(['od_w_in', 'od_w_out']))
    w_ups, w_downs = (w_up0, w_up1), (w_down0, w_down1)
    od_cw, od_cb = full['od_conv_w'][0], full['od_conv_b']
    y1, p1 = _norm_mm("od_in", h2, mix_norm_g[1], w_od_in, tm)
    p13 = p1.reshape(B, S, -1)
    sc = _od_act(p13, od_cw, od_cb)
    h3 = _mm_resid("od_out", sc.reshape(T, D), w_od_out, h2, tm)
    h4, ffn1, _, _ = ffn_fwd(1, h3, w_up1, w_down1)

    dh4, dh4b, loss_part, d_final_g = _final_loss(h4, final_norm_g, loss_target.reshape(T, D), tm)
    loss = lax.psum(loss_part[0, 0], ("x", "y", "c"))

    grads = {}

    halves = {}
    chips = lambda g: g.reshape(N_CHIPS, -1, D)

    def pair_sums(names, parts, recv):
        return [_pair_sum(f"pair_sum_{n}", g, r, c_idx, f32 if n == "small" else bf16)
                for n, g, r in zip(names, parts, recv)]

    def reduce_end(tag, names, hsum, r3):
        mine = [_chip_sum(f"chip_sum_{n}", r, h, k_idx) for n, r, h in zip(names, r3, hsum)]
        theirs = _comm_pair_share(tag, mine)
        halves.update({n: (a, b) for n, a, b in zip(names, mine, theirs)})

    def ffn_bwd(l, dh, dhb, h_in, saved, phase=None, swap=False):
        z, up3, c3 = saved
        w_down, w_up = w_downs[l], w_ups[l]
        da = _mm_nt(f"ffn_down_bwd{l}", dhb, w_down, tm)
        res = _ffn_act_bwd(f"ffn_act_bwd{l}", up3, c3, da.reshape(B, S, -1), full['ffn_conv_w'][l], phase=phase)
        (act, dg3, dv3, dcwg, dcwv, dcbg, dcbv), got = res if phase is not None else (res, None)
        g_down = _mm_tn(f"ffn_down_dw{l}", act.reshape(T, -1), dhb, tt)
        dupg, dupv = dg3.reshape(T, -1), dv3.reshape(T, -1)
        F = dupg.shape[1]
        g_up = _mm_tn(f"ffn_up_dw{l}_gate", dupg, z, tt, rows=2 * F)
        g_up = _mm_tn(f"ffn_up_dw{l}_val", dupv, z, tt, rows=2 * F, row_off=F, prev=g_up)
        parts = [chips(g_down), chips(g_up)]
        res = _mm_nt_normbwd(f"ffn_up_bwd{l}", [dupg, dupv], w_up, h_in, ffn_norm_g[l], dh, tm,
                             phase=_swap_phase(parts) if swap else None)
        (dh_new, dhb_new, dg), recv = res if swap else (res, None)
        F = dg3.shape[2]
        dcw = jnp.concatenate([dcwg[:, :F], dcwv[:, :F]], axis=1)
        dcb = jnp.concatenate([dcbg[:, :F], dcbv[:, :F]], axis=1)
        return dh_new, dhb_new, g_down, g_up, dcw, dcb[0], dg[0], got, parts, recv

    dh3, dh3b, gd1, gu1, gcw1, gcb1, gng1, _, _, _ = ffn_bwd(1, dh4, dh4b, h3, ffn1)
    dsc = _mm_nt("od_out_bwd", dh3b, w_od_out, tm)
    g_od_out = _mm_tn("od_out_dw", sc.reshape(T, D), dh3b, tt)
    dbg3, dcg3, dhx3, d_od_cw, d_od_cb = _od_act_bwd(p13, dsc.reshape(B, S, D), od_cw, od_cb)
    dp1 = [t.reshape(T, D) for t in (dbg3, dcg3, dhx3)]
    g_od_in = None
    for i, piece in enumerate(dp1):
        g_od_in = _mm_tn(f"od_in_dw{i}", piece, y1, tt, rows=3 * D, row_off=i * D, prev=g_od_in)
    grads['od_conv_w'] = d_od_cw[None]
    grads['od_conv_b'] = d_od_cb
    layer1 = ['ffn_w_down1', 'ffn_w_up1', 'od_w_out', 'od_w_in']
    parts1 = [chips(g) for g in (gd1, gu1, g_od_out, g_od_in)]
    (dh2, dh2b, gmix1), recv1 = _mm_nt_normbwd("od_in_bwd", dp1, w_od_in, h2, mix_norm_g[1], dh3, tm,
                                               phase=_swap_phase(parts1))
    hsum1 = pair_sums(layer1, parts1, recv1)
    dh1, dh1b, gd0, gu0, gcw0, gcb0, gng0, r3, parts0, recv0 = ffn_bwd(
        0, dh2, dh2b, h1, ffn0, phase=_exchange_phase(hsum1), swap=True)
    reduce_end("layer1", layer1, hsum1, r3)
    ffn0_names = ['ffn_w_down0', 'ffn_w_up0']
    hsum0 = pair_sums(ffn0_names, parts0, recv0)
    grads['ffn_conv_w'] = jnp.stack([gcw0, gcw1])
    grads['ffn_conv_b'] = jnp.stack([gcb0, gcb1])
    grads['ffn_norm_g'] = jnp.stack([gng0, gng1])
    grads['final_norm_g'] = d_final_g[0]

    dmix = _mm_nt("ev_out_bwd", dh1b, w_ev_out, tm)
    g_ev_out = _mm_tn("ev_out_dw", mixcat, dh1b, tt)
    part_evo = [chips(g_ev_out)]
    (duv, d_ws, d_bs, d_gv), recv_evo = _gmlp_bwd(p0, dmix, ws, bst, gv, SW, phase=_swap_phase(part_evo))
    hsum0 = hsum0 + pair_sums(['ev_w_out'], part_evo, recv_evo)
    ffn0_names = ffn0_names + ['ev_w_out']
    grads['gm_w_s'] = d_ws[None]
    grads['gm_b_s'] = d_bs[:, :, 0][None]
    grads['gm_v_g'] = d_gv
    dhr, dhi, du_skip, d_cbr, d_cbi, d_dsk, d_wglu, d_bglu = _s5_out_bwd(
        hr, hi, p03, dmix.reshape(B, S, D), cbr, cbi, dsk, w_glu, bglu, tm)
    grads['s5_c_re'] = jnp.swapaxes(_diag_blocks(d_cbr, G), 1, 2)[None]
    grads['s5_c_im'] = jnp.swapaxes(_diag_blocks(d_cbi, G), 1, 2)[None]
    grads['s5_d'] = d_dsk
    grads['s5_w_glu'] = d_wglu[None]
    grads['s5_b_glu'] = d_bglu
    (gr, gi, dar, dai), r3 = _s5_scan("s5_rscan", dhr, dhi, ar_s, ai_s, True, hr, hi, phase=_exchange_phase(hsum0))
    reduce_end("ffn0", ffn0_names, hsum0, r3)
    dp03, d_bbd = _s5_in_bwd(gr, gi, p03, bbd, du_skip, duv.reshape(B, S, -1), tm)
    d_bre, d_bim, d_zr, d_zi = _s5_bbd_bwd(d_bbd, zr_row, zi_row, bre, bim)
    grads['s5_b_re'] = jnp.swapaxes(_diag_blocks(d_bre, G), 1, 2)[None]
    grads['s5_b_im'] = jnp.swapaxes(_diag_blocks(d_bim, G), 1, 2)[None]
    shp = (-1, G, SSM_STATE)
    d_lr, d_li, d_ldt = _s5_prep_bwd(lr, li, ldt, dar.reshape(shp), dai.reshape(shp), d_zr.reshape(shp),
                                     d_zi.reshape(shp))
    grads['s5_lam_re'] = d_lr[None]
    grads['s5_lam_im'] = d_li[None]
    grads['s5_log_dt'] = d_ldt.reshape(1, G)
    dp0 = dp03.reshape(T, PW)
    g_ev_in = _mm_tn("ev_in_dw", dp0, y0, tt)
    grad_x, _, gmix0 = _mm_nt_normbwd("ev_in_bwd", [dp0], w_ev_in, h0, mix_norm_g[0], dh1, tm)
    grads['mix_norm_g'] = jnp.concatenate([gmix0, gmix1], axis=0)

    small = [n for n in WEIGHTS if n not in BIG]
    segs = []
    for n in small:
        gfull = grads[n].astype(f32)
        if n in SHARD_AXIS:
            segs.append(_split_chips(gfull, SHARD_AXIS[n]))
        else:
            segs.append(jnp.broadcast_to(gfull.reshape(1, -1), (N_CHIPS, gfull.size)))
    unit = 2 * SUBLANES * D
    gsmall = _pad_rows(jnp.concatenate(segs, axis=1), unit).reshape(N_CHIPS, -1, D)
    mixer0 = ['ev_w_in', 'small']
    parts = [chips(g_ev_in), gsmall]
    hsum = pair_sums(mixer0, parts, _run_phase("comm_pair_swap_mixer0", _swap_phase(parts)))
    reduce_end("mixer0", mixer0, hsum, _run_phase("comm_exchange_mixer0", _exchange_phase(hsum)))

    out_g, out_d, out_m, out_v = {}, {}, {}, {}

    def update(n, key, lead, transposed, prev=None):
        res = _adamw(f"adamw_{key}", *halves[key], c_idx, w[n], mom[n], var[n], lead, transposed, prev)
        out_g[n], out_d[n], out_m[n], out_v[n] = res
        return res

    update('ev_w_in', 'ev_w_in', 0, True)
    update('ev_w_out', 'ev_w_out', 0, False)
    update('od_w_in', 'od_w_in', 0, True)
    update('od_w_out', 'od_w_out', 0, False)
    update('ffn_w_up', 'ffn_w_up0', 0, True, prev=update('ffn_w_up', 'ffn_w_up1', 1, True))
    update('ffn_w_down', 'ffn_w_down0', 0, False, prev=update('ffn_w_down', 'ffn_w_down1', 1, False))

    mine, theirs = halves['small']
    first = lax.axis_index("c") == 0
    flat = jnp.concatenate([jnp.where(first, mine, theirs), jnp.where(first, theirs, mine)], axis=1).reshape(-1)
    off = 0
    for n in small:
        out_g[n] = flat[off:off + w[n].size].reshape(w[n].shape)
        off += w[n].size
    res = _adamw_small([out_g[n] for n in small], [w[n] for n in small], [mom[n] for n in small],
                       [var[n] for n in small])
    for i, n in enumerate(small):
        out_d[n], out_m[n], out_v[n] = res[i], res[len(small) + i], res[2 * len(small) + i]

    return (loss, grad_x.reshape(B, S, D), *[out_g[n] for n in WEIGHTS], *[out_d[n] for n in WEIGHTS],
            *[out_m[n] for n in WEIGHTS], *[out_v[n] for n in WEIGHTS])
```

```python
import functools
import math

import jax
import jax.numpy as jnp
from jax import lax
from jax.experimental import pallas as pl
from jax.experimental.pallas import tpu as pltpu

f32 = jnp.float32
bf16 = jnp.bfloat16
MESH = pl.DeviceIdType.MESH

SSM_GROUP = 16
SSM_STATE = 64
GMLP_HEAD = 128
CHUNK = 128
EPS = 1e-6
LAMBDA_RE_MAX = -1e-4
ADAM_LR, ADAM_B1, ADAM_B2, ADAM_EPS, ADAM_WD, ADAM_STEP = 0.001, 0.9, 0.999, 1e-08, 0.01, 10

LANES = 128
SUBLANES = 8
NSUB = 32
HALO = 16
VMEM_LIMIT = 56 * 1024 * 1024
N_CHIPS = 4

WEIGHTS = ['mix_norm_g', 'ffn_norm_g', 'final_norm_g', 'ev_w_in', 'ev_w_out', 's5_lam_re', 's5_lam_im', 's5_log_dt',
           's5_b_re', 's5_b_im', 's5_c_re', 's5_c_im', 's5_d', 's5_w_glu', 's5_b_glu', 'gm_w_s', 'gm_b_s', 'gm_v_g',
           'od_w_in', 'od_conv_w', 'od_conv_b', 'od_w_out', 'ffn_w_up', 'ffn_conv_w', 'ffn_conv_b', 'ffn_w_down']
SHARD_AXIS = {'ev_w_in': 2, 'ev_w_out': 1, 's5_w_glu': 1, 'od_w_in': 2, 'od_conv_w': 2, 'od_conv_b': 1, 'od_w_out': 1,
              'ffn_w_up': 2, 'ffn_conv_w': 2, 'ffn_w_down': 1}
GATHER_F32 = ['od_conv_w', 'od_conv_b', 'ffn_conv_w']

_GELU_K0 = math.sqrt(2.0 / math.pi)
_GELU_K1 = 0.044715
NT = (((1,), (1,)), ((), ()))
TN = (((0,), (0,)), ((), ()))


def _pick(n, cap):
    if n <= cap:
        return n
    best = None
    for d in range(LANES, cap + 1, LANES):
        if n % d == 0:
            best = d
    assert best is not None, (n, cap)
    return best


def _params(sem=None):
    return pltpu.CompilerParams(dimension_semantics=sem, vmem_limit_bytes=VMEM_LIMIT)


class _Phase:
    def __init__(self, ins, inplace, outs, sems, start, finish):
        self.ins, self.inplace, self.outs, self.sems = list(ins), list(inplace), list(outs), list(sems)
        self.start, self.finish = start, finish


def _call(body, name, grid, in_specs, out_specs, out_shape, args, scratch=(), sem=None, phase=None):
    if phase is None:
        return pl.pallas_call(body, name=name, grid=grid, in_specs=in_specs, out_specs=out_specs, out_shape=out_shape,
                              scratch_shapes=list(scratch), compiler_params=_params(sem))(*args)
    any_spec = pl.BlockSpec(memory_space=pl.ANY)
    n_in, n_out, n_scr = len(args), len(out_shape), len(scratch)
    p_in = phase.ins + phase.inplace
    ci, co = len(p_in), len(phase.outs)

    def wrapped(*refs):
        ins, cins = refs[:n_in], refs[n_in:n_in + len(phase.ins)]
        b = n_in + ci
        outs, couts = refs[b:b + n_out], refs[b + n_out:b + n_out + co]
        d = b + n_out + co
        scr, csem = refs[d:d + n_scr], refs[d + n_scr:]
        ids = [pl.program_id(i) for i in range(len(grid))]
        first = functools.reduce(jnp.logical_and, [i == 0 for i in ids])
        last = functools.reduce(jnp.logical_and, [i == g - 1 for i, g in zip(ids, grid)])

        @pl.when(first)
        def _():
            phase.start(cins, couts, csem)
        body(*ins, *outs, *scr)

        @pl.when(last)
        def _():
            phase.finish(cins, couts, csem)

    res = pl.pallas_call(
        wrapped, name=name, grid=grid, in_specs=list(in_specs) + [any_spec] * ci,
        out_specs=list(out_specs) + [any_spec] * co, out_shape=list(out_shape) + phase.outs,
        scratch_shapes=list(scratch) + phase.sems,
        input_output_aliases={n_in + len(phase.ins) + i: n_out + i for i in range(len(phase.inplace))},
        compiler_params=_params(tuple("arbitrary" for _ in grid)))(*args, *p_in)
    return res[:n_out], res[n_out:]


def _run_phase(name, phase):
    any_spec = pl.BlockSpec(memory_space=pl.ANY)
    ni, ci, co = len(phase.ins), len(phase.ins) + len(phase.inplace), len(phase.outs)

    def body(*refs):
        cins, couts, csem = refs[:ni], refs[ci:ci + co], refs[ci + co:]
        phase.start(cins, couts, csem)
        phase.finish(cins, couts, csem)

    return pl.pallas_call(
        body, name=name, in_specs=[any_spec] * ci, out_specs=[any_spec] * co, out_shape=phase.outs,
        scratch_shapes=phase.sems, input_output_aliases={ni + i: i for i in range(len(phase.inplace))})(
            *phase.ins, *phase.inplace)


def _gelu(x):
    return 0.5 * x * (1.0 + jnp.tanh(_GELU_K0 * (x + _GELU_K1 * x * x * x)))


def _gelu_grad(x):
    t = jnp.tanh(_GELU_K0 * (x + _GELU_K1 * x * x * x))
    return 0.5 * (1.0 + t) + 0.5 * x * (1.0 - t * t) * _GELU_K0 * (1.0 + 3.0 * _GELU_K1 * x * x)


def _rms_stats(x):
    r = lax.rsqrt(jnp.mean(x * x, axis=-1, keepdims=True) + EPS)
    return x * r, r


def _rms_bwd(dy, xh, r, g):
    dxh = dy * g
    dx = r * (dxh - xh * jnp.mean(dxh * xh, axis=-1, keepdims=True))
    return dx, jnp.sum(dy * xh, axis=0, keepdims=True)


def _dot(a, b):
    return jnp.dot(a, b, preferred_element_type=f32)


def _dg(a, b, dims):
    return lax.dot_general(a, b, dims, preferred_element_type=f32)


def _row_fold(z):
    return z.reshape(z.shape[0] // SUBLANES, SUBLANES, z.shape[1]).sum(axis=0)


def _norm_mm(name, h, g, w, tm, phase=None):
    T, D = h.shape
    N = w.shape[1]
    nc = _pick(N, 512)

    def body(h_ref, g_ref, w_ref, y_ref, o_ref):
        xh, _ = _rms_stats(h_ref[...])
        y = (xh * g_ref[...]).astype(bf16)
        y_ref[...] = y
        for j in range(N // nc):
            o_ref[:, j * nc:(j + 1) * nc] = _dot(y, w_ref[:, j * nc:(j + 1) * nc]).astype(bf16)

    return _call(
        body, name, (T // tm,),
        [pl.BlockSpec((tm, D), lambda i: (i, 0)), pl.BlockSpec((1, D), lambda i: (0, 0)),
         pl.BlockSpec((D, N), lambda i: (0, 0))],
        [pl.BlockSpec((tm, D), lambda i: (i, 0)), pl.BlockSpec((tm, N), lambda i: (i, 0))],
        [jax.ShapeDtypeStruct((T, D), bf16), jax.ShapeDtypeStruct((T, N), bf16)],
        [h, g.reshape(1, D), w], sem=("parallel",), phase=phase)


def _mm_resid(name, a, w, resid, tm):
    T, K = a.shape
    N = w.shape[1]

    def body(a_ref, w_ref, r_ref, o_ref):
        o_ref[...] = r_ref[...] + _dot(a_ref[...], w_ref[...])

    return pl.pallas_call(
        body, name=name, grid=(T // tm,),
        in_specs=[pl.BlockSpec((tm, K), lambda i: (i, 0)), pl.BlockSpec((K, N), lambda i: (0, 0)),
                  pl.BlockSpec((tm, N), lambda i: (i, 0))],
        out_specs=pl.BlockSpec((tm, N), lambda i: (i, 0)),
        out_shape=jax.ShapeDtypeStruct((T, N), f32),
        compiler_params=_params(("parallel",)))(a, w, resid)


def _mm_nt(name, dy, w, tm):
    T, N = dy.shape
    K = w.shape[0]
    kc = _pick(K, 512)

    def body(d_ref, w_ref, o_ref):
        d = d_ref[...].astype(bf16)
        for j in range(K // kc):
            o_ref[:, j * kc:(j + 1) * kc] = _dg(d, w_ref[j * kc:(j + 1) * kc, :], NT).astype(bf16)

    return pl.pallas_call(
        body, name=name, grid=(T // tm,),
        in_specs=[pl.BlockSpec((tm, N), lambda i: (i, 0)), pl.BlockSpec((K, N), lambda i: (0, 0))],
        out_specs=pl.BlockSpec((tm, K), lambda i: (i, 0)),
        out_shape=jax.ShapeDtypeStruct((T, K), bf16),
        compiler_params=_params(("parallel",)))(dy, w)


def _mm_nt_normbwd(name, dys, w, h, g, dh_in, tm, phase=None):
    n = len(dys)
    T = dys[0].shape[0]
    D = w.shape[0]
    widths = [d.shape[1] for d in dys]
    offs = [sum(widths[:i]) for i in range(n)]

    def body(*refs):
        d_refs = refs[:n]
        w_ref, h_ref, g_ref, dh_ref, o_ref, ob_ref, dg_ref = refs[n:]
        dz = _dg(d_refs[0][...], w_ref[:, :widths[0]], NT)
        for i in range(1, n):
            dz += _dg(d_refs[i][...], w_ref[:, offs[i]:offs[i] + widths[i]], NT)
        xh, r = _rms_stats(h_ref[...])
        dx, dg = _rms_bwd(dz, xh, r, g_ref[...])
        out = dh_ref[...] + dx
        o_ref[...] = out
        ob_ref[...] = out.astype(bf16)

        @pl.when(pl.program_id(0) == 0)
        def _():
            dg_ref[...] = jnp.zeros_like(dg_ref)
        dg_ref[...] += dg

    row = lambda c: pl.BlockSpec((tm, c), lambda i: (i, 0))
    return _call(
        body, name, (T // tm,),
        [row(c) for c in widths] + [pl.BlockSpec((D, sum(widths)), lambda i: (0, 0)), row(D),
                                    pl.BlockSpec((1, D), lambda i: (0, 0)), row(D)],
        [row(D), row(D), pl.BlockSpec((1, D), lambda i: (0, 0))],
        [jax.ShapeDtypeStruct((T, D), f32), jax.ShapeDtypeStruct((T, D), bf16), jax.ShapeDtypeStruct((1, D), f32)],
        [*dys, w, h, g.reshape(1, D), dh_in], sem=("arbitrary",), phase=phase)


def _mm_tn(name, a, b, tt, rows=None, row_off=0, prev=None):
    T, K = a.shape
    N = b.shape[1]
    rows = K if rows is None else rows
    tk = _pick(K, 1408)
    tn = _pick(N, 1024)
    assert row_off % tk == 0
    kb = row_off // tk

    def body(a_ref, b_ref, *rest):
        o_ref = rest[-1]

        @pl.when(pl.program_id(2) == 0)
        def _():
            o_ref[...] = jnp.zeros_like(o_ref)
        o_ref[...] += _dg(a_ref[...], b_ref[...], TN)

    in_specs = [pl.BlockSpec((tt, tk), lambda k, n, t: (t, k)), pl.BlockSpec((tt, tn), lambda k, n, t: (t, n))]
    args, aliases = [a, b], {}
    if prev is not None:
        in_specs.append(ANY)
        args.append(prev)
        aliases = {2: 0}
    return pl.pallas_call(
        body, name=name, grid=(K // tk, N // tn, T // tt), in_specs=in_specs,
        out_specs=pl.BlockSpec((tk, tn), lambda k, n, t: (k + kb, n)),
        out_shape=jax.ShapeDtypeStruct((rows, N), f32), input_output_aliases=aliases,
        compiler_params=_params(("parallel", "parallel", "arbitrary")))(*args)


def _loss_head(h, g_ref, t_ref, dh_ref, dhb_ref, loss_ref, dg_ref):
    D = h.shape[1]
    xh, r = _rms_stats(h)
    gg = g_ref[...]
    diff = xh * gg - t_ref[...]
    dx, dg = _rms_bwd(diff * (1.0 / D), xh, r, gg)
    dh_ref[...] = dx
    dhb_ref[...] = dx.astype(bf16)

    @pl.when(pl.program_id(0) == 0)
    def _():
        dg_ref[...] = jnp.zeros_like(dg_ref)
        loss_ref[...] = jnp.zeros_like(loss_ref)
    dg_ref[...] += dg
    loss_ref[...] += (0.5 / D) * jnp.sum(jnp.sum(diff * diff, axis=1, keepdims=True), axis=0, keepdims=True)


def _taps(load, r0, R):
    main = load(r0, R)
    hs = pl.multiple_of(jnp.maximum(r0 - HALO, 0), HALO)
    halo = load(hs, HALO) * (r0 > 0).astype(f32)
    ext = jnp.concatenate([halo, main], axis=0)
    xm1 = pltpu.roll(ext, 1, 0)[HALO:]
    xm2 = pltpu.roll(ext, 2, 0)[HALO:]
    return xm2, xm1, main


def _conv(w, b, taps):
    return b + w[0:1] * taps[0] + w[1:2] * taps[1] + w[2:3] * taps[2]


def _ffn_down(name, up, cw, cb, w_down, resid, S, tm, phase=None, head=None):
    T, F2 = up.shape
    F = F2 // 2
    D = w_down.shape[1]
    cwid = _pick(F, 256)
    per_seq = S // tm

    def body(u_ref, halo_ref, cw_ref, cb_ref, w_ref, r_ref, *rest):
        c_ref = rest[-1]
        keep = (pl.program_id(0) % per_seq > 0).astype(f32)

        def conv(off):
            cols = slice(off, off + cwid)
            main = u_ref[:, cols].astype(f32)
            ext = jnp.concatenate([halo_ref[:, cols].astype(f32) * keep, main], axis=0)
            taps = (pltpu.roll(ext, 2, 0)[HALO:], pltpu.roll(ext, 1, 0)[HALO:], main)
            return _conv(cw_ref[:, cols], cb_ref[:, cols], taps)

        acc = r_ref[...]
        for j in range(F // cwid):
            cg, cv = conv(j * cwid), conv(F + j * cwid)
            c_ref[:, j * cwid:(j + 1) * cwid] = cg.astype(bf16)
            c_ref[:, F + j * cwid:F + (j + 1) * cwid] = cv.astype(bf16)
            a = (cg * jax.nn.sigmoid(cg) * cv).astype(bf16)
            acc = acc + _dot(a, w_ref[j * cwid:(j + 1) * cwid, :])
        if head is None:
            rest[0][...] = acc
        else:
            _loss_head(acc, *rest[:-1])

    full = lambda r, c: pl.BlockSpec((r, c), lambda i: (0, 0))
    row = lambda c: pl.BlockSpec((tm, c), lambda i: (i, 0))
    in_specs = [row(F2), pl.BlockSpec((HALO, F2), lambda i: (jnp.maximum(i * (tm // HALO) - 1, 0), 0)),
                full(3, F2), full(1, F2), full(F, D), row(D)]
    args = [up, up, cw, cb, w_down, resid]
    if head is None:
        out_specs, out_shape = [row(D)], [jax.ShapeDtypeStruct((T, D), f32)]
    else:
        in_specs += [full(1, D), row(D)]
        args += [head[0].reshape(1, D), head[1]]
        out_specs = [row(D), row(D), full(1, 1), full(1, D)]
        out_shape = [jax.ShapeDtypeStruct((T, D), f32), jax.ShapeDtypeStruct((T, D), bf16),
                     jax.ShapeDtypeStruct((1, 1), f32), jax.ShapeDtypeStruct((1, D), f32)]
    return _call(body, name, (T // tm,), in_specs, out_specs + [row(F2)],
                 out_shape + [jax.ShapeDtypeStruct((T, F2), bf16)], args,
                 sem=("parallel",) if head is None else ("arbitrary",), phase=phase)


def _rev_conv_rows(d, nxt, w):
    R = d.shape[0]
    ext = jnp.concatenate([d, nxt], axis=0)
    n = R + HALO
    xp1 = pltpu.roll(ext, n - 1, 0)[:R]
    xp2 = pltpu.roll(ext, n - 2, 0)[:R]
    return w[2:3] * d + w[1:2] * xp1 + w[0:1] * xp2, xp1, xp2


def _conv_grad_acc(acc, dc, taps):
    return (acc[0] + _row_fold(dc * taps[0]), acc[1] + _row_fold(dc * taps[1]), acc[2] + _row_fold(dc * taps[2]),
            acc[3] + _row_fold(dc))


def _conv_grad_out(dcw_ref, dcb_ref, acc):
    @pl.when(pl.program_id(1) == 0)
    def _():
        dcw_ref[...] = jnp.zeros_like(dcw_ref)
        dcb_ref[...] = jnp.zeros_like(dcb_ref)
    for k in range(3):
        dcw_ref[k:k + 1, :] += jnp.sum(acc[k], axis=0, keepdims=True)
    dcb_ref[...] += jnp.sum(acc[3], axis=0, keepdims=True)


def _ffn_act_bwd(name, up3, c3, da3, cw, phase=None):
    B, S, F2 = up3.shape
    F = F2 // 2
    cwid = _pick(F, 256)
    nF = F // cwid
    R = min(256, S)
    nR = S // R

    def body(xg_ref, xv_ref, cg_ref, cv_ref, da_ref, wg_ref, wv_ref,
             act_ref, dg_ref, dv_ref, dcwg_ref, dcwv_ref, dcbg_ref, dcbv_ref, sum_scr):
        wg, wv = wg_ref[...], wv_ref[...]

        def half(d, nxt, w, x_ref, rows, acc, out_ref):
            out, xp1, xp2 = _rev_conv_rows(d, nxt, w)
            out_ref[rows, :] = out.astype(bf16)
            x = x_ref[rows, :].astype(f32)
            return (acc[0] + _row_fold(xp2 * x), acc[1] + _row_fold(xp1 * x), acc[2] + _row_fold(d * x),
                    acc[3] + _row_fold(d))

        def step(i, carry):
            ng, nv, accg, accv = carry
            rows = pl.ds(pl.multiple_of((nR - 1 - i) * R, R), R)
            cg, cv = cg_ref[rows, :].astype(f32), cv_ref[rows, :].astype(f32)
            da = da_ref[rows, :].astype(f32)
            sg = jax.nn.sigmoid(cg)
            act_ref[rows, :] = (cg * sg * cv).astype(bf16)
            dgate = da * cv * (sg * (1.0 + cg * (1.0 - sg)))
            dval = da * (cg * sg)
            accg = half(dgate, ng, wg, xg_ref, rows, accg, dg_ref)
            accv = half(dval, nv, wv, xv_ref, rows, accv, dv_ref)
            return dgate[:HALO], dval[:HALO], accg, accv
        z = jnp.zeros((SUBLANES, cwid), f32)
        zh = jnp.zeros((HALO, cwid), f32)
        _, _, accg, accv = lax.fori_loop(0, nR, step, (zh, zh, (z, z, z, z), (z, z, z, z)))
        j = pl.program_id(1)
        for half_i, (acc, dcw_ref, dcb_ref) in enumerate(((accg, dcwg_ref, dcbg_ref), (accv, dcwv_ref, dcbv_ref))):
            @pl.when(pl.program_id(0) == 0)
            def _():
                sum_scr[half_i, j] = jnp.zeros((SUBLANES, cwid), f32)
            for k in range(4):
                sum_scr[half_i, j, k:k + 1, :] += jnp.sum(acc[k], axis=0, keepdims=True)
            dcw_ref[...] = sum_scr[half_i, j, 0:3, :]
            dcb_ref[...] = sum_scr[half_i, j, 3:4, :]

    blk = lambda off: pl.BlockSpec((None, S, cwid), lambda b, j: (b, 0, off + j))
    wblk = lambda off: pl.BlockSpec((3, cwid), lambda b, j: (0, off + j))
    sums = lambda r: pl.BlockSpec((r, cwid), lambda b, j: (0, jnp.where(b == B - 1, j, nF)))
    half_shape = jax.ShapeDtypeStruct((B, S, F), bf16)
    return _call(
        body, name, (B, nF),
        [blk(0), blk(nF), blk(0), blk(nF), blk(0), wblk(0), wblk(nF)],
        [blk(0), blk(0), blk(0), sums(3), sums(3), sums(1), sums(1)],
        [half_shape, half_shape, half_shape, jax.ShapeDtypeStruct((3, F + cwid), f32),
         jax.ShapeDtypeStruct((3, F + cwid), f32), jax.ShapeDtypeStruct((1, F + cwid), f32),
         jax.ShapeDtypeStruct((1, F + cwid), f32)],
        [up3, up3, c3, c3, da3, cw, cw], scratch=[pltpu.VMEM((2, nF, SUBLANES, cwid), f32)],
        sem=("arbitrary", "arbitrary"), phase=phase)


def _od_act(p3, cw, cb):
    B, S, D3 = p3.shape
    D = D3 // 3
    cwid = _pick(D, 256)
    nD = D // cwid
    R = min(256, S)

    def body(bg_ref, cg_ref, hx_ref, w_ref, b_ref, o_ref):
        w, b = w_ref[...], b_ref[...]
        q = lambda s, n: cg_ref[pl.ds(s, n), :].astype(f32) * hx_ref[pl.ds(s, n), :].astype(f32)

        def chunk(r, c):
            r0 = pl.multiple_of(r * R, R)
            cq = _conv(w, b, _taps(q, r0, R))
            o_ref[pl.ds(r0, R), :] = (bg_ref[pl.ds(r0, R), :].astype(f32) * cq).astype(bf16)
            return c
        lax.fori_loop(0, S // R, chunk, 0)

    blk = lambda off: pl.BlockSpec((None, S, cwid), lambda b, j: (b, 0, off + j))
    return pl.pallas_call(
        body, name="od_act", grid=(B, nD),
        in_specs=[blk(0), blk(nD), blk(2 * nD), pl.BlockSpec((3, cwid), lambda b, j: (0, j)),
                  pl.BlockSpec((1, cwid), lambda b, j: (0, j))],
        out_specs=pl.BlockSpec((None, S, cwid), lambda b, j: (b, 0, j)),
        out_shape=jax.ShapeDtypeStruct((B, S, D), bf16),
        compiler_params=_params(("parallel", "parallel")))(p3, p3, p3, cw, cb)


def _od_act_bwd(p3, dsc3, cw, cb):
    B, S, D3 = p3.shape
    D = D3 // 3
    cwid = _pick(D, 256)
    nD = D // cwid
    R = min(256, S)
    nR = S // R

    def body(bg_ref, cg_ref, hx_ref, d_ref, w_ref, b_ref, dbg_ref, dcg_ref, dhx_ref, dcw_ref, dcb_ref):
        w, b = w_ref[...], b_ref[...]
        q = lambda s, n: cg_ref[pl.ds(s, n), :].astype(f32) * hx_ref[pl.ds(s, n), :].astype(f32)

        def step(i, carry):
            nxt, acc = carry
            r0 = pl.multiple_of((nR - 1 - i) * R, R)
            rows = pl.ds(r0, R)
            tq = _taps(q, r0, R)
            cq = _conv(w, b, tq)
            d = d_ref[rows, :].astype(f32)
            dbg_ref[rows, :] = (d * cq).astype(bf16)
            dcq = d * bg_ref[rows, :].astype(f32)
            dq, _, _ = _rev_conv_rows(dcq, nxt, w)
            dcg_ref[rows, :] = (dq * hx_ref[rows, :].astype(f32)).astype(bf16)
            dhx_ref[rows, :] = (dq * cg_ref[rows, :].astype(f32)).astype(bf16)
            return dcq[:HALO], _conv_grad_acc(acc, dcq, tq)
        z = jnp.zeros((SUBLANES, cwid), f32)
        _, acc = lax.fori_loop(0, nR, step, (jnp.zeros((HALO, cwid), f32), (z, z, z, z)))
        _conv_grad_out(dcw_ref, dcb_ref, acc)

    blk = lambda off: pl.BlockSpec((None, S, cwid), lambda j, b: (b, 0, off + j))
    part = jax.ShapeDtypeStruct((B, S, D), bf16)
    return pl.pallas_call(
        body, name="od_act_bwd", grid=(nD, B),
        in_specs=[blk(0), blk(nD), blk(2 * nD), blk(0), pl.BlockSpec((3, cwid), lambda j, b: (0, j)),
                  pl.BlockSpec((1, cwid), lambda j, b: (0, j))],
        out_specs=[blk(0), blk(0), blk(0), pl.BlockSpec((3, cwid), lambda j, b: (0, j)),
                   pl.BlockSpec((1, cwid), lambda j, b: (0, j))],
        out_shape=[part, part, part, jax.ShapeDtypeStruct((3, D), f32), jax.ShapeDtypeStruct((1, D), f32)],
        compiler_params=_params(("parallel", "arbitrary")))(p3, p3, p3, dsc3, cw, cb)


def _gmlp_parts(p, gv, SW, GW):
    uv = p[:, SW:].astype(f32)
    ge = _gelu(uv)
    u, v = ge[:, :GW], ge[:, GW:]
    vh, r = _rms_stats(v)
    return uv, u, vh, r, vh * gv


def _tril():
    rows = lax.broadcasted_iota(jnp.int32, (CHUNK, CHUNK), 0)
    cols = lax.broadcasted_iota(jnp.int32, (CHUNK, CHUNK), 1)
    return rows >= cols


def _chunks_per_step(T):
    return 4 if T % (4 * CHUNK) == 0 else 1


def _gmlp(p0, a_out, ws, bst, gv, SW):
    T, PW = p0.shape
    GW = (PW - SW) // 2
    H = GW // GMLP_HEAD
    D = SW + GW

    kc = _chunks_per_step(T)
    rb = kc * CHUNK

    def body(p_ref, a_ref, ws_ref, b_ref, gv_ref, o_ref):
        tri = _tril()
        o_ref[:, :SW] = a_ref[...]
        wm = [jnp.where(tri, ws_ref[hh], 0.0).astype(bf16) for hh in range(H)]
        for q in range(kc):
            rows = slice(q * CHUNK, (q + 1) * CHUNK)
            _, u, _, _, vn = _gmlp_parts(p_ref[rows, :], gv_ref[...], SW, GW)
            for hh in range(H):
                sl = slice(hh * GMLP_HEAD, (hh + 1) * GMLP_HEAD)
                gate = _dot(wm[hh], vn[:, sl].astype(bf16)) + b_ref[:, hh:hh + 1]
                o_ref[rows, SW + hh * GMLP_HEAD:SW + (hh + 1) * GMLP_HEAD] = (u[:, sl] * gate).astype(bf16)

    return pl.pallas_call(
        body, name="gmlp", grid=(T // rb,),
        in_specs=[pl.BlockSpec((rb, PW), lambda i: (i, 0)), pl.BlockSpec((rb, SW), lambda i: (i, 0)),
                  pl.BlockSpec((H, CHUNK, CHUNK), lambda i: (0, 0, 0)), pl.BlockSpec((CHUNK, H), lambda i: (0, 0)),
                  pl.BlockSpec((1, GW), lambda i: (0, 0))],
        out_specs=pl.BlockSpec((rb, D), lambda i: (i, 0)),
        out_shape=jax.ShapeDtypeStruct((T, D), bf16),
        compiler_params=_params(("parallel",)))(p0, a_out, ws, bst, gv)


def _gmlp_bwd(p0, dmix, ws, bst, gv, SW, phase=None):
    T, PW = p0.shape
    GW = (PW - SW) // 2
    H = GW // GMLP_HEAD
    D = SW + GW

    kc = _chunks_per_step(T)
    rb = kc * CHUNK

    def body(p_ref, d_ref, ws_ref, b_ref, gv_ref, duv_ref, dws_ref, dbs_ref, dgv_ref):
        gv_ = gv_ref[...]
        tri = _tril()

        @pl.when(pl.program_id(0) == 0)
        def _():
            dws_ref[...] = jnp.zeros_like(dws_ref)
            dbs_ref[...] = jnp.zeros_like(dbs_ref)
            dgv_ref[...] = jnp.zeros_like(dgv_ref)
        wm = [jnp.where(tri, ws_ref[hh], 0.0).astype(bf16) for hh in range(H)]
        for q in range(kc):
            rows = slice(q * CHUNK, (q + 1) * CHUNK)
            uv, u, vh, r, vn = _gmlp_parts(p_ref[rows, :], gv_, SW, GW)
            dout = d_ref[rows, SW:].astype(f32)
            du, dvn = [], []
            for hh in range(H):
                sl = slice(hh * GMLP_HEAD, (hh + 1) * GMLP_HEAD)
                vnh = vn[:, sl].astype(bf16)
                gate = _dot(wm[hh], vnh) + b_ref[:, hh:hh + 1]
                dgate = dout[:, sl] * u[:, sl]
                du.append(dout[:, sl] * gate)
                dgb = dgate.astype(bf16)
                dws_ref[hh] += jnp.where(tri, _dg(dgb, vnh, NT), 0.0)
                dbs_ref[hh] += jnp.broadcast_to(jnp.sum(dgate, axis=1, keepdims=True), (CHUNK, CHUNK))
                dvn.append(_dg(wm[hh], dgb, TN))
            dvn = jnp.concatenate(dvn, axis=1)
            dv, dgv = _rms_bwd(dvn, vh, r, gv_)
            dgv_ref[...] += dgv
            dge = jnp.concatenate(du + [dv], axis=1)
            duv_ref[rows, :] = (dge * _gelu_grad(uv)).astype(bf16)

    fixed = pl.BlockSpec((H, CHUNK, CHUNK), lambda i: (0, 0, 0))
    return _call(
        body, "gmlp_bwd", (T // rb,),
        [pl.BlockSpec((rb, PW), lambda i: (i, 0)), pl.BlockSpec((rb, D), lambda i: (i, 0)), fixed,
         pl.BlockSpec((CHUNK, H), lambda i: (0, 0)), pl.BlockSpec((1, GW), lambda i: (0, 0))],
        [pl.BlockSpec((rb, 2 * GW), lambda i: (i, 0)), fixed, fixed, pl.BlockSpec((1, GW), lambda i: (0, 0))],
        [jax.ShapeDtypeStruct((T, 2 * GW), bf16), jax.ShapeDtypeStruct((H, CHUNK, CHUNK), f32),
         jax.ShapeDtypeStruct((H, CHUNK, CHUNK), f32), jax.ShapeDtypeStruct((1, GW), f32)],
        [p0, dmix, ws, bst, gv], sem=("arbitrary",), phase=phase)


def _s5_disc(lr, li, ldt):
    lr = jnp.minimum(lr, LAMBDA_RE_MAX)
    dt = jnp.exp(ldt)
    mag = jnp.exp(lr * dt)
    ar = mag * jnp.cos(li * dt)
    ai = mag * jnp.sin(li * dt)
    den = lr * lr + li * li
    nr = ar - 1.0
    zr = (nr * lr + ai * li) / den
    zi = (ai * lr - nr * li) / den
    return ar, ai, zr, zi


def _s5_prep(lr, li, ldt):
    G, P = lr.shape

    def body(lr_ref, li_ref, ldt_ref, ar_ref, ai_ref, zr_ref, zi_ref):
        ar, ai, zr, zi = _s5_disc(lr_ref[...], li_ref[...], ldt_ref[...])
        ar_ref[...] = ar
        ai_ref[...] = ai
        zr_ref[...] = zr
        zi_ref[...] = zi

    s = jax.ShapeDtypeStruct((G, P), f32)
    return pl.pallas_call(body, name="s5_prep", out_shape=[s, s, s, s])(lr, li, ldt)


def _s5_prep_bwd(lr, li, ldt, dar, dai, dzr, dzi):
    G, P = lr.shape

    def body(lr_ref, li_ref, ldt_ref, dar_ref, dai_ref, dzr_ref, dzi_ref, o1, o2, o3):
        _, vjp = jax.vjp(_s5_disc, lr_ref[...], li_ref[...], ldt_ref[...])
        cts = tuple(jnp.sum(r[...], axis=0) for r in (dar_ref, dai_ref, dzr_ref, dzi_ref))
        a, b, c = vjp(cts)
        o1[...] = a
        o2[...] = b
        o3[...] = c

    s = jax.ShapeDtypeStruct((G, P), f32)
    return pl.pallas_call(body, name="s5_prep_bwd", out_shape=[s, s, jax.ShapeDtypeStruct((G, 1), f32)])(
        lr, li, ldt, dar, dai, dzr, dzi)


def _s5_bbd(zr, zi, bre, bim):
    SW, NS = bre.shape

    def body(zr_ref, zi_ref, br_ref, bi_ref, o_ref):
        zr_, zi_, br, bi = zr_ref[...], zi_ref[...], br_ref[...], bi_ref[...]
        o_ref[:, :NS] = (zr_ * br - zi_ * bi).astype(bf16)
        o_ref[:, NS:] = (zr_ * bi + zi_ * br).astype(bf16)

    return pl.pallas_call(body, name="s5_bbd", out_shape=jax.ShapeDtypeStruct((SW, 2 * NS), bf16))(zr, zi, bre, bim)


def _s5_bbd_bwd(dbbd, zr, zi, bre, bim):
    SW, NS = bre.shape

    def body(d_ref, zr_ref, zi_ref, br_ref, bi_ref, dbr_ref, dbi_ref, dzr_ref, dzi_ref):
        zr_, zi_, br, bi = zr_ref[...], zi_ref[...], br_ref[...], bi_ref[...]
        dr, di = d_ref[:, :NS], d_ref[:, NS:]
        dbr_ref[...] = zr_ * dr + zi_ * di
        dbi_ref[...] = zr_ * di - zi_ * dr
        dzr_ref[...] = jnp.sum(dr * br + di * bi, axis=0, keepdims=True)
        dzi_ref[...] = jnp.sum(di * br - dr * bi, axis=0, keepdims=True)

    m = jax.ShapeDtypeStruct((SW, NS), f32)
    v = jax.ShapeDtypeStruct((1, NS), f32)
    return pl.pallas_call(body, name="s5_bbd_bwd", out_shape=[m, m, v, v])(dbbd, zr, zi, bre, bim)


def _slab_cat(ref, NB):
    return jnp.concatenate([ref[j] for j in range(NB)], axis=1)


def _s5_in(p3, bbd, SW, tm):
    B, S, PW = p3.shape
    NS = bbd.shape[1] // 2
    NB = NS // LANES

    def body(u_ref, b_ref, xr_ref, xi_ref):
        x = _dot(u_ref[...], b_ref[...])
        for j in range(NB):
            xr_ref[j] = x[:, j * LANES:(j + 1) * LANES]
            xi_ref[j] = x[:, NS + j * LANES:NS + (j + 1) * LANES]

    slab = jax.ShapeDtypeStruct((B, NB, S, LANES), f32)
    sspec = pl.BlockSpec((None, NB, tm, LANES), lambda b, i: (b, 0, i, 0))
    return pl.pallas_call(
        body, name="s5_in", grid=(B, S // tm),
        in_specs=[pl.BlockSpec((None, tm, SW), lambda b, i: (b, i, 0)), pl.BlockSpec((SW, 2 * NS), lambda b, i: (0, 0))],
        out_specs=[sspec, sspec], out_shape=[slab, slab],
        compiler_params=_params(("parallel", "parallel")))(p3, bbd)


def _s5_scan(name, xr, xi, ar, ai, reverse, hr=None, hi=None, phase=None):
    B, NB, S, _ = xr.shape
    L = S // NSUB
    nb = 2 if (hr is None and NB % 2 == 0) else 1
    with_da = hr is not None

    def body(*refs):
        if with_da:
            xr_ref, xi_ref, ar_ref, ai_ref, hr_ref, hi_ref, or_ref, oi_ref, dar_ref, dai_ref, pr_scr, pi_scr = refs
        else:
            xr_ref, xi_ref, ar_ref, ai_ref, or_ref, oi_ref, pr_scr, pi_scr = refs
        sign = -1.0 if reverse else 1.0
        a_r = [jnp.broadcast_to(ar_ref[j], (NSUB, LANES)) for j in range(nb)]
        a_i = [jnp.broadcast_to(ai_ref[j], (NSUB, LANES)) * sign for j in range(nb)]

        def step(t, carry):
            row = (L - 1 - t) if reverse else t
            rows = pl.ds(row, NSUB, stride=L)
            out = []
            for j in range(nb):
                sr, si, pr, pi = carry[j]
                nr = a_r[j] * sr - a_i[j] * si + xr_ref.at[j][rows, :]
                ni = a_r[j] * si + a_i[j] * sr + xi_ref.at[j][rows, :]
                or_ref.at[j][rows, :] = nr
                oi_ref.at[j][rows, :] = ni
                npr = a_r[j] * pr - a_i[j] * pi
                npi = a_r[j] * pi + a_i[j] * pr
                pr_scr[j, pl.ds(row, 1), :] = npr[0:1]
                pi_scr[j, pl.ds(row, 1), :] = npi[0:1]
                out.append((nr, ni, npr, npi))
            return tuple(out)
        z = jnp.zeros((NSUB, LANES), f32)
        one = jnp.ones((NSUB, LANES), f32)
        fin = lax.fori_loop(0, L, step, tuple((z, z, one, z) for _ in range(nb)))

        for j in range(nb):
            sr, si, plr, pli = fin[j]
            plr, pli = plr[0:1], pli[0:1]
            cr = jnp.zeros((1, LANES), f32)
            ci = jnp.zeros((1, LANES), f32)
            order = range(NSUB - 2, -1, -1) if reverse else range(1, NSUB)
            for c in order:
                src = c + 1 if reverse else c - 1
                cr, ci = (sr[src:src + 1] + plr * cr - pli * ci, si[src:src + 1] + plr * ci + pli * cr)
                rows = slice(c * L, (c + 1) * L)
                tr, ti = pr_scr[j], pi_scr[j]
                or_ref[j, rows, :] += tr * cr - ti * ci
                oi_ref[j, rows, :] += tr * ci + ti * cr
            if with_da:
                first = lax.broadcasted_iota(jnp.int32, (L, LANES), 0) == 0
                dar = jnp.zeros((1, LANES), f32)
                dai = jnp.zeros((1, LANES), f32)
                for c in range(NSUB):
                    rows = slice(c * L, (c + 1) * L)
                    if c == 0:
                        lr_, li_ = jnp.zeros((1, LANES), f32), jnp.zeros((1, LANES), f32)
                    else:
                        lr_, li_ = hr_ref[j, c * L - 1:c * L, :], hi_ref[j, c * L - 1:c * L, :]
                    hpr = jnp.where(first, lr_, pltpu.roll(hr_ref[j, rows, :], 1, 0))
                    hpi = jnp.where(first, li_, pltpu.roll(hi_ref[j, rows, :], 1, 0))
                    gr, gi = or_ref[j, rows, :], oi_ref[j, rows, :]
                    dar += jnp.sum(hpr * gr + hpi * gi, axis=0, keepdims=True)
                    dai += jnp.sum(hpr * gi - hpi * gr, axis=0, keepdims=True)
                dar_ref[j] = dar
                dai_ref[j] = dai

    slab = jax.ShapeDtypeStruct((B, NB, S, LANES), f32)
    sspec = pl.BlockSpec((None, nb, S, LANES), lambda b, j: (b, j, 0, 0))
    aspec = pl.BlockSpec((nb, 1, LANES), lambda b, j: (j, 0, 0))
    in_specs = [sspec, sspec, aspec, aspec]
    out_specs = [sspec, sspec]
    out_shape = [slab, slab]
    args = [xr, xi, ar, ai]
    if with_da:
        in_specs += [sspec, sspec]
        args += [hr, hi]
        dspec = pl.BlockSpec((None, nb, 1, LANES), lambda b, j: (b, j, 0, 0))
        out_specs += [dspec, dspec]
        out_shape += [jax.ShapeDtypeStruct((B, NB, 1, LANES), f32)] * 2
    return _call(body, name, (B, NB // nb), in_specs, out_specs, out_shape, args,
                 scratch=[pltpu.VMEM((nb, L, LANES), f32), pltpu.VMEM((nb, L, LANES), f32)],
                 sem=("parallel", "parallel"), phase=phase)


def _s5_out_parts(hr_ref, hi_ref, u_ref, cr_ref, ci_ref, d_ref, wg_ref, bg_ref, NB):
    hcr = _slab_cat(hr_ref, NB).astype(bf16)
    hci = _slab_cat(hi_ref, NB).astype(bf16)
    u = u_ref[...].astype(f32)
    y2 = _dot(hcr, cr_ref[...]) - _dot(hci, ci_ref[...]) + d_ref[...] * u
    yg = _gelu(y2)
    s = jax.nn.sigmoid(_dot(yg.astype(bf16), wg_ref[...]) + bg_ref[...])
    return hcr, hci, u, y2, yg, s


def _s5_out_specs(B, S, NB, NS, SW, tm):
    sspec = pl.BlockSpec((None, NB, tm, LANES), lambda b, i: (b, 0, i, 0))
    full = lambda r, c: pl.BlockSpec((r, c), lambda b, i: (0, 0))
    return sspec, [sspec, sspec, pl.BlockSpec((None, tm, SW), lambda b, i: (b, i, 0)), full(NS, SW), full(NS, SW),
                   full(1, SW), full(SW, SW), full(1, SW)]


def _s5_out(hr, hi, p3, cbr, cbi, dsk, wglu, bglu, tm):
    B, NB, S, _ = hr.shape
    NS, SW = cbr.shape

    def body(hr_ref, hi_ref, u_ref, cr_ref, ci_ref, d_ref, wg_ref, bg_ref, o_ref):
        _, _, _, _, yg, s = _s5_out_parts(hr_ref, hi_ref, u_ref, cr_ref, ci_ref, d_ref, wg_ref, bg_ref, NB)
        o_ref[...] = (yg * s).astype(bf16)

    _, in_specs = _s5_out_specs(B, S, NB, NS, SW, tm)
    return pl.pallas_call(
        body, name="s5_out", grid=(B, S // tm), in_specs=in_specs,
        out_specs=pl.BlockSpec((None, tm, SW), lambda b, i: (b, i, 0)),
        out_shape=jax.ShapeDtypeStruct((B, S, SW), bf16),
        compiler_params=_params(("parallel", "parallel")))(hr, hi, p3, cbr, cbi, dsk, wglu, bglu)


def _s5_out_bwd(hr, hi, p3, dmix3, cbr, cbi, dsk, wglu, bglu, tm):
    B, NB, S, _ = hr.shape
    NS, SW = cbr.shape

    def body(hr_ref, hi_ref, u_ref, cr_ref, ci_ref, d_ref, wg_ref, bg_ref, da_ref,
             dhr_ref, dhi_ref, du_ref, dcr_ref, dci_ref, dd_ref, dwg_ref, dbg_ref):
        hcr, hci, u, y2, yg, s = _s5_out_parts(hr_ref, hi_ref, u_ref, cr_ref, ci_ref, d_ref, wg_ref, bg_ref, NB)
        da = da_ref[...].astype(f32)
        dz = da * yg * s * (1.0 - s)
        dzb = dz.astype(bf16)
        dyg = da * s + _dg(dzb, wg_ref[...], NT)
        dy2 = dyg * _gelu_grad(y2)
        dyb = dy2.astype(bf16)

        @pl.when((pl.program_id(0) == 0) & (pl.program_id(1) == 0))
        def _():
            for r in (dcr_ref, dci_ref, dd_ref, dwg_ref, dbg_ref):
                r[...] = jnp.zeros_like(r)
        dwg_ref[...] += _dg(yg.astype(bf16), dzb, TN)
        dbg_ref[...] += jnp.sum(dz, axis=0, keepdims=True)
        dd_ref[...] += jnp.sum(dy2 * u, axis=0, keepdims=True)
        dcr_ref[...] += _dg(hcr, dyb, TN)
        dci_ref[...] -= _dg(hci, dyb, TN)
        du_ref[...] = dy2 * d_ref[...]
        dhr = _dg(dyb, cr_ref[...], NT)
        dhi = _dg(dyb, ci_ref[...], NT)
        for j in range(NB):
            dhr_ref[j] = dhr[:, j * LANES:(j + 1) * LANES]
            dhi_ref[j] = -dhi[:, j * LANES:(j + 1) * LANES]

    sspec, in_specs = _s5_out_specs(B, S, NB, NS, SW, tm)
    in_specs = in_specs + [pl.BlockSpec((None, tm, SW), lambda b, i: (b, i, 0))]
    full = lambda r, c: pl.BlockSpec((r, c), lambda b, i: (0, 0))
    slab = jax.ShapeDtypeStruct((B, NB, S, LANES), f32)
    mat = lambda r, c: jax.ShapeDtypeStruct((r, c), f32)
    return pl.pallas_call(
        body, name="s5_out_bwd", grid=(B, S // tm), in_specs=in_specs,
        out_specs=[sspec, sspec, pl.BlockSpec((None, tm, SW), lambda b, i: (b, i, 0)), full(NS, SW), full(NS, SW),
                   full(1, SW), full(SW, SW), full(1, SW)],
        out_shape=[slab, slab, jax.ShapeDtypeStruct((B, S, SW), f32), mat(NS, SW), mat(NS, SW), mat(1, SW),
                   mat(SW, SW), mat(1, SW)],
        compiler_params=_params(("arbitrary", "arbitrary")))(hr, hi, p3, cbr, cbi, dsk, wglu, bglu, dmix3)


def _s5_in_bwd(gr, gi, p3, bbd, du_skip, duv3, tm):
    B, NB, S, _ = gr.shape
    SW, NS2 = bbd.shape
    PW = SW + duv3.shape[2]

    def body(gr_ref, gi_ref, u_ref, b_ref, ds_ref, duv_ref, dp_ref, db_ref):
        g = jnp.concatenate([_slab_cat(gr_ref, NB), _slab_cat(gi_ref, NB)], axis=1).astype(bf16)
        du = _dg(g, b_ref[...], NT) + ds_ref[...]
        dp_ref[:, :SW] = du.astype(bf16)
        dp_ref[:, SW:] = duv_ref[...]

        @pl.when((pl.program_id(0) == 0) & (pl.program_id(1) == 0))
        def _():
            db_ref[...] = jnp.zeros_like(db_ref)
        db_ref[...] += _dg(u_ref[...], g, TN)

    sspec = pl.BlockSpec((None, NB, tm, LANES), lambda b, i: (b, 0, i, 0))
    row = lambda c: pl.BlockSpec((None, tm, c), lambda b, i: (b, i, 0))
    return pl.pallas_call(
        body, name="s5_in_bwd", grid=(B, S // tm),
        in_specs=[sspec, sspec, row(SW), pl.BlockSpec((SW, NS2), lambda b, i: (0, 0)), row(SW), row(PW - SW)],
        out_specs=[row(PW), pl.BlockSpec((SW, NS2), lambda b, i: (0, 0))],
        out_shape=[jax.ShapeDtypeStruct((B, S, PW), bf16), jax.ShapeDtypeStruct((SW, NS2), f32)],
        compiler_params=_params(("arbitrary", "arbitrary")))(gr, gi, p3, bbd, du_skip, duv3)


BIG = ['ev_w_in', 'ev_w_out', 'od_w_in', 'od_w_out', 'ffn_w_up', 'ffn_w_down']
ANY = pl.BlockSpec(memory_space=pl.ANY)


def _rtile(rows, mult):
    best = None
    for d in range(mult, min(rows, 512) + 1, mult):
        if rows % d == 0:
            best = d
    assert best is not None, (rows, mult)
    return best


def _pair_sum(name, g, recv, c_idx, out_dtype):
    NCH, R, W = g.shape
    HALF_W = W // 2
    tr = _rtile(R, 16)

    def body(c_ref, a_ref, b_ref, o_ref):
        o_ref[...] = (a_ref[...] + b_ref[...]).astype(out_dtype)

    return pl.pallas_call(
        body, name=name,
        grid_spec=pltpu.PrefetchScalarGridSpec(
            num_scalar_prefetch=1, grid=(NCH, R // tr),
            in_specs=[pl.BlockSpec((None, tr, HALF_W), lambda j, i, c: (j, i, c[0])),
                      pl.BlockSpec((None, tr, HALF_W), lambda j, i, c: (j, i, 0))],
            out_specs=pl.BlockSpec((None, tr, HALF_W), lambda j, i, c: (j, i, 0))),
        out_shape=jax.ShapeDtypeStruct((NCH, R, HALF_W), out_dtype),
        compiler_params=_params(("parallel", "parallel")))(c_idx, g, recv)


def _chip_sum(name, r3, h, k_idx):
    NCH, R, Wh = r3.shape
    tr = _rtile(R, 16)

    def body(k_ref, a_ref, own_ref, o_ref):
        own = own_ref[...].astype(f32)
        t = [jnp.where(k_ref[0] == s, own, a_ref[s].astype(f32)) for s in range(NCH)]
        o_ref[...] = ((t[0] + t[1]) + t[2]) + t[3]

    return pl.pallas_call(
        body, name=name,
        grid_spec=pltpu.PrefetchScalarGridSpec(
            num_scalar_prefetch=1, grid=(R // tr,),
            in_specs=[pl.BlockSpec((NCH, tr, Wh), lambda i, k: (0, i, 0)),
                      pl.BlockSpec((None, tr, Wh), lambda i, k: (k[0], i, 0))],
            out_specs=pl.BlockSpec((tr, Wh), lambda i, k: (i, 0))),
        out_shape=jax.ShapeDtypeStruct((R, Wh), f32),
        compiler_params=_params(("parallel",)))(k_idx, r3, h)


def _adam_math(gg, w, m, v):
    nm = ADAM_B1 * m + (1.0 - ADAM_B1) * gg
    nv = ADAM_B2 * v + (1.0 - ADAM_B2) * jnp.square(gg)
    m_hat = nm / (1.0 - ADAM_B1 ** ADAM_STEP)
    v_hat = nv / (1.0 - ADAM_B2 ** ADAM_STEP)
    return -ADAM_LR * (m_hat / (jnp.sqrt(v_hat) + ADAM_EPS) + ADAM_WD * w), nm, nv


def _adamw(name, mine, theirs, c_idx, w, m, v, lead, transposed, prev=None):
    L, R, W = w.shape
    if transposed:
        bw = LANES if W % LANES == 0 else W
        gspec = pl.BlockSpec((bw, R // 2), lambda i, hf, c: (i, 0))
        wspec = pl.BlockSpec((None, R // 2, bw), lambda i, hf, c: (lead, hf, i))
        grid = (W // bw, 2)
    else:
        tr = _rtile(R, SUBLANES)
        gspec = pl.BlockSpec((tr, W // 2), lambda i, hf, c: (i, 0))
        wspec = pl.BlockSpec((None, tr, W // 2), lambda i, hf, c: (lead, i, hf))
        grid = (R // tr, 2)

    def body(c_ref, a_ref, b_ref, w_ref, m_ref, v_ref, *rest):
        go_ref, d_ref, nm_ref, nv_ref = rest[-4:]
        gg = jnp.where(pl.program_id(1) == c_ref[0], a_ref[...], b_ref[...])
        if transposed:
            gg = gg.T
        d, nm, nv = _adam_math(gg, w_ref[...], m_ref[...], v_ref[...])
        go_ref[...] = gg
        d_ref[...] = d
        nm_ref[...] = nm
        nv_ref[...] = nv

    in_specs = [gspec, gspec, wspec, wspec, wspec]
    args, aliases = [c_idx, mine, theirs, w, m, v], {}
    if prev is not None:
        in_specs += [ANY] * 4
        args += list(prev)
        aliases = {6: 0, 7: 1, 8: 2, 9: 3}
    s = jax.ShapeDtypeStruct((L, R, W), f32)
    return pl.pallas_call(
        body, name=name,
        grid_spec=pltpu.PrefetchScalarGridSpec(num_scalar_prefetch=1, grid=grid, in_specs=in_specs,
                                               out_specs=[wspec] * 4),
        out_shape=[s, s, s, s], input_output_aliases=aliases,
        compiler_params=_params(("parallel", "arbitrary")))(*args)


def _adamw_small(gs, ws, ms, vs):
    n = len(gs)

    def body(*refs):
        for i in range(n):
            d, nm, nv = _adam_math(refs[i][...], refs[n + i][...], refs[2 * n + i][...], refs[3 * n + i][...])
            refs[4 * n + i][...] = d
            refs[5 * n + i][...] = nm
            refs[6 * n + i][...] = nv

    return pl.pallas_call(body, name="adamw_small",
                          out_shape=[jax.ShapeDtypeStruct(t.shape, f32) for t in ws] * 3)(*gs, *ws, *ms, *vs)


def _place():
    x, y, c = lax.axis_index("x"), lax.axis_index("y"), lax.axis_index("c")
    return x, y, c, [(1 - x, y), (x, 1 - y), (1 - x, 1 - y)]


def _gathered_shape(sh, kind):
    if kind == "rows":
        return sh[:-2] + (N_CHIPS * sh[-2], sh[-1])
    if kind == "cols":
        return sh[:-1] + (N_CHIPS * sh[-1],)
    return (N_CHIPS,) + sh


def _place_shard(name, shard, kind, k_idx):
    sh = shard.shape
    r, C = sh[-2], sh[-1]
    L = sh[0] if len(sh) == 3 else 1
    tr = _rtile(r, 16)
    nr = r // tr
    if kind == "rows":
        out3, omap = (L, N_CHIPS * r, C), lambda l, i, k: (l, k[0] * nr + i, 0)
    elif kind == "cols":
        out3, omap = (L, r, N_CHIPS * C), lambda l, i, k: (l, i, k[0])
    else:
        out3, omap = (N_CHIPS, r, C), lambda l, i, k: (k[0], i, 0)

    def body(k_ref, s_ref, o_ref):
        o_ref[...] = s_ref[...]

    out = pl.pallas_call(
        body, name=name,
        grid_spec=pltpu.PrefetchScalarGridSpec(
            num_scalar_prefetch=1, grid=(L, nr),
            in_specs=[pl.BlockSpec((None, tr, C), lambda l, i, k: (l, i, 0))],
            out_specs=pl.BlockSpec((None, tr, C), omap)),
        out_shape=jax.ShapeDtypeStruct(out3, shard.dtype),
        compiler_params=_params(("parallel", "parallel")))(k_idx, shard.reshape(L, r, C))
    return out.reshape(_gathered_shape(sh, kind))


def _gather_phase(shards, fulls, kinds):
    n = len(shards)
    shapes = [s.shape for s in shards]

    def window(ref, a, k, h=None):
        sh, kind = shapes[a], kinds[a]
        r = sh[-2]
        start, size = (0, r) if h is None else (h * (r // 2), r // 2)
        lead = (slice(None),) * (len(sh) - 2)
        if kind == "rows":
            return ref.at[lead + (pl.ds(k * r + start, size), slice(None))]
        if kind == "cols":
            return ref.at[lead + (pl.ds(start, size), pl.ds(pl.multiple_of(k * sh[-1], LANES), sh[-1]))]
        return ref.at[(k,) + lead + (pl.ds(start, size), slice(None))]

    def copies(s_refs, o_refs, sems):
        send_sems, recv_sems = sems
        x, y, c, chips = _place()
        k = 2 * x + y

        def copy(a, j, kk, hh, to, src=None):
            dst = window(o_refs[a], a, kk, hh)
            return pltpu.make_async_remote_copy(
                src_ref=dst if src is None else src, dst_ref=dst, send_sem=send_sems.at[6 * a + j],
                recv_sem=recv_sems.at[6 * a + j], device_id=to, device_id_type=MESH)

        first = []
        for a in range(n):
            r = shapes[a][-2]
            lead = (slice(None),) * (len(shapes[a]) - 2)
            src = s_refs[a].at[lead + (pl.ds(c * (r // 2), r // 2), slice(None))]
            first += [copy(a, j, k, c, (*chip, c), src=src) for j, chip in enumerate(chips)]
        return copy, first, (x, y, c), (x, y, 1 - c), c, chips

    def start(s_refs, o_refs, sems):
        for cp in copies(s_refs, o_refs, sems)[1]:
            cp.start()

    def finish(s_refs, o_refs, sems):
        copy, first, me, sibling, c, chips = copies(s_refs, o_refs, sems)
        passed = []
        for j, (cx, cy) in enumerate(chips):
            for a in range(n):
                copy(a, j, 2 * cx + cy, c, me).wait_recv()
                fwd = copy(a, 3 + j, 2 * cx + cy, c, sibling)
                fwd.start()
                passed.append(fwd)
        for j, (cx, cy) in enumerate(chips):
            for a in range(n):
                copy(a, 3 + j, 2 * cx + cy, 1 - c, me).wait_recv()
        for cp in first + passed:
            cp.wait_send()

    return _Phase(shards, fulls, [jax.ShapeDtypeStruct(f.shape, f.dtype) for f in fulls],
                  [pltpu.SemaphoreType.DMA((6 * n,)), pltpu.SemaphoreType.DMA((6 * n,))], start, finish)


def _swap_phase(gs):
    n = len(gs)

    def copies(g_refs, o_refs, sems):
        send_sems, recv_sems = sems
        x, y, c, _ = _place()
        half = [g.shape[2] // 2 for g in gs]
        return [pltpu.make_async_remote_copy(
            src_ref=g_refs[a].at[:, :, pl.ds(pl.multiple_of((1 - c) * half[a], LANES), half[a])], dst_ref=o_refs[a],
            send_sem=send_sems.at[a], recv_sem=recv_sems.at[a], device_id=(x, y, 1 - c), device_id_type=MESH)
            for a in range(n)]

    def start(g_refs, o_refs, sems):
        for cp in copies(g_refs, o_refs, sems):
            cp.start()

    def finish(g_refs, o_refs, sems):
        for cp in copies(g_refs, o_refs, sems):
            cp.wait()

    return _Phase(gs, [], [jax.ShapeDtypeStruct(g.shape[:2] + (g.shape[2] // 2,), g.dtype) for g in gs],
                  [pltpu.SemaphoreType.DMA((n,)), pltpu.SemaphoreType.DMA((n,))], start, finish)


def _exchange_phase(hs):
    n = len(hs)

    def copies(h_refs, o_refs, sems):
        send_sems, recv_sems = sems
        x, y, c, chips = _place()
        k = 2 * x + y

        def copy(a, j, src_slot, dst_slot):
            cx, cy = chips[j]
            return pltpu.make_async_remote_copy(
                src_ref=h_refs[a].at[src_slot], dst_ref=o_refs[a].at[dst_slot], send_sem=send_sems.at[3 * a + j],
                recv_sem=recv_sems.at[3 * a + j], device_id=(cx, cy, c), device_id_type=MESH)

        sends = [copy(a, j, 2 * cx + cy, k) for a in range(n) for j, (cx, cy) in enumerate(chips)]
        return copy, sends, k, chips

    def start(h_refs, o_refs, sems):
        for cp in copies(h_refs, o_refs, sems)[1]:
            cp.start()

    def finish(h_refs, o_refs, sems):
        copy, sends, k, chips = copies(h_refs, o_refs, sems)
        for a in range(n):
            for j, (cx, cy) in enumerate(chips):
                copy(a, j, k, 2 * cx + cy).wait_recv()
        for cp in sends:
            cp.wait_send()

    return _Phase(hs, [], [jax.ShapeDtypeStruct(h.shape, h.dtype) for h in hs],
                  [pltpu.SemaphoreType.DMA((3 * n,)), pltpu.SemaphoreType.DMA((3 * n,))], start, finish)


def _comm_pair_share(tag, gs):
    n = len(gs)

    def body(*refs):
        g_refs, o_refs, send_sems, recv_sems = refs[:n], refs[n:2 * n], refs[2 * n], refs[2 * n + 1]
        x, y, c, _ = _place()
        cps = [pltpu.make_async_remote_copy(
            src_ref=g_refs[a], dst_ref=o_refs[a], send_sem=send_sems.at[a], recv_sem=recv_sems.at[a],
            device_id=(x, y, 1 - c), device_id_type=MESH) for a in range(n)]
        for cp in cps:
            cp.start()
        for cp in cps:
            cp.wait()

    return pl.pallas_call(
        body, name="comm_pair_share_" + tag, in_specs=[ANY] * n, out_specs=[ANY] * n,
        out_shape=[jax.ShapeDtypeStruct(g.shape, g.dtype) for g in gs],
        scratch_shapes=[pltpu.SemaphoreType.DMA((n,)), pltpu.SemaphoreType.DMA((n,))])(*gs)


def _pad_rows(flat, unit):
    n = flat.shape[-1]
    pad = (-n) % unit
    if pad:
        flat = jnp.pad(flat, [(0, 0)] * (flat.ndim - 1) + [(0, pad)])
    return flat


def _split_chips(full, axis):
    sh = full.shape
    t = full.reshape(sh[:axis] + (N_CHIPS, sh[axis] // N_CHIPS) + sh[axis + 1:])
    return jnp.moveaxis(t, axis, 0).reshape(N_CHIPS, -1)


def _join_chips(stack, shard_shape, axis):
    t = jnp.moveaxis(stack.reshape((N_CHIPS,) + tuple(shard_shape)), 0, axis)
    sh = t.shape
    return t.reshape(sh[:axis] + (sh[axis] * sh[axis + 1],) + sh[axis + 2:])


def _block_diag(blocks):
    G, r, c = blocks.shape
    eye = jnp.eye(G, dtype=blocks.dtype)
    return (blocks[:, :, None, :] * eye[:, None, :, None]).reshape(G * r, G * c)


def _diag_blocks(m, G):
    r, c = m.shape[0] // G, m.shape[1] // G
    idx = jnp.arange(G)
    return m.reshape(G, r, G, c)[idx, :, idx, :]


def _weight_shards(w):
    conv = jnp.concatenate([w[n].reshape(-1) for n in GATHER_F32])
    conv = _pad_rows(conv, 2 * SUBLANES * LANES).reshape(-1, LANES)
    b16 = lambda a: a.astype(bf16)
    return {'ev_w_in': (b16(w['ev_w_in'][0]), "chip"), 'ev_w_out': (b16(w['ev_w_out'][0]), "rows"),
            's5_w_glu': (b16(w['s5_w_glu'][0]), "rows"), 'conv': (conv, "chip"),
            'od_w_in': (b16(w['od_w_in'][0]), "cols"), 'od_w_out': (b16(w['od_w_out'][0]), "rows"),
            'ffn_w_up0': (b16(w['ffn_w_up'][0]), "cols"), 'ffn_w_up1': (b16(w['ffn_w_up'][1]), "cols"),
            'ffn_w_down0': (b16(w['ffn_w_down'][0]), "rows"), 'ffn_w_down1': (b16(w['ffn_w_down'][1]), "rows")}


def kernel(x, mix_norm_g, ffn_norm_g, final_norm_g, ev_w_in, ev_w_out, s5_lam_re, s5_lam_im, s5_log_dt, s5_b_re, s5_b_im, s5_c_re, s5_c_im, s5_d, s5_w_glu, s5_b_glu, gm_w_s, gm_b_s, gm_v_g, od_w_in, od_conv_w, od_conv_b, od_w_out, ffn_w_up, ffn_conv_w, ffn_conv_b, ffn_w_down, loss_target, m_mix_norm_g, m_ffn_norm_g, m_final_norm_g, m_ev_w_in, m_ev_w_out, m_s5_lam_re, m_s5_lam_im, m_s5_log_dt, m_s5_b_re, m_s5_b_im, m_s5_c_re, m_s5_c_im, m_s5_d, m_s5_w_glu, m_s5_b_glu, m_gm_w_s, m_gm_b_s, m_gm_v_g, m_od_w_in, m_od_conv_w, m_od_conv_b, m_od_w_out, m_ffn_w_up, m_ffn_conv_w, m_ffn_conv_b, m_ffn_w_down, v_mix_norm_g, v_ffn_norm_g, v_final_norm_g, v_ev_w_in, v_ev_w_out, v_s5_lam_re, v_s5_lam_im, v_s5_log_dt, v_s5_b_re, v_s5_b_im, v_s5_c_re, v_s5_c_im, v_s5_d, v_s5_w_glu, v_s5_b_glu, v_gm_w_s, v_gm_b_s, v_gm_v_g, v_od_w_in, v_od_conv_w, v_od_conv_b, v_od_w_out, v_ffn_w_up, v_ffn_conv_w, v_ffn_conv_b, v_ffn_w_down):
    loc = dict(locals())
    w = {n: loc[n] for n in WEIGHTS}
    mom = {n: loc["m_" + n] for n in WEIGHTS}
    var = {n: loc["v_" + n] for n in WEIGHTS}

    B, S, D = x.shape
    T = B * S
    SW = s5_d.shape[1]
    G = SW // SSM_GROUP
    NS = G * SSM_STATE
    NB = NS // LANES
    tm = min(512, S)
    tt = min(1024, T)
    c_idx = lax.axis_index("c").astype(jnp.int32).reshape(1)
    k_idx = (2 * lax.axis_index("x") + lax.axis_index("y")).astype(jnp.int32).reshape(1)
    shards = _weight_shards(w)
    placed = {n: _place_shard("place_" + n, s, kd, k_idx) for n, (s, kd) in shards.items()}

    def gather(names):
        return _gather_phase([shards[n][0] for n in names], [placed[n] for n in names], [shards[n][1] for n in names])

    (w_ev_in,) = _run_phase("comm_gather_ev_in", gather(['ev_w_in']))
    w_ev_in = jnp.swapaxes(w_ev_in, 0, 1).reshape(D, -1)

    h0 = x.reshape(T, D)
    (y0, p0), (w_ev_out, w_glu, conv) = _norm_mm("ev_in", h0, mix_norm_g[0], w_ev_in, tm,
                                                 phase=gather(['ev_w_out', 's5_w_glu', 'conv']))
    full, off = {}, 0
    for n in GATHER_F32:
        full[n] = _join_chips(conv.reshape(N_CHIPS, -1)[:, off:off + w[n].size], w[n].shape, SHARD_AXIS[n])
        off += w[n].size
    PW = p0.shape[1]
    p03 = p0.reshape(B, S, PW)
    lr, li, ldt = s5_lam_re[0], s5_lam_im[0], s5_log_dt[0].reshape(G, 1)
    ar, ai, zr, zi = _s5_prep(lr, li, ldt)
    bre = _block_diag(jnp.swapaxes(s5_b_re[0], 1, 2))
    bim = _block_diag(jnp.swapaxes(s5_b_im[0], 1, 2))
    cbr = _block_diag(jnp.swapaxes(s5_c_re[0], 1, 2)).astype(bf16)
    cbi = _block_diag(jnp.swapaxes(s5_c_im[0], 1, 2)).astype(bf16)
    zr_row, zi_row = zr.reshape(1, NS), zi.reshape(1, NS)
    bbd = _s5_bbd(zr_row, zi_row, bre, bim)
    ar_s, ai_s = ar.reshape(NB, 1, LANES), ai.reshape(NB, 1, LANES)
    xr, xi = _s5_in(p03, bbd, SW, tm)
    (hr, hi), (w_up0, w_down0) = _s5_scan("s5_scan", xr, xi, ar_s, ai_s, False,
                                           phase=gather(['ffn_w_up0', 'ffn_w_down0']))
    dsk, bglu = s5_d.reshape(1, SW), s5_b_glu.reshape(1, SW)
    a_out = _s5_out(hr, hi, p03, cbr, cbi, dsk, w_glu, bglu, tm)
    ws, bst, gv = gm_w_s[0], gm_b_s[0].T, gm_v_g.reshape(1, -1)
    mixcat = _gmlp(p0, a_out.reshape(T, SW), ws, bst, gv, SW)
    h1 = _mm_resid("ev_out", mixcat, w_ev_out, h0, tm)

    def ffn_fwd(l, h, w_up, w_down, up_phase=None, down_phase=None, head=None):
        res = _norm_mm(f"ffn_up{l}", h, ffn_norm_g[l], w_up, tm, phase=up_phase)
        (z, up), got_up = res if up_phase is not None else (res, None)
        res = _ffn_down(f"ffn_down{l}", up, full['ffn_conv_w'][l], ffn_conv_b[l].reshape(1, -1), w_down, h, S, tm,
                        phase=down_phase, head=head)
        (*hn, c), got_down = res if down_phase is not None else (res, None)
        return hn, (z, up.reshape(B, S, -1), c.reshape(B, S, -1)), got_up, got_down

    (h2,), ffn0, (w_up1, w_down1), (w_od_in, w_od_out) = ffn_fwd(
        0, h1, w_up0, w_down0, gather(['ffn_w_up1', 'ffn_w_down1']), gather(['od_w_in', 'od_w_out']))
    w_ups, w_downs = (w_up0, w_up1), (w_down0, w_down1)
    od_cw, od_cb = full['od_conv_w'][0], full['od_conv_b']
    y1, p1 = _norm_mm("od_in", h2, mix_norm_g[1], w_od_in, tm)
    p13 = p1.reshape(B, S, -1)
    sc = _od_act(p13, od_cw, od_cb)
    h3 = _mm_resid("od_out", sc.reshape(T, D), w_od_out, h2, tm)
    (dh4, dh4b, loss_part, d_final_g), ffn1, _, _ = ffn_fwd(
        1, h3, w_up1, w_down1, head=(final_norm_g, loss_target.reshape(T, D)))

    loss = lax.psum(loss_part[0, 0], ("x", "y", "c"))

    grads = {}

    halves = {}
    chips = lambda g: g.reshape(N_CHIPS, -1, D)

    def pair_sums(names, parts, recv):
        return [_pair_sum(f"pair_sum_{n}", g, r, c_idx, f32 if n == "small" else bf16)
                for n, g, r in zip(names, parts, recv)]

    def reduce_end(tag, names, hsum, r3):
        mine = [_chip_sum(f"chip_sum_{n}", r, h, k_idx) for n, r, h in zip(names, r3, hsum)]
        theirs = _comm_pair_share(tag, mine)
        halves.update({n: (a, b) for n, a, b in zip(names, mine, theirs)})

    def ffn_bwd(l, dh, dhb, h_in, saved, phase=None, swap=False):
        z, up3, c3 = saved
        w_down, w_up = w_downs[l], w_ups[l]
        da = _mm_nt(f"ffn_down_bwd{l}", dhb, w_down, tm)
        res = _ffn_act_bwd(f"ffn_act_bwd{l}", up3, c3, da.reshape(B, S, -1), full['ffn_conv_w'][l], phase=phase)
        (act, dg3, dv3, dcwg, dcwv, dcbg, dcbv), got = res if phase is not None else (res, None)
        g_down = _mm_tn(f"ffn_down_dw{l}", act.reshape(T, -1), dhb, tt)
        dupg, dupv = dg3.reshape(T, -1), dv3.reshape(T, -1)
        F = dupg.shape[1]
        g_up = _mm_tn(f"ffn_up_dw{l}_gate", dupg, z, tt, rows=2 * F)
        g_up = _mm_tn(f"ffn_up_dw{l}_val", dupv, z, tt, rows=2 * F, row_off=F, prev=g_up)
        parts = [chips(g_down), chips(g_up)]
        res = _mm_nt_normbwd(f"ffn_up_bwd{l}", [dupg, dupv], w_up, h_in, ffn_norm_g[l], dh, tm,
                             phase=_swap_phase(parts) if swap else None)
        (dh_new, dhb_new, dg), recv = res if swap else (res, None)
        F = dg3.shape[2]
        dcw = jnp.concatenate([dcwg[:, :F], dcwv[:, :F]], axis=1)
        dcb = jnp.concatenate([dcbg[:, :F], dcbv[:, :F]], axis=1)
        return dh_new, dhb_new, g_down, g_up, dcw, dcb[0], dg[0], got, parts, recv

    dh3, dh3b, gd1, gu1, gcw1, gcb1, gng1, _, _, _ = ffn_bwd(1, dh4, dh4b, h3, ffn1)
    dsc = _mm_nt("od_out_bwd", dh3b, w_od_out, tm)
    g_od_out = _mm_tn("od_out_dw", sc.reshape(T, D), dh3b, tt)
    dbg3, dcg3, dhx3, d_od_cw, d_od_cb = _od_act_bwd(p13, dsc.reshape(B, S, D), od_cw, od_cb)
    dp1 = [t.reshape(T, D) for t in (dbg3, dcg3, dhx3)]
    g_od_in = None
    for i, piece in enumerate(dp1):
        g_od_in = _mm_tn(f"od_in_dw{i}", piece, y1, tt, rows=3 * D, row_off=i * D, prev=g_od_in)
    grads['od_conv_w'] = d_od_cw[None]
    grads['od_conv_b'] = d_od_cb
    layer1 = ['ffn_w_down1', 'ffn_w_up1', 'od_w_out', 'od_w_in']
    parts1 = [chips(g) for g in (gd1, gu1, g_od_out, g_od_in)]
    (dh2, dh2b, gmix1), recv1 = _mm_nt_normbwd("od_in_bwd", dp1, w_od_in, h2, mix_norm_g[1], dh3, tm,
                                               phase=_swap_phase(parts1))
    hsum1 = pair_sums(layer1, parts1, recv1)
    dh1, dh1b, gd0, gu0, gcw0, gcb0, gng0, r3, parts0, recv0 = ffn_bwd(
        0, dh2, dh2b, h1, ffn0, phase=_exchange_phase(hsum1), swap=True)
    reduce_end("layer1", layer1, hsum1, r3)
    ffn0_names = ['ffn_w_down0', 'ffn_w_up0']
    hsum0 = pair_sums(ffn0_names, parts0, recv0)
    grads['ffn_conv_w'] = jnp.stack([gcw0, gcw1])
    grads['ffn_conv_b'] = jnp.stack([gcb0, gcb1])
    grads['ffn_norm_g'] = jnp.stack([gng0, gng1])
    grads['final_norm_g'] = d_final_g[0]

    dmix = _mm_nt("ev_out_bwd", dh1b, w_ev_out, tm)
    g_ev_out = _mm_tn("ev_out_dw", mixcat, dh1b, tt)
    part_evo = [chips(g_ev_out)]
    (duv, d_ws, d_bs, d_gv), recv_evo = _gmlp_bwd(p0, dmix, ws, bst, gv, SW, phase=_swap_phase(part_evo))
    hsum0 = hsum0 + pair_sums(['ev_w_out'], part_evo, recv_evo)
    ffn0_names = ffn0_names + ['ev_w_out']
    grads['gm_w_s'] = d_ws[None]
    grads['gm_b_s'] = d_bs[:, :, 0][None]
    grads['gm_v_g'] = d_gv
    dhr, dhi, du_skip, d_cbr, d_cbi, d_dsk, d_wglu, d_bglu = _s5_out_bwd(
        hr, hi, p03, dmix.reshape(B, S, D), cbr, cbi, dsk, w_glu, bglu, tm)
    grads['s5_c_re'] = jnp.swapaxes(_diag_blocks(d_cbr, G), 1, 2)[None]
    grads['s5_c_im'] = jnp.swapaxes(_diag_blocks(d_cbi, G), 1, 2)[None]
    grads['s5_d'] = d_dsk
    grads['s5_w_glu'] = d_wglu[None]
    grads['s5_b_glu'] = d_bglu
    (gr, gi, dar, dai), r3 = _s5_scan("s5_rscan", dhr, dhi, ar_s, ai_s, True, hr, hi, phase=_exchange_phase(hsum0))
    reduce_end("ffn0", ffn0_names, hsum0, r3)
    dp03, d_bbd = _s5_in_bwd(gr, gi, p03, bbd, du_skip, duv.reshape(B, S, -1), tm)
    d_bre, d_bim, d_zr, d_zi = _s5_bbd_bwd(d_bbd, zr_row, zi_row, bre, bim)
    grads['s5_b_re'] = jnp.swapaxes(_diag_blocks(d_bre, G), 1, 2)[None]
    grads['s5_b_im'] = jnp.swapaxes(_diag_blocks(d_bim, G), 1, 2)[None]
    shp = (-1, G, SSM_STATE)
    d_lr, d_li, d_ldt = _s5_prep_bwd(lr, li, ldt, dar.reshape(shp), dai.reshape(shp), d_zr.reshape(shp),
                                     d_zi.reshape(shp))
    grads['s5_lam_re'] = d_lr[None]
    grads['s5_lam_im'] = d_li[None]
    grads['s5_log_dt'] = d_ldt.reshape(1, G)
    dp0 = dp03.reshape(T, PW)
    g_ev_in = _mm_tn("ev_in_dw", dp0, y0, tt)
    grad_x, _, gmix0 = _mm_nt_normbwd("ev_in_bwd", [dp0], w_ev_in, h0, mix_norm_g[0], dh1, tm)
    grads['mix_norm_g'] = jnp.concatenate([gmix0, gmix1], axis=0)

    small = [n for n in WEIGHTS if n not in BIG]
    segs = []
    for n in small:
        gfull = grads[n].astype(f32)
        if n in SHARD_AXIS:
            segs.append(_split_chips(gfull, SHARD_AXIS[n]))
        else:
            segs.append(jnp.broadcast_to(gfull.reshape(1, -1), (N_CHIPS, gfull.size)))
    unit = 2 * SUBLANES * D
    gsmall = _pad_rows(jnp.concatenate(segs, axis=1), unit).reshape(N_CHIPS, -1, D)
    mixer0 = ['ev_w_in', 'small']
    parts = [chips(g_ev_in), gsmall]
    hsum = pair_sums(mixer0, parts, _run_phase("comm_pair_swap_mixer0", _swap_phase(parts)))
    reduce_end("mixer0", mixer0, hsum, _run_phase("comm_exchange_mixer0", _exchange_phase(hsum)))

    out_g, out_d, out_m, out_v = {}, {}, {}, {}

    def update(n, key, lead, transposed, prev=None):
        res = _adamw(f"adamw_{key}", *halves[key], c_idx, w[n], mom[n], var[n], lead, transposed, prev)
        out_g[n], out_d[n], out_m[n], out_v[n] = res
        return res

    update('ev_w_in', 'ev_w_in', 0, True)
    update('ev_w_out', 'ev_w_out', 0, False)
    update('od_w_in', 'od_w_in', 0, True)
    update('od_w_out', 'od_w_out', 0, False)
    update('ffn_w_up', 'ffn_w_up0', 0, True, prev=update('ffn_w_up', 'ffn_w_up1', 1, True))
    update('ffn_w_down', 'ffn_w_down0', 0, False, prev=update('ffn_w_down', 'ffn_w_down1', 1, False))

    mine, theirs = halves['small']
    first = lax.axis_index("c") == 0
    flat = jnp.concatenate([jnp.where(first, mine, theirs), jnp.where(first, theirs, mine)], axis=1).reshape(-1)
    off = 0
    for n in small:
        out_g[n] = flat[off:off + w[n].size].reshape(w[n].shape)
        off += w[n].size
    res = _adamw_small([out_g[n] for n in small], [w[n] for n in small], [mom[n] for n in small],
                       [var[n] for n in small])
    for i, n in enumerate(small):
        out_d[n], out_m[n], out_v[n] = res[i], res[len(small) + i], res[2 * len(small) + i]

    return (loss, grad_x.reshape(B, S, D), *[out_g[n] for n in WEIGHTS], *[out_d[n] for n in WEIGHTS],
            *[out_m[n] for n in WEIGHTS], *[out_v[n] for n in WEIGHTS])
```

```python
import functools
import math

import jax
import jax.numpy as jnp
from jax import lax
from jax.experimental import pallas as pl
from jax.experimental.pallas import tpu as pltpu

f32 = jnp.float32
bf16 = jnp.bfloat16
MESH = pl.DeviceIdType.MESH

SSM_GROUP = 16
SSM_STATE = 64
GMLP_HEAD = 128
CHUNK = 128
EPS = 1e-6
LAMBDA_RE_MAX = -1e-4
ADAM_LR, ADAM_B1, ADAM_B2, ADAM_EPS, ADAM_WD, ADAM_STEP = 0.001, 0.9, 0.999, 1e-08, 0.01, 10

LANES = 128
SUBLANES = 8
NSUB = 32
HALO = 16
VMEM_LIMIT = 56 * 1024 * 1024
N_CHIPS = 4

WEIGHTS = ['mix_norm_g', 'ffn_norm_g', 'final_norm_g', 'ev_w_in', 'ev_w_out', 's5_lam_re', 's5_lam_im', 's5_log_dt',
           's5_b_re', 's5_b_im', 's5_c_re', 's5_c_im', 's5_d', 's5_w_glu', 's5_b_glu', 'gm_w_s', 'gm_b_s', 'gm_v_g',
           'od_w_in', 'od_conv_w', 'od_conv_b', 'od_w_out', 'ffn_w_up', 'ffn_conv_w', 'ffn_conv_b', 'ffn_w_down']
SHARD_AXIS = {'ev_w_in': 2, 'ev_w_out': 1, 's5_w_glu': 1, 'od_w_in': 2, 'od_conv_w': 2, 'od_conv_b': 1, 'od_w_out': 1,
              'ffn_w_up': 2, 'ffn_conv_w': 2, 'ffn_w_down': 1}
GATHER_F32 = ['od_conv_w', 'od_conv_b', 'ffn_conv_w']

_GELU_K0 = math.sqrt(2.0 / math.pi)
_GELU_K1 = 0.044715
NT = (((1,), (1,)), ((), ()))
TN = (((0,), (0,)), ((), ()))


def _pick(n, cap):
    if n <= cap:
        return n
    best = None
    for d in range(LANES, cap + 1, LANES):
        if n % d == 0:
            best = d
    assert best is not None, (n, cap)
    return best


def _params(sem=None):
    return pltpu.CompilerParams(dimension_semantics=sem, vmem_limit_bytes=VMEM_LIMIT)


class _Phase:
    def __init__(self, ins, inplace, outs, sems, start, finish):
        self.ins, self.inplace, self.outs, self.sems = list(ins), list(inplace), list(outs), list(sems)
        self.start, self.finish = start, finish


def _call(body, name, grid, in_specs, out_specs, out_shape, args, scratch=(), sem=None, phase=None):
    if phase is None:
        return pl.pallas_call(body, name=name, grid=grid, in_specs=in_specs, out_specs=out_specs, out_shape=out_shape,
                              scratch_shapes=list(scratch), compiler_params=_params(sem))(*args)
    any_spec = pl.BlockSpec(memory_space=pl.ANY)
    n_in, n_out, n_scr = len(args), len(out_shape), len(scratch)
    p_in = phase.ins + phase.inplace
    ci, co = len(p_in), len(phase.outs)

    def wrapped(*refs):
        ins, cins = refs[:n_in], refs[n_in:n_in + len(phase.ins)]
        b = n_in + ci
        outs, couts = refs[b:b + n_out], refs[b + n_out:b + n_out + co]
        d = b + n_out + co
        scr, csem = refs[d:d + n_scr], refs[d + n_scr:]
        ids = [pl.program_id(i) for i in range(len(grid))]
        first = functools.reduce(jnp.logical_and, [i == 0 for i in ids])
        last = functools.reduce(jnp.logical_and, [i == g - 1 for i, g in zip(ids, grid)])

        @pl.when(first)
        def _():
            phase.start(cins, couts, csem)
        body(*ins, *outs, *scr)

        @pl.when(last)
        def _():
            phase.finish(cins, couts, csem)

    res = pl.pallas_call(
        wrapped, name=name, grid=grid, in_specs=list(in_specs) + [any_spec] * ci,
        out_specs=list(out_specs) + [any_spec] * co, out_shape=list(out_shape) + phase.outs,
        scratch_shapes=list(scratch) + phase.sems,
        input_output_aliases={n_in + len(phase.ins) + i: n_out + i for i in range(len(phase.inplace))},
        compiler_params=_params(tuple("arbitrary" for _ in grid)))(*args, *p_in)
    return res[:n_out], res[n_out:]


def _run_phase(name, phase):
    any_spec = pl.BlockSpec(memory_space=pl.ANY)
    ni, ci, co = len(phase.ins), len(phase.ins) + len(phase.inplace), len(phase.outs)

    def body(*refs):
        cins, couts, csem = refs[:ni], refs[ci:ci + co], refs[ci + co:]
        phase.start(cins, couts, csem)
        phase.finish(cins, couts, csem)

    return pl.pallas_call(
        body, name=name, in_specs=[any_spec] * ci, out_specs=[any_spec] * co, out_shape=phase.outs,
        scratch_shapes=phase.sems, input_output_aliases={ni + i: i for i in range(len(phase.inplace))})(
            *phase.ins, *phase.inplace)


def _gelu(x):
    return 0.5 * x * (1.0 + jnp.tanh(_GELU_K0 * (x + _GELU_K1 * x * x * x)))


def _gelu_grad(x):
    t = jnp.tanh(_GELU_K0 * (x + _GELU_K1 * x * x * x))
    return 0.5 * (1.0 + t) + 0.5 * x * (1.0 - t * t) * _GELU_K0 * (1.0 + 3.0 * _GELU_K1 * x * x)


def _rms_stats(x):
    r = lax.rsqrt(jnp.mean(x * x, axis=-1, keepdims=True) + EPS)
    return x * r, r


def _rms_bwd(dy, xh, r, g):
    dxh = dy * g
    dx = r * (dxh - xh * jnp.mean(dxh * xh, axis=-1, keepdims=True))
    return dx, jnp.sum(dy * xh, axis=0, keepdims=True)


def _dot(a, b):
    return jnp.dot(a, b, preferred_element_type=f32)


def _dg(a, b, dims):
    return lax.dot_general(a, b, dims, preferred_element_type=f32)


def _row_fold(z):
    return z.reshape(z.shape[0] // SUBLANES, SUBLANES, z.shape[1]).sum(axis=0)


def _norm_mm(name, h, g, w, tm, phase=None):
    T, D = h.shape
    N = w.shape[1]
    nc = _pick(N, 512)

    def body(h_ref, g_ref, w_ref, y_ref, o_ref):
        xh, _ = _rms_stats(h_ref[...])
        y = (xh * g_ref[...]).astype(bf16)
        y_ref[...] = y
        for j in range(N // nc):
            o_ref[:, j * nc:(j + 1) * nc] = _dot(y, w_ref[:, j * nc:(j + 1) * nc]).astype(bf16)

    return _call(
        body, name, (T // tm,),
        [pl.BlockSpec((tm, D), lambda i: (i, 0)), pl.BlockSpec((1, D), lambda i: (0, 0)),
         pl.BlockSpec((D, N), lambda i: (0, 0))],
        [pl.BlockSpec((tm, D), lambda i: (i, 0)), pl.BlockSpec((tm, N), lambda i: (i, 0))],
        [jax.ShapeDtypeStruct((T, D), bf16), jax.ShapeDtypeStruct((T, N), bf16)],
        [h, g.reshape(1, D), w], sem=("parallel",), phase=phase)


def _mm_resid(name, a, w, resid, tm, phase=None):
    T, K = a.shape
    N = w.shape[1]

    def body(a_ref, w_ref, r_ref, o_ref):
        o_ref[...] = r_ref[...] + _dot(a_ref[...], w_ref[...])

    return _call(
        body, name, (T // tm,),
        [pl.BlockSpec((tm, K), lambda i: (i, 0)), pl.BlockSpec((K, N), lambda i: (0, 0)),
         pl.BlockSpec((tm, N), lambda i: (i, 0))],
        [pl.BlockSpec((tm, N), lambda i: (i, 0))], [jax.ShapeDtypeStruct((T, N), f32)],
        [a, w, resid], sem=("parallel",), phase=phase)


def _mm_nt(name, dy, w, tm):
    T, N = dy.shape
    K = w.shape[0]
    kc = _pick(K, 512)

    def body(d_ref, w_ref, o_ref):
        d = d_ref[...].astype(bf16)
        for j in range(K // kc):
            o_ref[:, j * kc:(j + 1) * kc] = _dg(d, w_ref[j * kc:(j + 1) * kc, :], NT).astype(bf16)

    return pl.pallas_call(
        body, name=name, grid=(T // tm,),
        in_specs=[pl.BlockSpec((tm, N), lambda i: (i, 0)), pl.BlockSpec((K, N), lambda i: (0, 0))],
        out_specs=pl.BlockSpec((tm, K), lambda i: (i, 0)),
        out_shape=jax.ShapeDtypeStruct((T, K), bf16),
        compiler_params=_params(("parallel",)))(dy, w)


def _mm_nt_normbwd(name, dys, w, h, g, dh_in, tm, phase=None):
    n = len(dys)
    T = dys[0].shape[0]
    D = w.shape[0]
    widths = [d.shape[1] for d in dys]
    offs = [sum(widths[:i]) for i in range(n)]

    def body(*refs):
        d_refs = refs[:n]
        w_ref, h_ref, g_ref, dh_ref, o_ref, ob_ref, dg_ref = refs[n:]
        dz = _dg(d_refs[0][...], w_ref[:, :widths[0]], NT)
        for i in range(1, n):
            dz += _dg(d_refs[i][...], w_ref[:, offs[i]:offs[i] + widths[i]], NT)
        xh, r = _rms_stats(h_ref[...])
        dx, dg = _rms_bwd(dz, xh, r, g_ref[...])
        out = dh_ref[...] + dx
        o_ref[...] = out
        ob_ref[...] = out.astype(bf16)

        @pl.when(pl.program_id(0) == 0)
        def _():
            dg_ref[...] = jnp.zeros_like(dg_ref)
        dg_ref[...] += dg

    row = lambda c: pl.BlockSpec((tm, c), lambda i: (i, 0))
    return _call(
        body, name, (T // tm,),
        [row(c) for c in widths] + [pl.BlockSpec((D, sum(widths)), lambda i: (0, 0)), row(D),
                                    pl.BlockSpec((1, D), lambda i: (0, 0)), row(D)],
        [row(D), row(D), pl.BlockSpec((1, D), lambda i: (0, 0))],
        [jax.ShapeDtypeStruct((T, D), f32), jax.ShapeDtypeStruct((T, D), bf16), jax.ShapeDtypeStruct((1, D), f32)],
        [*dys, w, h, g.reshape(1, D), dh_in], sem=("arbitrary",), phase=phase)


def _mm_tn(name, a, b, tt, rows=None, row_off=0, prev=None):
    T, K = a.shape
    N = b.shape[1]
    rows = K if rows is None else rows
    tk = _pick(K, 1408)
    tn = _pick(N, 1024)
    assert row_off % tk == 0
    kb = row_off // tk

    def body(a_ref, b_ref, *rest):
        o_ref = rest[-1]

        @pl.when(pl.program_id(2) == 0)
        def _():
            o_ref[...] = jnp.zeros_like(o_ref)
        o_ref[...] += _dg(a_ref[...], b_ref[...], TN)

    in_specs = [pl.BlockSpec((tt, tk), lambda k, n, t: (t, k)), pl.BlockSpec((tt, tn), lambda k, n, t: (t, n))]
    args, aliases = [a, b], {}
    if prev is not None:
        in_specs.append(ANY)
        args.append(prev)
        aliases = {2: 0}
    return pl.pallas_call(
        body, name=name, grid=(K // tk, N // tn, T // tt), in_specs=in_specs,
        out_specs=pl.BlockSpec((tk, tn), lambda k, n, t: (k + kb, n)),
        out_shape=jax.ShapeDtypeStruct((rows, N), f32), input_output_aliases=aliases,
        compiler_params=_params(("parallel", "parallel", "arbitrary")))(*args)


def _loss_head(h, g_ref, t_ref, dh_ref, dhb_ref, loss_ref, dg_ref):
    D = h.shape[1]
    xh, r = _rms_stats(h)
    gg = g_ref[...]
    diff = xh * gg - t_ref[...]
    dx, dg = _rms_bwd(diff * (1.0 / D), xh, r, gg)
    dh_ref[...] = dx
    dhb_ref[...] = dx.astype(bf16)

    @pl.when(pl.program_id(0) == 0)
    def _():
        dg_ref[...] = jnp.zeros_like(dg_ref)
        loss_ref[...] = jnp.zeros_like(loss_ref)
    dg_ref[...] += dg
    loss_ref[...] += (0.5 / D) * jnp.sum(jnp.sum(diff * diff, axis=1, keepdims=True), axis=0, keepdims=True)


def _taps(load, r0, R):
    main = load(r0, R)
    hs = pl.multiple_of(jnp.maximum(r0 - HALO, 0), HALO)
    halo = load(hs, HALO) * (r0 > 0).astype(f32)
    ext = jnp.concatenate([halo, main], axis=0)
    xm1 = pltpu.roll(ext, 1, 0)[HALO:]
    xm2 = pltpu.roll(ext, 2, 0)[HALO:]
    return xm2, xm1, main


def _conv(w, b, taps):
    return b + w[0:1] * taps[0] + w[1:2] * taps[1] + w[2:3] * taps[2]


def _ffn_down(name, up, cw, cb, w_down, resid, S, tm, phase=None, head=None):
    T, F2 = up.shape
    F = F2 // 2
    D = w_down.shape[1]
    cwid = _pick(F, 256)
    per_seq = S // tm

    def body(u_ref, halo_ref, cw_ref, cb_ref, w_ref, r_ref, *rest):
        c_ref = rest[-1]
        keep = (pl.program_id(0) % per_seq > 0).astype(f32)

        def conv(off):
            cols = slice(off, off + cwid)
            main = u_ref[:, cols].astype(f32)
            ext = jnp.concatenate([halo_ref[:, cols].astype(f32) * keep, main], axis=0)
            taps = (pltpu.roll(ext, 2, 0)[HALO:], pltpu.roll(ext, 1, 0)[HALO:], main)
            return _conv(cw_ref[:, cols], cb_ref[:, cols], taps)

        acc = r_ref[...]
        for j in range(F // cwid):
            cg, cv = conv(j * cwid), conv(F + j * cwid)
            c_ref[:, j * cwid:(j + 1) * cwid] = cg.astype(bf16)
            c_ref[:, F + j * cwid:F + (j + 1) * cwid] = cv.astype(bf16)
            a = (cg * jax.nn.sigmoid(cg) * cv).astype(bf16)
            acc = acc + _dot(a, w_ref[j * cwid:(j + 1) * cwid, :])
        if head is None:
            rest[0][...] = acc
        else:
            _loss_head(acc, *rest[:-1])

    full = lambda r, c: pl.BlockSpec((r, c), lambda i: (0, 0))
    row = lambda c: pl.BlockSpec((tm, c), lambda i: (i, 0))
    in_specs = [row(F2), pl.BlockSpec((HALO, F2), lambda i: (jnp.maximum(i * (tm // HALO) - 1, 0), 0)),
                full(3, F2), full(1, F2), full(F, D), row(D)]
    args = [up, up, cw, cb, w_down, resid]
    if head is None:
        out_specs, out_shape = [row(D)], [jax.ShapeDtypeStruct((T, D), f32)]
    else:
        in_specs += [full(1, D), row(D)]
        args += [head[0].reshape(1, D), head[1]]
        out_specs = [row(D), row(D), full(1, 1), full(1, D)]
        out_shape = [jax.ShapeDtypeStruct((T, D), f32), jax.ShapeDtypeStruct((T, D), bf16),
                     jax.ShapeDtypeStruct((1, 1), f32), jax.ShapeDtypeStruct((1, D), f32)]
    return _call(body, name, (T // tm,), in_specs, out_specs + [row(F2)],
                 out_shape + [jax.ShapeDtypeStruct((T, F2), bf16)], args,
                 sem=("parallel",) if head is None else ("arbitrary",), phase=phase)


def _rev_conv_rows(d, nxt, w):
    R = d.shape[0]
    ext = jnp.concatenate([d, nxt], axis=0)
    n = R + HALO
    xp1 = pltpu.roll(ext, n - 1, 0)[:R]
    xp2 = pltpu.roll(ext, n - 2, 0)[:R]
    return w[2:3] * d + w[1:2] * xp1 + w[0:1] * xp2, xp1, xp2


def _conv_grad_acc(acc, dc, taps):
    return (acc[0] + _row_fold(dc * taps[0]), acc[1] + _row_fold(dc * taps[1]), acc[2] + _row_fold(dc * taps[2]),
            acc[3] + _row_fold(dc))


def _conv_grad_out(dcw_ref, dcb_ref, acc):
    @pl.when(pl.program_id(1) == 0)
    def _():
        dcw_ref[...] = jnp.zeros_like(dcw_ref)
        dcb_ref[...] = jnp.zeros_like(dcb_ref)
    for k in range(3):
        dcw_ref[k:k + 1, :] += jnp.sum(acc[k], axis=0, keepdims=True)
    dcb_ref[...] += jnp.sum(acc[3], axis=0, keepdims=True)


def _ffn_act_bwd(name, up3, c3, da3, cw, phase=None):
    B, S, F2 = up3.shape
    F = F2 // 2
    cwid = _pick(F, 256)
    nF = F // cwid
    R = min(256, S)
    nR = S // R

    def body(xg_ref, xv_ref, cg_ref, cv_ref, da_ref, wg_ref, wv_ref,
             act_ref, dg_ref, dv_ref, dcwg_ref, dcwv_ref, dcbg_ref, dcbv_ref, sum_scr):
        wg, wv = wg_ref[...], wv_ref[...]

        def half(d, nxt, w, x_ref, rows, acc, out_ref):
            out, xp1, xp2 = _rev_conv_rows(d, nxt, w)
            out_ref[rows, :] = out.astype(bf16)
            x = x_ref[rows, :].astype(f32)
            return (acc[0] + _row_fold(xp2 * x), acc[1] + _row_fold(xp1 * x), acc[2] + _row_fold(d * x),
                    acc[3] + _row_fold(d))

        def step(i, carry):
            ng, nv, accg, accv = carry
            rows = pl.ds(pl.multiple_of((nR - 1 - i) * R, R), R)
            cg, cv = cg_ref[rows, :].astype(f32), cv_ref[rows, :].astype(f32)
            da = da_ref[rows, :].astype(f32)
            sg = jax.nn.sigmoid(cg)
            act_ref[rows, :] = (cg * sg * cv).astype(bf16)
            dgate = da * cv * (sg * (1.0 + cg * (1.0 - sg)))
            dval = da * (cg * sg)
            accg = half(dgate, ng, wg, xg_ref, rows, accg, dg_ref)
            accv = half(dval, nv, wv, xv_ref, rows, accv, dv_ref)
            return dgate[:HALO], dval[:HALO], accg, accv
        z = jnp.zeros((SUBLANES, cwid), f32)
        zh = jnp.zeros((HALO, cwid), f32)
        _, _, accg, accv = lax.fori_loop(0, nR, step, (zh, zh, (z, z, z, z), (z, z, z, z)))
        j = pl.program_id(1)
        for half_i, (acc, dcw_ref, dcb_ref) in enumerate(((accg, dcwg_ref, dcbg_ref), (accv, dcwv_ref, dcbv_ref))):
            @pl.when(pl.program_id(0) == 0)
            def _():
                sum_scr[half_i, j] = jnp.zeros((SUBLANES, cwid), f32)
            for k in range(4):
                sum_scr[half_i, j, k:k + 1, :] += jnp.sum(acc[k], axis=0, keepdims=True)
            dcw_ref[...] = sum_scr[half_i, j, 0:3, :]
            dcb_ref[...] = sum_scr[half_i, j, 3:4, :]

    blk = lambda off: pl.BlockSpec((None, S, cwid), lambda b, j: (b, 0, off + j))
    wblk = lambda off: pl.BlockSpec((3, cwid), lambda b, j: (0, off + j))
    sums = lambda r: pl.BlockSpec((r, cwid), lambda b, j: (0, jnp.where(b == B - 1, j, nF)))
    half_shape = jax.ShapeDtypeStruct((B, S, F), bf16)
    return _call(
        body, name, (B, nF),
        [blk(0), blk(nF), blk(0), blk(nF), blk(0), wblk(0), wblk(nF)],
        [blk(0), blk(0), blk(0), sums(3), sums(3), sums(1), sums(1)],
        [half_shape, half_shape, half_shape, jax.ShapeDtypeStruct((3, F + cwid), f32),
         jax.ShapeDtypeStruct((3, F + cwid), f32), jax.ShapeDtypeStruct((1, F + cwid), f32),
         jax.ShapeDtypeStruct((1, F + cwid), f32)],
        [up3, up3, c3, c3, da3, cw, cw], scratch=[pltpu.VMEM((2, nF, SUBLANES, cwid), f32)],
        sem=("arbitrary", "arbitrary"), phase=phase)


def _od_out(p, cw, cb, w_out, resid, S, tm):
    T, D3 = p.shape
    D = D3 // 3
    cwid = _pick(D, 256)
    per_seq = S // tm

    def body(p_ref, halo_ref, cw_ref, cb_ref, w_ref, r_ref, o_ref):
        keep = (pl.program_id(0) % per_seq > 0).astype(f32)
        acc = r_ref[...]
        for j in range(D // cwid):
            cols = [slice(part * D + j * cwid, part * D + (j + 1) * cwid) for part in range(3)]
            q = p_ref[:, cols[1]].astype(f32) * p_ref[:, cols[2]].astype(f32)
            q_halo = halo_ref[:, cols[1]].astype(f32) * halo_ref[:, cols[2]].astype(f32) * keep
            ext = jnp.concatenate([q_halo, q], axis=0)
            taps = (pltpu.roll(ext, 2, 0)[HALO:], pltpu.roll(ext, 1, 0)[HALO:], q)
            cq = _conv(cw_ref[:, cols[0]], cb_ref[:, cols[0]], taps)
            sc = (p_ref[:, cols[0]].astype(f32) * cq).astype(bf16)
            acc = acc + _dot(sc, w_ref[j * cwid:(j + 1) * cwid, :])
        o_ref[...] = acc

    full = lambda r, c: pl.BlockSpec((r, c), lambda i: (0, 0))
    row = lambda c: pl.BlockSpec((tm, c), lambda i: (i, 0))
    return pl.pallas_call(
        body, name="od_out", grid=(T // tm,),
        in_specs=[row(D3), pl.BlockSpec((HALO, D3), lambda i: (jnp.maximum(i * (tm // HALO) - 1, 0), 0)),
                  full(3, D), full(1, D), full(D, D), row(D)],
        out_specs=row(D), out_shape=jax.ShapeDtypeStruct((T, D), f32),
        compiler_params=_params(("parallel",)))(p, p, cw, cb, w_out, resid)


def _od_act_bwd(p3, dsc3, cw, cb):
    B, S, D3 = p3.shape
    D = D3 // 3
    cwid = _pick(D, 256)
    nD = D // cwid
    R = min(256, S)
    nR = S // R

    def body(bg_ref, cg_ref, hx_ref, d_ref, w_ref, b_ref, sc_ref, dbg_ref, dcg_ref, dhx_ref, dcw_ref, dcb_ref):
        w, b = w_ref[...], b_ref[...]
        q = lambda s, n: cg_ref[pl.ds(s, n), :].astype(f32) * hx_ref[pl.ds(s, n), :].astype(f32)

        def step(i, carry):
            nxt, acc = carry
            r0 = pl.multiple_of((nR - 1 - i) * R, R)
            rows = pl.ds(r0, R)
            tq = _taps(q, r0, R)
            cq = _conv(w, b, tq)
            d = d_ref[rows, :].astype(f32)
            dbg_ref[rows, :] = (d * cq).astype(bf16)
            bg = bg_ref[rows, :].astype(f32)
            sc_ref[rows, :] = (bg * cq).astype(bf16)
            dcq = d * bg
            dq, _, _ = _rev_conv_rows(dcq, nxt, w)
            dcg_ref[rows, :] = (dq * hx_ref[rows, :].astype(f32)).astype(bf16)
            dhx_ref[rows, :] = (dq * cg_ref[rows, :].astype(f32)).astype(bf16)
            return dcq[:HALO], _conv_grad_acc(acc, dcq, tq)
        z = jnp.zeros((SUBLANES, cwid), f32)
        _, acc = lax.fori_loop(0, nR, step, (jnp.zeros((HALO, cwid), f32), (z, z, z, z)))
        _conv_grad_out(dcw_ref, dcb_ref, acc)

    blk = lambda off: pl.BlockSpec((None, S, cwid), lambda j, b: (b, 0, off + j))
    part = jax.ShapeDtypeStruct((B, S, D), bf16)
    return pl.pallas_call(
        body, name="od_act_bwd", grid=(nD, B),
        in_specs=[blk(0), blk(nD), blk(2 * nD), blk(0), pl.BlockSpec((3, cwid), lambda j, b: (0, j)),
                  pl.BlockSpec((1, cwid), lambda j, b: (0, j))],
        out_specs=[blk(0), blk(0), blk(0), blk(0), pl.BlockSpec((3, cwid), lambda j, b: (0, j)),
                   pl.BlockSpec((1, cwid), lambda j, b: (0, j))],
        out_shape=[part, part, part, part, jax.ShapeDtypeStruct((3, D), f32), jax.ShapeDtypeStruct((1, D), f32)],
        compiler_params=_params(("parallel", "arbitrary")))(p3, p3, p3, dsc3, cw, cb)


def _gmlp_parts(p, gv, SW, GW):
    uv = p[:, SW:].astype(f32)
    ge = _gelu(uv)
    u, v = ge[:, :GW], ge[:, GW:]
    vh, r = _rms_stats(v)
    return uv, u, vh, r, vh * gv


def _tril():
    rows = lax.broadcasted_iota(jnp.int32, (CHUNK, CHUNK), 0)
    cols = lax.broadcasted_iota(jnp.int32, (CHUNK, CHUNK), 1)
    return rows >= cols


def _chunks_per_step(T):
    return 4 if T % (4 * CHUNK) == 0 else 1


def _gmlp(p0, a_out, ws, bst, gv, SW):
    T, PW = p0.shape
    GW = (PW - SW) // 2
    H = GW // GMLP_HEAD
    D = SW + GW

    kc = _chunks_per_step(T)
    rb = kc * CHUNK

    def body(p_ref, a_ref, ws_ref, b_ref, gv_ref, o_ref):
        tri = _tril()
        o_ref[:, :SW] = a_ref[...]
        wm = [jnp.where(tri, ws_ref[hh], 0.0).astype(bf16) for hh in range(H)]
        for q in range(kc):
            rows = slice(q * CHUNK, (q + 1) * CHUNK)
            _, u, _, _, vn = _gmlp_parts(p_ref[rows, :], gv_ref[...], SW, GW)
            for hh in range(H):
                sl = slice(hh * GMLP_HEAD, (hh + 1) * GMLP_HEAD)
                gate = _dot(wm[hh], vn[:, sl].astype(bf16)) + b_ref[:, hh:hh + 1]
                o_ref[rows, SW + hh * GMLP_HEAD:SW + (hh + 1) * GMLP_HEAD] = (u[:, sl] * gate).astype(bf16)

    return pl.pallas_call(
        body, name="gmlp", grid=(T // rb,),
        in_specs=[pl.BlockSpec((rb, PW), lambda i: (i, 0)), pl.BlockSpec((rb, SW), lambda i: (i, 0)),
                  pl.BlockSpec((H, CHUNK, CHUNK), lambda i: (0, 0, 0)), pl.BlockSpec((CHUNK, H), lambda i: (0, 0)),
                  pl.BlockSpec((1, GW), lambda i: (0, 0))],
        out_specs=pl.BlockSpec((rb, D), lambda i: (i, 0)),
        out_shape=jax.ShapeDtypeStruct((T, D), bf16),
        compiler_params=_params(("parallel",)))(p0, a_out, ws, bst, gv)


def _gmlp_bwd(p0, dmix, ws, bst, gv, SW, phase=None):
    T, PW = p0.shape
    GW = (PW - SW) // 2
    H = GW // GMLP_HEAD
    D = SW + GW

    kc = _chunks_per_step(T)
    rb = kc * CHUNK

    def body(p_ref, d_ref, ws_ref, b_ref, gv_ref, duv_ref, dws_ref, dbs_ref, dgv_ref):
        gv_ = gv_ref[...]
        tri = _tril()

        @pl.when(pl.program_id(0) == 0)
        def _():
            dws_ref[...] = jnp.zeros_like(dws_ref)
            dbs_ref[...] = jnp.zeros_like(dbs_ref)
            dgv_ref[...] = jnp.zeros_like(dgv_ref)
        wm = [jnp.where(tri, ws_ref[hh], 0.0).astype(bf16) for hh in range(H)]
        for q in range(kc):
            rows = slice(q * CHUNK, (q + 1) * CHUNK)
            uv, u, vh, r, vn = _gmlp_parts(p_ref[rows, :], gv_, SW, GW)
            dout = d_ref[rows, SW:].astype(f32)
            du, dvn = [], []
            for hh in range(H):
                sl = slice(hh * GMLP_HEAD, (hh + 1) * GMLP_HEAD)
                vnh = vn[:, sl].astype(bf16)
                gate = _dot(wm[hh], vnh) + b_ref[:, hh:hh + 1]
                dgate = dout[:, sl] * u[:, sl]
                du.append(dout[:, sl] * gate)
                dgb = dgate.astype(bf16)
                dws_ref[hh] += jnp.where(tri, _dg(dgb, vnh, NT), 0.0)
                dbs_ref[hh] += jnp.broadcast_to(jnp.sum(dgate, axis=1, keepdims=True), (CHUNK, CHUNK))
                dvn.append(_dg(wm[hh], dgb, TN))
            dvn = jnp.concatenate(dvn, axis=1)
            dv, dgv = _rms_bwd(dvn, vh, r, gv_)
            dgv_ref[...] += dgv
            dge = jnp.concatenate(du + [dv], axis=1)
            duv_ref[rows, :] = (dge * _gelu_grad(uv)).astype(bf16)

    fixed = pl.BlockSpec((H, CHUNK, CHUNK), lambda i: (0, 0, 0))
    return _call(
        body, "gmlp_bwd", (T // rb,),
        [pl.BlockSpec((rb, PW), lambda i: (i, 0)), pl.BlockSpec((rb, D), lambda i: (i, 0)), fixed,
         pl.BlockSpec((CHUNK, H), lambda i: (0, 0)), pl.BlockSpec((1, GW), lambda i: (0, 0))],
        [pl.BlockSpec((rb, 2 * GW), lambda i: (i, 0)), fixed, fixed, pl.BlockSpec((1, GW), lambda i: (0, 0))],
        [jax.ShapeDtypeStruct((T, 2 * GW), bf16), jax.ShapeDtypeStruct((H, CHUNK, CHUNK), f32),
         jax.ShapeDtypeStruct((H, CHUNK, CHUNK), f32), jax.ShapeDtypeStruct((1, GW), f32)],
        [p0, dmix, ws, bst, gv], sem=("arbitrary",), phase=phase)


def _s5_disc(lr, li, ldt):
    lr = jnp.minimum(lr, LAMBDA_RE_MAX)
    dt = jnp.exp(ldt)
    mag = jnp.exp(lr * dt)
    ar = mag * jnp.cos(li * dt)
    ai = mag * jnp.sin(li * dt)
    den = lr * lr + li * li
    nr = ar - 1.0
    zr = (nr * lr + ai * li) / den
    zi = (ai * lr - nr * li) / den
    return ar, ai, zr, zi


def _s5_prep(lr, li, ldt):
    G, P = lr.shape

    def body(lr_ref, li_ref, ldt_ref, ar_ref, ai_ref, zr_ref, zi_ref):
        ar, ai, zr, zi = _s5_disc(lr_ref[...], li_ref[...], ldt_ref[...])
        ar_ref[...] = ar
        ai_ref[...] = ai
        zr_ref[...] = zr
        zi_ref[...] = zi

    s = jax.ShapeDtypeStruct((G, P), f32)
    return pl.pallas_call(body, name="s5_prep", out_shape=[s, s, s, s])(lr, li, ldt)


def _s5_prep_bwd(lr, li, ldt, dar, dai, dzr, dzi):
    G, P = lr.shape

    def body(lr_ref, li_ref, ldt_ref, dar_ref, dai_ref, dzr_ref, dzi_ref, o1, o2, o3):
        _, vjp = jax.vjp(_s5_disc, lr_ref[...], li_ref[...], ldt_ref[...])
        cts = tuple(jnp.sum(r[...], axis=0) for r in (dar_ref, dai_ref, dzr_ref, dzi_ref))
        a, b, c = vjp(cts)
        o1[...] = a
        o2[...] = b
        o3[...] = c

    s = jax.ShapeDtypeStruct((G, P), f32)
    return pl.pallas_call(body, name="s5_prep_bwd", out_shape=[s, s, jax.ShapeDtypeStruct((G, 1), f32)])(
        lr, li, ldt, dar, dai, dzr, dzi)


def _s5_bbd(zr, zi, bre, bim):
    SW, NS = bre.shape

    def body(zr_ref, zi_ref, br_ref, bi_ref, o_ref):
        zr_, zi_, br, bi = zr_ref[...], zi_ref[...], br_ref[...], bi_ref[...]
        o_ref[:, :NS] = (zr_ * br - zi_ * bi).astype(bf16)
        o_ref[:, NS:] = (zr_ * bi + zi_ * br).astype(bf16)

    return pl.pallas_call(body, name="s5_bbd", out_shape=jax.ShapeDtypeStruct((SW, 2 * NS), bf16))(zr, zi, bre, bim)


def _s5_bbd_bwd(dbbd, zr, zi, bre, bim):
    SW, NS = bre.shape

    def body(d_ref, zr_ref, zi_ref, br_ref, bi_ref, dbr_ref, dbi_ref, dzr_ref, dzi_ref):
        zr_, zi_, br, bi = zr_ref[...], zi_ref[...], br_ref[...], bi_ref[...]
        dr, di = d_ref[:, :NS], d_ref[:, NS:]
        dbr_ref[...] = zr_ * dr + zi_ * di
        dbi_ref[...] = zr_ * di - zi_ * dr
        dzr_ref[...] = jnp.sum(dr * br + di * bi, axis=0, keepdims=True)
        dzi_ref[...] = jnp.sum(di * br - dr * bi, axis=0, keepdims=True)

    m = jax.ShapeDtypeStruct((SW, NS), f32)
    v = jax.ShapeDtypeStruct((1, NS), f32)
    return pl.pallas_call(body, name="s5_bbd_bwd", out_shape=[m, m, v, v])(dbbd, zr, zi, bre, bim)


def _slab_cat(ref, NB):
    return jnp.concatenate([ref[j] for j in range(NB)], axis=1)


def _s5_in(p3, bbd, SW, tm):
    B, S, PW = p3.shape
    NS = bbd.shape[1] // 2
    NB = NS // LANES

    def body(u_ref, b_ref, xr_ref, xi_ref):
        x = _dot(u_ref[...], b_ref[...])
        for j in range(NB):
            xr_ref[j] = x[:, j * LANES:(j + 1) * LANES]
            xi_ref[j] = x[:, NS + j * LANES:NS + (j + 1) * LANES]

    slab = jax.ShapeDtypeStruct((B, NB, S, LANES), f32)
    sspec = pl.BlockSpec((None, NB, tm, LANES), lambda b, i: (b, 0, i, 0))
    return pl.pallas_call(
        body, name="s5_in", grid=(B, S // tm),
        in_specs=[pl.BlockSpec((None, tm, SW), lambda b, i: (b, i, 0)), pl.BlockSpec((SW, 2 * NS), lambda b, i: (0, 0))],
        out_specs=[sspec, sspec], out_shape=[slab, slab],
        compiler_params=_params(("parallel", "parallel")))(p3, bbd)


def _s5_scan(name, xr, xi, ar, ai, reverse, hr=None, hi=None, phase=None):
    B, NB, S, _ = xr.shape
    L = S // NSUB
    nb = 2 if (hr is None and NB % 2 == 0) else 1
    with_da = hr is not None

    def body(*refs):
        if with_da:
            xr_ref, xi_ref, ar_ref, ai_ref, hr_ref, hi_ref, or_ref, oi_ref, dar_ref, dai_ref, pr_scr, pi_scr = refs
        else:
            xr_ref, xi_ref, ar_ref, ai_ref, or_ref, oi_ref, pr_scr, pi_scr = refs
        sign = -1.0 if reverse else 1.0
        a_r = [jnp.broadcast_to(ar_ref[j], (NSUB, LANES)) for j in range(nb)]
        a_i = [jnp.broadcast_to(ai_ref[j], (NSUB, LANES)) * sign for j in range(nb)]

        def step(t, carry):
            row = (L - 1 - t) if reverse else t
            rows = pl.ds(row, NSUB, stride=L)
            out = []
            for j in range(nb):
                sr, si, pr, pi = carry[j]
                nr = a_r[j] * sr - a_i[j] * si + xr_ref.at[j][rows, :]
                ni = a_r[j] * si + a_i[j] * sr + xi_ref.at[j][rows, :]
                or_ref.at[j][rows, :] = nr
                oi_ref.at[j][rows, :] = ni
                npr = a_r[j] * pr - a_i[j] * pi
                npi = a_r[j] * pi + a_i[j] * pr
                pr_scr[j, pl.ds(row, 1), :] = npr[0:1]
                pi_scr[j, pl.ds(row, 1), :] = npi[0:1]
                out.append((nr, ni, npr, npi))
            return tuple(out)
        z = jnp.zeros((NSUB, LANES), f32)
        one = jnp.ones((NSUB, LANES), f32)
        fin = lax.fori_loop(0, L, step, tuple((z, z, one, z) for _ in range(nb)))

        for j in range(nb):
            sr, si, plr, pli = fin[j]
            plr, pli = plr[0:1], pli[0:1]
            cr = jnp.zeros((1, LANES), f32)
            ci = jnp.zeros((1, LANES), f32)
            order = range(NSUB - 2, -1, -1) if reverse else range(1, NSUB)
            for c in order:
                src = c + 1 if reverse else c - 1
                cr, ci = (sr[src:src + 1] + plr * cr - pli * ci, si[src:src + 1] + plr * ci + pli * cr)
                rows = slice(c * L, (c + 1) * L)
                tr, ti = pr_scr[j], pi_scr[j]
                or_ref[j, rows, :] += tr * cr - ti * ci
                oi_ref[j, rows, :] += tr * ci + ti * cr
            if with_da:
                first = lax.broadcasted_iota(jnp.int32, (L, LANES), 0) == 0
                dar = jnp.zeros((1, LANES), f32)
                dai = jnp.zeros((1, LANES), f32)
                for c in range(NSUB):
                    rows = slice(c * L, (c + 1) * L)
                    if c == 0:
                        lr_, li_ = jnp.zeros((1, LANES), f32), jnp.zeros((1, LANES), f32)
                    else:
                        lr_, li_ = hr_ref[j, c * L - 1:c * L, :], hi_ref[j, c * L - 1:c * L, :]
                    hpr = jnp.where(first, lr_, pltpu.roll(hr_ref[j, rows, :], 1, 0))
                    hpi = jnp.where(first, li_, pltpu.roll(hi_ref[j, rows, :], 1, 0))
                    gr, gi = or_ref[j, rows, :], oi_ref[j, rows, :]
                    dar += jnp.sum(hpr * gr + hpi * gi, axis=0, keepdims=True)
                    dai += jnp.sum(hpr * gi - hpi * gr, axis=0, keepdims=True)
                dar_ref[j] = dar
                dai_ref[j] = dai

    slab = jax.ShapeDtypeStruct((B, NB, S, LANES), f32)
    sspec = pl.BlockSpec((None, nb, S, LANES), lambda b, j: (b, j, 0, 0))
    aspec = pl.BlockSpec((nb, 1, LANES), lambda b, j: (j, 0, 0))
    in_specs = [sspec, sspec, aspec, aspec]
    out_specs = [sspec, sspec]
    out_shape = [slab, slab]
    args = [xr, xi, ar, ai]
    if with_da:
        in_specs += [sspec, sspec]
        args += [hr, hi]
        dspec = pl.BlockSpec((None, nb, 1, LANES), lambda b, j: (b, j, 0, 0))
        out_specs += [dspec, dspec]
        out_shape += [jax.ShapeDtypeStruct((B, NB, 1, LANES), f32)] * 2
    return _call(body, name, (B, NB // nb), in_specs, out_specs, out_shape, args,
                 scratch=[pltpu.VMEM((nb, L, LANES), f32), pltpu.VMEM((nb, L, LANES), f32)],
                 sem=("parallel", "parallel"), phase=phase)


def _s5_out_parts(hr_ref, hi_ref, u_ref, cr_ref, ci_ref, d_ref, wg_ref, bg_ref, NB):
    hcr = _slab_cat(hr_ref, NB).astype(bf16)
    hci = _slab_cat(hi_ref, NB).astype(bf16)
    u = u_ref[...].astype(f32)
    y2 = _dot(hcr, cr_ref[...]) - _dot(hci, ci_ref[...]) + d_ref[...] * u
    yg = _gelu(y2)
    s = jax.nn.sigmoid(_dot(yg.astype(bf16), wg_ref[...]) + bg_ref[...])
    return hcr, hci, u, y2, yg, s


def _s5_out_specs(B, S, NB, NS, SW, tm):
    sspec = pl.BlockSpec((None, NB, tm, LANES), lambda b, i: (b, 0, i, 0))
    full = lambda r, c: pl.BlockSpec((r, c), lambda b, i: (0, 0))
    return sspec, [sspec, sspec, pl.BlockSpec((None, tm, SW), lambda b, i: (b, i, 0)), full(NS, SW), full(NS, SW),
                   full(1, SW), full(SW, SW), full(1, SW)]


def _s5_out(hr, hi, p3, cbr, cbi, dsk, wglu, bglu, tm):
    B, NB, S, _ = hr.shape
    NS, SW = cbr.shape

    def body(hr_ref, hi_ref, u_ref, cr_ref, ci_ref, d_ref, wg_ref, bg_ref, o_ref):
        _, _, _, _, yg, s = _s5_out_parts(hr_ref, hi_ref, u_ref, cr_ref, ci_ref, d_ref, wg_ref, bg_ref, NB)
        o_ref[...] = (yg * s).astype(bf16)

    _, in_specs = _s5_out_specs(B, S, NB, NS, SW, tm)
    return pl.pallas_call(
        body, name="s5_out", grid=(B, S // tm), in_specs=in_specs,
        out_specs=pl.BlockSpec((None, tm, SW), lambda b, i: (b, i, 0)),
        out_shape=jax.ShapeDtypeStruct((B, S, SW), bf16),
        compiler_params=_params(("parallel", "parallel")))(hr, hi, p3, cbr, cbi, dsk, wglu, bglu)


def _s5_out_bwd(hr, hi, p3, dmix3, cbr, cbi, dsk, wglu, bglu, tm):
    B, NB, S, _ = hr.shape
    NS, SW = cbr.shape

    def body(hr_ref, hi_ref, u_ref, cr_ref, ci_ref, d_ref, wg_ref, bg_ref, da_ref,
             dhr_ref, dhi_ref, du_ref, dcr_ref, dci_ref, dd_ref, dwg_ref, dbg_ref):
        hcr, hci, u, y2, yg, s = _s5_out_parts(hr_ref, hi_ref, u_ref, cr_ref, ci_ref, d_ref, wg_ref, bg_ref, NB)
        da = da_ref[...].astype(f32)
        dz = da * yg * s * (1.0 - s)
        dzb = dz.astype(bf16)
        dyg = da * s + _dg(dzb, wg_ref[...], NT)
        dy2 = dyg * _gelu_grad(y2)
        dyb = dy2.astype(bf16)

        @pl.when((pl.program_id(0) == 0) & (pl.program_id(1) == 0))
        def _():
            for r in (dcr_ref, dci_ref, dd_ref, dwg_ref, dbg_ref):
                r[...] = jnp.zeros_like(r)
        dwg_ref[...] += _dg(yg.astype(bf16), dzb, TN)
        dbg_ref[...] += jnp.sum(dz, axis=0, keepdims=True)
        dd_ref[...] += jnp.sum(dy2 * u, axis=0, keepdims=True)
        dcr_ref[...] += _dg(hcr, dyb, TN)
        dci_ref[...] -= _dg(hci, dyb, TN)
        du_ref[...] = dy2 * d_ref[...]
        dhr = _dg(dyb, cr_ref[...], NT)
        dhi = _dg(dyb, ci_ref[...], NT)
        for j in range(NB):
            dhr_ref[j] = dhr[:, j * LANES:(j + 1) * LANES]
            dhi_ref[j] = -dhi[:, j * LANES:(j + 1) * LANES]

    sspec, in_specs = _s5_out_specs(B, S, NB, NS, SW, tm)
    in_specs = in_specs + [pl.BlockSpec((None, tm, SW), lambda b, i: (b, i, 0))]
    full = lambda r, c: pl.BlockSpec((r, c), lambda b, i: (0, 0))
    slab = jax.ShapeDtypeStruct((B, NB, S, LANES), f32)
    mat = lambda r, c: jax.ShapeDtypeStruct((r, c), f32)
    return pl.pallas_call(
        body, name="s5_out_bwd", grid=(B, S // tm), in_specs=in_specs,
        out_specs=[sspec, sspec, pl.BlockSpec((None, tm, SW), lambda b, i: (b, i, 0)), full(NS, SW), full(NS, SW),
                   full(1, SW), full(SW, SW), full(1, SW)],
        out_shape=[slab, slab, jax.ShapeDtypeStruct((B, S, SW), f32), mat(NS, SW), mat(NS, SW), mat(1, SW),
                   mat(SW, SW), mat(1, SW)],
        compiler_params=_params(("arbitrary", "arbitrary")))(hr, hi, p3, cbr, cbi, dsk, wglu, bglu, dmix3)


def _s5_in_bwd(gr, gi, p3, bbd, du_skip, duv3, tm):
    B, NB, S, _ = gr.shape
    SW, NS2 = bbd.shape
    PW = SW + duv3.shape[2]

    def body(gr_ref, gi_ref, u_ref, b_ref, ds_ref, duv_ref, dp_ref, db_ref):
        g = jnp.concatenate([_slab_cat(gr_ref, NB), _slab_cat(gi_ref, NB)], axis=1).astype(bf16)
        du = _dg(g, b_ref[...], NT) + ds_ref[...]
        dp_ref[:, :SW] = du.astype(bf16)
        dp_ref[:, SW:] = duv_ref[...]

        @pl.when((pl.program_id(0) == 0) & (pl.program_id(1) == 0))
        def _():
            db_ref[...] = jnp.zeros_like(db_ref)
        db_ref[...] += _dg(u_ref[...], g, TN)

    sspec = pl.BlockSpec((None, NB, tm, LANES), lambda b, i: (b, 0, i, 0))
    row = lambda c: pl.BlockSpec((None, tm, c), lambda b, i: (b, i, 0))
    return pl.pallas_call(
        body, name="s5_in_bwd", grid=(B, S // tm),
        in_specs=[sspec, sspec, row(SW), pl.BlockSpec((SW, NS2), lambda b, i: (0, 0)), row(SW), row(PW - SW)],
        out_specs=[row(PW), pl.BlockSpec((SW, NS2), lambda b, i: (0, 0))],
        out_shape=[jax.ShapeDtypeStruct((B, S, PW), bf16), jax.ShapeDtypeStruct((SW, NS2), f32)],
        compiler_params=_params(("arbitrary", "arbitrary")))(gr, gi, p3, bbd, du_skip, duv3)


BIG = ['ev_w_in', 'ev_w_out', 'od_w_in', 'od_w_out', 'ffn_w_up', 'ffn_w_down']
ANY = pl.BlockSpec(memory_space=pl.ANY)


def _rtile(rows, mult):
    best = None
    for d in range(mult, min(rows, 512) + 1, mult):
        if rows % d == 0:
            best = d
    assert best is not None, (rows, mult)
    return best


def _pair_sum(name, g, recv, c_idx, out_dtype):
    NCH, R, W = g.shape
    HALF_W = W // 2
    tr = _rtile(R, 16)

    def body(c_ref, a_ref, b_ref, o_ref):
        o_ref[...] = (a_ref[...] + b_ref[...]).astype(out_dtype)

    return pl.pallas_call(
        body, name=name,
        grid_spec=pltpu.PrefetchScalarGridSpec(
            num_scalar_prefetch=1, grid=(NCH, R // tr),
            in_specs=[pl.BlockSpec((None, tr, HALF_W), lambda j, i, c: (j, i, c[0])),
                      pl.BlockSpec((None, tr, HALF_W), lambda j, i, c: (j, i, 0))],
            out_specs=pl.BlockSpec((None, tr, HALF_W), lambda j, i, c: (j, i, 0))),
        out_shape=jax.ShapeDtypeStruct((NCH, R, HALF_W), out_dtype),
        compiler_params=_params(("parallel", "parallel")))(c_idx, g, recv)


def _chip_sum(name, r3, h, k_idx):
    NCH, R, Wh = r3.shape
    tr = _rtile(R, 16)

    def body(k_ref, a_ref, own_ref, o_ref):
        own = own_ref[...].astype(f32)
        t = [jnp.where(k_ref[0] == s, own, a_ref[s].astype(f32)) for s in range(NCH)]
        o_ref[...] = ((t[0] + t[1]) + t[2]) + t[3]

    return pl.pallas_call(
        body, name=name,
        grid_spec=pltpu.PrefetchScalarGridSpec(
            num_scalar_prefetch=1, grid=(R // tr,),
            in_specs=[pl.BlockSpec((NCH, tr, Wh), lambda i, k: (0, i, 0)),
                      pl.BlockSpec((None, tr, Wh), lambda i, k: (k[0], i, 0))],
            out_specs=pl.BlockSpec((tr, Wh), lambda i, k: (i, 0))),
        out_shape=jax.ShapeDtypeStruct((R, Wh), f32),
        compiler_params=_params(("parallel",)))(k_idx, r3, h)


def _adam_math(gg, w, m, v):
    nm = ADAM_B1 * m + (1.0 - ADAM_B1) * gg
    nv = ADAM_B2 * v + (1.0 - ADAM_B2) * jnp.square(gg)
    m_hat = nm / (1.0 - ADAM_B1 ** ADAM_STEP)
    v_hat = nv / (1.0 - ADAM_B2 ** ADAM_STEP)
    return -ADAM_LR * (m_hat / (jnp.sqrt(v_hat) + ADAM_EPS) + ADAM_WD * w), nm, nv


def _adamw(name, mine, theirs, c_idx, w, m, v, lead, transposed, prev=None):
    L, R, W = w.shape
    if transposed:
        bw = LANES if W % LANES == 0 else W
        gspec = pl.BlockSpec((bw, R // 2), lambda i, hf, c: (i, 0))
        wspec = pl.BlockSpec((None, R // 2, bw), lambda i, hf, c: (lead, hf, i))
        grid = (W // bw, 2)
    else:
        tr = _rtile(R, SUBLANES)
        gspec = pl.BlockSpec((tr, W // 2), lambda i, hf, c: (i, 0))
        wspec = pl.BlockSpec((None, tr, W // 2), lambda i, hf, c: (lead, i, hf))
        grid = (R // tr, 2)

    def body(c_ref, a_ref, b_ref, w_ref, m_ref, v_ref, *rest):
        go_ref, d_ref, nm_ref, nv_ref = rest[-4:]
        gg = jnp.where(pl.program_id(1) == c_ref[0], a_ref[...], b_ref[...])
        if transposed:
            gg = gg.T
        d, nm, nv = _adam_math(gg, w_ref[...], m_ref[...], v_ref[...])
        go_ref[...] = gg
        d_ref[...] = d
        nm_ref[...] = nm
        nv_ref[...] = nv

    in_specs = [gspec, gspec, wspec, wspec, wspec]
    args, aliases = [c_idx, mine, theirs, w, m, v], {}
    if prev is not None:
        in_specs += [ANY] * 4
        args += list(prev)
        aliases = {6: 0, 7: 1, 8: 2, 9: 3}
    s = jax.ShapeDtypeStruct((L, R, W), f32)
    return pl.pallas_call(
        body, name=name,
        grid_spec=pltpu.PrefetchScalarGridSpec(num_scalar_prefetch=1, grid=grid, in_specs=in_specs,
                                               out_specs=[wspec] * 4),
        out_shape=[s, s, s, s], input_output_aliases=aliases,
        compiler_params=_params(("parallel", "arbitrary")))(*args)


def _adamw_small(gs, ws, ms, vs):
    n = len(gs)

    def body(*refs):
        for i in range(n):
            d, nm, nv = _adam_math(refs[i][...], refs[n + i][...], refs[2 * n + i][...], refs[3 * n + i][...])
            refs[4 * n + i][...] = d
            refs[5 * n + i][...] = nm
            refs[6 * n + i][...] = nv

    return pl.pallas_call(body, name="adamw_small",
                          out_shape=[jax.ShapeDtypeStruct(t.shape, f32) for t in ws] * 3)(*gs, *ws, *ms, *vs)


def _place():
    x, y, c = lax.axis_index("x"), lax.axis_index("y"), lax.axis_index("c")
    return x, y, c, [(1 - x, y), (x, 1 - y), (1 - x, 1 - y)]


def _gathered_shape(sh, kind):
    if kind == "rows":
        return sh[:-2] + (N_CHIPS * sh[-2], sh[-1])
    if kind == "cols":
        return sh[:-1] + (N_CHIPS * sh[-1],)
    return (N_CHIPS,) + sh


def _place_shard(name, shard, kind, k_idx):
    sh = shard.shape
    r, C = sh[-2], sh[-1]
    L = sh[0] if len(sh) == 3 else 1
    tr = _rtile(r, 16)
    nr = r // tr
    if kind == "rows":
        out3, omap = (L, N_CHIPS * r, C), lambda l, i, k: (l, k[0] * nr + i, 0)
    elif kind == "cols":
        out3, omap = (L, r, N_CHIPS * C), lambda l, i, k: (l, i, k[0])
    else:
        out3, omap = (N_CHIPS, r, C), lambda l, i, k: (k[0], i, 0)

    def body(k_ref, s_ref, o_ref):
        o_ref[...] = s_ref[...]

    out = pl.pallas_call(
        body, name=name,
        grid_spec=pltpu.PrefetchScalarGridSpec(
            num_scalar_prefetch=1, grid=(L, nr),
            in_specs=[pl.BlockSpec((None, tr, C), lambda l, i, k: (l, i, 0))],
            out_specs=pl.BlockSpec((None, tr, C), omap)),
        out_shape=jax.ShapeDtypeStruct(out3, shard.dtype),
        compiler_params=_params(("parallel", "parallel")))(k_idx, shard.reshape(L, r, C))
    return out.reshape(_gathered_shape(sh, kind))


def _gather_phase(shards, fulls, kinds):
    n = len(shards)
    shapes = [s.shape for s in shards]

    def window(ref, a, k, h=None):
        sh, kind = shapes[a], kinds[a]
        r = sh[-2]
        start, size = (0, r) if h is None else (h * (r // 2), r // 2)
        lead = (slice(None),) * (len(sh) - 2)
        if kind == "rows":
            return ref.at[lead + (pl.ds(k * r + start, size), slice(None))]
        if kind == "cols":
            return ref.at[lead + (pl.ds(start, size), pl.ds(pl.multiple_of(k * sh[-1], LANES), sh[-1]))]
        return ref.at[(k,) + lead + (pl.ds(start, size), slice(None))]

    def copies(s_refs, o_refs, sems):
        send_sems, recv_sems = sems
        x, y, c, chips = _place()
        k = 2 * x + y

        def copy(a, j, kk, hh, to, src=None):
            dst = window(o_refs[a], a, kk, hh)
            return pltpu.make_async_remote_copy(
                src_ref=dst if src is None else src, dst_ref=dst, send_sem=send_sems.at[6 * a + j],
                recv_sem=recv_sems.at[6 * a + j], device_id=to, device_id_type=MESH)

        first = []
        for a in range(n):
            r = shapes[a][-2]
            lead = (slice(None),) * (len(shapes[a]) - 2)
            src = s_refs[a].at[lead + (pl.ds(c * (r // 2), r // 2), slice(None))]
            first += [copy(a, j, k, c, (*chip, c), src=src) for j, chip in enumerate(chips)]
        return copy, first, (x, y, c), (x, y, 1 - c), c, chips

    def start(s_refs, o_refs, sems):
        for cp in copies(s_refs, o_refs, sems)[1]:
            cp.start()

    def finish(s_refs, o_refs, sems):
        copy, first, me, sibling, c, chips = copies(s_refs, o_refs, sems)
        passed = []
        for j, (cx, cy) in enumerate(chips):
            for a in range(n):
                copy(a, j, 2 * cx + cy, c, me).wait_recv()
                fwd = copy(a, 3 + j, 2 * cx + cy, c, sibling)
                fwd.start()
                passed.append(fwd)
        for j, (cx, cy) in enumerate(chips):
            for a in range(n):
                copy(a, 3 + j, 2 * cx + cy, 1 - c, me).wait_recv()
        for cp in first + passed:
            cp.wait_send()

    return _Phase(shards, fulls, [jax.ShapeDtypeStruct(f.shape, f.dtype) for f in fulls],
                  [pltpu.SemaphoreType.DMA((6 * n,)), pltpu.SemaphoreType.DMA((6 * n,))], start, finish)


def _swap_phase(gs):
    n = len(gs)

    def copies(g_refs, o_refs, sems):
        send_sems, recv_sems = sems
        x, y, c, _ = _place()
        half = [g.shape[2] // 2 for g in gs]
        return [pltpu.make_async_remote_copy(
            src_ref=g_refs[a].at[:, :, pl.ds(pl.multiple_of((1 - c) * half[a], LANES), half[a])], dst_ref=o_refs[a],
            send_sem=send_sems.at[a], recv_sem=recv_sems.at[a], device_id=(x, y, 1 - c), device_id_type=MESH)
            for a in range(n)]

    def start(g_refs, o_refs, sems):
        for cp in copies(g_refs, o_refs, sems):
            cp.start()

    def finish(g_refs, o_refs, sems):
        for cp in copies(g_refs, o_refs, sems):
            cp.wait()

    return _Phase(gs, [], [jax.ShapeDtypeStruct(g.shape[:2] + (g.shape[2] // 2,), g.dtype) for g in gs],
                  [pltpu.SemaphoreType.DMA((n,)), pltpu.SemaphoreType.DMA((n,))], start, finish)


def _exchange_phase(hs):
    n = len(hs)

    def copies(h_refs, o_refs, sems):
        send_sems, recv_sems = sems
        x, y, c, chips = _place()
        k = 2 * x + y

        def copy(a, j, src_slot, dst_slot):
            cx, cy = chips[j]
            return pltpu.make_async_remote_copy(
                src_ref=h_refs[a].at[src_slot], dst_ref=o_refs[a].at[dst_slot], send_sem=send_sems.at[3 * a + j],
                recv_sem=recv_sems.at[3 * a + j], device_id=(cx, cy, c), device_id_type=MESH)

        sends = [copy(a, j, 2 * cx + cy, k) for a in range(n) for j, (cx, cy) in enumerate(chips)]
        return copy, sends, k, chips

    def start(h_refs, o_refs, sems):
        for cp in copies(h_refs, o_refs, sems)[1]:
            cp.start()

    def finish(h_refs, o_refs, sems):
        copy, sends, k, chips = copies(h_refs, o_refs, sems)
        for a in range(n):
            for j, (cx, cy) in enumerate(chips):
                copy(a, j, k, 2 * cx + cy).wait_recv()
        for cp in sends:
            cp.wait_send()

    return _Phase(hs, [], [jax.ShapeDtypeStruct(h.shape, h.dtype) for h in hs],
                  [pltpu.SemaphoreType.DMA((3 * n,)), pltpu.SemaphoreType.DMA((3 * n,))], start, finish)


def _comm_pair_share(tag, gs):
    n = len(gs)

    def body(*refs):
        g_refs, o_refs, send_sems, recv_sems = refs[:n], refs[n:2 * n], refs[2 * n], refs[2 * n + 1]
        x, y, c, _ = _place()
        cps = [pltpu.make_async_remote_copy(
            src_ref=g_refs[a], dst_ref=o_refs[a], send_sem=send_sems.at[a], recv_sem=recv_sems.at[a],
            device_id=(x, y, 1 - c), device_id_type=MESH) for a in range(n)]
        for cp in cps:
            cp.start()
        for cp in cps:
            cp.wait()

    return pl.pallas_call(
        body, name="comm_pair_share_" + tag, in_specs=[ANY] * n, out_specs=[ANY] * n,
        out_shape=[jax.ShapeDtypeStruct(g.shape, g.dtype) for g in gs],
        scratch_shapes=[pltpu.SemaphoreType.DMA((n,)), pltpu.SemaphoreType.DMA((n,))])(*gs)


def _pad_rows(flat, unit):
    n = flat.shape[-1]
    pad = (-n) % unit
    if pad:
        flat = jnp.pad(flat, [(0, 0)] * (flat.ndim - 1) + [(0, pad)])
    return flat


def _split_chips(full, axis):
    sh = full.shape
    t = full.reshape(sh[:axis] + (N_CHIPS, sh[axis] // N_CHIPS) + sh[axis + 1:])
    return jnp.moveaxis(t, axis, 0).reshape(N_CHIPS, -1)


def _join_chips(stack, shard_shape, axis):
    t = jnp.moveaxis(stack.reshape((N_CHIPS,) + tuple(shard_shape)), 0, axis)
    sh = t.shape
    return t.reshape(sh[:axis] + (sh[axis] * sh[axis + 1],) + sh[axis + 2:])


def _block_diag(blocks):
    G, r, c = blocks.shape
    eye = jnp.eye(G, dtype=blocks.dtype)
    return (blocks[:, :, None, :] * eye[:, None, :, None]).reshape(G * r, G * c)


def _diag_blocks(m, G):
    r, c = m.shape[0] // G, m.shape[1] // G
    idx = jnp.arange(G)
    return m.reshape(G, r, G, c)[idx, :, idx, :]


def _weight_shards(w):
    conv = jnp.concatenate([w[n].reshape(-1) for n in GATHER_F32])
    conv = _pad_rows(conv, 2 * SUBLANES * LANES).reshape(-1, LANES)
    b16 = lambda a: a.astype(bf16)
    return {'ev_w_in': (b16(w['ev_w_in'][0]), "chip"), 'ev_w_out': (b16(w['ev_w_out'][0]), "rows"),
            's5_w_glu': (b16(w['s5_w_glu'][0]), "rows"), 'conv': (conv, "chip"),
            'od_w_in': (b16(w['od_w_in'][0]), "cols"), 'od_w_out': (b16(w['od_w_out'][0]), "rows"),
            'ffn_w_up0': (b16(w['ffn_w_up'][0]), "cols"), 'ffn_w_up1': (b16(w['ffn_w_up'][1]), "cols"),
            'ffn_w_down0': (b16(w['ffn_w_down'][0]), "rows"), 'ffn_w_down1': (b16(w['ffn_w_down'][1]), "rows")}


def kernel(x, mix_norm_g, ffn_norm_g, final_norm_g, ev_w_in, ev_w_out, s5_lam_re, s5_lam_im, s5_log_dt, s5_b_re, s5_b_im, s5_c_re, s5_c_im, s5_d, s5_w_glu, s5_b_glu, gm_w_s, gm_b_s, gm_v_g, od_w_in, od_conv_w, od_conv_b, od_w_out, ffn_w_up, ffn_conv_w, ffn_conv_b, ffn_w_down, loss_target, m_mix_norm_g, m_ffn_norm_g, m_final_norm_g, m_ev_w_in, m_ev_w_out, m_s5_lam_re, m_s5_lam_im, m_s5_log_dt, m_s5_b_re, m_s5_b_im, m_s5_c_re, m_s5_c_im, m_s5_d, m_s5_w_glu, m_s5_b_glu, m_gm_w_s, m_gm_b_s, m_gm_v_g, m_od_w_in, m_od_conv_w, m_od_conv_b, m_od_w_out, m_ffn_w_up, m_ffn_conv_w, m_ffn_conv_b, m_ffn_w_down, v_mix_norm_g, v_ffn_norm_g, v_final_norm_g, v_ev_w_in, v_ev_w_out, v_s5_lam_re, v_s5_lam_im, v_s5_log_dt, v_s5_b_re, v_s5_b_im, v_s5_c_re, v_s5_c_im, v_s5_d, v_s5_w_glu, v_s5_b_glu, v_gm_w_s, v_gm_b_s, v_gm_v_g, v_od_w_in, v_od_conv_w, v_od_conv_b, v_od_w_out, v_ffn_w_up, v_ffn_conv_w, v_ffn_conv_b, v_ffn_w_down):
    loc = dict(locals())
    w = {n: loc[n] for n in WEIGHTS}
    mom = {n: loc["m_" + n] for n in WEIGHTS}
    var = {n: loc["v_" + n] for n in WEIGHTS}

    B, S, D = x.shape
    T = B * S
    SW = s5_d.shape[1]
    G = SW // SSM_GROUP
    NS = G * SSM_STATE
    NB = NS // LANES
    tm = min(512, S)
    tt = min(1024, T)
    c_idx = lax.axis_index("c").astype(jnp.int32).reshape(1)
    k_idx = (2 * lax.axis_index("x") + lax.axis_index("y")).astype(jnp.int32).reshape(1)
    shards = _weight_shards(w)
    placed = {n: _place_shard("place_" + n, s, kd, k_idx) for n, (s, kd) in shards.items()}

    def gather(names):
        return _gather_phase([shards[n][0] for n in names], [placed[n] for n in names], [shards[n][1] for n in names])

    (w_ev_in,) = _run_phase("comm_gather_ev_in", gather(['ev_w_in']))
    w_ev_in = jnp.swapaxes(w_ev_in, 0, 1).reshape(D, -1)

    h0 = x.reshape(T, D)
    (y0, p0), (w_ev_out, w_glu, conv) = _norm_mm("ev_in", h0, mix_norm_g[0], w_ev_in, tm,
                                                 phase=gather(['ev_w_out', 's5_w_glu', 'conv']))
    full, off = {}, 0
    for n in GATHER_F32:
        full[n] = _join_chips(conv.reshape(N_CHIPS, -1)[:, off:off + w[n].size], w[n].shape, SHARD_AXIS[n])
        off += w[n].size
    PW = p0.shape[1]
    p03 = p0.reshape(B, S, PW)
    lr, li, ldt = s5_lam_re[0], s5_lam_im[0], s5_log_dt[0].reshape(G, 1)
    ar, ai, zr, zi = _s5_prep(lr, li, ldt)
    bre = _block_diag(jnp.swapaxes(s5_b_re[0], 1, 2))
    bim = _block_diag(jnp.swapaxes(s5_b_im[0], 1, 2))
    cbr = _block_diag(jnp.swapaxes(s5_c_re[0], 1, 2)).astype(bf16)
    cbi = _block_diag(jnp.swapaxes(s5_c_im[0], 1, 2)).astype(bf16)
    zr_row, zi_row = zr.reshape(1, NS), zi.reshape(1, NS)
    bbd = _s5_bbd(zr_row, zi_row, bre, bim)
    ar_s, ai_s = ar.reshape(NB, 1, LANES), ai.reshape(NB, 1, LANES)
    xr, xi = _s5_in(p03, bbd, SW, tm)
    (hr, hi), (w_up0,) = _s5_scan("s5_scan", xr, xi, ar_s, ai_s, False,
                                           phase=gather(['ffn_w_up0']))
    dsk, bglu = s5_d.reshape(1, SW), s5_b_glu.reshape(1, SW)
    a_out = _s5_out(hr, hi, p03, cbr, cbi, dsk, w_glu, bglu, tm)
    ws, bst, gv = gm_w_s[0], gm_b_s[0].T, gm_v_g.reshape(1, -1)
    mixcat = _gmlp(p0, a_out.reshape(T, SW), ws, bst, gv, SW)
    (h1,), (w_down0,) = _mm_resid("ev_out", mixcat, w_ev_out, h0, tm, phase=gather(['ffn_w_down0']))

    def ffn_fwd(l, h, w_up, w_down, up_phase=None, down_phase=None, head=None):
        res = _norm_mm(f"ffn_up{l}", h, ffn_norm_g[l], w_up, tm, phase=up_phase)
        (z, up), got_up = res if up_phase is not None else (res, None)
        res = _ffn_down(f"ffn_down{l}", up, full['ffn_conv_w'][l], ffn_conv_b[l].reshape(1, -1), w_down, h, S, tm,
                        phase=down_phase, head=head)
        (*hn, c), got_down = res if down_phase is not None else (res, None)
        return hn, (z, up.reshape(B, S, -1), c.reshape(B, S, -1)), got_up, got_down

    (h2,), ffn0, (w_up1, w_down1), (w_od_in, w_od_out) = ffn_fwd(
        0, h1, w_up0, w_down0, gather(['ffn_w_up1', 'ffn_w_down1']), gather(['od_w_in', 'od_w_out']))
    w_ups, w_downs = (w_up0, w_up1), (w_down0, w_down1)
    od_cw, od_cb = full['od_conv_w'][0], full['od_conv_b']
    y1, p1 = _norm_mm("od_in", h2, mix_norm_g[1], w_od_in, tm)
    p13 = p1.reshape(B, S, -1)
    h3 = _od_out(p1, od_cw, od_cb, w_od_out, h2, S, tm)
    (dh4, dh4b, loss_part, d_final_g), ffn1, _, _ = ffn_fwd(
        1, h3, w_up1, w_down1, head=(final_norm_g, loss_target.reshape(T, D)))


    grads = {}

    halves = {}
    chips = lambda g: g.reshape(N_CHIPS, -1, D)

    def pair_sums(names, parts, recv):
        return [_pair_sum(f"pair_sum_{n}", g, r, c_idx, f32 if n == "small" else bf16)
                for n, g, r in zip(names, parts, recv)]

    def reduce_end(tag, names, hsum, r3):
        mine = [_chip_sum(f"chip_sum_{n}", r, h, k_idx) for n, r, h in zip(names, r3, hsum)]
        theirs = _comm_pair_share(tag, mine)
        halves.update({n: (a, b) for n, a, b in zip(names, mine, theirs)})

    def ffn_bwd(l, dh, dhb, h_in, saved, phase=None, swap=False):
        z, up3, c3 = saved
        w_down, w_up = w_downs[l], w_ups[l]
        da = _mm_nt(f"ffn_down_bwd{l}", dhb, w_down, tm)
        res = _ffn_act_bwd(f"ffn_act_bwd{l}", up3, c3, da.reshape(B, S, -1), full['ffn_conv_w'][l], phase=phase)
        (act, dg3, dv3, dcwg, dcwv, dcbg, dcbv), got = res if phase is not None else (res, None)
        g_down = _mm_tn(f"ffn_down_dw{l}", act.reshape(T, -1), dhb, tt)
        dupg, dupv = dg3.reshape(T, -1), dv3.reshape(T, -1)
        F = dupg.shape[1]
        g_up = _mm_tn(f"ffn_up_dw{l}_gate", dupg, z, tt, rows=2 * F)
        g_up = _mm_tn(f"ffn_up_dw{l}_val", dupv, z, tt, rows=2 * F, row_off=F, prev=g_up)
        parts = [chips(g_down), chips(g_up)]
        res = _mm_nt_normbwd(f"ffn_up_bwd{l}", [dupg, dupv], w_up, h_in, ffn_norm_g[l], dh, tm,
                             phase=_swap_phase(parts) if swap else None)
        (dh_new, dhb_new, dg), recv = res if swap else (res, None)
        F = dg3.shape[2]
        dcw = jnp.concatenate([dcwg[:, :F], dcwv[:, :F]], axis=1)
        dcb = jnp.concatenate([dcbg[:, :F], dcbv[:, :F]], axis=1)
        return dh_new, dhb_new, g_down, g_up, dcw, dcb[0], dg[0], got, parts, recv

    dh3, dh3b, gd1, gu1, gcw1, gcb1, gng1, _, _, _ = ffn_bwd(1, dh4, dh4b, h3, ffn1)
    dsc = _mm_nt("od_out_bwd", dh3b, w_od_out, tm)
    sc, dbg3, dcg3, dhx3, d_od_cw, d_od_cb = _od_act_bwd(p13, dsc.reshape(B, S, D), od_cw, od_cb)
    g_od_out = _mm_tn("od_out_dw", sc.reshape(T, D), dh3b, tt)
    dp1 = [t.reshape(T, D) for t in (dbg3, dcg3, dhx3)]
    g_od_in = None
    for i, piece in enumerate(dp1):
        g_od_in = _mm_tn(f"od_in_dw{i}", piece, y1, tt, rows=3 * D, row_off=i * D, prev=g_od_in)
    grads['od_conv_w'] = d_od_cw[None]
    grads['od_conv_b'] = d_od_cb
    layer1 = ['ffn_w_down1', 'ffn_w_up1', 'od_w_out', 'od_w_in']
    parts1 = [chips(g) for g in (gd1, gu1, g_od_out, g_od_in)]
    (dh2, dh2b, gmix1), recv1 = _mm_nt_normbwd("od_in_bwd", dp1, w_od_in, h2, mix_norm_g[1], dh3, tm,
                                               phase=_swap_phase(parts1))
    hsum1 = pair_sums(layer1, parts1, recv1)
    dh1, dh1b, gd0, gu0, gcw0, gcb0, gng0, r3, parts0, recv0 = ffn_bwd(
        0, dh2, dh2b, h1, ffn0, phase=_exchange_phase(hsum1), swap=True)
    reduce_end("layer1", layer1, hsum1, r3)
    ffn0_names = ['ffn_w_down0', 'ffn_w_up0']
    hsum0 = pair_sums(ffn0_names, parts0, recv0)
    grads['ffn_conv_w'] = jnp.stack([gcw0, gcw1])
    grads['ffn_conv_b'] = jnp.stack([gcb0, gcb1])
    grads['ffn_norm_g'] = jnp.stack([gng0, gng1])
    grads['final_norm_g'] = d_final_g[0]

    dmix = _mm_nt("ev_out_bwd", dh1b, w_ev_out, tm)
    g_ev_out = _mm_tn("ev_out_dw", mixcat, dh1b, tt)
    part_evo = [chips(g_ev_out)]
    (duv, d_ws, d_bs, d_gv), recv_evo = _gmlp_bwd(p0, dmix, ws, bst, gv, SW, phase=_swap_phase(part_evo))
    hsum0 = hsum0 + pair_sums(['ev_w_out'], part_evo, recv_evo)
    ffn0_names = ffn0_names + ['ev_w_out']
    grads['gm_w_s'] = d_ws[None]
    grads['gm_b_s'] = d_bs[:, :, 0][None]
    grads['gm_v_g'] = d_gv
    dhr, dhi, du_skip, d_cbr, d_cbi, d_dsk, d_wglu, d_bglu = _s5_out_bwd(
        hr, hi, p03, dmix.reshape(B, S, D), cbr, cbi, dsk, w_glu, bglu, tm)
    grads['s5_c_re'] = jnp.swapaxes(_diag_blocks(d_cbr, G), 1, 2)[None]
    grads['s5_c_im'] = jnp.swapaxes(_diag_blocks(d_cbi, G), 1, 2)[None]
    grads['s5_d'] = d_dsk
    grads['s5_w_glu'] = d_wglu[None]
    grads['s5_b_glu'] = d_bglu
    (gr, gi, dar, dai), r3 = _s5_scan("s5_rscan", dhr, dhi, ar_s, ai_s, True, hr, hi, phase=_exchange_phase(hsum0))
    reduce_end("ffn0", ffn0_names, hsum0, r3)
    dp03, d_bbd = _s5_in_bwd(gr, gi, p03, bbd, du_skip, duv.reshape(B, S, -1), tm)
    d_bre, d_bim, d_zr, d_zi = _s5_bbd_bwd(d_bbd, zr_row, zi_row, bre, bim)
    grads['s5_b_re'] = jnp.swapaxes(_diag_blocks(d_bre, G), 1, 2)[None]
    grads['s5_b_im'] = jnp.swapaxes(_diag_blocks(d_bim, G), 1, 2)[None]
    shp = (-1, G, SSM_STATE)
    d_lr, d_li, d_ldt = _s5_prep_bwd(lr, li, ldt, dar.reshape(shp), dai.reshape(shp), d_zr.reshape(shp),
                                     d_zi.reshape(shp))
    grads['s5_lam_re'] = d_lr[None]
    grads['s5_lam_im'] = d_li[None]
    grads['s5_log_dt'] = d_ldt.reshape(1, G)
    dp0 = dp03.reshape(T, PW)
    g_ev_in = _mm_tn("ev_in_dw", dp0, y0, tt)
    grad_x, _, gmix0 = _mm_nt_normbwd("ev_in_bwd", [dp0], w_ev_in, h0, mix_norm_g[0], dh1, tm)
    grads['mix_norm_g'] = jnp.concatenate([gmix0, gmix1], axis=0)

    small = [n for n in WEIGHTS if n not in BIG]
    segs = []
    for n in small:
        gfull = grads[n].astype(f32)
        if n in SHARD_AXIS:
            segs.append(_split_chips(gfull, SHARD_AXIS[n]))
        else:
            segs.append(jnp.broadcast_to(gfull.reshape(1, -1), (N_CHIPS, gfull.size)))
    segs.append(jnp.broadcast_to(loss_part, (N_CHIPS, 1)))
    unit = 2 * SUBLANES * D
    gsmall = _pad_rows(jnp.concatenate(segs, axis=1), unit).reshape(N_CHIPS, -1, D)
    mixer0 = ['ev_w_in', 'small']
    parts = [chips(g_ev_in), gsmall]
    hsum = pair_sums(mixer0, parts, _run_phase("comm_pair_swap_mixer0", _swap_phase(parts)))
    reduce_end("mixer0", mixer0, hsum, _run_phase("comm_exchange_mixer0", _exchange_phase(hsum)))

    out_g, out_d, out_m, out_v = {}, {}, {}, {}

    def update(n, key, lead, transposed, prev=None):
        res = _adamw(f"adamw_{key}", *halves[key], c_idx, w[n], mom[n], var[n], lead, transposed, prev)
        out_g[n], out_d[n], out_m[n], out_v[n] = res
        return res

    update('ev_w_in', 'ev_w_in', 0, True)
    update('ev_w_out', 'ev_w_out', 0, False)
    update('od_w_in', 'od_w_in', 0, True)
    update('od_w_out', 'od_w_out', 0, False)
    update('ffn_w_up', 'ffn_w_up0', 0, True, prev=update('ffn_w_up', 'ffn_w_up1', 1, True))
    update('ffn_w_down', 'ffn_w_down0', 0, False, prev=update('ffn_w_down', 'ffn_w_down1', 1, False))

    mine, theirs = halves['small']
    first = lax.axis_index("c") == 0
    flat = jnp.concatenate([jnp.where(first, mine, theirs), jnp.where(first, theirs, mine)], axis=1).reshape(-1)
    off = 0
    for n in small:
        out_g[n] = flat[off:off + w[n].size].reshape(w[n].shape)
        off += w[n].size
    loss = flat[off]
    res = _adamw_small([out_g[n] for n in small], [w[n] for n in small], [mom[n] for n in small],
                       [var[n] for n in small])
    for i, n in enumerate(small):
        out_d[n], out_m[n], out_v[n] = res[i], res[len(small) + i], res[2 * len(small) + i]

    return (loss, grad_x.reshape(B, S, D), *[out_g[n] for n in WEIGHTS], *[out_d[n] for n in WEIGHTS],
            *[out_m[n] for n in WEIGHTS], *[out_v[n] for n in WEIGHTS])
```

```python
import functools
import math

import jax
import jax.numpy as jnp
from jax import lax
from jax.experimental import pallas as pl
from jax.experimental.pallas import tpu as pltpu

f32 = jnp.float32
bf16 = jnp.bfloat16
MESH = pl.DeviceIdType.MESH

SSM_GROUP = 16
SSM_STATE = 64
GMLP_HEAD = 128
CHUNK = 128
EPS = 1e-6
LAMBDA_RE_MAX = -1e-4
ADAM_LR, ADAM_B1, ADAM_B2, ADAM_EPS, ADAM_WD, ADAM_STEP = 0.001, 0.9, 0.999, 1e-08, 0.01, 10

LANES = 128
SUBLANES = 8
NSUB = 32
HALO = 16
VMEM_LIMIT = 56 * 1024 * 1024
N_CHIPS = 4

WEIGHTS = ['mix_norm_g', 'ffn_norm_g', 'final_norm_g', 'ev_w_in', 'ev_w_out', 's5_lam_re', 's5_lam_im', 's5_log_dt',
           's5_b_re', 's5_b_im', 's5_c_re', 's5_c_im', 's5_d', 's5_w_glu', 's5_b_glu', 'gm_w_s', 'gm_b_s', 'gm_v_g',
           'od_w_in', 'od_conv_w', 'od_conv_b', 'od_w_out', 'ffn_w_up', 'ffn_conv_w', 'ffn_conv_b', 'ffn_w_down']
SHARD_AXIS = {'ev_w_in': 2, 'ev_w_out': 1, 's5_w_glu': 1, 'od_w_in': 2, 'od_conv_w': 2, 'od_conv_b': 1, 'od_w_out': 1,
              'ffn_w_up': 2, 'ffn_conv_w': 2, 'ffn_w_down': 1}
GATHER_F32 = ['od_conv_w', 'od_conv_b', 'ffn_conv_w']

_GELU_K0 = math.sqrt(2.0 / math.pi)
_GELU_K1 = 0.044715
NT = (((1,), (1,)), ((), ()))
TN = (((0,), (0,)), ((), ()))


def _pick(n, cap):
    if n <= cap:
        return n
    best = None
    for d in range(LANES, cap + 1, LANES):
        if n % d == 0:
            best = d
    assert best is not None, (n, cap)
    return best


def _params(sem=None):
    return pltpu.CompilerParams(dimension_semantics=sem, vmem_limit_bytes=VMEM_LIMIT)


class _Phase:
    def __init__(self, ins, inplace, outs, sems, start, finish):
        self.ins, self.inplace, self.outs, self.sems = list(ins), list(inplace), list(outs), list(sems)
        self.start, self.finish = start, finish


def _call(body, name, grid, in_specs, out_specs, out_shape, args, scratch=(), sem=None, phase=None):
    if phase is None:
        return pl.pallas_call(body, name=name, grid=grid, in_specs=in_specs, out_specs=out_specs, out_shape=out_shape,
                              scratch_shapes=list(scratch), compiler_params=_params(sem))(*args)
    any_spec = pl.BlockSpec(memory_space=pl.ANY)
    n_in, n_out, n_scr = len(args), len(out_shape), len(scratch)
    p_in = phase.ins + phase.inplace
    ci, co = len(p_in), len(phase.outs)

    def wrapped(*refs):
        ins, cins = refs[:n_in], refs[n_in:n_in + len(phase.ins)]
        b = n_in + ci
        outs, couts = refs[b:b + n_out], refs[b + n_out:b + n_out + co]
        d = b + n_out + co
        scr, csem = refs[d:d + n_scr], refs[d + n_scr:]
        ids = [pl.program_id(i) for i in range(len(grid))]
        first = functools.reduce(jnp.logical_and, [i == 0 for i in ids])
        last = functools.reduce(jnp.logical_and, [i == g - 1 for i, g in zip(ids, grid)])

        @pl.when(first)
        def _():
            phase.start(cins, couts, csem)
        body(*ins, *outs, *scr)

        @pl.when(last)
        def _():
            phase.finish(cins, couts, csem)

    res = pl.pallas_call(
        wrapped, name=name, grid=grid, in_specs=list(in_specs) + [any_spec] * ci,
        out_specs=list(out_specs) + [any_spec] * co, out_shape=list(out_shape) + phase.outs,
        scratch_shapes=list(scratch) + phase.sems,
        input_output_aliases={n_in + len(phase.ins) + i: n_out + i for i in range(len(phase.inplace))},
        compiler_params=_params(tuple("arbitrary" for _ in grid)))(*args, *p_in)
    return res[:n_out], res[n_out:]


def _run_phase(name, phase):
    any_spec = pl.BlockSpec(memory_space=pl.ANY)
    ni, ci, co = len(phase.ins), len(phase.ins) + len(phase.inplace), len(phase.outs)

    def body(*refs):
        cins, couts, csem = refs[:ni], refs[ci:ci + co], refs[ci + co:]
        phase.start(cins, couts, csem)
        phase.finish(cins, couts, csem)

    return pl.pallas_call(
        body, name=name, in_specs=[any_spec] * ci, out_specs=[any_spec] * co, out_shape=phase.outs,
        scratch_shapes=phase.sems, input_output_aliases={ni + i: i for i in range(len(phase.inplace))})(
            *phase.ins, *phase.inplace)


def _gelu(x):
    return 0.5 * x * (1.0 + jnp.tanh(_GELU_K0 * (x + _GELU_K1 * x * x * x)))


def _gelu_grad(x):
    t = jnp.tanh(_GELU_K0 * (x + _GELU_K1 * x * x * x))
    return 0.5 * (1.0 + t) + 0.5 * x * (1.0 - t * t) * _GELU_K0 * (1.0 + 3.0 * _GELU_K1 * x * x)


def _rms_stats(x):
    r = lax.rsqrt(jnp.mean(x * x, axis=-1, keepdims=True) + EPS)
    return x * r, r


def _rms_bwd(dy, xh, r, g):
    dxh = dy * g
    dx = r * (dxh - xh * jnp.mean(dxh * xh, axis=-1, keepdims=True))
    return dx, jnp.sum(dy * xh, axis=0, keepdims=True)


def _dot(a, b):
    return jnp.dot(a, b, preferred_element_type=f32)


def _dg(a, b, dims):
    return lax.dot_general(a, b, dims, preferred_element_type=f32)


def _row_fold(z):
    return z.reshape(z.shape[0] // SUBLANES, SUBLANES, z.shape[1]).sum(axis=0)


def _norm_mm(name, h, g, w, tm, phase=None):
    T, D = h.shape
    N = w.shape[1]
    nc = _pick(N, 512)

    def body(h_ref, g_ref, w_ref, y_ref, o_ref):
        xh, _ = _rms_stats(h_ref[...])
        y = (xh * g_ref[...]).astype(bf16)
        y_ref[...] = y
        for j in range(N // nc):
            o_ref[:, j * nc:(j + 1) * nc] = _dot(y, w_ref[:, j * nc:(j + 1) * nc]).astype(bf16)

    return _call(
        body, name, (T // tm,),
        [pl.BlockSpec((tm, D), lambda i: (i, 0)), pl.BlockSpec((1, D), lambda i: (0, 0)),
         pl.BlockSpec((D, N), lambda i: (0, 0))],
        [pl.BlockSpec((tm, D), lambda i: (i, 0)), pl.BlockSpec((tm, N), lambda i: (i, 0))],
        [jax.ShapeDtypeStruct((T, D), bf16), jax.ShapeDtypeStruct((T, N), bf16)],
        [h, g.reshape(1, D), w], sem=("parallel",), phase=phase)


def _mm_resid(name, a, w, resid, tm, phase=None):
    T, K = a.shape
    N = w.shape[1]

    def body(a_ref, w_ref, r_ref, o_ref):
        o_ref[...] = r_ref[...] + _dot(a_ref[...], w_ref[...])

    return _call(
        body, name, (T // tm,),
        [pl.BlockSpec((tm, K), lambda i: (i, 0)), pl.BlockSpec((K, N), lambda i: (0, 0)),
         pl.BlockSpec((tm, N), lambda i: (i, 0))],
        [pl.BlockSpec((tm, N), lambda i: (i, 0))], [jax.ShapeDtypeStruct((T, N), f32)],
        [a, w, resid], sem=("parallel",), phase=phase)


def _mm_nt(name, dy, w, tm):
    T, N = dy.shape
    K = w.shape[0]
    kc = _pick(K, 512)

    def body(d_ref, w_ref, o_ref):
        d = d_ref[...].astype(bf16)
        for j in range(K // kc):
            o_ref[:, j * kc:(j + 1) * kc] = _dg(d, w_ref[j * kc:(j + 1) * kc, :], NT).astype(bf16)

    return pl.pallas_call(
        body, name=name, grid=(T // tm,),
        in_specs=[pl.BlockSpec((tm, N), lambda i: (i, 0)), pl.BlockSpec((K, N), lambda i: (0, 0))],
        out_specs=pl.BlockSpec((tm, K), lambda i: (i, 0)),
        out_shape=jax.ShapeDtypeStruct((T, K), bf16),
        compiler_params=_params(("parallel",)))(dy, w)


def _mm_nt_normbwd(name, dys, w, h, g, dh_in, tm, phase=None):
    n = len(dys)
    T = dys[0].shape[0]
    D = w.shape[0]
    widths = [d.shape[1] for d in dys]
    offs = [sum(widths[:i]) for i in range(n)]

    def body(*refs):
        d_refs = refs[:n]
        w_ref, h_ref, g_ref, dh_ref, o_ref, ob_ref, dg_ref = refs[n:]
        dz = _dg(d_refs[0][...], w_ref[:, :widths[0]], NT)
        for i in range(1, n):
            dz += _dg(d_refs[i][...], w_ref[:, offs[i]:offs[i] + widths[i]], NT)
        xh, r = _rms_stats(h_ref[...])
        dx, dg = _rms_bwd(dz, xh, r, g_ref[...])
        out = dh_ref[...] + dx
        o_ref[...] = out
        ob_ref[...] = out.astype(bf16)

        @pl.when(pl.program_id(0) == 0)
        def _():
            dg_ref[...] = jnp.zeros_like(dg_ref)
        dg_ref[...] += dg

    row = lambda c: pl.BlockSpec((tm, c), lambda i: (i, 0))
    return _call(
        body, name, (T // tm,),
        [row(c) for c in widths] + [pl.BlockSpec((D, sum(widths)), lambda i: (0, 0)), row(D),
                                    pl.BlockSpec((1, D), lambda i: (0, 0)), row(D)],
        [row(D), row(D), pl.BlockSpec((1, D), lambda i: (0, 0))],
        [jax.ShapeDtypeStruct((T, D), f32), jax.ShapeDtypeStruct((T, D), bf16), jax.ShapeDtypeStruct((1, D), f32)],
        [*dys, w, h, g.reshape(1, D), dh_in], sem=("arbitrary",), phase=phase)


def _mm_tn(name, a, b, tt, rows=None, row_off=0, prev=None):
    T, K = a.shape
    N = b.shape[1]
    rows = K if rows is None else rows
    tk = _pick(K, 1408)
    tn = _pick(N, 1024)
    assert row_off % tk == 0
    kb = row_off // tk

    def body(a_ref, b_ref, *rest):
        o_ref = rest[-1]

        @pl.when(pl.program_id(2) == 0)
        def _():
            o_ref[...] = jnp.zeros_like(o_ref)
        o_ref[...] += _dg(a_ref[...], b_ref[...], TN)

    in_specs = [pl.BlockSpec((tt, tk), lambda k, n, t: (t, k)), pl.BlockSpec((tt, tn), lambda k, n, t: (t, n))]
    args, aliases = [a, b], {}
    if prev is not None:
        in_specs.append(ANY)
        args.append(prev)
        aliases = {2: 0}
    return pl.pallas_call(
        body, name=name, grid=(K // tk, N // tn, T // tt), in_specs=in_specs,
        out_specs=pl.BlockSpec((tk, tn), lambda k, n, t: (k + kb, n)),
        out_shape=jax.ShapeDtypeStruct((rows, N), f32), input_output_aliases=aliases,
        compiler_params=_params(("parallel", "parallel", "arbitrary")))(*args)


def _loss_head(h, g_ref, t_ref, dh_ref, dhb_ref, loss_ref, dg_ref):
    D = h.shape[1]
    xh, r = _rms_stats(h)
    gg = g_ref[...]
    diff = xh * gg - t_ref[...]
    dx, dg = _rms_bwd(diff * (1.0 / D), xh, r, gg)
    dh_ref[...] = dx
    dhb_ref[...] = dx.astype(bf16)

    @pl.when(pl.program_id(0) == 0)
    def _():
        dg_ref[...] = jnp.zeros_like(dg_ref)
        loss_ref[...] = jnp.zeros_like(loss_ref)
    dg_ref[...] += dg
    loss_ref[...] += (0.5 / D) * jnp.sum(jnp.sum(diff * diff, axis=1, keepdims=True), axis=0, keepdims=True)


def _taps(load, r0, R):
    main = load(r0, R)
    hs = pl.multiple_of(jnp.maximum(r0 - HALO, 0), HALO)
    halo = load(hs, HALO) * (r0 > 0).astype(f32)
    ext = jnp.concatenate([halo, main], axis=0)
    xm1 = pltpu.roll(ext, 1, 0)[HALO:]
    xm2 = pltpu.roll(ext, 2, 0)[HALO:]
    return xm2, xm1, main


def _conv(w, b, taps):
    return b + w[0:1] * taps[0] + w[1:2] * taps[1] + w[2:3] * taps[2]


def _ffn_down(name, up, cw, cb, w_down, resid, S, tm, phase=None, head=None):
    T, F2 = up.shape
    F = F2 // 2
    D = w_down.shape[1]
    cwid = _pick(F, 256)
    per_seq = S // tm

    def body(u_ref, halo_ref, cw_ref, cb_ref, w_ref, r_ref, *rest):
        c_ref = rest[-1]
        keep = (pl.program_id(0) % per_seq > 0).astype(f32)

        def conv(off):
            cols = slice(off, off + cwid)
            main = u_ref[:, cols].astype(f32)
            ext = jnp.concatenate([halo_ref[:, cols].astype(f32) * keep, main], axis=0)
            taps = (pltpu.roll(ext, 2, 0)[HALO:], pltpu.roll(ext, 1, 0)[HALO:], main)
            return _conv(cw_ref[:, cols], cb_ref[:, cols], taps)

        acc = r_ref[...]
        for j in range(F // cwid):
            cg, cv = conv(j * cwid), conv(F + j * cwid)
            c_ref[:, j * cwid:(j + 1) * cwid] = cg.astype(bf16)
            c_ref[:, F + j * cwid:F + (j + 1) * cwid] = cv.astype(bf16)
            a = (cg * jax.nn.sigmoid(cg) * cv).astype(bf16)
            acc = acc + _dot(a, w_ref[j * cwid:(j + 1) * cwid, :])
        if head is None:
            rest[0][...] = acc
        else:
            _loss_head(acc, *rest[:-1])

    full = lambda r, c: pl.BlockSpec((r, c), lambda i: (0, 0))
    row = lambda c: pl.BlockSpec((tm, c), lambda i: (i, 0))
    in_specs = [row(F2), pl.BlockSpec((HALO, F2), lambda i: (jnp.maximum(i * (tm // HALO) - 1, 0), 0)),
                full(3, F2), full(1, F2), full(F, D), row(D)]
    args = [up, up, cw, cb, w_down, resid]
    if head is None:
        out_specs, out_shape = [row(D)], [jax.ShapeDtypeStruct((T, D), f32)]
    else:
        in_specs += [full(1, D), row(D)]
        args += [head[0].reshape(1, D), head[1]]
        out_specs = [row(D), row(D), full(1, 1), full(1, D)]
        out_shape = [jax.ShapeDtypeStruct((T, D), f32), jax.ShapeDtypeStruct((T, D), bf16),
                     jax.ShapeDtypeStruct((1, 1), f32), jax.ShapeDtypeStruct((1, D), f32)]
    return _call(body, name, (T // tm,), in_specs, out_specs + [row(F2)],
                 out_shape + [jax.ShapeDtypeStruct((T, F2), bf16)], args,
                 sem=("parallel",) if head is None else ("arbitrary",), phase=phase)


def _rev_conv_rows(d, nxt, w):
    R = d.shape[0]
    ext = jnp.concatenate([d, nxt], axis=0)
    n = R + HALO
    xp1 = pltpu.roll(ext, n - 1, 0)[:R]
    xp2 = pltpu.roll(ext, n - 2, 0)[:R]
    return w[2:3] * d + w[1:2] * xp1 + w[0:1] * xp2, xp1, xp2


def _conv_grad_acc(acc, dc, taps):
    return (acc[0] + _row_fold(dc * taps[0]), acc[1] + _row_fold(dc * taps[1]), acc[2] + _row_fold(dc * taps[2]),
            acc[3] + _row_fold(dc))


def _conv_grad_out(dcw_ref, dcb_ref, acc):
    @pl.when(pl.program_id(1) == 0)
    def _():
        dcw_ref[...] = jnp.zeros_like(dcw_ref)
        dcb_ref[...] = jnp.zeros_like(dcb_ref)
    for k in range(3):
        dcw_ref[k:k + 1, :] += jnp.sum(acc[k], axis=0, keepdims=True)
    dcb_ref[...] += jnp.sum(acc[3], axis=0, keepdims=True)


def _ffn_act_bwd(name, up3, c3, da3, cw, phase=None):
    B, S, F2 = up3.shape
    F = F2 // 2
    cwid = _pick(F, 256)
    nF = F // cwid
    R = min(256, S)
    nR = S // R

    def body(xg_ref, xv_ref, cg_ref, cv_ref, da_ref, wg_ref, wv_ref,
             act_ref, dg_ref, dv_ref, dcwg_ref, dcwv_ref, dcbg_ref, dcbv_ref, sum_scr):
        wg, wv = wg_ref[...], wv_ref[...]

        def half(d, nxt, w, x_ref, rows, acc, out_ref):
            out, xp1, xp2 = _rev_conv_rows(d, nxt, w)
            out_ref[rows, :] = out.astype(bf16)
            x = x_ref[rows, :].astype(f32)
            return (acc[0] + _row_fold(xp2 * x), acc[1] + _row_fold(xp1 * x), acc[2] + _row_fold(d * x),
                    acc[3] + _row_fold(d))

        def step(i, carry):
            ng, nv, accg, accv = carry
            rows = pl.ds(pl.multiple_of((nR - 1 - i) * R, R), R)
            cg, cv = cg_ref[rows, :].astype(f32), cv_ref[rows, :].astype(f32)
            da = da_ref[rows, :].astype(f32)
            sg = jax.nn.sigmoid(cg)
            act_ref[rows, :] = (cg * sg * cv).astype(bf16)
            dgate = da * cv * (sg * (1.0 + cg * (1.0 - sg)))
            dval = da * (cg * sg)
            accg = half(dgate, ng, wg, xg_ref, rows, accg, dg_ref)
            accv = half(dval, nv, wv, xv_ref, rows, accv, dv_ref)
            return dgate[:HALO], dval[:HALO], accg, accv
        z = jnp.zeros((SUBLANES, cwid), f32)
        zh = jnp.zeros((HALO, cwid), f32)
        _, _, accg, accv = lax.fori_loop(0, nR, step, (zh, zh, (z, z, z, z), (z, z, z, z)))
        j = pl.program_id(1)
        for half_i, (acc, dcw_ref, dcb_ref) in enumerate(((accg, dcwg_ref, dcbg_ref), (accv, dcwv_ref, dcbv_ref))):
            @pl.when(pl.program_id(0) == 0)
            def _():
                sum_scr[half_i, j] = jnp.zeros((SUBLANES, cwid), f32)
            for k in range(4):
                sum_scr[half_i, j, k:k + 1, :] += jnp.sum(acc[k], axis=0, keepdims=True)
            dcw_ref[...] = sum_scr[half_i, j, 0:3, :]
            dcb_ref[...] = sum_scr[half_i, j, 3:4, :]

    blk = lambda off: pl.BlockSpec((None, S, cwid), lambda b, j: (b, 0, off + j))
    wblk = lambda off: pl.BlockSpec((3, cwid), lambda b, j: (0, off + j))
    sums = lambda r: pl.BlockSpec((r, cwid), lambda b, j: (0, jnp.where(b == B - 1, j, nF)))
    half_shape = jax.ShapeDtypeStruct((B, S, F), bf16)
    return _call(
        body, name, (B, nF),
        [blk(0), blk(nF), blk(0), blk(nF), blk(0), wblk(0), wblk(nF)],
        [blk(0), blk(0), blk(0), sums(3), sums(3), sums(1), sums(1)],
        [half_shape, half_shape, half_shape, jax.ShapeDtypeStruct((3, F + cwid), f32),
         jax.ShapeDtypeStruct((3, F + cwid), f32), jax.ShapeDtypeStruct((1, F + cwid), f32),
         jax.ShapeDtypeStruct((1, F + cwid), f32)],
        [up3, up3, c3, c3, da3, cw, cw], scratch=[pltpu.VMEM((2, nF, SUBLANES, cwid), f32)],
        sem=("arbitrary", "arbitrary"), phase=phase)


def _od_out(p, cw, cb, w_out, resid, S, tm):
    T, D3 = p.shape
    D = D3 // 3
    cwid = _pick(D, 256)
    per_seq = S // tm

    def body(p_ref, halo_ref, cw_ref, cb_ref, w_ref, r_ref, o_ref):
        keep = (pl.program_id(0) % per_seq > 0).astype(f32)
        acc = r_ref[...]
        for j in range(D // cwid):
            cols = [slice(part * D + j * cwid, part * D + (j + 1) * cwid) for part in range(3)]
            q = p_ref[:, cols[1]].astype(f32) * p_ref[:, cols[2]].astype(f32)
            q_halo = halo_ref[:, cols[1]].astype(f32) * halo_ref[:, cols[2]].astype(f32) * keep
            ext = jnp.concatenate([q_halo, q], axis=0)
            taps = (pltpu.roll(ext, 2, 0)[HALO:], pltpu.roll(ext, 1, 0)[HALO:], q)
            cq = _conv(cw_ref[:, cols[0]], cb_ref[:, cols[0]], taps)
            sc = (p_ref[:, cols[0]].astype(f32) * cq).astype(bf16)
            acc = acc + _dot(sc, w_ref[j * cwid:(j + 1) * cwid, :])
        o_ref[...] = acc

    full = lambda r, c: pl.BlockSpec((r, c), lambda i: (0, 0))
    row = lambda c: pl.BlockSpec((tm, c), lambda i: (i, 0))
    return pl.pallas_call(
        body, name="od_out", grid=(T // tm,),
        in_specs=[row(D3), pl.BlockSpec((HALO, D3), lambda i: (jnp.maximum(i * (tm // HALO) - 1, 0), 0)),
                  full(3, D), full(1, D), full(D, D), row(D)],
        out_specs=row(D), out_shape=jax.ShapeDtypeStruct((T, D), f32),
        compiler_params=_params(("parallel",)))(p, p, cw, cb, w_out, resid)


def _od_act_bwd(p3, dsc3, cw, cb):
    B, S, D3 = p3.shape
    D = D3 // 3
    cwid = _pick(D, 256)
    nD = D // cwid
    R = min(256, S)
    nR = S // R

    def body(bg_ref, cg_ref, hx_ref, d_ref, w_ref, b_ref, sc_ref, dbg_ref, dcg_ref, dhx_ref, dcw_ref, dcb_ref):
        w, b = w_ref[...], b_ref[...]
        q = lambda s, n: cg_ref[pl.ds(s, n), :].astype(f32) * hx_ref[pl.ds(s, n), :].astype(f32)

        def step(i, carry):
            nxt, acc = carry
            r0 = pl.multiple_of((nR - 1 - i) * R, R)
            rows = pl.ds(r0, R)
            tq = _taps(q, r0, R)
            cq = _conv(w, b, tq)
            d = d_ref[rows, :].astype(f32)
            dbg_ref[rows, :] = (d * cq).astype(bf16)
            bg = bg_ref[rows, :].astype(f32)
            sc_ref[rows, :] = (bg * cq).astype(bf16)
            dcq = d * bg
            dq, _, _ = _rev_conv_rows(dcq, nxt, w)
            dcg_ref[rows, :] = (dq * hx_ref[rows, :].astype(f32)).astype(bf16)
            dhx_ref[rows, :] = (dq * cg_ref[rows, :].astype(f32)).astype(bf16)
            return dcq[:HALO], _conv_grad_acc(acc, dcq, tq)
        z = jnp.zeros((SUBLANES, cwid), f32)
        _, acc = lax.fori_loop(0, nR, step, (jnp.zeros((HALO, cwid), f32), (z, z, z, z)))
        _conv_grad_out(dcw_ref, dcb_ref, acc)

    blk = lambda off: pl.BlockSpec((None, S, cwid), lambda j, b: (b, 0, off + j))
    part = jax.ShapeDtypeStruct((B, S, D), bf16)
    return pl.pallas_call(
        body, name="od_act_bwd", grid=(nD, B),
        in_specs=[blk(0), blk(nD), blk(2 * nD), blk(0), pl.BlockSpec((3, cwid), lambda j, b: (0, j)),
                  pl.BlockSpec((1, cwid), lambda j, b: (0, j))],
        out_specs=[blk(0), blk(0), blk(0), blk(0), pl.BlockSpec((3, cwid), lambda j, b: (0, j)),
                   pl.BlockSpec((1, cwid), lambda j, b: (0, j))],
        out_shape=[part, part, part, part, jax.ShapeDtypeStruct((3, D), f32), jax.ShapeDtypeStruct((1, D), f32)],
        compiler_params=_params(("parallel", "arbitrary")))(p3, p3, p3, dsc3, cw, cb)


def _gmlp_parts(p, gv, SW, GW):
    uv = p[:, SW:].astype(f32)
    ge = _gelu(uv)
    u, v = ge[:, :GW], ge[:, GW:]
    vh, r = _rms_stats(v)
    return uv, u, vh, r, vh * gv


def _tril():
    rows = lax.broadcasted_iota(jnp.int32, (CHUNK, CHUNK), 0)
    cols = lax.broadcasted_iota(jnp.int32, (CHUNK, CHUNK), 1)
    return rows >= cols


def _chunks_per_step(T):
    return 4 if T % (4 * CHUNK) == 0 else 1


def _gmlp(p0, a_out, ws, bst, gv, SW):
    T, PW = p0.shape
    GW = (PW - SW) // 2
    H = GW // GMLP_HEAD
    D = SW + GW

    kc = _chunks_per_step(T)
    rb = kc * CHUNK

    def body(p_ref, a_ref, ws_ref, b_ref, gv_ref, o_ref):
        tri = _tril()
        o_ref[:, :SW] = a_ref[...]
        wm = [jnp.where(tri, ws_ref[hh], 0.0).astype(bf16) for hh in range(H)]
        for q in range(kc):
            rows = slice(q * CHUNK, (q + 1) * CHUNK)
            _, u, _, _, vn = _gmlp_parts(p_ref[rows, :], gv_ref[...], SW, GW)
            for hh in range(H):
                sl = slice(hh * GMLP_HEAD, (hh + 1) * GMLP_HEAD)
                gate = _dot(wm[hh], vn[:, sl].astype(bf16)) + b_ref[:, hh:hh + 1]
                o_ref[rows, SW + hh * GMLP_HEAD:SW + (hh + 1) * GMLP_HEAD] = (u[:, sl] * gate).astype(bf16)

    return pl.pallas_call(
        body, name="gmlp", grid=(T // rb,),
        in_specs=[pl.BlockSpec((rb, PW), lambda i: (i, 0)), pl.BlockSpec((rb, SW), lambda i: (i, 0)),
                  pl.BlockSpec((H, CHUNK, CHUNK), lambda i: (0, 0, 0)), pl.BlockSpec((CHUNK, H), lambda i: (0, 0)),
                  pl.BlockSpec((1, GW), lambda i: (0, 0))],
        out_specs=pl.BlockSpec((rb, D), lambda i: (i, 0)),
        out_shape=jax.ShapeDtypeStruct((T, D), bf16),
        compiler_params=_params(("parallel",)))(p0, a_out, ws, bst, gv)


def _gmlp_bwd(p0, dmix, ws, bst, gv, SW, phase=None):
    T, PW = p0.shape
    GW = (PW - SW) // 2
    H = GW // GMLP_HEAD
    D = SW + GW

    kc = _chunks_per_step(T)
    rb = kc * CHUNK

    def body(p_ref, d_ref, ws_ref, b_ref, gv_ref, duv_ref, dws_ref, dbs_ref, dgv_ref):
        gv_ = gv_ref[...]
        tri = _tril()

        @pl.when(pl.program_id(0) == 0)
        def _():
            dws_ref[...] = jnp.zeros_like(dws_ref)
            dbs_ref[...] = jnp.zeros_like(dbs_ref)
            dgv_ref[...] = jnp.zeros_like(dgv_ref)
        wm = [jnp.where(tri, ws_ref[hh], 0.0).astype(bf16) for hh in range(H)]
        for q in range(kc):
            rows = slice(q * CHUNK, (q + 1) * CHUNK)
            uv, u, vh, r, vn = _gmlp_parts(p_ref[rows, :], gv_, SW, GW)
            dout = d_ref[rows, SW:].astype(f32)
            du, dvn = [], []
            for hh in range(H):
                sl = slice(hh * GMLP_HEAD, (hh + 1) * GMLP_HEAD)
                vnh = vn[:, sl].astype(bf16)
                gate = _dot(wm[hh], vnh) + b_ref[:, hh:hh + 1]
                dgate = dout[:, sl] * u[:, sl]
                du.append(dout[:, sl] * gate)
                dgb = dgate.astype(bf16)
                dws_ref[hh] += jnp.where(tri, _dg(dgb, vnh, NT), 0.0)
                dbs_ref[hh] += jnp.broadcast_to(jnp.sum(dgate, axis=1, keepdims=True), (CHUNK, CHUNK))
                dvn.append(_dg(wm[hh], dgb, TN))
            dvn = jnp.concatenate(dvn, axis=1)
            dv, dgv = _rms_bwd(dvn, vh, r, gv_)
            dgv_ref[...] += dgv
            dge = jnp.concatenate(du + [dv], axis=1)
            duv_ref[rows, :] = (dge * _gelu_grad(uv)).astype(bf16)

    fixed = pl.BlockSpec((H, CHUNK, CHUNK), lambda i: (0, 0, 0))
    return _call(
        body, "gmlp_bwd", (T // rb,),
        [pl.BlockSpec((rb, PW), lambda i: (i, 0)), pl.BlockSpec((rb, D), lambda i: (i, 0)), fixed,
         pl.BlockSpec((CHUNK, H), lambda i: (0, 0)), pl.BlockSpec((1, GW), lambda i: (0, 0))],
        [pl.BlockSpec((rb, 2 * GW), lambda i: (i, 0)), fixed, fixed, pl.BlockSpec((1, GW), lambda i: (0, 0))],
        [jax.ShapeDtypeStruct((T, 2 * GW), bf16), jax.ShapeDtypeStruct((H, CHUNK, CHUNK), f32),
         jax.ShapeDtypeStruct((H, CHUNK, CHUNK), f32), jax.ShapeDtypeStruct((1, GW), f32)],
        [p0, dmix, ws, bst, gv], sem=("arbitrary",), phase=phase)


def _s5_disc(lr, li, ldt):
    lr = jnp.minimum(lr, LAMBDA_RE_MAX)
    dt = jnp.exp(ldt)
    mag = jnp.exp(lr * dt)
    ar = mag * jnp.cos(li * dt)
    ai = mag * jnp.sin(li * dt)
    den = lr * lr + li * li
    nr = ar - 1.0
    zr = (nr * lr + ai * li) / den
    zi = (ai * lr - nr * li) / den
    return ar, ai, zr, zi


def _s5_prep(lr, li, ldt):
    G, P = lr.shape

    def body(lr_ref, li_ref, ldt_ref, ar_ref, ai_ref, zr_ref, zi_ref):
        ar, ai, zr, zi = _s5_disc(lr_ref[...], li_ref[...], ldt_ref[...])
        ar_ref[...] = ar
        ai_ref[...] = ai
        zr_ref[...] = zr
        zi_ref[...] = zi

    s = jax.ShapeDtypeStruct((G, P), f32)
    return pl.pallas_call(body, name="s5_prep", out_shape=[s, s, s, s])(lr, li, ldt)


def _s5_prep_bwd(lr, li, ldt, dar, dai, dzr, dzi):
    G, P = lr.shape

    def body(lr_ref, li_ref, ldt_ref, dar_ref, dai_ref, dzr_ref, dzi_ref, o1, o2, o3):
        _, vjp = jax.vjp(_s5_disc, lr_ref[...], li_ref[...], ldt_ref[...])
        cts = tuple(jnp.sum(r[...], axis=0) for r in (dar_ref, dai_ref, dzr_ref, dzi_ref))
        a, b, c = vjp(cts)
        o1[...] = a
        o2[...] = b
        o3[...] = c

    s = jax.ShapeDtypeStruct((G, P), f32)
    return pl.pallas_call(body, name="s5_prep_bwd", out_shape=[s, s, jax.ShapeDtypeStruct((G, 1), f32)])(
        lr, li, ldt, dar, dai, dzr, dzi)


def _s5_bbd(zr, zi, bre, bim):
    SW, NS = bre.shape

    def body(zr_ref, zi_ref, br_ref, bi_ref, o_ref):
        zr_, zi_, br, bi = zr_ref[...], zi_ref[...], br_ref[...], bi_ref[...]
        o_ref[:, :NS] = (zr_ * br - zi_ * bi).astype(bf16)
        o_ref[:, NS:] = (zr_ * bi + zi_ * br).astype(bf16)

    return pl.pallas_call(body, name="s5_bbd", out_shape=jax.ShapeDtypeStruct((SW, 2 * NS), bf16))(zr, zi, bre, bim)


def _s5_bbd_bwd(dbbd, zr, zi, bre, bim):
    SW, NS = bre.shape

    def body(d_ref, zr_ref, zi_ref, br_ref, bi_ref, dbr_ref, dbi_ref, dzr_ref, dzi_ref):
        zr_, zi_, br, bi = zr_ref[...], zi_ref[...], br_ref[...], bi_ref[...]
        dr, di = d_ref[:, :NS], d_ref[:, NS:]
        dbr_ref[...] = zr_ * dr + zi_ * di
        dbi_ref[...] = zr_ * di - zi_ * dr
        dzr_ref[...] = jnp.sum(dr * br + di * bi, axis=0, keepdims=True)
        dzi_ref[...] = jnp.sum(di * br - dr * bi, axis=0, keepdims=True)

    m = jax.ShapeDtypeStruct((SW, NS), f32)
    v = jax.ShapeDtypeStruct((1, NS), f32)
    return pl.pallas_call(body, name="s5_bbd_bwd", out_shape=[m, m, v, v])(dbbd, zr, zi, bre, bim)


def _slab_cat(ref, NB):
    return jnp.concatenate([ref[j] for j in range(NB)], axis=1)


def _s5_in(p3, bbd, SW, tm):
    B, S, PW = p3.shape
    NS = bbd.shape[1] // 2
    NB = NS // LANES

    def body(u_ref, b_ref, xr_ref, xi_ref):
        x = _dot(u_ref[...], b_ref[...])
        for j in range(NB):
            xr_ref[j] = x[:, j * LANES:(j + 1) * LANES]
            xi_ref[j] = x[:, NS + j * LANES:NS + (j + 1) * LANES]

    slab = jax.ShapeDtypeStruct((B, NB, S, LANES), f32)
    sspec = pl.BlockSpec((None, NB, tm, LANES), lambda b, i: (b, 0, i, 0))
    return pl.pallas_call(
        body, name="s5_in", grid=(B, S // tm),
        in_specs=[pl.BlockSpec((None, tm, SW), lambda b, i: (b, i, 0)), pl.BlockSpec((SW, 2 * NS), lambda b, i: (0, 0))],
        out_specs=[sspec, sspec], out_shape=[slab, slab],
        compiler_params=_params(("parallel", "parallel")))(p3, bbd)


def _s5_scan(name, xr, xi, ar, ai, reverse, hr=None, hi=None, phase=None):
    B, NB, S, _ = xr.shape
    L = S // NSUB
    nb = 2 if (hr is None and NB % 2 == 0) else 1
    with_da = hr is not None

    def body(*refs):
        if with_da:
            xr_ref, xi_ref, ar_ref, ai_ref, hr_ref, hi_ref, or_ref, oi_ref, dar_ref, dai_ref, pr_scr, pi_scr = refs
        else:
            xr_ref, xi_ref, ar_ref, ai_ref, or_ref, oi_ref, pr_scr, pi_scr = refs
        sign = -1.0 if reverse else 1.0
        a_r = [jnp.broadcast_to(ar_ref[j], (NSUB, LANES)) for j in range(nb)]
        a_i = [jnp.broadcast_to(ai_ref[j], (NSUB, LANES)) * sign for j in range(nb)]

        def step(t, carry):
            row = (L - 1 - t) if reverse else t
            rows = pl.ds(row, NSUB, stride=L)
            out = []
            for j in range(nb):
                sr, si, pr, pi = carry[j]
                nr = a_r[j] * sr - a_i[j] * si + xr_ref.at[j][rows, :]
                ni = a_r[j] * si + a_i[j] * sr + xi_ref.at[j][rows, :]
                or_ref.at[j][rows, :] = nr
                oi_ref.at[j][rows, :] = ni
                npr = a_r[j] * pr - a_i[j] * pi
                npi = a_r[j] * pi + a_i[j] * pr
                pr_scr[j, pl.ds(row, 1), :] = npr[0:1]
                pi_scr[j, pl.ds(row, 1), :] = npi[0:1]
                out.append((nr, ni, npr, npi))
            return tuple(out)
        z = jnp.zeros((NSUB, LANES), f32)
        one = jnp.ones((NSUB, LANES), f32)
        fin = lax.fori_loop(0, L, step, tuple((z, z, one, z) for _ in range(nb)))

        for j in range(nb):
            sr, si, plr, pli = fin[j]
            plr, pli = plr[0:1], pli[0:1]
            cr = jnp.zeros((1, LANES), f32)
            ci = jnp.zeros((1, LANES), f32)
            order = range(NSUB - 2, -1, -1) if reverse else range(1, NSUB)
            for c in order:
                src = c + 1 if reverse else c - 1
                cr, ci = (sr[src:src + 1] + plr * cr - pli * ci, si[src:src + 1] + plr * ci + pli * cr)
                rows = slice(c * L, (c + 1) * L)
                tr, ti = pr_scr[j], pi_scr[j]
                or_ref[j, rows, :] += tr * cr - ti * ci
                oi_ref[j, rows, :] += tr * ci + ti * cr
            if with_da:
                first = lax.broadcasted_iota(jnp.int32, (L, LANES), 0) == 0
                dar = jnp.zeros((1, LANES), f32)
                dai = jnp.zeros((1, LANES), f32)
                for c in range(NSUB):
                    rows = slice(c * L, (c + 1) * L)
                    if c == 0:
                        lr_, li_ = jnp.zeros((1, LANES), f32), jnp.zeros((1, LANES), f32)
                    else:
                        lr_, li_ = hr_ref[j, c * L - 1:c * L, :], hi_ref[j, c * L - 1:c * L, :]
                    hpr = jnp.where(first, lr_, pltpu.roll(hr_ref[j, rows, :], 1, 0))
                    hpi = jnp.where(first, li_, pltpu.roll(hi_ref[j, rows, :], 1, 0))
                    gr, gi = or_ref[j, rows, :], oi_ref[j, rows, :]
                    dar += jnp.sum(hpr * gr + hpi * gi, axis=0, keepdims=True)
                    dai += jnp.sum(hpr * gi - hpi * gr, axis=0, keepdims=True)
                dar_ref[j] = dar
                dai_ref[j] = dai

    slab = jax.ShapeDtypeStruct((B, NB, S, LANES), f32)
    sspec = pl.BlockSpec((None, nb, S, LANES), lambda b, j: (b, j, 0, 0))
    aspec = pl.BlockSpec((nb, 1, LANES), lambda b, j: (j, 0, 0))
    in_specs = [sspec, sspec, aspec, aspec]
    out_specs = [sspec, sspec]
    out_shape = [slab, slab]
    args = [xr, xi, ar, ai]
    if with_da:
        in_specs += [sspec, sspec]
        args += [hr, hi]
        dspec = pl.BlockSpec((None, nb, 1, LANES), lambda b, j: (b, j, 0, 0))
        out_specs += [dspec, dspec]
        out_shape += [jax.ShapeDtypeStruct((B, NB, 1, LANES), f32)] * 2
    return _call(body, name, (B, NB // nb), in_specs, out_specs, out_shape, args,
                 scratch=[pltpu.VMEM((nb, L, LANES), f32), pltpu.VMEM((nb, L, LANES), f32)],
                 sem=("parallel", "parallel"), phase=phase)


def _s5_out_parts(hr_ref, hi_ref, u_ref, cr_ref, ci_ref, d_ref, wg_ref, bg_ref, NB):
    hcr = _slab_cat(hr_ref, NB).astype(bf16)
    hci = _slab_cat(hi_ref, NB).astype(bf16)
    u = u_ref[...].astype(f32)
    y2 = _dot(hcr, cr_ref[...]) - _dot(hci, ci_ref[...]) + d_ref[...] * u
    yg = _gelu(y2)
    s = jax.nn.sigmoid(_dot(yg.astype(bf16), wg_ref[...]) + bg_ref[...])
    return hcr, hci, u, y2, yg, s


def _s5_out_specs(B, S, NB, NS, SW, tm):
    sspec = pl.BlockSpec((None, NB, tm, LANES), lambda b, i: (b, 0, i, 0))
    full = lambda r, c: pl.BlockSpec((r, c), lambda b, i: (0, 0))
    return sspec, [sspec, sspec, pl.BlockSpec((None, tm, SW), lambda b, i: (b, i, 0)), full(NS, SW), full(NS, SW),
                   full(1, SW), full(SW, SW), full(1, SW)]


def _s5_out(hr, hi, p3, cbr, cbi, dsk, wglu, bglu, tm):
    B, NB, S, _ = hr.shape
    NS, SW = cbr.shape

    def body(hr_ref, hi_ref, u_ref, cr_ref, ci_ref, d_ref, wg_ref, bg_ref, o_ref):
        _, _, _, _, yg, s = _s5_out_parts(hr_ref, hi_ref, u_ref, cr_ref, ci_ref, d_ref, wg_ref, bg_ref, NB)
        o_ref[...] = (yg * s).astype(bf16)

    _, in_specs = _s5_out_specs(B, S, NB, NS, SW, tm)
    return pl.pallas_call(
        body, name="s5_out", grid=(B, S // tm), in_specs=in_specs,
        out_specs=pl.BlockSpec((None, tm, SW), lambda b, i: (b, i, 0)),
        out_shape=jax.ShapeDtypeStruct((B, S, SW), bf16),
        compiler_params=_params(("parallel", "parallel")))(hr, hi, p3, cbr, cbi, dsk, wglu, bglu)


def _s5_out_bwd(hr, hi, p3, dmix3, cbr, cbi, dsk, wglu, bglu, tm):
    B, NB, S, _ = hr.shape
    NS, SW = cbr.shape

    def body(hr_ref, hi_ref, u_ref, cr_ref, ci_ref, d_ref, wg_ref, bg_ref, da_ref,
             dhr_ref, dhi_ref, du_ref, dcr_ref, dci_ref, dd_ref, dwg_ref, dbg_ref):
        hcr, hci, u, y2, yg, s = _s5_out_parts(hr_ref, hi_ref, u_ref, cr_ref, ci_ref, d_ref, wg_ref, bg_ref, NB)
        da = da_ref[...].astype(f32)
        dz = da * yg * s * (1.0 - s)
        dzb = dz.astype(bf16)
        dyg = da * s + _dg(dzb, wg_ref[...], NT)
        dy2 = dyg * _gelu_grad(y2)
        dyb = dy2.astype(bf16)

        @pl.when((pl.program_id(0) == 0) & (pl.program_id(1) == 0))
        def _():
            for r in (dcr_ref, dci_ref, dd_ref, dwg_ref, dbg_ref):
                r[...] = jnp.zeros_like(r)
        dwg_ref[...] += _dg(yg.astype(bf16), dzb, TN)
        dbg_ref[...] += jnp.sum(dz, axis=0, keepdims=True)
        dd_ref[...] += jnp.sum(dy2 * u, axis=0, keepdims=True)
        dcr_ref[...] += _dg(hcr, dyb, TN)
        dci_ref[...] -= _dg(hci, dyb, TN)
        du_ref[...] = dy2 * d_ref[...]
        dhr = _dg(dyb, cr_ref[...], NT)
        dhi = _dg(dyb, ci_ref[...], NT)
        for j in range(NB):
            dhr_ref[j] = dhr[:, j * LANES:(j + 1) * LANES]
            dhi_ref[j] = -dhi[:, j * LANES:(j + 1) * LANES]

    sspec, in_specs = _s5_out_specs(B, S, NB, NS, SW, tm)
    in_specs = in_specs + [pl.BlockSpec((None, tm, SW), lambda b, i: (b, i, 0))]
    full = lambda r, c: pl.BlockSpec((r, c), lambda b, i: (0, 0))
    slab = jax.ShapeDtypeStruct((B, NB, S, LANES), f32)
    mat = lambda r, c: jax.ShapeDtypeStruct((r, c), f32)
    return pl.pallas_call(
        body, name="s5_out_bwd", grid=(B, S // tm), in_specs=in_specs,
        out_specs=[sspec, sspec, pl.BlockSpec((None, tm, SW), lambda b, i: (b, i, 0)), full(NS, SW), full(NS, SW),
                   full(1, SW), full(SW, SW), full(1, SW)],
        out_shape=[slab, slab, jax.ShapeDtypeStruct((B, S, SW), f32), mat(NS, SW), mat(NS, SW), mat(1, SW),
                   mat(SW, SW), mat(1, SW)],
        compiler_params=_params(("arbitrary", "arbitrary")))(hr, hi, p3, cbr, cbi, dsk, wglu, bglu, dmix3)


def _s5_in_bwd(gr, gi, p3, bbd, du_skip, duv3, tm):
    B, NB, S, _ = gr.shape
    SW, NS2 = bbd.shape
    PW = SW + duv3.shape[2]

    def body(gr_ref, gi_ref, u_ref, b_ref, ds_ref, duv_ref, dp_ref, db_ref):
        g = jnp.concatenate([_slab_cat(gr_ref, NB), _slab_cat(gi_ref, NB)], axis=1).astype(bf16)
        du = _dg(g, b_ref[...], NT) + ds_ref[...]
        dp_ref[:, :SW] = du.astype(bf16)
        dp_ref[:, SW:] = duv_ref[...]

        @pl.when((pl.program_id(0) == 0) & (pl.program_id(1) == 0))
        def _():
            db_ref[...] = jnp.zeros_like(db_ref)
        db_ref[...] += _dg(u_ref[...], g, TN)

    sspec = pl.BlockSpec((None, NB, tm, LANES), lambda b, i: (b, 0, i, 0))
    row = lambda c: pl.BlockSpec((None, tm, c), lambda b, i: (b, i, 0))
    return pl.pallas_call(
        body, name="s5_in_bwd", grid=(B, S // tm),
        in_specs=[sspec, sspec, row(SW), pl.BlockSpec((SW, NS2), lambda b, i: (0, 0)), row(SW), row(PW - SW)],
        out_specs=[row(PW), pl.BlockSpec((SW, NS2), lambda b, i: (0, 0))],
        out_shape=[jax.ShapeDtypeStruct((B, S, PW), bf16), jax.ShapeDtypeStruct((SW, NS2), f32)],
        compiler_params=_params(("arbitrary", "arbitrary")))(gr, gi, p3, bbd, du_skip, duv3)


BIG = ['ev_w_in', 'ev_w_out', 'od_w_in', 'od_w_out', 'ffn_w_up', 'ffn_w_down']
ANY = pl.BlockSpec(memory_space=pl.ANY)


def _rtile(rows, mult):
    best = None
    for d in range(mult, min(rows, 512) + 1, mult):
        if rows % d == 0:
            best = d
    assert best is not None, (rows, mult)
    return best


def _pair_sum(name, g, recv, c_idx, out_dtype):
    NCH, R, W = g.shape
    HALF_W = W // 2
    tr = _rtile(R, 16)

    def body(c_ref, a_ref, b_ref, o_ref):
        o_ref[...] = (a_ref[...] + b_ref[...]).astype(out_dtype)

    return pl.pallas_call(
        body, name=name,
        grid_spec=pltpu.PrefetchScalarGridSpec(
            num_scalar_prefetch=1, grid=(NCH, R // tr),
            in_specs=[pl.BlockSpec((None, tr, HALF_W), lambda j, i, c: (j, i, c[0])),
                      pl.BlockSpec((None, tr, HALF_W), lambda j, i, c: (j, i, 0))],
            out_specs=pl.BlockSpec((None, tr, HALF_W), lambda j, i, c: (j, i, 0))),
        out_shape=jax.ShapeDtypeStruct((NCH, R, HALF_W), out_dtype),
        compiler_params=_params(("parallel", "parallel")))(c_idx, g, recv)


def _chip_sum(name, r3, h, k_idx):
    NCH, R, Wh = r3.shape
    tr = _rtile(R, 16)

    def body(k_ref, a_ref, own_ref, o_ref):
        own = own_ref[...].astype(f32)
        t = [jnp.where(k_ref[0] == s, own, a_ref[s].astype(f32)) for s in range(NCH)]
        o_ref[...] = ((t[0] + t[1]) + t[2]) + t[3]

    return pl.pallas_call(
        body, name=name,
        grid_spec=pltpu.PrefetchScalarGridSpec(
            num_scalar_prefetch=1, grid=(R // tr,),
            in_specs=[pl.BlockSpec((NCH, tr, Wh), lambda i, k: (0, i, 0)),
                      pl.BlockSpec((None, tr, Wh), lambda i, k: (k[0], i, 0))],
            out_specs=pl.BlockSpec((tr, Wh), lambda i, k: (i, 0))),
        out_shape=jax.ShapeDtypeStruct((R, Wh), f32),
        compiler_params=_params(("parallel",)))(k_idx, r3, h)


def _adam_math(gg, w, m, v):
    nm = ADAM_B1 * m + (1.0 - ADAM_B1) * gg
    nv = ADAM_B2 * v + (1.0 - ADAM_B2) * jnp.square(gg)
    m_hat = nm / (1.0 - ADAM_B1 ** ADAM_STEP)
    v_hat = nv / (1.0 - ADAM_B2 ** ADAM_STEP)
    return -ADAM_LR * (m_hat / (jnp.sqrt(v_hat) + ADAM_EPS) + ADAM_WD * w), nm, nv


def _adamw(name, mine, theirs, c_idx, w, m, v, lead, transposed, prev=None):
    L, R, W = w.shape
    if transposed:
        bw = LANES if W % LANES == 0 else W
        gspec = pl.BlockSpec((bw, R // 2), lambda i, hf, c: (i, 0))
        wspec = pl.BlockSpec((None, R // 2, bw), lambda i, hf, c: (lead, hf, i))
        grid = (W // bw, 2)
    else:
        tr = _rtile(R, SUBLANES)
        gspec = pl.BlockSpec((tr, W // 2), lambda i, hf, c: (i, 0))
        wspec = pl.BlockSpec((None, tr, W // 2), lambda i, hf, c: (lead, i, hf))
        grid = (R // tr, 2)

    def body(c_ref, a_ref, b_ref, w_ref, m_ref, v_ref, *rest):
        go_ref, d_ref, nm_ref, nv_ref = rest[-4:]
        gg = jnp.where(pl.program_id(1) == c_ref[0], a_ref[...], b_ref[...])
        if transposed:
            gg = gg.T
        d, nm, nv = _adam_math(gg, w_ref[...], m_ref[...], v_ref[...])
        go_ref[...] = gg
        d_ref[...] = d
        nm_ref[...] = nm
        nv_ref[...] = nv

    in_specs = [gspec, gspec, wspec, wspec, wspec]
    args, aliases = [c_idx, mine, theirs, w, m, v], {}
    if prev is not None:
        in_specs += [ANY] * 4
        args += list(prev)
        aliases = {6: 0, 7: 1, 8: 2, 9: 3}
    s = jax.ShapeDtypeStruct((L, R, W), f32)
    return pl.pallas_call(
        body, name=name,
        grid_spec=pltpu.PrefetchScalarGridSpec(num_scalar_prefetch=1, grid=grid, in_specs=in_specs,
                                               out_specs=[wspec] * 4),
        out_shape=[s, s, s, s], input_output_aliases=aliases,
        compiler_params=_params(("parallel", "arbitrary")))(*args)


def _adamw_small(gs, ws, ms, vs):
    n = len(gs)

    def body(*refs):
        for i in range(n):
            d, nm, nv = _adam_math(refs[i][...], refs[n + i][...], refs[2 * n + i][...], refs[3 * n + i][...])
            refs[4 * n + i][...] = d
            refs[5 * n + i][...] = nm
            refs[6 * n + i][...] = nv

    return pl.pallas_call(body, name="adamw_small",
                          out_shape=[jax.ShapeDtypeStruct(t.shape, f32) for t in ws] * 3)(*gs, *ws, *ms, *vs)


def _place():
    x, y, c = lax.axis_index("x"), lax.axis_index("y"), lax.axis_index("c")
    return x, y, c, [(1 - x, y), (x, 1 - y), (1 - x, 1 - y)]


def _gathered_shape(sh, kind):
    if kind == "rows":
        return sh[:-2] + (N_CHIPS * sh[-2], sh[-1])
    if kind == "cols":
        return sh[:-1] + (N_CHIPS * sh[-1],)
    return (N_CHIPS,) + sh


def _place_shard(name, shard, kind, k_idx):
    sh = shard.shape
    r, C = sh[-2], sh[-1]
    L = sh[0] if len(sh) == 3 else 1
    tr = _rtile(r, 16)
    nr = r // tr
    if kind == "rows":
        out3, omap = (L, N_CHIPS * r, C), lambda l, i, k: (l, k[0] * nr + i, 0)
    elif kind == "cols":
        out3, omap = (L, r, N_CHIPS * C), lambda l, i, k: (l, i, k[0])
    else:
        out3, omap = (N_CHIPS, r, C), lambda l, i, k: (k[0], i, 0)

    def body(k_ref, s_ref, o_ref):
        o_ref[...] = s_ref[...]

    out = pl.pallas_call(
        body, name=name,
        grid_spec=pltpu.PrefetchScalarGridSpec(
            num_scalar_prefetch=1, grid=(L, nr),
            in_specs=[pl.BlockSpec((None, tr, C), lambda l, i, k: (l, i, 0))],
            out_specs=pl.BlockSpec((None, tr, C), omap)),
        out_shape=jax.ShapeDtypeStruct(out3, shard.dtype),
        compiler_params=_params(("parallel", "parallel")))(k_idx, shard.reshape(L, r, C))
    return out.reshape(_gathered_shape(sh, kind))


def _gather_phase(shards, fulls, kinds):
    n = len(shards)
    shapes = [s.shape for s in shards]

    def window(ref, a, k, h=None):
        sh, kind = shapes[a], kinds[a]
        r = sh[-2]
        start, size = (0, r) if h is None else (h * (r // 2), r // 2)
        lead = (slice(None),) * (len(sh) - 2)
        if kind == "rows":
            return ref.at[lead + (pl.ds(k * r + start, size), slice(None))]
        if kind == "cols":
            return ref.at[lead + (pl.ds(start, size), pl.ds(pl.multiple_of(k * sh[-1], LANES), sh[-1]))]
        return ref.at[(k,) + lead + (pl.ds(start, size), slice(None))]

    def copies(s_refs, o_refs, sems):
        send_sems, recv_sems = sems
        x, y, c, chips = _place()
        k = 2 * x + y

        def copy(a, j, kk, hh, to, src=None):
            dst = window(o_refs[a], a, kk, hh)
            return pltpu.make_async_remote_copy(
                src_ref=dst if src is None else src, dst_ref=dst, send_sem=send_sems.at[6 * a + j],
                recv_sem=recv_sems.at[6 * a + j], device_id=to, device_id_type=MESH)

        first = []
        for a in range(n):
            r = shapes[a][-2]
            lead = (slice(None),) * (len(shapes[a]) - 2)
            src = s_refs[a].at[lead + (pl.ds(c * (r // 2), r // 2), slice(None))]
            first += [copy(a, j, k, c, (*chip, c), src=src) for j, chip in enumerate(chips)]
        return copy, first, (x, y, c), (x, y, 1 - c), c, chips

    def start(s_refs, o_refs, sems):
        for cp in copies(s_refs, o_refs, sems)[1]:
            cp.start()

    def finish(s_refs, o_refs, sems):
        copy, first, me, sibling, c, chips = copies(s_refs, o_refs, sems)
        passed = []
        for j, (cx, cy) in enumerate(chips):
            for a in range(n):
                copy(a, j, 2 * cx + cy, c, me).wait_recv()
                fwd = copy(a, 3 + j, 2 * cx + cy, c, sibling)
                fwd.start()
                passed.append(fwd)
        for j, (cx, cy) in enumerate(chips):
            for a in range(n):
                copy(a, 3 + j, 2 * cx + cy, 1 - c, me).wait_recv()
        for cp in first + passed:
            cp.wait_send()

    return _Phase(shards, fulls, [jax.ShapeDtypeStruct(f.shape, f.dtype) for f in fulls],
                  [pltpu.SemaphoreType.DMA((6 * n,)), pltpu.SemaphoreType.DMA((6 * n,))], start, finish)


def _swap_phase(gs):
    n = len(gs)

    def copies(g_refs, o_refs, sems):
        send_sems, recv_sems = sems
        x, y, c, _ = _place()
        half = [g.shape[2] // 2 for g in gs]
        return [pltpu.make_async_remote_copy(
            src_ref=g_refs[a].at[:, :, pl.ds(pl.multiple_of((1 - c) * half[a], LANES), half[a])], dst_ref=o_refs[a],
            send_sem=send_sems.at[a], recv_sem=recv_sems.at[a], device_id=(x, y, 1 - c), device_id_type=MESH)
            for a in range(n)]

    def start(g_refs, o_refs, sems):
        for cp in copies(g_refs, o_refs, sems):
            cp.start()

    def finish(g_refs, o_refs, sems):
        for cp in copies(g_refs, o_refs, sems):
            cp.wait()

    return _Phase(gs, [], [jax.ShapeDtypeStruct(g.shape[:2] + (g.shape[2] // 2,), g.dtype) for g in gs],
                  [pltpu.SemaphoreType.DMA((n,)), pltpu.SemaphoreType.DMA((n,))], start, finish)


def _exchange_phase(hs):
    n = len(hs)

    def copies(h_refs, o_refs, sems):
        send_sems, recv_sems = sems
        x, y, c, chips = _place()
        k = 2 * x + y

        def copy(a, j, src_slot, dst_slot):
            cx, cy = chips[j]
            return pltpu.make_async_remote_copy(
                src_ref=h_refs[a].at[src_slot], dst_ref=o_refs[a].at[dst_slot], send_sem=send_sems.at[3 * a + j],
                recv_sem=recv_sems.at[3 * a + j], device_id=(cx, cy, c), device_id_type=MESH)

        sends = [copy(a, j, 2 * cx + cy, k) for a in range(n) for j, (cx, cy) in enumerate(chips)]
        return copy, sends, k, chips

    def start(h_refs, o_refs, sems):
        for cp in copies(h_refs, o_refs, sems)[1]:
            cp.start()

    def finish(h_refs, o_refs, sems):
        copy, sends, k, chips = copies(h_refs, o_refs, sems)
        for a in range(n):
            for j, (cx, cy) in enumerate(chips):
                copy(a, j, k, 2 * cx + cy).wait_recv()
        for cp in sends:
            cp.wait_send()

    return _Phase(hs, [], [jax.ShapeDtypeStruct(h.shape, h.dtype) for h in hs],
                  [pltpu.SemaphoreType.DMA((3 * n,)), pltpu.SemaphoreType.DMA((3 * n,))], start, finish)


def _comm_pair_share(tag, gs):
    n = len(gs)

    def body(*refs):
        g_refs, o_refs, send_sems, recv_sems = refs[:n], refs[n:2 * n], refs[2 * n], refs[2 * n + 1]
        x, y, c, _ = _place()
        cps = [pltpu.make_async_remote_copy(
            src_ref=g_refs[a], dst_ref=o_refs[a], send_sem=send_sems.at[a], recv_sem=recv_sems.at[a],
            device_id=(x, y, 1 - c), device_id_type=MESH) for a in range(n)]
        for cp in cps:
            cp.start()
        for cp in cps:
            cp.wait()

    return pl.pallas_call(
        body, name="comm_pair_share_" + tag, in_specs=[ANY] * n, out_specs=[ANY] * n,
        out_shape=[jax.ShapeDtypeStruct(g.shape, g.dtype) for g in gs],
        scratch_shapes=[pltpu.SemaphoreType.DMA((n,)), pltpu.SemaphoreType.DMA((n,))])(*gs)


def _pad_rows(flat, unit):
    n = flat.shape[-1]
    pad = (-n) % unit
    if pad:
        flat = jnp.pad(flat, [(0, 0)] * (flat.ndim - 1) + [(0, pad)])
    return flat


def _split_chips(full, axis):
    sh = full.shape
    t = full.reshape(sh[:axis] + (N_CHIPS, sh[axis] // N_CHIPS) + sh[axis + 1:])
    return jnp.moveaxis(t, axis, 0).reshape(N_CHIPS, -1)


def _join_chips(stack, shard_shape, axis):
    t = jnp.moveaxis(stack.reshape((N_CHIPS,) + tuple(shard_shape)), 0, axis)
    sh = t.shape
    return t.reshape(sh[:axis] + (sh[axis] * sh[axis + 1],) + sh[axis + 2:])


def _block_diag(blocks):
    G, r, c = blocks.shape
    eye = jnp.eye(G, dtype=blocks.dtype)
    return (blocks[:, :, None, :] * eye[:, None, :, None]).reshape(G * r, G * c)


def _diag_blocks(m, G):
    r, c = m.shape[0] // G, m.shape[1] // G
    idx = jnp.arange(G)
    return m.reshape(G, r, G, c)[idx, :, idx, :]


def _weight_shards(w):
    conv = jnp.concatenate([w[n].reshape(-1) for n in GATHER_F32])
    conv = _pad_rows(conv, 2 * SUBLANES * LANES).reshape(-1, LANES)
    b16 = lambda a: a.astype(bf16)
    return {'ev_w_in': (b16(w['ev_w_in'][0]), "chip"), 'ev_w_out': (b16(w['ev_w_out'][0]), "rows"),
            's5_w_glu': (b16(w['s5_w_glu'][0]), "rows"), 'conv': (conv, "chip"),
            'od_w_in': (b16(w['od_w_in'][0]), "cols"), 'od_w_out': (b16(w['od_w_out'][0]), "rows"),
            'ffn_w_up0': (b16(w['ffn_w_up'][0]), "cols"), 'ffn_w_up1': (b16(w['ffn_w_up'][1]), "cols"),
            'ffn_w_down0': (b16(w['ffn_w_down'][0]), "rows"), 'ffn_w_down1': (b16(w['ffn_w_down'][1]), "rows")}


def kernel(x, mix_norm_g, ffn_norm_g, final_norm_g, ev_w_in, ev_w_out, s5_lam_re, s5_lam_im, s5_log_dt, s5_b_re, s5_b_im, s5_c_re, s5_c_im, s5_d, s5_w_glu, s5_b_glu, gm_w_s, gm_b_s, gm_v_g, od_w_in, od_conv_w, od_conv_b, od_w_out, ffn_w_up, ffn_conv_w, ffn_conv_b, ffn_w_down, loss_target, m_mix_norm_g, m_ffn_norm_g, m_final_norm_g, m_ev_w_in, m_ev_w_out, m_s5_lam_re, m_s5_lam_im, m_s5_log_dt, m_s5_b_re, m_s5_b_im, m_s5_c_re, m_s5_c_im, m_s5_d, m_s5_w_glu, m_s5_b_glu, m_gm_w_s, m_gm_b_s, m_gm_v_g, m_od_w_in, m_od_conv_w, m_od_conv_b, m_od_w_out, m_ffn_w_up, m_ffn_conv_w, m_ffn_conv_b, m_ffn_w_down, v_mix_norm_g, v_ffn_norm_g, v_final_norm_g, v_ev_w_in, v_ev_w_out, v_s5_lam_re, v_s5_lam_im, v_s5_log_dt, v_s5_b_re, v_s5_b_im, v_s5_c_re, v_s5_c_im, v_s5_d, v_s5_w_glu, v_s5_b_glu, v_gm_w_s, v_gm_b_s, v_gm_v_g, v_od_w_in, v_od_conv_w, v_od_conv_b, v_od_w_out, v_ffn_w_up, v_ffn_conv_w, v_ffn_conv_b, v_ffn_w_down):
    loc = dict(locals())
    w = {n: loc[n] for n in WEIGHTS}
    mom = {n: loc["m_" + n] for n in WEIGHTS}
    var = {n: loc["v_" + n] for n in WEIGHTS}

    B, S, D = x.shape
    T = B * S
    SW = s5_d.shape[1]
    G = SW // SSM_GROUP
    NS = G * SSM_STATE
    NB = NS // LANES
    tm = min(512, S)
    tt = min(2048, T)
    c_idx = lax.axis_index("c").astype(jnp.int32).reshape(1)
    k_idx = (2 * lax.axis_index("x") + lax.axis_index("y")).astype(jnp.int32).reshape(1)
    shards = _weight_shards(w)
    placed = {n: _place_shard("place_" + n, s, kd, k_idx) for n, (s, kd) in shards.items()}

    def gather(names):
        return _gather_phase([shards[n][0] for n in names], [placed[n] for n in names], [shards[n][1] for n in names])

    (w_ev_in,) = _run_phase("comm_gather_ev_in", gather(['ev_w_in']))
    w_ev_in = jnp.swapaxes(w_ev_in, 0, 1).reshape(D, -1)

    h0 = x.reshape(T, D)
    (y0, p0), (w_ev_out, w_glu, conv) = _norm_mm("ev_in", h0, mix_norm_g[0], w_ev_in, tm,
                                                 phase=gather(['ev_w_out', 's5_w_glu', 'conv']))
    full, off = {}, 0
    for n in GATHER_F32:
        full[n] = _join_chips(conv.reshape(N_CHIPS, -1)[:, off:off + w[n].size], w[n].shape, SHARD_AXIS[n])
        off += w[n].size
    PW = p0.shape[1]
    p03 = p0.reshape(B, S, PW)
    lr, li, ldt = s5_lam_re[0], s5_lam_im[0], s5_log_dt[0].reshape(G, 1)
    ar, ai, zr, zi = _s5_prep(lr, li, ldt)
    bre = _block_diag(jnp.swapaxes(s5_b_re[0], 1, 2))
    bim = _block_diag(jnp.swapaxes(s5_b_im[0], 1, 2))
    cbr = _block_diag(jnp.swapaxes(s5_c_re[0], 1, 2)).astype(bf16)
    cbi = _block_diag(jnp.swapaxes(s5_c_im[0], 1, 2)).astype(bf16)
    zr_row, zi_row = zr.reshape(1, NS), zi.reshape(1, NS)
    bbd = _s5_bbd(zr_row, zi_row, bre, bim)
    ar_s, ai_s = ar.reshape(NB, 1, LANES), ai.reshape(NB, 1, LANES)
    xr, xi = _s5_in(p03, bbd, SW, tm)
    (hr, hi), (w_up0,) = _s5_scan("s5_scan", xr, xi, ar_s, ai_s, False,
                                           phase=gather(['ffn_w_up0']))
    dsk, bglu = s5_d.reshape(1, SW), s5_b_glu.reshape(1, SW)
    a_out = _s5_out(hr, hi, p03, cbr, cbi, dsk, w_glu, bglu, tm)
    ws, bst, gv = gm_w_s[0], gm_b_s[0].T, gm_v_g.reshape(1, -1)
    mixcat = _gmlp(p0, a_out.reshape(T, SW), ws, bst, gv, SW)
    (h1,), (w_down0,) = _mm_resid("ev_out", mixcat, w_ev_out, h0, tm, phase=gather(['ffn_w_down0']))

    def ffn_fwd(l, h, w_up, w_down, up_phase=None, down_phase=None, head=None):
        res = _norm_mm(f"ffn_up{l}", h, ffn_norm_g[l], w_up, tm, phase=up_phase)
        (z, up), got_up = res if up_phase is not None else (res, None)
        res = _ffn_down(f"ffn_down{l}", up, full['ffn_conv_w'][l], ffn_conv_b[l].reshape(1, -1), w_down, h, S, tm,
                        phase=down_phase, head=head)
        (*hn, c), got_down = res if down_phase is not None else (res, None)
        return hn, (z, up.reshape(B, S, -1), c.reshape(B, S, -1)), got_up, got_down

    (h2,), ffn0, (w_up1, w_down1), (w_od_in, w_od_out) = ffn_fwd(
        0, h1, w_up0, w_down0, gather(['ffn_w_up1', 'ffn_w_down1']), gather(['od_w_in', 'od_w_out']))
    w_ups, w_downs = (w_up0, w_up1), (w_down0, w_down1)
    od_cw, od_cb = full['od_conv_w'][0], full['od_conv_b']
    y1, p1 = _norm_mm("od_in", h2, mix_norm_g[1], w_od_in, tm)
    p13 = p1.reshape(B, S, -1)
    h3 = _od_out(p1, od_cw, od_cb, w_od_out, h2, S, tm)
    (dh4, dh4b, loss_part, d_final_g), ffn1, _, _ = ffn_fwd(
        1, h3, w_up1, w_down1, head=(final_norm_g, loss_target.reshape(T, D)))


    grads = {}

    halves = {}
    chips = lambda g: g.reshape(N_CHIPS, -1, D)

    def pair_sums(names, parts, recv):
        return [_pair_sum(f"pair_sum_{n}", g, r, c_idx, f32 if n == "small" else bf16)
                for n, g, r in zip(names, parts, recv)]

    def reduce_end(tag, names, hsum, r3):
        mine = [_chip_sum(f"chip_sum_{n}", r, h, k_idx) for n, r, h in zip(names, r3, hsum)]
        theirs = _comm_pair_share(tag, mine)
        halves.update({n: (a, b) for n, a, b in zip(names, mine, theirs)})

    def ffn_bwd(l, dh, dhb, h_in, saved, phase=None, swap=False):
        z, up3, c3 = saved
        w_down, w_up = w_downs[l], w_ups[l]
        da = _mm_nt(f"ffn_down_bwd{l}", dhb, w_down, tm)
        res = _ffn_act_bwd(f"ffn_act_bwd{l}", up3, c3, da.reshape(B, S, -1), full['ffn_conv_w'][l], phase=phase)
        (act, dg3, dv3, dcwg, dcwv, dcbg, dcbv), got = res if phase is not None else (res, None)
        g_down = _mm_tn(f"ffn_down_dw{l}", act.reshape(T, -1), dhb, tt)
        dupg, dupv = dg3.reshape(T, -1), dv3.reshape(T, -1)
        F = dupg.shape[1]
        g_up = _mm_tn(f"ffn_up_dw{l}_gate", dupg, z, tt, rows=2 * F)
        g_up = _mm_tn(f"ffn_up_dw{l}_val", dupv, z, tt, rows=2 * F, row_off=F, prev=g_up)
        parts = [chips(g_down), chips(g_up)]
        res = _mm_nt_normbwd(f"ffn_up_bwd{l}", [dupg, dupv], w_up, h_in, ffn_norm_g[l], dh, tm,
                             phase=_swap_phase(parts) if swap else None)
        (dh_new, dhb_new, dg), recv = res if swap else (res, None)
        F = dg3.shape[2]
        dcw = jnp.concatenate([dcwg[:, :F], dcwv[:, :F]], axis=1)
        dcb = jnp.concatenate([dcbg[:, :F], dcbv[:, :F]], axis=1)
        return dh_new, dhb_new, g_down, g_up, dcw, dcb[0], dg[0], got, parts, recv

    dh3, dh3b, gd1, gu1, gcw1, gcb1, gng1, _, _, _ = ffn_bwd(1, dh4, dh4b, h3, ffn1)
    dsc = _mm_nt("od_out_bwd", dh3b, w_od_out, tm)
    sc, dbg3, dcg3, dhx3, d_od_cw, d_od_cb = _od_act_bwd(p13, dsc.reshape(B, S, D), od_cw, od_cb)
    g_od_out = _mm_tn("od_out_dw", sc.reshape(T, D), dh3b, tt)
    dp1 = [t.reshape(T, D) for t in (dbg3, dcg3, dhx3)]
    g_od_in = None
    for i, piece in enumerate(dp1):
        g_od_in = _mm_tn(f"od_in_dw{i}", piece, y1, tt, rows=3 * D, row_off=i * D, prev=g_od_in)
    grads['od_conv_w'] = d_od_cw[None]
    grads['od_conv_b'] = d_od_cb
    layer1 = ['ffn_w_down1', 'ffn_w_up1', 'od_w_out', 'od_w_in']
    parts1 = [chips(g) for g in (gd1, gu1, g_od_out, g_od_in)]
    (dh2, dh2b, gmix1), recv1 = _mm_nt_normbwd("od_in_bwd", dp1, w_od_in, h2, mix_norm_g[1], dh3, tm,
                                               phase=_swap_phase(parts1))
    hsum1 = pair_sums(layer1, parts1, recv1)
    dh1, dh1b, gd0, gu0, gcw0, gcb0, gng0, r3, parts0, recv0 = ffn_bwd(
        0, dh2, dh2b, h1, ffn0, phase=_exchange_phase(hsum1), swap=True)
    reduce_end("layer1", layer1, hsum1, r3)
    ffn0_names = ['ffn_w_down0', 'ffn_w_up0']
    hsum0 = pair_sums(ffn0_names, parts0, recv0)
    grads['ffn_conv_w'] = jnp.stack([gcw0, gcw1])
    grads['ffn_conv_b'] = jnp.stack([gcb0, gcb1])
    grads['ffn_norm_g'] = jnp.stack([gng0, gng1])
    grads['final_norm_g'] = d_final_g[0]

    dmix = _mm_nt("ev_out_bwd", dh1b, w_ev_out, tm)
    g_ev_out = _mm_tn("ev_out_dw", mixcat, dh1b, tt)
    part_evo = [chips(g_ev_out)]
    (duv, d_ws, d_bs, d_gv), recv_evo = _gmlp_bwd(p0, dmix, ws, bst, gv, SW, phase=_swap_phase(part_evo))
    hsum0 = hsum0 + pair_sums(['ev_w_out'], part_evo, recv_evo)
    ffn0_names = ffn0_names + ['ev_w_out']
    grads['gm_w_s'] = d_ws[None]
    grads['gm_b_s'] = d_bs[:, :, 0][None]
    grads['gm_v_g'] = d_gv
    dhr, dhi, du_skip, d_cbr, d_cbi, d_dsk, d_wglu, d_bglu = _s5_out_bwd(
        hr, hi, p03, dmix.reshape(B, S, D), cbr, cbi, dsk, w_glu, bglu, tm)
    grads['s5_c_re'] = jnp.swapaxes(_diag_blocks(d_cbr, G), 1, 2)[None]
    grads['s5_c_im'] = jnp.swapaxes(_diag_blocks(d_cbi, G), 1, 2)[None]
    grads['s5_d'] = d_dsk
    grads['s5_w_glu'] = d_wglu[None]
    grads['s5_b_glu'] = d_bglu
    (gr, gi, dar, dai), r3 = _s5_scan("s5_rscan", dhr, dhi, ar_s, ai_s, True, hr, hi, phase=_exchange_phase(hsum0))
    reduce_end("ffn0", ffn0_names, hsum0, r3)
    dp03, d_bbd = _s5_in_bwd(gr, gi, p03, bbd, du_skip, duv.reshape(B, S, -1), tm)
    d_bre, d_bim, d_zr, d_zi = _s5_bbd_bwd(d_bbd, zr_row, zi_row, bre, bim)
    grads['s5_b_re'] = jnp.swapaxes(_diag_blocks(d_bre, G), 1, 2)[None]
    grads['s5_b_im'] = jnp.swapaxes(_diag_blocks(d_bim, G), 1, 2)[None]
    shp = (-1, G, SSM_STATE)
    d_lr, d_li, d_ldt = _s5_prep_bwd(lr, li, ldt, dar.reshape(shp), dai.reshape(shp), d_zr.reshape(shp),
                                     d_zi.reshape(shp))
    grads['s5_lam_re'] = d_lr[None]
    grads['s5_lam_im'] = d_li[None]
    grads['s5_log_dt'] = d_ldt.reshape(1, G)
    dp0 = dp03.reshape(T, PW)
    g_ev_in = _mm_tn("ev_in_dw", dp0, y0, tt)
    grad_x, _, gmix0 = _mm_nt_normbwd("ev_in_bwd", [dp0], w_ev_in, h0, mix_norm_g[0], dh1, tm)
    grads['mix_norm_g'] = jnp.concatenate([gmix0, gmix1], axis=0)

    small = [n for n in WEIGHTS if n not in BIG]
    segs = []
    for n in small:
        gfull = grads[n].astype(f32)
        if n in SHARD_AXIS:
            segs.append(_split_chips(gfull, SHARD_AXIS[n]))
        else:
            segs.append(jnp.broadcast_to(gfull.reshape(1, -1), (N_CHIPS, gfull.size)))
    segs.append(jnp.broadcast_to(loss_part, (N_CHIPS, 1)))
    unit = 2 * SUBLANES * D
    gsmall = _pad_rows(jnp.concatenate(segs, axis=1), unit).reshape(N_CHIPS, -1, D)
    mixer0 = ['ev_w_in', 'small']
    parts = [chips(g_ev_in), gsmall]
    hsum = pair_sums(mixer0, parts, _run_phase("comm_pair_swap_mixer0", _swap_phase(parts)))
    reduce_end("mixer0", mixer0, hsum, _run_phase("comm_exchange_mixer0", _exchange_phase(hsum)))

    out_g, out_d, out_m, out_v = {}, {}, {}, {}

    def update(n, key, lead, transposed, prev=None):
        res = _adamw(f"adamw_{key}", *halves[key], c_idx, w[n], mom[n], var[n], lead, transposed, prev)
        out_g[n], out_d[n], out_m[n], out_v[n] = res
        return res

    update('ev_w_in', 'ev_w_in', 0, True)
    update('ev_w_out', 'ev_w_out', 0, False)
    update('od_w_in', 'od_w_in', 0, True)
    update('od_w_out', 'od_w_out', 0, False)
    update('ffn_w_up', 'ffn_w_up0', 0, True, prev=update('ffn_w_up', 'ffn_w_up1', 1, True))
    update('ffn_w_down', 'ffn_w_down0', 0, False, prev=update('ffn_w_down', 'ffn_w_down1', 1, False))

    mine, theirs = halves['small']
    first = lax.axis_index("c") == 0
    flat = jnp.concatenate([jnp.where(first, mine, theirs), jnp.where(first, theirs, mine)], axis=1).reshape(-1)
    off = 0
    for n in small:
        out_g[n] = flat[off:off + w[n].size].reshape(w[n].shape)
        off += w[n].size
    loss = flat[off]
    res = _adamw_small([out_g[n] for n in small], [w[n] for n in small], [mom[n] for n in small],
                       [var[n] for n in small])
    for i, n in enumerate(small):
        out_d[n], out_m[n], out_v[n] = res[i], res[len(small) + i], res[2 * len(small) + i]

    return (loss, grad_x.reshape(B, S, D), *[out_g[n] for n in WEIGHTS], *[out_d[n] for n in WEIGHTS],
            *[out_m[n] for n in WEIGHTS], *[out_v[n] for n in WEIGHTS])
```

```python
import functools
import math

import jax
import jax.numpy as jnp
from jax import lax
from jax.experimental import pallas as pl
from jax.experimental.pallas import tpu as pltpu

f32 = jnp.float32
bf16 = jnp.bfloat16
MESH = pl.DeviceIdType.MESH

SSM_GROUP = 16
SSM_STATE = 64
GMLP_HEAD = 128
CHUNK = 128
EPS = 1e-6
LAMBDA_RE_MAX = -1e-4
ADAM_LR, ADAM_B1, ADAM_B2, ADAM_EPS, ADAM_WD, ADAM_STEP = 0.001, 0.9, 0.999, 1e-08, 0.01, 10

LANES = 128
SUBLANES = 8
NSUB = 32
HALO = 16
VMEM_LIMIT = 56 * 1024 * 1024
N_CHIPS = 4

WEIGHTS = ['mix_norm_g', 'ffn_norm_g', 'final_norm_g', 'ev_w_in', 'ev_w_out', 's5_lam_re', 's5_lam_im', 's5_log_dt',
           's5_b_re', 's5_b_im', 's5_c_re', 's5_c_im', 's5_d', 's5_w_glu', 's5_b_glu', 'gm_w_s', 'gm_b_s', 'gm_v_g',
           'od_w_in', 'od_conv_w', 'od_conv_b', 'od_w_out', 'ffn_w_up', 'ffn_conv_w', 'ffn_conv_b', 'ffn_w_down']
SHARD_AXIS = {'ev_w_in': 2, 'ev_w_out': 1, 's5_w_glu': 1, 'od_w_in': 2, 'od_conv_w': 2, 'od_conv_b': 1, 'od_w_out': 1,
              'ffn_w_up': 2, 'ffn_conv_w': 2, 'ffn_w_down': 1}
GATHER_F32 = ['od_conv_w', 'od_conv_b', 'ffn_conv_w']

_GELU_K0 = math.sqrt(2.0 / math.pi)
_GELU_K1 = 0.044715
NT = (((1,), (1,)), ((), ()))
TN = (((0,), (0,)), ((), ()))


def _pick(n, cap):
    if n <= cap:
        return n
    best = None
    for d in range(LANES, cap + 1, LANES):
        if n % d == 0:
            best = d
    assert best is not None, (n, cap)
    return best


def _params(sem=None):
    return pltpu.CompilerParams(dimension_semantics=sem, vmem_limit_bytes=VMEM_LIMIT)


class _Phase:
    def __init__(self, ins, inplace, outs, sems, start, finish):
        self.ins, self.inplace, self.outs, self.sems = list(ins), list(inplace), list(outs), list(sems)
        self.start, self.finish = start, finish


def _call(body, name, grid, in_specs, out_specs, out_shape, args, scratch=(), sem=None, phase=None):
    if phase is None:
        return pl.pallas_call(body, name=name, grid=grid, in_specs=in_specs, out_specs=out_specs, out_shape=out_shape,
                              scratch_shapes=list(scratch), compiler_params=_params(sem))(*args)
    any_spec = pl.BlockSpec(memory_space=pl.ANY)
    n_in, n_out, n_scr = len(args), len(out_shape), len(scratch)
    p_in = phase.ins + phase.inplace
    ci, co = len(p_in), len(phase.outs)

    def wrapped(*refs):
        ins, cins = refs[:n_in], refs[n_in:n_in + len(phase.ins)]
        b = n_in + ci
        outs, couts = refs[b:b + n_out], refs[b + n_out:b + n_out + co]
        d = b + n_out + co
        scr, csem = refs[d:d + n_scr], refs[d + n_scr:]
        ids = [pl.program_id(i) for i in range(len(grid))]
        first = functools.reduce(jnp.logical_and, [i == 0 for i in ids])
        last = functools.reduce(jnp.logical_and, [i == g - 1 for i, g in zip(ids, grid)])

        @pl.when(first)
        def _():
            phase.start(cins, couts, csem)
        body(*ins, *outs, *scr)

        @pl.when(last)
        def _():
            phase.finish(cins, couts, csem)

    res = pl.pallas_call(
        wrapped, name=name, grid=grid, in_specs=list(in_specs) + [any_spec] * ci,
        out_specs=list(out_specs) + [any_spec] * co, out_shape=list(out_shape) + phase.outs,
        scratch_shapes=list(scratch) + phase.sems,
        input_output_aliases={n_in + len(phase.ins) + i: n_out + i for i in range(len(phase.inplace))},
        compiler_params=_params(tuple("arbitrary" for _ in grid)))(*args, *p_in)
    return res[:n_out], res[n_out:]


def _run_phase(name, phase):
    any_spec = pl.BlockSpec(memory_space=pl.ANY)
    ni, ci, co = len(phase.ins), len(phase.ins) + len(phase.inplace), len(phase.outs)

    def body(*refs):
        cins, couts, csem = refs[:ni], refs[ci:ci + co], refs[ci + co:]
        phase.start(cins, couts, csem)
        phase.finish(cins, couts, csem)

    return pl.pallas_call(
        body, name=name, in_specs=[any_spec] * ci, out_specs=[any_spec] * co, out_shape=phase.outs,
        scratch_shapes=phase.sems, input_output_aliases={ni + i: i for i in range(len(phase.inplace))})(
            *phase.ins, *phase.inplace)


def _gelu(x):
    return 0.5 * x * (1.0 + jnp.tanh(_GELU_K0 * (x + _GELU_K1 * x * x * x)))


def _gelu_grad(x):
    t = jnp.tanh(_GELU_K0 * (x + _GELU_K1 * x * x * x))
    return 0.5 * (1.0 + t) + 0.5 * x * (1.0 - t * t) * _GELU_K0 * (1.0 + 3.0 * _GELU_K1 * x * x)


def _rms_stats(x):
    r = lax.rsqrt(jnp.mean(x * x, axis=-1, keepdims=True) + EPS)
    return x * r, r


def _rms_bwd(dy, xh, r, g):
    dxh = dy * g
    dx = r * (dxh - xh * jnp.mean(dxh * xh, axis=-1, keepdims=True))
    return dx, jnp.sum(dy * xh, axis=0, keepdims=True)


def _dot(a, b):
    return jnp.dot(a, b, preferred_element_type=f32)


def _dg(a, b, dims):
    return lax.dot_general(a, b, dims, preferred_element_type=f32)


def _row_fold(z):
    return z.reshape(z.shape[0] // SUBLANES, SUBLANES, z.shape[1]).sum(axis=0)


def _norm_mm(name, h, g, w, tm, phase=None):
    T, D = h.shape
    N = w.shape[1]
    nc = _pick(N, 512)

    def body(h_ref, g_ref, w_ref, y_ref, o_ref):
        xh, _ = _rms_stats(h_ref[...])
        y = (xh * g_ref[...]).astype(bf16)
        y_ref[...] = y
        for j in range(N // nc):
            o_ref[:, j * nc:(j + 1) * nc] = _dot(y, w_ref[:, j * nc:(j + 1) * nc]).astype(bf16)

    return _call(
        body, name, (T // tm,),
        [pl.BlockSpec((tm, D), lambda i: (i, 0)), pl.BlockSpec((1, D), lambda i: (0, 0)),
         pl.BlockSpec((D, N), lambda i: (0, 0))],
        [pl.BlockSpec((tm, D), lambda i: (i, 0)), pl.BlockSpec((tm, N), lambda i: (i, 0))],
        [jax.ShapeDtypeStruct((T, D), bf16), jax.ShapeDtypeStruct((T, N), bf16)],
        [h, g.reshape(1, D), w], sem=("parallel",), phase=phase)


def _mm_resid(name, a, w, resid, tm, phase=None):
    T, K = a.shape
    N = w.shape[1]

    def body(a_ref, w_ref, r_ref, o_ref):
        o_ref[...] = r_ref[...] + _dot(a_ref[...], w_ref[...])

    return _call(
        body, name, (T // tm,),
        [pl.BlockSpec((tm, K), lambda i: (i, 0)), pl.BlockSpec((K, N), lambda i: (0, 0)),
         pl.BlockSpec((tm, N), lambda i: (i, 0))],
        [pl.BlockSpec((tm, N), lambda i: (i, 0))], [jax.ShapeDtypeStruct((T, N), f32)],
        [a, w, resid], sem=("parallel",), phase=phase)


def _mm_nt(name, dy, w, tm):
    T, N = dy.shape
    K = w.shape[0]
    kc = _pick(K, 512)

    def body(d_ref, w_ref, o_ref):
        d = d_ref[...].astype(bf16)
        for j in range(K // kc):
            o_ref[:, j * kc:(j + 1) * kc] = _dg(d, w_ref[j * kc:(j + 1) * kc, :], NT).astype(bf16)

    return pl.pallas_call(
        body, name=name, grid=(T // tm,),
        in_specs=[pl.BlockSpec((tm, N), lambda i: (i, 0)), pl.BlockSpec((K, N), lambda i: (0, 0))],
        out_specs=pl.BlockSpec((tm, K), lambda i: (i, 0)),
        out_shape=jax.ShapeDtypeStruct((T, K), bf16),
        compiler_params=_params(("parallel",)))(dy, w)


def _mm_nt_normbwd(name, dys, w, h, g, dh_in, tm, phase=None):
    n = len(dys)
    T = dys[0].shape[0]
    D = w.shape[0]
    widths = [d.shape[1] for d in dys]
    offs = [sum(widths[:i]) for i in range(n)]

    def body(*refs):
        d_refs = refs[:n]
        w_ref, h_ref, g_ref, dh_ref, o_ref, ob_ref, dg_ref = refs[n:]
        dz = _dg(d_refs[0][...], w_ref[:, :widths[0]], NT)
        for i in range(1, n):
            dz += _dg(d_refs[i][...], w_ref[:, offs[i]:offs[i] + widths[i]], NT)
        xh, r = _rms_stats(h_ref[...])
        dx, dg = _rms_bwd(dz, xh, r, g_ref[...])
        out = dh_ref[...] + dx
        o_ref[...] = out
        ob_ref[...] = out.astype(bf16)

        @pl.when(pl.program_id(0) == 0)
        def _():
            dg_ref[...] = jnp.zeros_like(dg_ref)
        dg_ref[...] += dg

    row = lambda c: pl.BlockSpec((tm, c), lambda i: (i, 0))
    return _call(
        body, name, (T // tm,),
        [row(c) for c in widths] + [pl.BlockSpec((D, sum(widths)), lambda i: (0, 0)), row(D),
                                    pl.BlockSpec((1, D), lambda i: (0, 0)), row(D)],
        [row(D), row(D), pl.BlockSpec((1, D), lambda i: (0, 0))],
        [jax.ShapeDtypeStruct((T, D), f32), jax.ShapeDtypeStruct((T, D), bf16), jax.ShapeDtypeStruct((1, D), f32)],
        [*dys, w, h, g.reshape(1, D), dh_in], sem=("arbitrary",), phase=phase)


def _mm_tn(name, a, b, tt, rows=None, row_off=0, prev=None):
    T, K = a.shape
    N = b.shape[1]
    rows = K if rows is None else rows
    tk = _pick(K, 1408)
    tn = _pick(N, 1024)
    assert row_off % tk == 0
    kb = row_off // tk

    def body(a_ref, b_ref, *rest):
        o_ref = rest[-1]

        @pl.when(pl.program_id(2) == 0)
        def _():
            o_ref[...] = jnp.zeros_like(o_ref)
        o_ref[...] += _dg(a_ref[...], b_ref[...], TN)

    in_specs = [pl.BlockSpec((tt, tk), lambda k, n, t: (t, k)), pl.BlockSpec((tt, tn), lambda k, n, t: (t, n))]
    args, aliases = [a, b], {}
    if prev is not None:
        in_specs.append(ANY)
        args.append(prev)
        aliases = {2: 0}
    return pl.pallas_call(
        body, name=name, grid=(K // tk, N // tn, T // tt), in_specs=in_specs,
        out_specs=pl.BlockSpec((tk, tn), lambda k, n, t: (k + kb, n)),
        out_shape=jax.ShapeDtypeStruct((rows, N), f32), input_output_aliases=aliases,
        compiler_params=_params(("parallel", "parallel", "arbitrary")))(*args)


def _loss_head(h, g_ref, t_ref, dh_ref, dhb_ref, loss_ref, dg_ref):
    D = h.shape[1]
    xh, r = _rms_stats(h)
    gg = g_ref[...]
    diff = xh * gg - t_ref[...]
    dx, dg = _rms_bwd(diff * (1.0 / D), xh, r, gg)
    dh_ref[...] = dx
    dhb_ref[...] = dx.astype(bf16)

    @pl.when(pl.program_id(0) == 0)
    def _():
        dg_ref[...] = jnp.zeros_like(dg_ref)
        loss_ref[...] = jnp.zeros_like(loss_ref)
    dg_ref[...] += dg
    loss_ref[...] += (0.5 / D) * jnp.sum(jnp.sum(diff * diff, axis=1, keepdims=True), axis=0, keepdims=True)


def _taps(load, r0, R):
    main = load(r0, R)
    hs = pl.multiple_of(jnp.maximum(r0 - HALO, 0), HALO)
    halo = load(hs, HALO) * (r0 > 0).astype(f32)
    ext = jnp.concatenate([halo, main], axis=0)
    xm1 = pltpu.roll(ext, 1, 0)[HALO:]
    xm2 = pltpu.roll(ext, 2, 0)[HALO:]
    return xm2, xm1, main


def _conv(w, b, taps):
    return b + w[0:1] * taps[0] + w[1:2] * taps[1] + w[2:3] * taps[2]


def _ffn_down(name, up, cw, cb, w_down, resid, S, tm, phase=None, head=None):
    T, F2 = up.shape
    F = F2 // 2
    D = w_down.shape[1]
    cwid = _pick(F, 256)
    per_seq = S // tm

    def body(u_ref, halo_ref, cw_ref, cb_ref, w_ref, r_ref, *rest):
        c_ref = rest[-1]
        keep = (pl.program_id(0) % per_seq > 0).astype(f32)

        def conv(off):
            cols = slice(off, off + cwid)
            main = u_ref[:, cols].astype(f32)
            ext = jnp.concatenate([halo_ref[:, cols].astype(f32) * keep, main], axis=0)
            taps = (pltpu.roll(ext, 2, 0)[HALO:], pltpu.roll(ext, 1, 0)[HALO:], main)
            return _conv(cw_ref[:, cols], cb_ref[:, cols], taps)

        acc = r_ref[...]
        for j in range(F // cwid):
            cg, cv = conv(j * cwid), conv(F + j * cwid)
            c_ref[:, j * cwid:(j + 1) * cwid] = cg.astype(bf16)
            c_ref[:, F + j * cwid:F + (j + 1) * cwid] = cv.astype(bf16)
            a = (cg * jax.nn.sigmoid(cg) * cv).astype(bf16)
            acc = acc + _dot(a, w_ref[j * cwid:(j + 1) * cwid, :])
        if head is None:
            rest[0][...] = acc
        else:
            _loss_head(acc, *rest[:-1])

    full = lambda r, c: pl.BlockSpec((r, c), lambda i: (0, 0))
    row = lambda c: pl.BlockSpec((tm, c), lambda i: (i, 0))
    in_specs = [row(F2), pl.BlockSpec((HALO, F2), lambda i: (jnp.maximum(i * (tm // HALO) - 1, 0), 0)),
                full(3, F2), full(1, F2), full(F, D), row(D)]
    args = [up, up, cw, cb, w_down, resid]
    if head is None:
        out_specs, out_shape = [row(D)], [jax.ShapeDtypeStruct((T, D), f32)]
    else:
        in_specs += [full(1, D), row(D)]
        args += [head[0].reshape(1, D), head[1]]
        out_specs = [row(D), row(D), full(1, 1), full(1, D)]
        out_shape = [jax.ShapeDtypeStruct((T, D), f32), jax.ShapeDtypeStruct((T, D), bf16),
                     jax.ShapeDtypeStruct((1, 1), f32), jax.ShapeDtypeStruct((1, D), f32)]
    return _call(body, name, (T // tm,), in_specs, out_specs + [row(F2)],
                 out_shape + [jax.ShapeDtypeStruct((T, F2), bf16)], args,
                 sem=("parallel",) if head is None else ("arbitrary",), phase=phase)


def _rev_conv_rows(d, nxt, w):
    R = d.shape[0]
    ext = jnp.concatenate([d, nxt], axis=0)
    n = R + HALO
    xp1 = pltpu.roll(ext, n - 1, 0)[:R]
    xp2 = pltpu.roll(ext, n - 2, 0)[:R]
    return w[2:3] * d + w[1:2] * xp1 + w[0:1] * xp2, xp1, xp2


def _conv_grad_acc(acc, dc, taps):
    return (acc[0] + _row_fold(dc * taps[0]), acc[1] + _row_fold(dc * taps[1]), acc[2] + _row_fold(dc * taps[2]),
            acc[3] + _row_fold(dc))


def _conv_grad_out(dcw_ref, dcb_ref, acc):
    @pl.when(pl.program_id(1) == 0)
    def _():
        dcw_ref[...] = jnp.zeros_like(dcw_ref)
        dcb_ref[...] = jnp.zeros_like(dcb_ref)
    for k in range(3):
        dcw_ref[k:k + 1, :] += jnp.sum(acc[k], axis=0, keepdims=True)
    dcb_ref[...] += jnp.sum(acc[3], axis=0, keepdims=True)


def _ffn_act_bwd(name, up3, c3, da3, cw, phase=None):
    B, S, F2 = up3.shape
    F = F2 // 2
    cwid = _pick(F, 256)
    nF = F // cwid
    R = min(256, S)
    nR = S // R

    def body(xg_ref, xv_ref, cg_ref, cv_ref, da_ref, wg_ref, wv_ref,
             act_ref, dg_ref, dv_ref, dcwg_ref, dcwv_ref, dcbg_ref, dcbv_ref, sum_scr):
        wg, wv = wg_ref[...], wv_ref[...]

        def half(d, nxt, w, x_ref, rows, acc, out_ref):
            out, xp1, xp2 = _rev_conv_rows(d, nxt, w)
            out_ref[rows, :] = out.astype(bf16)
            x = x_ref[rows, :].astype(f32)
            return (acc[0] + _row_fold(xp2 * x), acc[1] + _row_fold(xp1 * x), acc[2] + _row_fold(d * x),
                    acc[3] + _row_fold(d))

        def step(i, carry):
            ng, nv, accg, accv = carry
            rows = pl.ds(pl.multiple_of((nR - 1 - i) * R, R), R)
            cg, cv = cg_ref[rows, :].astype(f32), cv_ref[rows, :].astype(f32)
            da = da_ref[rows, :].astype(f32)
            sg = jax.nn.sigmoid(cg)
            act_ref[rows, :] = (cg * sg * cv).astype(bf16)
            dgate = da * cv * (sg * (1.0 + cg * (1.0 - sg)))
            dval = da * (cg * sg)
            accg = half(dgate, ng, wg, xg_ref, rows, accg, dg_ref)
            accv = half(dval, nv, wv, xv_ref, rows, accv, dv_ref)
            return dgate[:HALO], dval[:HALO], accg, accv
        z = jnp.zeros((SUBLANES, cwid), f32)
        zh = jnp.zeros((HALO, cwid), f32)
        _, _, accg, accv = lax.fori_loop(0, nR, step, (zh, zh, (z, z, z, z), (z, z, z, z)))
        j = pl.program_id(1)
        for half_i, (acc, dcw_ref, dcb_ref) in enumerate(((accg, dcwg_ref, dcbg_ref), (accv, dcwv_ref, dcbv_ref))):
            @pl.when(pl.program_id(0) == 0)
            def _():
                sum_scr[half_i, j] = jnp.zeros((SUBLANES, cwid), f32)
            for k in range(4):
                sum_scr[half_i, j, k:k + 1, :] += jnp.sum(acc[k], axis=0, keepdims=True)
            dcw_ref[...] = sum_scr[half_i, j, 0:3, :]
            dcb_ref[...] = sum_scr[half_i, j, 3:4, :]

    blk = lambda off: pl.BlockSpec((None, S, cwid), lambda b, j: (b, 0, off + j))
    wblk = lambda off: pl.BlockSpec((3, cwid), lambda b, j: (0, off + j))
    sums = lambda r: pl.BlockSpec((r, cwid), lambda b, j: (0, jnp.where(b == B - 1, j, nF)))
    half_shape = jax.ShapeDtypeStruct((B, S, F), bf16)
    return _call(
        body, name, (B, nF),
        [blk(0), blk(nF), blk(0), blk(nF), blk(0), wblk(0), wblk(nF)],
        [blk(0), blk(0), blk(0), sums(3), sums(3), sums(1), sums(1)],
        [half_shape, half_shape, half_shape, jax.ShapeDtypeStruct((3, F + cwid), f32),
         jax.ShapeDtypeStruct((3, F + cwid), f32), jax.ShapeDtypeStruct((1, F + cwid), f32),
         jax.ShapeDtypeStruct((1, F + cwid), f32)],
        [up3, up3, c3, c3, da3, cw, cw], scratch=[pltpu.VMEM((2, nF, SUBLANES, cwid), f32)],
        sem=("arbitrary", "arbitrary"), phase=phase)


def _od_out(p, cw, cb, w_out, resid, S, tm):
    T, D3 = p.shape
    D = D3 // 3
    cwid = _pick(D, 256)
    per_seq = S // tm

    def body(p_ref, halo_ref, cw_ref, cb_ref, w_ref, r_ref, o_ref, cq_ref):
        keep = (pl.program_id(0) % per_seq > 0).astype(f32)
        acc = r_ref[...]
        for j in range(D // cwid):
            cols = [slice(part * D + j * cwid, part * D + (j + 1) * cwid) for part in range(3)]
            q = p_ref[:, cols[1]].astype(f32) * p_ref[:, cols[2]].astype(f32)
            q_halo = halo_ref[:, cols[1]].astype(f32) * halo_ref[:, cols[2]].astype(f32) * keep
            ext = jnp.concatenate([q_halo, q], axis=0)
            taps = (pltpu.roll(ext, 2, 0)[HALO:], pltpu.roll(ext, 1, 0)[HALO:], q)
            cq = _conv(cw_ref[:, cols[0]], cb_ref[:, cols[0]], taps)
            cq_ref[:, cols[0]] = cq.astype(bf16)
            sc = (p_ref[:, cols[0]].astype(f32) * cq).astype(bf16)
            acc = acc + _dot(sc, w_ref[j * cwid:(j + 1) * cwid, :])
        o_ref[...] = acc

    full = lambda r, c: pl.BlockSpec((r, c), lambda i: (0, 0))
    row = lambda c: pl.BlockSpec((tm, c), lambda i: (i, 0))
    return pl.pallas_call(
        body, name="od_out", grid=(T // tm,),
        in_specs=[row(D3), pl.BlockSpec((HALO, D3), lambda i: (jnp.maximum(i * (tm // HALO) - 1, 0), 0)),
                  full(3, D), full(1, D), full(D, D), row(D)],
        out_specs=[row(D), row(D)],
        out_shape=[jax.ShapeDtypeStruct((T, D), f32), jax.ShapeDtypeStruct((T, D), bf16)],
        compiler_params=_params(("parallel",)))(p, p, cw, cb, w_out, resid)


def _od_act_bwd(p3, cq3, dsc3, cw):
    B, S, D3 = p3.shape
    D = D3 // 3
    cwid = _pick(D, 256)
    nD = D // cwid
    R = min(256, S)
    nR = S // R

    def body(bg_ref, cg_ref, hx_ref, d_ref, w_ref, cq_ref, sc_ref, dbg_ref, dcg_ref, dhx_ref, dcw_ref, dcb_ref):
        w = w_ref[...]

        def step(i, carry):
            nxt, acc = carry
            rows = pl.ds(pl.multiple_of((nR - 1 - i) * R, R), R)
            cq = cq_ref[rows, :].astype(f32)
            d = d_ref[rows, :].astype(f32)
            dbg_ref[rows, :] = (d * cq).astype(bf16)
            bg = bg_ref[rows, :].astype(f32)
            sc_ref[rows, :] = (bg * cq).astype(bf16)
            dcq = d * bg
            dq, xp1, xp2 = _rev_conv_rows(dcq, nxt, w)
            cg, hx = cg_ref[rows, :].astype(f32), hx_ref[rows, :].astype(f32)
            dcg_ref[rows, :] = (dq * hx).astype(bf16)
            dhx_ref[rows, :] = (dq * cg).astype(bf16)
            q = cg * hx
            return dcq[:HALO], (acc[0] + _row_fold(xp2 * q), acc[1] + _row_fold(xp1 * q),
                                acc[2] + _row_fold(dcq * q), acc[3] + _row_fold(dcq))
        z = jnp.zeros((SUBLANES, cwid), f32)
        _, acc = lax.fori_loop(0, nR, step, (jnp.zeros((HALO, cwid), f32), (z, z, z, z)))
        _conv_grad_out(dcw_ref, dcb_ref, acc)

    blk = lambda off: pl.BlockSpec((None, S, cwid), lambda j, b: (b, 0, off + j))
    part = jax.ShapeDtypeStruct((B, S, D), bf16)
    return pl.pallas_call(
        body, name="od_act_bwd", grid=(nD, B),
        in_specs=[blk(0), blk(nD), blk(2 * nD), blk(0), pl.BlockSpec((3, cwid), lambda j, b: (0, j)), blk(0)],
        out_specs=[blk(0), blk(0), blk(0), blk(0), pl.BlockSpec((3, cwid), lambda j, b: (0, j)),
                   pl.BlockSpec((1, cwid), lambda j, b: (0, j))],
        out_shape=[part, part, part, part, jax.ShapeDtypeStruct((3, D), f32), jax.ShapeDtypeStruct((1, D), f32)],
        compiler_params=_params(("parallel", "arbitrary")))(p3, p3, p3, dsc3, cw, cq3)


def _gmlp_parts(p, gv, SW, GW):
    uv = p[:, SW:].astype(f32)
    ge = _gelu(uv)
    u, v = ge[:, :GW], ge[:, GW:]
    vh, r = _rms_stats(v)
    return uv, u, vh, r, vh * gv


def _tril():
    rows = lax.broadcasted_iota(jnp.int32, (CHUNK, CHUNK), 0)
    cols = lax.broadcasted_iota(jnp.int32, (CHUNK, CHUNK), 1)
    return rows >= cols


def _chunks_per_step(T):
    return 4 if T % (4 * CHUNK) == 0 else 1


def _gmlp(p0, a_out, ws, bst, gv, SW):
    T, PW = p0.shape
    GW = (PW - SW) // 2
    H = GW // GMLP_HEAD
    D = SW + GW

    kc = _chunks_per_step(T)
    rb = kc * CHUNK

    def body(p_ref, a_ref, ws_ref, b_ref, gv_ref, o_ref):
        tri = _tril()
        o_ref[:, :SW] = a_ref[...]
        wm = [jnp.where(tri, ws_ref[hh], 0.0).astype(bf16) for hh in range(H)]
        for q in range(kc):
            rows = slice(q * CHUNK, (q + 1) * CHUNK)
            _, u, _, _, vn = _gmlp_parts(p_ref[rows, :], gv_ref[...], SW, GW)
            for hh in range(H):
                sl = slice(hh * GMLP_HEAD, (hh + 1) * GMLP_HEAD)
                gate = _dot(wm[hh], vn[:, sl].astype(bf16)) + b_ref[:, hh:hh + 1]
                o_ref[rows, SW + hh * GMLP_HEAD:SW + (hh + 1) * GMLP_HEAD] = (u[:, sl] * gate).astype(bf16)

    return pl.pallas_call(
        body, name="gmlp", grid=(T // rb,),
        in_specs=[pl.BlockSpec((rb, PW), lambda i: (i, 0)), pl.BlockSpec((rb, SW), lambda i: (i, 0)),
                  pl.BlockSpec((H, CHUNK, CHUNK), lambda i: (0, 0, 0)), pl.BlockSpec((CHUNK, H), lambda i: (0, 0)),
                  pl.BlockSpec((1, GW), lambda i: (0, 0))],
        out_specs=pl.BlockSpec((rb, D), lambda i: (i, 0)),
        out_shape=jax.ShapeDtypeStruct((T, D), bf16),
        compiler_params=_params(("parallel",)))(p0, a_out, ws, bst, gv)


def _gmlp_bwd(p0, dmix, ws, bst, gv, SW, phase=None):
    T, PW = p0.shape
    GW = (PW - SW) // 2
    H = GW // GMLP_HEAD
    D = SW + GW

    kc = _chunks_per_step(T)
    rb = kc * CHUNK

    def body(p_ref, d_ref, ws_ref, b_ref, gv_ref, duv_ref, dws_ref, dbs_ref, dgv_ref):
        gv_ = gv_ref[...]
        tri = _tril()

        @pl.when(pl.program_id(0) == 0)
        def _():
            dws_ref[...] = jnp.zeros_like(dws_ref)
            dbs_ref[...] = jnp.zeros_like(dbs_ref)
            dgv_ref[...] = jnp.zeros_like(dgv_ref)
        wm = [jnp.where(tri, ws_ref[hh], 0.0).astype(bf16) for hh in range(H)]
        for q in range(kc):
            rows = slice(q * CHUNK, (q + 1) * CHUNK)
            uv, u, vh, r, vn = _gmlp_parts(p_ref[rows, :], gv_, SW, GW)
            dout = d_ref[rows, SW:].astype(f32)
            du, dvn = [], []
            for hh in range(H):
                sl = slice(hh * GMLP_HEAD, (hh + 1) * GMLP_HEAD)
                vnh = vn[:, sl].astype(bf16)
                gate = _dot(wm[hh], vnh) + b_ref[:, hh:hh + 1]
                dgate = dout[:, sl] * u[:, sl]
                du.append(dout[:, sl] * gate)
                dgb = dgate.astype(bf16)
                dws_ref[hh] += jnp.where(tri, _dg(dgb, vnh, NT), 0.0)
                dbs_ref[hh] += jnp.broadcast_to(jnp.sum(dgate, axis=1, keepdims=True), (CHUNK, CHUNK))
                dvn.append(_dg(wm[hh], dgb, TN))
            dvn = jnp.concatenate(dvn, axis=1)
            dv, dgv = _rms_bwd(dvn, vh, r, gv_)
            dgv_ref[...] += dgv
            dge = jnp.concatenate(du + [dv], axis=1)
            duv_ref[rows, :] = (dge * _gelu_grad(uv)).astype(bf16)

    fixed = pl.BlockSpec((H, CHUNK, CHUNK), lambda i: (0, 0, 0))
    return _call(
        body, "gmlp_bwd", (T // rb,),
        [pl.BlockSpec((rb, PW), lambda i: (i, 0)), pl.BlockSpec((rb, D), lambda i: (i, 0)), fixed,
         pl.BlockSpec((CHUNK, H), lambda i: (0, 0)), pl.BlockSpec((1, GW), lambda i: (0, 0))],
        [pl.BlockSpec((rb, 2 * GW), lambda i: (i, 0)), fixed, fixed, pl.BlockSpec((1, GW), lambda i: (0, 0))],
        [jax.ShapeDtypeStruct((T, 2 * GW), bf16), jax.ShapeDtypeStruct((H, CHUNK, CHUNK), f32),
         jax.ShapeDtypeStruct((H, CHUNK, CHUNK), f32), jax.ShapeDtypeStruct((1, GW), f32)],
        [p0, dmix, ws, bst, gv], sem=("arbitrary",), phase=phase)


def _s5_disc(lr, li, ldt):
    lr = jnp.minimum(lr, LAMBDA_RE_MAX)
    dt = jnp.exp(ldt)
    mag = jnp.exp(lr * dt)
    ar = mag * jnp.cos(li * dt)
    ai = mag * jnp.sin(li * dt)
    den = lr * lr + li * li
    nr = ar - 1.0
    zr = (nr * lr + ai * li) / den
    zi = (ai * lr - nr * li) / den
    return ar, ai, zr, zi


def _s5_prep(lr, li, ldt):
    G, P = lr.shape

    def body(lr_ref, li_ref, ldt_ref, ar_ref, ai_ref, zr_ref, zi_ref):
        ar, ai, zr, zi = _s5_disc(lr_ref[...], li_ref[...], ldt_ref[...])
        ar_ref[...] = ar
        ai_ref[...] = ai
        zr_ref[...] = zr
        zi_ref[...] = zi

    s = jax.ShapeDtypeStruct((G, P), f32)
    return pl.pallas_call(body, name="s5_prep", out_shape=[s, s, s, s])(lr, li, ldt)


def _s5_prep_bwd(lr, li, ldt, dar, dai, dzr, dzi):
    G, P = lr.shape

    def body(lr_ref, li_ref, ldt_ref, dar_ref, dai_ref, dzr_ref, dzi_ref, o1, o2, o3):
        _, vjp = jax.vjp(_s5_disc, lr_ref[...], li_ref[...], ldt_ref[...])
        cts = tuple(jnp.sum(r[...], axis=0) for r in (dar_ref, dai_ref, dzr_ref, dzi_ref))
        a, b, c = vjp(cts)
        o1[...] = a
        o2[...] = b
        o3[...] = c

    s = jax.ShapeDtypeStruct((G, P), f32)
    return pl.pallas_call(body, name="s5_prep_bwd", out_shape=[s, s, jax.ShapeDtypeStruct((G, 1), f32)])(
        lr, li, ldt, dar, dai, dzr, dzi)


def _s5_bbd(zr, zi, bre, bim):
    SW, NS = bre.shape

    def body(zr_ref, zi_ref, br_ref, bi_ref, o_ref):
        zr_, zi_, br, bi = zr_ref[...], zi_ref[...], br_ref[...], bi_ref[...]
        o_ref[:, :NS] = (zr_ * br - zi_ * bi).astype(bf16)
        o_ref[:, NS:] = (zr_ * bi + zi_ * br).astype(bf16)

    return pl.pallas_call(body, name="s5_bbd", out_shape=jax.ShapeDtypeStruct((SW, 2 * NS), bf16))(zr, zi, bre, bim)


def _s5_bbd_bwd(dbbd, zr, zi, bre, bim):
    SW, NS = bre.shape

    def body(d_ref, zr_ref, zi_ref, br_ref, bi_ref, dbr_ref, dbi_ref, dzr_ref, dzi_ref):
        zr_, zi_, br, bi = zr_ref[...], zi_ref[...], br_ref[...], bi_ref[...]
        dr, di = d_ref[:, :NS], d_ref[:, NS:]
        dbr_ref[...] = zr_ * dr + zi_ * di
        dbi_ref[...] = zr_ * di - zi_ * dr
        dzr_ref[...] = jnp.sum(dr * br + di * bi, axis=0, keepdims=True)
        dzi_ref[...] = jnp.sum(di * br - dr * bi, axis=0, keepdims=True)

    m = jax.ShapeDtypeStruct((SW, NS), f32)
    v = jax.ShapeDtypeStruct((1, NS), f32)
    return pl.pallas_call(body, name="s5_bbd_bwd", out_shape=[m, m, v, v])(dbbd, zr, zi, bre, bim)


def _slab_cat(ref, NB):
    return jnp.concatenate([ref[j] for j in range(NB)], axis=1)


def _s5_in(p3, bbd, SW, tm):
    B, S, PW = p3.shape
    NS = bbd.shape[1] // 2
    NB = NS // LANES

    def body(u_ref, b_ref, xr_ref, xi_ref):
        x = _dot(u_ref[...], b_ref[...])
        for j in range(NB):
            xr_ref[j] = x[:, j * LANES:(j + 1) * LANES]
            xi_ref[j] = x[:, NS + j * LANES:NS + (j + 1) * LANES]

    slab = jax.ShapeDtypeStruct((B, NB, S, LANES), f32)
    sspec = pl.BlockSpec((None, NB, tm, LANES), lambda b, i: (b, 0, i, 0))
    return pl.pallas_call(
        body, name="s5_in", grid=(B, S // tm),
        in_specs=[pl.BlockSpec((None, tm, SW), lambda b, i: (b, i, 0)), pl.BlockSpec((SW, 2 * NS), lambda b, i: (0, 0))],
        out_specs=[sspec, sspec], out_shape=[slab, slab],
        compiler_params=_params(("parallel", "parallel")))(p3, bbd)


def _s5_scan(name, xr, xi, ar, ai, reverse, hr=None, hi=None, phase=None):
    B, NB, S, _ = xr.shape
    L = S // NSUB
    nb = 2 if (hr is None and NB % 2 == 0) else 1
    with_da = hr is not None

    def body(*refs):
        if with_da:
            xr_ref, xi_ref, ar_ref, ai_ref, hr_ref, hi_ref, or_ref, oi_ref, dar_ref, dai_ref, pr_scr, pi_scr = refs
        else:
            xr_ref, xi_ref, ar_ref, ai_ref, or_ref, oi_ref, pr_scr, pi_scr = refs
        sign = -1.0 if reverse else 1.0
        a_r = [jnp.broadcast_to(ar_ref[j], (NSUB, LANES)) for j in range(nb)]
        a_i = [jnp.broadcast_to(ai_ref[j], (NSUB, LANES)) * sign for j in range(nb)]

        def step(t, carry):
            row = (L - 1 - t) if reverse else t
            rows = pl.ds(row, NSUB, stride=L)
            out = []
            for j in range(nb):
                sr, si, pr, pi = carry[j]
                nr = a_r[j] * sr - a_i[j] * si + xr_ref.at[j][rows, :]
                ni = a_r[j] * si + a_i[j] * sr + xi_ref.at[j][rows, :]
                or_ref.at[j][rows, :] = nr
                oi_ref.at[j][rows, :] = ni
                npr = a_r[j] * pr - a_i[j] * pi
                npi = a_r[j] * pi + a_i[j] * pr
                pr_scr[j, pl.ds(row, 1), :] = npr[0:1]
                pi_scr[j, pl.ds(row, 1), :] = npi[0:1]
                out.append((nr, ni, npr, npi))
            return tuple(out)
        z = jnp.zeros((NSUB, LANES), f32)
        one = jnp.ones((NSUB, LANES), f32)
        fin = lax.fori_loop(0, L, step, tuple((z, z, one, z) for _ in range(nb)))

        for j in range(nb):
            sr, si, plr, pli = fin[j]
            plr, pli = plr[0:1], pli[0:1]
            cr = jnp.zeros((1, LANES), f32)
            ci = jnp.zeros((1, LANES), f32)
            order = range(NSUB - 2, -1, -1) if reverse else range(1, NSUB)
            for c in order:
                src = c + 1 if reverse else c - 1
                cr, ci = (sr[src:src + 1] + plr * cr - pli * ci, si[src:src + 1] + plr * ci + pli * cr)
                rows = slice(c * L, (c + 1) * L)
                tr, ti = pr_scr[j], pi_scr[j]
                or_ref[j, rows, :] += tr * cr - ti * ci
                oi_ref[j, rows, :] += tr * ci + ti * cr
            if with_da:
                first = lax.broadcasted_iota(jnp.int32, (L, LANES), 0) == 0
                dar = jnp.zeros((1, LANES), f32)
                dai = jnp.zeros((1, LANES), f32)
                for c in range(NSUB):
                    rows = slice(c * L, (c + 1) * L)
                    if c == 0:
                        lr_, li_ = jnp.zeros((1, LANES), f32), jnp.zeros((1, LANES), f32)
                    else:
                        lr_, li_ = hr_ref[j, c * L - 1:c * L, :], hi_ref[j, c * L - 1:c * L, :]
                    hpr = jnp.where(first, lr_, pltpu.roll(hr_ref[j, rows, :], 1, 0))
                    hpi = jnp.where(first, li_, pltpu.roll(hi_ref[j, rows, :], 1, 0))
                    gr, gi = or_ref[j, rows, :], oi_ref[j, rows, :]
                    dar += jnp.sum(hpr * gr + hpi * gi, axis=0, keepdims=True)
                    dai += jnp.sum(hpr * gi - hpi * gr, axis=0, keepdims=True)
                dar_ref[j] = dar
                dai_ref[j] = dai

    slab = jax.ShapeDtypeStruct((B, NB, S, LANES), f32)
    sspec = pl.BlockSpec((None, nb, S, LANES), lambda b, j: (b, j, 0, 0))
    aspec = pl.BlockSpec((nb, 1, LANES), lambda b, j: (j, 0, 0))
    in_specs = [sspec, sspec, aspec, aspec]
    out_specs = [sspec, sspec]
    out_shape = [slab, slab]
    args = [xr, xi, ar, ai]
    if with_da:
        in_specs += [sspec, sspec]
        args += [hr, hi]
        dspec = pl.BlockSpec((None, nb, 1, LANES), lambda b, j: (b, j, 0, 0))
        out_specs += [dspec, dspec]
        out_shape += [jax.ShapeDtypeStruct((B, NB, 1, LANES), f32)] * 2
    return _call(body, name, (B, NB // nb), in_specs, out_specs, out_shape, args,
                 scratch=[pltpu.VMEM((nb, L, LANES), f32), pltpu.VMEM((nb, L, LANES), f32)],
                 sem=("parallel", "parallel"), phase=phase)


def _s5_out_parts(hr_ref, hi_ref, u_ref, cr_ref, ci_ref, d_ref, wg_ref, bg_ref, NB):
    hcr = _slab_cat(hr_ref, NB).astype(bf16)
    hci = _slab_cat(hi_ref, NB).astype(bf16)
    u = u_ref[...].astype(f32)
    y2 = _dot(hcr, cr_ref[...]) - _dot(hci, ci_ref[...]) + d_ref[...] * u
    yg = _gelu(y2)
    s = jax.nn.sigmoid(_dot(yg.astype(bf16), wg_ref[...]) + bg_ref[...])
    return hcr, hci, u, y2, yg, s


def _s5_out_specs(B, S, NB, NS, SW, tm):
    sspec = pl.BlockSpec((None, NB, tm, LANES), lambda b, i: (b, 0, i, 0))
    full = lambda r, c: pl.BlockSpec((r, c), lambda b, i: (0, 0))
    return sspec, [sspec, sspec, pl.BlockSpec((None, tm, SW), lambda b, i: (b, i, 0)), full(NS, SW), full(NS, SW),
                   full(1, SW), full(SW, SW), full(1, SW)]


def _s5_out(hr, hi, p3, cbr, cbi, dsk, wglu, bglu, tm):
    B, NB, S, _ = hr.shape
    NS, SW = cbr.shape

    def body(hr_ref, hi_ref, u_ref, cr_ref, ci_ref, d_ref, wg_ref, bg_ref, o_ref):
        _, _, _, _, yg, s = _s5_out_parts(hr_ref, hi_ref, u_ref, cr_ref, ci_ref, d_ref, wg_ref, bg_ref, NB)
        o_ref[...] = (yg * s).astype(bf16)

    _, in_specs = _s5_out_specs(B, S, NB, NS, SW, tm)
    return pl.pallas_call(
        body, name="s5_out", grid=(B, S // tm), in_specs=in_specs,
        out_specs=pl.BlockSpec((None, tm, SW), lambda b, i: (b, i, 0)),
        out_shape=jax.ShapeDtypeStruct((B, S, SW), bf16),
        compiler_params=_params(("parallel", "parallel")))(hr, hi, p3, cbr, cbi, dsk, wglu, bglu)


def _s5_out_bwd(hr, hi, p3, dmix3, cbr, cbi, dsk, wglu, bglu, tm):
    B, NB, S, _ = hr.shape
    NS, SW = cbr.shape

    def body(hr_ref, hi_ref, u_ref, cr_ref, ci_ref, d_ref, wg_ref, bg_ref, da_ref,
             dhr_ref, dhi_ref, du_ref, dcr_ref, dci_ref, dd_ref, dwg_ref, dbg_ref):
        hcr, hci, u, y2, yg, s = _s5_out_parts(hr_ref, hi_ref, u_ref, cr_ref, ci_ref, d_ref, wg_ref, bg_ref, NB)
        da = da_ref[...].astype(f32)
        dz = da * yg * s * (1.0 - s)
        dzb = dz.astype(bf16)
        dyg = da * s + _dg(dzb, wg_ref[...], NT)
        dy2 = dyg * _gelu_grad(y2)
        dyb = dy2.astype(bf16)

        @pl.when((pl.program_id(0) == 0) & (pl.program_id(1) == 0))
        def _():
            for r in (dcr_ref, dci_ref, dd_ref, dwg_ref, dbg_ref):
                r[...] = jnp.zeros_like(r)
        dwg_ref[...] += _dg(yg.astype(bf16), dzb, TN)
        dbg_ref[...] += jnp.sum(dz, axis=0, keepdims=True)
        dd_ref[...] += jnp.sum(dy2 * u, axis=0, keepdims=True)
        dcr_ref[...] += _dg(hcr, dyb, TN)
        dci_ref[...] -= _dg(hci, dyb, TN)
        du_ref[...] = dy2 * d_ref[...]
        dhr = _dg(dyb, cr_ref[...], NT)
        dhi = _dg(dyb, ci_ref[...], NT)
        for j in range(NB):
            dhr_ref[j] = dhr[:, j * LANES:(j + 1) * LANES]
            dhi_ref[j] = -dhi[:, j * LANES:(j + 1) * LANES]

    sspec, in_specs = _s5_out_specs(B, S, NB, NS, SW, tm)
    in_specs = in_specs + [pl.BlockSpec((None, tm, SW), lambda b, i: (b, i, 0))]
    full = lambda r, c: pl.BlockSpec((r, c), lambda b, i: (0, 0))
    slab = jax.ShapeDtypeStruct((B, NB, S, LANES), f32)
    mat = lambda r, c: jax.ShapeDtypeStruct((r, c), f32)
    return pl.pallas_call(
        body, name="s5_out_bwd", grid=(B, S // tm), in_specs=in_specs,
        out_specs=[sspec, sspec, pl.BlockSpec((None, tm, SW), lambda b, i: (b, i, 0)), full(NS, SW), full(NS, SW),
                   full(1, SW), full(SW, SW), full(1, SW)],
        out_shape=[slab, slab, jax.ShapeDtypeStruct((B, S, SW), f32), mat(NS, SW), mat(NS, SW), mat(1, SW),
                   mat(SW, SW), mat(1, SW)],
        compiler_params=_params(("arbitrary", "arbitrary")))(hr, hi, p3, cbr, cbi, dsk, wglu, bglu, dmix3)


def _s5_in_bwd(gr, gi, p3, bbd, du_skip, duv3, tm):
    B, NB, S, _ = gr.shape
    SW, NS2 = bbd.shape
    PW = SW + duv3.shape[2]

    def body(gr_ref, gi_ref, u_ref, b_ref, ds_ref, duv_ref, dp_ref, db_ref):
        g = jnp.concatenate([_slab_cat(gr_ref, NB), _slab_cat(gi_ref, NB)], axis=1).astype(bf16)
        du = _dg(g, b_ref[...], NT) + ds_ref[...]
        dp_ref[:, :SW] = du.astype(bf16)
        dp_ref[:, SW:] = duv_ref[...]

        @pl.when((pl.program_id(0) == 0) & (pl.program_id(1) == 0))
        def _():
            db_ref[...] = jnp.zeros_like(db_ref)
        db_ref[...] += _dg(u_ref[...], g, TN)

    sspec = pl.BlockSpec((None, NB, tm, LANES), lambda b, i: (b, 0, i, 0))
    row = lambda c: pl.BlockSpec((None, tm, c), lambda b, i: (b, i, 0))
    return pl.pallas_call(
        body, name="s5_in_bwd", grid=(B, S // tm),
        in_specs=[sspec, sspec, row(SW), pl.BlockSpec((SW, NS2), lambda b, i: (0, 0)), row(SW), row(PW - SW)],
        out_specs=[row(PW), pl.BlockSpec((SW, NS2), lambda b, i: (0, 0))],
        out_shape=[jax.ShapeDtypeStruct((B, S, PW), bf16), jax.ShapeDtypeStruct((SW, NS2), f32)],
        compiler_params=_params(("arbitrary", "arbitrary")))(gr, gi, p3, bbd, du_skip, duv3)


BIG = ['ev_w_in', 'ev_w_out', 'od_w_in', 'od_w_out', 'ffn_w_up', 'ffn_w_down']
ANY = pl.BlockSpec(memory_space=pl.ANY)


def _rtile(rows, mult):
    best = None
    for d in range(mult, min(rows, 512) + 1, mult):
        if rows % d == 0:
            best = d
    assert best is not None, (rows, mult)
    return best


def _pair_sum(name, g, recv, c_idx, out_dtype):
    NCH, R, W = g.shape
    HALF_W = W // 2
    tr = _rtile(R, 16)

    def body(c_ref, a_ref, b_ref, o_ref):
        o_ref[...] = (a_ref[...] + b_ref[...]).astype(out_dtype)

    return pl.pallas_call(
        body, name=name,
        grid_spec=pltpu.PrefetchScalarGridSpec(
            num_scalar_prefetch=1, grid=(NCH, R // tr),
            in_specs=[pl.BlockSpec((None, tr, HALF_W), lambda j, i, c: (j, i, c[0])),
                      pl.BlockSpec((None, tr, HALF_W), lambda j, i, c: (j, i, 0))],
            out_specs=pl.BlockSpec((None, tr, HALF_W), lambda j, i, c: (j, i, 0))),
        out_shape=jax.ShapeDtypeStruct((NCH, R, HALF_W), out_dtype),
        compiler_params=_params(("parallel", "parallel")))(c_idx, g, recv)


def _chip_sum(name, r3, h, k_idx):
    NCH, R, Wh = r3.shape
    tr = _rtile(R, 16)

    def body(k_ref, a_ref, own_ref, o_ref):
        own = own_ref[...].astype(f32)
        t = [jnp.where(k_ref[0] == s, own, a_ref[s].astype(f32)) for s in range(NCH)]
        o_ref[...] = ((t[0] + t[1]) + t[2]) + t[3]

    return pl.pallas_call(
        body, name=name,
        grid_spec=pltpu.PrefetchScalarGridSpec(
            num_scalar_prefetch=1, grid=(R // tr,),
            in_specs=[pl.BlockSpec((NCH, tr, Wh), lambda i, k: (0, i, 0)),
                      pl.BlockSpec((None, tr, Wh), lambda i, k: (k[0], i, 0))],
            out_specs=pl.BlockSpec((tr, Wh), lambda i, k: (i, 0))),
        out_shape=jax.ShapeDtypeStruct((R, Wh), f32),
        compiler_params=_params(("parallel",)))(k_idx, r3, h)


def _adam_math(gg, w, m, v):
    nm = ADAM_B1 * m + (1.0 - ADAM_B1) * gg
    nv = ADAM_B2 * v + (1.0 - ADAM_B2) * jnp.square(gg)
    m_hat = nm / (1.0 - ADAM_B1 ** ADAM_STEP)
    v_hat = nv / (1.0 - ADAM_B2 ** ADAM_STEP)
    return -ADAM_LR * (m_hat / (jnp.sqrt(v_hat) + ADAM_EPS) + ADAM_WD * w), nm, nv


def _adamw(name, mine, theirs, c_idx, w, m, v, lead, transposed, prev=None):
    L, R, W = w.shape
    if transposed:
        bw = LANES if W % LANES == 0 else W
        gspec = pl.BlockSpec((bw, R // 2), lambda i, hf, c: (i, 0))
        wspec = pl.BlockSpec((None, R // 2, bw), lambda i, hf, c: (lead, hf, i))
        grid = (W // bw, 2)
    else:
        tr = _rtile(R, SUBLANES)
        gspec = pl.BlockSpec((tr, W // 2), lambda i, hf, c: (i, 0))
        wspec = pl.BlockSpec((None, tr, W // 2), lambda i, hf, c: (lead, i, hf))
        grid = (R // tr, 2)

    def body(c_ref, a_ref, b_ref, w_ref, m_ref, v_ref, *rest):
        go_ref, d_ref, nm_ref, nv_ref = rest[-4:]
        gg = jnp.where(pl.program_id(1) == c_ref[0], a_ref[...], b_ref[...])
        if transposed:
            gg = gg.T
        d, nm, nv = _adam_math(gg, w_ref[...], m_ref[...], v_ref[...])
        go_ref[...] = gg
        d_ref[...] = d
        nm_ref[...] = nm
        nv_ref[...] = nv

    in_specs = [gspec, gspec, wspec, wspec, wspec]
    args, aliases = [c_idx, mine, theirs, w, m, v], {}
    if prev is not None:
        in_specs += [ANY] * 4
        args += list(prev)
        aliases = {6: 0, 7: 1, 8: 2, 9: 3}
    s = jax.ShapeDtypeStruct((L, R, W), f32)
    return pl.pallas_call(
        body, name=name,
        grid_spec=pltpu.PrefetchScalarGridSpec(num_scalar_prefetch=1, grid=grid, in_specs=in_specs,
                                               out_specs=[wspec] * 4),
        out_shape=[s, s, s, s], input_output_aliases=aliases,
        compiler_params=_params(("parallel", "arbitrary")))(*args)


def _adamw_small(gs, ws, ms, vs):
    n = len(gs)

    def body(*refs):
        for i in range(n):
            d, nm, nv = _adam_math(refs[i][...], refs[n + i][...], refs[2 * n + i][...], refs[3 * n + i][...])
            refs[4 * n + i][...] = d
            refs[5 * n + i][...] = nm
            refs[6 * n + i][...] = nv

    return pl.pallas_call(body, name="adamw_small",
                          out_shape=[jax.ShapeDtypeStruct(t.shape, f32) for t in ws] * 3)(*gs, *ws, *ms, *vs)


def _place():
    x, y, c = lax.axis_index("x"), lax.axis_index("y"), lax.axis_index("c")
    return x, y, c, [(1 - x, y), (x, 1 - y), (1 - x, 1 - y)]


def _gathered_shape(sh, kind):
    if kind == "rows":
        return sh[:-2] + (N_CHIPS * sh[-2], sh[-1])
    if kind == "cols":
        return sh[:-1] + (N_CHIPS * sh[-1],)
    return (N_CHIPS,) + sh


def _place_shard(name, shard, kind, k_idx):
    sh = shard.shape
    r, C = sh[-2], sh[-1]
    L = sh[0] if len(sh) == 3 else 1
    tr = _rtile(r, 16)
    nr = r // tr
    if kind == "rows":
        out3, omap = (L, N_CHIPS * r, C), lambda l, i, k: (l, k[0] * nr + i, 0)
    elif kind == "cols":
        out3, omap = (L, r, N_CHIPS * C), lambda l, i, k: (l, i, k[0])
    else:
        out3, omap = (N_CHIPS, r, C), lambda l, i, k: (k[0], i, 0)

    def body(k_ref, s_ref, o_ref):
        o_ref[...] = s_ref[...]

    out = pl.pallas_call(
        body, name=name,
        grid_spec=pltpu.PrefetchScalarGridSpec(
            num_scalar_prefetch=1, grid=(L, nr),
            in_specs=[pl.BlockSpec((None, tr, C), lambda l, i, k: (l, i, 0))],
            out_specs=pl.BlockSpec((None, tr, C), omap)),
        out_shape=jax.ShapeDtypeStruct(out3, shard.dtype),
        compiler_params=_params(("parallel", "parallel")))(k_idx, shard.reshape(L, r, C))
    return out.reshape(_gathered_shape(sh, kind))


def _gather_phase(shards, fulls, kinds):
    n = len(shards)
    shapes = [s.shape for s in shards]

    def window(ref, a, k, h=None):
        sh, kind = shapes[a], kinds[a]
        r = sh[-2]
        start, size = (0, r) if h is None else (h * (r // 2), r // 2)
        lead = (slice(None),) * (len(sh) - 2)
        if kind == "rows":
            return ref.at[lead + (pl.ds(k * r + start, size), slice(None))]
        if kind == "cols":
            return ref.at[lead + (pl.ds(start, size), pl.ds(pl.multiple_of(k * sh[-1], LANES), sh[-1]))]
        return ref.at[(k,) + lead + (pl.ds(start, size), slice(None))]

    def copies(s_refs, o_refs, sems):
        send_sems, recv_sems = sems
        x, y, c, chips = _place()
        k = 2 * x + y

        def copy(a, j, kk, hh, to, src=None):
            dst = window(o_refs[a], a, kk, hh)
            return pltpu.make_async_remote_copy(
                src_ref=dst if src is None else src, dst_ref=dst, send_sem=send_sems.at[6 * a + j],
                recv_sem=recv_sems.at[6 * a + j], device_id=to, device_id_type=MESH)

        first = []
        for a in range(n):
            r = shapes[a][-2]
            lead = (slice(None),) * (len(shapes[a]) - 2)
            src = s_refs[a].at[lead + (pl.ds(c * (r // 2), r // 2), slice(None))]
            first += [copy(a, j, k, c, (*chip, c), src=src) for j, chip in enumerate(chips)]
        return copy, first, (x, y, c), (x, y, 1 - c), c, chips

    def start(s_refs, o_refs, sems):
        for cp in copies(s_refs, o_refs, sems)[1]:
            cp.start()

    def finish(s_refs, o_refs, sems):
        copy, first, me, sibling, c, chips = copies(s_refs, o_refs, sems)
        passed = []
        for j, (cx, cy) in enumerate(chips):
            for a in range(n):
                copy(a, j, 2 * cx + cy, c, me).wait_recv()
                fwd = copy(a, 3 + j, 2 * cx + cy, c, sibling)
                fwd.start()
                passed.append(fwd)
        for j, (cx, cy) in enumerate(chips):
            for a in range(n):
                copy(a, 3 + j, 2 * cx + cy, 1 - c, me).wait_recv()
        for cp in first + passed:
            cp.wait_send()

    return _Phase(shards, fulls, [jax.ShapeDtypeStruct(f.shape, f.dtype) for f in fulls],
                  [pltpu.SemaphoreType.DMA((6 * n,)), pltpu.SemaphoreType.DMA((6 * n,))], start, finish)


def _swap_phase(gs):
    n = len(gs)

    def copies(g_refs, o_refs, sems):
        send_sems, recv_sems = sems
        x, y, c, _ = _place()
        half = [g.shape[2] // 2 for g in gs]
        return [pltpu.make_async_remote_copy(
            src_ref=g_refs[a].at[:, :, pl.ds(pl.multiple_of((1 - c) * half[a], LANES), half[a])], dst_ref=o_refs[a],
            send_sem=send_sems.at[a], recv_sem=recv_sems.at[a], device_id=(x, y, 1 - c), device_id_type=MESH)
            for a in range(n)]

    def start(g_refs, o_refs, sems):
        for cp in copies(g_refs, o_refs, sems):
            cp.start()

    def finish(g_refs, o_refs, sems):
        for cp in copies(g_refs, o_refs, sems):
            cp.wait()

    return _Phase(gs, [], [jax.ShapeDtypeStruct(g.shape[:2] + (g.shape[2] // 2,), g.dtype) for g in gs],
                  [pltpu.SemaphoreType.DMA((n,)), pltpu.SemaphoreType.DMA((n,))], start, finish)


def _exchange_phase(hs):
    n = len(hs)

    def copies(h_refs, o_refs, sems):
        send_sems, recv_sems = sems
        x, y, c, chips = _place()
        k = 2 * x + y

        def copy(a, j, src_slot, dst_slot):
            cx, cy = chips[j]
            return pltpu.make_async_remote_copy(
                src_ref=h_refs[a].at[src_slot], dst_ref=o_refs[a].at[dst_slot], send_sem=send_sems.at[3 * a + j],
                recv_sem=recv_sems.at[3 * a + j], device_id=(cx, cy, c), device_id_type=MESH)

        sends = [copy(a, j, 2 * cx + cy, k) for a in range(n) for j, (cx, cy) in enumerate(chips)]
        return copy, sends, k, chips

    def start(h_refs, o_refs, sems):
        for cp in copies(h_refs, o_refs, sems)[1]:
            cp.start()

    def finish(h_refs, o_refs, sems):
        copy, sends, k, chips = copies(h_refs, o_refs, sems)
        for a in range(n):
            for j, (cx, cy) in enumerate(chips):
                copy(a, j, k, 2 * cx + cy).wait_recv()
        for cp in sends:
            cp.wait_send()

    return _Phase(hs, [], [jax.ShapeDtypeStruct(h.shape, h.dtype) for h in hs],
                  [pltpu.SemaphoreType.DMA((3 * n,)), pltpu.SemaphoreType.DMA((3 * n,))], start, finish)


def _comm_pair_share(tag, gs):
    n = len(gs)

    def body(*refs):
        g_refs, o_refs, send_sems, recv_sems = refs[:n], refs[n:2 * n], refs[2 * n], refs[2 * n + 1]
        x, y, c, _ = _place()
        cps = [pltpu.make_async_remote_copy(
            src_ref=g_refs[a], dst_ref=o_refs[a], send_sem=send_sems.at[a], recv_sem=recv_sems.at[a],
            device_id=(x, y, 1 - c), device_id_type=MESH) for a in range(n)]
        for cp in cps:
            cp.start()
        for cp in cps:
            cp.wait()

    return pl.pallas_call(
        body, name="comm_pair_share_" + tag, in_specs=[ANY] * n, out_specs=[ANY] * n,
        out_shape=[jax.ShapeDtypeStruct(g.shape, g.dtype) for g in gs],
        scratch_shapes=[pltpu.SemaphoreType.DMA((n,)), pltpu.SemaphoreType.DMA((n,))])(*gs)


def _pad_rows(flat, unit):
    n = flat.shape[-1]
    pad = (-n) % unit
    if pad:
        flat = jnp.pad(flat, [(0, 0)] * (flat.ndim - 1) + [(0, pad)])
    return flat


def _split_chips(full, axis):
    sh = full.shape
    t = full.reshape(sh[:axis] + (N_CHIPS, sh[axis] // N_CHIPS) + sh[axis + 1:])
    return jnp.moveaxis(t, axis, 0).reshape(N_CHIPS, -1)


def _join_chips(stack, shard_shape, axis):
    t = jnp.moveaxis(stack.reshape((N_CHIPS,) + tuple(shard_shape)), 0, axis)
    sh = t.shape
    return t.reshape(sh[:axis] + (sh[axis] * sh[axis + 1],) + sh[axis + 2:])


def _block_diag(blocks):
    G, r, c = blocks.shape
    eye = jnp.eye(G, dtype=blocks.dtype)
    return (blocks[:, :, None, :] * eye[:, None, :, None]).reshape(G * r, G * c)


def _diag_blocks(m, G):
    r, c = m.shape[0] // G, m.shape[1] // G
    idx = jnp.arange(G)
    return m.reshape(G, r, G, c)[idx, :, idx, :]


def _weight_shards(w):
    conv = jnp.concatenate([w[n].reshape(-1) for n in GATHER_F32])
    conv = _pad_rows(conv, 2 * SUBLANES * LANES).reshape(-1, LANES)
    b16 = lambda a: a.astype(bf16)
    return {'ev_w_in': (b16(w['ev_w_in'][0]), "chip"), 'ev_w_out': (b16(w['ev_w_out'][0]), "rows"),
            's5_w_glu': (b16(w['s5_w_glu'][0]), "rows"), 'conv': (conv, "chip"),
            'od_w_in': (b16(w['od_w_in'][0]), "cols"), 'od_w_out': (b16(w['od_w_out'][0]), "rows"),
            'ffn_w_up0': (b16(w['ffn_w_up'][0]), "cols"), 'ffn_w_up1': (b16(w['ffn_w_up'][1]), "cols"),
            'ffn_w_down0': (b16(w['ffn_w_down'][0]), "rows"), 'ffn_w_down1': (b16(w['ffn_w_down'][1]), "rows")}


def kernel(x, mix_norm_g, ffn_norm_g, final_norm_g, ev_w_in, ev_w_out, s5_lam_re, s5_lam_im, s5_log_dt, s5_b_re, s5_b_im, s5_c_re, s5_c_im, s5_d, s5_w_glu, s5_b_glu, gm_w_s, gm_b_s, gm_v_g, od_w_in, od_conv_w, od_conv_b, od_w_out, ffn_w_up, ffn_conv_w, ffn_conv_b, ffn_w_down, loss_target, m_mix_norm_g, m_ffn_norm_g, m_final_norm_g, m_ev_w_in, m_ev_w_out, m_s5_lam_re, m_s5_lam_im, m_s5_log_dt, m_s5_b_re, m_s5_b_im, m_s5_c_re, m_s5_c_im, m_s5_d, m_s5_w_glu, m_s5_b_glu, m_gm_w_s, m_gm_b_s, m_gm_v_g, m_od_w_in, m_od_conv_w, m_od_conv_b, m_od_w_out, m_ffn_w_up, m_ffn_conv_w, m_ffn_conv_b, m_ffn_w_down, v_mix_norm_g, v_ffn_norm_g, v_final_norm_g, v_ev_w_in, v_ev_w_out, v_s5_lam_re, v_s5_lam_im, v_s5_log_dt, v_s5_b_re, v_s5_b_im, v_s5_c_re, v_s5_c_im, v_s5_d, v_s5_w_glu, v_s5_b_glu, v_gm_w_s, v_gm_b_s, v_gm_v_g, v_od_w_in, v_od_conv_w, v_od_conv_b, v_od_w_out, v_ffn_w_up, v_ffn_conv_w, v_ffn_conv_b, v_ffn_w_down):
    loc = dict(locals())
    w = {n: loc[n] for n in WEIGHTS}
    mom = {n: loc["m_" + n] for n in WEIGHTS}
    var = {n: loc["v_" + n] for n in WEIGHTS}

    B, S, D = x.shape
    T = B * S
    SW = s5_d.shape[1]
    G = SW // SSM_GROUP
    NS = G * SSM_STATE
    NB = NS // LANES
    tm = min(512, S)
    tt = min(2048, T)
    c_idx = lax.axis_index("c").astype(jnp.int32).reshape(1)
    k_idx = (2 * lax.axis_index("x") + lax.axis_index("y")).astype(jnp.int32).reshape(1)
    shards = _weight_shards(w)
    placed = {n: _place_shard("place_" + n, s, kd, k_idx) for n, (s, kd) in shards.items()}

    def gather(names):
        return _gather_phase([shards[n][0] for n in names], [placed[n] for n in names], [shards[n][1] for n in names])

    (w_ev_in,) = _run_phase("comm_gather_ev_in", gather(['ev_w_in']))
    w_ev_in = jnp.swapaxes(w_ev_in, 0, 1).reshape(D, -1)

    h0 = x.reshape(T, D)
    (y0, p0), (w_ev_out, w_glu, conv) = _norm_mm("ev_in", h0, mix_norm_g[0], w_ev_in, tm,
                                                 phase=gather(['ev_w_out', 's5_w_glu', 'conv']))
    full, off = {}, 0
    for n in GATHER_F32:
        full[n] = _join_chips(conv.reshape(N_CHIPS, -1)[:, off:off + w[n].size], w[n].shape, SHARD_AXIS[n])
        off += w[n].size
    PW = p0.shape[1]
    p03 = p0.reshape(B, S, PW)
    lr, li, ldt = s5_lam_re[0], s5_lam_im[0], s5_log_dt[0].reshape(G, 1)
    ar, ai, zr, zi = _s5_prep(lr, li, ldt)
    bre = _block_diag(jnp.swapaxes(s5_b_re[0], 1, 2))
    bim = _block_diag(jnp.swapaxes(s5_b_im[0], 1, 2))
    cbr = _block_diag(jnp.swapaxes(s5_c_re[0], 1, 2)).astype(bf16)
    cbi = _block_diag(jnp.swapaxes(s5_c_im[0], 1, 2)).astype(bf16)
    zr_row, zi_row = zr.reshape(1, NS), zi.reshape(1, NS)
    bbd = _s5_bbd(zr_row, zi_row, bre, bim)
    ar_s, ai_s = ar.reshape(NB, 1, LANES), ai.reshape(NB, 1, LANES)
    xr, xi = _s5_in(p03, bbd, SW, tm)
    (hr, hi), (w_up0,) = _s5_scan("s5_scan", xr, xi, ar_s, ai_s, False,
                                           phase=gather(['ffn_w_up0']))
    dsk, bglu = s5_d.reshape(1, SW), s5_b_glu.reshape(1, SW)
    a_out = _s5_out(hr, hi, p03, cbr, cbi, dsk, w_glu, bglu, tm)
    ws, bst, gv = gm_w_s[0], gm_b_s[0].T, gm_v_g.reshape(1, -1)
    mixcat = _gmlp(p0, a_out.reshape(T, SW), ws, bst, gv, SW)
    (h1,), (w_down0,) = _mm_resid("ev_out", mixcat, w_ev_out, h0, tm, phase=gather(['ffn_w_down0']))

    def ffn_fwd(l, h, w_up, w_down, up_phase=None, down_phase=None, head=None):
        res = _norm_mm(f"ffn_up{l}", h, ffn_norm_g[l], w_up, tm, phase=up_phase)
        (z, up), got_up = res if up_phase is not None else (res, None)
        res = _ffn_down(f"ffn_down{l}", up, full['ffn_conv_w'][l], ffn_conv_b[l].reshape(1, -1), w_down, h, S, tm,
                        phase=down_phase, head=head)
        (*hn, c), got_down = res if down_phase is not None else (res, None)
        return hn, (z, up.reshape(B, S, -1), c.reshape(B, S, -1)), got_up, got_down

    (h2,), ffn0, (w_up1, w_down1), (w_od_in, w_od_out) = ffn_fwd(
        0, h1, w_up0, w_down0, gather(['ffn_w_up1', 'ffn_w_down1']), gather(['od_w_in', 'od_w_out']))
    w_ups, w_downs = (w_up0, w_up1), (w_down0, w_down1)
    od_cw, od_cb = full['od_conv_w'][0], full['od_conv_b']
    y1, p1 = _norm_mm("od_in", h2, mix_norm_g[1], w_od_in, tm)
    p13 = p1.reshape(B, S, -1)
    h3, cq1 = _od_out(p1, od_cw, od_cb, w_od_out, h2, S, tm)
    (dh4, dh4b, loss_part, d_final_g), ffn1, _, _ = ffn_fwd(
        1, h3, w_up1, w_down1, head=(final_norm_g, loss_target.reshape(T, D)))


    grads = {}

    halves = {}
    chips = lambda g: g.reshape(N_CHIPS, -1, D)

    def pair_sums(names, parts, recv):
        return [_pair_sum(f"pair_sum_{n}", g, r, c_idx, f32 if n == "small" else bf16)
                for n, g, r in zip(names, parts, recv)]

    def reduce_end(tag, names, hsum, r3):
        mine = [_chip_sum(f"chip_sum_{n}", r, h, k_idx) for n, r, h in zip(names, r3, hsum)]
        theirs = _comm_pair_share(tag, mine)
        halves.update({n: (a, b) for n, a, b in zip(names, mine, theirs)})

    def ffn_bwd(l, dh, dhb, h_in, saved, phase=None, swap=False):
        z, up3, c3 = saved
        w_down, w_up = w_downs[l], w_ups[l]
        da = _mm_nt(f"ffn_down_bwd{l}", dhb, w_down, tm)
        res = _ffn_act_bwd(f"ffn_act_bwd{l}", up3, c3, da.reshape(B, S, -1), full['ffn_conv_w'][l], phase=phase)
        (act, dg3, dv3, dcwg, dcwv, dcbg, dcbv), got = res if phase is not None else (res, None)
        g_down = _mm_tn(f"ffn_down_dw{l}", act.reshape(T, -1), dhb, tt)
        dupg, dupv = dg3.reshape(T, -1), dv3.reshape(T, -1)
        F = dupg.shape[1]
        g_up = _mm_tn(f"ffn_up_dw{l}_gate", dupg, z, tt, rows=2 * F)
        g_up = _mm_tn(f"ffn_up_dw{l}_val", dupv, z, tt, rows=2 * F, row_off=F, prev=g_up)
        parts = [chips(g_down), chips(g_up)]
        res = _mm_nt_normbwd(f"ffn_up_bwd{l}", [dupg, dupv], w_up, h_in, ffn_norm_g[l], dh, tm,
                             phase=_swap_phase(parts) if swap else None)
        (dh_new, dhb_new, dg), recv = res if swap else (res, None)
        F = dg3.shape[2]
        dcw = jnp.concatenate([dcwg[:, :F], dcwv[:, :F]], axis=1)
        dcb = jnp.concatenate([dcbg[:, :F], dcbv[:, :F]], axis=1)
        return dh_new, dhb_new, g_down, g_up, dcw, dcb[0], dg[0], got, parts, recv

    dh3, dh3b, gd1, gu1, gcw1, gcb1, gng1, _, _, _ = ffn_bwd(1, dh4, dh4b, h3, ffn1)
    dsc = _mm_nt("od_out_bwd", dh3b, w_od_out, tm)
    sc, dbg3, dcg3, dhx3, d_od_cw, d_od_cb = _od_act_bwd(p13, cq1.reshape(B, S, D), dsc.reshape(B, S, D), od_cw)
    g_od_out = _mm_tn("od_out_dw", sc.reshape(T, D), dh3b, tt)
    dp1 = [t.reshape(T, D) for t in (dbg3, dcg3, dhx3)]
    g_od_in = None
    for i, piece in enumerate(dp1):
        g_od_in = _mm_tn(f"od_in_dw{i}", piece, y1, tt, rows=3 * D, row_off=i * D, prev=g_od_in)
    grads['od_conv_w'] = d_od_cw[None]
    grads['od_conv_b'] = d_od_cb
    layer1 = ['ffn_w_down1', 'ffn_w_up1', 'od_w_out', 'od_w_in']
    parts1 = [chips(g) for g in (gd1, gu1, g_od_out, g_od_in)]
    (dh2, dh2b, gmix1), recv1 = _mm_nt_normbwd("od_in_bwd", dp1, w_od_in, h2, mix_norm_g[1], dh3, tm,
                                               phase=_swap_phase(parts1))
    hsum1 = pair_sums(layer1, parts1, recv1)
    dh1, dh1b, gd0, gu0, gcw0, gcb0, gng0, r3, parts0, recv0 = ffn_bwd(
        0, dh2, dh2b, h1, ffn0, phase=_exchange_phase(hsum1), swap=True)
    reduce_end("layer1", layer1, hsum1, r3)
    ffn0_names = ['ffn_w_down0', 'ffn_w_up0']
    hsum0 = pair_sums(ffn0_names, parts0, recv0)
    grads['ffn_conv_w'] = jnp.stack([gcw0, gcw1])
    grads['ffn_conv_b'] = jnp.stack([gcb0, gcb1])
    grads['ffn_norm_g'] = jnp.stack([gng0, gng1])
    grads['final_norm_g'] = d_final_g[0]

    dmix = _mm_nt("ev_out_bwd", dh1b, w_ev_out, tm)
    g_ev_out = _mm_tn("ev_out_dw", mixcat, dh1b, tt)
    part_evo = [chips(g_ev_out)]
    (duv, d_ws, d_bs, d_gv), recv_evo = _gmlp_bwd(p0, dmix, ws, bst, gv, SW, phase=_swap_phase(part_evo))
    hsum0 = hsum0 + pair_sums(['ev_w_out'], part_evo, recv_evo)
    ffn0_names = ffn0_names + ['ev_w_out']
    grads['gm_w_s'] = d_ws[None]
    grads['gm_b_s'] = d_bs[:, :, 0][None]
    grads['gm_v_g'] = d_gv
    dhr, dhi, du_skip, d_cbr, d_cbi, d_dsk, d_wglu, d_bglu = _s5_out_bwd(
        hr, hi, p03, dmix.reshape(B, S, D), cbr, cbi, dsk, w_glu, bglu, tm)
    grads['s5_c_re'] = jnp.swapaxes(_diag_blocks(d_cbr, G), 1, 2)[None]
    grads['s5_c_im'] = jnp.swapaxes(_diag_blocks(d_cbi, G), 1, 2)[None]
    grads['s5_d'] = d_dsk
    grads['s5_w_glu'] = d_wglu[None]
    grads['s5_b_glu'] = d_bglu
    (gr, gi, dar, dai), r3 = _s5_scan("s5_rscan", dhr, dhi, ar_s, ai_s, True, hr, hi, phase=_exchange_phase(hsum0))
    reduce_end("ffn0", ffn0_names, hsum0, r3)
    dp03, d_bbd = _s5_in_bwd(gr, gi, p03, bbd, du_skip, duv.reshape(B, S, -1), tm)
    d_bre, d_bim, d_zr, d_zi = _s5_bbd_bwd(d_bbd, zr_row, zi_row, bre, bim)
    grads['s5_b_re'] = jnp.swapaxes(_diag_blocks(d_bre, G), 1, 2)[None]
    grads['s5_b_im'] = jnp.swapaxes(_diag_blocks(d_bim, G), 1, 2)[None]
    shp = (-1, G, SSM_STATE)
    d_lr, d_li, d_ldt = _s5_prep_bwd(lr, li, ldt, dar.reshape(shp), dai.reshape(shp), d_zr.reshape(shp),
                                     d_zi.reshape(shp))
    grads['s5_lam_re'] = d_lr[None]
    grads['s5_lam_im'] = d_li[None]
    grads['s5_log_dt'] = d_ldt.reshape(1, G)
    dp0 = dp03.reshape(T, PW)
    g_ev_in = _mm_tn("ev_in_dw", dp0, y0, tt)
    grad_x, _, gmix0 = _mm_nt_normbwd("ev_in_bwd", [dp0], w_ev_in, h0, mix_norm_g[0], dh1, tm)
    grads['mix_norm_g'] = jnp.concatenate([gmix0, gmix1], axis=0)

    small = [n for n in WEIGHTS if n not in BIG]
    segs = []
    for n in small:
        gfull = grads[n].astype(f32)
        if n in SHARD_AXIS:
            segs.append(_split_chips(gfull, SHARD_AXIS[n]))
        else:
            segs.append(jnp.broadcast_to(gfull.reshape(1, -1), (N_CHIPS, gfull.size)))
    segs.append(jnp.broadcast_to(loss_part, (N_CHIPS, 1)))
    unit = 2 * SUBLANES * D
    gsmall = _pad_rows(jnp.concatenate(segs, axis=1), unit).reshape(N_CHIPS, -1, D)
    mixer0 = ['ev_w_in', 'small']
    parts = [chips(g_ev_in), gsmall]
    hsum = pair_sums(mixer0, parts, _run_phase("comm_pair_swap_mixer0", _swap_phase(parts)))
    reduce_end("mixer0", mixer0, hsum, _run_phase("comm_exchange_mixer0", _exchange_phase(hsum)))

    out_g, out_d, out_m, out_v = {}, {}, {}, {}

    def update(n, key, lead, transposed, prev=None):
        res = _adamw(f"adamw_{key}", *halves[key], c_idx, w[n], mom[n], var[n], lead, transposed, prev)
        out_g[n], out_d[n], out_m[n], out_v[n] = res
        return res

    update('ev_w_in', 'ev_w_in', 0, True)
    update('ev_w_out', 'ev_w_out', 0, False)
    update('od_w_in', 'od_w_in', 0, True)
    update('od_w_out', 'od_w_out', 0, False)
    update('ffn_w_up', 'ffn_w_up0', 0, True, prev=update('ffn_w_up', 'ffn_w_up1', 1, True))
    update('ffn_w_down', 'ffn_w_down0', 0, False, prev=update('ffn_w_down', 'ffn_w_down1', 1, False))

    mine, theirs = halves['small']
    first = lax.axis_index("c") == 0
    flat = jnp.concatenate([jnp.where(first, mine, theirs), jnp.where(first, theirs, mine)], axis=1).reshape(-1)
    off = 0
    for n in small:
        out_g[n] = flat[off:off + w[n].size].reshape(w[n].shape)
        off += w[n].size
    loss = flat[off]
    res = _adamw_small([out_g[n] for n in small], [w[n] for n in small], [mom[n] for n in small],
                       [var[n] for n in small])
    for i, n in enumerate(small):
        out_d[n], out_m[n], out_v[n] = res[i], res[len(small) + i], res[2 * len(small) + i]

    return (loss, grad_x.reshape(B, S, D), *[out_g[n] for n in WEIGHTS], *[out_d[n] for n in WEIGHTS],
            *[out_m[n] for n in WEIGHTS], *[out_v[n] for n in WEIGHTS])
```
